```python
import jax, jax.numpy as jnp
from jax import lax
import numpy as np

D_MODEL = 1024
BATCH = 1
SEQ = 16384
DEPTH = 2

CHUNK = 64
Q_BLOCK = 128
RMS_EPS = 1e-6

FOX_HEADS = 4
FOX_HEAD_DIM = 64
DSA_HEADS = 4
DSA_HEAD_DIM = 64
DSA_KV_RANK = 128
IDX_HEADS = 8
IDX_HEAD_DIM = 32
IDX_TOPK = 256
RWKV_HEADS = 8
RWKV_HEAD_DIM = 64
RWKV_W_LORA = 64
RWKV_A_LORA = 64
RWKV_V_LORA = 32
RWKV_G_LORA = 128
RWKV_LN_EPS = 64e-5

FOX_W = FOX_HEADS * FOX_HEAD_DIM
DSA_W = DSA_HEADS * DSA_HEAD_DIM
RWKV_W = RWKV_HEADS * RWKV_HEAD_DIM
N_BRANCH = 3
GATE_W = N_BRANCH * D_MODEL

D_FF_DENSE = 2816
N_EXPERTS = 8
TOP_K_EXPERTS = 2
D_FF_EXPERT = 3584

FOX_COLS = (FOX_W, FOX_W, FOX_W, FOX_HEADS)
DSA_COLS = (DSA_W, DSA_KV_RANK, IDX_HEADS * IDX_HEAD_DIM, IDX_HEAD_DIM, IDX_HEADS)
RWKV_COLS = (RWKV_W, RWKV_W, RWKV_W, RWKV_W_LORA, RWKV_A_LORA, RWKV_G_LORA)
FOX_IN = sum(FOX_COLS)
DSA_IN = sum(DSA_COLS)
RWKV_IN = sum(RWKV_COLS)
D_IN = FOX_IN + DSA_IN + RWKV_IN + GATE_W

kernel_name = "hybrid_fox_dsa_rwkv7_moe_trunk"


def _split(t, sizes):
    idx = [int(i) for i in np.cumsum(sizes)[:-1]]
    return jnp.split(t, idx, axis=-1)


def rmsnorm(x, g):
    xf = x.astype(jnp.float32)
    y = xf * lax.rsqrt(jnp.mean(xf * xf, axis=-1, keepdims=True) + RMS_EPS)
    return (y * g.astype(jnp.float32)).astype(x.dtype)


def token_shift_mix(p, mu):
    prev = jnp.pad(p, ((0, 0), (1, 0), (0, 0)))[:, :-1]
    return p + (prev - p) * mu


def fox_attention(q, k, v, f_logit, f_bias):
    B, S, H, Dh = q.shape
    log_f = jax.nn.log_sigmoid(f_logit.astype(jnp.float32) + f_bias.astype(jnp.float32))
    F = jnp.cumsum(log_f, axis=1).transpose(0, 2, 1)
    kpos = jnp.arange(S)
    scale = Dh ** -0.5

    def block(i):
        s0 = i * Q_BLOCK
        qb = lax.dynamic_slice_in_dim(q, s0, Q_BLOCK, axis=1)
        Fq = lax.dynamic_slice_in_dim(F, s0, Q_BLOCK, axis=2)
        logits = (jnp.einsum('bqhd,bkhd->bhqk', qb, k).astype(jnp.float32) * scale
                  + Fq[..., None] - F[:, :, None, :])
        qpos = s0 + jnp.arange(Q_BLOCK)
        logits = jnp.where(kpos[None, :] <= qpos[:, None], logits, -jnp.inf)
        p = jax.nn.softmax(logits, axis=-1).astype(v.dtype)
        return jnp.einsum('bhqk,bkhd->bqhd', p, v)

    out = lax.map(block, jnp.arange(S // Q_BLOCK))
    return out.transpose(1, 0, 2, 3, 4).reshape(B, S, H * Dh)


def dsa_attention(q, c_kv, q_idx, k_idx, w_idx, w_uk, w_uv, topk):
    B, S, H, Dh = q.shape
    q_lat = jnp.einsum('bshd,hdr->bshr', q, w_uk)
    key_chunk = jnp.arange(S) // CHUNK
    idx_scale = (IDX_HEADS * IDX_HEAD_DIM) ** -0.5
    scale = Dh ** -0.5

    def block(i):
        s0 = i * Q_BLOCK
        qi = lax.dynamic_slice_in_dim(q_idx, s0, Q_BLOCK, axis=1)
        wi = lax.dynamic_slice_in_dim(w_idx, s0, Q_BLOCK, axis=1)
        ql = lax.dynamic_slice_in_dim(q_lat, s0, Q_BLOCK, axis=1)
        rel = jax.nn.relu(jnp.einsum('bqhd,bkd->bqhk', qi, k_idx).astype(jnp.float32))
        score = jnp.einsum('bqhk,bqh->bqk', rel, wi.astype(jnp.float32)) * idx_scale
        q_chunk = (s0 + jnp.arange(Q_BLOCK)) // CHUNK
        score = jnp.where(key_chunk[None, :] <= q_chunk[:, None], score, -jnp.inf)
        top_s, top_i = lax.top_k(score, topk)
        valid = jnp.isfinite(top_s)
        c_sel = jax.vmap(lambda c, ix: c[ix])(c_kv, top_i)
        logits = jnp.einsum('bqhr,bqkr->bqhk', ql, c_sel).astype(jnp.float32) * scale
        logits = jnp.where(valid[:, :, None, :], logits, -jnp.inf)
        p = jax.nn.softmax(logits, axis=-1).astype(c_sel.dtype)
        o_lat = jnp.einsum('bqhk,bqkr->bqhr', p, c_sel)
        return jnp.einsum('bqhr,hrd->bqhd', o_lat, w_uv)

    out = lax.map(block, jnp.arange(S // Q_BLOCK))
    return out.transpose(1, 0, 2, 3, 4).reshape(B, S, H * Dh)


def rwkv7_scan(r, w, k, v, kk, a):
    B, S, H, N = r.shape

    def step(state, inp):
        r_t, w_t, k_t, v_t, kk_t, a_t = inp
        sa = jnp.einsum('bhvk,bhk->bhv', state, -kk_t)
        state = (state * w_t[:, :, None, :] + sa[..., None] * (kk_t * a_t)[:, :, None, :]
                 + v_t[..., None] * k_t[:, :, None, :])
        return state, jnp.einsum('bhvk,bhk->bhv', state, r_t)

    xs = tuple(jnp.moveaxis(t, 1, 0) for t in (r, w, k, v, kk, a))
    _, y = lax.scan(step, jnp.zeros((B, H, N, N), jnp.float32), xs)
    return jnp.moveaxis(y, 0, 1)


def rwkv7_time_mix(p, mu, w0, w_up, a0, a_up, g_up, k_k, k_a, r_k, ln_g, ln_b, v_first, v_res_logit):
    B, S, _ = p.shape
    f32 = jnp.float32
    r, k, v, wl, al, gl = [t.astype(f32) for t in _split(token_shift_mix(p, mu), RWKV_COLS)]
    if v_first is None:
        v_first = v
    else:
        v = v + (v_first - v) * jax.nn.sigmoid(v_res_logit.astype(f32))
    log_w = -jax.nn.softplus(-(w0 + jnp.tanh(wl) @ w_up)) - 0.5
    decay = jnp.exp(-jnp.exp(log_w))
    a = jax.nn.sigmoid(a0 + al @ a_up)
    g = jax.nn.sigmoid(gl) @ g_up
    heads = lambda t: t.reshape(B, S, RWKV_HEADS, RWKV_HEAD_DIM)
    kk = heads(k * k_k)
    kk = kk / jnp.sqrt(jnp.sum(kk * kk, axis=-1, keepdims=True) + 1e-12)
    k = k * (1.0 + (a - 1.0) * k_a)
    rh, kh, vh = heads(r), heads(k), heads(v)
    y = rwkv7_scan(rh, heads(decay), kh, vh, kk, heads(a))
    mean = jnp.mean(y, axis=-1, keepdims=True)
    var = jnp.mean(jnp.square(y - mean), axis=-1, keepdims=True)
    y = ((y - mean) * lax.rsqrt(var + RWKV_LN_EPS)).reshape(B, S, RWKV_W) * ln_g + ln_b
    bonus = (jnp.sum(rh * kh * r_k, axis=-1, keepdims=True) * vh).reshape(B, S, RWKV_W)
    return ((y + bonus) * g).astype(p.dtype), v_first


def swiglu(h, w_gate, w_up, w_down):
    return (jax.nn.silu(h @ w_gate) * (h @ w_up)) @ w_down


def moe_ffn(h, router_w, router_b, w_gate, w_up, w_down):
    logits = (h @ router_w).astype(jnp.float32) + router_b.astype(jnp.float32)
    top_v, top_i = lax.top_k(logits, TOP_K_EXPERTS)
    top_p = jax.nn.softmax(top_v, axis=-1)
    gates = jnp.sum(jax.nn.one_hot(top_i, N_EXPERTS, dtype=jnp.float32) * top_p[..., None], axis=-2)
    gates = gates.astype(h.dtype)
    y = jnp.zeros_like(h)
    for e in range(N_EXPERTS):
        y = y + gates[..., e:e + 1] * swiglu(h, w_gate[e], w_up[e], w_down[e])
    return y


def setup_inputs(seed: int = 0) -> dict:
    key = jax.random.key(seed)
    ks = iter(list(jax.random.split(key, 48)))
    f = jnp.float32
    n_dense = (DEPTH + 1) // 2
    n_moe = DEPTH // 2

    def nrm(shape, fan_in, s=1.0):
        return jax.random.normal(next(ks), shape, f) * (s * fan_in ** -0.5)

    def gain(shape):
        return 1.0 + 0.01 * jax.random.normal(next(ks), shape, f)

    def small(shape, s=0.01, c=0.0):
        return c + s * jax.random.normal(next(ks), shape, f)

    def unif(shape, lo, hi):
        return jax.random.uniform(next(ks), shape, f, lo, hi)

    return {
        "x": jax.random.normal(next(ks), (BATCH, SEQ, D_MODEL), f),
        "w_in": nrm((DEPTH, D_MODEL, D_IN), D_MODEL),
        "b_gate": small((DEPTH, GATE_W)),
        "g_mix": gain((DEPTH, D_MODEL)),
        "fox_f_bias": unif((DEPTH, FOX_HEADS), 1.0, 4.0),
        "dsa_kv_norm": gain((DEPTH, DSA_KV_RANK)),
        "dsa_w_uk": nrm((DEPTH, DSA_HEADS, DSA_HEAD_DIM, DSA_KV_RANK), DSA_KV_RANK),
        "dsa_w_uv": nrm((DEPTH, DSA_HEADS, DSA_KV_RANK, DSA_HEAD_DIM), DSA_KV_RANK),
        "rwkv_mu": unif((DEPTH, RWKV_IN), 0.0, 1.0),
        "rwkv_w0": unif((DEPTH, RWKV_W), -6.0, -1.0),
        "rwkv_w_up": nrm((DEPTH, RWKV_W_LORA, RWKV_W), RWKV_W_LORA, 0.5),
        "rwkv_a0": small((DEPTH, RWKV_W), 0.1),
        "rwkv_a_up": nrm((DEPTH, RWKV_A_LORA, RWKV_W), RWKV_A_LORA),
        "rwkv_g_up": nrm((DEPTH, RWKV_G_LORA, RWKV_W), RWKV_G_LORA),
        "rwkv_k_k": small((DEPTH, RWKV_W), 0.02, 0.85),
        "rwkv_k_a": small((DEPTH, RWKV_W), 0.02, 1.0),
        "rwkv_r_k": small((DEPTH, RWKV_HEADS, RWKV_HEAD_DIM), 0.1),
        "rwkv_ln_g": gain((DEPTH, RWKV_W)),
        "rwkv_ln_b": small((DEPTH, RWKV_W)),
        "vres_down": nrm((DEPTH - 1, D_MODEL, RWKV_V_LORA), D_MODEL),
        "vres_mu": unif((DEPTH - 1, RWKV_V_LORA), 0.0, 1.0),
        "vres_up": nrm((DEPTH - 1, RWKV_V_LORA, RWKV_W), RWKV_V_LORA),
        "vres_v0": small((DEPTH - 1, RWKV_W), 0.1, 0.5),
        "p_fox": nrm((DEPTH, FOX_W, D_MODEL), FOX_W),
        "p_dsa": nrm((DEPTH, DSA_W, D_MODEL), DSA_W),
        "p_rwkv": nrm((DEPTH, RWKV_W, D_MODEL), RWKV_W),
        "w_out": nrm((DEPTH, D_MODEL, D_MODEL), D_MODEL),
        "g_ffn": gain((DEPTH, D_MODEL)),
        "ffn_w_gate": nrm((n_dense, D_MODEL, D_FF_DENSE), D_MODEL),
        "ffn_w_up": nrm((n_dense, D_MODEL, D_FF_DENSE), D_MODEL),
        "ffn_w_down": nrm((n_dense, D_FF_DENSE, D_MODEL), D_FF_DENSE),
        "router_w": nrm((n_moe, D_MODEL, N_EXPERTS), D_MODEL),
        "router_b": small((n_moe, N_EXPERTS)),
        "moe_w_gate": nrm((n_moe, N_EXPERTS, D_MODEL, D_FF_EXPERT), D_MODEL),
        "moe_w_up": nrm((n_moe, N_EXPERTS, D_MODEL, D_FF_EXPERT), D_MODEL),
        "moe_w_down": nrm((n_moe, N_EXPERTS, D_FF_EXPERT, D_MODEL), D_FF_EXPERT),
        "g_final": gain((D_MODEL,)),
    }


def reference(x, w_in, b_gate, g_mix, fox_f_bias, dsa_kv_norm, dsa_w_uk, dsa_w_uv,
              rwkv_mu, rwkv_w0, rwkv_w_up, rwkv_a0, rwkv_a_up, rwkv_g_up, rwkv_k_k, rwkv_k_a,
              rwkv_r_k, rwkv_ln_g, rwkv_ln_b, vres_down, vres_mu, vres_up, vres_v0,
              p_fox, p_dsa, p_rwkv, w_out, g_ffn, ffn_w_gate, ffn_w_up, ffn_w_down,
              router_w, router_b, moe_w_gate, moe_w_up, moe_w_down, g_final):
    B, S, _ = x.shape
    topk = min(IDX_TOPK, S // 4)
    v_first = None
    for l in range(DEPTH):
        h = rmsnorm(x, g_mix[l])
        fox_p, dsa_p, rwkv_p, gate_p = _split(h @ w_in[l], (FOX_IN, DSA_IN, RWKV_IN, GATE_W))

        fq, fk, fv, f_logit = _split(fox_p, FOX_COLS)
        fh = lambda t: t.reshape(B, S, FOX_HEADS, FOX_HEAD_DIM)
        o_fox = fox_attention(fh(fq), fh(fk), fh(fv), f_logit, fox_f_bias[l])

        dq, dckv, diq, dik, diw = _split(dsa_p, DSA_COLS)
        o_dsa = dsa_attention(dq.reshape(B, S, DSA_HEADS, DSA_HEAD_DIM),
                              rmsnorm(dckv, dsa_kv_norm[l]),
                              diq.reshape(B, S, IDX_HEADS, IDX_HEAD_DIM), dik, diw,
                              dsa_w_uk[l], dsa_w_uv[l], topk)

        if l == 0:
            v_res_logit = None
        else:
            vl = token_shift_mix(h @ vres_down[l - 1], vres_mu[l - 1])
            v_res_logit = vres_v0[l - 1] + vl @ vres_up[l - 1]
        o_rwkv, v_first = rwkv7_time_mix(rwkv_p, rwkv_mu[l], rwkv_w0[l], rwkv_w_up[l], rwkv_a0[l],
                                         rwkv_a_up[l], rwkv_g_up[l], rwkv_k_k[l], rwkv_k_a[l],
                                         rwkv_r_k[l], rwkv_ln_g[l], rwkv_ln_b[l], v_first, v_res_logit)

        gates = jax.nn.sigmoid((gate_p + b_gate[l]).astype(jnp.float32)).astype(x.dtype)
        gates = gates.reshape(B, S, N_BRANCH, D_MODEL)
        merged = (gates[:, :, 0] * (o_fox @ p_fox[l])
                  + gates[:, :, 1] * (o_dsa @ p_dsa[l])
                  + gates[:, :, 2] * (o_rwkv @ p_rwkv[l]))
        x = x + merged @ w_out[l]

        h2 = rmsnorm(x, g_ffn[l])
        if l % 2 == 0:
            y = swiglu(h2, ffn_w_gate[l // 2], ffn_w_up[l // 2], ffn_w_down[l // 2])
        else:
            y = moe_ffn(h2, router_w[l // 2], router_b[l // 2],
                        moe_w_gate[l // 2], moe_w_up[l // 2], moe_w_down[l // 2])
        x = x + y
    return rmsnorm(x, g_final)
```

```python
import functools

import jax
import jax.numpy as jnp
from jax import lax
from jax.experimental import pallas as pl
from jax.experimental.pallas import tpu as pltpu

F32 = jnp.float32
BF16 = jnp.bfloat16
I32 = jnp.int32

D_MODEL = 1024
CHUNK = 64
RMS_EPS = 1e-6
FOX_HEADS, FOX_HEAD_DIM = 4, 64
DSA_HEADS, DSA_HEAD_DIM, DSA_KV_RANK = 4, 64, 128
IDX_HEADS, IDX_HEAD_DIM, IDX_TOPK = 8, 32, 256
RWKV_HEADS, RWKV_HEAD_DIM = 8, 64
RWKV_W_LORA, RWKV_A_LORA, RWKV_V_LORA, RWKV_G_LORA = 64, 64, 32, 128
RWKV_LN_EPS = 64e-5
FOX_W = FOX_HEADS * FOX_HEAD_DIM
DSA_W = DSA_HEADS * DSA_HEAD_DIM
RWKV_W = RWKV_HEADS * RWKV_HEAD_DIM
IDX_W = IDX_HEADS * IDX_HEAD_DIM
N_EXPERTS = 8
FOX_IN = 3 * FOX_W + FOX_HEADS
DSA_IN = DSA_W + DSA_KV_RANK + IDX_W + IDX_HEAD_DIM + IDX_HEADS
RWKV_IN = 3 * RWKV_W + RWKV_W_LORA + RWKV_A_LORA + RWKV_G_LORA

LANE = 128
VMEM_LIMIT = 52 * 1024 * 1024
NEG = -1e30
INT_MIN = -(2 ** 31)
HI = lax.Precision.HIGHEST


def _params(*sem):
    return pltpu.CompilerParams(dimension_semantics=sem, vmem_limit_bytes=VMEM_LIMIT)


def _pick_tile(n, cap):
    best = LANE
    for t in range(LANE, min(n, cap) + 1, LANE):
        if n % t == 0:
            best = t
    return best


def _softplus(x):
    return jnp.maximum(x, 0.0) + jnp.log1p(jnp.exp(-jnp.abs(x)))


def _sigmoid(x):
    return 1.0 / (1.0 + jnp.exp(-x))


def _rms(x, g):
    return x * lax.rsqrt(jnp.mean(x * x, axis=-1, keepdims=True) + RMS_EPS) * g


def _rms_proj_kernel(x_ref, g_ref, w_ref, o_ref):
    h = _rms(x_ref[...], g_ref[...])
    o_ref[...] = jnp.dot(h.astype(BF16), w_ref[...], preferred_element_type=F32)


def _rms_proj(x, g, w):
    S, D = x.shape
    N = w.shape[1]
    tm = min(512, S)
    tn = _pick_tile(N, 2304)
    return pl.pallas_call(
        _rms_proj_kernel,
        grid=(N // tn, S // tm),
        in_specs=[pl.BlockSpec((tm, D), lambda j, i: (i, 0)),
                  pl.BlockSpec((1, D), lambda j, i: (0, 0)),
                  pl.BlockSpec((D, tn), lambda j, i: (0, j))],
        out_specs=pl.BlockSpec((tm, tn), lambda j, i: (i, j)),
        out_shape=jax.ShapeDtypeStruct((S, N), F32),
        compiler_params=_params("arbitrary", "arbitrary"),
        name="rms_proj",
    )(x, g.reshape(1, D), w)


def _rmsnorm_kernel(x_ref, g_ref, o_ref):
    o_ref[...] = _rms(x_ref[...], g_ref[...]).astype(o_ref.dtype)


def _rmsnorm(x, g, dtype):
    S, D = x.shape
    tm = min(2048, S)
    return pl.pallas_call(
        _rmsnorm_kernel,
        grid=(S // tm,),
        in_specs=[pl.BlockSpec((tm, D), lambda i: (i, 0)),
                  pl.BlockSpec((1, D), lambda i: (0, 0))],
        out_specs=pl.BlockSpec((tm, D), lambda i: (i, 0)),
        out_shape=jax.ShapeDtypeStruct((S, D), dtype),
        compiler_params=_params("arbitrary"),
        name="rmsnorm",
    )(x, g.reshape(1, D))


def _fox_cumsum_kernel(fl_ref, b_ref, o_ref):
    H, R, _ = fl_ref.shape
    upper = (lax.broadcasted_iota(I32, (LANE, LANE), 0)
             <= lax.broadcasted_iota(I32, (LANE, LANE), 1)).astype(F32)
    strict_lower = (lax.broadcasted_iota(I32, (R, R), 0)
                    > lax.broadcasted_iota(I32, (R, R), 1)).astype(F32)
    for h in range(H):
        log_f = -_softplus(-(fl_ref[h] + b_ref[h]))
        within = jnp.dot(log_f, upper, preferred_element_type=F32, precision=HI)
        row_tot = jnp.broadcast_to(within[:, LANE - 1:LANE], (R, LANE))
        before = jnp.dot(strict_lower, row_tot, preferred_element_type=F32, precision=HI)
        o_ref[h] = within + before


def _fox_cumsum(fl, bias):
    H, R, _ = fl.shape
    return pl.pallas_call(
        _fox_cumsum_kernel,
        out_shape=jax.ShapeDtypeStruct((H, R, LANE), F32),
        compiler_params=pltpu.CompilerParams(vmem_limit_bytes=VMEM_LIMIT),
        name="fox_cumsum",
    )(fl, jnp.broadcast_to(bias.reshape(H, 1, 1), (H, 1, LANE)))


def _fox_kernel(q_ref, kT_ref, v_ref, fq_ref, fk_ref, o_ref, m_sc, l_sc, acc_sc, *, t):
    i = pl.program_id(1)
    q = q_ref[0]
    fq = fq_ref[0]
    m_sc[...] = jnp.full(m_sc.shape, NEG, F32)
    l_sc[...] = jnp.zeros(l_sc.shape, F32)
    acc_sc[...] = jnp.zeros(acc_sc.shape, F32)
    causal = (lax.broadcasted_iota(I32, (t, t), 1) <= lax.broadcasted_iota(I32, (t, t), 0))

    def step(j, diagonal):
        off = pl.multiple_of(j * t, t)
        s = jnp.dot(q, kT_ref[0, :, pl.ds(off, t)], preferred_element_type=F32)
        s = s + fq - fk_ref[0, :, pl.ds(off, t)]
        if diagonal:
            s = jnp.where(causal, s, NEG)
        m_prev = m_sc[...]
        m_new = jnp.maximum(m_prev, jnp.max(s, axis=-1, keepdims=True))
        alpha = jnp.exp(m_prev - m_new)
        p = jnp.exp(s - m_new)
        l_sc[...] = alpha * l_sc[...] + jnp.sum(p, axis=-1, keepdims=True)
        acc_sc[...] = alpha * acc_sc[...] + jnp.dot(
            p.astype(BF16), v_ref[0, pl.ds(off, t), :], preferred_element_type=F32)
        m_sc[...] = m_new

    def body(j, c):
        step(j, False)
        return c

    lax.fori_loop(0, i, body, 0)
    step(i, True)
    o_ref[0] = (acc_sc[...] / l_sc[...]).astype(o_ref.dtype)


def _fox_attention(q, kT, v, F):
    H, S, Dh = q.shape
    t = min(256, S)
    return pl.pallas_call(
        functools.partial(_fox_kernel, t=t),
        grid=(H, S // t),
        in_specs=[pl.BlockSpec((1, t, Dh), lambda h, i: (h, i, 0)),
                  pl.BlockSpec((1, Dh, S), lambda h, i: (h, 0, 0)),
                  pl.BlockSpec((1, S, Dh), lambda h, i: (h, 0, 0)),
                  pl.BlockSpec((1, t, 1), lambda h, i: (h, i, 0)),
                  pl.BlockSpec((1, 1, S), lambda h, i: (h, 0, 0))],
        out_specs=pl.BlockSpec((1, t, Dh), lambda h, i: (h, i, 0)),
        out_shape=jax.ShapeDtypeStruct((H, S, Dh), BF16),
        scratch_shapes=[pltpu.VMEM((t, 1), F32), pltpu.VMEM((t, 1), F32),
                        pltpu.VMEM((t, Dh), F32)],
        compiler_params=_params("arbitrary", "arbitrary"),
        name="fox_attention",
    )(q, kT, v, F.reshape(H, S, 1), F.reshape(H, 1, S))


def _dsa_kernel(qi_ref, w_ref, q_ref, wuk_ref, wuv_ref, kiT_ref, cT_ref, c_ref, o_ref,
                keys_sc, qlat_sc, m_sc, l_sc, acc_sc, *, t, topk):
    i = pl.program_id(0)
    nt = i + 1
    R = DSA_KV_RANK
    row = lax.broadcasted_iota(I32, (t, t), 0)
    col = lax.broadcasted_iota(I32, (t, t), 1)
    admissible = (col // CHUNK) <= (row // CHUNK)
    strict_upper = (row < col).astype(BF16)

    qi = qi_ref[...].reshape(IDX_HEADS * t, IDX_HEAD_DIM)
    w = w_ref[...]

    def score_keys(j):
        off = pl.multiple_of(j * t, t)
        rel = jnp.dot(qi, kiT_ref[:, pl.ds(off, t)], preferred_element_type=F32)
        rel = jnp.maximum(rel, 0.0).reshape(IDX_HEADS, t, t)
        sc = jnp.sum(rel * w, axis=0)
        bits = pltpu.bitcast(sc, I32)
        key = bits ^ ((bits >> 31) & 0x7FFFFFFF)
        return jnp.where(sc == 0.0, 0, key)

    def fill(j, c):
        keys_sc[:, pl.ds(pl.multiple_of(j * t, t), t)] = score_keys(j)
        return c

    lax.fori_loop(0, i, fill, 0)
    keys_sc[:, pl.ds(pl.multiple_of(i * t, t), t)] = jnp.where(admissible, score_keys(i), INT_MIN)

    def count_ge(cand):
        cb = jnp.broadcast_to(cand, (t, t))

        def body(j, acc):
            kt = keys_sc[:, pl.ds(pl.multiple_of(j * t, t), t)]
            return acc + (kt >= cb).astype(F32)

        acc = lax.fori_loop(0, nt, body, jnp.zeros((t, t), F32))
        return jnp.sum(acc, axis=1, keepdims=True)

    def bisect(s, thr):
        cand = thr + jnp.left_shift(jnp.int32(1), 31 - s)
        return jnp.where(count_ge(cand) >= topk, cand, thr)

    thr = lax.fori_loop(0, 32, bisect, jnp.full((t, 1), INT_MIN, I32))
    thr = jnp.maximum(thr, INT_MIN + 1)
    need = topk - count_ge(thr + 1)
    thr_b = jnp.broadcast_to(thr, (t, t))

    for h in range(DSA_HEADS):
        ql = jnp.dot(q_ref[h], wuk_ref[h], preferred_element_type=F32)
        qlat_sc[h * t:(h + 1) * t, :] = (ql * (DSA_HEAD_DIM ** -0.5)).astype(BF16)
    m_sc[...] = jnp.full(m_sc.shape, NEG, F32)
    l_sc[...] = jnp.zeros(l_sc.shape, F32)
    acc_sc[...] = jnp.zeros(acc_sc.shape, F32)

    def attend(j, seen_eq):
        off = pl.multiple_of(j * t, t)
        key = keys_sc[:, pl.ds(off, t)]
        eq = key == thr_b
        eqf = eq.astype(F32)
        rank = jnp.dot(eqf.astype(BF16), strict_upper, preferred_element_type=F32) + seen_eq
        sel = (key > thr_b) | (eq & (rank < need))
        lg = jnp.dot(qlat_sc[...], cT_ref[:, pl.ds(off, t)], preferred_element_type=F32)
        c_tile = c_ref[pl.ds(off, t), :]
        for h in range(DSA_HEADS):
            s = jnp.where(sel, lg[h * t:(h + 1) * t, :], NEG)
            m_prev = m_sc[h]
            m_new = jnp.maximum(m_prev, jnp.max(s, axis=-1, keepdims=True))
            alpha = jnp.exp(m_prev - m_new)
            p = jnp.exp(s - m_new)
            l_sc[h] = alpha * l_sc[h] + jnp.sum(p, axis=-1, keepdims=True)
            acc_sc[h] = alpha * acc_sc[h] + jnp.dot(p.astype(BF16), c_tile,
                                                    preferred_element_type=F32)
            m_sc[h] = m_new
        return seen_eq + jnp.sum(eqf, axis=1, keepdims=True)

    lax.fori_loop(0, nt, attend, jnp.zeros((t, 1), F32))
    for h in range(DSA_HEADS):
        o_lat = (acc_sc[h] / l_sc[h]).astype(BF16)
        o_ref[h] = jnp.dot(o_lat, wuv_ref[h], preferred_element_type=F32).astype(o_ref.dtype)


def _dsa_attention(qi, w, q, wuk, wuv, kiT, cT, c, topk):
    H, S, Dh = q.shape
    t = min(128, S)
    R = DSA_KV_RANK
    const2 = lambda i: (0, 0)
    const3 = lambda i: (0, 0, 0)
    return pl.pallas_call(
        functools.partial(_dsa_kernel, t=t, topk=topk),
        grid=(S // t,),
        in_specs=[pl.BlockSpec((IDX_HEADS, t, IDX_HEAD_DIM), lambda i: (0, i, 0)),
                  pl.BlockSpec((IDX_HEADS, t, 1), lambda i: (0, i, 0)),
                  pl.BlockSpec((H, t, Dh), lambda i: (0, i, 0)),
                  pl.BlockSpec((H, Dh, R), const3),
                  pl.BlockSpec((H, R, Dh), const3),
                  pl.BlockSpec((IDX_HEAD_DIM, S), const2),
                  pl.BlockSpec((R, S), const2),
                  pl.BlockSpec((S, R), const2)],
        out_specs=pl.BlockSpec((H, t, Dh), lambda i: (0, i, 0)),
        out_shape=jax.ShapeDtypeStruct((H, S, Dh), BF16),
        scratch_shapes=[pltpu.VMEM((t, S), I32), pltpu.VMEM((H * t, R), BF16),
                        pltpu.VMEM((H, t, 1), F32), pltpu.VMEM((H, t, 1), F32),
                        pltpu.VMEM((H, t, R), F32)],
        compiler_params=_params("arbitrary"),
        name="dsa_attention",
    )(qi, w, q, wuk, wuv, kiT, cT, c)


def _shift_mix(p, prev_tail, mu, first_block):
    rows = lax.broadcasted_iota(I32, p.shape, 0)
    tail = jnp.where(first_block, 0.0, prev_tail)
    prev = jnp.where(rows == 0, tail, pltpu.roll(p, 1, axis=0))
    return p + (prev - p) * mu


def _rwkv_pre_kernel(*refs, with_vres):
    if with_vres:
        (p_ref, pp_ref, mu_ref, w0_ref, a0_ref, wwa_ref, gup_ref, kk_ref, ka_ref,
         hv_ref, hvp_ref, vmu_ref, vup_ref, vv0_ref, vf_ref,
         r_o, k_o, v_o, lw_o, kk_o, a_o, g_o) = refs
    else:
        (p_ref, pp_ref, mu_ref, w0_ref, a0_ref, wwa_ref, gup_ref, kk_ref, ka_ref,
         r_o, k_o, v_o, lw_o, kk_o, a_o, g_o) = refs
    first = pl.program_id(0) == 0
    W = RWKV_W
    ps = _shift_mix(p_ref[...], pp_ref[7:8, :], mu_ref[...], first)
    r, k, v = ps[:, 0:W], ps[:, W:2 * W], ps[:, 2 * W:3 * W]
    wa = ps[:, 3 * W:3 * W + LANE]
    gl = ps[:, 3 * W + LANE:3 * W + 2 * LANE]
    lane = lax.broadcasted_iota(I32, wa.shape, 1)
    wa = jnp.where(lane < RWKV_W_LORA, jnp.tanh(wa), wa)
    up = jnp.dot(wa.astype(BF16), wwa_ref[...], preferred_element_type=F32)
    log_w = -_softplus(-(w0_ref[...] + up[:, 0:W])) - 0.5
    a = _sigmoid(a0_ref[...] + up[:, W:2 * W])
    g = jnp.dot(_sigmoid(gl).astype(BF16), gup_ref[...], preferred_element_type=F32)
    if with_vres:
        vl = _shift_mix(hv_ref[...], hvp_ref[7:8, :], vmu_ref[...], first)
        logit = vv0_ref[...] + jnp.dot(vl.astype(BF16), vup_ref[...], preferred_element_type=F32)
        v = v + (vf_ref[...] - v) * _sigmoid(logit)
    r_o[...] = r
    k_o[...] = k * (1.0 + (a - 1.0) * ka_ref[...])
    v_o[...] = v
    lw_o[...] = -jnp.exp(log_w)
    kk_o[...] = k * kk_ref[...]
    a_o[...] = a
    g_o[...] = g


def _rwkv_pre(p, mu, w0, a0, wwa, gup, k_k, k_a, vres=None):
    S, PW = p.shape
    W = RWKV_W
    tm = min(512, S)
    row = lambda i: (i, 0)
    const = lambda i: (0, 0)
    tail = lambda i: (jnp.maximum(i * (tm // 8) - 1, 0), 0)
    vec = lambda a: a.reshape(1, -1)
    args = [p, p, vec(mu), vec(w0), vec(a0), wwa, gup, vec(k_k), vec(k_a)]
    specs = [pl.BlockSpec((tm, PW), row), pl.BlockSpec((8, PW), tail),
             pl.BlockSpec((1, PW), const), pl.BlockSpec((1, W), const), pl.BlockSpec((1, W), const),
             pl.BlockSpec(wwa.shape, const), pl.BlockSpec(gup.shape, const),
             pl.BlockSpec((1, W), const), pl.BlockSpec((1, W), const)]
    if vres is not None:
        hv, vmu, vup, vv0, v_first = vres
        args += [hv, hv, vec(vmu), vup, vec(vv0), v_first]
        specs += [pl.BlockSpec((tm, LANE), row), pl.BlockSpec((8, LANE), tail),
                  pl.BlockSpec((1, LANE), const), pl.BlockSpec(vup.shape, const),
                  pl.BlockSpec((1, W), const), pl.BlockSpec((tm, W), row)]
    return pl.pallas_call(
        functools.partial(_rwkv_pre_kernel, with_vres=vres is not None),
        grid=(S // tm,),
        in_specs=specs,
        out_specs=[pl.BlockSpec((tm, W), row)] * 7,
        out_shape=[jax.ShapeDtypeStruct((S, W), F32)] * 7,
        compiler_params=_params("arbitrary"),
        name="rwkv_pre",
    )(*args)


def _rwkv_scan_kernel(r_ref, k_ref, v_ref, lw_ref, kk_ref, a_ref, g_ref, rk_ref, lng_ref, lnb_ref,
                      o_ref, h_sc, *, chunks):
    C = CHUNK
    N = RWKV_HEAD_DIM

    @pl.when(pl.program_id(0) == 0)
    def _():
        h_sc[...] = jnp.zeros(h_sc.shape, F32)

    ti = lax.broadcasted_iota(I32, (C, C), 0)
    tj = lax.broadcasted_iota(I32, (C, C), 1)
    lower_incl = ti >= tj
    lower_strict = ti > tj
    tri_incl = lower_incl.astype(F32)
    eye = (ti == tj).astype(F32)
    eye_n = (lax.broadcasted_iota(I32, (N, N), 0) == lax.broadcasted_iota(I32, (N, N), 1)).astype(F32)

    def mm(x, y):
        return jnp.dot(x.astype(BF16), y.astype(BF16), preferred_element_type=F32)

    def mm_nt(x, y):
        return lax.dot_general(x.astype(BF16), y.astype(BF16), (((1,), (1,)), ((), ())),
                               preferred_element_type=F32)

    def mm_tn(x, y):
        return lax.dot_general(x.astype(BF16), y.astype(BF16), (((0,), (0,)), ((), ())),
                               preferred_element_type=F32)

    for h in range(RWKV_HEADS):
        H = h_sc[h]
        for c in range(chunks):
            sl = slice(c * C, (c + 1) * C)
            r, k, v = r_ref[h, sl, :], k_ref[h, sl, :], v_ref[h, sl, :]
            lw, kkr, a = lw_ref[h, sl, :], kk_ref[h, sl, :], a_ref[h, sl, :]
            kk = kkr * lax.rsqrt(jnp.sum(kkr * kkr, axis=-1, keepdims=True) + 1e-12)
            cum = jnp.dot(tri_incl, lw, preferred_element_type=F32, precision=HI)
            cum_end = cum[C - 1:C, :]
            p_in = jnp.exp(cum)
            p_inv = jnp.exp(-cum)
            p_end = jnp.exp(cum_end - cum)
            beta = kk * a
            At = -kk * jnp.exp(cum - lw)
            Rt = r * p_in
            M = mm_nt(jnp.concatenate([At, Rt], axis=0),
                      jnp.concatenate([beta * p_inv, k * p_inv], axis=0))
            Lab = jnp.where(lower_strict, M[0:C, 0:C], 0.0)
            Lak = jnp.where(lower_strict, M[0:C, C:2 * C], 0.0)
            Mrb = jnp.where(lower_incl, M[C:2 * C, 0:C], 0.0)
            Mrk = jnp.where(lower_incl, M[C:2 * C, C:2 * C], 0.0)
            T = eye + Lab
            Lp = Lab
            span = 2
            while span < C:
                Lp = mm(Lp, Lp)
                T = T + mm(Lp, T)
                span *= 2
            AU = mm(T, jnp.concatenate([At, mm(Lak, v)], axis=1))
            RY = mm(Mrb, AU)
            R2 = Rt + RY[:, 0:N]
            Y0 = RY[:, N:2 * N] + mm(Mrk, v)
            GH = mm_tn(beta * p_end, AU)
            G = eye_n * jnp.exp(cum_end) + GH[:, 0:N]
            H0 = GH[:, N:2 * N] + mm_tn(k * p_end, v)
            y = mm(R2, H) + Y0
            H = mm(G, H) + H0
            mean = jnp.mean(y, axis=-1, keepdims=True)
            yc = y - mean
            var = jnp.mean(yc * yc, axis=-1, keepdims=True)
            yn = yc * lax.rsqrt(var + RWKV_LN_EPS) * lng_ref[h] + lnb_ref[h]
            bonus = jnp.sum(r * k * rk_ref[h], axis=-1, keepdims=True) * v
            o_ref[h, sl, :] = ((yn + bonus) * g_ref[h, sl, :]).astype(o_ref.dtype)
        h_sc[h] = H


def _rwkv_scan(r, k, v, lw, kk, a, g, r_k, ln_g, ln_b):
    Hr, S, N = r.shape
    chunks = 2 if S % (2 * CHUNK) == 0 else 1
    tb = chunks * CHUNK
    seq = pl.BlockSpec((Hr, tb, N), lambda i: (0, i, 0))
    par = pl.BlockSpec((Hr, 1, N), lambda i: (0, 0, 0))
    return pl.pallas_call(
        functools.partial(_rwkv_scan_kernel, chunks=chunks),
        grid=(S // tb,),
        in_specs=[seq] * 7 + [par] * 3,
        out_specs=seq,
        out_shape=jax.ShapeDtypeStruct((Hr, S, N), BF16),
        scratch_shapes=[pltpu.VMEM((Hr, N, N), F32)],
        compiler_params=_params("arbitrary"),
        name="rwkv_scan",
    )(r, k, v, lw, kk, a, g, r_k.reshape(Hr, 1, N), ln_g.reshape(Hr, 1, N), ln_b.reshape(Hr, 1, N))


def _merge_kernel(x_ref, of_ref, od_ref, or_ref, gp_ref, bg_ref, pf_ref, pd_ref, pr_ref, wo_ref,
                  gn_ref, xo_ref, h_ref):
    D = D_MODEL
    gates = _sigmoid(gp_ref[...] + bg_ref[...])
    merged = (gates[:, 0:D] * jnp.dot(of_ref[...], pf_ref[...], preferred_element_type=F32)
              + gates[:, D:2 * D] * jnp.dot(od_ref[...], pd_ref[...], preferred_element_type=F32)
              + gates[:, 2 * D:3 * D] * jnp.dot(or_ref[...], pr_ref[...], preferred_element_type=F32))
    x = x_ref[...] + jnp.dot(merged.astype(BF16), wo_ref[...], preferred_element_type=F32)
    xo_ref[...] = x
    h_ref[...] = _rms(x, gn_ref[...]).astype(h_ref.dtype)


def _merge(x, o_fox, o_dsa, o_rwkv, gate_p, b_gate, p_fox, p_dsa, p_rwkv, w_out, g_ffn):
    S, D = x.shape
    tm = min(512, S)
    row = lambda i: (i, 0)
    const = lambda i: (0, 0)
    return pl.pallas_call(
        _merge_kernel,
        grid=(S // tm,),
        in_specs=[pl.BlockSpec((tm, D), row), pl.BlockSpec((tm, FOX_W), row),
                  pl.BlockSpec((tm, DSA_W), row), pl.BlockSpec((tm, RWKV_W), row),
                  pl.BlockSpec((tm, 3 * D), row), pl.BlockSpec((1, 3 * D), const),
                  pl.BlockSpec((FOX_W, D), const), pl.BlockSpec((DSA_W, D), const),
                  pl.BlockSpec((RWKV_W, D), const), pl.BlockSpec((D, D), const),
                  pl.BlockSpec((1, D), const)],
        out_specs=[pl.BlockSpec((tm, D), row), pl.BlockSpec((tm, D), row)],
        out_shape=[jax.ShapeDtypeStruct((S, D), F32), jax.ShapeDtypeStruct((S, D), BF16)],
        compiler_params=_params("arbitrary"),
        name="merge",
    )(x, o_fox, o_dsa, o_rwkv, gate_p, b_gate.reshape(1, -1), p_fox, p_dsa, p_rwkv, w_out,
      g_ffn.reshape(1, D))


def _swiglu_tile(h, wg, wu):
    gate = jnp.dot(h, wg, preferred_element_type=F32)
    up = jnp.dot(h, wu, preferred_element_type=F32)
    return gate * _sigmoid(gate) * up


def _ffn_kernel(x_ref, h_ref, wg_ref, wu_ref, wd_ref, gf_ref, o_ref, acc_sc, *, final_norm):
    f = pl.program_id(1)

    @pl.when(f == 0)
    def _():
        acc_sc[...] = x_ref[...]

    act = _swiglu_tile(h_ref[...], wg_ref[...], wu_ref[...])
    acc_sc[...] += jnp.dot(act.astype(BF16), wd_ref[...], preferred_element_type=F32)

    @pl.when(f == pl.num_programs(1) - 1)
    def _():
        y = acc_sc[...]
        o_ref[...] = _rms(y, gf_ref[...]) if final_norm else y


def _ffn(x, h, wg, wu, wd, g_final, final_norm):
    S, D = x.shape
    Fd = wg.shape[1]
    tm = min(512, S)
    tf = _pick_tile(Fd, 1408)
    return pl.pallas_call(
        functools.partial(_ffn_kernel, final_norm=final_norm),
        grid=(S // tm, Fd // tf),
        in_specs=[pl.BlockSpec((tm, D), lambda i, f: (i, 0)),
                  pl.BlockSpec((tm, D), lambda i, f: (i, 0)),
                  pl.BlockSpec((D, tf), lambda i, f: (0, f)),
                  pl.BlockSpec((D, tf), lambda i, f: (0, f)),
                  pl.BlockSpec((tf, D), lambda i, f: (f, 0)),
                  pl.BlockSpec((1, D), lambda i, f: (0, 0))],
        out_specs=pl.BlockSpec((tm, D), lambda i, f: (i, 0)),
        out_shape=jax.ShapeDtypeStruct((S, D), F32),
        scratch_shapes=[pltpu.VMEM((tm, D), F32)],
        compiler_params=_params("arbitrary", "arbitrary"),
        name="ffn",
    )(x, h, wg, wu, wd, g_final.reshape(1, D))


def _moe_kernel(x_ref, h_ref, rw_ref, rb_ref, wg_ref, wu_ref, wd_ref, gf_ref, o_ref,
                acc_sc, gate_sc, *, final_norm):
    e = pl.program_id(1)
    f = pl.program_id(2)
    lane = lax.broadcasted_iota(I32, gate_sc.shape, 1)

    @pl.when((e == 0) & (f == 0))
    def _():
        acc_sc[...] = x_ref[...]
        logits = jnp.dot(h_ref[...], rw_ref[...], preferred_element_type=F32) + rb_ref[...]
        logits = jnp.where(lane < N_EXPERTS, logits, -jnp.inf)
        v1 = jnp.max(logits, axis=-1, keepdims=True)
        i1 = jnp.min(jnp.where(logits == v1, lane, LANE), axis=-1, keepdims=True)
        rest = jnp.where(lane == i1, -jnp.inf, logits)
        v2 = jnp.max(rest, axis=-1, keepdims=True)
        i2 = jnp.min(jnp.where(rest == v2, lane, LANE), axis=-1, keepdims=True)
        e2 = jnp.exp(v2 - v1)
        p1 = 1.0 / (1.0 + e2)
        gate_sc[...] = jnp.where(lane == i1, p1, 0.0) + jnp.where(lane == i2, e2 * p1, 0.0)

    gate = jnp.sum(jnp.where(lane == e, gate_sc[...], 0.0), axis=-1, keepdims=True)
    act = _swiglu_tile(h_ref[...], wg_ref[0], wu_ref[0]) * gate
    acc_sc[...] += jnp.dot(act.astype(BF16), wd_ref[0], preferred_element_type=F32)

    @pl.when((e == pl.num_programs(1) - 1) & (f == pl.num_programs(2) - 1))
    def _():
        y = acc_sc[...]
        o_ref[...] = _rms(y, gf_ref[...]) if final_norm else y


def _moe(x, h, rw, rb, wg, wu, wd, g_final, final_norm):
    S, D = x.shape
    E, _, Fe = wg.shape
    tm = min(1024, S)
    tf = _pick_tile(Fe, 896)
    return pl.pallas_call(
        functools.partial(_moe_kernel, final_norm=final_norm),
        grid=(S // tm, E, Fe // tf),
        in_specs=[pl.BlockSpec((tm, D), lambda i, e, f: (i, 0)),
                  pl.BlockSpec((tm, D), lambda i, e, f: (i, 0)),
                  pl.BlockSpec((D, LANE), lambda i, e, f: (0, 0)),
                  pl.BlockSpec((1, LANE), lambda i, e, f: (0, 0)),
                  pl.BlockSpec((1, D, tf), lambda i, e, f: (e, 0, f)),
                  pl.BlockSpec((1, D, tf), lambda i, e, f: (e, 0, f)),
                  pl.BlockSpec((1, tf, D), lambda i, e, f: (e, f, 0)),
                  pl.BlockSpec((1, D), lambda i, e, f: (0, 0))],
        out_specs=pl.BlockSpec((tm, D), lambda i, e, f: (i, 0)),
        out_shape=jax.ShapeDtypeStruct((S, D), F32),
        scratch_shapes=[pltpu.VMEM((tm, D), F32), pltpu.VMEM((tm, LANE), F32)],
        compiler_params=_params("arbitrary", "arbitrary", "arbitrary"),
        name="moe",
    )(x, h, rw, rb, wg, wu, wd, g_final.reshape(1, D))


def _pad_cols(a, n):
    return jnp.pad(a, ((0, 0), (0, n - a.shape[1])))


def _heads(a, n_heads):
    S = a.shape[0]
    return a.reshape(S, n_heads, -1).transpose(1, 0, 2)


def _unheads(a):
    H, S, Dh = a.shape
    return a.transpose(1, 0, 2).reshape(S, H * Dh)


def kernel(x, w_in, b_gate, g_mix, fox_f_bias, dsa_kv_norm, dsa_w_uk, dsa_w_uv, rwkv_mu, rwkv_w0, rwkv_w_up, rwkv_a0, rwkv_a_up, rwkv_g_up, rwkv_k_k, rwkv_k_a, rwkv_r_k, rwkv_ln_g, rwkv_ln_b, vres_down, vres_mu, vres_up, vres_v0, p_fox, p_dsa, p_rwkv, w_out, g_ffn, ffn_w_gate, ffn_w_up, ffn_w_down, router_w, router_b, moe_w_gate, moe_w_up, moe_w_down, g_final):
    B, S, D = x.shape
    assert B == 1 and D == D_MODEL and S % LANE == 0
    depth = w_in.shape[0]
    topk = min(IDX_TOPK, S // 4)
    bf = lambda a: a.astype(BF16)
    xs = x[0]
    v_first = None
    for l in range(depth):
        wl = w_in[l]
        w_fox, w_dsa, w_rwkv, w_gate = (wl[:, :FOX_IN], wl[:, FOX_IN:FOX_IN + DSA_IN],
                                        wl[:, FOX_IN + DSA_IN:FOX_IN + DSA_IN + RWKV_IN],
                                        wl[:, FOX_IN + DSA_IN + RWKV_IN:])
        o1 = DSA_W + DSA_KV_RANK + IDX_W
        w_att = jnp.concatenate([_pad_cols(w_fox, 3 * FOX_W + LANE), w_dsa[:, :o1],
                                 _pad_cols(w_dsa[:, o1:], LANE)], axis=1)
        if l > 0:
            w_rwkv = jnp.concatenate([w_rwkv, _pad_cols(vres_down[l - 1], LANE)], axis=1)
        pa = _rms_proj(xs, g_mix[l], bf(w_att))
        pr = _rms_proj(xs, g_mix[l], bf(w_rwkv))
        gate_p = _rms_proj(xs, g_mix[l], bf(w_gate))

        c0 = 3 * FOX_W
        fl = pa[:, c0:c0 + FOX_HEADS].T.reshape(FOX_HEADS, S // LANE, LANE)
        F = _fox_cumsum(fl, fox_f_bias[l]).reshape(FOX_HEADS, S)
        fq = bf(_heads(pa[:, 0:FOX_W], FOX_HEADS) * (FOX_HEAD_DIM ** -0.5))
        fkT = bf(pa[:, FOX_W:2 * FOX_W].reshape(S, FOX_HEADS, FOX_HEAD_DIM).transpose(1, 2, 0))
        fv = bf(_heads(pa[:, 2 * FOX_W:3 * FOX_W], FOX_HEADS))
        o_fox = _unheads(_fox_attention(fq, fkT, fv, F))

        c1 = c0 + LANE
        dq = bf(_heads(pa[:, c1:c1 + DSA_W], DSA_HEADS))
        c2 = c1 + DSA_W
        ckv = _rmsnorm(pa[:, c2:c2 + DSA_KV_RANK], dsa_kv_norm[l], BF16)
        c3 = c2 + DSA_KV_RANK
        qi = bf(_heads(pa[:, c3:c3 + IDX_W], IDX_HEADS))
        c4 = c3 + IDX_W
        kiT = bf(pa[:, c4:c4 + IDX_HEAD_DIM].T)
        c5 = c4 + IDX_HEAD_DIM
        wi = pa[:, c5:c5 + IDX_HEADS].T.reshape(IDX_HEADS, S, 1) * (IDX_W ** -0.5)
        o_dsa = _unheads(_dsa_attention(qi, wi, dq, bf(dsa_w_uk[l]), bf(dsa_w_uv[l]),
                                        kiT, ckv.T, ckv, topk))

        zw = jnp.zeros((RWKV_W_LORA, RWKV_W), F32)
        wwa = bf(jnp.concatenate([jnp.concatenate([rwkv_w_up[l], zw], axis=1),
                                  jnp.concatenate([zw, rwkv_a_up[l]], axis=1)], axis=0))
        vres = None
        if l > 0:
            vup = jnp.pad(vres_up[l - 1], ((0, LANE - RWKV_V_LORA), (0, 0)))
            vres = (pr[:, RWKV_IN:], jnp.pad(vres_mu[l - 1], (0, LANE - RWKV_V_LORA)),
                    bf(vup), vres_v0[l - 1], v_first)
        r, k, v, lw, kk, a, g = _rwkv_pre(pr[:, :RWKV_IN], rwkv_mu[l], rwkv_w0[l], rwkv_a0[l], wwa,
                                          bf(rwkv_g_up[l]), rwkv_k_k[l], rwkv_k_a[l], vres)
        if l == 0:
            v_first = v
        hd = lambda t: _heads(t, RWKV_HEADS)
        o_rwkv = _unheads(_rwkv_scan(hd(r), hd(k), hd(v), hd(lw), hd(kk), hd(a), hd(g),
                                     rwkv_r_k[l], rwkv_ln_g[l], rwkv_ln_b[l]))

        xs, h2 = _merge(xs, o_fox, o_dsa, o_rwkv, gate_p, b_gate[l], bf(p_fox[l]), bf(p_dsa[l]),
                        bf(p_rwkv[l]), bf(w_out[l]), g_ffn[l])

        last = l == depth - 1
        if l % 2 == 0:
            xs = _ffn(xs, h2, bf(ffn_w_gate[l // 2]), bf(ffn_w_up[l // 2]), bf(ffn_w_down[l // 2]),
                      g_final, last)
        else:
            rw = bf(_pad_cols(router_w[l // 2], LANE))
            rb = _pad_cols(router_b[l // 2].reshape(1, -1), LANE)
            xs = _moe(xs, h2, rw, rb, bf(moe_w_gate[l // 2]), bf(moe_w_up[l // 2]),
                      bf(moe_w_down[l // 2]), g_final, last)
    return xs[None]
```

```python
import functools

import jax
import jax.numpy as jnp
from jax import lax
from jax.experimental import pallas as pl
from jax.experimental.pallas import tpu as pltpu

F32 = jnp.float32
BF16 = jnp.bfloat16
I32 = jnp.int32

D_MODEL = 1024
CHUNK = 64
RMS_EPS = 1e-6
FOX_HEADS, FOX_HEAD_DIM = 4, 64
DSA_HEADS, DSA_HEAD_DIM, DSA_KV_RANK = 4, 64, 128
IDX_HEADS, IDX_HEAD_DIM, IDX_TOPK = 8, 32, 256
RWKV_HEADS, RWKV_HEAD_DIM = 8, 64
RWKV_W_LORA, RWKV_A_LORA, RWKV_V_LORA, RWKV_G_LORA = 64, 64, 32, 128
RWKV_LN_EPS = 64e-5
FOX_W = FOX_HEADS * FOX_HEAD_DIM
DSA_W = DSA_HEADS * DSA_HEAD_DIM
RWKV_W = RWKV_HEADS * RWKV_HEAD_DIM
IDX_W = IDX_HEADS * IDX_HEAD_DIM
N_EXPERTS = 8
FOX_IN = 3 * FOX_W + FOX_HEADS
DSA_IN = DSA_W + DSA_KV_RANK + IDX_W + IDX_HEAD_DIM + IDX_HEADS
RWKV_IN = 3 * RWKV_W + RWKV_W_LORA + RWKV_A_LORA + RWKV_G_LORA

LANE = 128
VMEM_LIMIT = 52 * 1024 * 1024
NEG = -1e30
INT_MIN = -(2 ** 31)
HI = lax.Precision.HIGHEST


def _params(*sem):
    return pltpu.CompilerParams(dimension_semantics=sem, vmem_limit_bytes=VMEM_LIMIT)


def _pick_tile(n, cap):
    best = LANE
    for t in range(LANE, min(n, cap) + 1, LANE):
        if n % t == 0:
            best = t
    return best


def _softplus(x):
    return jnp.maximum(x, 0.0) + jnp.log1p(jnp.exp(-jnp.abs(x)))


def _sigmoid(x):
    return 1.0 / (1.0 + jnp.exp(-x))


def _rms(x, g):
    return x * lax.rsqrt(jnp.mean(x * x, axis=-1, keepdims=True) + RMS_EPS) * g


def _rms_proj_kernel(x_ref, g_ref, w_ref, o_ref):
    h = _rms(x_ref[...], g_ref[...])
    o_ref[...] = jnp.dot(h.astype(BF16), w_ref[...], preferred_element_type=F32)


def _rms_proj(x, g, w):
    S, D = x.shape
    N = w.shape[1]
    tm = min(512, S)
    tn = _pick_tile(N, 2304)
    return pl.pallas_call(
        _rms_proj_kernel,
        grid=(N // tn, S // tm),
        in_specs=[pl.BlockSpec((tm, D), lambda j, i: (i, 0)),
                  pl.BlockSpec((1, D), lambda j, i: (0, 0)),
                  pl.BlockSpec((D, tn), lambda j, i: (0, j))],
        out_specs=pl.BlockSpec((tm, tn), lambda j, i: (i, j)),
        out_shape=jax.ShapeDtypeStruct((S, N), F32),
        compiler_params=_params("arbitrary", "arbitrary"),
        name="rms_proj",
    )(x, g.reshape(1, D), w)


def _rmsnorm_kernel(x_ref, g_ref, o_ref):
    o_ref[...] = _rms(x_ref[...], g_ref[...]).astype(o_ref.dtype)


def _rmsnorm(x, g, dtype):
    S, D = x.shape
    tm = min(2048, S)
    return pl.pallas_call(
        _rmsnorm_kernel,
        grid=(S // tm,),
        in_specs=[pl.BlockSpec((tm, D), lambda i: (i, 0)),
                  pl.BlockSpec((1, D), lambda i: (0, 0))],
        out_specs=pl.BlockSpec((tm, D), lambda i: (i, 0)),
        out_shape=jax.ShapeDtypeStruct((S, D), dtype),
        compiler_params=_params("arbitrary"),
        name="rmsnorm",
    )(x, g.reshape(1, D))


def _fox_cumsum_kernel(fl_ref, b_ref, o_ref):
    H, R, _ = fl_ref.shape
    upper = (lax.broadcasted_iota(I32, (LANE, LANE), 0)
             <= lax.broadcasted_iota(I32, (LANE, LANE), 1)).astype(F32)
    strict_lower = (lax.broadcasted_iota(I32, (R, R), 0)
                    > lax.broadcasted_iota(I32, (R, R), 1)).astype(F32)
    for h in range(H):
        log_f = -_softplus(-(fl_ref[h] + b_ref[h]))
        within = jnp.dot(log_f, upper, preferred_element_type=F32, precision=HI)
        row_tot = jnp.broadcast_to(within[:, LANE - 1:LANE], (R, LANE))
        before = jnp.dot(strict_lower, row_tot, preferred_element_type=F32, precision=HI)
        o_ref[h] = within + before


def _fox_cumsum(fl, bias):
    H, R, _ = fl.shape
    return pl.pallas_call(
        _fox_cumsum_kernel,
        out_shape=jax.ShapeDtypeStruct((H, R, LANE), F32),
        compiler_params=pltpu.CompilerParams(vmem_limit_bytes=VMEM_LIMIT),
        name="fox_cumsum",
    )(fl, jnp.broadcast_to(bias.reshape(H, 1, 1), (H, 1, LANE)))


FOX_KA = 128


def _fox_kernel(qa_ref, ka_ref, vT_ref, o_ref, *, t, tg):
    i = pl.program_id(1)
    qa = qa_ref[0]
    Dh = vT_ref.shape[1]
    last = (i * t) // tg

    def step(g, carry, masked):
        m_prev, l_prev, acc = carry
        off = pl.multiple_of(g * tg, tg)
        s = jnp.dot(ka_ref[0, pl.ds(off, tg), :], qa, preferred_element_type=F32)
        if masked:
            kpos = off + lax.broadcasted_iota(I32, (tg, t), 0)
            qpos = i * t + lax.broadcasted_iota(I32, (tg, t), 1)
            s = jnp.where(kpos <= qpos, s, NEG)
        m_new = jnp.maximum(m_prev, jnp.max(s, axis=0, keepdims=True))
        alpha = jnp.exp(m_prev - m_new)
        p = jnp.exp(s - m_new)
        l_new = alpha * l_prev + jnp.sum(p, axis=0, keepdims=True)
        acc = alpha * acc + jnp.dot(vT_ref[0, :, pl.ds(off, tg)], p.astype(BF16),
                                    preferred_element_type=F32)
        return m_new, l_new, acc

    init = (jnp.full((1, t), NEG, F32), jnp.zeros((1, t), F32), jnp.zeros((Dh, t), F32))
    carry = lax.fori_loop(0, last, lambda g, c: step(g, c, False), init)
    _, l_fin, acc = step(last, carry, True)
    o_ref[0] = (acc / l_fin).astype(o_ref.dtype)


def _fox_attention(qaT, ka, vT):
    H, KA, S = qaT.shape
    Dh = vT.shape[1]
    t = min(256, S)
    tg = min(1024, S)
    return pl.pallas_call(
        functools.partial(_fox_kernel, t=t, tg=tg),
        grid=(H, S // t),
        in_specs=[pl.BlockSpec((1, KA, t), lambda h, i: (h, 0, i)),
                  pl.BlockSpec((1, S, KA), lambda h, i: (h, 0, 0)),
                  pl.BlockSpec((1, Dh, S), lambda h, i: (h, 0, 0))],
        out_specs=pl.BlockSpec((1, Dh, t), lambda h, i: (h, 0, i)),
        out_shape=jax.ShapeDtypeStruct((H, Dh, S), BF16),
        compiler_params=_params("arbitrary", "arbitrary"),
        name="fox_attention",
    )(qaT, ka, vT)


def _split3(f):
    hi = f.astype(BF16)
    r1 = f - hi.astype(F32)
    mid = r1.astype(BF16)
    lo = (r1 - mid.astype(F32)).astype(BF16)
    return hi, mid, lo


DSA_T = 128


def _dsa_kernel(qiT_ref, w_ref, qT_ref, wukT_ref, wuvT_ref, ki_ref, c_ref, cT_ref, o_ref,
                keys_sc, qlat_sc, m_sc, l_sc, acc_sc, *, t, tg, topk):
    i = pl.program_id(0)
    last = (i * t) // tg
    ng = last + 1

    qiT = qiT_ref[0]
    w = w_ref[0]

    def score_keys(g, masked):
        off = pl.multiple_of(g * tg, tg)
        rel = jnp.dot(ki_ref[pl.ds(off, tg), :], qiT, preferred_element_type=F32)
        sc = jnp.maximum(rel[:, 0:t], 0.0) * w[0:1, :]
        for h in range(1, IDX_HEADS):
            sc = sc + jnp.maximum(rel[:, h * t:(h + 1) * t], 0.0) * w[h:h + 1, :]
        bits = pltpu.bitcast(sc, I32)
        key = bits ^ ((bits >> 31) & 0x7FFFFFFF)
        key = jnp.where(sc == 0.0, 0, key)
        if masked:
            kchunk = (off + lax.broadcasted_iota(I32, (tg, t), 0)) // CHUNK
            qchunk = (i * t + lax.broadcasted_iota(I32, (tg, t), 1)) // CHUNK
            key = jnp.where(kchunk <= qchunk, key, INT_MIN)
        keys_sc[pl.ds(off, tg), :] = key

    def fill(g, c):
        score_keys(g, False)
        return c

    lax.fori_loop(0, last, fill, 0)
    score_keys(last, True)

    def count_ge(cand):
        cb = jnp.broadcast_to(cand, (t, t))

        def body(g, acc):
            off = pl.multiple_of(g * tg, tg)
            for k in range(tg // t):
                acc = acc + (keys_sc[pl.ds(off + k * t, t), :] >= cb).astype(F32)
            return acc

        acc = lax.fori_loop(0, ng, body, jnp.zeros((t, t), F32))
        return jnp.sum(acc, axis=0, keepdims=True)

    def bisect(s, thr):
        cand = thr + jnp.left_shift(jnp.int32(1), 31 - s)
        return jnp.where(count_ge(cand) >= topk, cand, thr)

    thr = lax.fori_loop(0, 32, bisect, jnp.full((1, t), INT_MIN, I32))
    thr = jnp.maximum(thr, INT_MIN + 1)
    n_ge = count_ge(thr)
    need = topk - count_ge(thr + 1)
    has_ties = jnp.max(jnp.where(n_ge > topk, 1.0, 0.0)) > 0.0
    thr_b = jnp.broadcast_to(thr, (tg, t))

    for h in range(DSA_HEADS):
        ql = jnp.dot(wukT_ref[h], qT_ref[0, h], preferred_element_type=F32)
        qlat_sc[:, h * t:(h + 1) * t] = (ql * (DSA_HEAD_DIM ** -0.5)).astype(BF16)
    m_sc[...] = jnp.full(m_sc.shape, NEG, F32)
    l_sc[...] = jnp.zeros(l_sc.shape, F32)
    acc_sc[...] = jnp.zeros(acc_sc.shape, F32)

    def attend(g, seen_eq, with_ties):
        off = pl.multiple_of(g * tg, tg)
        key = keys_sc[pl.ds(off, tg), :]
        if with_ties:
            eq = key == thr_b
            eqf = eq.astype(F32)
            strict_lower = (lax.broadcasted_iota(I32, (tg, tg), 0)
                            > lax.broadcasted_iota(I32, (tg, tg), 1)).astype(BF16)
            rank = jnp.dot(strict_lower, eqf.astype(BF16), preferred_element_type=F32) + seen_eq
            sel = (key > thr_b) | (eq & (rank < need))
            seen_eq = seen_eq + jnp.sum(eqf, axis=0, keepdims=True)
        else:
            sel = key >= thr_b
        bias = jnp.where(sel, 0.0, NEG)
        lg = jnp.dot(c_ref[pl.ds(off, tg), :], qlat_sc[...], preferred_element_type=F32)
        cT_tile = cT_ref[:, pl.ds(off, tg)]
        for h in range(DSA_HEADS):
            s = lg[:, h * t:(h + 1) * t] + bias
            m_prev = m_sc[h]
            m_new = jnp.maximum(m_prev, jnp.max(s, axis=0, keepdims=True))
            alpha = jnp.exp(m_prev - m_new)
            p = jnp.exp(s - m_new)
            l_sc[h] = alpha * l_sc[h] + jnp.sum(p, axis=0, keepdims=True)
            acc_sc[h] = alpha * acc_sc[h] + jnp.dot(cT_tile, p.astype(BF16),
                                                    preferred_element_type=F32)
            m_sc[h] = m_new
        return seen_eq

    @pl.when(has_ties)
    def _():
        lax.fori_loop(0, ng, lambda g, s: attend(g, s, True), jnp.zeros((1, t), F32))

    @pl.when(jnp.logical_not(has_ties))
    def _():
        lax.fori_loop(0, ng, lambda g, s: attend(g, s, False), jnp.zeros((1, t), F32))

    for h in range(DSA_HEADS):
        o_lat = (acc_sc[h] / l_sc[h]).astype(BF16)
        o_ref[0, h] = jnp.dot(wuvT_ref[h], o_lat, preferred_element_type=F32).astype(o_ref.dtype)


def _dsa_attention(qiT, w, qT, wukT, wuvT, ki, c, cT, topk):
    NB, H, Dh, t = qT.shape
    S = NB * t
    R = DSA_KV_RANK
    tg = min(512, S)
    const2 = lambda i: (0, 0)
    const3 = lambda i: (0, 0, 0)
    return pl.pallas_call(
        functools.partial(_dsa_kernel, t=t, tg=tg, topk=topk),
        grid=(NB,),
        in_specs=[pl.BlockSpec((1, IDX_HEAD_DIM, IDX_HEADS * t), lambda i: (i, 0, 0)),
                  pl.BlockSpec((1, IDX_HEADS, t), lambda i: (i, 0, 0)),
                  pl.BlockSpec((1, H, Dh, t), lambda i: (i, 0, 0, 0)),
                  pl.BlockSpec((H, R, Dh), const3),
                  pl.BlockSpec((H, Dh, R), const3),
                  pl.BlockSpec((S, IDX_HEAD_DIM), const2),
                  pl.BlockSpec((S, R), const2),
                  pl.BlockSpec((R, S), const2)],
        out_specs=pl.BlockSpec((1, H, Dh, t), lambda i: (i, 0, 0, 0)),
        out_shape=jax.ShapeDtypeStruct((NB, H, Dh, t), BF16),
        scratch_shapes=[pltpu.VMEM((S, t), I32), pltpu.VMEM((R, H * t), BF16),
                        pltpu.VMEM((H, 1, t), F32), pltpu.VMEM((H, 1, t), F32),
                        pltpu.VMEM((H, R, t), F32)],
        compiler_params=_params("arbitrary"),
        name="dsa_attention",
    )(qiT, w, qT, wukT, wuvT, ki, c, cT)


def _shift_mix(p, prev_tail, mu, first_block):
    rows = lax.broadcasted_iota(I32, p.shape, 0)
    tail = jnp.where(first_block, 0.0, prev_tail)
    prev = jnp.where(rows == 0, tail, pltpu.roll(p, 1, axis=0))
    return p + (prev - p) * mu


def _rwkv_pre_kernel(*refs, with_vres):
    if with_vres:
        (p_ref, pp_ref, mu_ref, w0_ref, a0_ref, wwa_ref, gup_ref, kk_ref, ka_ref,
         hv_ref, hvp_ref, vmu_ref, vup_ref, vv0_ref, vf_ref,
         r_o, k_o, v_o, lw_o, kk_o, a_o, g_o) = refs
    else:
        (p_ref, pp_ref, mu_ref, w0_ref, a0_ref, wwa_ref, gup_ref, kk_ref, ka_ref,
         r_o, k_o, v_o, lw_o, kk_o, a_o, g_o) = refs
    first = pl.program_id(0) == 0
    W = RWKV_W
    ps = _shift_mix(p_ref[...], pp_ref[7:8, :], mu_ref[...], first)
    r, k, v = ps[:, 0:W], ps[:, W:2 * W], ps[:, 2 * W:3 * W]
    wa = ps[:, 3 * W:3 * W + LANE]
    gl = ps[:, 3 * W + LANE:3 * W + 2 * LANE]
    lane = lax.broadcasted_iota(I32, wa.shape, 1)
    wa = jnp.where(lane < RWKV_W_LORA, jnp.tanh(wa), wa)
    up = jnp.dot(wa.astype(BF16), wwa_ref[...], preferred_element_type=F32)
    log_w = -_softplus(-(w0_ref[...] + up[:, 0:W])) - 0.5
    a = _sigmoid(a0_ref[...] + up[:, W:2 * W])
    g = jnp.dot(_sigmoid(gl).astype(BF16), gup_ref[...], preferred_element_type=F32)
    if with_vres:
        vl = _shift_mix(hv_ref[...], hvp_ref[7:8, :], vmu_ref[...], first)
        logit = vv0_ref[...] + jnp.dot(vl.astype(BF16), vup_ref[...], preferred_element_type=F32)
        v = v + (vf_ref[...] - v) * _sigmoid(logit)
    r_o[...] = r
    k_o[...] = k * (1.0 + (a - 1.0) * ka_ref[...])
    v_o[...] = v
    lw_o[...] = -jnp.exp(log_w)
    kk_o[...] = k * kk_ref[...]
    a_o[...] = a
    g_o[...] = g


def _rwkv_pre(p, mu, w0, a0, wwa, gup, k_k, k_a, vres=None):
    S, PW = p.shape
    W = RWKV_W
    tm = min(512, S)
    row = lambda i: (i, 0)
    const = lambda i: (0, 0)
    tail = lambda i: (jnp.maximum(i * (tm // 8) - 1, 0), 0)
    vec = lambda a: a.reshape(1, -1)
    args = [p, p, vec(mu), vec(w0), vec(a0), wwa, gup, vec(k_k), vec(k_a)]
    specs = [pl.BlockSpec((tm, PW), row), pl.BlockSpec((8, PW), tail),
             pl.BlockSpec((1, PW), const), pl.BlockSpec((1, W), const), pl.BlockSpec((1, W), const),
             pl.BlockSpec(wwa.shape, const), pl.BlockSpec(gup.shape, const),
             pl.BlockSpec((1, W), const), pl.BlockSpec((1, W), const)]
    if vres is not None:
        hv, vmu, vup, vv0, v_first = vres
        args += [hv, hv, vec(vmu), vup, vec(vv0), v_first]
        specs += [pl.BlockSpec((tm, LANE), row), pl.BlockSpec((8, LANE), tail),
                  pl.BlockSpec((1, LANE), const), pl.BlockSpec(vup.shape, const),
                  pl.BlockSpec((1, W), const), pl.BlockSpec((tm, W), row)]
    return pl.pallas_call(
        functools.partial(_rwkv_pre_kernel, with_vres=vres is not None),
        grid=(S // tm,),
        in_specs=specs,
        out_specs=[pl.BlockSpec((tm, W), row)] * 7,
        out_shape=[jax.ShapeDtypeStruct((S, W), F32)] * 7,
        compiler_params=_params("arbitrary"),
        name="rwkv_pre",
    )(*args)


def _rwkv_scan_kernel(r_ref, k_ref, v_ref, lw_ref, kk_ref, a_ref, g_ref, rk_ref, lng_ref, lnb_ref,
                      o_ref, h_sc, *, chunks):
    C = CHUNK
    N = RWKV_HEAD_DIM

    @pl.when(pl.program_id(0) == 0)
    def _():
        h_sc[...] = jnp.zeros(h_sc.shape, F32)

    ti = lax.broadcasted_iota(I32, (C, C), 0)
    tj = lax.broadcasted_iota(I32, (C, C), 1)
    lower_incl = ti >= tj
    lower_strict = ti > tj
    tri_incl = lower_incl.astype(F32)
    eye = (ti == tj).astype(F32)
    eye_n = (lax.broadcasted_iota(I32, (N, N), 0) == lax.broadcasted_iota(I32, (N, N), 1)).astype(F32)

    def mm(x, y):
        return jnp.dot(x.astype(BF16), y.astype(BF16), preferred_element_type=F32)

    def mm_nt(x, y):
        return lax.dot_general(x.astype(BF16), y.astype(BF16), (((1,), (1,)), ((), ())),
                               preferred_element_type=F32)

    def mm_tn(x, y):
        return lax.dot_general(x.astype(BF16), y.astype(BF16), (((0,), (0,)), ((), ())),
                               preferred_element_type=F32)

    for h in range(RWKV_HEADS):
        H = h_sc[h]
        for c in range(chunks):
            sl = slice(c * C, (c + 1) * C)
            r, k, v = r_ref[h, sl, :], k_ref[h, sl, :], v_ref[h, sl, :]
            lw, kkr, a = lw_ref[h, sl, :], kk_ref[h, sl, :], a_ref[h, sl, :]
            kk = kkr * lax.rsqrt(jnp.sum(kkr * kkr, axis=-1, keepdims=True) + 1e-12)
            cum = jnp.dot(tri_incl, lw, preferred_element_type=F32, precision=HI)
            cum_end = cum[C - 1:C, :]
            p_in = jnp.exp(cum)
            p_inv = jnp.exp(-cum)
            p_end = jnp.exp(cum_end - cum)
            beta = kk * a
            At = -kk * jnp.exp(cum - lw)
            Rt = r * p_in
            M = mm_nt(jnp.concatenate([At, Rt], axis=0),
                      jnp.concatenate([beta * p_inv, k * p_inv], axis=0))
            Lab = jnp.where(lower_strict, M[0:C, 0:C], 0.0)
            Lak = jnp.where(lower_strict, M[0:C, C:2 * C], 0.0)
            Mrb = jnp.where(lower_incl, M[C:2 * C, 0:C], 0.0)
            Mrk = jnp.where(lower_incl, M[C:2 * C, C:2 * C], 0.0)
            T = eye + Lab
            Lp = Lab
            span = 2
            while span < C:
                Lp = mm(Lp, Lp)
                T = T + mm(Lp, T)
                span *= 2
            AU = mm(T, jnp.concatenate([At, mm(Lak, v)], axis=1))
            RY = mm(Mrb, AU)
            R2 = Rt + RY[:, 0:N]
            Y0 = RY[:, N:2 * N] + mm(Mrk, v)
            GH = mm_tn(beta * p_end, AU)
            G = eye_n * jnp.exp(cum_end) + GH[:, 0:N]
            H0 = GH[:, N:2 * N] + mm_tn(k * p_end, v)
            y = mm(R2, H) + Y0
            H = mm(G, H) + H0
            mean = jnp.mean(y, axis=-1, keepdims=True)
            yc = y - mean
            var = jnp.mean(yc * yc, axis=-1, keepdims=True)
            yn = yc * lax.rsqrt(var + RWKV_LN_EPS) * lng_ref[h] + lnb_ref[h]
            bonus = jnp.sum(r * k * rk_ref[h], axis=-1, keepdims=True) * v
            o_ref[h, sl, :] = ((yn + bonus) * g_ref[h, sl, :]).astype(o_ref.dtype)
        h_sc[h] = H


def _rwkv_scan(r, k, v, lw, kk, a, g, r_k, ln_g, ln_b):
    Hr, S, N = r.shape
    chunks = 2 if S % (2 * CHUNK) == 0 else 1
    tb = chunks * CHUNK
    seq = pl.BlockSpec((Hr, tb, N), lambda i: (0, i, 0))
    par = pl.BlockSpec((Hr, 1, N), lambda i: (0, 0, 0))
    return pl.pallas_call(
        functools.partial(_rwkv_scan_kernel, chunks=chunks),
        grid=(S // tb,),
        in_specs=[seq] * 7 + [par] * 3,
        out_specs=seq,
        out_shape=jax.ShapeDtypeStruct((Hr, S, N), BF16),
        scratch_shapes=[pltpu.VMEM((Hr, N, N), F32)],
        compiler_params=_params("arbitrary"),
        name="rwkv_scan",
    )(r, k, v, lw, kk, a, g, r_k.reshape(Hr, 1, N), ln_g.reshape(Hr, 1, N), ln_b.reshape(Hr, 1, N))


def _merge_kernel(x_ref, of_ref, od_ref, or_ref, gp_ref, bg_ref, pf_ref, pd_ref, pr_ref, wo_ref,
                  gn_ref, xo_ref, h_ref):
    D = D_MODEL
    gates = _sigmoid(gp_ref[...] + bg_ref[...])
    merged = (gates[:, 0:D] * jnp.dot(of_ref[...], pf_ref[...], preferred_element_type=F32)
              + gates[:, D:2 * D] * jnp.dot(od_ref[...], pd_ref[...], preferred_element_type=F32)
              + gates[:, 2 * D:3 * D] * jnp.dot(or_ref[...], pr_ref[...], preferred_element_type=F32))
    x = x_ref[...] + jnp.dot(merged.astype(BF16), wo_ref[...], preferred_element_type=F32)
    xo_ref[...] = x
    h_ref[...] = _rms(x, gn_ref[...]).astype(h_ref.dtype)


def _merge(x, o_fox, o_dsa, o_rwkv, gate_p, b_gate, p_fox, p_dsa, p_rwkv, w_out, g_ffn):
    S, D = x.shape
    tm = min(512, S)
    row = lambda i: (i, 0)
    const = lambda i: (0, 0)
    return pl.pallas_call(
        _merge_kernel,
        grid=(S // tm,),
        in_specs=[pl.BlockSpec((tm, D), row), pl.BlockSpec((tm, FOX_W), row),
                  pl.BlockSpec((tm, DSA_W), row), pl.BlockSpec((tm, RWKV_W), row),
                  pl.BlockSpec((tm, 3 * D), row), pl.BlockSpec((1, 3 * D), const),
                  pl.BlockSpec((FOX_W, D), const), pl.BlockSpec((DSA_W, D), const),
                  pl.BlockSpec((RWKV_W, D), const), pl.BlockSpec((D, D), const),
                  pl.BlockSpec((1, D), const)],
        out_specs=[pl.BlockSpec((tm, D), row), pl.BlockSpec((tm, D), row)],
        out_shape=[jax.ShapeDtypeStruct((S, D), F32), jax.ShapeDtypeStruct((S, D), BF16)],
        compiler_params=_params("arbitrary"),
        name="merge",
    )(x, o_fox, o_dsa, o_rwkv, gate_p, b_gate.reshape(1, -1), p_fox, p_dsa, p_rwkv, w_out,
      g_ffn.reshape(1, D))


def _swiglu_tile(h, wg, wu):
    gate = jnp.dot(h, wg, preferred_element_type=F32)
    up = jnp.dot(h, wu, preferred_element_type=F32)
    return gate * _sigmoid(gate) * up


def _ffn_kernel(x_ref, h_ref, wg_ref, wu_ref, wd_ref, gf_ref, o_ref, acc_sc, *, final_norm):
    f = pl.program_id(1)

    @pl.when(f == 0)
    def _():
        acc_sc[...] = x_ref[...]

    act = _swiglu_tile(h_ref[...], wg_ref[...], wu_ref[...])
    acc_sc[...] += jnp.dot(act.astype(BF16), wd_ref[...], preferred_element_type=F32)

    @pl.when(f == pl.num_programs(1) - 1)
    def _():
        y = acc_sc[...]
        o_ref[...] = _rms(y, gf_ref[...]) if final_norm else y


def _ffn(x, h, wg, wu, wd, g_final, final_norm):
    S, D = x.shape
    Fd = wg.shape[1]
    tm = min(512, S)
    tf = _pick_tile(Fd, 1408)
    return pl.pallas_call(
        functools.partial(_ffn_kernel, final_norm=final_norm),
        grid=(S // tm, Fd // tf),
        in_specs=[pl.BlockSpec((tm, D), lambda i, f: (i, 0)),
                  pl.BlockSpec((tm, D), lambda i, f: (i, 0)),
                  pl.BlockSpec((D, tf), lambda i, f: (0, f)),
                  pl.BlockSpec((D, tf), lambda i, f: (0, f)),
                  pl.BlockSpec((tf, D), lambda i, f: (f, 0)),
                  pl.BlockSpec((1, D), lambda i, f: (0, 0))],
        out_specs=pl.BlockSpec((tm, D), lambda i, f: (i, 0)),
        out_shape=jax.ShapeDtypeStruct((S, D), F32),
        scratch_shapes=[pltpu.VMEM((tm, D), F32)],
        compiler_params=_params("arbitrary", "arbitrary"),
        name="ffn",
    )(x, h, wg, wu, wd, g_final.reshape(1, D))


def _moe_kernel(x_ref, h_ref, rw_ref, rb_ref, wg_ref, wu_ref, wd_ref, gf_ref, o_ref,
                acc_sc, gate_sc, *, final_norm):
    e = pl.program_id(1)
    f = pl.program_id(2)
    lane = lax.broadcasted_iota(I32, gate_sc.shape, 1)

    @pl.when((e == 0) & (f == 0))
    def _():
        acc_sc[...] = x_ref[...]
        logits = jnp.dot(h_ref[...], rw_ref[...], preferred_element_type=F32) + rb_ref[...]
        logits = jnp.where(lane < N_EXPERTS, logits, -jnp.inf)
        v1 = jnp.max(logits, axis=-1, keepdims=True)
        i1 = jnp.min(jnp.where(logits == v1, lane, LANE), axis=-1, keepdims=True)
        rest = jnp.where(lane == i1, -jnp.inf, logits)
        v2 = jnp.max(rest, axis=-1, keepdims=True)
        i2 = jnp.min(jnp.where(rest == v2, lane, LANE), axis=-1, keepdims=True)
        e2 = jnp.exp(v2 - v1)
        p1 = 1.0 / (1.0 + e2)
        gate_sc[...] = jnp.where(lane == i1, p1, 0.0) + jnp.where(lane == i2, e2 * p1, 0.0)

    gate = jnp.sum(jnp.where(lane == e, gate_sc[...], 0.0), axis=-1, keepdims=True)
    act = _swiglu_tile(h_ref[...], wg_ref[0], wu_ref[0]) * gate
    acc_sc[...] += jnp.dot(act.astype(BF16), wd_ref[0], preferred_element_type=F32)

    @pl.when((e == pl.num_programs(1) - 1) & (f == pl.num_programs(2) - 1))
    def _():
        y = acc_sc[...]
        o_ref[...] = _rms(y, gf_ref[...]) if final_norm else y


def _moe(x, h, rw, rb, wg, wu, wd, g_final, final_norm):
    S, D = x.shape
    E, _, Fe = wg.shape
    tm = min(1024, S)
    tf = _pick_tile(Fe, 896)
    return pl.pallas_call(
        functools.partial(_moe_kernel, final_norm=final_norm),
        grid=(S // tm, E, Fe // tf),
        in_specs=[pl.BlockSpec((tm, D), lambda i, e, f: (i, 0)),
                  pl.BlockSpec((tm, D), lambda i, e, f: (i, 0)),
                  pl.BlockSpec((D, LANE), lambda i, e, f: (0, 0)),
                  pl.BlockSpec((1, LANE), lambda i, e, f: (0, 0)),
                  pl.BlockSpec((1, D, tf), lambda i, e, f: (e, 0, f)),
                  pl.BlockSpec((1, D, tf), lambda i, e, f: (e, 0, f)),
                  pl.BlockSpec((1, tf, D), lambda i, e, f: (e, f, 0)),
                  pl.BlockSpec((1, D), lambda i, e, f: (0, 0))],
        out_specs=pl.BlockSpec((tm, D), lambda i, e, f: (i, 0)),
        out_shape=jax.ShapeDtypeStruct((S, D), F32),
        scratch_shapes=[pltpu.VMEM((tm, D), F32), pltpu.VMEM((tm, LANE), F32)],
        compiler_params=_params("arbitrary", "arbitrary", "arbitrary"),
        name="moe",
    )(x, h, rw, rb, wg, wu, wd, g_final.reshape(1, D))


def _pad_cols(a, n):
    return jnp.pad(a, ((0, 0), (0, n - a.shape[1])))


def _heads(a, n_heads):
    S = a.shape[0]
    return a.reshape(S, n_heads, -1).transpose(1, 0, 2)


def _unheads(a):
    H, S, Dh = a.shape
    return a.transpose(1, 0, 2).reshape(S, H * Dh)


def kernel(x, w_in, b_gate, g_mix, fox_f_bias, dsa_kv_norm, dsa_w_uk, dsa_w_uv, rwkv_mu, rwkv_w0, rwkv_w_up, rwkv_a0, rwkv_a_up, rwkv_g_up, rwkv_k_k, rwkv_k_a, rwkv_r_k, rwkv_ln_g, rwkv_ln_b, vres_down, vres_mu, vres_up, vres_v0, p_fox, p_dsa, p_rwkv, w_out, g_ffn, ffn_w_gate, ffn_w_up, ffn_w_down, router_w, router_b, moe_w_gate, moe_w_up, moe_w_down, g_final):
    B, S, D = x.shape
    assert B == 1 and D == D_MODEL and S % LANE == 0
    depth = w_in.shape[0]
    topk = min(IDX_TOPK, S // 4)
    bf = lambda a: a.astype(BF16)
    xs = x[0]
    v_first = None
    for l in range(depth):
        wl = w_in[l]
        w_fox, w_dsa, w_rwkv, w_gate = (wl[:, :FOX_IN], wl[:, FOX_IN:FOX_IN + DSA_IN],
                                        wl[:, FOX_IN + DSA_IN:FOX_IN + DSA_IN + RWKV_IN],
                                        wl[:, FOX_IN + DSA_IN + RWKV_IN:])
        o1 = DSA_W + DSA_KV_RANK + IDX_W
        w_att = jnp.concatenate([_pad_cols(w_fox, 3 * FOX_W + LANE), w_dsa[:, :o1],
                                 _pad_cols(w_dsa[:, o1:], LANE)], axis=1)
        if l > 0:
            w_rwkv = jnp.concatenate([w_rwkv, _pad_cols(vres_down[l - 1], LANE)], axis=1)
        pa = _rms_proj(xs, g_mix[l], bf(w_att))
        pr = _rms_proj(xs, g_mix[l], bf(w_rwkv))
        gate_p = _rms_proj(xs, g_mix[l], bf(w_gate))

        c0 = 3 * FOX_W
        fl = pa[:, c0:c0 + FOX_HEADS].T.reshape(FOX_HEADS, S // LANE, LANE)
        F = _fox_cumsum(fl, fox_f_bias[l]).reshape(FOX_HEADS, S)
        Fh, Fm, Fl = _split3(F)
        one = jnp.ones((FOX_HEADS, S), BF16)
        zpad = jnp.zeros((FOX_HEADS, FOX_KA - FOX_HEAD_DIM - 6, S), BF16)
        qT = bf(pa[:, 0:FOX_W].T.reshape(FOX_HEADS, FOX_HEAD_DIM, S) * (FOX_HEAD_DIM ** -0.5))
        kT = bf(pa[:, FOX_W:2 * FOX_W].T.reshape(FOX_HEADS, FOX_HEAD_DIM, S))
        stack = lambda *rows: jnp.stack(rows, axis=1)
        qaT = jnp.concatenate([qT, stack(Fh, Fm, Fl, one, one, one), zpad], axis=1)
        kaT = jnp.concatenate([kT, stack(one, one, one, -Fh, -Fm, -Fl), zpad], axis=1)
        fvT = bf(pa[:, 2 * FOX_W:3 * FOX_W].T.reshape(FOX_HEADS, FOX_HEAD_DIM, S))
        o_fox = _fox_attention(qaT, kaT.transpose(0, 2, 1), fvT).reshape(FOX_W, S).T

        t = min(DSA_T, S)
        nb = S // t
        c1 = c0 + LANE
        dqT = bf(pa[:, c1:c1 + DSA_W].reshape(nb, t, DSA_HEADS, DSA_HEAD_DIM).transpose(0, 2, 3, 1))
        c2 = c1 + DSA_W
        ckv = _rmsnorm(pa[:, c2:c2 + DSA_KV_RANK], dsa_kv_norm[l], BF16)
        c3 = c2 + DSA_KV_RANK
        qiT = bf(pa[:, c3:c3 + IDX_W].reshape(nb, t, IDX_HEADS, IDX_HEAD_DIM).transpose(0, 3, 2, 1)
                 .reshape(nb, IDX_HEAD_DIM, IDX_HEADS * t))
        c4 = c3 + IDX_W
        ki = bf(pa[:, c4:c4 + IDX_HEAD_DIM])
        c5 = c4 + IDX_HEAD_DIM
        wi = pa[:, c5:c5 + IDX_HEADS].reshape(nb, t, IDX_HEADS).transpose(0, 2, 1) * (IDX_W ** -0.5)
        o_dsa = _dsa_attention(qiT, wi, dqT, bf(dsa_w_uk[l].transpose(0, 2, 1)),
                               bf(dsa_w_uv[l].transpose(0, 2, 1)), ki, ckv, ckv.T, topk)
        o_dsa = o_dsa.transpose(0, 3, 1, 2).reshape(S, DSA_W)

        zw = jnp.zeros((RWKV_W_LORA, RWKV_W), F32)
        wwa = bf(jnp.concatenate([jnp.concatenate([rwkv_w_up[l], zw], axis=1),
                                  jnp.concatenate([zw, rwkv_a_up[l]], axis=1)], axis=0))
        vres = None
        if l > 0:
            vup = jnp.pad(vres_up[l - 1], ((0, LANE - RWKV_V_LORA), (0, 0)))
            vres = (pr[:, RWKV_IN:], jnp.pad(vres_mu[l - 1], (0, LANE - RWKV_V_LORA)),
                    bf(vup), vres_v0[l - 1], v_first)
        r, k, v, lw, kk, a, g = _rwkv_pre(pr[:, :RWKV_IN], rwkv_mu[l], rwkv_w0[l], rwkv_a0[l], wwa,
                                          bf(rwkv_g_up[l]), rwkv_k_k[l], rwkv_k_a[l], vres)
        if l == 0:
            v_first = v
        hd = lambda t: _heads(t, RWKV_HEADS)
        o_rwkv = _unheads(_rwkv_scan(hd(r), hd(k), hd(v), hd(lw), hd(kk), hd(a), hd(g),
                                     rwkv_r_k[l], rwkv_ln_g[l], rwkv_ln_b[l]))

        xs, h2 = _merge(xs, o_fox, o_dsa, o_rwkv, gate_p, b_gate[l], bf(p_fox[l]), bf(p_dsa[l]),
                        bf(p_rwkv[l]), bf(w_out[l]), g_ffn[l])

        last = l == depth - 1
        if l % 2 == 0:
            xs = _ffn(xs, h2, bf(ffn_w_gate[l // 2]), bf(ffn_w_up[l // 2]), bf(ffn_w_down[l // 2]),
                      g_final, last)
        else:
            rw = bf(_pad_cols(router_w[l // 2], LANE))
            rb = _pad_cols(router_b[l // 2].reshape(1, -1), LANE)
            xs = _moe(xs, h2, rw, rb, bf(moe_w_gate[l // 2]), bf(moe_w_up[l // 2]),
                      bf(moe_w_down[l // 2]), g_final, last)
    return xs[None]
```

```python
import functools

import jax
import jax.numpy as jnp
from jax import lax
from jax.experimental import pallas as pl
from jax.experimental.pallas import tpu as pltpu

F32 = jnp.float32
BF16 = jnp.bfloat16
I32 = jnp.int32

D_MODEL = 1024
CHUNK = 64
RMS_EPS = 1e-6
FOX_HEADS, FOX_HEAD_DIM = 4, 64
DSA_HEADS, DSA_HEAD_DIM, DSA_KV_RANK = 4, 64, 128
IDX_HEADS, IDX_HEAD_DIM, IDX_TOPK = 8, 32, 256
RWKV_HEADS, RWKV_HEAD_DIM = 8, 64
RWKV_W_LORA, RWKV_A_LORA, RWKV_V_LORA, RWKV_G_LORA = 64, 64, 32, 128
RWKV_LN_EPS = 64e-5
FOX_W = FOX_HEADS * FOX_HEAD_DIM
DSA_W = DSA_HEADS * DSA_HEAD_DIM
RWKV_W = RWKV_HEADS * RWKV_HEAD_DIM
IDX_W = IDX_HEADS * IDX_HEAD_DIM
N_EXPERTS = 8
FOX_IN = 3 * FOX_W + FOX_HEADS
DSA_IN = DSA_W + DSA_KV_RANK + IDX_W + IDX_HEAD_DIM + IDX_HEADS
RWKV_IN = 3 * RWKV_W + RWKV_W_LORA + RWKV_A_LORA + RWKV_G_LORA

LANE = 128
VMEM_LIMIT = 52 * 1024 * 1024
NEG = -1e30
INT_MIN = -(2 ** 31)
HI = lax.Precision.HIGHEST


def _params(*sem):
    return pltpu.CompilerParams(dimension_semantics=sem, vmem_limit_bytes=VMEM_LIMIT)


def _pick_tile(n, cap):
    best = LANE
    for t in range(LANE, min(n, cap) + 1, LANE):
        if n % t == 0:
            best = t
    return best


def _softplus(x):
    return jnp.maximum(x, 0.0) + jnp.log1p(jnp.exp(-jnp.abs(x)))


def _sigmoid(x):
    return 1.0 / (1.0 + jnp.exp(-x))


def _rms(x, g):
    return x * lax.rsqrt(jnp.mean(x * x, axis=-1, keepdims=True) + RMS_EPS) * g


def _rms_proj_kernel(x_ref, g_ref, w_ref, o_ref):
    h = _rms(x_ref[...], g_ref[...])
    o_ref[...] = jnp.dot(h.astype(BF16), w_ref[...], preferred_element_type=F32)


def _rms_proj(x, g, w):
    S, D = x.shape
    N = w.shape[1]
    tm = min(512, S)
    tn = _pick_tile(N, 2304)
    return pl.pallas_call(
        _rms_proj_kernel,
        grid=(N // tn, S // tm),
        in_specs=[pl.BlockSpec((tm, D), lambda j, i: (i, 0)),
                  pl.BlockSpec((1, D), lambda j, i: (0, 0)),
                  pl.BlockSpec((D, tn), lambda j, i: (0, j))],
        out_specs=pl.BlockSpec((tm, tn), lambda j, i: (i, j)),
        out_shape=jax.ShapeDtypeStruct((S, N), F32),
        compiler_params=_params("arbitrary", "arbitrary"),
        name="rms_proj",
    )(x, g.reshape(1, D), w)


def _rmsnorm_kernel(x_ref, g_ref, o_ref):
    o_ref[...] = _rms(x_ref[...], g_ref[...]).astype(o_ref.dtype)


def _rmsnorm(x, g, dtype):
    S, D = x.shape
    tm = min(2048, S)
    return pl.pallas_call(
        _rmsnorm_kernel,
        grid=(S // tm,),
        in_specs=[pl.BlockSpec((tm, D), lambda i: (i, 0)),
                  pl.BlockSpec((1, D), lambda i: (0, 0))],
        out_specs=pl.BlockSpec((tm, D), lambda i: (i, 0)),
        out_shape=jax.ShapeDtypeStruct((S, D), dtype),
        compiler_params=_params("arbitrary"),
        name="rmsnorm",
    )(x, g.reshape(1, D))


def _fox_cumsum_kernel(fl_ref, b_ref, o_ref):
    H, R, _ = fl_ref.shape
    upper = (lax.broadcasted_iota(I32, (LANE, LANE), 0)
             <= lax.broadcasted_iota(I32, (LANE, LANE), 1)).astype(F32)
    strict_lower = (lax.broadcasted_iota(I32, (R, R), 0)
                    > lax.broadcasted_iota(I32, (R, R), 1)).astype(F32)
    for h in range(H):
        log_f = -_softplus(-(fl_ref[h] + b_ref[h]))
        within = jnp.dot(log_f, upper, preferred_element_type=F32, precision=HI)
        row_tot = jnp.broadcast_to(within[:, LANE - 1:LANE], (R, LANE))
        before = jnp.dot(strict_lower, row_tot, preferred_element_type=F32, precision=HI)
        o_ref[h] = within + before


def _fox_cumsum(fl, bias):
    H, R, _ = fl.shape
    return pl.pallas_call(
        _fox_cumsum_kernel,
        out_shape=jax.ShapeDtypeStruct((H, R, LANE), F32),
        compiler_params=pltpu.CompilerParams(vmem_limit_bytes=VMEM_LIMIT),
        name="fox_cumsum",
    )(fl, jnp.broadcast_to(bias.reshape(H, 1, 1), (H, 1, LANE)))


FOX_KA = 128


def _fox_kernel(qa_ref, ka_ref, vT_ref, o_ref, *, t, tc, tg):
    i = pl.program_id(1)
    hb = qa_ref.shape[0]
    Dh = vT_ref.shape[1]
    last = (i * t) // tg
    chains = [(h, q0) for h in range(hb) for q0 in range(0, t, tc)]

    def step(g, carry, masked):
        off = pl.multiple_of(g * tg, tg)
        out = []
        logits = [jnp.dot(ka_ref[h, pl.ds(off, tg), :], qa_ref[h, :, q0:q0 + tc],
                          preferred_element_type=F32) for h, q0 in chains]
        for n, (h, q0) in enumerate(chains):
            m_prev, l_prev, acc = carry[n]
            s = logits[n]
            if masked:
                kpos = off + lax.broadcasted_iota(I32, (tg, tc), 0)
                qpos = i * t + q0 + lax.broadcasted_iota(I32, (tg, tc), 1)
                s = jnp.where(kpos <= qpos, s, NEG)
            m_new = jnp.maximum(m_prev, jnp.max(s, axis=0, keepdims=True))
            alpha = jnp.exp(m_prev - m_new)
            p = jnp.exp(s - m_new)
            l_new = alpha * l_prev + jnp.sum(p, axis=0, keepdims=True)
            acc = alpha * acc + jnp.dot(vT_ref[h, :, pl.ds(off, tg)], p.astype(BF16),
                                        preferred_element_type=F32)
            out.append((m_new, l_new, acc))
        return tuple(out)

    init = tuple((jnp.full((1, tc), NEG, F32), jnp.zeros((1, tc), F32), jnp.zeros((Dh, tc), F32))
                 for _ in chains)
    carry = lax.fori_loop(0, last, lambda g, c: step(g, c, False), init)
    for (h, q0), (_, l_fin, acc) in zip(chains, step(last, carry, True)):
        o_ref[h, :, q0:q0 + tc] = (acc / l_fin).astype(o_ref.dtype)


def _fox_attention(qaT, ka, vT):
    H, KA, S = qaT.shape
    Dh = vT.shape[1]
    t = min(512, S)
    tc = min(256, t)
    tg = min(1024, S)
    hb = 2
    return pl.pallas_call(
        functools.partial(_fox_kernel, t=t, tc=tc, tg=tg),
        grid=(H // hb, S // t),
        in_specs=[pl.BlockSpec((hb, KA, t), lambda h, i: (h, 0, i)),
                  pl.BlockSpec((hb, S, KA), lambda h, i: (h, 0, 0)),
                  pl.BlockSpec((hb, Dh, S), lambda h, i: (h, 0, 0))],
        out_specs=pl.BlockSpec((hb, Dh, t), lambda h, i: (h, 0, i)),
        out_shape=jax.ShapeDtypeStruct((H, Dh, S), BF16),
        compiler_params=_params("arbitrary", "arbitrary"),
        name="fox_attention",
    )(qaT, ka, vT)


DSA_T = 128


def _dsa_kernel(qiT_ref, w_ref, qT_ref, wukT_ref, wuvT_ref, ki_ref, c_ref, cT_ref, o_ref,
                keys_sc, qlat_sc, m_sc, l_sc, acc_sc, *, t, tg, topk):
    i = pl.program_id(0)
    last = (i * t) // tg
    ng = last + 1

    qiT = qiT_ref[0]
    w = w_ref[0]

    def score_keys(g, masked):
        off = pl.multiple_of(g * tg, tg)
        rel = jnp.dot(ki_ref[pl.ds(off, tg), :], qiT, preferred_element_type=F32)
        sc = jnp.maximum(rel[:, 0:t], 0.0) * w[0:1, :]
        for h in range(1, IDX_HEADS):
            sc = sc + jnp.maximum(rel[:, h * t:(h + 1) * t], 0.0) * w[h:h + 1, :]
        bits = pltpu.bitcast(sc, I32)
        key = bits ^ ((bits >> 31) & 0x7FFFFFFF)
        key = jnp.where(sc == 0.0, 0, key)
        if masked:
            kchunk = (off + lax.broadcasted_iota(I32, (tg, t), 0)) // CHUNK
            qchunk = (i * t + lax.broadcasted_iota(I32, (tg, t), 1)) // CHUNK
            key = jnp.where(kchunk <= qchunk, key, INT_MIN)
        keys_sc[pl.ds(off, tg), :] = key

    def fill(g, c):
        score_keys(g, False)
        return c

    lax.fori_loop(0, last, fill, 0)
    score_keys(last, True)

    def count_ge(cand):
        cb = jnp.broadcast_to(cand, (t, t))

        def body(g, acc):
            off = pl.multiple_of(g * tg, tg)
            for k in range(tg // t):
                acc = acc + (keys_sc[pl.ds(off + k * t, t), :] >= cb).astype(F32)
            return acc

        acc = lax.fori_loop(0, ng, body, jnp.zeros((t, t), F32))
        return jnp.sum(acc, axis=0, keepdims=True)

    def bisect(s, thr):
        cand = thr + jnp.left_shift(jnp.int32(1), 31 - s)
        return jnp.where(count_ge(cand) >= topk, cand, thr)

    thr = lax.fori_loop(0, 32, bisect, jnp.full((1, t), INT_MIN, I32))
    thr = jnp.maximum(thr, INT_MIN + 1)
    n_ge = count_ge(thr)
    need = topk - count_ge(thr + 1)
    has_ties = jnp.max(jnp.where(n_ge > topk, 1.0, 0.0)) > 0.0
    thr_b = jnp.broadcast_to(thr, (tg, t))

    for h in range(DSA_HEADS):
        ql = jnp.dot(wukT_ref[h], qT_ref[0, h], preferred_element_type=F32)
        qlat_sc[:, h * t:(h + 1) * t] = (ql * (DSA_HEAD_DIM ** -0.5)).astype(BF16)
    m_sc[...] = jnp.full(m_sc.shape, NEG, F32)
    l_sc[...] = jnp.zeros(l_sc.shape, F32)
    acc_sc[...] = jnp.zeros(acc_sc.shape, F32)

    def attend(g, seen_eq, with_ties):
        off = pl.multiple_of(g * tg, tg)
        key = keys_sc[pl.ds(off, tg), :]
        if with_ties:
            eq = key == thr_b
            eqf = eq.astype(F32)
            strict_lower = (lax.broadcasted_iota(I32, (tg, tg), 0)
                            > lax.broadcasted_iota(I32, (tg, tg), 1)).astype(BF16)
            rank = jnp.dot(strict_lower, eqf.astype(BF16), preferred_element_type=F32) + seen_eq
            sel = (key > thr_b) | (eq & (rank < need))
            seen_eq = seen_eq + jnp.sum(eqf, axis=0, keepdims=True)
        else:
            sel = key >= thr_b
        bias = jnp.where(sel, 0.0, NEG)
        lg = jnp.dot(c_ref[pl.ds(off, tg), :], qlat_sc[...], preferred_element_type=F32)
        cT_tile = cT_ref[:, pl.ds(off, tg)]
        for h in range(DSA_HEADS):
            s = lg[:, h * t:(h + 1) * t] + bias
            m_prev = m_sc[h]
            m_new = jnp.maximum(m_prev, jnp.max(s, axis=0, keepdims=True))
            alpha = jnp.exp(m_prev - m_new)
            p = jnp.exp(s - m_new)
            l_sc[h] = alpha * l_sc[h] + jnp.sum(p, axis=0, keepdims=True)
            acc_sc[h] = alpha * acc_sc[h] + jnp.dot(cT_tile, p.astype(BF16),
                                                    preferred_element_type=F32)
            m_sc[h] = m_new
        return seen_eq

    @pl.when(has_ties)
    def _():
        lax.fori_loop(0, ng, lambda g, s: attend(g, s, True), jnp.zeros((1, t), F32))

    @pl.when(jnp.logical_not(has_ties))
    def _():
        lax.fori_loop(0, ng, lambda g, s: attend(g, s, False), jnp.zeros((1, t), F32))

    for h in range(DSA_HEADS):
        o_lat = (acc_sc[h] / l_sc[h]).astype(BF16)
        o_ref[0, h] = jnp.dot(wuvT_ref[h], o_lat, preferred_element_type=F32).astype(o_ref.dtype)


def _dsa_attention(qiT, w, qT, wukT, wuvT, ki, c, cT, topk):
    NB, H, Dh, t = qT.shape
    S = NB * t
    R = DSA_KV_RANK
    tg = min(512, S)
    const2 = lambda i: (0, 0)
    const3 = lambda i: (0, 0, 0)
    return pl.pallas_call(
        functools.partial(_dsa_kernel, t=t, tg=tg, topk=topk),
        grid=(NB,),
        in_specs=[pl.BlockSpec((1, IDX_HEAD_DIM, IDX_HEADS * t), lambda i: (i, 0, 0)),
                  pl.BlockSpec((1, IDX_HEADS, t), lambda i: (i, 0, 0)),
                  pl.BlockSpec((1, H, Dh, t), lambda i: (i, 0, 0, 0)),
                  pl.BlockSpec((H, R, Dh), const3),
                  pl.BlockSpec((H, Dh, R), const3),
                  pl.BlockSpec((S, IDX_HEAD_DIM), const2),
                  pl.BlockSpec((S, R), const2),
                  pl.BlockSpec((R, S), const2)],
        out_specs=pl.BlockSpec((1, H, Dh, t), lambda i: (i, 0, 0, 0)),
        out_shape=jax.ShapeDtypeStruct((NB, H, Dh, t), BF16),
        scratch_shapes=[pltpu.VMEM((S, t), I32), pltpu.VMEM((R, H * t), BF16),
                        pltpu.VMEM((H, 1, t), F32), pltpu.VMEM((H, 1, t), F32),
                        pltpu.VMEM((H, R, t), F32)],
        compiler_params=_params("arbitrary"),
        name="dsa_attention",
    )(qiT, w, qT, wukT, wuvT, ki, c, cT)


def _shift_mix(p, prev_tail, mu, first_block):
    rows = lax.broadcasted_iota(I32, p.shape, 0)
    tail = jnp.where(first_block, 0.0, prev_tail)
    prev = jnp.where(rows == 0, tail, pltpu.roll(p, 1, axis=0))
    return p + (prev - p) * mu


def _split3(f):
    hi = f.astype(BF16)
    r1 = f - hi.astype(F32)
    mid = r1.astype(BF16)
    lo = (r1 - mid.astype(F32)).astype(BF16)
    return hi, mid, lo


def _dot3(x, w01):
    return sum(jnp.dot(part, w01, preferred_element_type=F32) for part in _split3(x))


def _dot3_left(w01, x):
    return sum(jnp.dot(w01, part, preferred_element_type=F32) for part in _split3(x))


def _rwkv_pre_kernel(*refs, with_vres):
    (p_ref, pp_ref, mu_ref, w0_ref, a0_ref, wwa_ref, gup_ref, kk_ref, ka_ref, rk_ref,
     tri_ref, blk_ref, bd_ref) = refs[:13]
    if with_vres:
        hv_ref, hvp_ref, vmu_ref, vup_ref, vv0_ref, vf_ref = refs[13:19]
    (at_o, rt_o, bt_o, kt_o, bh_o, kh_o, vb_o, pc_o, bonus_o, g_o, v_o) = refs[-11:]
    first = pl.program_id(0) == 0
    W = RWKV_W
    ps = _shift_mix(p_ref[...], pp_ref[7:8, :], mu_ref[...], first)
    r, k, v = ps[:, 0:W], ps[:, W:2 * W], ps[:, 2 * W:3 * W]
    wa = ps[:, 3 * W:3 * W + LANE]
    gl = ps[:, 3 * W + LANE:3 * W + 2 * LANE]
    lane = lax.broadcasted_iota(I32, wa.shape, 1)
    wa = jnp.where(lane < RWKV_W_LORA, jnp.tanh(wa), wa)
    up = jnp.dot(wa.astype(BF16), wwa_ref[...], preferred_element_type=F32)
    log_w = -_softplus(-(w0_ref[...] + up[:, 0:W])) - 0.5
    a = _sigmoid(a0_ref[...] + up[:, W:2 * W])
    g = jnp.dot(_sigmoid(gl).astype(BF16), gup_ref[...], preferred_element_type=F32)
    if with_vres:
        vl = _shift_mix(hv_ref[...], hvp_ref[7:8, :], vmu_ref[...], first)
        logit = vv0_ref[...] + jnp.dot(vl.astype(BF16), vup_ref[...], preferred_element_type=F32)
        v = v + (vf_ref[...] - v) * _sigmoid(logit)
    k2 = k * (1.0 + (a - 1.0) * ka_ref[...])
    kkr = k * kk_ref[...]
    bd = bd_ref[...]
    kk = kkr * lax.rsqrt(_dot3(kkr * kkr, bd) + 1e-12)
    lw = -jnp.exp(log_w)
    cum = _dot3_left(tri_ref[...], lw)
    cend = _dot3_left(blk_ref[...], lw)
    p_inv = jnp.exp(-cum)
    p_end = jnp.exp(cend - cum)
    beta = kk * a
    at_o[...] = (-kk * jnp.exp(cum - lw)).astype(BF16)
    rt_o[...] = (r * jnp.exp(cum)).astype(BF16)
    bt_o[...] = (beta * p_inv).astype(BF16)
    kt_o[...] = (k2 * p_inv).astype(BF16)
    bh_o[...] = (beta * p_end).astype(BF16)
    kh_o[...] = (k2 * p_end).astype(BF16)
    vb_o[...] = v.astype(BF16)
    pc_o[...] = jnp.exp(cend)
    bonus_o[...] = _dot3(r * k2 * rk_ref[...], bd) * v
    g_o[...] = g
    v_o[...] = v


def _block_ones(n, block, lower_tri=False):
    i = jnp.arange(n)
    m = (i[:, None] // block) == (i[None, :] // block)
    if lower_tri:
        m = m & (i[:, None] >= i[None, :])
    return m.astype(BF16)


def _rwkv_pre(p, mu, w0, a0, wwa, gup, k_k, k_a, r_k, vres=None):
    S, PW = p.shape
    W = RWKV_W
    tm = min(512, S)
    row = lambda i: (i, 0)
    const = lambda i: (0, 0)
    tail = lambda i: (jnp.maximum(i * (tm // 8) - 1, 0), 0)
    vec = lambda a: a.reshape(1, -1)
    args = [p, p, vec(mu), vec(w0), vec(a0), wwa, gup, vec(k_k), vec(k_a), vec(r_k),
            _block_ones(tm, CHUNK, lower_tri=True), _block_ones(tm, CHUNK),
            _block_ones(W, RWKV_HEAD_DIM)]
    specs = [pl.BlockSpec((tm, PW), row), pl.BlockSpec((8, PW), tail),
             pl.BlockSpec((1, PW), const), pl.BlockSpec((1, W), const), pl.BlockSpec((1, W), const),
             pl.BlockSpec(wwa.shape, const), pl.BlockSpec(gup.shape, const),
             pl.BlockSpec((1, W), const), pl.BlockSpec((1, W), const), pl.BlockSpec((1, W), const),
             pl.BlockSpec((tm, tm), const), pl.BlockSpec((tm, tm), const),
             pl.BlockSpec((W, W), const)]
    if vres is not None:
        hv, vmu, vup, vv0, v_first = vres
        args += [hv, hv, vec(vmu), vup, vec(vv0), v_first]
        specs += [pl.BlockSpec((tm, LANE), row), pl.BlockSpec((8, LANE), tail),
                  pl.BlockSpec((1, LANE), const), pl.BlockSpec(vup.shape, const),
                  pl.BlockSpec((1, W), const), pl.BlockSpec((tm, W), row)]
    return pl.pallas_call(
        functools.partial(_rwkv_pre_kernel, with_vres=vres is not None),
        grid=(S // tm,),
        in_specs=specs,
        out_specs=[pl.BlockSpec((tm, W), row)] * 11,
        out_shape=[jax.ShapeDtypeStruct((S, W), BF16)] * 7 + [jax.ShapeDtypeStruct((S, W), F32)] * 4,
        compiler_params=_params("arbitrary"),
        name="rwkv_pre",
    )(*args)


def _rwkv_scan_kernel(at_ref, rt_ref, bt_ref, kt_ref, bh_ref, kh_ref, v_ref, pc_ref, y_ref, h_sc,
                      *, chunks):
    C = CHUNK
    N = RWKV_HEAD_DIM

    @pl.when(pl.program_id(0) == 0)
    def _():
        h_sc[...] = jnp.zeros(h_sc.shape, F32)

    ti = lax.broadcasted_iota(I32, (C, C), 0)
    tj = lax.broadcasted_iota(I32, (C, C), 1)
    lower_incl = ti >= tj
    lower_strict = ti > tj
    eye = (ti == tj).astype(F32)
    eye_n = (lax.broadcasted_iota(I32, (N, N), 0) == lax.broadcasted_iota(I32, (N, N), 1)).astype(F32)

    def mm(x, y):
        return jnp.dot(x.astype(BF16), y.astype(BF16), preferred_element_type=F32)

    def mm_nt(x, y):
        return lax.dot_general(x.astype(BF16), y.astype(BF16), (((1,), (1,)), ((), ())),
                               preferred_element_type=F32)

    def mm_tn(x, y):
        return lax.dot_general(x.astype(BF16), y.astype(BF16), (((0,), (0,)), ((), ())),
                               preferred_element_type=F32)

    units = [(c, h) for c in range(chunks) for h in range(RWKV_HEADS)]
    tile = lambda ref, u: ref[u[0] * C:(u[0] + 1) * C, u[1] * N:(u[1] + 1) * N]
    each = lambda fn: {u: fn(u) for u in units}

    At, Rt, Bt, Kt = (each(lambda u, r=ref: tile(r, u)) for ref in (at_ref, rt_ref, bt_ref, kt_ref))
    Bh, Kh, V = (each(lambda u, r=ref: tile(r, u)) for ref in (bh_ref, kh_ref, v_ref))
    AR = each(lambda u: jnp.concatenate([At[u], Rt[u]], axis=0))
    Mb = each(lambda u: mm_nt(AR[u], Bt[u]))
    Mk = each(lambda u: mm_nt(AR[u], Kt[u]))
    Lab = each(lambda u: jnp.where(lower_strict, Mb[u][0:C], 0.0))
    Mrb = each(lambda u: jnp.where(lower_incl, Mb[u][C:2 * C], 0.0))
    Lak = each(lambda u: jnp.where(lower_strict, Mk[u][0:C], 0.0))
    Mrk = each(lambda u: jnp.where(lower_incl, Mk[u][C:2 * C], 0.0))
    T = each(lambda u: eye + Lab[u])
    Lp = Lab
    span = 2
    while span < C:
        Lp = each(lambda u, Lp=Lp: mm(Lp[u], Lp[u]))
        T = each(lambda u, T=T, Lp=Lp: T[u] + mm(Lp[u], T[u]))
        span *= 2
    W1 = each(lambda u: mm(Lak[u], V[u]))
    A2 = each(lambda u: mm(T[u], At[u]))
    U0 = each(lambda u: mm(T[u], W1[u]))
    R2 = each(lambda u: Rt[u].astype(F32) + mm(Mrb[u], A2[u]))
    Y0 = each(lambda u: mm(Mrb[u], U0[u]) + mm(Mrk[u], V[u]))
    G = each(lambda u: eye_n * tile(pc_ref, u)[0:1, :] + mm_tn(Bh[u], A2[u]))
    H0 = each(lambda u: mm_tn(Bh[u], U0[u]) + mm_tn(Kh[u], V[u]))

    H = {h: h_sc[h] for h in range(RWKV_HEADS)}
    for c in range(chunks):
        ys = []
        for h in range(RWKV_HEADS):
            u = (c, h)
            ys.append(mm(R2[u], H[h]) + Y0[u])
            H[h] = mm(G[u], H[h]) + H0[u]
        y_ref[c * C:(c + 1) * C, :] = jnp.concatenate(ys, axis=1)
    for h in range(RWKV_HEADS):
        h_sc[h] = H[h]


def _rwkv_scan(at, rt, bt, kt, bh, kh, vb, pc):
    S, W = at.shape
    chunks = 2 if S % (2 * CHUNK) == 0 else 1
    tb = chunks * CHUNK
    seq = pl.BlockSpec((tb, W), lambda i: (i, 0))
    return pl.pallas_call(
        functools.partial(_rwkv_scan_kernel, chunks=chunks),
        grid=(S // tb,),
        in_specs=[seq] * 8,
        out_specs=seq,
        out_shape=jax.ShapeDtypeStruct((S, W), F32),
        scratch_shapes=[pltpu.VMEM((RWKV_HEADS, RWKV_HEAD_DIM, RWKV_HEAD_DIM), F32)],
        compiler_params=_params("arbitrary"),
        name="rwkv_scan",
    )(at, rt, bt, kt, bh, kh, vb, pc)


def _merge_kernel(x_ref, of_ref, od_ref, y_ref, bonus_ref, g_ref, lng_ref, lnb_ref, bd_ref,
                  gp_ref, bg_ref, pf_ref, pd_ref, pr_ref, wo_ref, gn_ref, xo_ref, h_ref):
    D = D_MODEL
    bd = bd_ref[...]
    inv_n = 1.0 / RWKV_HEAD_DIM
    y = y_ref[...]
    yc = y - _dot3(y, bd) * inv_n
    var = _dot3(yc * yc, bd) * inv_n
    yn = yc * lax.rsqrt(var + RWKV_LN_EPS) * lng_ref[...] + lnb_ref[...]
    o_rwkv = ((yn + bonus_ref[...]) * g_ref[...]).astype(BF16)
    gates = _sigmoid(gp_ref[...] + bg_ref[...])
    merged = (gates[:, 0:D] * jnp.dot(of_ref[...], pf_ref[...], preferred_element_type=F32)
              + gates[:, D:2 * D] * jnp.dot(od_ref[...], pd_ref[...], preferred_element_type=F32)
              + gates[:, 2 * D:3 * D] * jnp.dot(o_rwkv, pr_ref[...], preferred_element_type=F32))
    x = x_ref[...] + jnp.dot(merged.astype(BF16), wo_ref[...], preferred_element_type=F32)
    xo_ref[...] = x
    h_ref[...] = _rms(x, gn_ref[...]).astype(h_ref.dtype)


def _merge(x, o_fox, o_dsa, y_rwkv, bonus, g_rwkv, ln_g, ln_b, gate_p, b_gate, p_fox, p_dsa, p_rwkv,
           w_out, g_ffn):
    S, D = x.shape
    W = RWKV_W
    tm = min(512, S)
    row = lambda i: (i, 0)
    const = lambda i: (0, 0)
    return pl.pallas_call(
        _merge_kernel,
        grid=(S // tm,),
        in_specs=[pl.BlockSpec((tm, D), row), pl.BlockSpec((tm, FOX_W), row),
                  pl.BlockSpec((tm, DSA_W), row), pl.BlockSpec((tm, W), row),
                  pl.BlockSpec((tm, W), row), pl.BlockSpec((tm, W), row),
                  pl.BlockSpec((1, W), const), pl.BlockSpec((1, W), const),
                  pl.BlockSpec((W, W), const),
                  pl.BlockSpec((tm, 3 * D), row), pl.BlockSpec((1, 3 * D), const),
                  pl.BlockSpec((FOX_W, D), const), pl.BlockSpec((DSA_W, D), const),
                  pl.BlockSpec((W, D), const), pl.BlockSpec((D, D), const),
                  pl.BlockSpec((1, D), const)],
        out_specs=[pl.BlockSpec((tm, D), row), pl.BlockSpec((tm, D), row)],
        out_shape=[jax.ShapeDtypeStruct((S, D), F32), jax.ShapeDtypeStruct((S, D), BF16)],
        compiler_params=_params("arbitrary"),
        name="merge",
    )(x, o_fox, o_dsa, y_rwkv, bonus, g_rwkv, ln_g.reshape(1, W), ln_b.reshape(1, W),
      _block_ones(W, RWKV_HEAD_DIM), gate_p, b_gate.reshape(1, -1), p_fox, p_dsa, p_rwkv, w_out,
      g_ffn.reshape(1, D))


def _swiglu_tile(h, wg, wu):
    gate = jnp.dot(h, wg, preferred_element_type=F32)
    up = jnp.dot(h, wu, preferred_element_type=F32)
    return gate * _sigmoid(gate) * up


def _ffn_kernel(x_ref, h_ref, wg_ref, wu_ref, wd_ref, gf_ref, o_ref, acc_sc, *, final_norm):
    f = pl.program_id(1)

    @pl.when(f == 0)
    def _():
        acc_sc[...] = x_ref[...]

    act = _swiglu_tile(h_ref[...], wg_ref[...], wu_ref[...])
    acc_sc[...] += jnp.dot(act.astype(BF16), wd_ref[...], preferred_element_type=F32)

    @pl.when(f == pl.num_programs(1) - 1)
    def _():
        y = acc_sc[...]
        o_ref[...] = _rms(y, gf_ref[...]) if final_norm else y


def _ffn(x, h, wg, wu, wd, g_final, final_norm):
    S, D = x.shape
    Fd = wg.shape[1]
    tm = min(512, S)
    tf = _pick_tile(Fd, 1408)
    return pl.pallas_call(
        functools.partial(_ffn_kernel, final_norm=final_norm),
        grid=(S // tm, Fd // tf),
        in_specs=[pl.BlockSpec((tm, D), lambda i, f: (i, 0)),
                  pl.BlockSpec((tm, D), lambda i, f: (i, 0)),
                  pl.BlockSpec((D, tf), lambda i, f: (0, f)),
                  pl.BlockSpec((D, tf), lambda i, f: (0, f)),
                  pl.BlockSpec((tf, D), lambda i, f: (f, 0)),
                  pl.BlockSpec((1, D), lambda i, f: (0, 0))],
        out_specs=pl.BlockSpec((tm, D), lambda i, f: (i, 0)),
        out_shape=jax.ShapeDtypeStruct((S, D), F32),
        scratch_shapes=[pltpu.VMEM((tm, D), F32)],
        compiler_params=_params("arbitrary", "arbitrary"),
        name="ffn",
    )(x, h, wg, wu, wd, g_final.reshape(1, D))


def _moe_kernel(x_ref, h_ref, rw_ref, rb_ref, wg_ref, wu_ref, wd_ref, gf_ref, o_ref,
                acc_sc, gate_sc, *, final_norm):
    e = pl.program_id(1)
    f = pl.program_id(2)
    lane = lax.broadcasted_iota(I32, gate_sc.shape, 1)

    @pl.when((e == 0) & (f == 0))
    def _():
        acc_sc[...] = x_ref[...]
        logits = jnp.dot(h_ref[...], rw_ref[...], preferred_element_type=F32) + rb_ref[...]
        logits = jnp.where(lane < N_EXPERTS, logits, -jnp.inf)
        v1 = jnp.max(logits, axis=-1, keepdims=True)
        i1 = jnp.min(jnp.where(logits == v1, lane, LANE), axis=-1, keepdims=True)
        rest = jnp.where(lane == i1, -jnp.inf, logits)
        v2 = jnp.max(rest, axis=-1, keepdims=True)
        i2 = jnp.min(jnp.where(rest == v2, lane, LANE), axis=-1, keepdims=True)
        e2 = jnp.exp(v2 - v1)
        p1 = 1.0 / (1.0 + e2)
        gate_sc[...] = jnp.where(lane == i1, p1, 0.0) + jnp.where(lane == i2, e2 * p1, 0.0)

    gate = jnp.sum(jnp.where(lane == e, gate_sc[...], 0.0), axis=-1, keepdims=True)
    act = _swiglu_tile(h_ref[...], wg_ref[0], wu_ref[0]) * gate
    acc_sc[...] += jnp.dot(act.astype(BF16), wd_ref[0], preferred_element_type=F32)

    @pl.when((e == pl.num_programs(1) - 1) & (f == pl.num_programs(2) - 1))
    def _():
        y = acc_sc[...]
        o_ref[...] = _rms(y, gf_ref[...]) if final_norm else y


def _moe(x, h, rw, rb, wg, wu, wd, g_final, final_norm):
    S, D = x.shape
    E, _, Fe = wg.shape
    tm = min(1024, S)
    tf = _pick_tile(Fe, 896)
    return pl.pallas_call(
        functools.partial(_moe_kernel, final_norm=final_norm),
        grid=(S // tm, E, Fe // tf),
        in_specs=[pl.BlockSpec((tm, D), lambda i, e, f: (i, 0)),
                  pl.BlockSpec((tm, D), lambda i, e, f: (i, 0)),
                  pl.BlockSpec((D, LANE), lambda i, e, f: (0, 0)),
                  pl.BlockSpec((1, LANE), lambda i, e, f: (0, 0)),
                  pl.BlockSpec((1, D, tf), lambda i, e, f: (e, 0, f)),
                  pl.BlockSpec((1, D, tf), lambda i, e, f: (e, 0, f)),
                  pl.BlockSpec((1, tf, D), lambda i, e, f: (e, f, 0)),
                  pl.BlockSpec((1, D), lambda i, e, f: (0, 0))],
        out_specs=pl.BlockSpec((tm, D), lambda i, e, f: (i, 0)),
        out_shape=jax.ShapeDtypeStruct((S, D), F32),
        scratch_shapes=[pltpu.VMEM((tm, D), F32), pltpu.VMEM((tm, LANE), F32)],
        compiler_params=_params("arbitrary", "arbitrary", "arbitrary"),
        name="moe",
    )(x, h, rw, rb, wg, wu, wd, g_final.reshape(1, D))


def _pad_cols(a, n):
    return jnp.pad(a, ((0, 0), (0, n - a.shape[1])))


def _heads(a, n_heads):
    S = a.shape[0]
    return a.reshape(S, n_heads, -1).transpose(1, 0, 2)


def _unheads(a):
    H, S, Dh = a.shape
    return a.transpose(1, 0, 2).reshape(S, H * Dh)


def kernel(x, w_in, b_gate, g_mix, fox_f_bias, dsa_kv_norm, dsa_w_uk, dsa_w_uv, rwkv_mu, rwkv_w0, rwkv_w_up, rwkv_a0, rwkv_a_up, rwkv_g_up, rwkv_k_k, rwkv_k_a, rwkv_r_k, rwkv_ln_g, rwkv_ln_b, vres_down, vres_mu, vres_up, vres_v0, p_fox, p_dsa, p_rwkv, w_out, g_ffn, ffn_w_gate, ffn_w_up, ffn_w_down, router_w, router_b, moe_w_gate, moe_w_up, moe_w_down, g_final):
    B, S, D = x.shape
    assert B == 1 and D == D_MODEL and S % LANE == 0
    depth = w_in.shape[0]
    topk = min(IDX_TOPK, S // 4)
    bf = lambda a: a.astype(BF16)
    xs = x[0]
    v_first = None
    for l in range(depth):
        wl = w_in[l]
        w_fox, w_dsa, w_rwkv, w_gate = (wl[:, :FOX_IN], wl[:, FOX_IN:FOX_IN + DSA_IN],
                                        wl[:, FOX_IN + DSA_IN:FOX_IN + DSA_IN + RWKV_IN],
                                        wl[:, FOX_IN + DSA_IN + RWKV_IN:])
        o1 = DSA_W + DSA_KV_RANK + IDX_W
        w_att = jnp.concatenate([_pad_cols(w_fox, 3 * FOX_W + LANE), w_dsa[:, :o1],
                                 _pad_cols(w_dsa[:, o1:], LANE)], axis=1)
        if l > 0:
            w_rwkv = jnp.concatenate([w_rwkv, _pad_cols(vres_down[l - 1], LANE)], axis=1)
        pa = _rms_proj(xs, g_mix[l], bf(w_att))
        pr = _rms_proj(xs, g_mix[l], bf(w_rwkv))
        gate_p = _rms_proj(xs, g_mix[l], bf(w_gate))

        c0 = 3 * FOX_W
        fl = pa[:, c0:c0 + FOX_HEADS].T.reshape(FOX_HEADS, S // LANE, LANE)
        F = _fox_cumsum(fl, fox_f_bias[l]).reshape(FOX_HEADS, S)
        Fh, Fm, Fl = _split3(F)
        one = jnp.ones((FOX_HEADS, S), BF16)
        zpad = jnp.zeros((FOX_HEADS, FOX_KA - FOX_HEAD_DIM - 6, S), BF16)
        qT = bf(pa[:, 0:FOX_W].T.reshape(FOX_HEADS, FOX_HEAD_DIM, S) * (FOX_HEAD_DIM ** -0.5))
        kT = bf(pa[:, FOX_W:2 * FOX_W].T.reshape(FOX_HEADS, FOX_HEAD_DIM, S))
        stack = lambda *rows: jnp.stack(rows, axis=1)
        qaT = jnp.concatenate([qT, stack(Fh, Fm, Fl, one, one, one), zpad], axis=1)
        kaT = jnp.concatenate([kT, stack(one, one, one, -Fh, -Fm, -Fl), zpad], axis=1)
        fvT = bf(pa[:, 2 * FOX_W:3 * FOX_W].T.reshape(FOX_HEADS, FOX_HEAD_DIM, S))
        o_fox = _fox_attention(qaT, kaT.transpose(0, 2, 1), fvT).reshape(FOX_W, S).T

        t = min(DSA_T, S)
        nb = S // t
        c1 = c0 + LANE
        dqT = bf(pa[:, c1:c1 + DSA_W].reshape(nb, t, DSA_HEADS, DSA_HEAD_DIM).transpose(0, 2, 3, 1))
        c2 = c1 + DSA_W
        ckv = _rmsnorm(pa[:, c2:c2 + DSA_KV_RANK], dsa_kv_norm[l], BF16)
        c3 = c2 + DSA_KV_RANK
        qiT = bf(pa[:, c3:c3 + IDX_W].reshape(nb, t, IDX_HEADS, IDX_HEAD_DIM).transpose(0, 3, 2, 1)
                 .reshape(nb, IDX_HEAD_DIM, IDX_HEADS * t))
        c4 = c3 + IDX_W
        ki = bf(pa[:, c4:c4 + IDX_HEAD_DIM])
        c5 = c4 + IDX_HEAD_DIM
        wi = pa[:, c5:c5 + IDX_HEADS].reshape(nb, t, IDX_HEADS).transpose(0, 2, 1) * (IDX_W ** -0.5)
        o_dsa = _dsa_attention(qiT, wi, dqT, bf(dsa_w_uk[l].transpose(0, 2, 1)),
                               bf(dsa_w_uv[l].transpose(0, 2, 1)), ki, ckv, ckv.T, topk)
        o_dsa = o_dsa.transpose(0, 3, 1, 2).reshape(S, DSA_W)

        zw = jnp.zeros((RWKV_W_LORA, RWKV_W), F32)
        wwa = bf(jnp.concatenate([jnp.concatenate([rwkv_w_up[l], zw], axis=1),
                                  jnp.concatenate([zw, rwkv_a_up[l]], axis=1)], axis=0))
        vres = None
        if l > 0:
            vup = jnp.pad(vres_up[l - 1], ((0, LANE - RWKV_V_LORA), (0, 0)))
            vres = (pr[:, RWKV_IN:], jnp.pad(vres_mu[l - 1], (0, LANE - RWKV_V_LORA)),
                    bf(vup), vres_v0[l - 1], v_first)
        *scan_ops, bonus, g_rwkv, v = _rwkv_pre(
            pr[:, :RWKV_IN], rwkv_mu[l], rwkv_w0[l], rwkv_a0[l], wwa, bf(rwkv_g_up[l]),
            rwkv_k_k[l], rwkv_k_a[l], rwkv_r_k[l], vres)
        if l == 0:
            v_first = v
        y_rwkv = _rwkv_scan(*scan_ops)

        xs, h2 = _merge(xs, o_fox, o_dsa, y_rwkv, bonus, g_rwkv, rwkv_ln_g[l], rwkv_ln_b[l],
                        gate_p, b_gate[l], bf(p_fox[l]), bf(p_dsa[l]), bf(p_rwkv[l]), bf(w_out[l]),
                        g_ffn[l])

        last = l == depth - 1
        if l % 2 == 0:
            xs = _ffn(xs, h2, bf(ffn_w_gate[l // 2]), bf(ffn_w_up[l // 2]), bf(ffn_w_down[l // 2]),
                      g_final, last)
        else:
            rw = bf(_pad_cols(router_w[l // 2], LANE))
            rb = _pad_cols(router_b[l // 2].reshape(1, -1), LANE)
            xs = _moe(xs, h2, rw, rb, bf(moe_w_gate[l // 2]), bf(moe_w_up[l // 2]),
                      bf(moe_w_down[l // 2]), g_final, last)
    return xs[None]
```

```python
import functools

import jax
import jax.numpy as jnp
from jax import lax
from jax.experimental import pallas as pl
from jax.experimental.pallas import tpu as pltpu

F32 = jnp.float32
BF16 = jnp.bfloat16
I32 = jnp.int32

D_MODEL = 1024
CHUNK = 64
RMS_EPS = 1e-6
FOX_HEADS, FOX_HEAD_DIM = 4, 64
DSA_HEADS, DSA_HEAD_DIM, DSA_KV_RANK = 4, 64, 128
IDX_HEADS, IDX_HEAD_DIM, IDX_TOPK = 8, 32, 256
RWKV_HEADS, RWKV_HEAD_DIM = 8, 64
RWKV_W_LORA, RWKV_A_LORA, RWKV_V_LORA, RWKV_G_LORA = 64, 64, 32, 128
RWKV_LN_EPS = 64e-5
FOX_W = FOX_HEADS * FOX_HEAD_DIM
DSA_W = DSA_HEADS * DSA_HEAD_DIM
RWKV_W = RWKV_HEADS * RWKV_HEAD_DIM
IDX_W = IDX_HEADS * IDX_HEAD_DIM
N_EXPERTS = 8
FOX_IN = 3 * FOX_W + FOX_HEADS
DSA_IN = DSA_W + DSA_KV_RANK + IDX_W + IDX_HEAD_DIM + IDX_HEADS
RWKV_IN = 3 * RWKV_W + RWKV_W_LORA + RWKV_A_LORA + RWKV_G_LORA

LANE = 128
VMEM_LIMIT = 52 * 1024 * 1024
NEG = -1e30
HI = lax.Precision.HIGHEST


def _params(*sem):
    return pltpu.CompilerParams(dimension_semantics=sem, vmem_limit_bytes=VMEM_LIMIT)


def _pick_tile(n, cap):
    best = LANE
    for t in range(LANE, min(n, cap) + 1, LANE):
        if n % t == 0:
            best = t
    return best


def _softplus(x):
    return jnp.maximum(x, 0.0) + jnp.log1p(jnp.exp(-jnp.abs(x)))


def _sigmoid(x):
    return 1.0 / (1.0 + jnp.exp(-x))


def _rms(x, g):
    return x * lax.rsqrt(jnp.mean(x * x, axis=-1, keepdims=True) + RMS_EPS) * g


def _rms_proj_kernel(x_ref, g_ref, w_ref, o_ref):
    h = _rms(x_ref[...], g_ref[...])
    o_ref[...] = jnp.dot(h.astype(BF16), w_ref[...], preferred_element_type=F32)


def _rms_proj(x, g, w):
    S, D = x.shape
    N = w.shape[1]
    tm = min(512, S)
    tn = _pick_tile(N, 2304)
    return pl.pallas_call(
        _rms_proj_kernel,
        grid=(N // tn, S // tm),
        in_specs=[pl.BlockSpec((tm, D), lambda j, i: (i, 0)),
                  pl.BlockSpec((1, D), lambda j, i: (0, 0)),
                  pl.BlockSpec((D, tn), lambda j, i: (0, j))],
        out_specs=pl.BlockSpec((tm, tn), lambda j, i: (i, j)),
        out_shape=jax.ShapeDtypeStruct((S, N), F32),
        compiler_params=_params("arbitrary", "arbitrary"),
        name="rms_proj",
    )(x, g.reshape(1, D), w)


def _rmsnorm_kernel(x_ref, g_ref, o_ref):
    o_ref[...] = _rms(x_ref[...], g_ref[...]).astype(o_ref.dtype)


def _rmsnorm(x, g, dtype):
    S, D = x.shape
    tm = min(2048, S)
    return pl.pallas_call(
        _rmsnorm_kernel,
        grid=(S // tm,),
        in_specs=[pl.BlockSpec((tm, D), lambda i: (i, 0)),
                  pl.BlockSpec((1, D), lambda i: (0, 0))],
        out_specs=pl.BlockSpec((tm, D), lambda i: (i, 0)),
        out_shape=jax.ShapeDtypeStruct((S, D), dtype),
        compiler_params=_params("arbitrary"),
        name="rmsnorm",
    )(x, g.reshape(1, D))


def _fox_cumsum_kernel(fl_ref, b_ref, o_ref):
    H, R, _ = fl_ref.shape
    upper = (lax.broadcasted_iota(I32, (LANE, LANE), 0)
             <= lax.broadcasted_iota(I32, (LANE, LANE), 1)).astype(F32)
    strict_lower = (lax.broadcasted_iota(I32, (R, R), 0)
                    > lax.broadcasted_iota(I32, (R, R), 1)).astype(F32)
    for h in range(H):
        log_f = -_softplus(-(fl_ref[h] + b_ref[h]))
        within = jnp.dot(log_f, upper, preferred_element_type=F32, precision=HI)
        row_tot = jnp.broadcast_to(within[:, LANE - 1:LANE], (R, LANE))
        before = jnp.dot(strict_lower, row_tot, preferred_element_type=F32, precision=HI)
        o_ref[h] = within + before


def _fox_cumsum(fl, bias):
    H, R, _ = fl.shape
    return pl.pallas_call(
        _fox_cumsum_kernel,
        out_shape=jax.ShapeDtypeStruct((H, R, LANE), F32),
        compiler_params=pltpu.CompilerParams(vmem_limit_bytes=VMEM_LIMIT),
        name="fox_cumsum",
    )(fl, jnp.broadcast_to(bias.reshape(H, 1, 1), (H, 1, LANE)))


FOX_KA = 128


def _fox_kernel(qa_ref, ka_ref, vT_ref, o_ref, *, t, tc, tg):
    i = pl.program_id(1)
    hb = qa_ref.shape[0]
    Dh = vT_ref.shape[1]
    last = (i * t) // tg
    chains = [(h, q0) for h in range(hb) for q0 in range(0, t, tc)]

    def step(g, carry, masked):
        off = pl.multiple_of(g * tg, tg)
        out = []
        logits = [jnp.dot(ka_ref[h, pl.ds(off, tg), :], qa_ref[h, :, q0:q0 + tc],
                          preferred_element_type=F32) for h, q0 in chains]
        for n, (h, q0) in enumerate(chains):
            m_prev, l_prev, acc = carry[n]
            s = logits[n]
            if masked:
                kpos = off + lax.broadcasted_iota(I32, (tg, tc), 0)
                qpos = i * t + q0 + lax.broadcasted_iota(I32, (tg, tc), 1)
                s = jnp.where(kpos <= qpos, s, NEG)
            m_new = jnp.maximum(m_prev, jnp.max(s, axis=0, keepdims=True))
            alpha = jnp.exp(m_prev - m_new)
            p = jnp.exp(s - m_new)
            l_new = alpha * l_prev + jnp.sum(p, axis=0, keepdims=True)
            acc = alpha * acc + jnp.dot(vT_ref[h, :, pl.ds(off, tg)], p.astype(BF16),
                                        preferred_element_type=F32)
            out.append((m_new, l_new, acc))
        return tuple(out)

    init = tuple((jnp.full((1, tc), NEG, F32), jnp.zeros((1, tc), F32), jnp.zeros((Dh, tc), F32))
                 for _ in chains)
    carry = lax.fori_loop(0, last, lambda g, c: step(g, c, False), init)
    for (h, q0), (_, l_fin, acc) in zip(chains, step(last, carry, True)):
        o_ref[h, :, q0:q0 + tc] = (acc / l_fin).astype(o_ref.dtype)


def _fox_attention(qaT, ka, vT):
    H, KA, S = qaT.shape
    Dh = vT.shape[1]
    t = min(512, S)
    tc = min(256, t)
    tg = min(1024, S)
    hb = 2
    return pl.pallas_call(
        functools.partial(_fox_kernel, t=t, tc=tc, tg=tg),
        grid=(H // hb, S // t),
        in_specs=[pl.BlockSpec((hb, KA, t), lambda h, i: (h, 0, i)),
                  pl.BlockSpec((hb, S, KA), lambda h, i: (h, 0, 0)),
                  pl.BlockSpec((hb, Dh, S), lambda h, i: (h, 0, 0))],
        out_specs=pl.BlockSpec((hb, Dh, t), lambda h, i: (h, 0, i)),
        out_shape=jax.ShapeDtypeStruct((H, Dh, S), BF16),
        compiler_params=_params("arbitrary", "arbitrary"),
        name="fox_attention",
    )(qaT, ka, vT)


DSA_T = 128


def _dsa_kernel(qiT_ref, w_ref, qT_ref, wukT_ref, wuvT_ref, ki_ref, c_ref, cT_ref, o_ref,
                keys_sc, qlat_sc, m_sc, l_sc, acc_sc, *, t, tg, topk):
    i = pl.program_id(0)
    last = (i * t) // tg
    ng = last + 1

    qiT = qiT_ref[0]
    w = w_ref[0]

    def score_keys(g, masked):
        off = pl.multiple_of(g * tg, tg)
        rel = jnp.dot(ki_ref[pl.ds(off, tg), :], qiT, preferred_element_type=F32)
        sc = jnp.maximum(rel[:, 0:t], 0.0) * w[0:1, :]
        for h in range(1, IDX_HEADS):
            sc = sc + jnp.maximum(rel[:, h * t:(h + 1) * t], 0.0) * w[h:h + 1, :]
        if masked:
            kchunk = (off + lax.broadcasted_iota(I32, (tg, t), 0)) // CHUNK
            qchunk = (i * t + lax.broadcasted_iota(I32, (tg, t), 1)) // CHUNK
            sc = jnp.where(kchunk <= qchunk, sc, -jnp.inf)
        keys_sc[pl.ds(off, tg), :] = sc

    def fill(g, c):
        score_keys(g, False)
        return c

    lax.fori_loop(0, last, fill, 0)
    score_keys(last, True)

    def reduce_keys(fn, init):
        def body(g, acc):
            off = pl.multiple_of(g * tg, tg)
            for k in range(tg // t):
                acc = fn(acc, keys_sc[pl.ds(off + k * t, t), :])
            return acc
        return lax.fori_loop(0, ng, body, jnp.full((t, t), init, F32))

    def count(pred, cand):
        cb = jnp.broadcast_to(cand, (t, t))
        acc = reduce_keys(lambda a, x: a + pred(x, cb).astype(F32), 0.0)
        return jnp.sum(acc, axis=0, keepdims=True)

    count_ge = lambda cand: count(lambda x, c: x >= c, cand)
    mx = jnp.max(reduce_keys(jnp.maximum, -jnp.inf), axis=0, keepdims=True)
    mn = jnp.min(reduce_keys(lambda a, x: jnp.minimum(a, jnp.where(x == -jnp.inf, jnp.inf, x)),
                             jnp.inf), axis=0, keepdims=True)
    qpos = i * t + lax.broadcasted_iota(I32, (1, t), 1)
    n_adm = ((qpos // CHUNK + 1) * CHUNK).astype(F32)
    lo0 = mn
    hi0 = mx + jnp.maximum(jnp.abs(mx) * 1e-6, 1e-30)
    done0 = (n_adm <= topk).astype(F32)

    def unsettled(st):
        it, _, _, _, done = st
        return (it < 320) & (jnp.min(done) < 0.5)

    def bisect(st):
        it, lo, hi, c_lo, done = st
        cand = 0.5 * lo + 0.5 * hi
        collapsed = (cand <= lo) | (cand >= hi)
        c = count_ge(cand)
        move = (done < 0.5) & jnp.logical_not(collapsed)
        up = move & (c >= topk)
        lo = jnp.where(up, cand, lo)
        c_lo = jnp.where(up, c, c_lo)
        hi = jnp.where(move & (c < topk), cand, hi)
        done = jnp.maximum(done, ((c_lo == topk) | collapsed).astype(F32))
        return it + 1, lo, hi, c_lo, done

    _, thr, _, n_ge, _ = lax.while_loop(unsettled, bisect, (jnp.int32(0), lo0, hi0, n_adm, done0))
    has_ties = jnp.max(jnp.where(n_ge > topk, 1.0, 0.0)) > 0.0
    thr_b = jnp.broadcast_to(thr, (tg, t))

    for h in range(DSA_HEADS):
        ql = jnp.dot(wukT_ref[h], qT_ref[0, h], preferred_element_type=F32)
        qlat_sc[:, h * t:(h + 1) * t] = (ql * (DSA_HEAD_DIM ** -0.5)).astype(BF16)
    m_sc[...] = jnp.full(m_sc.shape, NEG, F32)
    l_sc[...] = jnp.zeros(l_sc.shape, F32)
    acc_sc[...] = jnp.zeros(acc_sc.shape, F32)

    def attend(g, seen_eq, with_ties, need=None):
        off = pl.multiple_of(g * tg, tg)
        key = keys_sc[pl.ds(off, tg), :]
        if with_ties:
            eq = key == thr_b
            eqf = eq.astype(F32)
            strict_lower = (lax.broadcasted_iota(I32, (tg, tg), 0)
                            > lax.broadcasted_iota(I32, (tg, tg), 1)).astype(BF16)
            rank = jnp.dot(strict_lower, eqf.astype(BF16), preferred_element_type=F32) + seen_eq
            sel = (key > thr_b) | (eq & (rank < need))
            seen_eq = seen_eq + jnp.sum(eqf, axis=0, keepdims=True)
        else:
            sel = key >= thr_b
        bias = jnp.where(sel, 0.0, NEG)
        lg = jnp.dot(c_ref[pl.ds(off, tg), :], qlat_sc[...], preferred_element_type=F32)
        cT_tile = cT_ref[:, pl.ds(off, tg)]
        for h in range(DSA_HEADS):
            s = lg[:, h * t:(h + 1) * t] + bias
            m_prev = m_sc[h]
            m_new = jnp.maximum(m_prev, jnp.max(s, axis=0, keepdims=True))
            alpha = jnp.exp(m_prev - m_new)
            p = jnp.exp(s - m_new)
            l_sc[h] = alpha * l_sc[h] + jnp.sum(p, axis=0, keepdims=True)
            acc_sc[h] = alpha * acc_sc[h] + jnp.dot(cT_tile, p.astype(BF16),
                                                    preferred_element_type=F32)
            m_sc[h] = m_new
        return seen_eq

    @pl.when(has_ties)
    def _():
        need = topk - count(lambda x, c: x > c, thr)
        lax.fori_loop(0, ng, lambda g, s: attend(g, s, True, need), jnp.zeros((1, t), F32))

    @pl.when(jnp.logical_not(has_ties))
    def _():
        lax.fori_loop(0, ng, lambda g, s: attend(g, s, False), jnp.zeros((1, t), F32))

    for h in range(DSA_HEADS):
        o_lat = (acc_sc[h] / l_sc[h]).astype(BF16)
        o_ref[0, h] = jnp.dot(wuvT_ref[h], o_lat, preferred_element_type=F32).astype(o_ref.dtype)


def _dsa_attention(qiT, w, qT, wukT, wuvT, ki, c, cT, topk):
    NB, H, Dh, t = qT.shape
    S = NB * t
    R = DSA_KV_RANK
    tg = min(512, S)
    const2 = lambda i: (0, 0)
    const3 = lambda i: (0, 0, 0)
    return pl.pallas_call(
        functools.partial(_dsa_kernel, t=t, tg=tg, topk=topk),
        grid=(NB,),
        in_specs=[pl.BlockSpec((1, IDX_HEAD_DIM, IDX_HEADS * t), lambda i: (i, 0, 0)),
                  pl.BlockSpec((1, IDX_HEADS, t), lambda i: (i, 0, 0)),
                  pl.BlockSpec((1, H, Dh, t), lambda i: (i, 0, 0, 0)),
                  pl.BlockSpec((H, R, Dh), const3),
                  pl.BlockSpec((H, Dh, R), const3),
                  pl.BlockSpec((S, IDX_HEAD_DIM), const2),
                  pl.BlockSpec((S, R), const2),
                  pl.BlockSpec((R, S), const2)],
        out_specs=pl.BlockSpec((1, H, Dh, t), lambda i: (i, 0, 0, 0)),
        out_shape=jax.ShapeDtypeStruct((NB, H, Dh, t), BF16),
        scratch_shapes=[pltpu.VMEM((S, t), F32), pltpu.VMEM((R, H * t), BF16),
                        pltpu.VMEM((H, 1, t), F32), pltpu.VMEM((H, 1, t), F32),
                        pltpu.VMEM((H, R, t), F32)],
        compiler_params=_params("arbitrary"),
        name="dsa_attention",
    )(qiT, w, qT, wukT, wuvT, ki, c, cT)


def _shift_mix(p, prev_tail, mu, first_block):
    rows = lax.broadcasted_iota(I32, p.shape, 0)
    tail = jnp.where(first_block, 0.0, prev_tail)
    prev = jnp.where(rows == 0, tail, pltpu.roll(p, 1, axis=0))
    return p + (prev - p) * mu


def _split3(f):
    hi = f.astype(BF16)
    r1 = f - hi.astype(F32)
    mid = r1.astype(BF16)
    lo = (r1 - mid.astype(F32)).astype(BF16)
    return hi, mid, lo


def _dot3(x, w01):
    return sum(jnp.dot(part, w01, preferred_element_type=F32) for part in _split3(x))


def _dot3_left(w01, x):
    return sum(jnp.dot(w01, part, preferred_element_type=F32) for part in _split3(x))


def _rwkv_pre_kernel(*refs, with_vres):
    (p_ref, pp_ref, mu_ref, w0_ref, a0_ref, wwa_ref, gup_ref, kk_ref, ka_ref, rk_ref,
     tri_ref, blk_ref, bd_ref) = refs[:13]
    if with_vres:
        hv_ref, hvp_ref, vmu_ref, vup_ref, vv0_ref, vf_ref = refs[13:19]
    (at_o, rt_o, bt_o, kt_o, bh_o, kh_o, vb_o, pc_o, bonus_o, g_o, v_o) = refs[-11:]
    first = pl.program_id(0) == 0
    W = RWKV_W
    ps = _shift_mix(p_ref[...], pp_ref[7:8, :], mu_ref[...], first)
    r, k, v = ps[:, 0:W], ps[:, W:2 * W], ps[:, 2 * W:3 * W]
    wa = ps[:, 3 * W:3 * W + LANE]
    gl = ps[:, 3 * W + LANE:3 * W + 2 * LANE]
    lane = lax.broadcasted_iota(I32, wa.shape, 1)
    wa = jnp.where(lane < RWKV_W_LORA, jnp.tanh(wa), wa)
    up = jnp.dot(wa.astype(BF16), wwa_ref[...], preferred_element_type=F32)
    log_w = -_softplus(-(w0_ref[...] + up[:, 0:W])) - 0.5
    a = _sigmoid(a0_ref[...] + up[:, W:2 * W])
    g = jnp.dot(_sigmoid(gl).astype(BF16), gup_ref[...], preferred_element_type=F32)
    if with_vres:
        vl = _shift_mix(hv_ref[...], hvp_ref[7:8, :], vmu_ref[...], first)
        logit = vv0_ref[...] + jnp.dot(vl.astype(BF16), vup_ref[...], preferred_element_type=F32)
        v = v + (vf_ref[...] - v) * _sigmoid(logit)
    k2 = k * (1.0 + (a - 1.0) * ka_ref[...])
    kkr = k * kk_ref[...]
    bd = bd_ref[...]
    kk = kkr * lax.rsqrt(_dot3(kkr * kkr, bd) + 1e-12)
    lw = -jnp.exp(log_w)
    cum = _dot3_left(tri_ref[...], lw)
    cend = _dot3_left(blk_ref[...], lw)
    p_inv = jnp.exp(-cum)
    p_end = jnp.exp(cend - cum)
    beta = kk * a
    at_o[...] = (-kk * jnp.exp(cum - lw)).astype(BF16)
    rt_o[...] = (r * jnp.exp(cum)).astype(BF16)
    bt_o[...] = (beta * p_inv).astype(BF16)
    kt_o[...] = (k2 * p_inv).astype(BF16)
    bh_o[...] = (beta * p_end).astype(BF16)
    kh_o[...] = (k2 * p_end).astype(BF16)
    vb_o[...] = v.astype(BF16)
    pc_o[...] = jnp.exp(cend)
    bonus_o[...] = _dot3(r * k2 * rk_ref[...], bd) * v
    g_o[...] = g
    v_o[...] = v


def _block_ones(n, block, lower_tri=False):
    i = jnp.arange(n)
    m = (i[:, None] // block) == (i[None, :] // block)
    if lower_tri:
        m = m & (i[:, None] >= i[None, :])
    return m.astype(BF16)


def _rwkv_pre(p, mu, w0, a0, wwa, gup, k_k, k_a, r_k, vres=None):
    S, PW = p.shape
    W = RWKV_W
    tm = min(512, S)
    row = lambda i: (i, 0)
    const = lambda i: (0, 0)
    tail = lambda i: (jnp.maximum(i * (tm // 8) - 1, 0), 0)
    vec = lambda a: a.reshape(1, -1)
    args = [p, p, vec(mu), vec(w0), vec(a0), wwa, gup, vec(k_k), vec(k_a), vec(r_k),
            _block_ones(tm, CHUNK, lower_tri=True), _block_ones(tm, CHUNK),
            _block_ones(W, RWKV_HEAD_DIM)]
    specs = [pl.BlockSpec((tm, PW), row), pl.BlockSpec((8, PW), tail),
             pl.BlockSpec((1, PW), const), pl.BlockSpec((1, W), const), pl.BlockSpec((1, W), const),
             pl.BlockSpec(wwa.shape, const), pl.BlockSpec(gup.shape, const),
             pl.BlockSpec((1, W), const), pl.BlockSpec((1, W), const), pl.BlockSpec((1, W), const),
             pl.BlockSpec((tm, tm), const), pl.BlockSpec((tm, tm), const),
             pl.BlockSpec((W, W), const)]
    if vres is not None:
        hv, vmu, vup, vv0, v_first = vres
        args += [hv, hv, vec(vmu), vup, vec(vv0), v_first]
        specs += [pl.BlockSpec((tm, LANE), row), pl.BlockSpec((8, LANE), tail),
                  pl.BlockSpec((1, LANE), const), pl.BlockSpec(vup.shape, const),
                  pl.BlockSpec((1, W), const), pl.BlockSpec((tm, W), row)]
    return pl.pallas_call(
        functools.partial(_rwkv_pre_kernel, with_vres=vres is not None),
        grid=(S // tm,),
        in_specs=specs,
        out_specs=[pl.BlockSpec((tm, W), row)] * 11,
        out_shape=[jax.ShapeDtypeStruct((S, W), BF16)] * 7 + [jax.ShapeDtypeStruct((S, W), F32)] * 4,
        compiler_params=_params("arbitrary"),
        name="rwkv_pre",
    )(*args)


def _rwkv_scan_kernel(at_ref, rt_ref, bt_ref, kt_ref, bh_ref, kh_ref, v_ref, pc_ref, y_ref, h_sc,
                      *, chunks):
    C = CHUNK
    N = RWKV_HEAD_DIM

    @pl.when(pl.program_id(0) == 0)
    def _():
        h_sc[...] = jnp.zeros(h_sc.shape, F32)

    ti = lax.broadcasted_iota(I32, (C, C), 0)
    tj = lax.broadcasted_iota(I32, (C, C), 1)
    lower_incl = ti >= tj
    lower_strict = ti > tj
    eye = (ti == tj).astype(F32)
    eye_n = (lax.broadcasted_iota(I32, (N, N), 0) == lax.broadcasted_iota(I32, (N, N), 1)).astype(F32)

    def mm(x, y):
        return jnp.dot(x.astype(BF16), y.astype(BF16), preferred_element_type=F32)

    def mm_nt(x, y):
        return lax.dot_general(x.astype(BF16), y.astype(BF16), (((1,), (1,)), ((), ())),
                               preferred_element_type=F32)

    def mm_tn(x, y):
        return lax.dot_general(x.astype(BF16), y.astype(BF16), (((0,), (0,)), ((), ())),
                               preferred_element_type=F32)

    units = [(c, h) for c in range(chunks) for h in range(RWKV_HEADS)]
    tile = lambda ref, u: ref[u[0] * C:(u[0] + 1) * C, u[1] * N:(u[1] + 1) * N]
    each = lambda fn: {u: fn(u) for u in units}

    At, Rt, Bt, Kt = (each(lambda u, r=ref: tile(r, u)) for ref in (at_ref, rt_ref, bt_ref, kt_ref))
    Bh, Kh, V = (each(lambda u, r=ref: tile(r, u)) for ref in (bh_ref, kh_ref, v_ref))
    AR = each(lambda u: jnp.concatenate([At[u], Rt[u]], axis=0))
    Mb = each(lambda u: mm_nt(AR[u], Bt[u]))
    Mk = each(lambda u: mm_nt(AR[u], Kt[u]))
    Lab = each(lambda u: jnp.where(lower_strict, Mb[u][0:C], 0.0))
    Mrb = each(lambda u: jnp.where(lower_incl, Mb[u][C:2 * C], 0.0))
    Lak = each(lambda u: jnp.where(lower_strict, Mk[u][0:C], 0.0))
    Mrk = each(lambda u: jnp.where(lower_incl, Mk[u][C:2 * C], 0.0))
    T = each(lambda u: eye + Lab[u])
    Lp = Lab
    span = 2
    while span < C:
        Lp = each(lambda u, Lp=Lp: mm(Lp[u], Lp[u]))
        T = each(lambda u, T=T, Lp=Lp: T[u] + mm(Lp[u], T[u]))
        span *= 2
    W1 = each(lambda u: mm(Lak[u], V[u]))
    A2 = each(lambda u: mm(T[u], At[u]))
    U0 = each(lambda u: mm(T[u], W1[u]))
    R2 = each(lambda u: Rt[u].astype(F32) + mm(Mrb[u], A2[u]))
    Y0 = each(lambda u: mm(Mrb[u], U0[u]) + mm(Mrk[u], V[u]))
    G = each(lambda u: eye_n * tile(pc_ref, u)[0:1, :] + mm_tn(Bh[u], A2[u]))
    H0 = each(lambda u: mm_tn(Bh[u], U0[u]) + mm_tn(Kh[u], V[u]))

    H = {h: h_sc[h] for h in range(RWKV_HEADS)}
    for c in range(chunks):
        ys = []
        for h in range(RWKV_HEADS):
            u = (c, h)
            ys.append(mm(R2[u], H[h]) + Y0[u])
            H[h] = mm(G[u], H[h]) + H0[u]
        y_ref[c * C:(c + 1) * C, :] = jnp.concatenate(ys, axis=1)
    for h in range(RWKV_HEADS):
        h_sc[h] = H[h]


def _rwkv_scan(at, rt, bt, kt, bh, kh, vb, pc):
    S, W = at.shape
    chunks = 2 if S % (2 * CHUNK) == 0 else 1
    tb = chunks * CHUNK
    seq = pl.BlockSpec((tb, W), lambda i: (i, 0))
    return pl.pallas_call(
        functools.partial(_rwkv_scan_kernel, chunks=chunks),
        grid=(S // tb,),
        in_specs=[seq] * 8,
        out_specs=seq,
        out_shape=jax.ShapeDtypeStruct((S, W), F32),
        scratch_shapes=[pltpu.VMEM((RWKV_HEADS, RWKV_HEAD_DIM, RWKV_HEAD_DIM), F32)],
        compiler_params=_params("arbitrary"),
        name="rwkv_scan",
    )(at, rt, bt, kt, bh, kh, vb, pc)


def _merge_kernel(x_ref, of_ref, od_ref, y_ref, bonus_ref, g_ref, lng_ref, lnb_ref, bd_ref,
                  gp_ref, bg_ref, pf_ref, pd_ref, pr_ref, wo_ref, gn_ref, xo_ref, h_ref):
    D = D_MODEL
    bd = bd_ref[...]
    inv_n = 1.0 / RWKV_HEAD_DIM
    y = y_ref[...]
    yc = y - _dot3(y, bd) * inv_n
    var = _dot3(yc * yc, bd) * inv_n
    yn = yc * lax.rsqrt(var + RWKV_LN_EPS) * lng_ref[...] + lnb_ref[...]
    o_rwkv = ((yn + bonus_ref[...]) * g_ref[...]).astype(BF16)
    gates = _sigmoid(gp_ref[...] + bg_ref[...])
    merged = (gates[:, 0:D] * jnp.dot(of_ref[...], pf_ref[...], preferred_element_type=F32)
              + gates[:, D:2 * D] * jnp.dot(od_ref[...], pd_ref[...], preferred_element_type=F32)
              + gates[:, 2 * D:3 * D] * jnp.dot(o_rwkv, pr_ref[...], preferred_element_type=F32))
    x = x_ref[...] + jnp.dot(merged.astype(BF16), wo_ref[...], preferred_element_type=F32)
    xo_ref[...] = x
    h_ref[...] = _rms(x, gn_ref[...]).astype(h_ref.dtype)


def _merge(x, o_fox, o_dsa, y_rwkv, bonus, g_rwkv, ln_g, ln_b, gate_p, b_gate, p_fox, p_dsa, p_rwkv,
           w_out, g_ffn):
    S, D = x.shape
    W = RWKV_W
    tm = min(512, S)
    row = lambda i: (i, 0)
    const = lambda i: (0, 0)
    return pl.pallas_call(
        _merge_kernel,
        grid=(S // tm,),
        in_specs=[pl.BlockSpec((tm, D), row), pl.BlockSpec((tm, FOX_W), row),
                  pl.BlockSpec((tm, DSA_W), row), pl.BlockSpec((tm, W), row),
                  pl.BlockSpec((tm, W), row), pl.BlockSpec((tm, W), row),
                  pl.BlockSpec((1, W), const), pl.BlockSpec((1, W), const),
                  pl.BlockSpec((W, W), const),
                  pl.BlockSpec((tm, 3 * D), row), pl.BlockSpec((1, 3 * D), const),
                  pl.BlockSpec((FOX_W, D), const), pl.BlockSpec((DSA_W, D), const),
                  pl.BlockSpec((W, D), const), pl.BlockSpec((D, D), const),
                  pl.BlockSpec((1, D), const)],
        out_specs=[pl.BlockSpec((tm, D), row), pl.BlockSpec((tm, D), row)],
        out_shape=[jax.ShapeDtypeStruct((S, D), F32), jax.ShapeDtypeStruct((S, D), BF16)],
        compiler_params=_params("arbitrary"),
        name="merge",
    )(x, o_fox, o_dsa, y_rwkv, bonus, g_rwkv, ln_g.reshape(1, W), ln_b.reshape(1, W),
      _block_ones(W, RWKV_HEAD_DIM), gate_p, b_gate.reshape(1, -1), p_fox, p_dsa, p_rwkv, w_out,
      g_ffn.reshape(1, D))


def _swiglu_tile(h, wg, wu):
    gate = jnp.dot(h, wg, preferred_element_type=F32)
    up = jnp.dot(h, wu, preferred_element_type=F32)
    return gate * _sigmoid(gate) * up


def _ffn_kernel(x_ref, h_ref, wg_ref, wu_ref, wd_ref, gf_ref, o_ref, acc_sc, *, final_norm):
    f = pl.program_id(1)

    @pl.when(f == 0)
    def _():
        acc_sc[...] = x_ref[...]

    act = _swiglu_tile(h_ref[...], wg_ref[...], wu_ref[...])
    acc_sc[...] += jnp.dot(act.astype(BF16), wd_ref[...], preferred_element_type=F32)

    @pl.when(f == pl.num_programs(1) - 1)
    def _():
        y = acc_sc[...]
        o_ref[...] = _rms(y, gf_ref[...]) if final_norm else y


def _ffn(x, h, wg, wu, wd, g_final, final_norm):
    S, D = x.shape
    Fd = wg.shape[1]
    tm = min(512, S)
    tf = _pick_tile(Fd, 1408)
    return pl.pallas_call(
        functools.partial(_ffn_kernel, final_norm=final_norm),
        grid=(S // tm, Fd // tf),
        in_specs=[pl.BlockSpec((tm, D), lambda i, f: (i, 0)),
                  pl.BlockSpec((tm, D), lambda i, f: (i, 0)),
                  pl.BlockSpec((D, tf), lambda i, f: (0, f)),
                  pl.BlockSpec((D, tf), lambda i, f: (0, f)),
                  pl.BlockSpec((tf, D), lambda i, f: (f, 0)),
                  pl.BlockSpec((1, D), lambda i, f: (0, 0))],
        out_specs=pl.BlockSpec((tm, D), lambda i, f: (i, 0)),
        out_shape=jax.ShapeDtypeStruct((S, D), F32),
        scratch_shapes=[pltpu.VMEM((tm, D), F32)],
        compiler_params=_params("arbitrary", "arbitrary"),
        name="ffn",
    )(x, h, wg, wu, wd, g_final.reshape(1, D))


def _moe_kernel(x_ref, h_ref, rw_ref, rb_ref, wg_ref, wu_ref, wd_ref, gf_ref, o_ref,
                acc_sc, gate_sc, *, final_norm):
    e = pl.program_id(1)
    f = pl.program_id(2)
    lane = lax.broadcasted_iota(I32, gate_sc.shape, 1)

    @pl.when((e == 0) & (f == 0))
    def _():
        acc_sc[...] = x_ref[...]
        logits = jnp.dot(h_ref[...], rw_ref[...], preferred_element_type=F32) + rb_ref[...]
        logits = jnp.where(lane < N_EXPERTS, logits, -jnp.inf)
        v1 = jnp.max(logits, axis=-1, keepdims=True)
        i1 = jnp.min(jnp.where(logits == v1, lane, LANE), axis=-1, keepdims=True)
        rest = jnp.where(lane == i1, -jnp.inf, logits)
        v2 = jnp.max(rest, axis=-1, keepdims=True)
        i2 = jnp.min(jnp.where(rest == v2, lane, LANE), axis=-1, keepdims=True)
        e2 = jnp.exp(v2 - v1)
        p1 = 1.0 / (1.0 + e2)
        gate_sc[...] = jnp.where(lane == i1, p1, 0.0) + jnp.where(lane == i2, e2 * p1, 0.0)

    gate = jnp.sum(jnp.where(lane == e, gate_sc[...], 0.0), axis=-1, keepdims=True)
    act = _swiglu_tile(h_ref[...], wg_ref[0], wu_ref[0]) * gate
    acc_sc[...] += jnp.dot(act.astype(BF16), wd_ref[0], preferred_element_type=F32)

    @pl.when((e == pl.num_programs(1) - 1) & (f == pl.num_programs(2) - 1))
    def _():
        y = acc_sc[...]
        o_ref[...] = _rms(y, gf_ref[...]) if final_norm else y


def _moe(x, h, rw, rb, wg, wu, wd, g_final, final_norm):
    S, D = x.shape
    E, _, Fe = wg.shape
    tm = min(1024, S)
    tf = _pick_tile(Fe, 896)
    return pl.pallas_call(
        functools.partial(_moe_kernel, final_norm=final_norm),
        grid=(S // tm, E, Fe // tf),
        in_specs=[pl.BlockSpec((tm, D), lambda i, e, f: (i, 0)),
                  pl.BlockSpec((tm, D), lambda i, e, f: (i, 0)),
                  pl.BlockSpec((D, LANE), lambda i, e, f: (0, 0)),
                  pl.BlockSpec((1, LANE), lambda i, e, f: (0, 0)),
                  pl.BlockSpec((1, D, tf), lambda i, e, f: (e, 0, f)),
                  pl.BlockSpec((1, D, tf), lambda i, e, f: (e, 0, f)),
                  pl.BlockSpec((1, tf, D), lambda i, e, f: (e, f, 0)),
                  pl.BlockSpec((1, D), lambda i, e, f: (0, 0))],
        out_specs=pl.BlockSpec((tm, D), lambda i, e, f: (i, 0)),
        out_shape=jax.ShapeDtypeStruct((S, D), F32),
        scratch_shapes=[pltpu.VMEM((tm, D), F32), pltpu.VMEM((tm, LANE), F32)],
        compiler_params=_params("arbitrary", "arbitrary", "arbitrary"),
        name="moe",
    )(x, h, rw, rb, wg, wu, wd, g_final.reshape(1, D))


def _pad_cols(a, n):
    return jnp.pad(a, ((0, 0), (0, n - a.shape[1])))


def kernel(x, w_in, b_gate, g_mix, fox_f_bias, dsa_kv_norm, dsa_w_uk, dsa_w_uv, rwkv_mu, rwkv_w0, rwkv_w_up, rwkv_a0, rwkv_a_up, rwkv_g_up, rwkv_k_k, rwkv_k_a, rwkv_r_k, rwkv_ln_g, rwkv_ln_b, vres_down, vres_mu, vres_up, vres_v0, p_fox, p_dsa, p_rwkv, w_out, g_ffn, ffn_w_gate, ffn_w_up, ffn_w_down, router_w, router_b, moe_w_gate, moe_w_up, moe_w_down, g_final):
    B, S, D = x.shape
    assert B == 1 and D == D_MODEL and S % LANE == 0
    depth = w_in.shape[0]
    topk = min(IDX_TOPK, S // 4)
    bf = lambda a: a.astype(BF16)
    xs = x[0]
    v_first = None
    for l in range(depth):
        wl = w_in[l]
        w_fox, w_dsa, w_rwkv, w_gate = (wl[:, :FOX_IN], wl[:, FOX_IN:FOX_IN + DSA_IN],
                                        wl[:, FOX_IN + DSA_IN:FOX_IN + DSA_IN + RWKV_IN],
                                        wl[:, FOX_IN + DSA_IN + RWKV_IN:])
        o1 = DSA_W + DSA_KV_RANK + IDX_W
        w_att = jnp.concatenate([_pad_cols(w_fox, 3 * FOX_W + LANE), w_dsa[:, :o1],
                                 _pad_cols(w_dsa[:, o1:], LANE)], axis=1)
        if l > 0:
            w_rwkv = jnp.concatenate([w_rwkv, _pad_cols(vres_down[l - 1], LANE)], axis=1)
        pa = _rms_proj(xs, g_mix[l], bf(w_att))
        pr = _rms_proj(xs, g_mix[l], bf(w_rwkv))
        gate_p = _rms_proj(xs, g_mix[l], bf(w_gate))

        c0 = 3 * FOX_W
        fl = pa[:, c0:c0 + FOX_HEADS].T.reshape(FOX_HEADS, S // LANE, LANE)
        F = _fox_cumsum(fl, fox_f_bias[l]).reshape(FOX_HEADS, S)
        Fh, Fm, Fl = _split3(F)
        one = jnp.ones((FOX_HEADS, S), BF16)
        zpad = jnp.zeros((FOX_HEADS, FOX_KA - FOX_HEAD_DIM - 6, S), BF16)
        qT = bf(pa[:, 0:FOX_W].T.reshape(FOX_HEADS, FOX_HEAD_DIM, S) * (FOX_HEAD_DIM ** -0.5))
        kT = bf(pa[:, FOX_W:2 * FOX_W].T.reshape(FOX_HEADS, FOX_HEAD_DIM, S))
        stack = lambda *rows: jnp.stack(rows, axis=1)
        qaT = jnp.concatenate([qT, stack(Fh, Fm, Fl, one, one, one), zpad], axis=1)
        kaT = jnp.concatenate([kT, stack(one, one, one, -Fh, -Fm, -Fl), zpad], axis=1)
        fvT = bf(pa[:, 2 * FOX_W:3 * FOX_W].T.reshape(FOX_HEADS, FOX_HEAD_DIM, S))
        o_fox = _fox_attention(qaT, kaT.transpose(0, 2, 1), fvT).reshape(FOX_W, S).T

        t = min(DSA_T, S)
        nb = S // t
        c1 = c0 + LANE
        dqT = bf(pa[:, c1:c1 + DSA_W].T.reshape(DSA_HEADS, DSA_HEAD_DIM, nb, t).transpose(2, 0, 1, 3))
        c2 = c1 + DSA_W
        ckv = _rmsnorm(pa[:, c2:c2 + DSA_KV_RANK], dsa_kv_norm[l], BF16)
        c3 = c2 + DSA_KV_RANK
        qiT = bf(pa[:, c3:c3 + IDX_W].T.reshape(IDX_HEADS, IDX_HEAD_DIM, nb, t).transpose(2, 1, 0, 3)
                 .reshape(nb, IDX_HEAD_DIM, IDX_HEADS * t))
        c4 = c3 + IDX_W
        ki = bf(pa[:, c4:c4 + IDX_HEAD_DIM])
        c5 = c4 + IDX_HEAD_DIM
        wi = pa[:, c5:c5 + IDX_HEADS].T.reshape(IDX_HEADS, nb, t).transpose(1, 0, 2) * (IDX_W ** -0.5)
        o_dsa = _dsa_attention(qiT, wi, dqT, bf(dsa_w_uk[l].transpose(0, 2, 1)),
                               bf(dsa_w_uv[l].transpose(0, 2, 1)), ki, ckv, ckv.T, topk)
        o_dsa = o_dsa.transpose(1, 2, 0, 3).reshape(DSA_W, S).T

        zw = jnp.zeros((RWKV_W_LORA, RWKV_W), F32)
        wwa = bf(jnp.concatenate([jnp.concatenate([rwkv_w_up[l], zw], axis=1),
                                  jnp.concatenate([zw, rwkv_a_up[l]], axis=1)], axis=0))
        vres = None
        if l > 0:
            vup = jnp.pad(vres_up[l - 1], ((0, LANE - RWKV_V_LORA), (0, 0)))
            vres = (pr[:, RWKV_IN:], jnp.pad(vres_mu[l - 1], (0, LANE - RWKV_V_LORA)),
                    bf(vup), vres_v0[l - 1], v_first)
        *scan_ops, bonus, g_rwkv, v = _rwkv_pre(
            pr[:, :RWKV_IN], rwkv_mu[l], rwkv_w0[l], rwkv_a0[l], wwa, bf(rwkv_g_up[l]),
            rwkv_k_k[l], rwkv_k_a[l], rwkv_r_k[l], vres)
        if l == 0:
            v_first = v
        y_rwkv = _rwkv_scan(*scan_ops)

        xs, h2 = _merge(xs, o_fox, o_dsa, y_rwkv, bonus, g_rwkv, rwkv_ln_g[l], rwkv_ln_b[l],
                        gate_p, b_gate[l], bf(p_fox[l]), bf(p_dsa[l]), bf(p_rwkv[l]), bf(w_out[l]),
                        g_ffn[l])

        last = l == depth - 1
        if l % 2 == 0:
            xs = _ffn(xs, h2, bf(ffn_w_gate[l // 2]), bf(ffn_w_up[l // 2]), bf(ffn_w_down[l // 2]),
                      g_final, last)
        else:
            rw = bf(_pad_cols(router_w[l // 2], LANE))
            rb = _pad_cols(router_b[l // 2].reshape(1, -1), LANE)
            xs = _moe(xs, h2, rw, rb, bf(moe_w_gate[l // 2]), bf(moe_w_up[l // 2]),
                      bf(moe_w_down[l // 2]), g_final, last)
    return xs[None]
```

```python
import functools

import jax
import jax.numpy as jnp
from jax import lax
from jax.experimental import pallas as pl
from jax.experimental.pallas import tpu as pltpu

F32 = jnp.float32
BF16 = jnp.bfloat16
I32 = jnp.int32

D_MODEL = 1024
CHUNK = 64
RMS_EPS = 1e-6
FOX_HEADS, FOX_HEAD_DIM = 4, 64
DSA_HEADS, DSA_HEAD_DIM, DSA_KV_RANK = 4, 64, 128
IDX_HEADS, IDX_HEAD_DIM, IDX_TOPK = 8, 32, 256
RWKV_HEADS, RWKV_HEAD_DIM = 8, 64
RWKV_W_LORA, RWKV_A_LORA, RWKV_V_LORA, RWKV_G_LORA = 64, 64, 32, 128
RWKV_LN_EPS = 64e-5
FOX_W = FOX_HEADS * FOX_HEAD_DIM
DSA_W = DSA_HEADS * DSA_HEAD_DIM
RWKV_W = RWKV_HEADS * RWKV_HEAD_DIM
IDX_W = IDX_HEADS * IDX_HEAD_DIM
N_EXPERTS = 8
FOX_IN = 3 * FOX_W + FOX_HEADS
DSA_IN = DSA_W + DSA_KV_RANK + IDX_W + IDX_HEAD_DIM + IDX_HEADS
RWKV_IN = 3 * RWKV_W + RWKV_W_LORA + RWKV_A_LORA + RWKV_G_LORA

LANE = 128
VMEM_LIMIT = 52 * 1024 * 1024
NEG = -1e30
INT_MIN = -(2 ** 31)
HI = lax.Precision.HIGHEST


def _params(*sem):
    return pltpu.CompilerParams(dimension_semantics=sem, vmem_limit_bytes=VMEM_LIMIT)


def _pick_tile(n, cap):
    best = LANE
    for t in range(LANE, min(n, cap) + 1, LANE):
        if n % t == 0:
            best = t
    return best


def _softplus(x):
    return jnp.maximum(x, 0.0) + jnp.log1p(jnp.exp(-jnp.abs(x)))


def _sigmoid(x):
    return 1.0 / (1.0 + jnp.exp(-x))


def _rms(x, g):
    return x * lax.rsqrt(jnp.mean(x * x, axis=-1, keepdims=True) + RMS_EPS) * g


def _rms_proj_kernel(x_ref, g_ref, w_ref, o_ref):
    h = _rms(x_ref[...], g_ref[...])
    o_ref[...] = jnp.dot(h.astype(BF16), w_ref[...], preferred_element_type=F32)


def _rms_proj(x, g, w):
    S, D = x.shape
    N = w.shape[1]
    tm = min(512, S)
    tn = _pick_tile(N, 2304)
    return pl.pallas_call(
        _rms_proj_kernel,
        grid=(N // tn, S // tm),
        in_specs=[pl.BlockSpec((tm, D), lambda j, i: (i, 0)),
                  pl.BlockSpec((1, D), lambda j, i: (0, 0)),
                  pl.BlockSpec((D, tn), lambda j, i: (0, j))],
        out_specs=pl.BlockSpec((tm, tn), lambda j, i: (i, j)),
        out_shape=jax.ShapeDtypeStruct((S, N), F32),
        compiler_params=_params("arbitrary", "arbitrary"),
        name="rms_proj",
    )(x, g.reshape(1, D), w)


def _rmsnorm_kernel(x_ref, g_ref, o_ref):
    o_ref[...] = _rms(x_ref[...], g_ref[...]).astype(o_ref.dtype)


def _rmsnorm(x, g, dtype):
    S, D = x.shape
    tm = min(2048, S)
    return pl.pallas_call(
        _rmsnorm_kernel,
        grid=(S // tm,),
        in_specs=[pl.BlockSpec((tm, D), lambda i: (i, 0)),
                  pl.BlockSpec((1, D), lambda i: (0, 0))],
        out_specs=pl.BlockSpec((tm, D), lambda i: (i, 0)),
        out_shape=jax.ShapeDtypeStruct((S, D), dtype),
        compiler_params=_params("arbitrary"),
        name="rmsnorm",
    )(x, g.reshape(1, D))


def _fox_cumsum_kernel(fl_ref, b_ref, o_ref):
    H, R, _ = fl_ref.shape
    upper = (lax.broadcasted_iota(I32, (LANE, LANE), 0)
             <= lax.broadcasted_iota(I32, (LANE, LANE), 1)).astype(F32)
    strict_lower = (lax.broadcasted_iota(I32, (R, R), 0)
                    > lax.broadcasted_iota(I32, (R, R), 1)).astype(F32)
    for h in range(H):
        log_f = -_softplus(-(fl_ref[h] + b_ref[h]))
        within = jnp.dot(log_f, upper, preferred_element_type=F32, precision=HI)
        row_tot = jnp.broadcast_to(within[:, LANE - 1:LANE], (R, LANE))
        before = jnp.dot(strict_lower, row_tot, preferred_element_type=F32, precision=HI)
        o_ref[h] = within + before


def _fox_cumsum(fl, bias):
    H, R, _ = fl.shape
    return pl.pallas_call(
        _fox_cumsum_kernel,
        out_shape=jax.ShapeDtypeStruct((H, R, LANE), F32),
        compiler_params=pltpu.CompilerParams(vmem_limit_bytes=VMEM_LIMIT),
        name="fox_cumsum",
    )(fl, jnp.broadcast_to(bias.reshape(H, 1, 1), (H, 1, LANE)))


FOX_KA = 128


def _fox_kernel(qa_ref, ka_ref, vT_ref, o_ref, lga_sc, lgb_sc, m_sc, l_sc, acc_sc, *, t, tc, tg):
    i = pl.program_id(1)
    hb = qa_ref.shape[0]
    last = (i * t) // tg
    chains = [(h, q0) for h in range(hb) for q0 in range(0, t, tc)]

    def logits_into(dst_ref, g):
        off = pl.multiple_of(jnp.minimum(g, last) * tg, tg)
        for n, (h, q0) in enumerate(chains):
            dst_ref[n] = jnp.dot(ka_ref[h, pl.ds(off, tg), :], qa_ref[h, :, q0:q0 + tc],
                                 preferred_element_type=F32)

    def update(lg_ref, g, masked):
        off = pl.multiple_of(g * tg, tg)
        for n, (h, q0) in enumerate(chains):
            s = lg_ref[n]
            if masked:
                kpos = off + lax.broadcasted_iota(I32, (tg, tc), 0)
                qpos = i * t + q0 + lax.broadcasted_iota(I32, (tg, tc), 1)
                s = jnp.where(kpos <= qpos, s, NEG)
            m_prev = m_sc[n]
            m_new = jnp.maximum(m_prev, jnp.max(s, axis=0, keepdims=True))
            alpha = jnp.exp(m_prev - m_new)
            p = jnp.exp(s - m_new)
            l_sc[n] = alpha * l_sc[n] + jnp.sum(p, axis=0, keepdims=True)
            acc_sc[n] = alpha * acc_sc[n] + jnp.dot(vT_ref[h, :, pl.ds(off, tg)], p.astype(BF16),
                                                    preferred_element_type=F32)
            m_sc[n] = m_new

    m_sc[...] = jnp.full(m_sc.shape, NEG, F32)
    l_sc[...] = jnp.zeros(l_sc.shape, F32)
    acc_sc[...] = jnp.zeros(acc_sc.shape, F32)
    logits_into(lga_sc, 0)

    def pair(j, c):
        logits_into(lgb_sc, 2 * j + 1)
        update(lga_sc, 2 * j, False)
        logits_into(lga_sc, 2 * j + 2)
        update(lgb_sc, 2 * j + 1, False)
        return c

    lax.fori_loop(0, last // 2, pair, 0)
    tail = 2 * (last // 2)
    logits_into(lgb_sc, tail + 1)
    update(lga_sc, tail, True)

    @pl.when(tail + 1 <= last)
    def _():
        update(lgb_sc, tail + 1, True)

    for n, (h, q0) in enumerate(chains):
        o_ref[h, :, q0:q0 + tc] = (acc_sc[n] / l_sc[n]).astype(o_ref.dtype)


def _fox_attention(qaT, ka, vT):
    H, KA, S = qaT.shape
    Dh = vT.shape[1]
    t = min(512, S)
    tc = min(256, t)
    tg = min(1024, S)
    hb = 2
    nc = hb * (t // tc)
    return pl.pallas_call(
        functools.partial(_fox_kernel, t=t, tc=tc, tg=tg),
        grid=(H // hb, S // t),
        in_specs=[pl.BlockSpec((hb, KA, t), lambda h, i: (h, 0, i)),
                  pl.BlockSpec((hb, S, KA), lambda h, i: (h, 0, 0)),
                  pl.BlockSpec((hb, Dh, S), lambda h, i: (h, 0, 0))],
        out_specs=pl.BlockSpec((hb, Dh, t), lambda h, i: (h, 0, i)),
        out_shape=jax.ShapeDtypeStruct((H, Dh, S), BF16),
        scratch_shapes=[pltpu.VMEM((nc, tg, tc), F32), pltpu.VMEM((nc, tg, tc), F32),
                        pltpu.VMEM((nc, 1, tc), F32), pltpu.VMEM((nc, 1, tc), F32),
                        pltpu.VMEM((nc, Dh, tc), F32)],
        compiler_params=_params("arbitrary", "arbitrary"),
        name="fox_attention",
    )(qaT, ka, vT)


DSA_T = 128


def _dsa_kernel(qiT_ref, w_ref, qT_ref, wukT_ref, wuvT_ref, ki_ref, c_ref, cT_ref, o_ref,
                keys_sc, qlat_sc, m_sc, l_sc, acc_sc, lga_sc, lgb_sc, rela_sc, relb_sc,
                *, t, tg, topk):
    i = pl.program_id(0)
    last = (i * t) // tg
    ng = last + 1

    qiT = qiT_ref[0]
    w = w_ref[0]

    def rel_into(dst_ref, g):
        off = pl.multiple_of(jnp.minimum(g, last) * tg, tg)
        dst_ref[...] = jnp.dot(ki_ref[pl.ds(off, tg), :], qiT, preferred_element_type=F32)

    def score_keys(rel_ref, g, masked):
        off = pl.multiple_of(g * tg, tg)
        sc = jnp.maximum(rel_ref[:, 0:t], 0.0) * w[0:1, :]
        for h in range(1, IDX_HEADS):
            sc = sc + jnp.maximum(rel_ref[:, h * t:(h + 1) * t], 0.0) * w[h:h + 1, :]
        bits = pltpu.bitcast(sc, I32)
        key = bits ^ ((bits >> 31) & 0x7FFFFFFF)
        key = jnp.where(sc == 0.0, 0, key)
        if masked:
            kchunk = (off + lax.broadcasted_iota(I32, (tg, t), 0)) // CHUNK
            qchunk = (i * t + lax.broadcasted_iota(I32, (tg, t), 1)) // CHUNK
            key = jnp.where(kchunk <= qchunk, key, INT_MIN)
        keys_sc[pl.ds(off, tg), :] = key

    def fill_pair(j, c):
        rel_into(relb_sc, 2 * j + 1)
        score_keys(rela_sc, 2 * j, False)
        rel_into(rela_sc, 2 * j + 2)
        score_keys(relb_sc, 2 * j + 1, False)
        return c

    rel_into(rela_sc, 0)
    lax.fori_loop(0, last // 2, fill_pair, 0)
    tail = 2 * (last // 2)
    rel_into(relb_sc, tail + 1)
    score_keys(rela_sc, tail, True)

    @pl.when(tail + 1 <= last)
    def _():
        score_keys(relb_sc, tail + 1, True)

    def count_ge(cand):
        cb = jnp.broadcast_to(cand, (t, t))

        def body(g, acc):
            off = pl.multiple_of(g * tg, tg)
            for k in range(tg // t):
                acc = acc + (keys_sc[pl.ds(off + k * t, t), :] >= cb).astype(F32)
            return acc

        acc = lax.fori_loop(0, ng, body, jnp.zeros((t, t), F32))
        return jnp.sum(acc, axis=0, keepdims=True)

    def bisect(s, st):
        thr, n_ge = st
        cand = thr + jnp.left_shift(jnp.int32(1), 31 - s)
        c = count_ge(cand)
        keep = c >= topk
        return jnp.where(keep, cand, thr), jnp.where(keep, c, n_ge)

    thr, n_ge = lax.fori_loop(0, 32, bisect,
                              (jnp.full((1, t), INT_MIN, I32), jnp.zeros((1, t), F32)))
    thr = jnp.maximum(thr, INT_MIN + 1)
    has_ties = jnp.max(jnp.where(n_ge > topk, 1.0, 0.0)) > 0.0
    thr_b = jnp.broadcast_to(thr, (tg, t))

    for h in range(DSA_HEADS):
        ql = jnp.dot(wukT_ref[h], qT_ref[0, h], preferred_element_type=F32)
        qlat_sc[:, h * t:(h + 1) * t] = (ql * (DSA_HEAD_DIM ** -0.5)).astype(BF16)
    m_sc[...] = jnp.full(m_sc.shape, NEG, F32)
    l_sc[...] = jnp.zeros(l_sc.shape, F32)
    acc_sc[...] = jnp.zeros(acc_sc.shape, F32)

    def logits_into(dst_ref, g):
        off = pl.multiple_of(jnp.minimum(g, last) * tg, tg)
        dst_ref[...] = jnp.dot(c_ref[pl.ds(off, tg), :], qlat_sc[...],
                               preferred_element_type=F32)

    def attend(lg_ref, g, seen_eq, with_ties, need):
        valid = g <= last
        off = pl.multiple_of(jnp.minimum(g, last) * tg, tg)
        key = keys_sc[pl.ds(off, tg), :]
        if with_ties:
            eq = (key == thr_b) & valid
            eqf = eq.astype(F32)
            strict_lower = (lax.broadcasted_iota(I32, (tg, tg), 0)
                            > lax.broadcasted_iota(I32, (tg, tg), 1)).astype(BF16)
            rank = jnp.dot(strict_lower, eqf.astype(BF16), preferred_element_type=F32) + seen_eq
            sel = (key > thr_b) | (eq & (rank < need))
            seen_eq = seen_eq + jnp.sum(eqf, axis=0, keepdims=True)
        else:
            sel = key >= thr_b
        bias = jnp.where(sel & valid, 0.0, NEG)
        cT_tile = cT_ref[:, pl.ds(off, tg)]
        for h in range(DSA_HEADS):
            s = lg_ref[:, h * t:(h + 1) * t] + bias
            m_prev = m_sc[h]
            m_new = jnp.maximum(m_prev, jnp.max(s, axis=0, keepdims=True))
            alpha = jnp.exp(m_prev - m_new)
            p = jnp.exp(s - m_new)
            l_sc[h] = alpha * l_sc[h] + jnp.sum(p, axis=0, keepdims=True)
            acc_sc[h] = alpha * acc_sc[h] + jnp.dot(cT_tile, p.astype(BF16),
                                                    preferred_element_type=F32)
            m_sc[h] = m_new
        return seen_eq

    def sweep(with_ties, need=None):
        logits_into(lga_sc, 0)

        def pair(j, seen_eq):
            g0 = 2 * j
            logits_into(lgb_sc, g0 + 1)
            seen_eq = attend(lga_sc, g0, seen_eq, with_ties, need)
            logits_into(lga_sc, g0 + 2)
            return attend(lgb_sc, g0 + 1, seen_eq, with_ties, need)

        lax.fori_loop(0, last // 2 + 1, pair, jnp.zeros((1, t), F32))

    @pl.when(has_ties)
    def _():
        sweep(True, topk - count_ge(thr + 1))

    @pl.when(jnp.logical_not(has_ties))
    def _():
        sweep(False)

    for h in range(DSA_HEADS):
        o_lat = (acc_sc[h] / l_sc[h]).astype(BF16)
        o_ref[0, h] = jnp.dot(wuvT_ref[h], o_lat, preferred_element_type=F32).astype(o_ref.dtype)


def _dsa_attention(qiT, w, qT, wukT, wuvT, ki, c, cT, topk):
    NB, H, Dh, t = qT.shape
    S = NB * t
    R = DSA_KV_RANK
    tg = min(512, S)
    const2 = lambda i: (0, 0)
    const3 = lambda i: (0, 0, 0)
    return pl.pallas_call(
        functools.partial(_dsa_kernel, t=t, tg=tg, topk=topk),
        grid=(NB,),
        in_specs=[pl.BlockSpec((1, IDX_HEAD_DIM, IDX_HEADS * t), lambda i: (i, 0, 0)),
                  pl.BlockSpec((1, IDX_HEADS, t), lambda i: (i, 0, 0)),
                  pl.BlockSpec((1, H, Dh, t), lambda i: (i, 0, 0, 0)),
                  pl.BlockSpec((H, R, Dh), const3),
                  pl.BlockSpec((H, Dh, R), const3),
                  pl.BlockSpec((S, IDX_HEAD_DIM), const2),
                  pl.BlockSpec((S, R), const2),
                  pl.BlockSpec((R, S), const2)],
        out_specs=pl.BlockSpec((1, H, Dh, t), lambda i: (i, 0, 0, 0)),
        out_shape=jax.ShapeDtypeStruct((NB, H, Dh, t), BF16),
        scratch_shapes=[pltpu.VMEM((S, t), I32), pltpu.VMEM((R, H * t), BF16),
                        pltpu.VMEM((H, 1, t), F32), pltpu.VMEM((H, 1, t), F32),
                        pltpu.VMEM((H, R, t), F32),
                        pltpu.VMEM((tg, H * t), F32), pltpu.VMEM((tg, H * t), F32),
                        pltpu.VMEM((tg, IDX_HEADS * t), F32), pltpu.VMEM((tg, IDX_HEADS * t), F32)],
        compiler_params=_params("arbitrary"),
        name="dsa_attention",
    )(qiT, w, qT, wukT, wuvT, ki, c, cT)


def _shift_mix(p, prev_tail, mu, first_block):
    rows = lax.broadcasted_iota(I32, p.shape, 0)
    tail = jnp.where(first_block, 0.0, prev_tail)
    prev = jnp.where(rows == 0, tail, pltpu.roll(p, 1, axis=0))
    return p + (prev - p) * mu


def _split3(f):
    hi = f.astype(BF16)
    r1 = f - hi.astype(F32)
    mid = r1.astype(BF16)
    lo = (r1 - mid.astype(F32)).astype(BF16)
    return hi, mid, lo


def _dot3(x, w01):
    return sum(jnp.dot(part, w01, preferred_element_type=F32) for part in _split3(x))


def _dot3_left(w01, x):
    return sum(jnp.dot(w01, part, preferred_element_type=F32) for part in _split3(x))


def _rwkv_pre_kernel(*refs, with_vres):
    (p_ref, pp_ref, mu_ref, w0_ref, a0_ref, wwa_ref, gup_ref, kk_ref, ka_ref, rk_ref,
     tri_ref, blk_ref, bd_ref) = refs[:13]
    if with_vres:
        hv_ref, hvp_ref, vmu_ref, vup_ref, vv0_ref, vf_ref = refs[13:19]
    (at_o, rt_o, bt_o, kt_o, bh_o, kh_o, vb_o, pc_o, bonus_o, g_o, v_o) = refs[-11:]
    first = pl.program_id(0) == 0
    W = RWKV_W
    ps = _shift_mix(p_ref[...], pp_ref[7:8, :], mu_ref[...], first)
    r, k, v = ps[:, 0:W], ps[:, W:2 * W], ps[:, 2 * W:3 * W]
    wa = ps[:, 3 * W:3 * W + LANE]
    gl = ps[:, 3 * W + LANE:3 * W + 2 * LANE]
    lane = lax.broadcasted_iota(I32, wa.shape, 1)
    wa = jnp.where(lane < RWKV_W_LORA, jnp.tanh(wa), wa)
    up = jnp.dot(wa.astype(BF16), wwa_ref[...], preferred_element_type=F32)
    log_w = -_softplus(-(w0_ref[...] + up[:, 0:W])) - 0.5
    a = _sigmoid(a0_ref[...] + up[:, W:2 * W])
    g = jnp.dot(_sigmoid(gl).astype(BF16), gup_ref[...], preferred_element_type=F32)
    if with_vres:
        vl = _shift_mix(hv_ref[...], hvp_ref[7:8, :], vmu_ref[...], first)
        logit = vv0_ref[...] + jnp.dot(vl.astype(BF16), vup_ref[...], preferred_element_type=F32)
        v = v + (vf_ref[...] - v) * _sigmoid(logit)
    k2 = k * (1.0 + (a - 1.0) * ka_ref[...])
    kkr = k * kk_ref[...]
    bd = bd_ref[...]
    kk = kkr * lax.rsqrt(_dot3(kkr * kkr, bd) + 1e-12)
    lw = -jnp.exp(log_w)
    cum = _dot3_left(tri_ref[...], lw)
    cend = _dot3_left(blk_ref[...], lw)
    p_inv = jnp.exp(-cum)
    p_end = jnp.exp(cend - cum)
    beta = kk * a
    at_o[...] = (-kk * jnp.exp(cum - lw)).astype(BF16)
    rt_o[...] = (r * jnp.exp(cum)).astype(BF16)
    bt_o[...] = (beta * p_inv).astype(BF16)
    kt_o[...] = (k2 * p_inv).astype(BF16)
    bh_o[...] = (beta * p_end).astype(BF16)
    kh_o[...] = (k2 * p_end).astype(BF16)
    vb_o[...] = v.astype(BF16)
    pc_o[...] = jnp.exp(cend)
    bonus_o[...] = _dot3(r * k2 * rk_ref[...], bd) * v
    g_o[...] = g
    v_o[...] = v


def _block_ones(n, block, lower_tri=False):
    i = jnp.arange(n)
    m = (i[:, None] // block) == (i[None, :] // block)
    if lower_tri:
        m = m & (i[:, None] >= i[None, :])
    return m.astype(BF16)


def _rwkv_pre(p, mu, w0, a0, wwa, gup, k_k, k_a, r_k, vres=None):
    S, PW = p.shape
    W = RWKV_W
    tm = min(512, S)
    row = lambda i: (i, 0)
    const = lambda i: (0, 0)
    tail = lambda i: (jnp.maximum(i * (tm // 8) - 1, 0), 0)
    vec = lambda a: a.reshape(1, -1)
    args = [p, p, vec(mu), vec(w0), vec(a0), wwa, gup, vec(k_k), vec(k_a), vec(r_k),
            _block_ones(tm, CHUNK, lower_tri=True), _block_ones(tm, CHUNK),
            _block_ones(W, RWKV_HEAD_DIM)]
    specs = [pl.BlockSpec((tm, PW), row), pl.BlockSpec((8, PW), tail),
             pl.BlockSpec((1, PW), const), pl.BlockSpec((1, W), const), pl.BlockSpec((1, W), const),
             pl.BlockSpec(wwa.shape, const), pl.BlockSpec(gup.shape, const),
             pl.BlockSpec((1, W), const), pl.BlockSpec((1, W), const), pl.BlockSpec((1, W), const),
             pl.BlockSpec((tm, tm), const), pl.BlockSpec((tm, tm), const),
             pl.BlockSpec((W, W), const)]
    if vres is not None:
        hv, vmu, vup, vv0, v_first = vres
        args += [hv, hv, vec(vmu), vup, vec(vv0), v_first]
        specs += [pl.BlockSpec((tm, LANE), row), pl.BlockSpec((8, LANE), tail),
                  pl.BlockSpec((1, LANE), const), pl.BlockSpec(vup.shape, const),
                  pl.BlockSpec((1, W), const), pl.BlockSpec((tm, W), row)]
    return pl.pallas_call(
        functools.partial(_rwkv_pre_kernel, with_vres=vres is not None),
        grid=(S // tm,),
        in_specs=specs,
        out_specs=[pl.BlockSpec((tm, W), row)] * 11,
        out_shape=[jax.ShapeDtypeStruct((S, W), BF16)] * 7 + [jax.ShapeDtypeStruct((S, W), F32)] * 4,
        compiler_params=_params("arbitrary"),
        name="rwkv_pre",
    )(*args)


def _rwkv_scan_kernel(at_ref, rt_ref, bt_ref, kt_ref, bh_ref, kh_ref, v_ref, pc_ref, y_ref, h_sc,
                      *, chunks):
    C = CHUNK
    N = RWKV_HEAD_DIM

    @pl.when(pl.program_id(0) == 0)
    def _():
        h_sc[...] = jnp.zeros(h_sc.shape, F32)

    ti = lax.broadcasted_iota(I32, (C, C), 0)
    tj = lax.broadcasted_iota(I32, (C, C), 1)
    lower_incl = ti >= tj
    lower_strict = ti > tj
    eye = (ti == tj).astype(F32)
    eye_n = (lax.broadcasted_iota(I32, (N, N), 0) == lax.broadcasted_iota(I32, (N, N), 1)).astype(F32)

    def mm(x, y):
        return jnp.dot(x.astype(BF16), y.astype(BF16), preferred_element_type=F32)

    def mm_nt(x, y):
        return lax.dot_general(x.astype(BF16), y.astype(BF16), (((1,), (1,)), ((), ())),
                               preferred_element_type=F32)

    def mm_tn(x, y):
        return lax.dot_general(x.astype(BF16), y.astype(BF16), (((0,), (0,)), ((), ())),
                               preferred_element_type=F32)

    units = [(c, h) for c in range(chunks) for h in range(RWKV_HEADS)]
    tile = lambda ref, u: ref[u[0] * C:(u[0] + 1) * C, u[1] * N:(u[1] + 1) * N]
    each = lambda fn: {u: fn(u) for u in units}

    At, Rt, Bt, Kt = (each(lambda u, r=ref: tile(r, u)) for ref in (at_ref, rt_ref, bt_ref, kt_ref))
    Bh, Kh, V = (each(lambda u, r=ref: tile(r, u)) for ref in (bh_ref, kh_ref, v_ref))
    AR = each(lambda u: jnp.concatenate([At[u], Rt[u]], axis=0))
    Mb = each(lambda u: mm_nt(AR[u], Bt[u]))
    Mk = each(lambda u: mm_nt(AR[u], Kt[u]))
    Lab = each(lambda u: jnp.where(lower_strict, Mb[u][0:C], 0.0))
    Mrb = each(lambda u: jnp.where(lower_incl, Mb[u][C:2 * C], 0.0))
    Lak = each(lambda u: jnp.where(lower_strict, Mk[u][0:C], 0.0))
    Mrk = each(lambda u: jnp.where(lower_incl, Mk[u][C:2 * C], 0.0))
    T = each(lambda u: eye + Lab[u])
    Lp = Lab
    span = 2
    while span < C:
        Lp = each(lambda u, Lp=Lp: mm(Lp[u], Lp[u]))
        T = each(lambda u, T=T, Lp=Lp: T[u] + mm(Lp[u], T[u]))
        span *= 2
    W1 = each(lambda u: mm(Lak[u], V[u]))
    A2 = each(lambda u: mm(T[u], At[u]))
    U0 = each(lambda u: mm(T[u], W1[u]))
    R2 = each(lambda u: Rt[u].astype(F32) + mm(Mrb[u], A2[u]))
    Y0 = each(lambda u: mm(Mrb[u], U0[u]) + mm(Mrk[u], V[u]))
    G = each(lambda u: eye_n * tile(pc_ref, u)[0:1, :] + mm_tn(Bh[u], A2[u]))
    H0 = each(lambda u: mm_tn(Bh[u], U0[u]) + mm_tn(Kh[u], V[u]))

    H = {h: h_sc[h] for h in range(RWKV_HEADS)}
    for c in range(chunks):
        ys = []
        for h in range(RWKV_HEADS):
            u = (c, h)
            ys.append(mm(R2[u], H[h]) + Y0[u])
            H[h] = mm(G[u], H[h]) + H0[u]
        y_ref[c * C:(c + 1) * C, :] = jnp.concatenate(ys, axis=1)
    for h in range(RWKV_HEADS):
        h_sc[h] = H[h]


def _rwkv_scan(at, rt, bt, kt, bh, kh, vb, pc):
    S, W = at.shape
    chunks = 2 if S % (2 * CHUNK) == 0 else 1
    tb = chunks * CHUNK
    seq = pl.BlockSpec((tb, W), lambda i: (i, 0))
    return pl.pallas_call(
        functools.partial(_rwkv_scan_kernel, chunks=chunks),
        grid=(S // tb,),
        in_specs=[seq] * 8,
        out_specs=seq,
        out_shape=jax.ShapeDtypeStruct((S, W), F32),
        scratch_shapes=[pltpu.VMEM((RWKV_HEADS, RWKV_HEAD_DIM, RWKV_HEAD_DIM), F32)],
        compiler_params=_params("arbitrary"),
        name="rwkv_scan",
    )(at, rt, bt, kt, bh, kh, vb, pc)


def _merge_kernel(x_ref, of_ref, od_ref, y_ref, bonus_ref, g_ref, lng_ref, lnb_ref, bd_ref,
                  gp_ref, bg_ref, pf_ref, pd_ref, pr_ref, wo_ref, gn_ref, xo_ref, h_ref):
    D = D_MODEL
    bd = bd_ref[...]
    inv_n = 1.0 / RWKV_HEAD_DIM
    y = y_ref[...]
    yc = y - _dot3(y, bd) * inv_n
    var = _dot3(yc * yc, bd) * inv_n
    yn = yc * lax.rsqrt(var + RWKV_LN_EPS) * lng_ref[...] + lnb_ref[...]
    o_rwkv = ((yn + bonus_ref[...]) * g_ref[...]).astype(BF16)
    gates = _sigmoid(gp_ref[...] + bg_ref[...])
    merged = (gates[:, 0:D] * jnp.dot(of_ref[...], pf_ref[...], preferred_element_type=F32)
              + gates[:, D:2 * D] * jnp.dot(od_ref[...], pd_ref[...], preferred_element_type=F32)
              + gates[:, 2 * D:3 * D] * jnp.dot(o_rwkv, pr_ref[...], preferred_element_type=F32))
    x = x_ref[...] + jnp.dot(merged.astype(BF16), wo_ref[...], preferred_element_type=F32)
    xo_ref[...] = x
    h_ref[...] = _rms(x, gn_ref[...]).astype(h_ref.dtype)


def _merge(x, o_fox, o_dsa, y_rwkv, bonus, g_rwkv, ln_g, ln_b, gate_p, b_gate, p_fox, p_dsa, p_rwkv,
           w_out, g_ffn):
    S, D = x.shape
    W = RWKV_W
    tm = min(512, S)
    row = lambda i: (i, 0)
    const = lambda i: (0, 0)
    return pl.pallas_call(
        _merge_kernel,
        grid=(S // tm,),
        in_specs=[pl.BlockSpec((tm, D), row), pl.BlockSpec((tm, FOX_W), row),
                  pl.BlockSpec((tm, DSA_W), row), pl.BlockSpec((tm, W), row),
                  pl.BlockSpec((tm, W), row), pl.BlockSpec((tm, W), row),
                  pl.BlockSpec((1, W), const), pl.BlockSpec((1, W), const),
                  pl.BlockSpec((W, W), const),
                  pl.BlockSpec((tm, 3 * D), row), pl.BlockSpec((1, 3 * D), const),
                  pl.BlockSpec((FOX_W, D), const), pl.BlockSpec((DSA_W, D), const),
                  pl.BlockSpec((W, D), const), pl.BlockSpec((D, D), const),
                  pl.BlockSpec((1, D), const)],
        out_specs=[pl.BlockSpec((tm, D), row), pl.BlockSpec((tm, D), row)],
        out_shape=[jax.ShapeDtypeStruct((S, D), F32), jax.ShapeDtypeStruct((S, D), BF16)],
        compiler_params=_params("arbitrary"),
        name="merge",
    )(x, o_fox, o_dsa, y_rwkv, bonus, g_rwkv, ln_g.reshape(1, W), ln_b.reshape(1, W),
      _block_ones(W, RWKV_HEAD_DIM), gate_p, b_gate.reshape(1, -1), p_fox, p_dsa, p_rwkv, w_out,
      g_ffn.reshape(1, D))


def _swiglu_tile(h, wg, wu):
    gate = jnp.dot(h, wg, preferred_element_type=F32)
    up = jnp.dot(h, wu, preferred_element_type=F32)
    return gate * _sigmoid(gate) * up


def _ffn_kernel(x_ref, h_ref, wg_ref, wu_ref, wd_ref, gf_ref, o_ref, acc_sc, *, final_norm):
    f = pl.program_id(1)

    @pl.when(f == 0)
    def _():
        acc_sc[...] = x_ref[...]

    act = _swiglu_tile(h_ref[...], wg_ref[...], wu_ref[...])
    acc_sc[...] += jnp.dot(act.astype(BF16), wd_ref[...], preferred_element_type=F32)

    @pl.when(f == pl.num_programs(1) - 1)
    def _():
        y = acc_sc[...]
        o_ref[...] = _rms(y, gf_ref[...]) if final_norm else y


def _ffn(x, h, wg, wu, wd, g_final, final_norm):
    S, D = x.shape
    Fd = wg.shape[1]
    tm = min(512, S)
    tf = _pick_tile(Fd, 1408)
    return pl.pallas_call(
        functools.partial(_ffn_kernel, final_norm=final_norm),
        grid=(S // tm, Fd // tf),
        in_specs=[pl.BlockSpec((tm, D), lambda i, f: (i, 0)),
                  pl.BlockSpec((tm, D), lambda i, f: (i, 0)),
                  pl.BlockSpec((D, tf), lambda i, f: (0, f)),
                  pl.BlockSpec((D, tf), lambda i, f: (0, f)),
                  pl.BlockSpec((tf, D), lambda i, f: (f, 0)),
                  pl.BlockSpec((1, D), lambda i, f: (0, 0))],
        out_specs=pl.BlockSpec((tm, D), lambda i, f: (i, 0)),
        out_shape=jax.ShapeDtypeStruct((S, D), F32),
        scratch_shapes=[pltpu.VMEM((tm, D), F32)],
        compiler_params=_params("arbitrary", "arbitrary"),
        name="ffn",
    )(x, h, wg, wu, wd, g_final.reshape(1, D))


def _moe_kernel(x_ref, h_ref, rw_ref, rb_ref, wg_ref, wu_ref, wd_ref, gf_ref, o_ref,
                acc_sc, gate_sc, *, final_norm):
    e = pl.program_id(1)
    f = pl.program_id(2)
    lane = lax.broadcasted_iota(I32, gate_sc.shape, 1)

    @pl.when((e == 0) & (f == 0))
    def _():
        acc_sc[...] = x_ref[...]
        logits = jnp.dot(h_ref[...], rw_ref[...], preferred_element_type=F32) + rb_ref[...]
        logits = jnp.where(lane < N_EXPERTS, logits, -jnp.inf)
        v1 = jnp.max(logits, axis=-1, keepdims=True)
        i1 = jnp.min(jnp.where(logits == v1, lane, LANE), axis=-1, keepdims=True)
        rest = jnp.where(lane == i1, -jnp.inf, logits)
        v2 = jnp.max(rest, axis=-1, keepdims=True)
        i2 = jnp.min(jnp.where(rest == v2, lane, LANE), axis=-1, keepdims=True)
        e2 = jnp.exp(v2 - v1)
        p1 = 1.0 / (1.0 + e2)
        gate_sc[...] = jnp.where(lane == i1, p1, 0.0) + jnp.where(lane == i2, e2 * p1, 0.0)

    gate = jnp.sum(jnp.where(lane == e, gate_sc[...], 0.0), axis=-1, keepdims=True)
    act = _swiglu_tile(h_ref[...], wg_ref[0], wu_ref[0]) * gate
    acc_sc[...] += jnp.dot(act.astype(BF16), wd_ref[0], preferred_element_type=F32)

    @pl.when((e == pl.num_programs(1) - 1) & (f == pl.num_programs(2) - 1))
    def _():
        y = acc_sc[...]
        o_ref[...] = _rms(y, gf_ref[...]) if final_norm else y


def _moe(x, h, rw, rb, wg, wu, wd, g_final, final_norm):
    S, D = x.shape
    E, _, Fe = wg.shape
    tm = min(1024, S)
    tf = _pick_tile(Fe, 896)
    return pl.pallas_call(
        functools.partial(_moe_kernel, final_norm=final_norm),
        grid=(S // tm, E, Fe // tf),
        in_specs=[pl.BlockSpec((tm, D), lambda i, e, f: (i, 0)),
                  pl.BlockSpec((tm, D), lambda i, e, f: (i, 0)),
                  pl.BlockSpec((D, LANE), lambda i, e, f: (0, 0)),
                  pl.BlockSpec((1, LANE), lambda i, e, f: (0, 0)),
                  pl.BlockSpec((1, D, tf), lambda i, e, f: (e, 0, f)),
                  pl.BlockSpec((1, D, tf), lambda i, e, f: (e, 0, f)),
                  pl.BlockSpec((1, tf, D), lambda i, e, f: (e, f, 0)),
                  pl.BlockSpec((1, D), lambda i, e, f: (0, 0))],
        out_specs=pl.BlockSpec((tm, D), lambda i, e, f: (i, 0)),
        out_shape=jax.ShapeDtypeStruct((S, D), F32),
        scratch_shapes=[pltpu.VMEM((tm, D), F32), pltpu.VMEM((tm, LANE), F32)],
        compiler_params=_params("arbitrary", "arbitrary", "arbitrary"),
        name="moe",
    )(x, h, rw, rb, wg, wu, wd, g_final.reshape(1, D))


def _cast_kernel(x_ref, o_ref):
    o_ref[...] = x_ref[...].astype(o_ref.dtype)


def _to_bf16(w):
    E, A, B = w.shape
    ta = 256
    return pl.pallas_call(
        _cast_kernel,
        grid=(E, A // ta),
        in_specs=[pl.BlockSpec((1, ta, B), lambda e, a: (e, a, 0))],
        out_specs=pl.BlockSpec((1, ta, B), lambda e, a: (e, a, 0)),
        out_shape=jax.ShapeDtypeStruct(w.shape, BF16),
        compiler_params=_params("arbitrary", "arbitrary"),
        name="to_bf16",
    )(w)


def _pad_cols(a, n):
    return jnp.pad(a, ((0, 0), (0, n - a.shape[1])))


def kernel(x, w_in, b_gate, g_mix, fox_f_bias, dsa_kv_norm, dsa_w_uk, dsa_w_uv, rwkv_mu, rwkv_w0, rwkv_w_up, rwkv_a0, rwkv_a_up, rwkv_g_up, rwkv_k_k, rwkv_k_a, rwkv_r_k, rwkv_ln_g, rwkv_ln_b, vres_down, vres_mu, vres_up, vres_v0, p_fox, p_dsa, p_rwkv, w_out, g_ffn, ffn_w_gate, ffn_w_up, ffn_w_down, router_w, router_b, moe_w_gate, moe_w_up, moe_w_down, g_final):
    B, S, D = x.shape
    assert B == 1 and D == D_MODEL and S % LANE == 0
    depth = w_in.shape[0]
    topk = min(IDX_TOPK, S // 4)
    bf = lambda a: a.astype(BF16)
    xs = x[0]
    v_first = None
    for l in range(depth):
        wl = w_in[l]
        w_fox, w_dsa, w_rwkv, w_gate = (wl[:, :FOX_IN], wl[:, FOX_IN:FOX_IN + DSA_IN],
                                        wl[:, FOX_IN + DSA_IN:FOX_IN + DSA_IN + RWKV_IN],
                                        wl[:, FOX_IN + DSA_IN + RWKV_IN:])
        o1 = DSA_W + DSA_KV_RANK + IDX_W
        w_att = jnp.concatenate([_pad_cols(w_fox, 3 * FOX_W + LANE), w_dsa[:, :o1],
                                 _pad_cols(w_dsa[:, o1:], LANE)], axis=1)
        if l > 0:
            w_rwkv = jnp.concatenate([w_rwkv, _pad_cols(vres_down[l - 1], LANE)], axis=1)
        pa = _rms_proj(xs, g_mix[l], bf(w_att))
        pr = _rms_proj(xs, g_mix[l], bf(w_rwkv))
        gate_p = _rms_proj(xs, g_mix[l], bf(w_gate))

        c0 = 3 * FOX_W
        fl = pa[:, c0:c0 + FOX_HEADS].T.reshape(FOX_HEADS, S // LANE, LANE)
        F = _fox_cumsum(fl, fox_f_bias[l]).reshape(FOX_HEADS, S)
        Fh, Fm, Fl = _split3(F)
        one = jnp.ones((FOX_HEADS, S), BF16)
        zpad = jnp.zeros((FOX_HEADS, FOX_KA - FOX_HEAD_DIM - 6, S), BF16)
        qT = bf(pa[:, 0:FOX_W].T.reshape(FOX_HEADS, FOX_HEAD_DIM, S) * (FOX_HEAD_DIM ** -0.5))
        kT = bf(pa[:, FOX_W:2 * FOX_W].T.reshape(FOX_HEADS, FOX_HEAD_DIM, S))
        stack = lambda *rows: jnp.stack(rows, axis=1)
        qaT = jnp.concatenate([qT, stack(Fh, Fm, Fl, one, one, one), zpad], axis=1)
        kaT = jnp.concatenate([kT, stack(one, one, one, -Fh, -Fm, -Fl), zpad], axis=1)
        fvT = bf(pa[:, 2 * FOX_W:3 * FOX_W].T.reshape(FOX_HEADS, FOX_HEAD_DIM, S))
        o_fox = _fox_attention(qaT, kaT.transpose(0, 2, 1), fvT).reshape(FOX_W, S).T

        t = min(DSA_T, S)
        nb = S // t
        c1 = c0 + LANE
        dqT = bf(pa[:, c1:c1 + DSA_W].T.reshape(DSA_HEADS, DSA_HEAD_DIM, nb, t).transpose(2, 0, 1, 3))
        c2 = c1 + DSA_W
        ckv = _rmsnorm(pa[:, c2:c2 + DSA_KV_RANK], dsa_kv_norm[l], BF16)
        c3 = c2 + DSA_KV_RANK
        qiT = bf(pa[:, c3:c3 + IDX_W].T.reshape(IDX_HEADS, IDX_HEAD_DIM, nb, t).transpose(2, 1, 0, 3)
                 .reshape(nb, IDX_HEAD_DIM, IDX_HEADS * t))
        c4 = c3 + IDX_W
        ki = bf(pa[:, c4:c4 + IDX_HEAD_DIM])
        c5 = c4 + IDX_HEAD_DIM
        wi = pa[:, c5:c5 + IDX_HEADS].T.reshape(IDX_HEADS, nb, t).transpose(1, 0, 2) * (IDX_W ** -0.5)
        o_dsa = _dsa_attention(qiT, wi, dqT, bf(dsa_w_uk[l].transpose(0, 2, 1)),
                               bf(dsa_w_uv[l].transpose(0, 2, 1)), ki, ckv, ckv.T, topk)
        o_dsa = o_dsa.transpose(1, 2, 0, 3).reshape(DSA_W, S).T

        zw = jnp.zeros((RWKV_W_LORA, RWKV_W), F32)
        wwa = bf(jnp.concatenate([jnp.concatenate([rwkv_w_up[l], zw], axis=1),
                                  jnp.concatenate([zw, rwkv_a_up[l]], axis=1)], axis=0))
        vres = None
        if l > 0:
            vup = jnp.pad(vres_up[l - 1], ((0, LANE - RWKV_V_LORA), (0, 0)))
            vres = (pr[:, RWKV_IN:], jnp.pad(vres_mu[l - 1], (0, LANE - RWKV_V_LORA)),
                    bf(vup), vres_v0[l - 1], v_first)
        *scan_ops, bonus, g_rwkv, v = _rwkv_pre(
            pr[:, :RWKV_IN], rwkv_mu[l], rwkv_w0[l], rwkv_a0[l], wwa, bf(rwkv_g_up[l]),
            rwkv_k_k[l], rwkv_k_a[l], rwkv_r_k[l], vres)
        if l == 0:
            v_first = v
        y_rwkv = _rwkv_scan(*scan_ops)

        xs, h2 = _merge(xs, o_fox, o_dsa, y_rwkv, bonus, g_rwkv, rwkv_ln_g[l], rwkv_ln_b[l],
                        gate_p, b_gate[l], bf(p_fox[l]), bf(p_dsa[l]), bf(p_rwkv[l]), bf(w_out[l]),
                        g_ffn[l])

        last = l == depth - 1
        if l % 2 == 0:
            xs = _ffn(xs, h2, bf(ffn_w_gate[l // 2]), bf(ffn_w_up[l // 2]), bf(ffn_w_down[l // 2]),
                      g_final, last)
        else:
            rw = bf(_pad_cols(router_w[l // 2], LANE))
            rb = _pad_cols(router_b[l // 2].reshape(1, -1), LANE)
            xs = _moe(xs, h2, rw, rb, _to_bf16(moe_w_gate[l // 2]), _to_bf16(moe_w_up[l // 2]),
                      _to_bf16(moe_w_down[l // 2]), g_final, last)
    return xs[None]
```

```python
import functools

import jax
import jax.numpy as jnp
from jax import lax
from jax.experimental import pallas as pl
from jax.experimental.pallas import tpu as pltpu

F32 = jnp.float32
BF16 = jnp.bfloat16
I32 = jnp.int32

D_MODEL = 1024
CHUNK = 64
RMS_EPS = 1e-6
FOX_HEADS, FOX_HEAD_DIM = 4, 64
DSA_HEADS, DSA_HEAD_DIM, DSA_KV_RANK = 4, 64, 128
IDX_HEADS, IDX_HEAD_DIM, IDX_TOPK = 8, 32, 256
RWKV_HEADS, RWKV_HEAD_DIM = 8, 64
RWKV_W_LORA, RWKV_A_LORA, RWKV_V_LORA, RWKV_G_LORA = 64, 64, 32, 128
RWKV_LN_EPS = 64e-5
FOX_W = FOX_HEADS * FOX_HEAD_DIM
DSA_W = DSA_HEADS * DSA_HEAD_DIM
RWKV_W = RWKV_HEADS * RWKV_HEAD_DIM
IDX_W = IDX_HEADS * IDX_HEAD_DIM
N_EXPERTS = 8
FOX_IN = 3 * FOX_W + FOX_HEADS
DSA_IN = DSA_W + DSA_KV_RANK + IDX_W + IDX_HEAD_DIM + IDX_HEADS
RWKV_IN = 3 * RWKV_W + RWKV_W_LORA + RWKV_A_LORA + RWKV_G_LORA

LANE = 128
VMEM_LIMIT = 52 * 1024 * 1024
NEG = -1e30
INT_MIN = -(2 ** 31)
HI = lax.Precision.HIGHEST


def _params(*sem):
    return pltpu.CompilerParams(dimension_semantics=sem, vmem_limit_bytes=VMEM_LIMIT)


def _pick_tile(n, cap):
    best = LANE
    for t in range(LANE, min(n, cap) + 1, LANE):
        if n % t == 0:
            best = t
    return best


def _softplus(x):
    return jnp.maximum(x, 0.0) + jnp.log1p(jnp.exp(-jnp.abs(x)))


def _sigmoid(x):
    return 1.0 / (1.0 + jnp.exp(-x))


def _rms(x, g):
    return x * lax.rsqrt(jnp.mean(x * x, axis=-1, keepdims=True) + RMS_EPS) * g


def _rms_proj_kernel(x_ref, g_ref, w_ref, o_ref):
    h = _rms(x_ref[...], g_ref[...])
    o_ref[...] = jnp.dot(h.astype(BF16), w_ref[...], preferred_element_type=F32)


def _rms_proj(x, g, w):
    S, D = x.shape
    N = w.shape[1]
    tm = min(512, S)
    tn = _pick_tile(N, 2304)
    return pl.pallas_call(
        _rms_proj_kernel,
        grid=(N // tn, S // tm),
        in_specs=[pl.BlockSpec((tm, D), lambda j, i: (i, 0)),
                  pl.BlockSpec((1, D), lambda j, i: (0, 0)),
                  pl.BlockSpec((D, tn), lambda j, i: (0, j))],
        out_specs=pl.BlockSpec((tm, tn), lambda j, i: (i, j)),
        out_shape=jax.ShapeDtypeStruct((S, N), F32),
        compiler_params=_params("arbitrary", "arbitrary"),
        name="rms_proj",
    )(x, g.reshape(1, D), w)


def _rmsnorm_kernel(x_ref, g_ref, o_ref):
    o_ref[...] = _rms(x_ref[...], g_ref[...]).astype(o_ref.dtype)


def _rmsnorm(x, g, dtype):
    S, D = x.shape
    tm = min(2048, S)
    return pl.pallas_call(
        _rmsnorm_kernel,
        grid=(S // tm,),
        in_specs=[pl.BlockSpec((tm, D), lambda i: (i, 0)),
                  pl.BlockSpec((1, D), lambda i: (0, 0))],
        out_specs=pl.BlockSpec((tm, D), lambda i: (i, 0)),
        out_shape=jax.ShapeDtypeStruct((S, D), dtype),
        compiler_params=_params("arbitrary"),
        name="rmsnorm",
    )(x, g.reshape(1, D))


def _fox_cumsum_kernel(fl_ref, b_ref, o_ref):
    H, R, _ = fl_ref.shape
    upper = (lax.broadcasted_iota(I32, (LANE, LANE), 0)
             <= lax.broadcasted_iota(I32, (LANE, LANE), 1)).astype(F32)
    strict_lower = (lax.broadcasted_iota(I32, (R, R), 0)
                    > lax.broadcasted_iota(I32, (R, R), 1)).astype(F32)
    for h in range(H):
        log_f = -_softplus(-(fl_ref[h] + b_ref[h]))
        within = jnp.dot(log_f, upper, preferred_element_type=F32, precision=HI)
        row_tot = jnp.broadcast_to(within[:, LANE - 1:LANE], (R, LANE))
        before = jnp.dot(strict_lower, row_tot, preferred_element_type=F32, precision=HI)
        o_ref[h] = within + before


def _fox_cumsum(fl, bias):
    H, R, _ = fl.shape
    return pl.pallas_call(
        _fox_cumsum_kernel,
        out_shape=jax.ShapeDtypeStruct((H, R, LANE), F32),
        compiler_params=pltpu.CompilerParams(vmem_limit_bytes=VMEM_LIMIT),
        name="fox_cumsum",
    )(fl, jnp.broadcast_to(bias.reshape(H, 1, 1), (H, 1, LANE)))


FOX_KA = 128


def _fox_kernel(qa_ref, ka_ref, vT_ref, o_ref, lga_sc, lgb_sc, m_sc, l_sc, acc_sc, *, t, tc, tg):
    i = pl.program_id(1)
    hb = qa_ref.shape[0]
    last = (i * t) // tg
    chains = [(h, q0) for h in range(hb) for q0 in range(0, t, tc)]

    def logits_into(dst_ref, g):
        off = pl.multiple_of(jnp.minimum(g, last) * tg, tg)
        for n, (h, q0) in enumerate(chains):
            dst_ref[n] = jnp.dot(ka_ref[h, pl.ds(off, tg), :], qa_ref[h, :, q0:q0 + tc],
                                 preferred_element_type=F32)

    def update(lg_ref, g, masked):
        off = pl.multiple_of(g * tg, tg)
        for n, (h, q0) in enumerate(chains):
            s = lg_ref[n]
            if masked:
                kpos = off + lax.broadcasted_iota(I32, (tg, tc), 0)
                qpos = i * t + q0 + lax.broadcasted_iota(I32, (tg, tc), 1)
                s = jnp.where(kpos <= qpos, s, NEG)
            m_prev = m_sc[n]
            m_new = jnp.maximum(m_prev, jnp.max(s, axis=0, keepdims=True))
            alpha = jnp.exp(m_prev - m_new)
            p = jnp.exp(s - m_new)
            l_sc[n] = alpha * l_sc[n] + jnp.sum(p, axis=0, keepdims=True)
            acc_sc[n] = alpha * acc_sc[n] + jnp.dot(vT_ref[h, :, pl.ds(off, tg)], p.astype(BF16),
                                                    preferred_element_type=F32)
            m_sc[n] = m_new

    m_sc[...] = jnp.full(m_sc.shape, NEG, F32)
    l_sc[...] = jnp.zeros(l_sc.shape, F32)
    acc_sc[...] = jnp.zeros(acc_sc.shape, F32)
    logits_into(lga_sc, 0)

    def pair(j, c):
        logits_into(lgb_sc, 2 * j + 1)
        update(lga_sc, 2 * j, False)
        logits_into(lga_sc, 2 * j + 2)
        update(lgb_sc, 2 * j + 1, False)
        return c

    lax.fori_loop(0, last // 2, pair, 0)
    tail = 2 * (last // 2)
    logits_into(lgb_sc, tail + 1)
    update(lga_sc, tail, True)

    @pl.when(tail + 1 <= last)
    def _():
        update(lgb_sc, tail + 1, True)

    for n, (h, q0) in enumerate(chains):
        o_ref[h, :, q0:q0 + tc] = (acc_sc[n] / l_sc[n]).astype(o_ref.dtype)


def _fox_attention(qaT, ka, vT):
    H, KA, S = qaT.shape
    Dh = vT.shape[1]
    t = min(512, S)
    tc = min(256, t)
    tg = min(1024, S)
    hb = 2
    nc = hb * (t // tc)
    return pl.pallas_call(
        functools.partial(_fox_kernel, t=t, tc=tc, tg=tg),
        grid=(H // hb, S // t),
        in_specs=[pl.BlockSpec((hb, KA, t), lambda h, i: (h, 0, i)),
                  pl.BlockSpec((hb, S, KA), lambda h, i: (h, 0, 0)),
                  pl.BlockSpec((hb, Dh, S), lambda h, i: (h, 0, 0))],
        out_specs=pl.BlockSpec((hb, Dh, t), lambda h, i: (h, 0, i)),
        out_shape=jax.ShapeDtypeStruct((H, Dh, S), BF16),
        scratch_shapes=[pltpu.VMEM((nc, tg, tc), F32), pltpu.VMEM((nc, tg, tc), F32),
                        pltpu.VMEM((nc, 1, tc), F32), pltpu.VMEM((nc, 1, tc), F32),
                        pltpu.VMEM((nc, Dh, tc), F32)],
        compiler_params=_params("arbitrary", "arbitrary"),
        name="fox_attention",
    )(qaT, ka, vT)


DSA_T = 128


def _dsa_kernel(qiT_ref, w_ref, qT_ref, wukT_ref, wuvT_ref, ki_ref, c_ref, cT_ref, o_ref,
                keys_sc, qlat_sc, m_sc, l_sc, acc_sc, lga_sc, lgb_sc, rela_sc, relb_sc,
                *, t, tg, topk):
    i = pl.program_id(0)
    last = (i * t) // tg
    ng = last + 1

    qiT = qiT_ref[0]
    w = w_ref[0]

    def rel_into(dst_ref, g):
        off = pl.multiple_of(jnp.minimum(g, last) * tg, tg)
        dst_ref[...] = jnp.dot(ki_ref[pl.ds(off, tg), :], qiT, preferred_element_type=F32)

    def score_keys(rel_ref, g, masked):
        off = pl.multiple_of(g * tg, tg)
        sc = jnp.maximum(rel_ref[:, 0:t], 0.0) * w[0:1, :]
        for h in range(1, IDX_HEADS):
            sc = sc + jnp.maximum(rel_ref[:, h * t:(h + 1) * t], 0.0) * w[h:h + 1, :]
        bits = pltpu.bitcast(sc, I32)
        key = bits ^ ((bits >> 31) & 0x7FFFFFFF)
        key = jnp.where(sc == 0.0, 0, key)
        if masked:
            kchunk = (off + lax.broadcasted_iota(I32, (tg, t), 0)) // CHUNK
            qchunk = (i * t + lax.broadcasted_iota(I32, (tg, t), 1)) // CHUNK
            key = jnp.where(kchunk <= qchunk, key, INT_MIN)
        keys_sc[pl.ds(off, tg), :] = key

    def fill_pair(j, c):
        rel_into(relb_sc, 2 * j + 1)
        score_keys(rela_sc, 2 * j, False)
        rel_into(rela_sc, 2 * j + 2)
        score_keys(relb_sc, 2 * j + 1, False)
        return c

    rel_into(rela_sc, 0)
    lax.fori_loop(0, last // 2, fill_pair, 0)
    tail = 2 * (last // 2)
    rel_into(relb_sc, tail + 1)
    score_keys(rela_sc, tail, True)

    @pl.when(tail + 1 <= last)
    def _():
        score_keys(relb_sc, tail + 1, True)

    def count_ge(cand):
        cb = jnp.broadcast_to(cand, (t, t))

        def body(g, acc):
            off = pl.multiple_of(g * tg, tg)
            for k in range(tg // t):
                acc = acc + (keys_sc[pl.ds(off + k * t, t), :] >= cb).astype(F32)
            return acc

        acc = lax.fori_loop(0, ng, body, jnp.zeros((t, t), F32))
        return jnp.sum(acc, axis=0, keepdims=True)

    def bisect(s, st):
        thr, n_ge = st
        cand = thr + jnp.left_shift(jnp.int32(1), 31 - s)
        c = count_ge(cand)
        keep = c >= topk
        return jnp.where(keep, cand, thr), jnp.where(keep, c, n_ge)

    thr, n_ge = lax.fori_loop(0, 32, bisect,
                              (jnp.full((1, t), INT_MIN, I32), jnp.zeros((1, t), F32)))
    thr = jnp.maximum(thr, INT_MIN + 1)
    has_ties = jnp.max(jnp.where(n_ge > topk, 1.0, 0.0)) > 0.0
    thr_b = jnp.broadcast_to(thr, (tg, t))

    for h in range(DSA_HEADS):
        ql = jnp.dot(wukT_ref[h], qT_ref[0, h], preferred_element_type=F32)
        qlat_sc[:, h * t:(h + 1) * t] = (ql * (DSA_HEAD_DIM ** -0.5)).astype(BF16)
    m_sc[...] = jnp.full(m_sc.shape, NEG, F32)
    l_sc[...] = jnp.zeros(l_sc.shape, F32)
    acc_sc[...] = jnp.zeros(acc_sc.shape, F32)

    def logits_into(dst_ref, g):
        off = pl.multiple_of(jnp.minimum(g, last) * tg, tg)
        dst_ref[...] = jnp.dot(c_ref[pl.ds(off, tg), :], qlat_sc[...],
                               preferred_element_type=F32)

    def attend(lg_ref, g, seen_eq, with_ties, need):
        valid = g <= last
        off = pl.multiple_of(jnp.minimum(g, last) * tg, tg)
        key = keys_sc[pl.ds(off, tg), :]
        if with_ties:
            eq = (key == thr_b) & valid
            eqf = eq.astype(F32)
            strict_lower = (lax.broadcasted_iota(I32, (tg, tg), 0)
                            > lax.broadcasted_iota(I32, (tg, tg), 1)).astype(BF16)
            rank = jnp.dot(strict_lower, eqf.astype(BF16), preferred_element_type=F32) + seen_eq
            sel = (key > thr_b) | (eq & (rank < need))
            seen_eq = seen_eq + jnp.sum(eqf, axis=0, keepdims=True)
        else:
            sel = key >= thr_b
        bias = jnp.where(sel & valid, 0.0, NEG)
        cT_tile = cT_ref[:, pl.ds(off, tg)]
        for h in range(DSA_HEADS):
            s = lg_ref[:, h * t:(h + 1) * t] + bias
            m_prev = m_sc[h]
            m_new = jnp.maximum(m_prev, jnp.max(s, axis=0, keepdims=True))
            alpha = jnp.exp(m_prev - m_new)
            p = jnp.exp(s - m_new)
            l_sc[h] = alpha * l_sc[h] + jnp.sum(p, axis=0, keepdims=True)
            acc_sc[h] = alpha * acc_sc[h] + jnp.dot(cT_tile, p.astype(BF16),
                                                    preferred_element_type=F32)
            m_sc[h] = m_new
        return seen_eq

    def sweep(with_ties, need=None):
        logits_into(lga_sc, 0)

        def pair(j, seen_eq):
            g0 = 2 * j
            logits_into(lgb_sc, g0 + 1)
            seen_eq = attend(lga_sc, g0, seen_eq, with_ties, need)
            logits_into(lga_sc, g0 + 2)
            return attend(lgb_sc, g0 + 1, seen_eq, with_ties, need)

        lax.fori_loop(0, last // 2 + 1, pair, jnp.zeros((1, t), F32))

    @pl.when(has_ties)
    def _():
        sweep(True, topk - count_ge(thr + 1))

    @pl.when(jnp.logical_not(has_ties))
    def _():
        sweep(False)

    for h in range(DSA_HEADS):
        o_lat = (acc_sc[h] / l_sc[h]).astype(BF16)
        o_ref[0, h] = jnp.dot(wuvT_ref[h], o_lat, preferred_element_type=F32).astype(o_ref.dtype)


def _dsa_attention(qiT, w, qT, wukT, wuvT, ki, c, cT, topk):
    NB, H, Dh, t = qT.shape
    S = NB * t
    R = DSA_KV_RANK
    tg = min(512, S)
    const2 = lambda i: (0, 0)
    const3 = lambda i: (0, 0, 0)
    return pl.pallas_call(
        functools.partial(_dsa_kernel, t=t, tg=tg, topk=topk),
        grid=(NB,),
        in_specs=[pl.BlockSpec((1, IDX_HEAD_DIM, IDX_HEADS * t), lambda i: (i, 0, 0)),
                  pl.BlockSpec((1, IDX_HEADS, t), lambda i: (i, 0, 0)),
                  pl.BlockSpec((1, H, Dh, t), lambda i: (i, 0, 0, 0)),
                  pl.BlockSpec((H, R, Dh), const3),
                  pl.BlockSpec((H, Dh, R), const3),
                  pl.BlockSpec((S, IDX_HEAD_DIM), const2),
                  pl.BlockSpec((S, R), const2),
                  pl.BlockSpec((R, S), const2)],
        out_specs=pl.BlockSpec((1, H, Dh, t), lambda i: (i, 0, 0, 0)),
        out_shape=jax.ShapeDtypeStruct((NB, H, Dh, t), BF16),
        scratch_shapes=[pltpu.VMEM((S, t), I32), pltpu.VMEM((R, H * t), BF16),
                        pltpu.VMEM((H, 1, t), F32), pltpu.VMEM((H, 1, t), F32),
                        pltpu.VMEM((H, R, t), F32),
                        pltpu.VMEM((tg, H * t), F32), pltpu.VMEM((tg, H * t), F32),
                        pltpu.VMEM((tg, IDX_HEADS * t), F32), pltpu.VMEM((tg, IDX_HEADS * t), F32)],
        compiler_params=_params("arbitrary"),
        name="dsa_attention",
    )(qiT, w, qT, wukT, wuvT, ki, c, cT)


def _shift_mix(p, prev_tail, mu, first_block):
    rows = lax.broadcasted_iota(I32, p.shape, 0)
    tail = jnp.where(first_block, 0.0, prev_tail)
    prev = jnp.where(rows == 0, tail, pltpu.roll(p, 1, axis=0))
    return p + (prev - p) * mu


def _split3(f):
    hi = f.astype(BF16)
    r1 = f - hi.astype(F32)
    mid = r1.astype(BF16)
    lo = (r1 - mid.astype(F32)).astype(BF16)
    return hi, mid, lo


def _dot3(x, w01):
    return sum(jnp.dot(part, w01, preferred_element_type=F32) for part in _split3(x))


def _dot3_left(w01, x):
    return sum(jnp.dot(w01, part, preferred_element_type=F32) for part in _split3(x))


def _rwkv_pre_kernel(*refs, with_vres):
    (p_ref, pp_ref, mu_ref, w0_ref, a0_ref, wwa_ref, gup_ref, kk_ref, ka_ref, rk_ref,
     tri_ref, blk_ref, bd_ref) = refs[:13]
    if with_vres:
        hv_ref, hvp_ref, vmu_ref, vup_ref, vv0_ref, vf_ref = refs[13:19]
    (at_o, rt_o, bt_o, kt_o, bh_o, kh_o, vb_o, pc_o, bonus_o, g_o, v_o) = refs[-11:]
    first = pl.program_id(0) == 0
    W = RWKV_W
    ps = _shift_mix(p_ref[...], pp_ref[7:8, :], mu_ref[...], first)
    r, k, v = ps[:, 0:W], ps[:, W:2 * W], ps[:, 2 * W:3 * W]
    wa = ps[:, 3 * W:3 * W + LANE]
    gl = ps[:, 3 * W + LANE:3 * W + 2 * LANE]
    lane = lax.broadcasted_iota(I32, wa.shape, 1)
    wa = jnp.where(lane < RWKV_W_LORA, jnp.tanh(wa), wa)
    up = jnp.dot(wa.astype(BF16), wwa_ref[...], preferred_element_type=F32)
    log_w = -_softplus(-(w0_ref[...] + up[:, 0:W])) - 0.5
    a = _sigmoid(a0_ref[...] + up[:, W:2 * W])
    g = jnp.dot(_sigmoid(gl).astype(BF16), gup_ref[...], preferred_element_type=F32)
    if with_vres:
        vl = _shift_mix(hv_ref[...], hvp_ref[7:8, :], vmu_ref[...], first)
        logit = vv0_ref[...] + jnp.dot(vl.astype(BF16), vup_ref[...], preferred_element_type=F32)
        v = v + (vf_ref[...] - v) * _sigmoid(logit)
    k2 = k * (1.0 + (a - 1.0) * ka_ref[...])
    kkr = k * kk_ref[...]
    bd = bd_ref[...]
    kk = kkr * lax.rsqrt(_dot3(kkr * kkr, bd) + 1e-12)
    lw = -jnp.exp(log_w)
    cum = _dot3_left(tri_ref[...], lw)
    cend = _dot3_left(blk_ref[...], lw)
    p_inv = jnp.exp(-cum)
    p_end = jnp.exp(cend - cum)
    beta = kk * a
    at_o[...] = (-kk * jnp.exp(cum - lw)).astype(BF16)
    rt_o[...] = (r * jnp.exp(cum)).astype(BF16)
    bt_o[...] = (beta * p_inv).astype(BF16)
    kt_o[...] = (k2 * p_inv).astype(BF16)
    bh_o[...] = (beta * p_end).astype(BF16)
    kh_o[...] = (k2 * p_end).astype(BF16)
    vb_o[...] = v.astype(BF16)
    pc_o[...] = jnp.exp(cend)
    bonus_o[...] = _dot3(r * k2 * rk_ref[...], bd) * v
    g_o[...] = g
    v_o[...] = v


def _block_ones(n, block, lower_tri=False):
    i = jnp.arange(n)
    m = (i[:, None] // block) == (i[None, :] // block)
    if lower_tri:
        m = m & (i[:, None] >= i[None, :])
    return m.astype(BF16)


def _rwkv_pre(p, mu, w0, a0, wwa, gup, k_k, k_a, r_k, vres=None):
    S, PW = p.shape
    W = RWKV_W
    tm = min(512, S)
    row = lambda i: (i, 0)
    const = lambda i: (0, 0)
    tail = lambda i: (jnp.maximum(i * (tm // 8) - 1, 0), 0)
    vec = lambda a: a.reshape(1, -1)
    args = [p, p, vec(mu), vec(w0), vec(a0), wwa, gup, vec(k_k), vec(k_a), vec(r_k),
            _block_ones(tm, CHUNK, lower_tri=True), _block_ones(tm, CHUNK),
            _block_ones(W, RWKV_HEAD_DIM)]
    specs = [pl.BlockSpec((tm, PW), row), pl.BlockSpec((8, PW), tail),
             pl.BlockSpec((1, PW), const), pl.BlockSpec((1, W), const), pl.BlockSpec((1, W), const),
             pl.BlockSpec(wwa.shape, const), pl.BlockSpec(gup.shape, const),
             pl.BlockSpec((1, W), const), pl.BlockSpec((1, W), const), pl.BlockSpec((1, W), const),
             pl.BlockSpec((tm, tm), const), pl.BlockSpec((tm, tm), const),
             pl.BlockSpec((W, W), const)]
    if vres is not None:
        hv, vmu, vup, vv0, v_first = vres
        args += [hv, hv, vec(vmu), vup, vec(vv0), v_first]
        specs += [pl.BlockSpec((tm, LANE), row), pl.BlockSpec((8, LANE), tail),
                  pl.BlockSpec((1, LANE), const), pl.BlockSpec(vup.shape, const),
                  pl.BlockSpec((1, W), const), pl.BlockSpec((tm, W), row)]
    return pl.pallas_call(
        functools.partial(_rwkv_pre_kernel, with_vres=vres is not None),
        grid=(S // tm,),
        in_specs=specs,
        out_specs=[pl.BlockSpec((tm, W), row)] * 11,
        out_shape=[jax.ShapeDtypeStruct((S, W), BF16)] * 7 + [jax.ShapeDtypeStruct((S, W), F32)] * 4,
        compiler_params=_params("arbitrary"),
        name="rwkv_pre",
    )(*args)


def _rwkv_scan_kernel(at_ref, rt_ref, bt_ref, kt_ref, bh_ref, kh_ref, v_ref, pc_ref, y_ref, h_sc,
                      *, chunks):
    C = CHUNK
    N = RWKV_HEAD_DIM

    @pl.when(pl.program_id(0) == 0)
    def _():
        h_sc[...] = jnp.zeros(h_sc.shape, F32)

    ti = lax.broadcasted_iota(I32, (C, C), 0)
    tj = lax.broadcasted_iota(I32, (C, C), 1)
    lower_incl = ti >= tj
    lower_strict = ti > tj
    eye = (ti == tj).astype(F32)
    eye_n = (lax.broadcasted_iota(I32, (N, N), 0) == lax.broadcasted_iota(I32, (N, N), 1)).astype(F32)

    def mm(x, y):
        return jnp.dot(x.astype(BF16), y.astype(BF16), preferred_element_type=F32)

    def mm_nt(x, y):
        return lax.dot_general(x.astype(BF16), y.astype(BF16), (((1,), (1,)), ((), ())),
                               preferred_element_type=F32)

    def mm_tn(x, y):
        return lax.dot_general(x.astype(BF16), y.astype(BF16), (((0,), (0,)), ((), ())),
                               preferred_element_type=F32)

    units = [(c, h) for c in range(chunks) for h in range(RWKV_HEADS)]
    tile = lambda ref, u: ref[u[0] * C:(u[0] + 1) * C, u[1] * N:(u[1] + 1) * N]
    each = lambda fn: {u: fn(u) for u in units}

    At, Rt, Bt, Kt = (each(lambda u, r=ref: tile(r, u)) for ref in (at_ref, rt_ref, bt_ref, kt_ref))
    Bh, Kh, V = (each(lambda u, r=ref: tile(r, u)) for ref in (bh_ref, kh_ref, v_ref))
    AR = each(lambda u: jnp.concatenate([At[u], Rt[u]], axis=0))
    Mb = each(lambda u: mm_nt(AR[u], Bt[u]))
    Mk = each(lambda u: mm_nt(AR[u], Kt[u]))
    Lab = each(lambda u: jnp.where(lower_strict, Mb[u][0:C], 0.0))
    Mrb = each(lambda u: jnp.where(lower_incl, Mb[u][C:2 * C], 0.0))
    Lak = each(lambda u: jnp.where(lower_strict, Mk[u][0:C], 0.0))
    Mrk = each(lambda u: jnp.where(lower_incl, Mk[u][C:2 * C], 0.0))
    T = each(lambda u: eye + Lab[u])
    Lp = Lab
    span = 2
    while span < C:
        Lp = each(lambda u, Lp=Lp: mm(Lp[u], Lp[u]))
        T = each(lambda u, T=T, Lp=Lp: T[u] + mm(Lp[u], T[u]))
        span *= 2
    W1 = each(lambda u: mm(Lak[u], V[u]))
    A2 = each(lambda u: mm(T[u], At[u]))
    U0 = each(lambda u: mm(T[u], W1[u]))
    R2 = each(lambda u: Rt[u].astype(F32) + mm(Mrb[u], A2[u]))
    Y0 = each(lambda u: mm(Mrb[u], U0[u]) + mm(Mrk[u], V[u]))
    G = each(lambda u: eye_n * tile(pc_ref, u)[0:1, :] + mm_tn(Bh[u], A2[u]))
    H0 = each(lambda u: mm_tn(Bh[u], U0[u]) + mm_tn(Kh[u], V[u]))

    H = {h: h_sc[h] for h in range(RWKV_HEADS)}
    for c in range(chunks):
        ys = []
        for h in range(RWKV_HEADS):
            u = (c, h)
            ys.append(mm(R2[u], H[h]) + Y0[u])
            H[h] = mm(G[u], H[h]) + H0[u]
        y_ref[c * C:(c + 1) * C, :] = jnp.concatenate(ys, axis=1)
    for h in range(RWKV_HEADS):
        h_sc[h] = H[h]


def _rwkv_scan(at, rt, bt, kt, bh, kh, vb, pc):
    S, W = at.shape
    chunks = 2 if S % (2 * CHUNK) == 0 else 1
    tb = chunks * CHUNK
    seq = pl.BlockSpec((tb, W), lambda i: (i, 0))
    return pl.pallas_call(
        functools.partial(_rwkv_scan_kernel, chunks=chunks),
        grid=(S // tb,),
        in_specs=[seq] * 8,
        out_specs=seq,
        out_shape=jax.ShapeDtypeStruct((S, W), F32),
        scratch_shapes=[pltpu.VMEM((RWKV_HEADS, RWKV_HEAD_DIM, RWKV_HEAD_DIM), F32)],
        compiler_params=_params("arbitrary"),
        name="rwkv_scan",
    )(at, rt, bt, kt, bh, kh, vb, pc)


def _merge_kernel(x_ref, of_ref, od_ref, y_ref, bonus_ref, g_ref, lng_ref, lnb_ref, bd_ref,
                  gp_ref, bg_ref, pf_ref, pd_ref, pr_ref, wo_ref, gn_ref, xo_ref, h_ref):
    D = D_MODEL
    bd = bd_ref[...]
    inv_n = 1.0 / RWKV_HEAD_DIM
    y = y_ref[...]
    yc = y - _dot3(y, bd) * inv_n
    var = _dot3(yc * yc, bd) * inv_n
    yn = yc * lax.rsqrt(var + RWKV_LN_EPS) * lng_ref[...] + lnb_ref[...]
    o_rwkv = ((yn + bonus_ref[...]) * g_ref[...]).astype(BF16)
    gates = _sigmoid(gp_ref[...] + bg_ref[...])
    merged = (gates[:, 0:D] * jnp.dot(of_ref[...], pf_ref[...], preferred_element_type=F32)
              + gates[:, D:2 * D] * jnp.dot(od_ref[...], pd_ref[...], preferred_element_type=F32)
              + gates[:, 2 * D:3 * D] * jnp.dot(o_rwkv, pr_ref[...], preferred_element_type=F32))
    x = x_ref[...] + jnp.dot(merged.astype(BF16), wo_ref[...], preferred_element_type=F32)
    xo_ref[...] = x
    h_ref[...] = _rms(x, gn_ref[...]).astype(h_ref.dtype)


def _merge(x, o_fox, o_dsa, y_rwkv, bonus, g_rwkv, ln_g, ln_b, gate_p, b_gate, p_fox, p_dsa, p_rwkv,
           w_out, g_ffn):
    S, D = x.shape
    W = RWKV_W
    tm = min(512, S)
    row = lambda i: (i, 0)
    const = lambda i: (0, 0)
    return pl.pallas_call(
        _merge_kernel,
        grid=(S // tm,),
        in_specs=[pl.BlockSpec((tm, D), row), pl.BlockSpec((tm, FOX_W), row),
                  pl.BlockSpec((tm, DSA_W), row), pl.BlockSpec((tm, W), row),
                  pl.BlockSpec((tm, W), row), pl.BlockSpec((tm, W), row),
                  pl.BlockSpec((1, W), const), pl.BlockSpec((1, W), const),
                  pl.BlockSpec((W, W), const),
                  pl.BlockSpec((tm, 3 * D), row), pl.BlockSpec((1, 3 * D), const),
                  pl.BlockSpec((FOX_W, D), const), pl.BlockSpec((DSA_W, D), const),
                  pl.BlockSpec((W, D), const), pl.BlockSpec((D, D), const),
                  pl.BlockSpec((1, D), const)],
        out_specs=[pl.BlockSpec((tm, D), row), pl.BlockSpec((tm, D), row)],
        out_shape=[jax.ShapeDtypeStruct((S, D), F32), jax.ShapeDtypeStruct((S, D), BF16)],
        compiler_params=_params("arbitrary"),
        name="merge",
    )(x, o_fox, o_dsa, y_rwkv, bonus, g_rwkv, ln_g.reshape(1, W), ln_b.reshape(1, W),
      _block_ones(W, RWKV_HEAD_DIM), gate_p, b_gate.reshape(1, -1), p_fox, p_dsa, p_rwkv, w_out,
      g_ffn.reshape(1, D))


def _swiglu_tile(h, wg, wu):
    gate = jnp.dot(h, wg, preferred_element_type=F32)
    up = jnp.dot(h, wu, preferred_element_type=F32)
    return gate * _sigmoid(gate) * up


def _ffn_kernel(x_ref, h_ref, wg_ref, wu_ref, wd_ref, gf_ref, o_ref, acc_sc, *, final_norm):
    f = pl.program_id(1)

    @pl.when(f == 0)
    def _():
        acc_sc[...] = x_ref[...]

    act = _swiglu_tile(h_ref[...], wg_ref[...], wu_ref[...])
    acc_sc[...] += jnp.dot(act.astype(BF16), wd_ref[...], preferred_element_type=F32)

    @pl.when(f == pl.num_programs(1) - 1)
    def _():
        y = acc_sc[...]
        o_ref[...] = _rms(y, gf_ref[...]) if final_norm else y


def _ffn(x, h, wg, wu, wd, g_final, final_norm):
    S, D = x.shape
    Fd = wg.shape[1]
    tm = min(512, S)
    tf = _pick_tile(Fd, 1408)
    return pl.pallas_call(
        functools.partial(_ffn_kernel, final_norm=final_norm),
        grid=(S // tm, Fd // tf),
        in_specs=[pl.BlockSpec((tm, D), lambda i, f: (i, 0)),
                  pl.BlockSpec((tm, D), lambda i, f: (i, 0)),
                  pl.BlockSpec((D, tf), lambda i, f: (0, f)),
                  pl.BlockSpec((D, tf), lambda i, f: (0, f)),
                  pl.BlockSpec((tf, D), lambda i, f: (f, 0)),
                  pl.BlockSpec((1, D), lambda i, f: (0, 0))],
        out_specs=pl.BlockSpec((tm, D), lambda i, f: (i, 0)),
        out_shape=jax.ShapeDtypeStruct((S, D), F32),
        scratch_shapes=[pltpu.VMEM((tm, D), F32)],
        compiler_params=_params("arbitrary", "arbitrary"),
        name="ffn",
    )(x, h, wg, wu, wd, g_final.reshape(1, D))


def _moe_kernel(x_ref, h_ref, rw_ref, rb_ref, wg_ref, wu_ref, wd_ref, gf_ref, o_ref,
                acc_sc, gate_sc, *, final_norm):
    e = pl.program_id(1)
    f = pl.program_id(2)
    lane = lax.broadcasted_iota(I32, gate_sc.shape, 1)

    @pl.when((e == 0) & (f == 0))
    def _():
        acc_sc[...] = x_ref[...]
        logits = jnp.dot(h_ref[...], rw_ref[...], preferred_element_type=F32) + rb_ref[...]
        logits = jnp.where(lane < N_EXPERTS, logits, -jnp.inf)
        v1 = jnp.max(logits, axis=-1, keepdims=True)
        i1 = jnp.min(jnp.where(logits == v1, lane, LANE), axis=-1, keepdims=True)
        rest = jnp.where(lane == i1, -jnp.inf, logits)
        v2 = jnp.max(rest, axis=-1, keepdims=True)
        i2 = jnp.min(jnp.where(rest == v2, lane, LANE), axis=-1, keepdims=True)
        e2 = jnp.exp(v2 - v1)
        p1 = 1.0 / (1.0 + e2)
        gate_sc[...] = jnp.where(lane == i1, p1, 0.0) + jnp.where(lane == i2, e2 * p1, 0.0)

    gate = jnp.sum(jnp.where(lane == e, gate_sc[...], 0.0), axis=-1, keepdims=True)
    act = _swiglu_tile(h_ref[...], wg_ref[0], wu_ref[0]) * gate
    acc_sc[...] += jnp.dot(act.astype(BF16), wd_ref[0], preferred_element_type=F32)

    @pl.when((e == pl.num_programs(1) - 1) & (f == pl.num_programs(2) - 1))
    def _():
        y = acc_sc[...]
        o_ref[...] = _rms(y, gf_ref[...]) if final_norm else y


def _moe(x, h, rw, rb, wg, wu, wd, g_final, final_norm):
    S, D = x.shape
    E, _, Fe = wg.shape
    tm = min(1024, S)
    tf = _pick_tile(Fe, 896)
    return pl.pallas_call(
        functools.partial(_moe_kernel, final_norm=final_norm),
        grid=(S // tm, E, Fe // tf),
        in_specs=[pl.BlockSpec((tm, D), lambda i, e, f: (i, 0)),
                  pl.BlockSpec((tm, D), lambda i, e, f: (i, 0)),
                  pl.BlockSpec((D, LANE), lambda i, e, f: (0, 0)),
                  pl.BlockSpec((1, LANE), lambda i, e, f: (0, 0)),
                  pl.BlockSpec((1, D, tf), lambda i, e, f: (e, 0, f)),
                  pl.BlockSpec((1, D, tf), lambda i, e, f: (e, 0, f)),
                  pl.BlockSpec((1, tf, D), lambda i, e, f: (e, f, 0)),
                  pl.BlockSpec((1, D), lambda i, e, f: (0, 0))],
        out_specs=pl.BlockSpec((tm, D), lambda i, e, f: (i, 0)),
        out_shape=jax.ShapeDtypeStruct((S, D), F32),
        scratch_shapes=[pltpu.VMEM((tm, D), F32), pltpu.VMEM((tm, LANE), F32)],
        compiler_params=_params("arbitrary", "arbitrary", "arbitrary"),
        name="moe",
    )(x, h, rw, rb, wg, wu, wd, g_final.reshape(1, D))


def _cast_kernel(x_ref, o_ref):
    o_ref[...] = x_ref[...].astype(o_ref.dtype)


def _to_bf16(w):
    E, A, B = w.shape
    ta = 256
    return pl.pallas_call(
        _cast_kernel,
        grid=(E, A // ta),
        in_specs=[pl.BlockSpec((1, ta, B), lambda e, a: (e, a, 0))],
        out_specs=pl.BlockSpec((1, ta, B), lambda e, a: (e, a, 0)),
        out_shape=jax.ShapeDtypeStruct(w.shape, BF16),
        compiler_params=_params("arbitrary", "arbitrary"),
        name="to_bf16",
    )(w)


def _pad_cols(a, n):
    return jnp.pad(a, ((0, 0), (0, n - a.shape[1])))


def kernel(x, w_in, b_gate, g_mix, fox_f_bias, dsa_kv_norm, dsa_w_uk, dsa_w_uv, rwkv_mu, rwkv_w0, rwkv_w_up, rwkv_a0, rwkv_a_up, rwkv_g_up, rwkv_k_k, rwkv_k_a, rwkv_r_k, rwkv_ln_g, rwkv_ln_b, vres_down, vres_mu, vres_up, vres_v0, p_fox, p_dsa, p_rwkv, w_out, g_ffn, ffn_w_gate, ffn_w_up, ffn_w_down, router_w, router_b, moe_w_gate, moe_w_up, moe_w_down, g_final):
    B, S, D = x.shape
    assert B == 1 and D == D_MODEL and S % LANE == 0
    depth = w_in.shape[0]
    topk = min(IDX_TOPK, S // 4)
    bf = lambda a: a.astype(BF16)
    xs = x[0]
    v_first = None
    w_in_bf = _to_bf16(w_in)
    for l in range(depth):
        wl = w_in_bf[l]
        w_fox, w_dsa, w_rwkv, w_gate = (wl[:, :FOX_IN], wl[:, FOX_IN:FOX_IN + DSA_IN],
                                        wl[:, FOX_IN + DSA_IN:FOX_IN + DSA_IN + RWKV_IN],
                                        wl[:, FOX_IN + DSA_IN + RWKV_IN:])
        o1 = DSA_W + DSA_KV_RANK + IDX_W
        w_att = jnp.concatenate([_pad_cols(w_fox, 3 * FOX_W + LANE), w_dsa[:, :o1],
                                 _pad_cols(w_dsa[:, o1:], LANE)], axis=1)
        if l > 0:
            w_rwkv = jnp.concatenate([w_rwkv, bf(_pad_cols(vres_down[l - 1], LANE))], axis=1)
        pa = _rms_proj(xs, g_mix[l], w_att)
        pr = _rms_proj(xs, g_mix[l], w_rwkv)
        gate_p = _rms_proj(xs, g_mix[l], w_gate)

        c0 = 3 * FOX_W
        fl = pa[:, c0:c0 + FOX_HEADS].T.reshape(FOX_HEADS, S // LANE, LANE)
        F = _fox_cumsum(fl, fox_f_bias[l]).reshape(FOX_HEADS, S)
        Fh, Fm, Fl = _split3(F)
        one = jnp.ones((FOX_HEADS, S), BF16)
        zero = jnp.zeros((FOX_HEADS, S), BF16)
        zpad = jnp.zeros((FOX_HEADS, FOX_KA - FOX_HEAD_DIM - 16, S), BF16)
        qT = bf(pa[:, 0:FOX_W].T.reshape(FOX_HEADS, FOX_HEAD_DIM, S) * (FOX_HEAD_DIM ** -0.5))
        kT = bf(pa[:, FOX_W:2 * FOX_W].T.reshape(FOX_HEADS, FOX_HEAD_DIM, S))
        stack = lambda *rows: jnp.stack(rows + (zero,) * (16 - len(rows)), axis=1)
        qaT = jnp.concatenate([qT, stack(Fh, Fm, Fl, one, one, one), zpad], axis=1)
        kaT = jnp.concatenate([kT, stack(one, one, one, -Fh, -Fm, -Fl), zpad], axis=1)
        fvT = bf(pa[:, 2 * FOX_W:3 * FOX_W].T.reshape(FOX_HEADS, FOX_HEAD_DIM, S))
        o_fox = _fox_attention(qaT, kaT.transpose(0, 2, 1), fvT).reshape(FOX_W, S).T

        t = min(DSA_T, S)
        nb = S // t
        c1 = c0 + LANE
        dqT = bf(pa[:, c1:c1 + DSA_W].T.reshape(DSA_HEADS, DSA_HEAD_DIM, nb, t).transpose(2, 0, 1, 3))
        c2 = c1 + DSA_W
        ckv = _rmsnorm(pa[:, c2:c2 + DSA_KV_RANK], dsa_kv_norm[l], BF16)
        c3 = c2 + DSA_KV_RANK
        qiT = bf(pa[:, c3:c3 + IDX_W].T.reshape(IDX_HEADS, IDX_HEAD_DIM, nb, t).transpose(2, 1, 0, 3)
                 .reshape(nb, IDX_HEAD_DIM, IDX_HEADS * t))
        c4 = c3 + IDX_W
        ki = bf(pa[:, c4:c4 + IDX_HEAD_DIM])
        c5 = c4 + IDX_HEAD_DIM
        wi = pa[:, c5:c5 + IDX_HEADS].T.reshape(IDX_HEADS, nb, t).transpose(1, 0, 2) * (IDX_W ** -0.5)
        o_dsa = _dsa_attention(qiT, wi, dqT, bf(dsa_w_uk[l].transpose(0, 2, 1)),
                               bf(dsa_w_uv[l].transpose(0, 2, 1)), ki, ckv, ckv.T, topk)
        o_dsa = o_dsa.transpose(1, 2, 0, 3).reshape(DSA_W, S).T

        zw = jnp.zeros((RWKV_W_LORA, RWKV_W), F32)
        wwa = bf(jnp.concatenate([jnp.concatenate([rwkv_w_up[l], zw], axis=1),
                                  jnp.concatenate([zw, rwkv_a_up[l]], axis=1)], axis=0))
        vres = None
        if l > 0:
            vup = jnp.pad(vres_up[l - 1], ((0, LANE - RWKV_V_LORA), (0, 0)))
            vres = (pr[:, RWKV_IN:], jnp.pad(vres_mu[l - 1], (0, LANE - RWKV_V_LORA)),
                    bf(vup), vres_v0[l - 1], v_first)
        *scan_ops, bonus, g_rwkv, v = _rwkv_pre(
            pr[:, :RWKV_IN], rwkv_mu[l], rwkv_w0[l], rwkv_a0[l], wwa, bf(rwkv_g_up[l]),
            rwkv_k_k[l], rwkv_k_a[l], rwkv_r_k[l], vres)
        if l == 0:
            v_first = v
        y_rwkv = _rwkv_scan(*scan_ops)

        xs, h2 = _merge(xs, o_fox, o_dsa, y_rwkv, bonus, g_rwkv, rwkv_ln_g[l], rwkv_ln_b[l],
                        gate_p, b_gate[l], bf(p_fox[l]), bf(p_dsa[l]), bf(p_rwkv[l]), bf(w_out[l]),
                        g_ffn[l])

        last = l == depth - 1
        if l % 2 == 0:
            xs = _ffn(xs, h2, bf(ffn_w_gate[l // 2]), bf(ffn_w_up[l // 2]), bf(ffn_w_down[l // 2]),
                      g_final, last)
        else:
            rw = bf(_pad_cols(router_w[l // 2], LANE))
            rb = _pad_cols(router_b[l // 2].reshape(1, -1), LANE)
            xs = _moe(xs, h2, rw, rb, _to_bf16(moe_w_gate[l // 2]), _to_bf16(moe_w_up[l // 2]),
                      _to_bf16(moe_w_down[l // 2]), g_final, last)
    return xs[None]
```

```python
import functools

import jax
import jax.numpy as jnp
from jax import lax
from jax.experimental import pallas as pl
from jax.experimental.pallas import tpu as pltpu

F32 = jnp.float32
BF16 = jnp.bfloat16
I32 = jnp.int32
I16 = jnp.int16
I16_MIN, I16_MAX = -(2 ** 15), 2 ** 15 - 1

D_MODEL = 1024
CHUNK = 64
RMS_EPS = 1e-6
FOX_HEADS, FOX_HEAD_DIM = 4, 64
DSA_HEADS, DSA_HEAD_DIM, DSA_KV_RANK = 4, 64, 128
IDX_HEADS, IDX_HEAD_DIM, IDX_TOPK = 8, 32, 256
RWKV_HEADS, RWKV_HEAD_DIM = 8, 64
RWKV_W_LORA, RWKV_A_LORA, RWKV_V_LORA, RWKV_G_LORA = 64, 64, 32, 128
RWKV_LN_EPS = 64e-5
FOX_W = FOX_HEADS * FOX_HEAD_DIM
DSA_W = DSA_HEADS * DSA_HEAD_DIM
RWKV_W = RWKV_HEADS * RWKV_HEAD_DIM
IDX_W = IDX_HEADS * IDX_HEAD_DIM
N_EXPERTS = 8
FOX_IN = 3 * FOX_W + FOX_HEADS
DSA_IN = DSA_W + DSA_KV_RANK + IDX_W + IDX_HEAD_DIM + IDX_HEADS
RWKV_IN = 3 * RWKV_W + RWKV_W_LORA + RWKV_A_LORA + RWKV_G_LORA

LANE = 128
VMEM_LIMIT = 52 * 1024 * 1024
NEG = -1e30
INT_MIN = -(2 ** 31)
HI = lax.Precision.HIGHEST


def _params(*sem):
    return pltpu.CompilerParams(dimension_semantics=sem, vmem_limit_bytes=VMEM_LIMIT)


def _pick_tile(n, cap):
    best = LANE
    for t in range(LANE, min(n, cap) + 1, LANE):
        if n % t == 0:
            best = t
    return best


def _softplus(x):
    return jnp.maximum(x, 0.0) + jnp.log1p(jnp.exp(-jnp.abs(x)))


def _sigmoid(x):
    return 1.0 / (1.0 + jnp.exp(-x))


def _rms(x, g):
    return x * lax.rsqrt(jnp.mean(x * x, axis=-1, keepdims=True) + RMS_EPS) * g


def _rms_proj_kernel(x_ref, g_ref, w_ref, o_ref):
    h = _rms(x_ref[...], g_ref[...])
    o_ref[...] = jnp.dot(h.astype(BF16), w_ref[...], preferred_element_type=F32)


def _rms_proj(x, g, w):
    S, D = x.shape
    N = w.shape[1]
    tm = min(512, S)
    tn = _pick_tile(N, 2304)
    return pl.pallas_call(
        _rms_proj_kernel,
        grid=(N // tn, S // tm),
        in_specs=[pl.BlockSpec((tm, D), lambda j, i: (i, 0)),
                  pl.BlockSpec((1, D), lambda j, i: (0, 0)),
                  pl.BlockSpec((D, tn), lambda j, i: (0, j))],
        out_specs=pl.BlockSpec((tm, tn), lambda j, i: (i, j)),
        out_shape=jax.ShapeDtypeStruct((S, N), F32),
        compiler_params=_params("arbitrary", "arbitrary"),
        name="rms_proj",
    )(x, g.reshape(1, D), w)


def _rmsnorm_kernel(x_ref, g_ref, o_ref):
    o_ref[...] = _rms(x_ref[...], g_ref[...]).astype(o_ref.dtype)


def _rmsnorm(x, g, dtype):
    S, D = x.shape
    tm = min(2048, S)
    return pl.pallas_call(
        _rmsnorm_kernel,
        grid=(S // tm,),
        in_specs=[pl.BlockSpec((tm, D), lambda i: (i, 0)),
                  pl.BlockSpec((1, D), lambda i: (0, 0))],
        out_specs=pl.BlockSpec((tm, D), lambda i: (i, 0)),
        out_shape=jax.ShapeDtypeStruct((S, D), dtype),
        compiler_params=_params("arbitrary"),
        name="rmsnorm",
    )(x, g.reshape(1, D))


def _fox_cumsum_kernel(fl_ref, b_ref, hi_ref, mid_ref, lo_ref):
    H, R, _ = fl_ref.shape
    upper = (lax.broadcasted_iota(I32, (LANE, LANE), 0)
             <= lax.broadcasted_iota(I32, (LANE, LANE), 1)).astype(F32)
    strict_lower = (lax.broadcasted_iota(I32, (R, R), 0)
                    > lax.broadcasted_iota(I32, (R, R), 1)).astype(F32)
    for h in range(H):
        log_f = -_softplus(-(fl_ref[h] + b_ref[h]))
        within = jnp.dot(log_f, upper, preferred_element_type=F32, precision=HI)
        row_tot = jnp.broadcast_to(within[:, LANE - 1:LANE], (R, LANE))
        before = jnp.dot(strict_lower, row_tot, preferred_element_type=F32, precision=HI)
        hi, mid, lo = _split3(within + before)
        hi_ref[h], mid_ref[h], lo_ref[h] = hi, mid, lo


def _fox_cumsum(fl, bias):
    H, R, _ = fl.shape
    return pl.pallas_call(
        _fox_cumsum_kernel,
        out_shape=[jax.ShapeDtypeStruct((H, R, LANE), BF16)] * 3,
        compiler_params=pltpu.CompilerParams(vmem_limit_bytes=VMEM_LIMIT),
        name="fox_cumsum",
    )(fl, jnp.broadcast_to(bias.reshape(H, 1, 1), (H, 1, LANE)))


FOX_KA = 128


def _fox_kernel(qa_ref, ka_ref, vT_ref, o_ref, lga_sc, lgb_sc, m_sc, l_sc, acc_sc, *, t, tc, tg):
    i = pl.program_id(1)
    hb = qa_ref.shape[0]
    last = (i * t) // tg
    chains = [(h, q0) for h in range(hb) for q0 in range(0, t, tc)]

    def logits_into(dst_ref, g):
        off = pl.multiple_of(jnp.minimum(g, last) * tg, tg)
        for n, (h, q0) in enumerate(chains):
            dst_ref[n] = jnp.dot(ka_ref[h, pl.ds(off, tg), :], qa_ref[h, :, q0:q0 + tc],
                                 preferred_element_type=F32)

    def update(lg_ref, g, masked):
        off = pl.multiple_of(g * tg, tg)
        for n, (h, q0) in enumerate(chains):
            s = lg_ref[n]
            if masked:
                kpos = off + lax.broadcasted_iota(I32, (tg, tc), 0)
                qpos = i * t + q0 + lax.broadcasted_iota(I32, (tg, tc), 1)
                s = jnp.where(kpos <= qpos, s, NEG)
            m_prev = m_sc[n]
            m_new = jnp.maximum(m_prev, jnp.max(s, axis=0, keepdims=True))
            alpha = jnp.exp(m_prev - m_new)
            p = jnp.exp(s - m_new)
            l_sc[n] = alpha * l_sc[n] + jnp.sum(p, axis=0, keepdims=True)
            acc_sc[n] = alpha * acc_sc[n] + jnp.dot(vT_ref[h, :, pl.ds(off, tg)], p.astype(BF16),
                                                    preferred_element_type=F32)
            m_sc[n] = m_new

    m_sc[...] = jnp.full(m_sc.shape, NEG, F32)
    l_sc[...] = jnp.zeros(l_sc.shape, F32)
    acc_sc[...] = jnp.zeros(acc_sc.shape, F32)
    logits_into(lga_sc, 0)

    def pair(j, c):
        logits_into(lgb_sc, 2 * j + 1)
        update(lga_sc, 2 * j, False)
        logits_into(lga_sc, 2 * j + 2)
        update(lgb_sc, 2 * j + 1, False)
        return c

    lax.fori_loop(0, last // 2, pair, 0)
    tail = 2 * (last // 2)
    logits_into(lgb_sc, tail + 1)
    update(lga_sc, tail, True)

    @pl.when(tail + 1 <= last)
    def _():
        update(lgb_sc, tail + 1, True)

    for n, (h, q0) in enumerate(chains):
        o_ref[h, :, q0:q0 + tc] = (acc_sc[n] / l_sc[n]).astype(o_ref.dtype)


def _fox_attention(qaT, ka, vT):
    H, KA, S = qaT.shape
    Dh = vT.shape[1]
    t = min(512, S)
    tc = min(256, t)
    tg = min(1024, S)
    hb = 2
    nc = hb * (t // tc)
    return pl.pallas_call(
        functools.partial(_fox_kernel, t=t, tc=tc, tg=tg),
        grid=(H // hb, S // t),
        in_specs=[pl.BlockSpec((hb, KA, t), lambda h, i: (h, 0, i)),
                  pl.BlockSpec((hb, S, KA), lambda h, i: (h, 0, 0)),
                  pl.BlockSpec((hb, Dh, S), lambda h, i: (h, 0, 0))],
        out_specs=pl.BlockSpec((hb, Dh, t), lambda h, i: (h, 0, i)),
        out_shape=jax.ShapeDtypeStruct((H, Dh, S), BF16),
        scratch_shapes=[pltpu.VMEM((nc, tg, tc), F32), pltpu.VMEM((nc, tg, tc), F32),
                        pltpu.VMEM((nc, 1, tc), F32), pltpu.VMEM((nc, 1, tc), F32),
                        pltpu.VMEM((nc, Dh, tc), F32)],
        compiler_params=_params("arbitrary", "arbitrary"),
        name="fox_attention",
    )(qaT, ka, vT)


DSA_T = 128


def _dsa_kernel(qiT_ref, w_ref, qT_ref, wukT_ref, wuvT_ref, ki_ref, c_ref, cT_ref, o_ref,
                keys_sc, khi_sc, klo_sc, qlat_sc, m_sc, l_sc, acc_sc, lga_sc, lgb_sc, rela_sc, relb_sc,
                *, t, tg, topk):
    i = pl.program_id(0)
    last = (i * t) // tg
    ng = last + 1

    qiT = qiT_ref[0]
    w = w_ref[0]

    def rel_into(dst_ref, g):
        off = pl.multiple_of(jnp.minimum(g, last) * tg, tg)
        dst_ref[...] = jnp.dot(ki_ref[pl.ds(off, tg), :], qiT, preferred_element_type=F32)

    def score_keys(rel_ref, g, masked):
        off = pl.multiple_of(g * tg, tg)
        sc = jnp.maximum(rel_ref[:, 0:t], 0.0) * w[0:1, :]
        for h in range(1, IDX_HEADS):
            sc = sc + jnp.maximum(rel_ref[:, h * t:(h + 1) * t], 0.0) * w[h:h + 1, :]
        bits = pltpu.bitcast(sc, I32)
        key = bits ^ ((bits >> 31) & 0x7FFFFFFF)
        key = jnp.where(sc == 0.0, 0, key)
        if masked:
            kchunk = (off + lax.broadcasted_iota(I32, (tg, t), 0)) // CHUNK
            qchunk = (i * t + lax.broadcasted_iota(I32, (tg, t), 1)) // CHUNK
            key = jnp.where(kchunk <= qchunk, key, INT_MIN)
        keys_sc[pl.ds(off, tg), :] = key
        khi_sc[pl.ds(off, tg), :] = (key >> 16).astype(I16)
        klo_sc[pl.ds(off, tg), :] = ((key & 0xFFFF) + I16_MIN).astype(I16)

    def fill_pair(j, c):
        rel_into(relb_sc, 2 * j + 1)
        score_keys(rela_sc, 2 * j, False)
        rel_into(rela_sc, 2 * j + 2)
        score_keys(relb_sc, 2 * j + 1, False)
        return c

    rel_into(rela_sc, 0)
    lax.fori_loop(0, last // 2, fill_pair, 0)
    tail = 2 * (last // 2)
    rel_into(relb_sc, tail + 1)
    score_keys(rela_sc, tail, True)

    @pl.when(tail + 1 <= last)
    def _():
        score_keys(relb_sc, tail + 1, True)

    def count_ge(ref, cand, dtype):
        cb = jnp.broadcast_to(cand.astype(dtype), (t, t))

        def body(g, acc):
            off = pl.multiple_of(g * tg, tg)
            for k in range(tg // t):
                acc = acc + (ref[pl.ds(off + k * t, t), :] >= cb).astype(dtype)
            return acc

        acc = lax.fori_loop(0, ng, body, jnp.zeros((t, t), dtype))
        return jnp.sum(acc.astype(F32), axis=0, keepdims=True)

    def bisect16(ref, want):
        def step(s, thr):
            cand = thr + jnp.left_shift(jnp.int32(1), 15 - s)
            return jnp.where(count_ge(ref, cand, I16) >= want, cand, thr)
        return lax.fori_loop(0, 16, step, jnp.full((1, t), I16_MIN, I32))

    thr_hi = bisect16(khi_sc, topk)
    above = jnp.where(thr_hi == I16_MAX, 0.0, count_ge(khi_sc, jnp.minimum(thr_hi + 1, I16_MAX), I16))
    thr_hi_b = jnp.broadcast_to(thr_hi.astype(I16), (t, t))

    def mask_bucket(g, c):
        off = pl.multiple_of(g * tg, tg)
        for k in range(tg // t):
            rows = pl.ds(off + k * t, t)
            klo_sc[rows, :] = jnp.where(khi_sc[rows, :] == thr_hi_b, klo_sc[rows, :], I16_MIN)
        return c

    lax.fori_loop(0, ng, mask_bucket, 0)
    thr_lo = bisect16(klo_sc, topk - above)
    thr = thr_hi * 65536 + (thr_lo - I16_MIN)
    thr = jnp.maximum(thr, INT_MIN + 1)
    n_ge = count_ge(keys_sc, thr, I32)
    has_ties = jnp.max(jnp.where(n_ge > topk, 1.0, 0.0)) > 0.0
    thr_b = jnp.broadcast_to(thr, (tg, t))

    for h in range(DSA_HEADS):
        ql = jnp.dot(wukT_ref[h], qT_ref[0, h], preferred_element_type=F32)
        qlat_sc[:, h * t:(h + 1) * t] = (ql * (DSA_HEAD_DIM ** -0.5)).astype(BF16)
    m_sc[...] = jnp.full(m_sc.shape, NEG, F32)
    l_sc[...] = jnp.zeros(l_sc.shape, F32)
    acc_sc[...] = jnp.zeros(acc_sc.shape, F32)

    def logits_into(dst_ref, g):
        off = pl.multiple_of(jnp.minimum(g, last) * tg, tg)
        dst_ref[...] = jnp.dot(c_ref[pl.ds(off, tg), :], qlat_sc[...],
                               preferred_element_type=F32)

    def attend(lg_ref, g, seen_eq, with_ties, need):
        valid = g <= last
        off = pl.multiple_of(jnp.minimum(g, last) * tg, tg)
        key = keys_sc[pl.ds(off, tg), :]
        if with_ties:
            eq = (key == thr_b) & valid
            eqf = eq.astype(F32)
            strict_lower = (lax.broadcasted_iota(I32, (tg, tg), 0)
                            > lax.broadcasted_iota(I32, (tg, tg), 1)).astype(BF16)
            rank = jnp.dot(strict_lower, eqf.astype(BF16), preferred_element_type=F32) + seen_eq
            sel = (key > thr_b) | (eq & (rank < need))
            seen_eq = seen_eq + jnp.sum(eqf, axis=0, keepdims=True)
        else:
            sel = key >= thr_b
        bias = jnp.where(sel & valid, 0.0, NEG)
        cT_tile = cT_ref[:, pl.ds(off, tg)]
        for h in range(DSA_HEADS):
            s = lg_ref[:, h * t:(h + 1) * t] + bias
            m_prev = m_sc[h]
            m_new = jnp.maximum(m_prev, jnp.max(s, axis=0, keepdims=True))
            alpha = jnp.exp(m_prev - m_new)
            p = jnp.exp(s - m_new)
            l_sc[h] = alpha * l_sc[h] + jnp.sum(p, axis=0, keepdims=True)
            acc_sc[h] = alpha * acc_sc[h] + jnp.dot(cT_tile, p.astype(BF16),
                                                    preferred_element_type=F32)
            m_sc[h] = m_new
        return seen_eq

    def sweep(with_ties, need=None):
        logits_into(lga_sc, 0)

        def pair(j, seen_eq):
            g0 = 2 * j
            logits_into(lgb_sc, g0 + 1)
            seen_eq = attend(lga_sc, g0, seen_eq, with_ties, need)
            logits_into(lga_sc, g0 + 2)
            return attend(lgb_sc, g0 + 1, seen_eq, with_ties, need)

        lax.fori_loop(0, last // 2 + 1, pair, jnp.zeros((1, t), F32))

    @pl.when(has_ties)
    def _():
        sweep(True, topk - count_ge(keys_sc, thr + 1, I32))

    @pl.when(jnp.logical_not(has_ties))
    def _():
        sweep(False)

    for h in range(DSA_HEADS):
        o_lat = (acc_sc[h] / l_sc[h]).astype(BF16)
        o_ref[0, h] = jnp.dot(wuvT_ref[h], o_lat, preferred_element_type=F32).astype(o_ref.dtype)


def _dsa_attention(qiT, w, qT, wukT, wuvT, ki, c, cT, topk):
    NB, H, Dh, t = qT.shape
    S = NB * t
    R = DSA_KV_RANK
    tg = min(512, S)
    const2 = lambda i: (0, 0)
    const3 = lambda i: (0, 0, 0)
    return pl.pallas_call(
        functools.partial(_dsa_kernel, t=t, tg=tg, topk=topk),
        grid=(NB,),
        in_specs=[pl.BlockSpec((1, IDX_HEAD_DIM, IDX_HEADS * t), lambda i: (i, 0, 0)),
                  pl.BlockSpec((1, IDX_HEADS, t), lambda i: (i, 0, 0)),
                  pl.BlockSpec((1, H, Dh, t), lambda i: (i, 0, 0, 0)),
                  pl.BlockSpec((H, R, Dh), const3),
                  pl.BlockSpec((H, Dh, R), const3),
                  pl.BlockSpec((S, IDX_HEAD_DIM), const2, pipeline_mode=pl.Buffered(1)),
                  pl.BlockSpec((S, R), const2, pipeline_mode=pl.Buffered(1)),
                  pl.BlockSpec((R, S), const2, pipeline_mode=pl.Buffered(1))],
        out_specs=pl.BlockSpec((1, H, Dh, t), lambda i: (i, 0, 0, 0)),
        out_shape=jax.ShapeDtypeStruct((NB, H, Dh, t), BF16),
        scratch_shapes=[pltpu.VMEM((S, t), I32), pltpu.VMEM((S, t), I16), pltpu.VMEM((S, t), I16),
                        pltpu.VMEM((R, H * t), BF16),
                        pltpu.VMEM((H, 1, t), F32), pltpu.VMEM((H, 1, t), F32),
                        pltpu.VMEM((H, R, t), F32),
                        pltpu.VMEM((tg, H * t), F32), pltpu.VMEM((tg, H * t), F32),
                        pltpu.VMEM((tg, IDX_HEADS * t), F32), pltpu.VMEM((tg, IDX_HEADS * t), F32)],
        compiler_params=_params("arbitrary"),
        name="dsa_attention",
    )(qiT, w, qT, wukT, wuvT, ki, c, cT)


def _shift_mix(p, prev_tail, mu, first_block):
    rows = lax.broadcasted_iota(I32, p.shape, 0)
    tail = jnp.where(first_block, 0.0, prev_tail)
    prev = jnp.where(rows == 0, tail, pltpu.roll(p, 1, axis=0))
    return p + (prev - p) * mu


def _split3(f):
    hi = f.astype(BF16)
    r1 = f - hi.astype(F32)
    mid = r1.astype(BF16)
    lo = (r1 - mid.astype(F32)).astype(BF16)
    return hi, mid, lo


def _dot3(x, w01):
    return sum(jnp.dot(part, w01, preferred_element_type=F32) for part in _split3(x))


def _dot3_left(w01, x):
    return sum(jnp.dot(w01, part, preferred_element_type=F32) for part in _split3(x))


def _rwkv_pre_kernel(*refs, with_vres):
    (p_ref, pp_ref, mu_ref, w0_ref, a0_ref, wwa_ref, gup_ref, kk_ref, ka_ref, rk_ref,
     tri_ref, blk_ref, bd_ref) = refs[:13]
    if with_vres:
        vup_ref, vv0_ref, vf_ref = refs[13:16]
    (at_o, rt_o, bt_o, kt_o, bh_o, kh_o, vb_o, pc_o, bonus_o, g_o, v_o) = refs[-11:]
    first = pl.program_id(0) == 0
    W = RWKV_W
    ps = _shift_mix(p_ref[...], pp_ref[7:8, :], mu_ref[...], first)
    r, k, v = ps[:, 0:W], ps[:, W:2 * W], ps[:, 2 * W:3 * W]
    wa = ps[:, 3 * W:3 * W + LANE]
    gl = ps[:, 3 * W + LANE:3 * W + 2 * LANE]
    lane = lax.broadcasted_iota(I32, wa.shape, 1)
    wa = jnp.where(lane < RWKV_W_LORA, jnp.tanh(wa), wa)
    up = jnp.dot(wa.astype(BF16), wwa_ref[...], preferred_element_type=F32)
    log_w = -_softplus(-(w0_ref[...] + up[:, 0:W])) - 0.5
    a = _sigmoid(a0_ref[...] + up[:, W:2 * W])
    g = jnp.dot(_sigmoid(gl).astype(BF16), gup_ref[...], preferred_element_type=F32)
    if with_vres:
        vl = ps[:, RWKV_IN:RWKV_IN + LANE]
        logit = vv0_ref[...] + jnp.dot(vl.astype(BF16), vup_ref[...], preferred_element_type=F32)
        v = v + (vf_ref[...] - v) * _sigmoid(logit)
    k2 = k * (1.0 + (a - 1.0) * ka_ref[...])
    kkr = k * kk_ref[...]
    bd = bd_ref[...]
    kk = kkr * lax.rsqrt(_dot3(kkr * kkr, bd) + 1e-12)
    lw = -jnp.exp(log_w)
    cum = _dot3_left(tri_ref[...], lw)
    cend = _dot3_left(blk_ref[...], lw)
    p_inv = jnp.exp(-cum)
    p_end = jnp.exp(cend - cum)
    beta = kk * a
    at_o[...] = (-kk * jnp.exp(cum - lw)).astype(BF16)
    rt_o[...] = (r * jnp.exp(cum)).astype(BF16)
    bt_o[...] = (beta * p_inv).astype(BF16)
    kt_o[...] = (k2 * p_inv).astype(BF16)
    bh_o[...] = (beta * p_end).astype(BF16)
    kh_o[...] = (k2 * p_end).astype(BF16)
    vb_o[...] = v.astype(BF16)
    pc_o[...] = jnp.exp(cend)
    bonus_o[...] = _dot3(r * k2 * rk_ref[...], bd) * v
    g_o[...] = g
    v_o[...] = v


def _block_ones(n, block, lower_tri=False):
    i = jnp.arange(n)
    m = (i[:, None] // block) == (i[None, :] // block)
    if lower_tri:
        m = m & (i[:, None] >= i[None, :])
    return m.astype(BF16)


def _rwkv_pre(p, mu, w0, a0, wwa, gup, k_k, k_a, r_k, vres=None):
    S, PW = p.shape
    W = RWKV_W
    tm = min(512, S)
    row = lambda i: (i, 0)
    const = lambda i: (0, 0)
    tail = lambda i: (jnp.maximum(i * (tm // 8) - 1, 0), 0)
    vec = lambda a: a.reshape(1, -1)
    args = [p, p, vec(mu), vec(w0), vec(a0), wwa, gup, vec(k_k), vec(k_a), vec(r_k),
            _block_ones(tm, CHUNK, lower_tri=True), _block_ones(tm, CHUNK),
            _block_ones(W, RWKV_HEAD_DIM)]
    specs = [pl.BlockSpec((tm, PW), row), pl.BlockSpec((8, PW), tail),
             pl.BlockSpec((1, PW), const), pl.BlockSpec((1, W), const), pl.BlockSpec((1, W), const),
             pl.BlockSpec(wwa.shape, const), pl.BlockSpec(gup.shape, const),
             pl.BlockSpec((1, W), const), pl.BlockSpec((1, W), const), pl.BlockSpec((1, W), const),
             pl.BlockSpec((tm, tm), const), pl.BlockSpec((tm, tm), const),
             pl.BlockSpec((W, W), const)]
    if vres is not None:
        vup, vv0, v_first = vres
        args += [vup, vec(vv0), v_first]
        specs += [pl.BlockSpec(vup.shape, const), pl.BlockSpec((1, W), const),
                  pl.BlockSpec((tm, W), row)]
    return pl.pallas_call(
        functools.partial(_rwkv_pre_kernel, with_vres=vres is not None),
        grid=(S // tm,),
        in_specs=specs,
        out_specs=[pl.BlockSpec((tm, W), row)] * 11,
        out_shape=[jax.ShapeDtypeStruct((S, W), BF16)] * 7 + [jax.ShapeDtypeStruct((S, W), F32)] * 4,
        compiler_params=_params("arbitrary"),
        name="rwkv_pre",
    )(*args)


def _rwkv_scan_kernel(at_ref, rt_ref, bt_ref, kt_ref, bh_ref, kh_ref, v_ref, pc_ref, y_ref, h_sc,
                      *, chunks):
    C = CHUNK
    N = RWKV_HEAD_DIM

    @pl.when(pl.program_id(0) == 0)
    def _():
        h_sc[...] = jnp.zeros(h_sc.shape, F32)

    ti = lax.broadcasted_iota(I32, (C, C), 0)
    tj = lax.broadcasted_iota(I32, (C, C), 1)
    lower_incl = ti >= tj
    lower_strict = ti > tj
    eye = (ti == tj).astype(F32)
    eye_n = (lax.broadcasted_iota(I32, (N, N), 0) == lax.broadcasted_iota(I32, (N, N), 1)).astype(F32)

    def mm(x, y):
        return jnp.dot(x.astype(BF16), y.astype(BF16), preferred_element_type=F32)

    def mm_nt(x, y):
        return lax.dot_general(x.astype(BF16), y.astype(BF16), (((1,), (1,)), ((), ())),
                               preferred_element_type=F32)

    def mm_tn(x, y):
        return lax.dot_general(x.astype(BF16), y.astype(BF16), (((0,), (0,)), ((), ())),
                               preferred_element_type=F32)

    units = [(c, h) for c in range(chunks) for h in range(RWKV_HEADS)]
    tile = lambda ref, u: ref[u[0] * C:(u[0] + 1) * C, u[1] * N:(u[1] + 1) * N]
    each = lambda fn: {u: fn(u) for u in units}

    At, Rt, Bt, Kt = (each(lambda u, r=ref: tile(r, u)) for ref in (at_ref, rt_ref, bt_ref, kt_ref))
    Bh, Kh, V = (each(lambda u, r=ref: tile(r, u)) for ref in (bh_ref, kh_ref, v_ref))
    AR = each(lambda u: jnp.concatenate([At[u], Rt[u]], axis=0))
    Mb = each(lambda u: mm_nt(AR[u], Bt[u]))
    Mk = each(lambda u: mm_nt(AR[u], Kt[u]))
    Lab = each(lambda u: jnp.where(lower_strict, Mb[u][0:C], 0.0))
    Mrb = each(lambda u: jnp.where(lower_incl, Mb[u][C:2 * C], 0.0))
    Lak = each(lambda u: jnp.where(lower_strict, Mk[u][0:C], 0.0))
    Mrk = each(lambda u: jnp.where(lower_incl, Mk[u][C:2 * C], 0.0))
    T = each(lambda u: eye + Lab[u])
    Lp = Lab
    span = 2
    while span < C:
        Lp = each(lambda u, Lp=Lp: mm(Lp[u], Lp[u]))
        T = each(lambda u, T=T, Lp=Lp: T[u] + mm(Lp[u], T[u]))
        span *= 2
    W1 = each(lambda u: mm(Lak[u], V[u]))
    A2 = each(lambda u: mm(T[u], At[u]))
    U0 = each(lambda u: mm(T[u], W1[u]))
    R2 = each(lambda u: Rt[u].astype(F32) + mm(Mrb[u], A2[u]))
    Y0 = each(lambda u: mm(Mrb[u], U0[u]) + mm(Mrk[u], V[u]))
    G = each(lambda u: eye_n * tile(pc_ref, u)[0:1, :] + mm_tn(Bh[u], A2[u]))
    H0 = each(lambda u: mm_tn(Bh[u], U0[u]) + mm_tn(Kh[u], V[u]))

    H = {h: h_sc[h] for h in range(RWKV_HEADS)}
    for c in range(chunks):
        ys = []
        for h in range(RWKV_HEADS):
            u = (c, h)
            ys.append(mm(R2[u], H[h]) + Y0[u])
            H[h] = mm(G[u], H[h]) + H0[u]
        y_ref[c * C:(c + 1) * C, :] = jnp.concatenate(ys, axis=1)
    for h in range(RWKV_HEADS):
        h_sc[h] = H[h]


def _rwkv_scan(at, rt, bt, kt, bh, kh, vb, pc):
    S, W = at.shape
    chunks = 2 if S % (2 * CHUNK) == 0 else 1
    tb = chunks * CHUNK
    seq = pl.BlockSpec((tb, W), lambda i: (i, 0))
    return pl.pallas_call(
        functools.partial(_rwkv_scan_kernel, chunks=chunks),
        grid=(S // tb,),
        in_specs=[seq] * 8,
        out_specs=seq,
        out_shape=jax.ShapeDtypeStruct((S, W), F32),
        scratch_shapes=[pltpu.VMEM((RWKV_HEADS, RWKV_HEAD_DIM, RWKV_HEAD_DIM), F32)],
        compiler_params=_params("arbitrary"),
        name="rwkv_scan",
    )(at, rt, bt, kt, bh, kh, vb, pc)


def _merge_kernel(x_ref, of_ref, od_ref, y_ref, bonus_ref, g_ref, lng_ref, lnb_ref, bd_ref,
                  gp_ref, bg_ref, pf_ref, pd_ref, pr_ref, wo_ref, gn_ref, xo_ref, h_ref):
    D = D_MODEL
    bd = bd_ref[...]
    inv_n = 1.0 / RWKV_HEAD_DIM
    y = y_ref[...]
    yc = y - _dot3(y, bd) * inv_n
    var = _dot3(yc * yc, bd) * inv_n
    yn = yc * lax.rsqrt(var + RWKV_LN_EPS) * lng_ref[...] + lnb_ref[...]
    o_rwkv = ((yn + bonus_ref[...]) * g_ref[...]).astype(BF16)
    gates = _sigmoid(gp_ref[...] + bg_ref[...])
    merged = (gates[:, 0:D] * jnp.dot(of_ref[...], pf_ref[...], preferred_element_type=F32)
              + gates[:, D:2 * D] * jnp.dot(od_ref[...], pd_ref[...], preferred_element_type=F32)
              + gates[:, 2 * D:3 * D] * jnp.dot(o_rwkv, pr_ref[...], preferred_element_type=F32))
    x = x_ref[...] + jnp.dot(merged.astype(BF16), wo_ref[...], preferred_element_type=F32)
    xo_ref[...] = x
    h_ref[...] = _rms(x, gn_ref[...]).astype(h_ref.dtype)


def _merge(x, o_fox, o_dsa, y_rwkv, bonus, g_rwkv, ln_g, ln_b, gate_p, b_gate, p_fox, p_dsa, p_rwkv,
           w_out, g_ffn):
    S, D = x.shape
    W = RWKV_W
    tm = min(512, S)
    row = lambda i: (i, 0)
    const = lambda i: (0, 0)
    return pl.pallas_call(
        _merge_kernel,
        grid=(S // tm,),
        in_specs=[pl.BlockSpec((tm, D), row), pl.BlockSpec((tm, FOX_W), row),
                  pl.BlockSpec((tm, DSA_W), row), pl.BlockSpec((tm, W), row),
                  pl.BlockSpec((tm, W), row), pl.BlockSpec((tm, W), row),
                  pl.BlockSpec((1, W), const), pl.BlockSpec((1, W), const),
                  pl.BlockSpec((W, W), const),
                  pl.BlockSpec((tm, 3 * D), row), pl.BlockSpec((1, 3 * D), const),
                  pl.BlockSpec((FOX_W, D), const), pl.BlockSpec((DSA_W, D), const),
                  pl.BlockSpec((W, D), const), pl.BlockSpec((D, D), const),
                  pl.BlockSpec((1, D), const)],
        out_specs=[pl.BlockSpec((tm, D), row), pl.BlockSpec((tm, D), row)],
        out_shape=[jax.ShapeDtypeStruct((S, D), F32), jax.ShapeDtypeStruct((S, D), BF16)],
        compiler_params=_params("arbitrary"),
        name="merge",
    )(x, o_fox, o_dsa, y_rwkv, bonus, g_rwkv, ln_g.reshape(1, W), ln_b.reshape(1, W),
      _block_ones(W, RWKV_HEAD_DIM), gate_p, b_gate.reshape(1, -1), p_fox, p_dsa, p_rwkv, w_out,
      g_ffn.reshape(1, D))


def _swiglu_tile(h, wg, wu):
    gate = jnp.dot(h, wg, preferred_element_type=F32)
    up = jnp.dot(h, wu, preferred_element_type=F32)
    return gate * _sigmoid(gate) * up


def _ffn_kernel(x_ref, h_ref, wg_ref, wu_ref, wd_ref, gf_ref, o_ref, acc_sc, *, final_norm):
    f = pl.program_id(1)

    @pl.when(f == 0)
    def _():
        acc_sc[...] = x_ref[...]

    act = _swiglu_tile(h_ref[...], wg_ref[...], wu_ref[...])
    acc_sc[...] += jnp.dot(act.astype(BF16), wd_ref[...], preferred_element_type=F32)

    @pl.when(f == pl.num_programs(1) - 1)
    def _():
        y = acc_sc[...]
        o_ref[...] = _rms(y, gf_ref[...]) if final_norm else y


def _ffn(x, h, wg, wu, wd, g_final, final_norm):
    S, D = x.shape
    Fd = wg.shape[1]
    tm = min(512, S)
    tf = _pick_tile(Fd, 1408)
    return pl.pallas_call(
        functools.partial(_ffn_kernel, final_norm=final_norm),
        grid=(S // tm, Fd // tf),
        in_specs=[pl.BlockSpec((tm, D), lambda i, f: (i, 0)),
                  pl.BlockSpec((tm, D), lambda i, f: (i, 0)),
                  pl.BlockSpec((D, tf), lambda i, f: (0, f)),
                  pl.BlockSpec((D, tf), lambda i, f: (0, f)),
                  pl.BlockSpec((tf, D), lambda i, f: (f, 0)),
                  pl.BlockSpec((1, D), lambda i, f: (0, 0))],
        out_specs=pl.BlockSpec((tm, D), lambda i, f: (i, 0)),
        out_shape=jax.ShapeDtypeStruct((S, D), F32),
        scratch_shapes=[pltpu.VMEM((tm, D), F32)],
        compiler_params=_params("arbitrary", "arbitrary"),
        name="ffn",
    )(x, h, wg, wu, wd, g_final.reshape(1, D))


def _moe_kernel(x_ref, h_ref, rw_ref, rb_ref, wg_ref, wu_ref, wd_ref, gf_ref, o_ref,
                acc_sc, gate_sc, *, final_norm):
    e = pl.program_id(1)
    f = pl.program_id(2)
    lane = lax.broadcasted_iota(I32, gate_sc.shape, 1)

    @pl.when((e == 0) & (f == 0))
    def _():
        acc_sc[...] = x_ref[...]
        logits = jnp.dot(h_ref[...], rw_ref[...], preferred_element_type=F32) + rb_ref[...]
        logits = jnp.where(lane < N_EXPERTS, logits, -jnp.inf)
        v1 = jnp.max(logits, axis=-1, keepdims=True)
        i1 = jnp.min(jnp.where(logits == v1, lane, LANE), axis=-1, keepdims=True)
        rest = jnp.where(lane == i1, -jnp.inf, logits)
        v2 = jnp.max(rest, axis=-1, keepdims=True)
        i2 = jnp.min(jnp.where(rest == v2, lane, LANE), axis=-1, keepdims=True)
        e2 = jnp.exp(v2 - v1)
        p1 = 1.0 / (1.0 + e2)
        gate_sc[...] = jnp.where(lane == i1, p1, 0.0) + jnp.where(lane == i2, e2 * p1, 0.0)

    gate = jnp.sum(jnp.where(lane == e, gate_sc[...], 0.0), axis=-1, keepdims=True)
    act = _swiglu_tile(h_ref[...], wg_ref[0], wu_ref[0]) * gate
    acc_sc[...] += jnp.dot(act.astype(BF16), wd_ref[0], preferred_element_type=F32)

    @pl.when((e == pl.num_programs(1) - 1) & (f == pl.num_programs(2) - 1))
    def _():
        y = acc_sc[...]
        o_ref[...] = _rms(y, gf_ref[...]) if final_norm else y


def _moe(x, h, rw, rb, wg, wu, wd, g_final, final_norm):
    S, D = x.shape
    E, _, Fe = wg.shape
    tm = min(1024, S)
    tf = _pick_tile(Fe, 896)
    return pl.pallas_call(
        functools.partial(_moe_kernel, final_norm=final_norm),
        grid=(S // tm, E, Fe // tf),
        in_specs=[pl.BlockSpec((tm, D), lambda i, e, f: (i, 0)),
                  pl.BlockSpec((tm, D), lambda i, e, f: (i, 0)),
                  pl.BlockSpec((D, LANE), lambda i, e, f: (0, 0)),
                  pl.BlockSpec((1, LANE), lambda i, e, f: (0, 0)),
                  pl.BlockSpec((1, D, tf), lambda i, e, f: (e, 0, f)),
                  pl.BlockSpec((1, D, tf), lambda i, e, f: (e, 0, f)),
                  pl.BlockSpec((1, tf, D), lambda i, e, f: (e, f, 0)),
                  pl.BlockSpec((1, D), lambda i, e, f: (0, 0))],
        out_specs=pl.BlockSpec((tm, D), lambda i, e, f: (i, 0)),
        out_shape=jax.ShapeDtypeStruct((S, D), F32),
        scratch_shapes=[pltpu.VMEM((tm, D), F32), pltpu.VMEM((tm, LANE), F32)],
        compiler_params=_params("arbitrary", "arbitrary", "arbitrary"),
        name="moe",
    )(x, h, rw, rb, wg, wu, wd, g_final.reshape(1, D))


def _cast_kernel(x_ref, o_ref):
    o_ref[...] = x_ref[...].astype(o_ref.dtype)


def _to_bf16(w):
    E, A, B = w.shape
    ta = 256
    return pl.pallas_call(
        _cast_kernel,
        grid=(E, A // ta),
        in_specs=[pl.BlockSpec((1, ta, B), lambda e, a: (e, a, 0))],
        out_specs=pl.BlockSpec((1, ta, B), lambda e, a: (e, a, 0)),
        out_shape=jax.ShapeDtypeStruct(w.shape, BF16),
        compiler_params=_params("arbitrary", "arbitrary"),
        name="to_bf16",
    )(w)


def _pad_cols(a, n):
    return jnp.pad(a, ((0, 0), (0, n - a.shape[1])))


def kernel(x, w_in, b_gate, g_mix, fox_f_bias, dsa_kv_norm, dsa_w_uk, dsa_w_uv, rwkv_mu, rwkv_w0, rwkv_w_up, rwkv_a0, rwkv_a_up, rwkv_g_up, rwkv_k_k, rwkv_k_a, rwkv_r_k, rwkv_ln_g, rwkv_ln_b, vres_down, vres_mu, vres_up, vres_v0, p_fox, p_dsa, p_rwkv, w_out, g_ffn, ffn_w_gate, ffn_w_up, ffn_w_down, router_w, router_b, moe_w_gate, moe_w_up, moe_w_down, g_final):
    B, S, D = x.shape
    assert B == 1 and D == D_MODEL and S % LANE == 0
    depth = w_in.shape[0]
    topk = min(IDX_TOPK, S // 4)
    bf = lambda a: a.astype(BF16)
    xs = x[0]
    v_first = None
    w_in_bf = _to_bf16(w_in)
    for l in range(depth):
        wl = w_in_bf[l]
        w_fox, w_dsa, w_rwkv, w_gate = (wl[:, :FOX_IN], wl[:, FOX_IN:FOX_IN + DSA_IN],
                                        wl[:, FOX_IN + DSA_IN:FOX_IN + DSA_IN + RWKV_IN],
                                        wl[:, FOX_IN + DSA_IN + RWKV_IN:])
        o1 = DSA_W + DSA_KV_RANK + IDX_W
        w_att = jnp.concatenate([_pad_cols(w_fox, 3 * FOX_W + LANE), w_dsa[:, :o1],
                                 _pad_cols(w_dsa[:, o1:], LANE)], axis=1)
        if l > 0:
            w_rwkv = jnp.concatenate([w_rwkv, bf(_pad_cols(vres_down[l - 1], LANE))], axis=1)
        pa = _rms_proj(xs, g_mix[l], w_att)
        pr = _rms_proj(xs, g_mix[l], w_rwkv)
        gate_p = _rms_proj(xs, g_mix[l], w_gate)

        c0 = 3 * FOX_W
        fl = pa[:, c0:c0 + FOX_HEADS].T.reshape(FOX_HEADS, S // LANE, LANE)
        Fh, Fm, Fl = (a.reshape(FOX_HEADS, S) for a in _fox_cumsum(fl, fox_f_bias[l]))
        qT = bf(pa[:, 0:FOX_W].T.reshape(FOX_HEADS, FOX_HEAD_DIM, S) * (FOX_HEAD_DIM ** -0.5))
        r = jnp.arange(FOX_KA - FOX_HEAD_DIM)
        rq, ck = r[None, :, None], r[None, None, :]
        pick = lambda idx, a, b, c, d: jnp.where(idx == a[0], a[1], jnp.where(
            idx == b[0], b[1], jnp.where(idx == c[0], c[1], jnp.where(idx < 6, d, 0)))).astype(BF16)
        fq = pick(rq, (0, Fh[:, None, :]), (1, Fm[:, None, :]), (2, Fl[:, None, :]), 1)
        fk = pick(ck, (3, -Fh[:, :, None]), (4, -Fm[:, :, None]), (5, -Fl[:, :, None]), 1)
        qaT = jnp.concatenate([qT, fq], axis=1)
        k4 = bf(pa[:, FOX_W:2 * FOX_W].reshape(S, FOX_HEADS, FOX_HEAD_DIM).transpose(1, 0, 2))
        ka = jnp.concatenate([k4, fk], axis=2)
        fvT = bf(pa[:, 2 * FOX_W:3 * FOX_W].T.reshape(FOX_HEADS, FOX_HEAD_DIM, S))
        o_fox = _fox_attention(qaT, ka, fvT).reshape(FOX_W, S).T

        t = min(DSA_T, S)
        nb = S // t
        c1 = c0 + LANE
        dqT = bf(pa[:, c1:c1 + DSA_W].T.reshape(DSA_HEADS, DSA_HEAD_DIM, nb, t).transpose(2, 0, 1, 3))
        c2 = c1 + DSA_W
        ckv = _rmsnorm(pa[:, c2:c2 + DSA_KV_RANK], dsa_kv_norm[l], BF16)
        c3 = c2 + DSA_KV_RANK
        qiT = bf(pa[:, c3:c3 + IDX_W].T.reshape(IDX_HEADS, IDX_HEAD_DIM, nb, t).transpose(2, 1, 0, 3)
                 .reshape(nb, IDX_HEAD_DIM, IDX_HEADS * t))
        c4 = c3 + IDX_W
        ki = bf(pa[:, c4:c4 + IDX_HEAD_DIM])
        c5 = c4 + IDX_HEAD_DIM
        wi = pa[:, c5:c5 + IDX_HEADS].T.reshape(IDX_HEADS, nb, t).transpose(1, 0, 2) * (IDX_W ** -0.5)
        o_dsa = _dsa_attention(qiT, wi, dqT, bf(dsa_w_uk[l].transpose(0, 2, 1)),
                               bf(dsa_w_uv[l].transpose(0, 2, 1)), ki, ckv, ckv.T, topk)
        o_dsa = o_dsa.transpose(1, 2, 0, 3).reshape(DSA_W, S).T

        zw = jnp.zeros((RWKV_W_LORA, RWKV_W), F32)
        wwa = bf(jnp.concatenate([jnp.concatenate([rwkv_w_up[l], zw], axis=1),
                                  jnp.concatenate([zw, rwkv_a_up[l]], axis=1)], axis=0))
        vres = None
        mu = rwkv_mu[l]
        if l > 0:
            vup = jnp.pad(vres_up[l - 1], ((0, LANE - RWKV_V_LORA), (0, 0)))
            vres = (bf(vup), vres_v0[l - 1], v_first)
            mu = jnp.concatenate([mu, jnp.pad(vres_mu[l - 1], (0, LANE - RWKV_V_LORA))])
        *scan_ops, bonus, g_rwkv, v = _rwkv_pre(
            pr, mu, rwkv_w0[l], rwkv_a0[l], wwa, bf(rwkv_g_up[l]),
            rwkv_k_k[l], rwkv_k_a[l], rwkv_r_k[l], vres)
        if l == 0:
            v_first = v
        y_rwkv = _rwkv_scan(*scan_ops)

        xs, h2 = _merge(xs, o_fox, o_dsa, y_rwkv, bonus, g_rwkv, rwkv_ln_g[l], rwkv_ln_b[l],
                        gate_p, b_gate[l], bf(p_fox[l]), bf(p_dsa[l]), bf(p_rwkv[l]), bf(w_out[l]),
                        g_ffn[l])

        last = l == depth - 1
        if l % 2 == 0:
            xs = _ffn(xs, h2, bf(ffn_w_gate[l // 2]), bf(ffn_w_up[l // 2]), bf(ffn_w_down[l // 2]),
                      g_final, last)
        else:
            rw = bf(_pad_cols(router_w[l // 2], LANE))
            rb = _pad_cols(router_b[l // 2].reshape(1, -1), LANE)
            xs = _moe(xs, h2, rw, rb, _to_bf16(moe_w_gate[l // 2]), _to_bf16(moe_w_up[l // 2]),
                      _to_bf16(moe_w_down[l // 2]), g_final, last)
    return xs[None]
```

```python
import functools

import jax
import jax.numpy as jnp
from jax import lax
from jax.experimental import pallas as pl
from jax.experimental.pallas import tpu as pltpu

F32 = jnp.float32
BF16 = jnp.bfloat16
I32 = jnp.int32

D_MODEL = 1024
CHUNK = 64
RMS_EPS = 1e-6
FOX_HEADS, FOX_HEAD_DIM = 4, 64
DSA_HEADS, DSA_HEAD_DIM, DSA_KV_RANK = 4, 64, 128
IDX_HEADS, IDX_HEAD_DIM, IDX_TOPK = 8, 32, 256
RWKV_HEADS, RWKV_HEAD_DIM = 8, 64
RWKV_W_LORA, RWKV_A_LORA, RWKV_V_LORA, RWKV_G_LORA = 64, 64, 32, 128
RWKV_LN_EPS = 64e-5
FOX_W = FOX_HEADS * FOX_HEAD_DIM
DSA_W = DSA_HEADS * DSA_HEAD_DIM
RWKV_W = RWKV_HEADS * RWKV_HEAD_DIM
IDX_W = IDX_HEADS * IDX_HEAD_DIM
N_EXPERTS = 8
FOX_IN = 3 * FOX_W + FOX_HEADS
DSA_IN = DSA_W + DSA_KV_RANK + IDX_W + IDX_HEAD_DIM + IDX_HEADS
RWKV_IN = 3 * RWKV_W + RWKV_W_LORA + RWKV_A_LORA + RWKV_G_LORA

LANE = 128
VMEM_LIMIT = 52 * 1024 * 1024
NEG = -1e30
INT_MIN = -(2 ** 31)
HI = lax.Precision.HIGHEST


def _params(*sem):
    return pltpu.CompilerParams(dimension_semantics=sem, vmem_limit_bytes=VMEM_LIMIT)


def _pick_tile(n, cap):
    best = LANE
    for t in range(LANE, min(n, cap) + 1, LANE):
        if n % t == 0:
            best = t
    return best


def _softplus(x):
    return jnp.maximum(x, 0.0) + jnp.log1p(jnp.exp(-jnp.abs(x)))


def _sigmoid(x):
    return 1.0 / (1.0 + jnp.exp(-x))


def _rms(x, g):
    return x * lax.rsqrt(jnp.mean(x * x, axis=-1, keepdims=True) + RMS_EPS) * g


def _rms_proj_kernel(x_ref, g_ref, w_ref, o_ref):
    h = _rms(x_ref[...], g_ref[...])
    o_ref[...] = jnp.dot(h.astype(BF16), w_ref[...], preferred_element_type=F32)


def _rms_proj(x, g, w):
    S, D = x.shape
    N = w.shape[1]
    tm = min(512, S)
    tn = _pick_tile(N, 2304)
    return pl.pallas_call(
        _rms_proj_kernel,
        grid=(N // tn, S // tm),
        in_specs=[pl.BlockSpec((tm, D), lambda j, i: (i, 0)),
                  pl.BlockSpec((1, D), lambda j, i: (0, 0)),
                  pl.BlockSpec((D, tn), lambda j, i: (0, j))],
        out_specs=pl.BlockSpec((tm, tn), lambda j, i: (i, j)),
        out_shape=jax.ShapeDtypeStruct((S, N), F32),
        compiler_params=_params("arbitrary", "arbitrary"),
        name="rms_proj",
    )(x, g.reshape(1, D), w)


def _rmsnorm_kernel(x_ref, g_ref, o_ref):
    o_ref[...] = _rms(x_ref[...], g_ref[...]).astype(o_ref.dtype)


def _rmsnorm(x, g, dtype):
    S, D = x.shape
    tm = min(2048, S)
    return pl.pallas_call(
        _rmsnorm_kernel,
        grid=(S // tm,),
        in_specs=[pl.BlockSpec((tm, D), lambda i: (i, 0)),
                  pl.BlockSpec((1, D), lambda i: (0, 0))],
        out_specs=pl.BlockSpec((tm, D), lambda i: (i, 0)),
        out_shape=jax.ShapeDtypeStruct((S, D), dtype),
        compiler_params=_params("arbitrary"),
        name="rmsnorm",
    )(x, g.reshape(1, D))


def _fox_cumsum_kernel(fl_ref, b_ref, hi_ref, mid_ref, lo_ref):
    H, R, _ = fl_ref.shape
    upper = (lax.broadcasted_iota(I32, (LANE, LANE), 0)
             <= lax.broadcasted_iota(I32, (LANE, LANE), 1)).astype(F32)
    strict_lower = (lax.broadcasted_iota(I32, (R, R), 0)
                    > lax.broadcasted_iota(I32, (R, R), 1)).astype(F32)
    for h in range(H):
        log_f = -_softplus(-(fl_ref[h] + b_ref[h]))
        within = jnp.dot(log_f, upper, preferred_element_type=F32, precision=HI)
        row_tot = jnp.broadcast_to(within[:, LANE - 1:LANE], (R, LANE))
        before = jnp.dot(strict_lower, row_tot, preferred_element_type=F32, precision=HI)
        hi, mid, lo = _split3(within + before)
        hi_ref[h], mid_ref[h], lo_ref[h] = hi, mid, lo


def _fox_cumsum(fl, bias):
    H, R, _ = fl.shape
    return pl.pallas_call(
        _fox_cumsum_kernel,
        out_shape=[jax.ShapeDtypeStruct((H, R, LANE), BF16)] * 3,
        compiler_params=pltpu.CompilerParams(vmem_limit_bytes=VMEM_LIMIT),
        name="fox_cumsum",
    )(fl, jnp.broadcast_to(bias.reshape(H, 1, 1), (H, 1, LANE)))


FOX_KA = 128


def _fox_kernel(qa_ref, ka_ref, vT_ref, o_ref, lga_sc, lgb_sc, m_sc, l_sc, acc_sc, *, t, tc, tg):
    i = pl.program_id(1)
    hb = qa_ref.shape[0]
    last = (i * t) // tg
    chains = [(h, q0) for h in range(hb) for q0 in range(0, t, tc)]

    def logits_into(dst_ref, g):
        off = pl.multiple_of(jnp.minimum(g, last) * tg, tg)
        for n, (h, q0) in enumerate(chains):
            dst_ref[n] = jnp.dot(ka_ref[h, pl.ds(off, tg), :], qa_ref[h, :, q0:q0 + tc],
                                 preferred_element_type=F32)

    def update(lg_ref, g, masked):
        off = pl.multiple_of(g * tg, tg)
        for n, (h, q0) in enumerate(chains):
            s = lg_ref[n]
            if masked:
                kpos = off + lax.broadcasted_iota(I32, (tg, tc), 0)
                qpos = i * t + q0 + lax.broadcasted_iota(I32, (tg, tc), 1)
                s = jnp.where(kpos <= qpos, s, NEG)
            m_prev = m_sc[n]
            m_new = jnp.maximum(m_prev, jnp.max(s, axis=0, keepdims=True))
            alpha = jnp.exp(m_prev - m_new)
            p = jnp.exp(s - m_new)
            l_sc[n] = alpha * l_sc[n] + jnp.sum(p, axis=0, keepdims=True)
            acc_sc[n] = alpha * acc_sc[n] + jnp.dot(vT_ref[h, :, pl.ds(off, tg)], p.astype(BF16),
                                                    preferred_element_type=F32)
            m_sc[n] = m_new

    m_sc[...] = jnp.full(m_sc.shape, NEG, F32)
    l_sc[...] = jnp.zeros(l_sc.shape, F32)
    acc_sc[...] = jnp.zeros(acc_sc.shape, F32)
    logits_into(lga_sc, 0)

    def pair(j, c):
        logits_into(lgb_sc, 2 * j + 1)
        update(lga_sc, 2 * j, False)
        logits_into(lga_sc, 2 * j + 2)
        update(lgb_sc, 2 * j + 1, False)
        return c

    lax.fori_loop(0, last // 2, pair, 0)
    tail = 2 * (last // 2)
    logits_into(lgb_sc, tail + 1)
    update(lga_sc, tail, True)

    @pl.when(tail + 1 <= last)
    def _():
        update(lgb_sc, tail + 1, True)

    for n, (h, q0) in enumerate(chains):
        o_ref[h, :, q0:q0 + tc] = (acc_sc[n] / l_sc[n]).astype(o_ref.dtype)


def _fox_attention(qaT, ka, vT):
    H, KA, S = qaT.shape
    Dh = vT.shape[1]
    t = min(512, S)
    tc = min(256, t)
    tg = min(1024, S)
    hb = 2
    nc = hb * (t // tc)
    return pl.pallas_call(
        functools.partial(_fox_kernel, t=t, tc=tc, tg=tg),
        grid=(H // hb, S // t),
        in_specs=[pl.BlockSpec((hb, KA, t), lambda h, i: (h, 0, i)),
                  pl.BlockSpec((hb, S, KA), lambda h, i: (h, 0, 0)),
                  pl.BlockSpec((hb, Dh, S), lambda h, i: (h, 0, 0))],
        out_specs=pl.BlockSpec((hb, Dh, t), lambda h, i: (h, 0, i)),
        out_shape=jax.ShapeDtypeStruct((H, Dh, S), BF16),
        scratch_shapes=[pltpu.VMEM((nc, tg, tc), F32), pltpu.VMEM((nc, tg, tc), F32),
                        pltpu.VMEM((nc, 1, tc), F32), pltpu.VMEM((nc, 1, tc), F32),
                        pltpu.VMEM((nc, Dh, tc), F32)],
        compiler_params=_params("arbitrary", "arbitrary"),
        name="fox_attention",
    )(qaT, ka, vT)


DSA_T = 128


def _dsa_kernel(qiT_ref, w_ref, qT_ref, wukT_ref, wuvT_ref, ki_ref, c_ref, cT_ref, o_ref,
                keys_sc, qlat_sc, m_sc, l_sc, acc_sc, lga_sc, lgb_sc, rela_sc, relb_sc,
                *, t, tg, topk):
    i = pl.program_id(0)
    last = (i * t) // tg
    ng = last + 1

    qiT = qiT_ref[0]
    w = w_ref[0]

    def rel_into(dst_ref, g):
        off = pl.multiple_of(jnp.minimum(g, last) * tg, tg)
        dst_ref[...] = jnp.dot(ki_ref[pl.ds(off, tg), :], qiT, preferred_element_type=F32)

    def score_keys(rel_ref, g, masked):
        off = pl.multiple_of(g * tg, tg)
        sc = jnp.maximum(rel_ref[:, 0:t], 0.0) * w[0:1, :]
        for h in range(1, IDX_HEADS):
            sc = sc + jnp.maximum(rel_ref[:, h * t:(h + 1) * t], 0.0) * w[h:h + 1, :]
        bits = pltpu.bitcast(sc, I32)
        key = bits ^ ((bits >> 31) & 0x7FFFFFFF)
        key = jnp.where(sc == 0.0, 0, key)
        if masked:
            kchunk = (off + lax.broadcasted_iota(I32, (tg, t), 0)) // CHUNK
            qchunk = (i * t + lax.broadcasted_iota(I32, (tg, t), 1)) // CHUNK
            key = jnp.where(kchunk <= qchunk, key, INT_MIN)
        keys_sc[pl.ds(off, tg), :] = key

    def fill_pair(j, c):
        rel_into(relb_sc, 2 * j + 1)
        score_keys(rela_sc, 2 * j, False)
        rel_into(rela_sc, 2 * j + 2)
        score_keys(relb_sc, 2 * j + 1, False)
        return c

    rel_into(rela_sc, 0)
    lax.fori_loop(0, last // 2, fill_pair, 0)
    tail = 2 * (last // 2)
    rel_into(relb_sc, tail + 1)
    score_keys(rela_sc, tail, True)

    @pl.when(tail + 1 <= last)
    def _():
        score_keys(relb_sc, tail + 1, True)

    def count_ge(cand):
        cb = jnp.broadcast_to(cand, (t, t))

        def body(g, acc):
            off = pl.multiple_of(g * tg, tg)
            for k in range(tg // t):
                acc = acc + (keys_sc[pl.ds(off + k * t, t), :] >= cb).astype(F32)
            return acc

        acc = lax.fori_loop(0, ng, body, jnp.zeros((t, t), F32))
        return jnp.sum(acc, axis=0, keepdims=True)

    def bisect(s, st):
        thr, n_ge = st
        cand = thr + jnp.left_shift(jnp.int32(1), 31 - s)
        c = count_ge(cand)
        keep = c >= topk
        return jnp.where(keep, cand, thr), jnp.where(keep, c, n_ge)

    thr, n_ge = lax.fori_loop(0, 32, bisect,
                              (jnp.full((1, t), INT_MIN, I32), jnp.zeros((1, t), F32)))
    thr = jnp.maximum(thr, INT_MIN + 1)
    has_ties = jnp.max(jnp.where(n_ge > topk, 1.0, 0.0)) > 0.0
    thr_b = jnp.broadcast_to(thr, (tg, t))

    for h in range(DSA_HEADS):
        ql = jnp.dot(wukT_ref[h], qT_ref[0, h], preferred_element_type=F32)
        qlat_sc[:, h * t:(h + 1) * t] = (ql * (DSA_HEAD_DIM ** -0.5)).astype(BF16)
    m_sc[...] = jnp.full(m_sc.shape, NEG, F32)
    l_sc[...] = jnp.zeros(l_sc.shape, F32)
    acc_sc[...] = jnp.zeros(acc_sc.shape, F32)

    def logits_into(dst_ref, g):
        off = pl.multiple_of(jnp.minimum(g, last) * tg, tg)
        dst_ref[...] = jnp.dot(c_ref[pl.ds(off, tg), :], qlat_sc[...],
                               preferred_element_type=F32)

    def attend(lg_ref, g, seen_eq, with_ties, need):
        valid = g <= last
        off = pl.multiple_of(jnp.minimum(g, last) * tg, tg)
        key = keys_sc[pl.ds(off, tg), :]
        if with_ties:
            eq = (key == thr_b) & valid
            eqf = eq.astype(F32)
            strict_lower = (lax.broadcasted_iota(I32, (tg, tg), 0)
                            > lax.broadcasted_iota(I32, (tg, tg), 1)).astype(BF16)
            rank = jnp.dot(strict_lower, eqf.astype(BF16), preferred_element_type=F32) + seen_eq
            sel = (key > thr_b) | (eq & (rank < need))
            seen_eq = seen_eq + jnp.sum(eqf, axis=0, keepdims=True)
        else:
            sel = key >= thr_b
        bias = jnp.where(sel & valid, 0.0, NEG)
        cT_tile = cT_ref[:, pl.ds(off, tg)]
        for h in range(DSA_HEADS):
            s = lg_ref[:, h * t:(h + 1) * t] + bias
            m_prev = m_sc[h]
            m_new = jnp.maximum(m_prev, jnp.max(s, axis=0, keepdims=True))
            alpha = jnp.exp(m_prev - m_new)
            p = jnp.exp(s - m_new)
            l_sc[h] = alpha * l_sc[h] + jnp.sum(p, axis=0, keepdims=True)
            acc_sc[h] = alpha * acc_sc[h] + jnp.dot(cT_tile, p.astype(BF16),
                                                    preferred_element_type=F32)
            m_sc[h] = m_new
        return seen_eq

    def sweep(with_ties, need=None):
        logits_into(lga_sc, 0)

        def pair(j, seen_eq):
            g0 = 2 * j
            logits_into(lgb_sc, g0 + 1)
            seen_eq = attend(lga_sc, g0, seen_eq, with_ties, need)
            logits_into(lga_sc, g0 + 2)
            return attend(lgb_sc, g0 + 1, seen_eq, with_ties, need)

        lax.fori_loop(0, last // 2 + 1, pair, jnp.zeros((1, t), F32))

    @pl.when(has_ties)
    def _():
        sweep(True, topk - count_ge(thr + 1))

    @pl.when(jnp.logical_not(has_ties))
    def _():
        sweep(False)

    for h in range(DSA_HEADS):
        o_lat = (acc_sc[h] / l_sc[h]).astype(BF16)
        o_ref[0, h] = jnp.dot(wuvT_ref[h], o_lat, preferred_element_type=F32).astype(o_ref.dtype)


def _dsa_attention(qiT, w, qT, wukT, wuvT, ki, c, cT, topk):
    NB, H, Dh, t = qT.shape
    S = NB * t
    R = DSA_KV_RANK
    tg = min(512, S)
    const2 = lambda i: (0, 0)
    const3 = lambda i: (0, 0, 0)
    return pl.pallas_call(
        functools.partial(_dsa_kernel, t=t, tg=tg, topk=topk),
        grid=(NB,),
        in_specs=[pl.BlockSpec((1, IDX_HEAD_DIM, IDX_HEADS * t), lambda i: (i, 0, 0)),
                  pl.BlockSpec((1, IDX_HEADS, t), lambda i: (i, 0, 0)),
                  pl.BlockSpec((1, H, Dh, t), lambda i: (i, 0, 0, 0)),
                  pl.BlockSpec((H, R, Dh), const3),
                  pl.BlockSpec((H, Dh, R), const3),
                  pl.BlockSpec((S, IDX_HEAD_DIM), const2, pipeline_mode=pl.Buffered(1)),
                  pl.BlockSpec((S, R), const2, pipeline_mode=pl.Buffered(1)),
                  pl.BlockSpec((R, S), const2, pipeline_mode=pl.Buffered(1))],
        out_specs=pl.BlockSpec((1, H, Dh, t), lambda i: (i, 0, 0, 0)),
        out_shape=jax.ShapeDtypeStruct((NB, H, Dh, t), BF16),
        scratch_shapes=[pltpu.VMEM((S, t), I32), pltpu.VMEM((R, H * t), BF16),
                        pltpu.VMEM((H, 1, t), F32), pltpu.VMEM((H, 1, t), F32),
                        pltpu.VMEM((H, R, t), F32),
                        pltpu.VMEM((tg, H * t), F32), pltpu.VMEM((tg, H * t), F32),
                        pltpu.VMEM((tg, IDX_HEADS * t), F32), pltpu.VMEM((tg, IDX_HEADS * t), F32)],
        compiler_params=_params("arbitrary"),
        name="dsa_attention",
    )(qiT, w, qT, wukT, wuvT, ki, c, cT)


def _shift_mix(p, prev_tail, mu, first_block):
    rows = lax.broadcasted_iota(I32, p.shape, 0)
    tail = jnp.where(first_block, 0.0, prev_tail)
    prev = jnp.where(rows == 0, tail, pltpu.roll(p, 1, axis=0))
    return p + (prev - p) * mu


def _split3(f):
    hi = f.astype(BF16)
    r1 = f - hi.astype(F32)
    mid = r1.astype(BF16)
    lo = (r1 - mid.astype(F32)).astype(BF16)
    return hi, mid, lo


def _dot3(x, w01):
    return sum(jnp.dot(part, w01, preferred_element_type=F32) for part in _split3(x))


def _dot3_left(w01, x):
    return sum(jnp.dot(w01, part, preferred_element_type=F32) for part in _split3(x))


def _rwkv_pre_kernel(*refs, with_vres):
    (p_ref, pp_ref, mu_ref, w0_ref, a0_ref, wwa_ref, gup_ref, kk_ref, ka_ref, rk_ref,
     tri_ref, blk_ref, bd_ref) = refs[:13]
    if with_vres:
        vup_ref, vv0_ref, vf_ref = refs[13:16]
    (at_o, rt_o, bt_o, kt_o, bh_o, kh_o, vb_o, pc_o, bonus_o, g_o, v_o) = refs[-11:]
    first = pl.program_id(0) == 0
    W = RWKV_W
    ps = _shift_mix(p_ref[...], pp_ref[7:8, :], mu_ref[...], first)
    r, k, v = ps[:, 0:W], ps[:, W:2 * W], ps[:, 2 * W:3 * W]
    wa = ps[:, 3 * W:3 * W + LANE]
    gl = ps[:, 3 * W + LANE:3 * W + 2 * LANE]
    lane = lax.broadcasted_iota(I32, wa.shape, 1)
    wa = jnp.where(lane < RWKV_W_LORA, jnp.tanh(wa), wa)
    up = jnp.dot(wa.astype(BF16), wwa_ref[...], preferred_element_type=F32)
    log_w = -_softplus(-(w0_ref[...] + up[:, 0:W])) - 0.5
    a = _sigmoid(a0_ref[...] + up[:, W:2 * W])
    g = jnp.dot(_sigmoid(gl).astype(BF16), gup_ref[...], preferred_element_type=F32)
    if with_vres:
        vl = ps[:, RWKV_IN:RWKV_IN + LANE]
        logit = vv0_ref[...] + jnp.dot(vl.astype(BF16), vup_ref[...], preferred_element_type=F32)
        v = v + (vf_ref[...] - v) * _sigmoid(logit)
    k2 = k * (1.0 + (a - 1.0) * ka_ref[...])
    kkr = k * kk_ref[...]
    bd = bd_ref[...]
    kk = kkr * lax.rsqrt(_dot3(kkr * kkr, bd) + 1e-12)
    lw = -jnp.exp(log_w)
    cum = _dot3_left(tri_ref[...], lw)
    cend = _dot3_left(blk_ref[...], lw)
    p_inv = jnp.exp(-cum)
    p_end = jnp.exp(cend - cum)
    beta = kk * a
    at_o[...] = (-kk * jnp.exp(cum - lw)).astype(BF16)
    rt_o[...] = (r * jnp.exp(cum)).astype(BF16)
    bt_o[...] = (beta * p_inv).astype(BF16)
    kt_o[...] = (k2 * p_inv).astype(BF16)
    bh_o[...] = (beta * p_end).astype(BF16)
    kh_o[...] = (k2 * p_end).astype(BF16)
    vb_o[...] = v.astype(BF16)
    pc_o[...] = jnp.exp(cend)
    bonus_o[...] = _dot3(r * k2 * rk_ref[...], bd) * v
    g_o[...] = g
    v_o[...] = v


def _block_ones(n, block, lower_tri=False):
    i = jnp.arange(n)
    m = (i[:, None] // block) == (i[None, :] // block)
    if lower_tri:
        m = m & (i[:, None] >= i[None, :])
    return m.astype(BF16)


def _rwkv_pre(p, mu, w0, a0, wwa, gup, k_k, k_a, r_k, vres=None):
    S, PW = p.shape
    W = RWKV_W
    tm = min(512, S)
    row = lambda i: (i, 0)
    const = lambda i: (0, 0)
    tail = lambda i: (jnp.maximum(i * (tm // 8) - 1, 0), 0)
    vec = lambda a: a.reshape(1, -1)
    args = [p, p, vec(mu), vec(w0), vec(a0), wwa, gup, vec(k_k), vec(k_a), vec(r_k),
            _block_ones(tm, CHUNK, lower_tri=True), _block_ones(tm, CHUNK),
            _block_ones(W, RWKV_HEAD_DIM)]
    specs = [pl.BlockSpec((tm, PW), row), pl.BlockSpec((8, PW), tail),
             pl.BlockSpec((1, PW), const), pl.BlockSpec((1, W), const), pl.BlockSpec((1, W), const),
             pl.BlockSpec(wwa.shape, const), pl.BlockSpec(gup.shape, const),
             pl.BlockSpec((1, W), const), pl.BlockSpec((1, W), const), pl.BlockSpec((1, W), const),
             pl.BlockSpec((tm, tm), const), pl.BlockSpec((tm, tm), const),
             pl.BlockSpec((W, W), const)]
    if vres is not None:
        vup, vv0, v_first = vres
        args += [vup, vec(vv0), v_first]
        specs += [pl.BlockSpec(vup.shape, const), pl.BlockSpec((1, W), const),
                  pl.BlockSpec((tm, W), row)]
    return pl.pallas_call(
        functools.partial(_rwkv_pre_kernel, with_vres=vres is not None),
        grid=(S // tm,),
        in_specs=specs,
        out_specs=[pl.BlockSpec((tm, W), row)] * 11,
        out_shape=[jax.ShapeDtypeStruct((S, W), BF16)] * 7 + [jax.ShapeDtypeStruct((S, W), F32)] * 4,
        compiler_params=_params("arbitrary"),
        name="rwkv_pre",
    )(*args)


def _rwkv_scan_kernel(at_ref, rt_ref, bt_ref, kt_ref, bh_ref, kh_ref, v_ref, pc_ref, y_ref, h_sc,
                      *, chunks):
    C = CHUNK
    N = RWKV_HEAD_DIM

    @pl.when(pl.program_id(0) == 0)
    def _():
        h_sc[...] = jnp.zeros(h_sc.shape, F32)

    ti = lax.broadcasted_iota(I32, (C, C), 0)
    tj = lax.broadcasted_iota(I32, (C, C), 1)
    lower_incl = ti >= tj
    lower_strict = ti > tj
    eye = (ti == tj).astype(F32)
    eye_n = (lax.broadcasted_iota(I32, (N, N), 0) == lax.broadcasted_iota(I32, (N, N), 1)).astype(F32)

    def mm(x, y):
        return jnp.dot(x.astype(BF16), y.astype(BF16), preferred_element_type=F32)

    def mm_nt(x, y):
        return lax.dot_general(x.astype(BF16), y.astype(BF16), (((1,), (1,)), ((), ())),
                               preferred_element_type=F32)

    def mm_tn(x, y):
        return lax.dot_general(x.astype(BF16), y.astype(BF16), (((0,), (0,)), ((), ())),
                               preferred_element_type=F32)

    units = [(c, h) for c in range(chunks) for h in range(RWKV_HEADS)]
    tile = lambda ref, u: ref[u[0] * C:(u[0] + 1) * C, u[1] * N:(u[1] + 1) * N]
    each = lambda fn: {u: fn(u) for u in units}

    At, Rt, Bt, Kt = (each(lambda u, r=ref: tile(r, u)) for ref in (at_ref, rt_ref, bt_ref, kt_ref))
    Bh, Kh, V = (each(lambda u, r=ref: tile(r, u)) for ref in (bh_ref, kh_ref, v_ref))
    AR = each(lambda u: jnp.concatenate([At[u], Rt[u]], axis=0))
    Mb = each(lambda u: mm_nt(AR[u], Bt[u]))
    Mk = each(lambda u: mm_nt(AR[u], Kt[u]))
    Lab = each(lambda u: jnp.where(lower_strict, Mb[u][0:C], 0.0))
    Mrb = each(lambda u: jnp.where(lower_incl, Mb[u][C:2 * C], 0.0))
    Lak = each(lambda u: jnp.where(lower_strict, Mk[u][0:C], 0.0))
    Mrk = each(lambda u: jnp.where(lower_incl, Mk[u][C:2 * C], 0.0))
    T = each(lambda u: eye + Lab[u])
    Lp = Lab
    span = 2
    while span < C:
        Lp = each(lambda u, Lp=Lp: mm(Lp[u], Lp[u]))
        T = each(lambda u, T=T, Lp=Lp: T[u] + mm(Lp[u], T[u]))
        span *= 2
    W1 = each(lambda u: mm(Lak[u], V[u]))
    A2 = each(lambda u: mm(T[u], At[u]))
    U0 = each(lambda u: mm(T[u], W1[u]))
    R2 = each(lambda u: Rt[u].astype(F32) + mm(Mrb[u], A2[u]))
    Y0 = each(lambda u: mm(Mrb[u], U0[u]) + mm(Mrk[u], V[u]))
    G = each(lambda u: eye_n * tile(pc_ref, u)[0:1, :] + mm_tn(Bh[u], A2[u]))
    H0 = each(lambda u: mm_tn(Bh[u], U0[u]) + mm_tn(Kh[u], V[u]))

    H = {h: h_sc[h] for h in range(RWKV_HEADS)}
    for c in range(chunks):
        ys = []
        for h in range(RWKV_HEADS):
            u = (c, h)
            ys.append(mm(R2[u], H[h]) + Y0[u])
            H[h] = mm(G[u], H[h]) + H0[u]
        y_ref[c * C:(c + 1) * C, :] = jnp.concatenate(ys, axis=1)
    for h in range(RWKV_HEADS):
        h_sc[h] = H[h]


def _rwkv_scan(at, rt, bt, kt, bh, kh, vb, pc):
    S, W = at.shape
    chunks = 2 if S % (2 * CHUNK) == 0 else 1
    tb = chunks * CHUNK
    seq = pl.BlockSpec((tb, W), lambda i: (i, 0))
    return pl.pallas_call(
        functools.partial(_rwkv_scan_kernel, chunks=chunks),
        grid=(S // tb,),
        in_specs=[seq] * 8,
        out_specs=seq,
        out_shape=jax.ShapeDtypeStruct((S, W), F32),
        scratch_shapes=[pltpu.VMEM((RWKV_HEADS, RWKV_HEAD_DIM, RWKV_HEAD_DIM), F32)],
        compiler_params=_params("arbitrary"),
        name="rwkv_scan",
    )(at, rt, bt, kt, bh, kh, vb, pc)


def _merge_kernel(x_ref, of_ref, od_ref, y_ref, bonus_ref, g_ref, lng_ref, lnb_ref, bd_ref,
                  gp_ref, bg_ref, pf_ref, pd_ref, pr_ref, wo_ref, gn_ref, xo_ref, h_ref):
    D = D_MODEL
    bd = bd_ref[...]
    inv_n = 1.0 / RWKV_HEAD_DIM
    y = y_ref[...]
    yc = y - _dot3(y, bd) * inv_n
    var = _dot3(yc * yc, bd) * inv_n
    yn = yc * lax.rsqrt(var + RWKV_LN_EPS) * lng_ref[...] + lnb_ref[...]
    o_rwkv = ((yn + bonus_ref[...]) * g_ref[...]).astype(BF16)
    gates = _sigmoid(gp_ref[...] + bg_ref[...])
    merged = (gates[:, 0:D] * jnp.dot(of_ref[...], pf_ref[...], preferred_element_type=F32)
              + gates[:, D:2 * D] * jnp.dot(od_ref[...], pd_ref[...], preferred_element_type=F32)
              + gates[:, 2 * D:3 * D] * jnp.dot(o_rwkv, pr_ref[...], preferred_element_type=F32))
    x = x_ref[...] + jnp.dot(merged.astype(BF16), wo_ref[...], preferred_element_type=F32)
    xo_ref[...] = x
    h_ref[...] = _rms(x, gn_ref[...]).astype(h_ref.dtype)


def _merge(x, o_fox, o_dsa, y_rwkv, bonus, g_rwkv, ln_g, ln_b, gate_p, b_gate, p_fox, p_dsa, p_rwkv,
           w_out, g_ffn):
    S, D = x.shape
    W = RWKV_W
    tm = min(512, S)
    row = lambda i: (i, 0)
    const = lambda i: (0, 0)
    return pl.pallas_call(
        _merge_kernel,
        grid=(S // tm,),
        in_specs=[pl.BlockSpec((tm, D), row), pl.BlockSpec((tm, FOX_W), row),
                  pl.BlockSpec((tm, DSA_W), row), pl.BlockSpec((tm, W), row),
                  pl.BlockSpec((tm, W), row), pl.BlockSpec((tm, W), row),
                  pl.BlockSpec((1, W), const), pl.BlockSpec((1, W), const),
                  pl.BlockSpec((W, W), const),
                  pl.BlockSpec((tm, 3 * D), row), pl.BlockSpec((1, 3 * D), const),
                  pl.BlockSpec((FOX_W, D), const), pl.BlockSpec((DSA_W, D), const),
                  pl.BlockSpec((W, D), const), pl.BlockSpec((D, D), const),
                  pl.BlockSpec((1, D), const)],
        out_specs=[pl.BlockSpec((tm, D), row), pl.BlockSpec((tm, D), row)],
        out_shape=[jax.ShapeDtypeStruct((S, D), F32), jax.ShapeDtypeStruct((S, D), BF16)],
        compiler_params=_params("arbitrary"),
        name="merge",
    )(x, o_fox, o_dsa, y_rwkv, bonus, g_rwkv, ln_g.reshape(1, W), ln_b.reshape(1, W),
      _block_ones(W, RWKV_HEAD_DIM), gate_p, b_gate.reshape(1, -1), p_fox, p_dsa, p_rwkv, w_out,
      g_ffn.reshape(1, D))


def _swiglu_tile(h, wg, wu):
    gate = jnp.dot(h, wg, preferred_element_type=F32)
    up = jnp.dot(h, wu, preferred_element_type=F32)
    return gate * _sigmoid(gate) * up


def _ffn_kernel(x_ref, h_ref, wg_ref, wu_ref, wd_ref, gf_ref, o_ref, acc_sc, *, final_norm):
    f = pl.program_id(1)

    @pl.when(f == 0)
    def _():
        acc_sc[...] = x_ref[...]

    act = _swiglu_tile(h_ref[...], wg_ref[...], wu_ref[...])
    acc_sc[...] += jnp.dot(act.astype(BF16), wd_ref[...], preferred_element_type=F32)

    @pl.when(f == pl.num_programs(1) - 1)
    def _():
        y = acc_sc[...]
        o_ref[...] = _rms(y, gf_ref[...]) if final_norm else y


def _ffn(x, h, wg, wu, wd, g_final, final_norm):
    S, D = x.shape
    Fd = wg.shape[1]
    tm = min(512, S)
    tf = _pick_tile(Fd, 1408)
    return pl.pallas_call(
        functools.partial(_ffn_kernel, final_norm=final_norm),
        grid=(S // tm, Fd // tf),
        in_specs=[pl.BlockSpec((tm, D), lambda i, f: (i, 0)),
                  pl.BlockSpec((tm, D), lambda i, f: (i, 0)),
                  pl.BlockSpec((D, tf), lambda i, f: (0, f)),
                  pl.BlockSpec((D, tf), lambda i, f: (0, f)),
                  pl.BlockSpec((tf, D), lambda i, f: (f, 0)),
                  pl.BlockSpec((1, D), lambda i, f: (0, 0))],
        out_specs=pl.BlockSpec((tm, D), lambda i, f: (i, 0)),
        out_shape=jax.ShapeDtypeStruct((S, D), F32),
        scratch_shapes=[pltpu.VMEM((tm, D), F32)],
        compiler_params=_params("arbitrary", "arbitrary"),
        name="ffn",
    )(x, h, wg, wu, wd, g_final.reshape(1, D))


def _moe_kernel(x_ref, h_ref, rw_ref, rb_ref, wg_ref, wu_ref, wd_ref, gf_ref, o_ref,
                acc_sc, gate_sc, *, final_norm):
    e = pl.program_id(1)
    f = pl.program_id(2)
    lane = lax.broadcasted_iota(I32, gate_sc.shape, 1)

    @pl.when((e == 0) & (f == 0))
    def _():
        acc_sc[...] = x_ref[...]
        logits = jnp.dot(h_ref[...], rw_ref[...], preferred_element_type=F32) + rb_ref[...]
        logits = jnp.where(lane < N_EXPERTS, logits, -jnp.inf)
        v1 = jnp.max(logits, axis=-1, keepdims=True)
        i1 = jnp.min(jnp.where(logits == v1, lane, LANE), axis=-1, keepdims=True)
        rest = jnp.where(lane == i1, -jnp.inf, logits)
        v2 = jnp.max(rest, axis=-1, keepdims=True)
        i2 = jnp.min(jnp.where(rest == v2, lane, LANE), axis=-1, keepdims=True)
        e2 = jnp.exp(v2 - v1)
        p1 = 1.0 / (1.0 + e2)
        gate_sc[...] = jnp.where(lane == i1, p1, 0.0) + jnp.where(lane == i2, e2 * p1, 0.0)

    gate = jnp.sum(jnp.where(lane == e, gate_sc[...], 0.0), axis=-1, keepdims=True)
    act = _swiglu_tile(h_ref[...], wg_ref[0], wu_ref[0]) * gate
    acc_sc[...] += jnp.dot(act.astype(BF16), wd_ref[0], preferred_element_type=F32)

    @pl.when((e == pl.num_programs(1) - 1) & (f == pl.num_programs(2) - 1))
    def _():
        y = acc_sc[...]
        o_ref[...] = _rms(y, gf_ref[...]) if final_norm else y


def _moe(x, h, rw, rb, wg, wu, wd, g_final, final_norm):
    S, D = x.shape
    E, _, Fe = wg.shape
    tm = min(1024, S)
    tf = _pick_tile(Fe, 896)
    return pl.pallas_call(
        functools.partial(_moe_kernel, final_norm=final_norm),
        grid=(S // tm, E, Fe // tf),
        in_specs=[pl.BlockSpec((tm, D), lambda i, e, f: (i, 0)),
                  pl.BlockSpec((tm, D), lambda i, e, f: (i, 0)),
                  pl.BlockSpec((D, LANE), lambda i, e, f: (0, 0)),
                  pl.BlockSpec((1, LANE), lambda i, e, f: (0, 0)),
                  pl.BlockSpec((1, D, tf), lambda i, e, f: (e, 0, f)),
                  pl.BlockSpec((1, D, tf), lambda i, e, f: (e, 0, f)),
                  pl.BlockSpec((1, tf, D), lambda i, e, f: (e, f, 0)),
                  pl.BlockSpec((1, D), lambda i, e, f: (0, 0))],
        out_specs=pl.BlockSpec((tm, D), lambda i, e, f: (i, 0)),
        out_shape=jax.ShapeDtypeStruct((S, D), F32),
        scratch_shapes=[pltpu.VMEM((tm, D), F32), pltpu.VMEM((tm, LANE), F32)],
        compiler_params=_params("arbitrary", "arbitrary", "arbitrary"),
        name="moe",
    )(x, h, rw, rb, wg, wu, wd, g_final.reshape(1, D))


def _cast_kernel(x_ref, o_ref):
    o_ref[...] = x_ref[...].astype(o_ref.dtype)


def _to_bf16(w):
    E, A, B = w.shape
    ta = 256
    return pl.pallas_call(
        _cast_kernel,
        grid=(E, A // ta),
        in_specs=[pl.BlockSpec((1, ta, B), lambda e, a: (e, a, 0))],
        out_specs=pl.BlockSpec((1, ta, B), lambda e, a: (e, a, 0)),
        out_shape=jax.ShapeDtypeStruct(w.shape, BF16),
        compiler_params=_params("arbitrary", "arbitrary"),
        name="to_bf16",
    )(w)


def _pad_cols(a, n):
    return jnp.pad(a, ((0, 0), (0, n - a.shape[1])))


def kernel(x, w_in, b_gate, g_mix, fox_f_bias, dsa_kv_norm, dsa_w_uk, dsa_w_uv, rwkv_mu, rwkv_w0, rwkv_w_up, rwkv_a0, rwkv_a_up, rwkv_g_up, rwkv_k_k, rwkv_k_a, rwkv_r_k, rwkv_ln_g, rwkv_ln_b, vres_down, vres_mu, vres_up, vres_v0, p_fox, p_dsa, p_rwkv, w_out, g_ffn, ffn_w_gate, ffn_w_up, ffn_w_down, router_w, router_b, moe_w_gate, moe_w_up, moe_w_down, g_final):
    B, S, D = x.shape
    assert B == 1 and D == D_MODEL and S % LANE == 0
    depth = w_in.shape[0]
    topk = min(IDX_TOPK, S // 4)
    bf = lambda a: a.astype(BF16)
    xs = x[0]
    v_first = None
    w_in_bf = _to_bf16(w_in)
    for l in range(depth):
        wl = w_in_bf[l]
        w_fox, w_dsa, w_rwkv, w_gate = (wl[:, :FOX_IN], wl[:, FOX_IN:FOX_IN + DSA_IN],
                                        wl[:, FOX_IN + DSA_IN:FOX_IN + DSA_IN + RWKV_IN],
                                        wl[:, FOX_IN + DSA_IN + RWKV_IN:])
        o1 = DSA_W + DSA_KV_RANK + IDX_W
        w_att = jnp.concatenate([_pad_cols(w_fox, 3 * FOX_W + LANE), w_dsa[:, :o1],
                                 _pad_cols(w_dsa[:, o1:], LANE)], axis=1)
        if l > 0:
            w_rwkv = jnp.concatenate([w_rwkv, bf(_pad_cols(vres_down[l - 1], LANE))], axis=1)
        pa = _rms_proj(xs, g_mix[l], w_att)
        pr = _rms_proj(xs, g_mix[l], w_rwkv)
        gate_p = _rms_proj(xs, g_mix[l], w_gate)

        c0 = 3 * FOX_W
        fl = pa[:, c0:c0 + FOX_HEADS].T.reshape(FOX_HEADS, S // LANE, LANE)
        Fh, Fm, Fl = (a.reshape(FOX_HEADS, S) for a in _fox_cumsum(fl, fox_f_bias[l]))
        qT = bf(pa[:, 0:FOX_W].T.reshape(FOX_HEADS, FOX_HEAD_DIM, S) * (FOX_HEAD_DIM ** -0.5))
        r = jnp.arange(FOX_KA - FOX_HEAD_DIM)
        rq, ck = r[None, :, None], r[None, None, :]
        pick = lambda idx, a, b, c, d: jnp.where(idx == a[0], a[1], jnp.where(
            idx == b[0], b[1], jnp.where(idx == c[0], c[1], jnp.where(idx < 6, d, 0)))).astype(BF16)
        fq = pick(rq, (0, Fh[:, None, :]), (1, Fm[:, None, :]), (2, Fl[:, None, :]), 1)
        fk = pick(ck, (3, -Fh[:, :, None]), (4, -Fm[:, :, None]), (5, -Fl[:, :, None]), 1)
        qaT = jnp.concatenate([qT, fq], axis=1)
        k4 = bf(pa[:, FOX_W:2 * FOX_W].reshape(S, FOX_HEADS, FOX_HEAD_DIM).transpose(1, 0, 2))
        ka = jnp.concatenate([k4, fk], axis=2)
        fvT = bf(pa[:, 2 * FOX_W:3 * FOX_W].T.reshape(FOX_HEADS, FOX_HEAD_DIM, S))
        o_fox = _fox_attention(qaT, ka, fvT).reshape(FOX_W, S).T

        t = min(DSA_T, S)
        nb = S // t
        c1 = c0 + LANE
        dqT = bf(pa[:, c1:c1 + DSA_W].T.reshape(DSA_HEADS, DSA_HEAD_DIM, nb, t).transpose(2, 0, 1, 3))
        c2 = c1 + DSA_W
        ckv = _rmsnorm(pa[:, c2:c2 + DSA_KV_RANK], dsa_kv_norm[l], BF16)
        c3 = c2 + DSA_KV_RANK
        qiT = bf(pa[:, c3:c3 + IDX_W].T.reshape(IDX_HEADS, IDX_HEAD_DIM, nb, t).transpose(2, 1, 0, 3)
                 .reshape(nb, IDX_HEAD_DIM, IDX_HEADS * t))
        c4 = c3 + IDX_W
        ki = bf(pa[:, c4:c4 + IDX_HEAD_DIM])
        c5 = c4 + IDX_HEAD_DIM
        wi = pa[:, c5:c5 + IDX_HEADS].T.reshape(IDX_HEADS, nb, t).transpose(1, 0, 2) * (IDX_W ** -0.5)
        o_dsa = _dsa_attention(qiT, wi, dqT, bf(dsa_w_uk[l].transpose(0, 2, 1)),
                               bf(dsa_w_uv[l].transpose(0, 2, 1)), ki, ckv, ckv.T, topk)
        o_dsa = o_dsa.transpose(1, 2, 0, 3).reshape(DSA_W, S).T

        zw = jnp.zeros((RWKV_W_LORA, RWKV_W), F32)
        wwa = bf(jnp.concatenate([jnp.concatenate([rwkv_w_up[l], zw], axis=1),
                                  jnp.concatenate([zw, rwkv_a_up[l]], axis=1)], axis=0))
        vres = None
        mu = rwkv_mu[l]
        if l > 0:
            vup = jnp.pad(vres_up[l - 1], ((0, LANE - RWKV_V_LORA), (0, 0)))
            vres = (bf(vup), vres_v0[l - 1], v_first)
            mu = jnp.concatenate([mu, jnp.pad(vres_mu[l - 1], (0, LANE - RWKV_V_LORA))])
        *scan_ops, bonus, g_rwkv, v = _rwkv_pre(
            pr, mu, rwkv_w0[l], rwkv_a0[l], wwa, bf(rwkv_g_up[l]),
            rwkv_k_k[l], rwkv_k_a[l], rwkv_r_k[l], vres)
        if l == 0:
            v_first = v
        y_rwkv = _rwkv_scan(*scan_ops)

        xs, h2 = _merge(xs, o_fox, o_dsa, y_rwkv, bonus, g_rwkv, rwkv_ln_g[l], rwkv_ln_b[l],
                        gate_p, b_gate[l], bf(p_fox[l]), bf(p_dsa[l]), bf(p_rwkv[l]), bf(w_out[l]),
                        g_ffn[l])

        last = l == depth - 1
        if l % 2 == 0:
            xs = _ffn(xs, h2, bf(ffn_w_gate[l // 2]), bf(ffn_w_up[l // 2]), bf(ffn_w_down[l // 2]),
                      g_final, last)
        else:
            rw = bf(_pad_cols(router_w[l // 2], LANE))
            rb = _pad_cols(router_b[l // 2].reshape(1, -1), LANE)
            xs = _moe(xs, h2, rw, rb, _to_bf16(moe_w_gate[l // 2]), _to_bf16(moe_w_up[l // 2]),
                      _to_bf16(moe_w_down[l // 2]), g_final, last)
    return xs[None]
```

```python
import functools

import jax
import jax.numpy as jnp
from jax import lax
from jax.experimental import pallas as pl
from jax.experimental.pallas import tpu as pltpu

F32 = jnp.float32
BF16 = jnp.bfloat16
I32 = jnp.int32

D_MODEL = 1024
CHUNK = 64
RMS_EPS = 1e-6
FOX_HEADS, FOX_HEAD_DIM = 4, 64
DSA_HEADS, DSA_HEAD_DIM, DSA_KV_RANK = 4, 64, 128
IDX_HEADS, IDX_HEAD_DIM, IDX_TOPK = 8, 32, 256
RWKV_HEADS, RWKV_HEAD_DIM = 8, 64
RWKV_W_LORA, RWKV_A_LORA, RWKV_V_LORA, RWKV_G_LORA = 64, 64, 32, 128
RWKV_LN_EPS = 64e-5
FOX_W = FOX_HEADS * FOX_HEAD_DIM
DSA_W = DSA_HEADS * DSA_HEAD_DIM
RWKV_W = RWKV_HEADS * RWKV_HEAD_DIM
IDX_W = IDX_HEADS * IDX_HEAD_DIM
N_EXPERTS = 8
FOX_IN = 3 * FOX_W + FOX_HEADS
DSA_IN = DSA_W + DSA_KV_RANK + IDX_W + IDX_HEAD_DIM + IDX_HEADS
RWKV_IN = 3 * RWKV_W + RWKV_W_LORA + RWKV_A_LORA + RWKV_G_LORA

LANE = 128
VMEM_LIMIT = 52 * 1024 * 1024
NEG = -1e30
INT_MIN = -(2 ** 31)
HI = lax.Precision.HIGHEST


def _params(*sem):
    return pltpu.CompilerParams(dimension_semantics=sem, vmem_limit_bytes=VMEM_LIMIT)


def _pick_tile(n, cap):
    best = LANE
    for t in range(LANE, min(n, cap) + 1, LANE):
        if n % t == 0:
            best = t
    return best


def _softplus(x):
    return jnp.maximum(x, 0.0) + jnp.log1p(jnp.exp(-jnp.abs(x)))


def _sigmoid(x):
    return 1.0 / (1.0 + jnp.exp(-x))


def _rms(x, g):
    return x * lax.rsqrt(jnp.mean(x * x, axis=-1, keepdims=True) + RMS_EPS) * g


def _rms_proj_kernel(x_ref, g_ref, w_ref, o_ref):
    h = _rms(x_ref[...], g_ref[...])
    o_ref[...] = jnp.dot(h.astype(BF16), w_ref[...], preferred_element_type=F32)


def _rms_proj(x, g, w):
    S, D = x.shape
    N = w.shape[1]
    tm = min(512, S)
    tn = _pick_tile(N, 2304)
    return pl.pallas_call(
        _rms_proj_kernel,
        grid=(N // tn, S // tm),
        in_specs=[pl.BlockSpec((tm, D), lambda j, i: (i, 0)),
                  pl.BlockSpec((1, D), lambda j, i: (0, 0)),
                  pl.BlockSpec((D, tn), lambda j, i: (0, j))],
        out_specs=pl.BlockSpec((tm, tn), lambda j, i: (i, j)),
        out_shape=jax.ShapeDtypeStruct((S, N), F32),
        compiler_params=_params("arbitrary", "arbitrary"),
        name="rms_proj",
    )(x, g.reshape(1, D), w)


def _rmsnorm_kernel(x_ref, g_ref, o_ref):
    o_ref[...] = _rms(x_ref[...], g_ref[...]).astype(o_ref.dtype)


def _rmsnorm(x, g, dtype):
    S, D = x.shape
    tm = min(2048, S)
    return pl.pallas_call(
        _rmsnorm_kernel,
        grid=(S // tm,),
        in_specs=[pl.BlockSpec((tm, D), lambda i: (i, 0)),
                  pl.BlockSpec((1, D), lambda i: (0, 0))],
        out_specs=pl.BlockSpec((tm, D), lambda i: (i, 0)),
        out_shape=jax.ShapeDtypeStruct((S, D), dtype),
        compiler_params=_params("arbitrary"),
        name="rmsnorm",
    )(x, g.reshape(1, D))


def _fox_cumsum_kernel(fl_ref, b_ref, hi_ref, mid_ref, lo_ref):
    H, R, _ = fl_ref.shape
    upper = (lax.broadcasted_iota(I32, (LANE, LANE), 0)
             <= lax.broadcasted_iota(I32, (LANE, LANE), 1)).astype(F32)
    strict_lower = (lax.broadcasted_iota(I32, (R, R), 0)
                    > lax.broadcasted_iota(I32, (R, R), 1)).astype(F32)
    for h in range(H):
        log_f = -_softplus(-(fl_ref[h] + b_ref[h]))
        within = jnp.dot(log_f, upper, preferred_element_type=F32, precision=HI)
        row_tot = jnp.broadcast_to(within[:, LANE - 1:LANE], (R, LANE))
        before = jnp.dot(strict_lower, row_tot, preferred_element_type=F32, precision=HI)
        hi, mid, lo = _split3(within + before)
        hi_ref[h], mid_ref[h], lo_ref[h] = hi, mid, lo


def _fox_cumsum(fl, bias):
    H, R, _ = fl.shape
    return pl.pallas_call(
        _fox_cumsum_kernel,
        out_shape=[jax.ShapeDtypeStruct((H, R, LANE), BF16)] * 3,
        compiler_params=pltpu.CompilerParams(vmem_limit_bytes=VMEM_LIMIT),
        name="fox_cumsum",
    )(fl, jnp.broadcast_to(bias.reshape(H, 1, 1), (H, 1, LANE)))


FOX_KA = 128


def _fox_kernel(qa_ref, ka_ref, vT_ref, o_ref, lga_sc, lgb_sc, m_sc, l_sc, acc_sc, *, t, tc, tg):
    i = pl.program_id(1)
    hb = qa_ref.shape[0]
    last = (i * t) // tg
    chains = [(h, q0) for h in range(hb) for q0 in range(0, t, tc)]

    def logits_into(dst_ref, g):
        off = pl.multiple_of(jnp.minimum(g, last) * tg, tg)
        for n, (h, q0) in enumerate(chains):
            dst_ref[n] = jnp.dot(ka_ref[h, pl.ds(off, tg), :], qa_ref[h, :, q0:q0 + tc],
                                 preferred_element_type=F32)

    def update(lg_ref, g, masked):
        off = pl.multiple_of(g * tg, tg)
        for n, (h, q0) in enumerate(chains):
            s = lg_ref[n]
            if masked:
                kpos = off + lax.broadcasted_iota(I32, (tg, tc), 0)
                qpos = i * t + q0 + lax.broadcasted_iota(I32, (tg, tc), 1)
                s = jnp.where(kpos <= qpos, s, NEG)
            m_prev = m_sc[n]
            m_new = jnp.maximum(m_prev, jnp.max(s, axis=0, keepdims=True))
            alpha = jnp.exp(m_prev - m_new)
            p = jnp.exp(s - m_new)
            l_sc[n] = alpha * l_sc[n] + jnp.sum(p, axis=0, keepdims=True)
            acc_sc[n] = alpha * acc_sc[n] + jnp.dot(vT_ref[h, :, pl.ds(off, tg)], p.astype(BF16),
                                                    preferred_element_type=F32)
            m_sc[n] = m_new

    m_sc[...] = jnp.full(m_sc.shape, NEG, F32)
    l_sc[...] = jnp.zeros(l_sc.shape, F32)
    acc_sc[...] = jnp.zeros(acc_sc.shape, F32)
    logits_into(lga_sc, 0)

    def pair(j, c):
        logits_into(lgb_sc, 2 * j + 1)
        update(lga_sc, 2 * j, False)
        logits_into(lga_sc, 2 * j + 2)
        update(lgb_sc, 2 * j + 1, False)
        return c

    lax.fori_loop(0, last // 2, pair, 0)
    tail = 2 * (last // 2)
    logits_into(lgb_sc, tail + 1)
    update(lga_sc, tail, True)

    @pl.when(tail + 1 <= last)
    def _():
        update(lgb_sc, tail + 1, True)

    for n, (h, q0) in enumerate(chains):
        o_ref[h, :, q0:q0 + tc] = (acc_sc[n] / l_sc[n]).astype(o_ref.dtype)


def _fox_attention(qaT, ka, vT):
    H, KA, S = qaT.shape
    Dh = vT.shape[1]
    t = min(512, S)
    tc = min(256, t)
    tg = min(1024, S)
    hb = 2
    nc = hb * (t // tc)
    return pl.pallas_call(
        functools.partial(_fox_kernel, t=t, tc=tc, tg=tg),
        grid=(H // hb, S // t),
        in_specs=[pl.BlockSpec((hb, KA, t), lambda h, i: (h, 0, i)),
                  pl.BlockSpec((hb, S, KA), lambda h, i: (h, 0, 0)),
                  pl.BlockSpec((hb, Dh, S), lambda h, i: (h, 0, 0))],
        out_specs=pl.BlockSpec((hb, Dh, t), lambda h, i: (h, 0, i)),
        out_shape=jax.ShapeDtypeStruct((H, Dh, S), BF16),
        scratch_shapes=[pltpu.VMEM((nc, tg, tc), F32), pltpu.VMEM((nc, tg, tc), F32),
                        pltpu.VMEM((nc, 1, tc), F32), pltpu.VMEM((nc, 1, tc), F32),
                        pltpu.VMEM((nc, Dh, tc), F32)],
        compiler_params=_params("arbitrary", "arbitrary"),
        name="fox_attention",
    )(qaT, ka, vT)


DSA_T = 128


def _dsa_kernel(qiT_ref, w_ref, qT_ref, wukT_ref, wuvT_ref, ki_ref, c_ref, cT_ref, o_ref,
                keys_sc, qlat_sc, m_sc, l_sc, acc_sc, lga_sc, lgb_sc, rela_sc, relb_sc,
                *, t, tg, topk):
    i = pl.program_id(0)
    last = (i * t) // tg
    ng = last + 1

    qiT = qiT_ref[0]
    w = w_ref[0]

    def rel_into(dst_ref, g):
        off = pl.multiple_of(jnp.minimum(g, last) * tg, tg)
        dst_ref[...] = jnp.dot(ki_ref[pl.ds(off, tg), :], qiT, preferred_element_type=F32)

    def score_keys(rel_ref, g, masked):
        off = pl.multiple_of(g * tg, tg)
        sc = jnp.maximum(rel_ref[:, 0:t], 0.0) * w[0:1, :]
        for h in range(1, IDX_HEADS):
            sc = sc + jnp.maximum(rel_ref[:, h * t:(h + 1) * t], 0.0) * w[h:h + 1, :]
        bits = pltpu.bitcast(sc, I32)
        key = bits ^ ((bits >> 31) & 0x7FFFFFFF)
        key = jnp.where(sc == 0.0, 0, key)
        if masked:
            kchunk = (off + lax.broadcasted_iota(I32, (tg, t), 0)) // CHUNK
            qchunk = (i * t + lax.broadcasted_iota(I32, (tg, t), 1)) // CHUNK
            key = jnp.where(kchunk <= qchunk, key, INT_MIN)
        keys_sc[pl.ds(off, tg), :] = key

    def fill_pair(j, c):
        rel_into(relb_sc, 2 * j + 1)
        score_keys(rela_sc, 2 * j, False)
        rel_into(rela_sc, 2 * j + 2)
        score_keys(relb_sc, 2 * j + 1, False)
        return c

    rel_into(rela_sc, 0)
    lax.fori_loop(0, last // 2, fill_pair, 0)
    tail = 2 * (last // 2)
    rel_into(relb_sc, tail + 1)
    score_keys(rela_sc, tail, True)

    @pl.when(tail + 1 <= last)
    def _():
        score_keys(relb_sc, tail + 1, True)

    def count_ge(cand):
        cb = jnp.broadcast_to(cand, (t, t))

        def body(g, acc):
            off = pl.multiple_of(g * tg, tg)
            for k in range(tg // t):
                acc = acc + (keys_sc[pl.ds(off + k * t, t), :] >= cb).astype(F32)
            return acc

        acc = lax.fori_loop(0, ng, body, jnp.zeros((t, t), F32))
        return jnp.sum(acc, axis=0, keepdims=True)

    def bisect(s, st):
        thr, n_ge = st
        cand = thr + jnp.left_shift(jnp.int32(1), 31 - s)
        c = count_ge(cand)
        keep = c >= topk
        return jnp.where(keep, cand, thr), jnp.where(keep, c, n_ge)

    thr, n_ge = lax.fori_loop(0, 32, bisect,
                              (jnp.full((1, t), INT_MIN, I32), jnp.zeros((1, t), F32)))
    thr = jnp.maximum(thr, INT_MIN + 1)
    has_ties = jnp.max(jnp.where(n_ge > topk, 1.0, 0.0)) > 0.0
    thr_b = jnp.broadcast_to(thr, (tg, t))

    for h in range(DSA_HEADS):
        ql = jnp.dot(wukT_ref[h], qT_ref[0, h], preferred_element_type=F32)
        qlat_sc[:, h * t:(h + 1) * t] = (ql * (DSA_HEAD_DIM ** -0.5)).astype(BF16)
    m_sc[...] = jnp.full(m_sc.shape, NEG, F32)
    l_sc[...] = jnp.zeros(l_sc.shape, F32)
    acc_sc[...] = jnp.zeros(acc_sc.shape, F32)

    def logits_into(dst_ref, g):
        off = pl.multiple_of(jnp.minimum(g, last) * tg, tg)
        dst_ref[...] = jnp.dot(c_ref[pl.ds(off, tg), :], qlat_sc[...],
                               preferred_element_type=F32)

    def attend(lg_ref, g, seen_eq, with_ties, need):
        valid = g <= last
        off = pl.multiple_of(jnp.minimum(g, last) * tg, tg)
        key = keys_sc[pl.ds(off, tg), :]
        if with_ties:
            eq = (key == thr_b) & valid
            eqf = eq.astype(F32)
            strict_lower = (lax.broadcasted_iota(I32, (tg, tg), 0)
                            > lax.broadcasted_iota(I32, (tg, tg), 1)).astype(BF16)
            rank = jnp.dot(strict_lower, eqf.astype(BF16), preferred_element_type=F32) + seen_eq
            sel = (key > thr_b) | (eq & (rank < need))
            seen_eq = seen_eq + jnp.sum(eqf, axis=0, keepdims=True)
        else:
            sel = key >= thr_b
        bias = jnp.where(sel & valid, 0.0, NEG)
        cT_tile = cT_ref[:, pl.ds(off, tg)]
        for h in range(DSA_HEADS):
            s = lg_ref[:, h * t:(h + 1) * t] + bias
            m_prev = m_sc[h]
            m_new = jnp.maximum(m_prev, jnp.max(s, axis=0, keepdims=True))
            alpha = jnp.exp(m_prev - m_new)
            p = jnp.exp(s - m_new)
            l_sc[h] = alpha * l_sc[h] + jnp.sum(p, axis=0, keepdims=True)
            acc_sc[h] = alpha * acc_sc[h] + jnp.dot(cT_tile, p.astype(BF16),
                                                    preferred_element_type=F32)
            m_sc[h] = m_new
        return seen_eq

    def sweep(with_ties, need=None):
        logits_into(lga_sc, 0)

        def pair(j, seen_eq):
            g0 = 2 * j
            logits_into(lgb_sc, g0 + 1)
            seen_eq = attend(lga_sc, g0, seen_eq, with_ties, need)
            logits_into(lga_sc, g0 + 2)
            return attend(lgb_sc, g0 + 1, seen_eq, with_ties, need)

        lax.fori_loop(0, last // 2 + 1, pair, jnp.zeros((1, t), F32))

    @pl.when(has_ties)
    def _():
        sweep(True, topk - count_ge(thr + 1))

    @pl.when(jnp.logical_not(has_ties))
    def _():
        sweep(False)

    for h in range(DSA_HEADS):
        o_lat = (acc_sc[h] / l_sc[h]).astype(BF16)
        o_ref[0, h] = jnp.dot(wuvT_ref[h], o_lat, preferred_element_type=F32).astype(o_ref.dtype)


def _dsa_attention(qiT, w, qT, wukT, wuvT, ki, c, cT, topk):
    NB, H, Dh, t = qT.shape
    S = NB * t
    R = DSA_KV_RANK
    tg = min(512, S)
    const2 = lambda i: (0, 0)
    const3 = lambda i: (0, 0, 0)
    return pl.pallas_call(
        functools.partial(_dsa_kernel, t=t, tg=tg, topk=topk),
        grid=(NB,),
        in_specs=[pl.BlockSpec((1, IDX_HEAD_DIM, IDX_HEADS * t), lambda i: (i, 0, 0)),
                  pl.BlockSpec((1, IDX_HEADS, t), lambda i: (i, 0, 0)),
                  pl.BlockSpec((1, H, Dh, t), lambda i: (i, 0, 0, 0)),
                  pl.BlockSpec((H, R, Dh), const3),
                  pl.BlockSpec((H, Dh, R), const3),
                  pl.BlockSpec((S, IDX_HEAD_DIM), const2, pipeline_mode=pl.Buffered(1)),
                  pl.BlockSpec((S, R), const2, pipeline_mode=pl.Buffered(1)),
                  pl.BlockSpec((R, S), const2, pipeline_mode=pl.Buffered(1))],
        out_specs=pl.BlockSpec((1, H, Dh, t), lambda i: (i, 0, 0, 0)),
        out_shape=jax.ShapeDtypeStruct((NB, H, Dh, t), BF16),
        scratch_shapes=[pltpu.VMEM((S, t), I32), pltpu.VMEM((R, H * t), BF16),
                        pltpu.VMEM((H, 1, t), F32), pltpu.VMEM((H, 1, t), F32),
                        pltpu.VMEM((H, R, t), F32),
                        pltpu.VMEM((tg, H * t), F32), pltpu.VMEM((tg, H * t), F32),
                        pltpu.VMEM((tg, IDX_HEADS * t), F32), pltpu.VMEM((tg, IDX_HEADS * t), F32)],
        compiler_params=_params("arbitrary"),
        name="dsa_attention",
    )(qiT, w, qT, wukT, wuvT, ki, c, cT)


def _shift_mix(p, prev_tail, mu, first_block):
    rows = lax.broadcasted_iota(I32, p.shape, 0)
    tail = jnp.where(first_block, 0.0, prev_tail)
    prev = jnp.where(rows == 0, tail, pltpu.roll(p, 1, axis=0))
    return p + (prev - p) * mu


def _split3(f):
    hi = f.astype(BF16)
    r1 = f - hi.astype(F32)
    mid = r1.astype(BF16)
    lo = (r1 - mid.astype(F32)).astype(BF16)
    return hi, mid, lo


def _dot3(x, w01):
    return sum(jnp.dot(part, w01, preferred_element_type=F32) for part in _split3(x))


def _dot3_left(w01, x):
    return sum(jnp.dot(w01, part, preferred_element_type=F32) for part in _split3(x))


def _rwkv_pre_kernel(*refs, with_vres):
    (p_ref, pp_ref, mu_ref, w0_ref, a0_ref, wwa_ref, gup_ref, kk_ref, ka_ref, rk_ref,
     tri_ref, blk_ref, bd_ref) = refs[:13]
    if with_vres:
        vup_ref, vv0_ref, vf_ref = refs[13:16]
    (at_o, rt_o, bt_o, kt_o, bh_o, kh_o, vb_o, pc_o, bonus_o, g_o, v_o) = refs[-11:]
    first = pl.program_id(0) == 0
    W = RWKV_W
    ps = _shift_mix(p_ref[...], pp_ref[7:8, :], mu_ref[...], first)
    r, k, v = ps[:, 0:W], ps[:, W:2 * W], ps[:, 2 * W:3 * W]
    wa = ps[:, 3 * W:3 * W + LANE]
    gl = ps[:, 3 * W + LANE:3 * W + 2 * LANE]
    lane = lax.broadcasted_iota(I32, wa.shape, 1)
    wa = jnp.where(lane < RWKV_W_LORA, jnp.tanh(wa), wa)
    up = jnp.dot(wa.astype(BF16), wwa_ref[...], preferred_element_type=F32)
    log_w = -_softplus(-(w0_ref[...] + up[:, 0:W])) - 0.5
    a = _sigmoid(a0_ref[...] + up[:, W:2 * W])
    g = jnp.dot(_sigmoid(gl).astype(BF16), gup_ref[...], preferred_element_type=F32)
    if with_vres:
        vl = ps[:, RWKV_IN:RWKV_IN + LANE]
        logit = vv0_ref[...] + jnp.dot(vl.astype(BF16), vup_ref[...], preferred_element_type=F32)
        v = v + (vf_ref[...] - v) * _sigmoid(logit)
    k2 = k * (1.0 + (a - 1.0) * ka_ref[...])
    kkr = k * kk_ref[...]
    bd = bd_ref[...]
    kk = kkr * lax.rsqrt(_dot3(kkr * kkr, bd) + 1e-12)
    lw = -jnp.exp(log_w)
    cum = _dot3_left(tri_ref[...], lw)
    cend = _dot3_left(blk_ref[...], lw)
    p_inv = jnp.exp(-cum)
    p_end = jnp.exp(cend - cum)
    beta = kk * a
    at_o[...] = (-kk * jnp.exp(cum - lw)).astype(BF16)
    rt_o[...] = (r * jnp.exp(cum)).astype(BF16)
    bt_o[...] = (beta * p_inv).astype(BF16)
    kt_o[...] = (k2 * p_inv).astype(BF16)
    bh_o[...] = (beta * p_end).astype(BF16)
    kh_o[...] = (k2 * p_end).astype(BF16)
    vb_o[...] = v.astype(BF16)
    pc_o[...] = jnp.exp(cend)
    bonus_o[...] = _dot3(r * k2 * rk_ref[...], bd) * v
    g_o[...] = g
    v_o[...] = v


def _block_ones(n, block, lower_tri=False):
    i = jnp.arange(n)
    m = (i[:, None] // block) == (i[None, :] // block)
    if lower_tri:
        m = m & (i[:, None] >= i[None, :])
    return m.astype(BF16)


def _rwkv_pre(p, mu, w0, a0, wwa, gup, k_k, k_a, r_k, vres=None):
    S, PW = p.shape
    W = RWKV_W
    tm = min(512, S)
    row = lambda i: (i, 0)
    const = lambda i: (0, 0)
    tail = lambda i: (jnp.maximum(i * (tm // 8) - 1, 0), 0)
    vec = lambda a: a.reshape(1, -1)
    args = [p, p, vec(mu), vec(w0), vec(a0), wwa, gup, vec(k_k), vec(k_a), vec(r_k),
            _block_ones(tm, CHUNK, lower_tri=True), _block_ones(tm, CHUNK),
            _block_ones(W, RWKV_HEAD_DIM)]
    specs = [pl.BlockSpec((tm, PW), row), pl.BlockSpec((8, PW), tail),
             pl.BlockSpec((1, PW), const), pl.BlockSpec((1, W), const), pl.BlockSpec((1, W), const),
             pl.BlockSpec(wwa.shape, const), pl.BlockSpec(gup.shape, const),
             pl.BlockSpec((1, W), const), pl.BlockSpec((1, W), const), pl.BlockSpec((1, W), const),
             pl.BlockSpec((tm, tm), const), pl.BlockSpec((tm, tm), const),
             pl.BlockSpec((W, W), const)]
    if vres is not None:
        vup, vv0, v_first = vres
        args += [vup, vec(vv0), v_first]
        specs += [pl.BlockSpec(vup.shape, const), pl.BlockSpec((1, W), const),
                  pl.BlockSpec((tm, W), row)]
    return pl.pallas_call(
        functools.partial(_rwkv_pre_kernel, with_vres=vres is not None),
        grid=(S // tm,),
        in_specs=specs,
        out_specs=[pl.BlockSpec((tm, W), row)] * 11,
        out_shape=[jax.ShapeDtypeStruct((S, W), BF16)] * 7 + [jax.ShapeDtypeStruct((S, W), F32)] * 4,
        compiler_params=_params("arbitrary"),
        name="rwkv_pre",
    )(*args)


def _rwkv_scan_kernel(at_ref, rt_ref, bt_ref, kt_ref, bh_ref, kh_ref, v_ref, pc_ref, y_ref, h_sc,
                      *, chunks):
    C = CHUNK
    N = RWKV_HEAD_DIM

    @pl.when(pl.program_id(0) == 0)
    def _():
        h_sc[...] = jnp.zeros(h_sc.shape, F32)

    ti = lax.broadcasted_iota(I32, (C, C), 0)
    tj = lax.broadcasted_iota(I32, (C, C), 1)
    lower_incl = ti >= tj
    lower_strict = ti > tj
    eye = (ti == tj).astype(F32)
    eye_n = (lax.broadcasted_iota(I32, (N, N), 0) == lax.broadcasted_iota(I32, (N, N), 1)).astype(F32)

    def mm(x, y):
        return jnp.dot(x.astype(BF16), y.astype(BF16), preferred_element_type=F32)

    def mm_nt(x, y):
        return lax.dot_general(x.astype(BF16), y.astype(BF16), (((1,), (1,)), ((), ())),
                               preferred_element_type=F32)

    def mm_tn(x, y):
        return lax.dot_general(x.astype(BF16), y.astype(BF16), (((0,), (0,)), ((), ())),
                               preferred_element_type=F32)

    units = [(c, h) for c in range(chunks) for h in range(RWKV_HEADS)]
    tile = lambda ref, u: ref[u[0] * C:(u[0] + 1) * C, u[1] * N:(u[1] + 1) * N]
    each = lambda fn: {u: fn(u) for u in units}

    At, Rt, Bt, Kt = (each(lambda u, r=ref: tile(r, u)) for ref in (at_ref, rt_ref, bt_ref, kt_ref))
    Bh, Kh, V = (each(lambda u, r=ref: tile(r, u)) for ref in (bh_ref, kh_ref, v_ref))
    AR = each(lambda u: jnp.concatenate([At[u], Rt[u]], axis=0))
    Mb = each(lambda u: mm_nt(AR[u], Bt[u]))
    Mk = each(lambda u: mm_nt(AR[u], Kt[u]))
    Lab = each(lambda u: jnp.where(lower_strict, Mb[u][0:C], 0.0))
    Mrb = each(lambda u: jnp.where(lower_incl, Mb[u][C:2 * C], 0.0))
    Lak = each(lambda u: jnp.where(lower_strict, Mk[u][0:C], 0.0))
    Mrk = each(lambda u: jnp.where(lower_incl, Mk[u][C:2 * C], 0.0))
    T = each(lambda u: eye + Lab[u])
    Lp = Lab
    span = 2
    while span < C:
        Lp = each(lambda u, Lp=Lp: mm(Lp[u], Lp[u]))
        T = each(lambda u, T=T, Lp=Lp: T[u] + mm(Lp[u], T[u]))
        span *= 2
    W1 = each(lambda u: mm(Lak[u], V[u]))
    A2 = each(lambda u: mm(T[u], At[u]))
    U0 = each(lambda u: mm(T[u], W1[u]))
    R2 = each(lambda u: Rt[u].astype(F32) + mm(Mrb[u], A2[u]))
    Y0 = each(lambda u: mm(Mrb[u], U0[u]) + mm(Mrk[u], V[u]))
    G = each(lambda u: eye_n * tile(pc_ref, u)[0:1, :] + mm_tn(Bh[u], A2[u]))
    H0 = each(lambda u: mm_tn(Bh[u], U0[u]) + mm_tn(Kh[u], V[u]))

    H = {h: h_sc[h] for h in range(RWKV_HEADS)}
    for c in range(chunks):
        ys = []
        for h in range(RWKV_HEADS):
            u = (c, h)
            ys.append(mm(R2[u], H[h]) + Y0[u])
            H[h] = mm(G[u], H[h]) + H0[u]
        y_ref[c * C:(c + 1) * C, :] = jnp.concatenate(ys, axis=1)
    for h in range(RWKV_HEADS):
        h_sc[h] = H[h]


def _rwkv_scan(at, rt, bt, kt, bh, kh, vb, pc):
    S, W = at.shape
    chunks = 2 if S % (2 * CHUNK) == 0 else 1
    tb = chunks * CHUNK
    seq = pl.BlockSpec((tb, W), lambda i: (i, 0))
    return pl.pallas_call(
        functools.partial(_rwkv_scan_kernel, chunks=chunks),
        grid=(S // tb,),
        in_specs=[seq] * 8,
        out_specs=seq,
        out_shape=jax.ShapeDtypeStruct((S, W), F32),
        scratch_shapes=[pltpu.VMEM((RWKV_HEADS, RWKV_HEAD_DIM, RWKV_HEAD_DIM), F32)],
        compiler_params=_params("arbitrary"),
        name="rwkv_scan",
    )(at, rt, bt, kt, bh, kh, vb, pc)


def _merge_kernel(x_ref, of_ref, od_ref, y_ref, bonus_ref, g_ref, lng_ref, lnb_ref, bd_ref,
                  gp_ref, bg_ref, pf_ref, pd_ref, pr_ref, wo_ref, gn_ref, xo_ref, h_ref):
    D = D_MODEL
    bd = bd_ref[...]
    inv_n = 1.0 / RWKV_HEAD_DIM
    y = y_ref[...]
    yc = y - _dot3(y, bd) * inv_n
    var = _dot3(yc * yc, bd) * inv_n
    yn = yc * lax.rsqrt(var + RWKV_LN_EPS) * lng_ref[...] + lnb_ref[...]
    o_rwkv = ((yn + bonus_ref[...]) * g_ref[...]).astype(BF16)
    gates = _sigmoid(gp_ref[...] + bg_ref[...])
    merged = (gates[:, 0:D] * jnp.dot(of_ref[...], pf_ref[...], preferred_element_type=F32)
              + gates[:, D:2 * D] * jnp.dot(od_ref[...], pd_ref[...], preferred_element_type=F32)
              + gates[:, 2 * D:3 * D] * jnp.dot(o_rwkv, pr_ref[...], preferred_element_type=F32))
    x = x_ref[...] + jnp.dot(merged.astype(BF16), wo_ref[...], preferred_element_type=F32)
    xo_ref[...] = x
    h_ref[...] = _rms(x, gn_ref[...]).astype(h_ref.dtype)


def _merge(x, o_fox, o_dsa, y_rwkv, bonus, g_rwkv, ln_g, ln_b, gate_p, b_gate, p_fox, p_dsa, p_rwkv,
           w_out, g_ffn):
    S, D = x.shape
    W = RWKV_W
    tm = min(512, S)
    row = lambda i: (i, 0)
    const = lambda i: (0, 0)
    return pl.pallas_call(
        _merge_kernel,
        grid=(S // tm,),
        in_specs=[pl.BlockSpec((tm, D), row), pl.BlockSpec((tm, FOX_W), row),
                  pl.BlockSpec((tm, DSA_W), row), pl.BlockSpec((tm, W), row),
                  pl.BlockSpec((tm, W), row), pl.BlockSpec((tm, W), row),
                  pl.BlockSpec((1, W), const), pl.BlockSpec((1, W), const),
                  pl.BlockSpec((W, W), const),
                  pl.BlockSpec((tm, 3 * D), row), pl.BlockSpec((1, 3 * D), const),
                  pl.BlockSpec((FOX_W, D), const), pl.BlockSpec((DSA_W, D), const),
                  pl.BlockSpec((W, D), const), pl.BlockSpec((D, D), const),
                  pl.BlockSpec((1, D), const)],
        out_specs=[pl.BlockSpec((tm, D), row), pl.BlockSpec((tm, D), row)],
        out_shape=[jax.ShapeDtypeStruct((S, D), F32), jax.ShapeDtypeStruct((S, D), BF16)],
        compiler_params=_params("arbitrary"),
        name="merge",
    )(x, o_fox, o_dsa, y_rwkv, bonus, g_rwkv, ln_g.reshape(1, W), ln_b.reshape(1, W),
      _block_ones(W, RWKV_HEAD_DIM), gate_p, b_gate.reshape(1, -1), p_fox, p_dsa, p_rwkv, w_out,
      g_ffn.reshape(1, D))


def _swiglu_tile(h, wg, wu):
    gate = jnp.dot(h, wg, preferred_element_type=F32)
    up = jnp.dot(h, wu, preferred_element_type=F32)
    return gate * _sigmoid(gate) * up


def _ffn_kernel(x_ref, h_ref, wg_ref, wu_ref, wd_ref, gf_ref, o_ref, acc_sc, *, final_norm):
    f = pl.program_id(1)

    @pl.when(f == 0)
    def _():
        acc_sc[...] = x_ref[...]

    act = _swiglu_tile(h_ref[...], wg_ref[...], wu_ref[...])
    acc_sc[...] += jnp.dot(act.astype(BF16), wd_ref[...], preferred_element_type=F32)

    @pl.when(f == pl.num_programs(1) - 1)
    def _():
        y = acc_sc[...]
        o_ref[...] = _rms(y, gf_ref[...]) if final_norm else y


def _ffn(x, h, wg, wu, wd, g_final, final_norm):
    S, D = x.shape
    Fd = wg.shape[1]
    tm = min(512, S)
    tf = _pick_tile(Fd, 1408)
    return pl.pallas_call(
        functools.partial(_ffn_kernel, final_norm=final_norm),
        grid=(S // tm, Fd // tf),
        in_specs=[pl.BlockSpec((tm, D), lambda i, f: (i, 0)),
                  pl.BlockSpec((tm, D), lambda i, f: (i, 0)),
                  pl.BlockSpec((D, tf), lambda i, f: (0, f)),
                  pl.BlockSpec((D, tf), lambda i, f: (0, f)),
                  pl.BlockSpec((tf, D), lambda i, f: (f, 0)),
                  pl.BlockSpec((1, D), lambda i, f: (0, 0))],
        out_specs=pl.BlockSpec((tm, D), lambda i, f: (i, 0)),
        out_shape=jax.ShapeDtypeStruct((S, D), F32),
        scratch_shapes=[pltpu.VMEM((tm, D), F32)],
        compiler_params=_params("arbitrary", "arbitrary"),
        name="ffn",
    )(x, h, wg, wu, wd, g_final.reshape(1, D))


MOE_CHUNK = 128


def _moe_kernel(x_ref, h_ref, rw_ref, rb_ref, wg_ref, wu_ref, wd_ref, gf_ref, o_ref,
                acc_sc, gate_sc, rank_sc, cnt_sc, hs_sc, ys_sc, *, final_norm):
    e = pl.program_id(1)
    f = pl.program_id(2)
    last_f = pl.num_programs(2) - 1
    T, D = h_ref.shape
    C = MOE_CHUNK
    lane = lax.broadcasted_iota(I32, (T, LANE), 1)

    @pl.when((e == 0) & (f == 0))
    def _():
        acc_sc[...] = x_ref[...]
        logits = jnp.dot(h_ref[...], rw_ref[...], preferred_element_type=F32) + rb_ref[...]
        logits = jnp.where(lane < N_EXPERTS, logits, -jnp.inf)
        v1 = jnp.max(logits, axis=-1, keepdims=True)
        i1 = jnp.min(jnp.where(logits == v1, lane, LANE), axis=-1, keepdims=True)
        rest = jnp.where(lane == i1, -jnp.inf, logits)
        v2 = jnp.max(rest, axis=-1, keepdims=True)
        i2 = jnp.min(jnp.where(rest == v2, lane, LANE), axis=-1, keepdims=True)
        e2 = jnp.exp(v2 - v1)
        p1 = 1.0 / (1.0 + e2)
        gate_sc[...] = jnp.where(lane == i1, p1, 0.0) + jnp.where(lane == i2, e2 * p1, 0.0)
        routed = ((lane == i1) | (lane == i2)).astype(F32)
        B = 256
        earlier = (lax.broadcasted_iota(I32, (B, B), 0) > lax.broadcasted_iota(I32, (B, B), 1)).astype(BF16)
        seen = jnp.zeros((1, LANE), F32)
        for b in range(T // B):
            blk = routed[b * B:(b + 1) * B]
            before = jnp.dot(earlier, blk.astype(BF16), preferred_element_type=F32) + seen
            rank_sc[b * B:(b + 1) * B, :] = jnp.where(blk > 0.0, before, -1.0)
            seen = seen + jnp.sum(blk, axis=0, keepdims=True)
        cnt_sc[...] = seen

    n_e = jnp.sum(jnp.where(lax.broadcasted_iota(I32, (1, LANE), 1) == e, cnt_sc[...], 0.0))
    n_chunks = sum(jnp.where(n_e > j * C, 1, 0) for j in range(T // C))

    @pl.when(f == 0)
    def _():
        pick8 = (lax.broadcasted_iota(I32, (8, LANE), 1) == e).astype(BF16)
        rk = rank_sc[...]
        rk_hi = rk.astype(BF16)
        rk_lo = (rk - rk_hi.astype(F32)).astype(BF16)
        nt = (((1,), (1,)), ((), ()))
        rrow = (lax.dot_general(pick8, rk_hi, nt, preferred_element_type=F32)
                + lax.dot_general(pick8, rk_lo, nt, preferred_element_type=F32))[0:1, :]

        def gather(c, carry):
            off = pl.multiple_of(c * C, C)
            slot = (off + lax.broadcasted_iota(I32, (C, T), 0)).astype(F32)
            onehot = (rrow == slot).astype(BF16)
            hs_sc[pl.ds(off, C), :] = jnp.dot(onehot, h_ref[...],
                                              preferred_element_type=F32).astype(BF16)
            ys_sc[pl.ds(off, C), :] = jnp.zeros((C, D), F32)
            return carry

        lax.fori_loop(0, n_chunks, gather, 0)

    def expert(c, carry):
        off = pl.multiple_of(c * C, C)
        act = _swiglu_tile(hs_sc[pl.ds(off, C), :], wg_ref[0], wu_ref[0])
        ys_sc[pl.ds(off, C), :] += jnp.dot(act.astype(BF16), wd_ref[0], preferred_element_type=F32)
        return carry

    lax.fori_loop(0, n_chunks, expert, 0)

    @pl.when(f == last_f)
    def _():
        mine = lane == e
        gate = jnp.sum(jnp.where(mine, gate_sc[...], 0.0), axis=-1, keepdims=True)
        rank = jnp.sum(jnp.where(mine, rank_sc[...], 0.0), axis=-1, keepdims=True)
        rank_b = jnp.broadcast_to(rank, (T, C))
        col = lax.broadcasted_iota(I32, (T, C), 1)

        def scatter(c, carry):
            off = pl.multiple_of(c * C, C)
            onehot_t = (rank_b == (col + off).astype(F32)).astype(BF16)
            y = jnp.dot(onehot_t, ys_sc[pl.ds(off, C), :].astype(BF16), preferred_element_type=F32)
            acc_sc[...] += gate * y
            return carry

        lax.fori_loop(0, n_chunks, scatter, 0)

    @pl.when((e == pl.num_programs(1) - 1) & (f == last_f))
    def _():
        y = acc_sc[...]
        o_ref[...] = _rms(y, gf_ref[...]) if final_norm else y


def _moe(x, h, rw, rb, wg, wu, wd, g_final, final_norm):
    S, D = x.shape
    E, _, Fe = wg.shape
    tm = min(1024, S)
    tf = _pick_tile(Fe, 896)
    return pl.pallas_call(
        functools.partial(_moe_kernel, final_norm=final_norm),
        grid=(S // tm, E, Fe // tf),
        in_specs=[pl.BlockSpec((tm, D), lambda i, e, f: (i, 0)),
                  pl.BlockSpec((tm, D), lambda i, e, f: (i, 0)),
                  pl.BlockSpec((D, LANE), lambda i, e, f: (0, 0)),
                  pl.BlockSpec((1, LANE), lambda i, e, f: (0, 0)),
                  pl.BlockSpec((1, D, tf), lambda i, e, f: (e, 0, f)),
                  pl.BlockSpec((1, D, tf), lambda i, e, f: (e, 0, f)),
                  pl.BlockSpec((1, tf, D), lambda i, e, f: (e, f, 0)),
                  pl.BlockSpec((1, D), lambda i, e, f: (0, 0))],
        out_specs=pl.BlockSpec((tm, D), lambda i, e, f: (i, 0)),
        out_shape=jax.ShapeDtypeStruct((S, D), F32),
        scratch_shapes=[pltpu.VMEM((tm, D), F32), pltpu.VMEM((tm, LANE), F32),
                        pltpu.VMEM((tm, LANE), F32), pltpu.VMEM((1, LANE), F32),
                        pltpu.VMEM((tm, D), BF16), pltpu.VMEM((tm, D), F32)],
        compiler_params=_params("arbitrary", "arbitrary", "arbitrary"),
        name="moe",
    )(x, h, rw, rb, wg, wu, wd, g_final.reshape(1, D))


def _cast_kernel(x_ref, o_ref):
    o_ref[...] = x_ref[...].astype(o_ref.dtype)


def _to_bf16(w):
    E, A, B = w.shape
    ta = 256
    return pl.pallas_call(
        _cast_kernel,
        grid=(E, A // ta),
        in_specs=[pl.BlockSpec((1, ta, B), lambda e, a: (e, a, 0))],
        out_specs=pl.BlockSpec((1, ta, B), lambda e, a: (e, a, 0)),
        out_shape=jax.ShapeDtypeStruct(w.shape, BF16),
        compiler_params=_params("arbitrary", "arbitrary"),
        name="to_bf16",
    )(w)


def _pad_cols(a, n):
    return jnp.pad(a, ((0, 0), (0, n - a.shape[1])))


def kernel(x, w_in, b_gate, g_mix, fox_f_bias, dsa_kv_norm, dsa_w_uk, dsa_w_uv, rwkv_mu, rwkv_w0, rwkv_w_up, rwkv_a0, rwkv_a_up, rwkv_g_up, rwkv_k_k, rwkv_k_a, rwkv_r_k, rwkv_ln_g, rwkv_ln_b, vres_down, vres_mu, vres_up, vres_v0, p_fox, p_dsa, p_rwkv, w_out, g_ffn, ffn_w_gate, ffn_w_up, ffn_w_down, router_w, router_b, moe_w_gate, moe_w_up, moe_w_down, g_final):
    B, S, D = x.shape
    assert B == 1 and D == D_MODEL and S % LANE == 0
    depth = w_in.shape[0]
    topk = min(IDX_TOPK, S // 4)
    bf = lambda a: a.astype(BF16)
    xs = x[0]
    v_first = None
    w_in_bf = _to_bf16(w_in)
    for l in range(depth):
        wl = w_in_bf[l]
        w_fox, w_dsa, w_rwkv, w_gate = (wl[:, :FOX_IN], wl[:, FOX_IN:FOX_IN + DSA_IN],
                                        wl[:, FOX_IN + DSA_IN:FOX_IN + DSA_IN + RWKV_IN],
                                        wl[:, FOX_IN + DSA_IN + RWKV_IN:])
        o1 = DSA_W + DSA_KV_RANK + IDX_W
        w_att = jnp.concatenate([_pad_cols(w_fox, 3 * FOX_W + LANE), w_dsa[:, :o1],
                                 _pad_cols(w_dsa[:, o1:], LANE)], axis=1)
        if l > 0:
            w_rwkv = jnp.concatenate([w_rwkv, bf(_pad_cols(vres_down[l - 1], LANE))], axis=1)
        pa = _rms_proj(xs, g_mix[l], w_att)
        pr = _rms_proj(xs, g_mix[l], w_rwkv)
        gate_p = _rms_proj(xs, g_mix[l], w_gate)

        c0 = 3 * FOX_W
        fl = pa[:, c0:c0 + FOX_HEADS].T.reshape(FOX_HEADS, S // LANE, LANE)
        Fh, Fm, Fl = (a.reshape(FOX_HEADS, S) for a in _fox_cumsum(fl, fox_f_bias[l]))
        qT = bf(pa[:, 0:FOX_W].T.reshape(FOX_HEADS, FOX_HEAD_DIM, S) * (FOX_HEAD_DIM ** -0.5))
        r = jnp.arange(FOX_KA - FOX_HEAD_DIM)
        rq, ck = r[None, :, None], r[None, None, :]
        pick = lambda idx, a, b, c, d: jnp.where(idx == a[0], a[1], jnp.where(
            idx == b[0], b[1], jnp.where(idx == c[0], c[1], jnp.where(idx < 6, d, 0)))).astype(BF16)
        fq = pick(rq, (0, Fh[:, None, :]), (1, Fm[:, None, :]), (2, Fl[:, None, :]), 1)
        fk = pick(ck, (3, -Fh[:, :, None]), (4, -Fm[:, :, None]), (5, -Fl[:, :, None]), 1)
        qaT = jnp.concatenate([qT, fq], axis=1)
        k4 = bf(pa[:, FOX_W:2 * FOX_W].reshape(S, FOX_HEADS, FOX_HEAD_DIM).transpose(1, 0, 2))
        ka = jnp.concatenate([k4, fk], axis=2)
        fvT = bf(pa[:, 2 * FOX_W:3 * FOX_W].T.reshape(FOX_HEADS, FOX_HEAD_DIM, S))
        o_fox = _fox_attention(qaT, ka, fvT).reshape(FOX_W, S).T

        t = min(DSA_T, S)
        nb = S // t
        c1 = c0 + LANE
        dqT = bf(pa[:, c1:c1 + DSA_W].T.reshape(DSA_HEADS, DSA_HEAD_DIM, nb, t).transpose(2, 0, 1, 3))
        c2 = c1 + DSA_W
        ckv = _rmsnorm(pa[:, c2:c2 + DSA_KV_RANK], dsa_kv_norm[l], BF16)
        c3 = c2 + DSA_KV_RANK
        qiT = bf(pa[:, c3:c3 + IDX_W].T.reshape(IDX_HEADS, IDX_HEAD_DIM, nb, t).transpose(2, 1, 0, 3)
                 .reshape(nb, IDX_HEAD_DIM, IDX_HEADS * t))
        c4 = c3 + IDX_W
        ki = bf(pa[:, c4:c4 + IDX_HEAD_DIM])
        c5 = c4 + IDX_HEAD_DIM
        wi = pa[:, c5:c5 + IDX_HEADS].T.reshape(IDX_HEADS, nb, t).transpose(1, 0, 2) * (IDX_W ** -0.5)
        o_dsa = _dsa_attention(qiT, wi, dqT, bf(dsa_w_uk[l].transpose(0, 2, 1)),
                               bf(dsa_w_uv[l].transpose(0, 2, 1)), ki, ckv, ckv.T, topk)
        o_dsa = o_dsa.transpose(1, 2, 0, 3).reshape(DSA_W, S).T

        zw = jnp.zeros((RWKV_W_LORA, RWKV_W), F32)
        wwa = bf(jnp.concatenate([jnp.concatenate([rwkv_w_up[l], zw], axis=1),
                                  jnp.concatenate([zw, rwkv_a_up[l]], axis=1)], axis=0))
        vres = None
        mu = rwkv_mu[l]
        if l > 0:
            vup = jnp.pad(vres_up[l - 1], ((0, LANE - RWKV_V_LORA), (0, 0)))
            vres = (bf(vup), vres_v0[l - 1], v_first)
            mu = jnp.concatenate([mu, jnp.pad(vres_mu[l - 1], (0, LANE - RWKV_V_LORA))])
        *scan_ops, bonus, g_rwkv, v = _rwkv_pre(
            pr, mu, rwkv_w0[l], rwkv_a0[l], wwa, bf(rwkv_g_up[l]),
            rwkv_k_k[l], rwkv_k_a[l], rwkv_r_k[l], vres)
        if l == 0:
            v_first = v
        y_rwkv = _rwkv_scan(*scan_ops)

        xs, h2 = _merge(xs, o_fox, o_dsa, y_rwkv, bonus, g_rwkv, rwkv_ln_g[l], rwkv_ln_b[l],
                        gate_p, b_gate[l], bf(p_fox[l]), bf(p_dsa[l]), bf(p_rwkv[l]), bf(w_out[l]),
                        g_ffn[l])

        last = l == depth - 1
        if l % 2 == 0:
            xs = _ffn(xs, h2, bf(ffn_w_gate[l // 2]), bf(ffn_w_up[l // 2]), bf(ffn_w_down[l // 2]),
                      g_final, last)
        else:
            rw = bf(_pad_cols(router_w[l // 2], LANE))
            rb = _pad_cols(router_b[l // 2].reshape(1, -1), LANE)
            xs = _moe(xs, h2, rw, rb, _to_bf16(moe_w_gate[l // 2]), _to_bf16(moe_w_up[l // 2]),
                      _to_bf16(moe_w_down[l // 2]), g_final, last)
    return xs[None]
```

```python
import functools

import jax
import jax.numpy as jnp
from jax import lax
from jax.experimental import pallas as pl
from jax.experimental.pallas import tpu as pltpu

F32 = jnp.float32
BF16 = jnp.bfloat16
I32 = jnp.int32

D_MODEL = 1024
CHUNK = 64
RMS_EPS = 1e-6
FOX_HEADS, FOX_HEAD_DIM = 4, 64
DSA_HEADS, DSA_HEAD_DIM, DSA_KV_RANK = 4, 64, 128
IDX_HEADS, IDX_HEAD_DIM, IDX_TOPK = 8, 32, 256
RWKV_HEADS, RWKV_HEAD_DIM = 8, 64
RWKV_W_LORA, RWKV_A_LORA, RWKV_V_LORA, RWKV_G_LORA = 64, 64, 32, 128
RWKV_LN_EPS = 64e-5
FOX_W = FOX_HEADS * FOX_HEAD_DIM
DSA_W = DSA_HEADS * DSA_HEAD_DIM
RWKV_W = RWKV_HEADS * RWKV_HEAD_DIM
IDX_W = IDX_HEADS * IDX_HEAD_DIM
N_EXPERTS = 8
FOX_IN = 3 * FOX_W + FOX_HEADS
DSA_IN = DSA_W + DSA_KV_RANK + IDX_W + IDX_HEAD_DIM + IDX_HEADS
RWKV_IN = 3 * RWKV_W + RWKV_W_LORA + RWKV_A_LORA + RWKV_G_LORA

LANE = 128
VMEM_LIMIT = 52 * 1024 * 1024
NEG = -1e30
INT_MIN = -(2 ** 31)
LOG2E = 1.4426950408889634
F32_MIN_NORMAL = 2.0 ** -126
HI = lax.Precision.HIGHEST


def _params(*sem):
    return pltpu.CompilerParams(dimension_semantics=sem, vmem_limit_bytes=VMEM_LIMIT)


def _pick_tile(n, cap):
    best = LANE
    for t in range(LANE, min(n, cap) + 1, LANE):
        if n % t == 0:
            best = t
    return best


def _softplus(x):
    return jnp.maximum(x, 0.0) + jnp.log1p(jnp.exp(-jnp.abs(x)))


def _sigmoid(x):
    return 1.0 / (1.0 + jnp.exp(-x))


def _rms(x, g):
    return x * lax.rsqrt(jnp.mean(x * x, axis=-1, keepdims=True) + RMS_EPS) * g


def _rms_proj_kernel(x_ref, g_ref, w_ref, o_ref):
    h = _rms(x_ref[...], g_ref[...])
    o_ref[...] = jnp.dot(h.astype(BF16), w_ref[...], preferred_element_type=F32)


def _rms_proj(x, g, w):
    S, D = x.shape
    N = w.shape[1]
    tm = min(512, S)
    tn = _pick_tile(N, 2304)
    return pl.pallas_call(
        _rms_proj_kernel,
        grid=(N // tn, S // tm),
        in_specs=[pl.BlockSpec((tm, D), lambda j, i: (i, 0)),
                  pl.BlockSpec((1, D), lambda j, i: (0, 0)),
                  pl.BlockSpec((D, tn), lambda j, i: (0, j))],
        out_specs=pl.BlockSpec((tm, tn), lambda j, i: (i, j)),
        out_shape=jax.ShapeDtypeStruct((S, N), F32),
        compiler_params=_params("arbitrary", "arbitrary"),
        name="rms_proj",
    )(x, g.reshape(1, D), w)


def _rmsnorm_kernel(x_ref, g_ref, o_ref):
    o_ref[...] = _rms(x_ref[...], g_ref[...]).astype(o_ref.dtype)


def _rmsnorm(x, g, dtype):
    S, D = x.shape
    tm = min(2048, S)
    return pl.pallas_call(
        _rmsnorm_kernel,
        grid=(S // tm,),
        in_specs=[pl.BlockSpec((tm, D), lambda i: (i, 0)),
                  pl.BlockSpec((1, D), lambda i: (0, 0))],
        out_specs=pl.BlockSpec((tm, D), lambda i: (i, 0)),
        out_shape=jax.ShapeDtypeStruct((S, D), dtype),
        compiler_params=_params("arbitrary"),
        name="rmsnorm",
    )(x, g.reshape(1, D))


def _fox_cumsum_kernel(fl_ref, b_ref, hi_ref, mid_ref, lo_ref):
    H, R, _ = fl_ref.shape
    upper = (lax.broadcasted_iota(I32, (LANE, LANE), 0)
             <= lax.broadcasted_iota(I32, (LANE, LANE), 1)).astype(F32)
    strict_lower = (lax.broadcasted_iota(I32, (R, R), 0)
                    > lax.broadcasted_iota(I32, (R, R), 1)).astype(F32)
    for h in range(H):
        log_f = -_softplus(-(fl_ref[h] + b_ref[h]))
        within = jnp.dot(log_f, upper, preferred_element_type=F32, precision=HI)
        row_tot = jnp.broadcast_to(within[:, LANE - 1:LANE], (R, LANE))
        before = jnp.dot(strict_lower, row_tot, preferred_element_type=F32, precision=HI)
        hi, mid, lo = _split3((within + before) * LOG2E)
        hi_ref[h], mid_ref[h], lo_ref[h] = hi, mid, lo


def _fox_cumsum(fl, bias):
    H, R, _ = fl.shape
    return pl.pallas_call(
        _fox_cumsum_kernel,
        out_shape=[jax.ShapeDtypeStruct((H, R, LANE), BF16)] * 3,
        compiler_params=pltpu.CompilerParams(vmem_limit_bytes=VMEM_LIMIT),
        name="fox_cumsum",
    )(fl, jnp.broadcast_to(bias.reshape(H, 1, 1), (H, 1, LANE)))


FOX_KA = 128


def _fox_kernel(qa_ref, ka_ref, vT_ref, o_ref, lga_sc, lgb_sc, m_sc, l_sc, acc_sc, *, t, tc, tg):
    i = pl.program_id(1)
    hb = qa_ref.shape[0]
    last = (i * t) // tg
    chains = [(h, q0) for h in range(hb) for q0 in range(0, t, tc)]

    def logits_into(dst_ref, g):
        off = pl.multiple_of(jnp.minimum(g, last) * tg, tg)
        for n, (h, q0) in enumerate(chains):
            dst_ref[n] = jnp.dot(ka_ref[h, pl.ds(off, tg), :], qa_ref[h, :, q0:q0 + tc],
                                 preferred_element_type=F32)

    def update(lg_ref, g, masked):
        off = pl.multiple_of(g * tg, tg)
        for n, (h, q0) in enumerate(chains):
            s = lg_ref[n]
            if masked:
                kpos = off + lax.broadcasted_iota(I32, (tg, tc), 0)
                qpos = i * t + q0 + lax.broadcasted_iota(I32, (tg, tc), 1)
                s = jnp.where(kpos <= qpos, s, NEG)
            m_prev = m_sc[n]
            m_new = jnp.maximum(m_prev, jnp.max(s, axis=0, keepdims=True))
            alpha = jnp.exp2(m_prev - m_new)
            p = jnp.exp2(s - m_new)
            l_sc[n] = alpha * l_sc[n] + jnp.sum(p, axis=0, keepdims=True)
            acc_sc[n] = alpha * acc_sc[n] + jnp.dot(vT_ref[h, :, pl.ds(off, tg)], p.astype(BF16),
                                                    preferred_element_type=F32)
            m_sc[n] = m_new

    m_sc[...] = jnp.full(m_sc.shape, NEG, F32)
    l_sc[...] = jnp.zeros(l_sc.shape, F32)
    acc_sc[...] = jnp.zeros(acc_sc.shape, F32)
    logits_into(lga_sc, 0)

    def pair(j, c):
        logits_into(lgb_sc, 2 * j + 1)
        update(lga_sc, 2 * j, False)
        logits_into(lga_sc, 2 * j + 2)
        update(lgb_sc, 2 * j + 1, False)
        return c

    lax.fori_loop(0, last // 2, pair, 0)
    tail = 2 * (last // 2)
    logits_into(lgb_sc, tail + 1)
    update(lga_sc, tail, True)

    @pl.when(tail + 1 <= last)
    def _():
        update(lgb_sc, tail + 1, True)

    for n, (h, q0) in enumerate(chains):
        o_ref[h, :, q0:q0 + tc] = (acc_sc[n] / l_sc[n]).astype(o_ref.dtype)


def _fox_attention(qaT, ka, vT):
    H, KA, S = qaT.shape
    Dh = vT.shape[1]
    t = min(512, S)
    tc = min(256, t)
    tg = min(1024, S)
    hb = 2
    nc = hb * (t // tc)
    return pl.pallas_call(
        functools.partial(_fox_kernel, t=t, tc=tc, tg=tg),
        grid=(H // hb, S // t),
        in_specs=[pl.BlockSpec((hb, KA, t), lambda h, i: (h, 0, i)),
                  pl.BlockSpec((hb, S, KA), lambda h, i: (h, 0, 0)),
                  pl.BlockSpec((hb, Dh, S), lambda h, i: (h, 0, 0))],
        out_specs=pl.BlockSpec((hb, Dh, t), lambda h, i: (h, 0, i)),
        out_shape=jax.ShapeDtypeStruct((H, Dh, S), BF16),
        scratch_shapes=[pltpu.VMEM((nc, tg, tc), F32), pltpu.VMEM((nc, tg, tc), F32),
                        pltpu.VMEM((nc, 1, tc), F32), pltpu.VMEM((nc, 1, tc), F32),
                        pltpu.VMEM((nc, Dh, tc), F32)],
        compiler_params=_params("arbitrary", "arbitrary"),
        name="fox_attention",
    )(qaT, ka, vT)


DSA_T = 128


def _dsa_kernel(qiT_ref, w_ref, qT_ref, wukT_ref, wuvT_ref, ki_ref, c_ref, cT_ref, o_ref,
                keys_sc, qlat_sc, m_sc, l_sc, acc_sc, lga_sc, lgb_sc, rela_sc, relb_sc,
                *, t, tg, topk):
    i = pl.program_id(0)
    last = (i * t) // tg
    ng = last + 1

    qiT = qiT_ref[0]
    w = w_ref[0]

    def rel_into(dst_ref, g):
        off = pl.multiple_of(jnp.minimum(g, last) * tg, tg)
        dst_ref[...] = jnp.dot(ki_ref[pl.ds(off, tg), :], qiT, preferred_element_type=F32)

    def score_keys(rel_ref, g, masked):
        off = pl.multiple_of(g * tg, tg)
        sc = jnp.maximum(rel_ref[:, 0:t], 0.0) * w[0:1, :]
        for h in range(1, IDX_HEADS):
            sc = sc + jnp.maximum(rel_ref[:, h * t:(h + 1) * t], 0.0) * w[h:h + 1, :]
        bits = pltpu.bitcast(sc, I32)
        key = bits ^ ((bits >> 31) & 0x7FFFFFFF)
        kpos = off + lax.broadcasted_iota(I32, (tg, t), 0)
        key = jnp.where(jnp.abs(sc) < F32_MIN_NORMAL, -1 - kpos, key)
        if masked:
            qchunk = (i * t + lax.broadcasted_iota(I32, (tg, t), 1)) // CHUNK
            key = jnp.where(kpos // CHUNK <= qchunk, key, INT_MIN)
        keys_sc[pl.ds(off, tg), :] = key

    def fill_pair(j, c):
        rel_into(relb_sc, 2 * j + 1)
        score_keys(rela_sc, 2 * j, False)
        rel_into(rela_sc, 2 * j + 2)
        score_keys(relb_sc, 2 * j + 1, False)
        return c

    rel_into(rela_sc, 0)
    lax.fori_loop(0, last // 2, fill_pair, 0)
    tail = 2 * (last // 2)
    rel_into(relb_sc, tail + 1)
    score_keys(rela_sc, tail, True)

    @pl.when(tail + 1 <= last)
    def _():
        score_keys(relb_sc, tail + 1, True)

    def count_ge(cand):
        cb = jnp.broadcast_to(cand, (t, t))

        def body(g, acc):
            off = pl.multiple_of(g * tg, tg)
            for k in range(tg // t):
                acc = acc + (keys_sc[pl.ds(off + k * t, t), :] >= cb).astype(F32)
            return acc

        acc = lax.fori_loop(0, ng, body, jnp.zeros((t, t), F32))
        return jnp.sum(acc, axis=0, keepdims=True)

    def bisect(s, st):
        thr, n_ge = st
        cand = thr + jnp.left_shift(jnp.int32(1), 31 - s)
        c = count_ge(cand)
        keep = c >= topk
        return jnp.where(keep, cand, thr), jnp.where(keep, c, n_ge)

    thr, n_ge = lax.fori_loop(0, 32, bisect,
                              (jnp.full((1, t), INT_MIN, I32), jnp.zeros((1, t), F32)))
    thr = jnp.maximum(thr, INT_MIN + 1)
    has_ties = jnp.max(jnp.where(n_ge > topk, 1.0, 0.0)) > 0.0
    thr_b = jnp.broadcast_to(thr, (tg, t))

    for h in range(DSA_HEADS):
        ql = jnp.dot(wukT_ref[h], qT_ref[0, h], preferred_element_type=F32)
        qlat_sc[:, h * t:(h + 1) * t] = (ql * (DSA_HEAD_DIM ** -0.5 * LOG2E)).astype(BF16)
    m_sc[...] = jnp.full(m_sc.shape, NEG, F32)
    l_sc[...] = jnp.zeros(l_sc.shape, F32)
    acc_sc[...] = jnp.zeros(acc_sc.shape, F32)

    def logits_into(dst_ref, g):
        off = pl.multiple_of(jnp.minimum(g, last) * tg, tg)
        dst_ref[...] = jnp.dot(c_ref[pl.ds(off, tg), :], qlat_sc[...],
                               preferred_element_type=F32)

    def attend(lg_ref, g, seen_eq, with_ties, need):
        valid = g <= last
        off = pl.multiple_of(jnp.minimum(g, last) * tg, tg)
        key = keys_sc[pl.ds(off, tg), :]
        if with_ties:
            eq = (key == thr_b) & valid
            eqf = eq.astype(F32)
            strict_lower = (lax.broadcasted_iota(I32, (tg, tg), 0)
                            > lax.broadcasted_iota(I32, (tg, tg), 1)).astype(BF16)
            rank = jnp.dot(strict_lower, eqf.astype(BF16), preferred_element_type=F32) + seen_eq
            sel = (key > thr_b) | (eq & (rank < need))
            seen_eq = seen_eq + jnp.sum(eqf, axis=0, keepdims=True)
        else:
            sel = key >= thr_b
        bias = jnp.where(sel & valid, 0.0, NEG)
        cT_tile = cT_ref[:, pl.ds(off, tg)]
        for h in range(DSA_HEADS):
            s = lg_ref[:, h * t:(h + 1) * t] + bias
            m_prev = m_sc[h]
            m_new = jnp.maximum(m_prev, jnp.max(s, axis=0, keepdims=True))
            alpha = jnp.exp2(m_prev - m_new)
            p = jnp.exp2(s - m_new)
            l_sc[h] = alpha * l_sc[h] + jnp.sum(p, axis=0, keepdims=True)
            acc_sc[h] = alpha * acc_sc[h] + jnp.dot(cT_tile, p.astype(BF16),
                                                    preferred_element_type=F32)
            m_sc[h] = m_new
        return seen_eq

    def sweep(with_ties, need=None):
        logits_into(lga_sc, 0)

        def pair(j, seen_eq):
            g0 = 2 * j
            logits_into(lgb_sc, g0 + 1)
            seen_eq = attend(lga_sc, g0, seen_eq, with_ties, need)
            logits_into(lga_sc, g0 + 2)
            return attend(lgb_sc, g0 + 1, seen_eq, with_ties, need)

        lax.fori_loop(0, last // 2 + 1, pair, jnp.zeros((1, t), F32))

    @pl.when(has_ties)
    def _():
        sweep(True, topk - count_ge(thr + 1))

    @pl.when(jnp.logical_not(has_ties))
    def _():
        sweep(False)

    for h in range(DSA_HEADS):
        o_lat = (acc_sc[h] / l_sc[h]).astype(BF16)
        o_ref[0, h] = jnp.dot(wuvT_ref[h], o_lat, preferred_element_type=F32).astype(o_ref.dtype)


def _dsa_attention(qiT, w, qT, wukT, wuvT, ki, c, cT, topk):
    NB, H, Dh, t = qT.shape
    S = NB * t
    assert S < 2 ** 23
    R = DSA_KV_RANK
    tg = min(512, S)
    const2 = lambda i: (0, 0)
    const3 = lambda i: (0, 0, 0)
    return pl.pallas_call(
        functools.partial(_dsa_kernel, t=t, tg=tg, topk=topk),
        grid=(NB,),
        in_specs=[pl.BlockSpec((1, IDX_HEAD_DIM, IDX_HEADS * t), lambda i: (i, 0, 0)),
                  pl.BlockSpec((1, IDX_HEADS, t), lambda i: (i, 0, 0)),
                  pl.BlockSpec((1, H, Dh, t), lambda i: (i, 0, 0, 0)),
                  pl.BlockSpec((H, R, Dh), const3),
                  pl.BlockSpec((H, Dh, R), const3),
                  pl.BlockSpec((S, IDX_HEAD_DIM), const2, pipeline_mode=pl.Buffered(1)),
                  pl.BlockSpec((S, R), const2, pipeline_mode=pl.Buffered(1)),
                  pl.BlockSpec((R, S), const2, pipeline_mode=pl.Buffered(1))],
        out_specs=pl.BlockSpec((1, H, Dh, t), lambda i: (i, 0, 0, 0)),
        out_shape=jax.ShapeDtypeStruct((NB, H, Dh, t), BF16),
        scratch_shapes=[pltpu.VMEM((S, t), I32), pltpu.VMEM((R, H * t), BF16),
                        pltpu.VMEM((H, 1, t), F32), pltpu.VMEM((H, 1, t), F32),
                        pltpu.VMEM((H, R, t), F32),
                        pltpu.VMEM((tg, H * t), F32), pltpu.VMEM((tg, H * t), F32),
                        pltpu.VMEM((tg, IDX_HEADS * t), F32), pltpu.VMEM((tg, IDX_HEADS * t), F32)],
        compiler_params=_params("arbitrary"),
        name="dsa_attention",
    )(qiT, w, qT, wukT, wuvT, ki, c, cT)


def _shift_mix(p, prev_tail, mu, first_block):
    rows = lax.broadcasted_iota(I32, p.shape, 0)
    tail = jnp.where(first_block, 0.0, prev_tail)
    prev = jnp.where(rows == 0, tail, pltpu.roll(p, 1, axis=0))
    return p + (prev - p) * mu


def _split3(f):
    hi = f.astype(BF16)
    r1 = f - hi.astype(F32)
    mid = r1.astype(BF16)
    lo = (r1 - mid.astype(F32)).astype(BF16)
    return hi, mid, lo


def _dot3(x, w01):
    return sum(jnp.dot(part, w01, preferred_element_type=F32) for part in _split3(x))


def _dot3_left(w01, x):
    return sum(jnp.dot(w01, part, preferred_element_type=F32) for part in _split3(x))


def _rwkv_pre_kernel(*refs, with_vres):
    (p_ref, pp_ref, mu_ref, w0_ref, a0_ref, wwa_ref, gup_ref, kk_ref, ka_ref, rk_ref,
     tri_ref, blk_ref, bd_ref) = refs[:13]
    if with_vres:
        vup_ref, vv0_ref, vf_ref = refs[13:16]
    (at_o, rt_o, bt_o, kt_o, bh_o, kh_o, vb_o, pc_o, bonus_o, g_o, v_o) = refs[-11:]
    first = pl.program_id(0) == 0
    W = RWKV_W
    ps = _shift_mix(p_ref[...], pp_ref[7:8, :], mu_ref[...], first)
    r, k, v = ps[:, 0:W], ps[:, W:2 * W], ps[:, 2 * W:3 * W]
    wa = ps[:, 3 * W:3 * W + LANE]
    gl = ps[:, 3 * W + LANE:3 * W + 2 * LANE]
    lane = lax.broadcasted_iota(I32, wa.shape, 1)
    wa = jnp.where(lane < RWKV_W_LORA, jnp.tanh(wa), wa)
    up = jnp.dot(wa.astype(BF16), wwa_ref[...], preferred_element_type=F32)
    log_w = -_softplus(-(w0_ref[...] + up[:, 0:W])) - 0.5
    a = _sigmoid(a0_ref[...] + up[:, W:2 * W])
    g = jnp.dot(_sigmoid(gl).astype(BF16), gup_ref[...], preferred_element_type=F32)
    if with_vres:
        vl = ps[:, RWKV_IN:RWKV_IN + LANE]
        logit = vv0_ref[...] + jnp.dot(vl.astype(BF16), vup_ref[...], preferred_element_type=F32)
        v = v + (vf_ref[...] - v) * _sigmoid(logit)
    k2 = k * (1.0 + (a - 1.0) * ka_ref[...])
    kkr = k * kk_ref[...]
    bd = bd_ref[...]
    kk = kkr * lax.rsqrt(_dot3(kkr * kkr, bd) + 1e-12)
    lw = -jnp.exp(log_w)
    cum = _dot3_left(tri_ref[...], lw)
    cend = _dot3_left(blk_ref[...], lw)
    p_inv = jnp.exp(-cum)
    p_end = jnp.exp(cend - cum)
    beta = kk * a
    at_o[...] = (-kk * jnp.exp(cum - lw)).astype(BF16)
    rt_o[...] = (r * jnp.exp(cum)).astype(BF16)
    bt_o[...] = (beta * p_inv).astype(BF16)
    kt_o[...] = (k2 * p_inv).astype(BF16)
    bh_o[...] = (beta * p_end).astype(BF16)
    kh_o[...] = (k2 * p_end).astype(BF16)
    vb_o[...] = v.astype(BF16)
    pc_o[...] = jnp.exp(cend)
    bonus_o[...] = _dot3(r * k2 * rk_ref[...], bd) * v
    g_o[...] = g
    v_o[...] = v


def _block_ones(n, block, lower_tri=False):
    i = jnp.arange(n)
    m = (i[:, None] // block) == (i[None, :] // block)
    if lower_tri:
        m = m & (i[:, None] >= i[None, :])
    return m.astype(BF16)


def _rwkv_pre(p, mu, w0, a0, wwa, gup, k_k, k_a, r_k, vres=None):
    S, PW = p.shape
    W = RWKV_W
    tm = min(512, S)
    row = lambda i: (i, 0)
    const = lambda i: (0, 0)
    tail = lambda i: (jnp.maximum(i * (tm // 8) - 1, 0), 0)
    vec = lambda a: a.reshape(1, -1)
    args = [p, p, vec(mu), vec(w0), vec(a0), wwa, gup, vec(k_k), vec(k_a), vec(r_k),
            _block_ones(tm, CHUNK, lower_tri=True), _block_ones(tm, CHUNK),
            _block_ones(W, RWKV_HEAD_DIM)]
    specs = [pl.BlockSpec((tm, PW), row), pl.BlockSpec((8, PW), tail),
             pl.BlockSpec((1, PW), const), pl.BlockSpec((1, W), const), pl.BlockSpec((1, W), const),
             pl.BlockSpec(wwa.shape, const), pl.BlockSpec(gup.shape, const),
             pl.BlockSpec((1, W), const), pl.BlockSpec((1, W), const), pl.BlockSpec((1, W), const),
             pl.BlockSpec((tm, tm), const), pl.BlockSpec((tm, tm), const),
             pl.BlockSpec((W, W), const)]
    if vres is not None:
        vup, vv0, v_first = vres
        args += [vup, vec(vv0), v_first]
        specs += [pl.BlockSpec(vup.shape, const), pl.BlockSpec((1, W), const),
                  pl.BlockSpec((tm, W), row)]
    return pl.pallas_call(
        functools.partial(_rwkv_pre_kernel, with_vres=vres is not None),
        grid=(S // tm,),
        in_specs=specs,
        out_specs=[pl.BlockSpec((tm, W), row)] * 11,
        out_shape=[jax.ShapeDtypeStruct((S, W), BF16)] * 7 + [jax.ShapeDtypeStruct((S, W), F32)] * 4,
        compiler_params=_params("arbitrary"),
        name="rwkv_pre",
    )(*args)


def _rwkv_scan_kernel(at_ref, rt_ref, bt_ref, kt_ref, bh_ref, kh_ref, v_ref, pc_ref, y_ref, h_sc,
                      *, chunks):
    C = CHUNK
    N = RWKV_HEAD_DIM

    @pl.when(pl.program_id(0) == 0)
    def _():
        h_sc[...] = jnp.zeros(h_sc.shape, F32)

    ti = lax.broadcasted_iota(I32, (C, C), 0)
    tj = lax.broadcasted_iota(I32, (C, C), 1)
    lower_incl = ti >= tj
    lower_strict = ti > tj
    eye = (ti == tj).astype(F32)
    eye_n = (lax.broadcasted_iota(I32, (N, N), 0) == lax.broadcasted_iota(I32, (N, N), 1)).astype(F32)

    def mm(x, y):
        return jnp.dot(x.astype(BF16), y.astype(BF16), preferred_element_type=F32)

    def mm_nt(x, y):
        return lax.dot_general(x.astype(BF16), y.astype(BF16), (((1,), (1,)), ((), ())),
                               preferred_element_type=F32)

    def mm_tn(x, y):
        return lax.dot_general(x.astype(BF16), y.astype(BF16), (((0,), (0,)), ((), ())),
                               preferred_element_type=F32)

    units = [(c, h) for c in range(chunks) for h in range(RWKV_HEADS)]
    tile = lambda ref, u: ref[u[0] * C:(u[0] + 1) * C, u[1] * N:(u[1] + 1) * N]
    each = lambda fn: {u: fn(u) for u in units}

    At, Rt, Bt, Kt = (each(lambda u, r=ref: tile(r, u)) for ref in (at_ref, rt_ref, bt_ref, kt_ref))
    Bh, Kh, V = (each(lambda u, r=ref: tile(r, u)) for ref in (bh_ref, kh_ref, v_ref))
    AR = each(lambda u: jnp.concatenate([At[u], Rt[u]], axis=0))
    Mb = each(lambda u: mm_nt(AR[u], Bt[u]))
    Mk = each(lambda u: mm_nt(AR[u], Kt[u]))
    Lab = each(lambda u: jnp.where(lower_strict, Mb[u][0:C], 0.0))
    Mrb = each(lambda u: jnp.where(lower_incl, Mb[u][C:2 * C], 0.0))
    Lak = each(lambda u: jnp.where(lower_strict, Mk[u][0:C], 0.0))
    Mrk = each(lambda u: jnp.where(lower_incl, Mk[u][C:2 * C], 0.0))
    T = each(lambda u: eye + Lab[u])
    Lp = Lab
    span = 2
    while span < C:
        Lp = each(lambda u, Lp=Lp: mm(Lp[u], Lp[u]))
        T = each(lambda u, T=T, Lp=Lp: T[u] + mm(Lp[u], T[u]))
        span *= 2
    W1 = each(lambda u: mm(Lak[u], V[u]))
    A2 = each(lambda u: mm(T[u], At[u]))
    U0 = each(lambda u: mm(T[u], W1[u]))
    R2 = each(lambda u: Rt[u].astype(F32) + mm(Mrb[u], A2[u]))
    Y0 = each(lambda u: mm(Mrb[u], U0[u]) + mm(Mrk[u], V[u]))
    G = each(lambda u: eye_n * tile(pc_ref, u)[0:1, :] + mm_tn(Bh[u], A2[u]))
    H0 = each(lambda u: mm_tn(Bh[u], U0[u]) + mm_tn(Kh[u], V[u]))

    H = {h: h_sc[h] for h in range(RWKV_HEADS)}
    for c in range(chunks):
        ys = []
        for h in range(RWKV_HEADS):
            u = (c, h)
            ys.append(mm(R2[u], H[h]) + Y0[u])
            H[h] = mm(G[u], H[h]) + H0[u]
        y_ref[c * C:(c + 1) * C, :] = jnp.concatenate(ys, axis=1)
    for h in range(RWKV_HEADS):
        h_sc[h] = H[h]


def _rwkv_scan(at, rt, bt, kt, bh, kh, vb, pc):
    S, W = at.shape
    chunks = 2 if S % (2 * CHUNK) == 0 else 1
    tb = chunks * CHUNK
    seq = pl.BlockSpec((tb, W), lambda i: (i, 0))
    return pl.pallas_call(
        functools.partial(_rwkv_scan_kernel, chunks=chunks),
        grid=(S // tb,),
        in_specs=[seq] * 8,
        out_specs=seq,
        out_shape=jax.ShapeDtypeStruct((S, W), F32),
        scratch_shapes=[pltpu.VMEM((RWKV_HEADS, RWKV_HEAD_DIM, RWKV_HEAD_DIM), F32)],
        compiler_params=_params("arbitrary"),
        name="rwkv_scan",
    )(at, rt, bt, kt, bh, kh, vb, pc)


def _merge_kernel(x_ref, of_ref, od_ref, y_ref, bonus_ref, g_ref, lng_ref, lnb_ref, bd_ref,
                  gp_ref, bg_ref, pf_ref, pd_ref, pr_ref, wo_ref, gn_ref, xo_ref, h_ref):
    D = D_MODEL
    bd = bd_ref[...]
    inv_n = 1.0 / RWKV_HEAD_DIM
    y = y_ref[...]
    yc = y - _dot3(y, bd) * inv_n
    var = _dot3(yc * yc, bd) * inv_n
    yn = yc * lax.rsqrt(var + RWKV_LN_EPS) * lng_ref[...] + lnb_ref[...]
    o_rwkv = ((yn + bonus_ref[...]) * g_ref[...]).astype(BF16)
    gates = _sigmoid(gp_ref[...] + bg_ref[...])
    merged = (gates[:, 0:D] * jnp.dot(of_ref[...], pf_ref[...], preferred_element_type=F32)
              + gates[:, D:2 * D] * jnp.dot(od_ref[...], pd_ref[...], preferred_element_type=F32)
              + gates[:, 2 * D:3 * D] * jnp.dot(o_rwkv, pr_ref[...], preferred_element_type=F32))
    x = x_ref[...] + jnp.dot(merged.astype(BF16), wo_ref[...], preferred_element_type=F32)
    xo_ref[...] = x
    h_ref[...] = _rms(x, gn_ref[...]).astype(h_ref.dtype)


def _merge(x, o_fox, o_dsa, y_rwkv, bonus, g_rwkv, ln_g, ln_b, gate_p, b_gate, p_fox, p_dsa, p_rwkv,
           w_out, g_ffn):
    S, D = x.shape
    W = RWKV_W
    tm = min(512, S)
    row = lambda i: (i, 0)
    const = lambda i: (0, 0)
    return pl.pallas_call(
        _merge_kernel,
        grid=(S // tm,),
        in_specs=[pl.BlockSpec((tm, D), row), pl.BlockSpec((tm, FOX_W), row),
                  pl.BlockSpec((tm, DSA_W), row), pl.BlockSpec((tm, W), row),
                  pl.BlockSpec((tm, W), row), pl.BlockSpec((tm, W), row),
                  pl.BlockSpec((1, W), const), pl.BlockSpec((1, W), const),
                  pl.BlockSpec((W, W), const),
                  pl.BlockSpec((tm, 3 * D), row), pl.BlockSpec((1, 3 * D), const),
                  pl.BlockSpec((FOX_W, D), const), pl.BlockSpec((DSA_W, D), const),
                  pl.BlockSpec((W, D), const), pl.BlockSpec((D, D), const),
                  pl.BlockSpec((1, D), const)],
        out_specs=[pl.BlockSpec((tm, D), row), pl.BlockSpec((tm, D), row)],
        out_shape=[jax.ShapeDtypeStruct((S, D), F32), jax.ShapeDtypeStruct((S, D), BF16)],
        compiler_params=_params("arbitrary"),
        name="merge",
    )(x, o_fox, o_dsa, y_rwkv, bonus, g_rwkv, ln_g.reshape(1, W), ln_b.reshape(1, W),
      _block_ones(W, RWKV_HEAD_DIM), gate_p, b_gate.reshape(1, -1), p_fox, p_dsa, p_rwkv, w_out,
      g_ffn.reshape(1, D))


def _swiglu_tile(h, wg, wu):
    gate = jnp.dot(h, wg, preferred_element_type=F32)
    up = jnp.dot(h, wu, preferred_element_type=F32)
    return gate * _sigmoid(gate) * up


def _ffn_kernel(x_ref, h_ref, wg_ref, wu_ref, wd_ref, gf_ref, o_ref, acc_sc, *, final_norm):
    f = pl.program_id(1)

    @pl.when(f == 0)
    def _():
        acc_sc[...] = x_ref[...]

    act = _swiglu_tile(h_ref[...], wg_ref[...], wu_ref[...])
    acc_sc[...] += jnp.dot(act.astype(BF16), wd_ref[...], preferred_element_type=F32)

    @pl.when(f == pl.num_programs(1) - 1)
    def _():
        y = acc_sc[...]
        o_ref[...] = _rms(y, gf_ref[...]) if final_norm else y


def _ffn(x, h, wg, wu, wd, g_final, final_norm):
    S, D = x.shape
    Fd = wg.shape[1]
    tm = min(512, S)
    tf = _pick_tile(Fd, 1408)
    return pl.pallas_call(
        functools.partial(_ffn_kernel, final_norm=final_norm),
        grid=(S // tm, Fd // tf),
        in_specs=[pl.BlockSpec((tm, D), lambda i, f: (i, 0)),
                  pl.BlockSpec((tm, D), lambda i, f: (i, 0)),
                  pl.BlockSpec((D, tf), lambda i, f: (0, f)),
                  pl.BlockSpec((D, tf), lambda i, f: (0, f)),
                  pl.BlockSpec((tf, D), lambda i, f: (f, 0)),
                  pl.BlockSpec((1, D), lambda i, f: (0, 0))],
        out_specs=pl.BlockSpec((tm, D), lambda i, f: (i, 0)),
        out_shape=jax.ShapeDtypeStruct((S, D), F32),
        scratch_shapes=[pltpu.VMEM((tm, D), F32)],
        compiler_params=_params("arbitrary", "arbitrary"),
        name="ffn",
    )(x, h, wg, wu, wd, g_final.reshape(1, D))


MOE_CHUNK = 128


def _moe_kernel(x_ref, h_ref, rw_ref, rb_ref, wg_ref, wu_ref, wd_ref, gf_ref, o_ref,
                acc_sc, gate_sc, rank_sc, cnt_sc, hs_sc, ys_sc, *, final_norm):
    e = pl.program_id(1)
    f = pl.program_id(2)
    last_f = pl.num_programs(2) - 1
    T, D = h_ref.shape
    C = MOE_CHUNK
    lane = lax.broadcasted_iota(I32, (T, LANE), 1)

    @pl.when((e == 0) & (f == 0))
    def _():
        acc_sc[...] = x_ref[...]
        logits = jnp.dot(h_ref[...], rw_ref[...], preferred_element_type=F32) + rb_ref[...]
        logits = jnp.where(lane < N_EXPERTS, logits, -jnp.inf)
        v1 = jnp.max(logits, axis=-1, keepdims=True)
        i1 = jnp.min(jnp.where(logits == v1, lane, LANE), axis=-1, keepdims=True)
        rest = jnp.where(lane == i1, -jnp.inf, logits)
        v2 = jnp.max(rest, axis=-1, keepdims=True)
        i2 = jnp.min(jnp.where(rest == v2, lane, LANE), axis=-1, keepdims=True)
        e2 = jnp.exp(v2 - v1)
        p1 = 1.0 / (1.0 + e2)
        gate_sc[...] = jnp.where(lane == i1, p1, 0.0) + jnp.where(lane == i2, e2 * p1, 0.0)
        routed = ((lane == i1) | (lane == i2)).astype(F32)
        B = 256
        earlier = (lax.broadcasted_iota(I32, (B, B), 0) > lax.broadcasted_iota(I32, (B, B), 1)).astype(BF16)
        seen = jnp.zeros((1, LANE), F32)
        for b in range(T // B):
            blk = routed[b * B:(b + 1) * B]
            before = jnp.dot(earlier, blk.astype(BF16), preferred_element_type=F32) + seen
            rank_sc[b * B:(b + 1) * B, :] = jnp.where(blk > 0.0, before, -1.0)
            seen = seen + jnp.sum(blk, axis=0, keepdims=True)
        cnt_sc[...] = seen

    n_e = jnp.sum(jnp.where(lax.broadcasted_iota(I32, (1, LANE), 1) == e, cnt_sc[...], 0.0))
    n_chunks = sum(jnp.where(n_e > j * C, 1, 0) for j in range(T // C))

    @pl.when(f == 0)
    def _():
        pick8 = (lax.broadcasted_iota(I32, (8, LANE), 1) == e).astype(BF16)
        rk = rank_sc[...]
        rk_hi = rk.astype(BF16)
        rk_lo = (rk - rk_hi.astype(F32)).astype(BF16)
        nt = (((1,), (1,)), ((), ()))
        rrow = (lax.dot_general(pick8, rk_hi, nt, preferred_element_type=F32)
                + lax.dot_general(pick8, rk_lo, nt, preferred_element_type=F32))[0:1, :]

        def gather(c, carry):
            off = pl.multiple_of(c * C, C)
            slot = (off + lax.broadcasted_iota(I32, (C, T), 0)).astype(F32)
            onehot = (rrow == slot).astype(BF16)
            hs_sc[pl.ds(off, C), :] = jnp.dot(onehot, h_ref[...],
                                              preferred_element_type=F32).astype(BF16)
            ys_sc[pl.ds(off, C), :] = jnp.zeros((C, D), F32)
            return carry

        lax.fori_loop(0, n_chunks, gather, 0)

    def expert(c, carry):
        off = pl.multiple_of(c * C, C)
        act = _swiglu_tile(hs_sc[pl.ds(off, C), :], wg_ref[0], wu_ref[0])
        ys_sc[pl.ds(off, C), :] += jnp.dot(act.astype(BF16), wd_ref[0], preferred_element_type=F32)
        return carry

    lax.fori_loop(0, n_chunks, expert, 0)

    @pl.when(f == last_f)
    def _():
        mine = lane == e
        gate = jnp.sum(jnp.where(mine, gate_sc[...], 0.0), axis=-1, keepdims=True)
        rank = jnp.sum(jnp.where(mine, rank_sc[...], 0.0), axis=-1, keepdims=True)
        rank_b = jnp.broadcast_to(rank, (T, C))
        col = lax.broadcasted_iota(I32, (T, C), 1)

        def scatter(c, carry):
            off = pl.multiple_of(c * C, C)
            onehot_t = (rank_b == (col + off).astype(F32)).astype(BF16)
            y = jnp.dot(onehot_t, ys_sc[pl.ds(off, C), :].astype(BF16), preferred_element_type=F32)
            acc_sc[...] += gate * y
            return carry

        lax.fori_loop(0, n_chunks, scatter, 0)

    @pl.when((e == pl.num_programs(1) - 1) & (f == last_f))
    def _():
        y = acc_sc[...]
        o_ref[...] = _rms(y, gf_ref[...]) if final_norm else y


def _moe(x, h, rw, rb, wg, wu, wd, g_final, final_norm):
    S, D = x.shape
    E, _, Fe = wg.shape
    tm = min(1024, S)
    tf = _pick_tile(Fe, 896)
    return pl.pallas_call(
        functools.partial(_moe_kernel, final_norm=final_norm),
        grid=(S // tm, E, Fe // tf),
        in_specs=[pl.BlockSpec((tm, D), lambda i, e, f: (i, 0)),
                  pl.BlockSpec((tm, D), lambda i, e, f: (i, 0)),
                  pl.BlockSpec((D, LANE), lambda i, e, f: (0, 0)),
                  pl.BlockSpec((1, LANE), lambda i, e, f: (0, 0)),
                  pl.BlockSpec((1, D, tf), lambda i, e, f: (e, 0, f)),
                  pl.BlockSpec((1, D, tf), lambda i, e, f: (e, 0, f)),
                  pl.BlockSpec((1, tf, D), lambda i, e, f: (e, f, 0)),
                  pl.BlockSpec((1, D), lambda i, e, f: (0, 0))],
        out_specs=pl.BlockSpec((tm, D), lambda i, e, f: (i, 0)),
        out_shape=jax.ShapeDtypeStruct((S, D), F32),
        scratch_shapes=[pltpu.VMEM((tm, D), F32), pltpu.VMEM((tm, LANE), F32),
                        pltpu.VMEM((tm, LANE), F32), pltpu.VMEM((1, LANE), F32),
                        pltpu.VMEM((tm, D), BF16), pltpu.VMEM((tm, D), F32)],
        compiler_params=_params("arbitrary", "arbitrary", "arbitrary"),
        name="moe",
    )(x, h, rw, rb, wg, wu, wd, g_final.reshape(1, D))


def _cast_kernel(x_ref, o_ref):
    o_ref[...] = x_ref[...].astype(o_ref.dtype)


def _to_bf16(w):
    E, A, B = w.shape
    ta = 256
    return pl.pallas_call(
        _cast_kernel,
        grid=(E, A // ta),
        in_specs=[pl.BlockSpec((1, ta, B), lambda e, a: (e, a, 0))],
        out_specs=pl.BlockSpec((1, ta, B), lambda e, a: (e, a, 0)),
        out_shape=jax.ShapeDtypeStruct(w.shape, BF16),
        compiler_params=_params("arbitrary", "arbitrary"),
        name="to_bf16",
    )(w)


def _pad_cols(a, n):
    return jnp.pad(a, ((0, 0), (0, n - a.shape[1])))


def kernel(x, w_in, b_gate, g_mix, fox_f_bias, dsa_kv_norm, dsa_w_uk, dsa_w_uv, rwkv_mu, rwkv_w0, rwkv_w_up, rwkv_a0, rwkv_a_up, rwkv_g_up, rwkv_k_k, rwkv_k_a, rwkv_r_k, rwkv_ln_g, rwkv_ln_b, vres_down, vres_mu, vres_up, vres_v0, p_fox, p_dsa, p_rwkv, w_out, g_ffn, ffn_w_gate, ffn_w_up, ffn_w_down, router_w, router_b, moe_w_gate, moe_w_up, moe_w_down, g_final):
    B, S, D = x.shape
    assert B == 1 and D == D_MODEL and S % LANE == 0
    depth = w_in.shape[0]
    topk = min(IDX_TOPK, S // 4)
    bf = lambda a: a.astype(BF16)
    xs = x[0]
    v_first = None
    w_in_bf = _to_bf16(w_in)
    for l in range(depth):
        wl = w_in_bf[l]
        w_fox, w_dsa, w_rwkv, w_gate = (wl[:, :FOX_IN], wl[:, FOX_IN:FOX_IN + DSA_IN],
                                        wl[:, FOX_IN + DSA_IN:FOX_IN + DSA_IN + RWKV_IN],
                                        wl[:, FOX_IN + DSA_IN + RWKV_IN:])
        o1 = DSA_W + DSA_KV_RANK + IDX_W
        w_att = jnp.concatenate([_pad_cols(w_fox, 3 * FOX_W + LANE), w_dsa[:, :o1],
                                 _pad_cols(w_dsa[:, o1:], LANE)], axis=1)
        if l > 0:
            w_rwkv = jnp.concatenate([w_rwkv, bf(_pad_cols(vres_down[l - 1], LANE))], axis=1)
        pa = _rms_proj(xs, g_mix[l], w_att)
        pr = _rms_proj(xs, g_mix[l], w_rwkv)
        gate_p = _rms_proj(xs, g_mix[l], w_gate)

        c0 = 3 * FOX_W
        fl = pa[:, c0:c0 + FOX_HEADS].T.reshape(FOX_HEADS, S // LANE, LANE)
        Fh, Fm, Fl = (a.reshape(FOX_HEADS, S) for a in _fox_cumsum(fl, fox_f_bias[l]))
        qT = bf(pa[:, 0:FOX_W].T.reshape(FOX_HEADS, FOX_HEAD_DIM, S) * (FOX_HEAD_DIM ** -0.5 * LOG2E))
        r = jnp.arange(FOX_KA - FOX_HEAD_DIM)
        rq, ck = r[None, :, None], r[None, None, :]
        pick = lambda idx, a, b, c, d: jnp.where(idx == a[0], a[1], jnp.where(
            idx == b[0], b[1], jnp.where(idx == c[0], c[1], jnp.where(idx < 6, d, 0)))).astype(BF16)
        fq = pick(rq, (0, Fh[:, None, :]), (1, Fm[:, None, :]), (2, Fl[:, None, :]), 1)
        fk = pick(ck, (3, -Fh[:, :, None]), (4, -Fm[:, :, None]), (5, -Fl[:, :, None]), 1)
        qaT = jnp.concatenate([qT, fq], axis=1)
        k4 = bf(pa[:, FOX_W:2 * FOX_W].reshape(S, FOX_HEADS, FOX_HEAD_DIM).transpose(1, 0, 2))
        ka = jnp.concatenate([k4, fk], axis=2)
        fvT = bf(pa[:, 2 * FOX_W:3 * FOX_W].T.reshape(FOX_HEADS, FOX_HEAD_DIM, S))
        o_fox = _fox_attention(qaT, ka, fvT).reshape(FOX_W, S).T

        t = min(DSA_T, S)
        nb = S // t
        c1 = c0 + LANE
        dqT = bf(pa[:, c1:c1 + DSA_W].T.reshape(DSA_HEADS, DSA_HEAD_DIM, nb, t).transpose(2, 0, 1, 3))
        c2 = c1 + DSA_W
        ckv = _rmsnorm(pa[:, c2:c2 + DSA_KV_RANK], dsa_kv_norm[l], BF16)
        c3 = c2 + DSA_KV_RANK
        qiT = bf(pa[:, c3:c3 + IDX_W].T.reshape(IDX_HEADS, IDX_HEAD_DIM, nb, t).transpose(2, 1, 0, 3)
                 .reshape(nb, IDX_HEAD_DIM, IDX_HEADS * t))
        c4 = c3 + IDX_W
        ki = bf(pa[:, c4:c4 + IDX_HEAD_DIM])
        c5 = c4 + IDX_HEAD_DIM
        wi = pa[:, c5:c5 + IDX_HEADS].T.reshape(IDX_HEADS, nb, t).transpose(1, 0, 2) * (IDX_W ** -0.5)
        o_dsa = _dsa_attention(qiT, wi, dqT, bf(dsa_w_uk[l].transpose(0, 2, 1)),
                               bf(dsa_w_uv[l].transpose(0, 2, 1)), ki, ckv, ckv.T, topk)
        o_dsa = o_dsa.transpose(1, 2, 0, 3).reshape(DSA_W, S).T

        zw = jnp.zeros((RWKV_W_LORA, RWKV_W), F32)
        wwa = bf(jnp.concatenate([jnp.concatenate([rwkv_w_up[l], zw], axis=1),
                                  jnp.concatenate([zw, rwkv_a_up[l]], axis=1)], axis=0))
        vres = None
        mu = rwkv_mu[l]
        if l > 0:
            vup = jnp.pad(vres_up[l - 1], ((0, LANE - RWKV_V_LORA), (0, 0)))
            vres = (bf(vup), vres_v0[l - 1], v_first)
            mu = jnp.concatenate([mu, jnp.pad(vres_mu[l - 1], (0, LANE - RWKV_V_LORA))])
        *scan_ops, bonus, g_rwkv, v = _rwkv_pre(
            pr, mu, rwkv_w0[l], rwkv_a0[l], wwa, bf(rwkv_g_up[l]),
            rwkv_k_k[l], rwkv_k_a[l], rwkv_r_k[l], vres)
        if l == 0:
            v_first = v
        y_rwkv = _rwkv_scan(*scan_ops)

        xs, h2 = _merge(xs, o_fox, o_dsa, y_rwkv, bonus, g_rwkv, rwkv_ln_g[l], rwkv_ln_b[l],
                        gate_p, b_gate[l], bf(p_fox[l]), bf(p_dsa[l]), bf(p_rwkv[l]), bf(w_out[l]),
                        g_ffn[l])

        last = l == depth - 1
        if l % 2 == 0:
            xs = _ffn(xs, h2, bf(ffn_w_gate[l // 2]), bf(ffn_w_up[l // 2]), bf(ffn_w_down[l // 2]),
                      g_final, last)
        else:
            rw = bf(_pad_cols(router_w[l // 2], LANE))
            rb = _pad_cols(router_b[l // 2].reshape(1, -1), LANE)
            xs = _moe(xs, h2, rw, rb, _to_bf16(moe_w_gate[l // 2]), _to_bf16(moe_w_up[l // 2]),
                      _to_bf16(moe_w_down[l // 2]), g_final, last)
    return xs[None]
```

```python
import functools

import jax
import jax.numpy as jnp
from jax import lax
from jax.experimental import pallas as pl
from jax.experimental.pallas import tpu as pltpu

F32 = jnp.float32
BF16 = jnp.bfloat16
I32 = jnp.int32

D_MODEL = 1024
CHUNK = 64
RMS_EPS = 1e-6
FOX_HEADS, FOX_HEAD_DIM = 4, 64
DSA_HEADS, DSA_HEAD_DIM, DSA_KV_RANK = 4, 64, 128
IDX_HEADS, IDX_HEAD_DIM, IDX_TOPK = 8, 32, 256
RWKV_HEADS, RWKV_HEAD_DIM = 8, 64
RWKV_W_LORA, RWKV_A_LORA, RWKV_V_LORA, RWKV_G_LORA = 64, 64, 32, 128
RWKV_LN_EPS = 64e-5
FOX_W = FOX_HEADS * FOX_HEAD_DIM
DSA_W = DSA_HEADS * DSA_HEAD_DIM
RWKV_W = RWKV_HEADS * RWKV_HEAD_DIM
IDX_W = IDX_HEADS * IDX_HEAD_DIM
N_EXPERTS = 8
FOX_IN = 3 * FOX_W + FOX_HEADS
DSA_IN = DSA_W + DSA_KV_RANK + IDX_W + IDX_HEAD_DIM + IDX_HEADS
RWKV_IN = 3 * RWKV_W + RWKV_W_LORA + RWKV_A_LORA + RWKV_G_LORA

LANE = 128
VMEM_LIMIT = 52 * 1024 * 1024
NEG = -1e30
INT_MIN = -(2 ** 31)
LOG2E = 1.4426950408889634
F32_MIN_NORMAL = 2.0 ** -126
HI = lax.Precision.HIGHEST


def _params(*sem):
    return pltpu.CompilerParams(dimension_semantics=sem, vmem_limit_bytes=VMEM_LIMIT)


def _pick_tile(n, cap):
    best = LANE
    for t in range(LANE, min(n, cap) + 1, LANE):
        if n % t == 0:
            best = t
    return best


def _softplus(x):
    return jnp.maximum(x, 0.0) + jnp.log1p(jnp.exp(-jnp.abs(x)))


def _sigmoid(x):
    return 1.0 / (1.0 + jnp.exp(-x))


def _rms(x, g):
    return x * lax.rsqrt(jnp.mean(x * x, axis=-1, keepdims=True) + RMS_EPS) * g


def _rms_proj_kernel(x_ref, g_ref, w_ref, o_ref):
    h = _rms(x_ref[...], g_ref[...])
    o_ref[...] = jnp.dot(h.astype(BF16), w_ref[...], preferred_element_type=F32)


def _rms_proj(x, g, w):
    S, D = x.shape
    N = w.shape[1]
    tm = min(512, S)
    tn = _pick_tile(N, 2304)
    return pl.pallas_call(
        _rms_proj_kernel,
        grid=(N // tn, S // tm),
        in_specs=[pl.BlockSpec((tm, D), lambda j, i: (i, 0)),
                  pl.BlockSpec((1, D), lambda j, i: (0, 0)),
                  pl.BlockSpec((D, tn), lambda j, i: (0, j))],
        out_specs=pl.BlockSpec((tm, tn), lambda j, i: (i, j)),
        out_shape=jax.ShapeDtypeStruct((S, N), F32),
        compiler_params=_params("arbitrary", "arbitrary"),
        name="rms_proj",
    )(x, g.reshape(1, D), w)


def _rmsnorm_kernel(x_ref, g_ref, o_ref):
    o_ref[...] = _rms(x_ref[...], g_ref[...]).astype(o_ref.dtype)


def _rmsnorm(x, g, dtype):
    S, D = x.shape
    tm = min(2048, S)
    return pl.pallas_call(
        _rmsnorm_kernel,
        grid=(S // tm,),
        in_specs=[pl.BlockSpec((tm, D), lambda i: (i, 0)),
                  pl.BlockSpec((1, D), lambda i: (0, 0))],
        out_specs=pl.BlockSpec((tm, D), lambda i: (i, 0)),
        out_shape=jax.ShapeDtypeStruct((S, D), dtype),
        compiler_params=_params("arbitrary"),
        name="rmsnorm",
    )(x, g.reshape(1, D))


def _fox_cumsum_kernel(fl_ref, b_ref, hi_ref, mid_ref, lo_ref):
    H, R, _ = fl_ref.shape
    upper = (lax.broadcasted_iota(I32, (LANE, LANE), 0)
             <= lax.broadcasted_iota(I32, (LANE, LANE), 1)).astype(F32)
    strict_lower = (lax.broadcasted_iota(I32, (R, R), 0)
                    > lax.broadcasted_iota(I32, (R, R), 1)).astype(F32)
    for h in range(H):
        log_f = -_softplus(-(fl_ref[h] + b_ref[h]))
        within = jnp.dot(log_f, upper, preferred_element_type=F32, precision=HI)
        row_tot = jnp.broadcast_to(within[:, LANE - 1:LANE], (R, LANE))
        before = jnp.dot(strict_lower, row_tot, preferred_element_type=F32, precision=HI)
        hi, mid, lo = _split3((within + before) * LOG2E)
        hi_ref[h], mid_ref[h], lo_ref[h] = hi, mid, lo


def _fox_cumsum(fl, bias):
    H, R, _ = fl.shape
    return pl.pallas_call(
        _fox_cumsum_kernel,
        out_shape=[jax.ShapeDtypeStruct((H, R, LANE), BF16)] * 3,
        compiler_params=pltpu.CompilerParams(vmem_limit_bytes=VMEM_LIMIT),
        name="fox_cumsum",
    )(fl, jnp.broadcast_to(bias.reshape(H, 1, 1), (H, 1, LANE)))


FOX_KA = 128


def _fox_kernel(qa_ref, ka_ref, vT_ref, o_ref, lga_sc, lgb_sc, m_sc, l_sc, acc_sc, *, t, tc, tg):
    i = pl.program_id(1)
    hb = qa_ref.shape[0]
    last = (i * t) // tg
    chains = [(h, q0) for h in range(hb) for q0 in range(0, t, tc)]

    def logits_into(dst_ref, g):
        off = pl.multiple_of(jnp.minimum(g, last) * tg, tg)
        for n, (h, q0) in enumerate(chains):
            dst_ref[n] = jnp.dot(ka_ref[h, pl.ds(off, tg), :], qa_ref[h, :, q0:q0 + tc],
                                 preferred_element_type=F32)

    def update(lg_ref, g, masked):
        off = pl.multiple_of(g * tg, tg)
        for n, (h, q0) in enumerate(chains):
            s = lg_ref[n]
            if masked:
                kpos = off + lax.broadcasted_iota(I32, (tg, tc), 0)
                qpos = i * t + q0 + lax.broadcasted_iota(I32, (tg, tc), 1)
                s = jnp.where(kpos <= qpos, s, NEG)
            m_prev = m_sc[n]
            m_new = jnp.maximum(m_prev, jnp.max(s, axis=0, keepdims=True))
            alpha = jnp.exp2(m_prev - m_new)
            p = jnp.exp2(s - m_new)
            l_sc[n] = alpha * l_sc[n] + jnp.sum(p, axis=0, keepdims=True)
            acc_sc[n] = alpha * acc_sc[n] + jnp.dot(vT_ref[h, :, pl.ds(off, tg)], p.astype(BF16),
                                                    preferred_element_type=F32)
            m_sc[n] = m_new

    m_sc[...] = jnp.full(m_sc.shape, NEG, F32)
    l_sc[...] = jnp.zeros(l_sc.shape, F32)
    acc_sc[...] = jnp.zeros(acc_sc.shape, F32)
    logits_into(lga_sc, 0)

    def pair(j, c):
        logits_into(lgb_sc, 2 * j + 1)
        update(lga_sc, 2 * j, False)
        logits_into(lga_sc, 2 * j + 2)
        update(lgb_sc, 2 * j + 1, False)
        return c

    lax.fori_loop(0, last // 2, pair, 0)
    tail = 2 * (last // 2)
    logits_into(lgb_sc, tail + 1)
    update(lga_sc, tail, True)

    @pl.when(tail + 1 <= last)
    def _():
        update(lgb_sc, tail + 1, True)

    for n, (h, q0) in enumerate(chains):
        o_ref[h, :, q0:q0 + tc] = (acc_sc[n] / l_sc[n]).astype(o_ref.dtype)


def _fox_attention(qaT, ka, vT):
    H, KA, S = qaT.shape
    Dh = vT.shape[1]
    t = min(512, S)
    tc = min(256, t)
    tg = min(1024, S)
    hb = 2
    nc = hb * (t // tc)
    return pl.pallas_call(
        functools.partial(_fox_kernel, t=t, tc=tc, tg=tg),
        grid=(H // hb, S // t),
        in_specs=[pl.BlockSpec((hb, KA, t), lambda h, i: (h, 0, i)),
                  pl.BlockSpec((hb, S, KA), lambda h, i: (h, 0, 0)),
                  pl.BlockSpec((hb, Dh, S), lambda h, i: (h, 0, 0))],
        out_specs=pl.BlockSpec((hb, Dh, t), lambda h, i: (h, 0, i)),
        out_shape=jax.ShapeDtypeStruct((H, Dh, S), BF16),
        scratch_shapes=[pltpu.VMEM((nc, tg, tc), F32), pltpu.VMEM((nc, tg, tc), F32),
                        pltpu.VMEM((nc, 1, tc), F32), pltpu.VMEM((nc, 1, tc), F32),
                        pltpu.VMEM((nc, Dh, tc), F32)],
        compiler_params=_params("arbitrary", "arbitrary"),
        name="fox_attention",
    )(qaT, ka, vT)


DSA_T = 128


def _dsa_kernel(qiT_ref, w_ref, qT_ref, wukT_ref, wuvT_ref, ki_ref, c_ref, cT_ref, o_ref,
                keys_sc, qlat_sc, m_sc, l_sc, acc_sc, lga_sc, lgb_sc, rela_sc, relb_sc,
                *, t, tg, topk):
    i = pl.program_id(0)
    last = (i * t) // tg
    ng = last + 1

    qiT = qiT_ref[0]
    w = w_ref[0]

    def rel_into(dst_ref, g):
        off = pl.multiple_of(jnp.minimum(g, last) * tg, tg)
        dst_ref[...] = jnp.dot(ki_ref[pl.ds(off, tg), :], qiT, preferred_element_type=F32)

    def score_keys(rel_ref, g, masked):
        off = pl.multiple_of(g * tg, tg)
        sc = jnp.maximum(rel_ref[:, 0:t], 0.0) * w[0:1, :]
        for h in range(1, IDX_HEADS):
            sc = sc + jnp.maximum(rel_ref[:, h * t:(h + 1) * t], 0.0) * w[h:h + 1, :]
        bits = pltpu.bitcast(sc, I32)
        key = bits ^ ((bits >> 31) & 0x7FFFFFFF)
        kpos = off + lax.broadcasted_iota(I32, (tg, t), 0)
        key = jnp.where(jnp.abs(sc) < F32_MIN_NORMAL, -1 - kpos, key)
        if masked:
            qchunk = (i * t + lax.broadcasted_iota(I32, (tg, t), 1)) // CHUNK
            key = jnp.where(kpos // CHUNK <= qchunk, key, INT_MIN)
        keys_sc[pl.ds(off, tg), :] = key

    def fill_pair(j, c):
        rel_into(relb_sc, 2 * j + 1)
        score_keys(rela_sc, 2 * j, False)
        rel_into(rela_sc, 2 * j + 2)
        score_keys(relb_sc, 2 * j + 1, False)
        return c

    rel_into(rela_sc, 0)
    lax.fori_loop(0, last // 2, fill_pair, 0)
    tail = 2 * (last // 2)
    rel_into(relb_sc, tail + 1)
    score_keys(rela_sc, tail, True)

    @pl.when(tail + 1 <= last)
    def _():
        score_keys(relb_sc, tail + 1, True)

    def count_ge(cand):
        cb = jnp.broadcast_to(cand, (t, t))

        def body(g, acc):
            off = pl.multiple_of(g * tg, tg)
            for k in range(tg // t):
                acc = acc + (keys_sc[pl.ds(off + k * t, t), :] >= cb).astype(F32)
            return acc

        acc = lax.fori_loop(0, ng, body, jnp.zeros((t, t), F32))
        return jnp.sum(acc, axis=0, keepdims=True)

    def bisect(s, st):
        thr, n_ge = st
        cand = thr + jnp.left_shift(jnp.int32(1), 31 - s)
        c = count_ge(cand)
        keep = c >= topk
        return jnp.where(keep, cand, thr), jnp.where(keep, c, n_ge)

    thr, n_ge = lax.fori_loop(0, 32, bisect,
                              (jnp.full((1, t), INT_MIN, I32), jnp.zeros((1, t), F32)))
    thr = jnp.maximum(thr, INT_MIN + 1)
    has_ties = jnp.max(jnp.where(n_ge > topk, 1.0, 0.0)) > 0.0
    thr_b = jnp.broadcast_to(thr, (tg, t))

    for h in range(DSA_HEADS):
        ql = jnp.dot(wukT_ref[h], qT_ref[0, h], preferred_element_type=F32)
        qlat_sc[:, h * t:(h + 1) * t] = (ql * (DSA_HEAD_DIM ** -0.5 * LOG2E)).astype(BF16)
    m_sc[...] = jnp.full(m_sc.shape, NEG, F32)
    l_sc[...] = jnp.zeros(l_sc.shape, F32)
    acc_sc[...] = jnp.zeros(acc_sc.shape, F32)

    def logits_into(dst_ref, g):
        off = pl.multiple_of(jnp.minimum(g, last) * tg, tg)
        dst_ref[...] = jnp.dot(c_ref[pl.ds(off, tg), :], qlat_sc[...],
                               preferred_element_type=F32)

    def attend(lg_ref, g, seen_eq, with_ties, need):
        valid = g <= last
        off = pl.multiple_of(jnp.minimum(g, last) * tg, tg)
        key = keys_sc[pl.ds(off, tg), :]
        if with_ties:
            eq = (key == thr_b) & valid
            eqf = eq.astype(F32)
            strict_lower = (lax.broadcasted_iota(I32, (tg, tg), 0)
                            > lax.broadcasted_iota(I32, (tg, tg), 1)).astype(BF16)
            rank = jnp.dot(strict_lower, eqf.astype(BF16), preferred_element_type=F32) + seen_eq
            sel = (key > thr_b) | (eq & (rank < need))
            seen_eq = seen_eq + jnp.sum(eqf, axis=0, keepdims=True)
        else:
            sel = key >= thr_b
        bias = jnp.where(sel & valid, 0.0, NEG)
        probs, alphas = [], []
        for h in range(DSA_HEADS):
            s = lg_ref[:, h * t:(h + 1) * t] + bias
            m_prev = m_sc[h]
            m_new = jnp.maximum(m_prev, jnp.max(s, axis=0, keepdims=True))
            alpha = jnp.exp2(m_prev - m_new)
            p = jnp.exp2(s - m_new)
            l_sc[h] = alpha * l_sc[h] + jnp.sum(p, axis=0, keepdims=True)
            m_sc[h] = m_new
            probs.append(p.astype(BF16))
            alphas.append(alpha)
        pv = jnp.dot(cT_ref[:, pl.ds(off, tg)], jnp.concatenate(probs, axis=1),
                     preferred_element_type=F32)
        for h in range(DSA_HEADS):
            acc_sc[h] = alphas[h] * acc_sc[h] + pv[:, h * t:(h + 1) * t]
        return seen_eq

    def sweep(with_ties, need=None):
        logits_into(lga_sc, 0)

        def pair(j, seen_eq):
            g0 = 2 * j
            logits_into(lgb_sc, g0 + 1)
            seen_eq = attend(lga_sc, g0, seen_eq, with_ties, need)
            logits_into(lga_sc, g0 + 2)
            return attend(lgb_sc, g0 + 1, seen_eq, with_ties, need)

        lax.fori_loop(0, last // 2 + 1, pair, jnp.zeros((1, t), F32))

    @pl.when(has_ties)
    def _():
        sweep(True, topk - count_ge(thr + 1))

    @pl.when(jnp.logical_not(has_ties))
    def _():
        sweep(False)

    for h in range(DSA_HEADS):
        o_lat = (acc_sc[h] / l_sc[h]).astype(BF16)
        o_ref[0, h] = jnp.dot(wuvT_ref[h], o_lat, preferred_element_type=F32).astype(o_ref.dtype)


def _dsa_attention(qiT, w, qT, wukT, wuvT, ki, c, cT, topk):
    NB, H, Dh, t = qT.shape
    S = NB * t
    assert S < 2 ** 23
    R = DSA_KV_RANK
    tg = min(512, S)
    const2 = lambda i: (0, 0)
    const3 = lambda i: (0, 0, 0)
    return pl.pallas_call(
        functools.partial(_dsa_kernel, t=t, tg=tg, topk=topk),
        grid=(NB,),
        in_specs=[pl.BlockSpec((1, IDX_HEAD_DIM, IDX_HEADS * t), lambda i: (i, 0, 0)),
                  pl.BlockSpec((1, IDX_HEADS, t), lambda i: (i, 0, 0)),
                  pl.BlockSpec((1, H, Dh, t), lambda i: (i, 0, 0, 0)),
                  pl.BlockSpec((H, R, Dh), const3),
                  pl.BlockSpec((H, Dh, R), const3),
                  pl.BlockSpec((S, IDX_HEAD_DIM), const2, pipeline_mode=pl.Buffered(1)),
                  pl.BlockSpec((S, R), const2, pipeline_mode=pl.Buffered(1)),
                  pl.BlockSpec((R, S), const2, pipeline_mode=pl.Buffered(1))],
        out_specs=pl.BlockSpec((1, H, Dh, t), lambda i: (i, 0, 0, 0)),
        out_shape=jax.ShapeDtypeStruct((NB, H, Dh, t), BF16),
        scratch_shapes=[pltpu.VMEM((S, t), I32), pltpu.VMEM((R, H * t), BF16),
                        pltpu.VMEM((H, 1, t), F32), pltpu.VMEM((H, 1, t), F32),
                        pltpu.VMEM((H, R, t), F32),
                        pltpu.VMEM((tg, H * t), F32), pltpu.VMEM((tg, H * t), F32),
                        pltpu.VMEM((tg, IDX_HEADS * t), F32), pltpu.VMEM((tg, IDX_HEADS * t), F32)],
        compiler_params=_params("arbitrary"),
        name="dsa_attention",
    )(qiT, w, qT, wukT, wuvT, ki, c, cT)


def _shift_mix(p, prev_tail, mu, first_block):
    rows = lax.broadcasted_iota(I32, p.shape, 0)
    tail = jnp.where(first_block, 0.0, prev_tail)
    prev = jnp.where(rows == 0, tail, pltpu.roll(p, 1, axis=0))
    return p + (prev - p) * mu


def _split3(f):
    hi = f.astype(BF16)
    r1 = f - hi.astype(F32)
    mid = r1.astype(BF16)
    lo = (r1 - mid.astype(F32)).astype(BF16)
    return hi, mid, lo


def _dot3(x, w01):
    return sum(jnp.dot(part, w01, preferred_element_type=F32) for part in _split3(x))


def _dot3_left(w01, x):
    return sum(jnp.dot(w01, part, preferred_element_type=F32) for part in _split3(x))


def _rwkv_pre_kernel(*refs, with_vres):
    (p_ref, pp_ref, mu_ref, w0_ref, a0_ref, wwa_ref, gup_ref, kk_ref, ka_ref, rk_ref,
     tri_ref, blk_ref, bd_ref) = refs[:13]
    if with_vres:
        vup_ref, vv0_ref, vf_ref = refs[13:16]
    (at_o, rt_o, bt_o, kt_o, bh_o, kh_o, vb_o, pc_o, bonus_o, g_o, v_o) = refs[-11:]
    first = pl.program_id(0) == 0
    W = RWKV_W
    ps = _shift_mix(p_ref[...], pp_ref[7:8, :], mu_ref[...], first)
    r, k, v = ps[:, 0:W], ps[:, W:2 * W], ps[:, 2 * W:3 * W]
    wa = ps[:, 3 * W:3 * W + LANE]
    gl = ps[:, 3 * W + LANE:3 * W + 2 * LANE]
    lane = lax.broadcasted_iota(I32, wa.shape, 1)
    wa = jnp.where(lane < RWKV_W_LORA, jnp.tanh(wa), wa)
    up = jnp.dot(wa.astype(BF16), wwa_ref[...], preferred_element_type=F32)
    log_w = -_softplus(-(w0_ref[...] + up[:, 0:W])) - 0.5
    a = _sigmoid(a0_ref[...] + up[:, W:2 * W])
    g = jnp.dot(_sigmoid(gl).astype(BF16), gup_ref[...], preferred_element_type=F32)
    if with_vres:
        vl = ps[:, RWKV_IN:RWKV_IN + LANE]
        logit = vv0_ref[...] + jnp.dot(vl.astype(BF16), vup_ref[...], preferred_element_type=F32)
        v = v + (vf_ref[...] - v) * _sigmoid(logit)
    k2 = k * (1.0 + (a - 1.0) * ka_ref[...])
    kkr = k * kk_ref[...]
    bd = bd_ref[...]
    kk = kkr * lax.rsqrt(_dot3(kkr * kkr, bd) + 1e-12)
    lw = -jnp.exp(log_w)
    cum = _dot3_left(tri_ref[...], lw)
    cend = _dot3_left(blk_ref[...], lw)
    p_inv = jnp.exp(-cum)
    p_end = jnp.exp(cend - cum)
    beta = kk * a
    at_o[...] = (-kk * jnp.exp(cum - lw)).astype(BF16)
    rt_o[...] = (r * jnp.exp(cum)).astype(BF16)
    bt_o[...] = (beta * p_inv).astype(BF16)
    kt_o[...] = (k2 * p_inv).astype(BF16)
    bh_o[...] = (beta * p_end).astype(BF16)
    kh_o[...] = (k2 * p_end).astype(BF16)
    vb_o[...] = v.astype(BF16)
    pc_o[...] = jnp.exp(cend)
    bonus_o[...] = _dot3(r * k2 * rk_ref[...], bd) * v
    g_o[...] = g
    v_o[...] = v


def _block_ones(n, block, lower_tri=False):
    i = jnp.arange(n)
    m = (i[:, None] // block) == (i[None, :] // block)
    if lower_tri:
        m = m & (i[:, None] >= i[None, :])
    return m.astype(BF16)


def _rwkv_pre(p, mu, w0, a0, wwa, gup, k_k, k_a, r_k, vres=None):
    S, PW = p.shape
    W = RWKV_W
    tm = min(512, S)
    row = lambda i: (i, 0)
    const = lambda i: (0, 0)
    tail = lambda i: (jnp.maximum(i * (tm // 8) - 1, 0), 0)
    vec = lambda a: a.reshape(1, -1)
    args = [p, p, vec(mu), vec(w0), vec(a0), wwa, gup, vec(k_k), vec(k_a), vec(r_k),
            _block_ones(tm, CHUNK, lower_tri=True), _block_ones(tm, CHUNK),
            _block_ones(W, RWKV_HEAD_DIM)]
    specs = [pl.BlockSpec((tm, PW), row), pl.BlockSpec((8, PW), tail),
             pl.BlockSpec((1, PW), const), pl.BlockSpec((1, W), const), pl.BlockSpec((1, W), const),
             pl.BlockSpec(wwa.shape, const), pl.BlockSpec(gup.shape, const),
             pl.BlockSpec((1, W), const), pl.BlockSpec((1, W), const), pl.BlockSpec((1, W), const),
             pl.BlockSpec((tm, tm), const), pl.BlockSpec((tm, tm), const),
             pl.BlockSpec((W, W), const)]
    if vres is not None:
        vup, vv0, v_first = vres
        args += [vup, vec(vv0), v_first]
        specs += [pl.BlockSpec(vup.shape, const), pl.BlockSpec((1, W), const),
                  pl.BlockSpec((tm, W), row)]
    return pl.pallas_call(
        functools.partial(_rwkv_pre_kernel, with_vres=vres is not None),
        grid=(S // tm,),
        in_specs=specs,
        out_specs=[pl.BlockSpec((tm, W), row)] * 11,
        out_shape=[jax.ShapeDtypeStruct((S, W), BF16)] * 7 + [jax.ShapeDtypeStruct((S, W), F32)] * 4,
        compiler_params=_params("arbitrary"),
        name="rwkv_pre",
    )(*args)


def _rwkv_scan_kernel(at_ref, rt_ref, bt_ref, kt_ref, bh_ref, kh_ref, v_ref, pc_ref, y_ref, h_sc,
                      *, chunks):
    C = CHUNK
    N = RWKV_HEAD_DIM

    @pl.when(pl.program_id(0) == 0)
    def _():
        h_sc[...] = jnp.zeros(h_sc.shape, F32)

    ti = lax.broadcasted_iota(I32, (C, C), 0)
    tj = lax.broadcasted_iota(I32, (C, C), 1)
    lower_incl = ti >= tj
    lower_strict = ti > tj
    eye = (ti == tj).astype(F32)
    eye_n = (lax.broadcasted_iota(I32, (N, N), 0) == lax.broadcasted_iota(I32, (N, N), 1)).astype(F32)

    def mm(x, y):
        return jnp.dot(x.astype(BF16), y.astype(BF16), preferred_element_type=F32)

    def mm_nt(x, y):
        return lax.dot_general(x.astype(BF16), y.astype(BF16), (((1,), (1,)), ((), ())),
                               preferred_element_type=F32)

    def mm_tn(x, y):
        return lax.dot_general(x.astype(BF16), y.astype(BF16), (((0,), (0,)), ((), ())),
                               preferred_element_type=F32)

    units = [(c, h) for c in range(chunks) for h in range(RWKV_HEADS)]
    tile = lambda ref, u: ref[u[0] * C:(u[0] + 1) * C, u[1] * N:(u[1] + 1) * N]
    each = lambda fn: {u: fn(u) for u in units}

    At, Rt, Bt, Kt = (each(lambda u, r=ref: tile(r, u)) for ref in (at_ref, rt_ref, bt_ref, kt_ref))
    Bh, Kh, V = (each(lambda u, r=ref: tile(r, u)) for ref in (bh_ref, kh_ref, v_ref))
    AR = each(lambda u: jnp.concatenate([At[u], Rt[u]], axis=0))
    Mb = each(lambda u: mm_nt(AR[u], Bt[u]))
    Mk = each(lambda u: mm_nt(AR[u], Kt[u]))
    Lab = each(lambda u: jnp.where(lower_strict, Mb[u][0:C], 0.0))
    Mrb = each(lambda u: jnp.where(lower_incl, Mb[u][C:2 * C], 0.0))
    Lak = each(lambda u: jnp.where(lower_strict, Mk[u][0:C], 0.0))
    Mrk = each(lambda u: jnp.where(lower_incl, Mk[u][C:2 * C], 0.0))
    T = each(lambda u: eye + Lab[u])
    Lp = Lab
    span = 2
    while span < C:
        Lp = each(lambda u, Lp=Lp: mm(Lp[u], Lp[u]))
        T = each(lambda u, T=T, Lp=Lp: T[u] + mm(Lp[u], T[u]))
        span *= 2
    W1 = each(lambda u: mm(Lak[u], V[u]))
    A2 = each(lambda u: mm(T[u], At[u]))
    U0 = each(lambda u: mm(T[u], W1[u]))
    R2 = each(lambda u: Rt[u].astype(F32) + mm(Mrb[u], A2[u]))
    Y0 = each(lambda u: mm(Mrb[u], U0[u]) + mm(Mrk[u], V[u]))
    G = each(lambda u: eye_n * tile(pc_ref, u)[0:1, :] + mm_tn(Bh[u], A2[u]))
    H0 = each(lambda u: mm_tn(Bh[u], U0[u]) + mm_tn(Kh[u], V[u]))

    H = {h: h_sc[h] for h in range(RWKV_HEADS)}
    for c in range(chunks):
        ys = []
        for h in range(RWKV_HEADS):
            u = (c, h)
            ys.append(mm(R2[u], H[h]) + Y0[u])
            H[h] = mm(G[u], H[h]) + H0[u]
        y_ref[c * C:(c + 1) * C, :] = jnp.concatenate(ys, axis=1)
    for h in range(RWKV_HEADS):
        h_sc[h] = H[h]


def _rwkv_scan(at, rt, bt, kt, bh, kh, vb, pc):
    S, W = at.shape
    chunks = 2 if S % (2 * CHUNK) == 0 else 1
    tb = chunks * CHUNK
    seq = pl.BlockSpec((tb, W), lambda i: (i, 0))
    return pl.pallas_call(
        functools.partial(_rwkv_scan_kernel, chunks=chunks),
        grid=(S // tb,),
        in_specs=[seq] * 8,
        out_specs=seq,
        out_shape=jax.ShapeDtypeStruct((S, W), F32),
        scratch_shapes=[pltpu.VMEM((RWKV_HEADS, RWKV_HEAD_DIM, RWKV_HEAD_DIM), F32)],
        compiler_params=_params("arbitrary"),
        name="rwkv_scan",
    )(at, rt, bt, kt, bh, kh, vb, pc)


def _merge_kernel(x_ref, of_ref, od_ref, y_ref, bonus_ref, g_ref, lng_ref, lnb_ref, bd_ref,
                  gp_ref, bg_ref, pf_ref, pd_ref, pr_ref, wo_ref, gn_ref, xo_ref, h_ref):
    D = D_MODEL
    bd = bd_ref[...]
    inv_n = 1.0 / RWKV_HEAD_DIM
    y = y_ref[...]
    yc = y - _dot3(y, bd) * inv_n
    var = _dot3(yc * yc, bd) * inv_n
    yn = yc * lax.rsqrt(var + RWKV_LN_EPS) * lng_ref[...] + lnb_ref[...]
    o_rwkv = ((yn + bonus_ref[...]) * g_ref[...]).astype(BF16)
    gates = _sigmoid(gp_ref[...] + bg_ref[...])
    merged = (gates[:, 0:D] * jnp.dot(of_ref[...], pf_ref[...], preferred_element_type=F32)
              + gates[:, D:2 * D] * jnp.dot(od_ref[...], pd_ref[...], preferred_element_type=F32)
              + gates[:, 2 * D:3 * D] * jnp.dot(o_rwkv, pr_ref[...], preferred_element_type=F32))
    x = x_ref[...] + jnp.dot(merged.astype(BF16), wo_ref[...], preferred_element_type=F32)
    xo_ref[...] = x
    h_ref[...] = _rms(x, gn_ref[...]).astype(h_ref.dtype)


def _merge(x, o_fox, o_dsa, y_rwkv, bonus, g_rwkv, ln_g, ln_b, gate_p, b_gate, p_fox, p_dsa, p_rwkv,
           w_out, g_ffn):
    S, D = x.shape
    W = RWKV_W
    tm = min(512, S)
    row = lambda i: (i, 0)
    const = lambda i: (0, 0)
    return pl.pallas_call(
        _merge_kernel,
        grid=(S // tm,),
        in_specs=[pl.BlockSpec((tm, D), row), pl.BlockSpec((tm, FOX_W), row),
                  pl.BlockSpec((tm, DSA_W), row), pl.BlockSpec((tm, W), row),
                  pl.BlockSpec((tm, W), row), pl.BlockSpec((tm, W), row),
                  pl.BlockSpec((1, W), const), pl.BlockSpec((1, W), const),
                  pl.BlockSpec((W, W), const),
                  pl.BlockSpec((tm, 3 * D), row), pl.BlockSpec((1, 3 * D), const),
                  pl.BlockSpec((FOX_W, D), const), pl.BlockSpec((DSA_W, D), const),
                  pl.BlockSpec((W, D), const), pl.BlockSpec((D, D), const),
                  pl.BlockSpec((1, D), const)],
        out_specs=[pl.BlockSpec((tm, D), row), pl.BlockSpec((tm, D), row)],
        out_shape=[jax.ShapeDtypeStruct((S, D), F32), jax.ShapeDtypeStruct((S, D), BF16)],
        compiler_params=_params("arbitrary"),
        name="merge",
    )(x, o_fox, o_dsa, y_rwkv, bonus, g_rwkv, ln_g.reshape(1, W), ln_b.reshape(1, W),
      _block_ones(W, RWKV_HEAD_DIM), gate_p, b_gate.reshape(1, -1), p_fox, p_dsa, p_rwkv, w_out,
      g_ffn.reshape(1, D))


def _swiglu_tile(h, wg, wu):
    gate = jnp.dot(h, wg, preferred_element_type=F32)
    up = jnp.dot(h, wu, preferred_element_type=F32)
    return gate * _sigmoid(gate) * up


def _ffn_kernel(x_ref, h_ref, wg_ref, wu_ref, wd_ref, gf_ref, o_ref, acc_sc, *, final_norm):
    f = pl.program_id(1)

    @pl.when(f == 0)
    def _():
        acc_sc[...] = x_ref[...]

    act = _swiglu_tile(h_ref[...], wg_ref[...], wu_ref[...])
    acc_sc[...] += jnp.dot(act.astype(BF16), wd_ref[...], preferred_element_type=F32)

    @pl.when(f == pl.num_programs(1) - 1)
    def _():
        y = acc_sc[...]
        o_ref[...] = _rms(y, gf_ref[...]) if final_norm else y


def _ffn(x, h, wg, wu, wd, g_final, final_norm):
    S, D = x.shape
    Fd = wg.shape[1]
    tm = min(512, S)
    tf = _pick_tile(Fd, 1408)
    return pl.pallas_call(
        functools.partial(_ffn_kernel, final_norm=final_norm),
        grid=(S // tm, Fd // tf),
        in_specs=[pl.BlockSpec((tm, D), lambda i, f: (i, 0)),
                  pl.BlockSpec((tm, D), lambda i, f: (i, 0)),
                  pl.BlockSpec((D, tf), lambda i, f: (0, f)),
                  pl.BlockSpec((D, tf), lambda i, f: (0, f)),
                  pl.BlockSpec((tf, D), lambda i, f: (f, 0)),
                  pl.BlockSpec((1, D), lambda i, f: (0, 0))],
        out_specs=pl.BlockSpec((tm, D), lambda i, f: (i, 0)),
        out_shape=jax.ShapeDtypeStruct((S, D), F32),
        scratch_shapes=[pltpu.VMEM((tm, D), F32)],
        compiler_params=_params("arbitrary", "arbitrary"),
        name="ffn",
    )(x, h, wg, wu, wd, g_final.reshape(1, D))


MOE_CHUNK = 128


def _moe_kernel(x_ref, h_ref, rw_ref, rb_ref, wg_ref, wu_ref, wd_ref, gf_ref, o_ref,
                acc_sc, gate_sc, rank_sc, cnt_sc, hs_sc, ys_sc, *, final_norm):
    e = pl.program_id(1)
    f = pl.program_id(2)
    last_f = pl.num_programs(2) - 1
    T, D = h_ref.shape
    C = MOE_CHUNK
    lane = lax.broadcasted_iota(I32, (T, LANE), 1)

    @pl.when((e == 0) & (f == 0))
    def _():
        acc_sc[...] = x_ref[...]
        logits = jnp.dot(h_ref[...], rw_ref[...], preferred_element_type=F32) + rb_ref[...]
        logits = jnp.where(lane < N_EXPERTS, logits, -jnp.inf)
        v1 = jnp.max(logits, axis=-1, keepdims=True)
        i1 = jnp.min(jnp.where(logits == v1, lane, LANE), axis=-1, keepdims=True)
        rest = jnp.where(lane == i1, -jnp.inf, logits)
        v2 = jnp.max(rest, axis=-1, keepdims=True)
        i2 = jnp.min(jnp.where(rest == v2, lane, LANE), axis=-1, keepdims=True)
        e2 = jnp.exp(v2 - v1)
        p1 = 1.0 / (1.0 + e2)
        gate_sc[...] = jnp.where(lane == i1, p1, 0.0) + jnp.where(lane == i2, e2 * p1, 0.0)
        routed = ((lane == i1) | (lane == i2)).astype(F32)
        B = 256
        earlier = (lax.broadcasted_iota(I32, (B, B), 0) > lax.broadcasted_iota(I32, (B, B), 1)).astype(BF16)
        seen = jnp.zeros((1, LANE), F32)
        for b in range(T // B):
            blk = routed[b * B:(b + 1) * B]
            before = jnp.dot(earlier, blk.astype(BF16), preferred_element_type=F32) + seen
            rank_sc[b * B:(b + 1) * B, :] = jnp.where(blk > 0.0, before, -1.0)
            seen = seen + jnp.sum(blk, axis=0, keepdims=True)
        cnt_sc[...] = seen

    n_e = jnp.sum(jnp.where(lax.broadcasted_iota(I32, (1, LANE), 1) == e, cnt_sc[...], 0.0))
    n_chunks = sum(jnp.where(n_e > j * C, 1, 0) for j in range(T // C))

    @pl.when(f == 0)
    def _():
        pick8 = (lax.broadcasted_iota(I32, (8, LANE), 1) == e).astype(BF16)
        rk = rank_sc[...]
        rk_hi = rk.astype(BF16)
        rk_lo = (rk - rk_hi.astype(F32)).astype(BF16)
        nt = (((1,), (1,)), ((), ()))
        rrow = (lax.dot_general(pick8, rk_hi, nt, preferred_element_type=F32)
                + lax.dot_general(pick8, rk_lo, nt, preferred_element_type=F32))[0:1, :]

        def gather(c, carry):
            off = pl.multiple_of(c * C, C)
            slot = (off + lax.broadcasted_iota(I32, (C, T), 0)).astype(F32)
            onehot = (rrow == slot).astype(BF16)
            hs_sc[pl.ds(off, C), :] = jnp.dot(onehot, h_ref[...],
                                              preferred_element_type=F32).astype(BF16)
            ys_sc[pl.ds(off, C), :] = jnp.zeros((C, D), F32)
            return carry

        lax.fori_loop(0, n_chunks, gather, 0)

    def expert(c, carry):
        off = pl.multiple_of(c * C, C)
        act = _swiglu_tile(hs_sc[pl.ds(off, C), :], wg_ref[0], wu_ref[0])
        ys_sc[pl.ds(off, C), :] += jnp.dot(act.astype(BF16), wd_ref[0], preferred_element_type=F32)
        return carry

    lax.fori_loop(0, n_chunks, expert, 0)

    @pl.when(f == last_f)
    def _():
        mine = lane == e
        gate = jnp.sum(jnp.where(mine, gate_sc[...], 0.0), axis=-1, keepdims=True)
        rank = jnp.sum(jnp.where(mine, rank_sc[...], 0.0), axis=-1, keepdims=True)
        rank_b = jnp.broadcast_to(rank, (T, C))
        col = lax.broadcasted_iota(I32, (T, C), 1)

        def scatter(c, carry):
            off = pl.multiple_of(c * C, C)
            onehot_t = (rank_b == (col + off).astype(F32)).astype(BF16)
            y = jnp.dot(onehot_t, ys_sc[pl.ds(off, C), :].astype(BF16), preferred_element_type=F32)
            acc_sc[...] += gate * y
            return carry

        lax.fori_loop(0, n_chunks, scatter, 0)

    @pl.when((e == pl.num_programs(1) - 1) & (f == last_f))
    def _():
        y = acc_sc[...]
        o_ref[...] = _rms(y, gf_ref[...]) if final_norm else y


def _moe(x, h, rw, rb, wg, wu, wd, g_final, final_norm):
    S, D = x.shape
    E, _, Fe = wg.shape
    tm = min(1024, S)
    tf = _pick_tile(Fe, 896)
    return pl.pallas_call(
        functools.partial(_moe_kernel, final_norm=final_norm),
        grid=(S // tm, E, Fe // tf),
        in_specs=[pl.BlockSpec((tm, D), lambda i, e, f: (i, 0)),
                  pl.BlockSpec((tm, D), lambda i, e, f: (i, 0)),
                  pl.BlockSpec((D, LANE), lambda i, e, f: (0, 0)),
                  pl.BlockSpec((1, LANE), lambda i, e, f: (0, 0)),
                  pl.BlockSpec((1, D, tf), lambda i, e, f: (e, 0, f)),
                  pl.BlockSpec((1, D, tf), lambda i, e, f: (e, 0, f)),
                  pl.BlockSpec((1, tf, D), lambda i, e, f: (e, f, 0)),
                  pl.BlockSpec((1, D), lambda i, e, f: (0, 0))],
        out_specs=pl.BlockSpec((tm, D), lambda i, e, f: (i, 0)),
        out_shape=jax.ShapeDtypeStruct((S, D), F32),
        scratch_shapes=[pltpu.VMEM((tm, D), F32), pltpu.VMEM((tm, LANE), F32),
                        pltpu.VMEM((tm, LANE), F32), pltpu.VMEM((1, LANE), F32),
                        pltpu.VMEM((tm, D), BF16), pltpu.VMEM((tm, D), F32)],
        compiler_params=_params("arbitrary", "arbitrary", "arbitrary"),
        name="moe",
    )(x, h, rw, rb, wg, wu, wd, g_final.reshape(1, D))


def _cast_kernel(x_ref, o_ref):
    o_ref[...] = x_ref[...].astype(o_ref.dtype)


def _to_bf16(w):
    E, A, B = w.shape
    ta = 256
    return pl.pallas_call(
        _cast_kernel,
        grid=(E, A // ta),
        in_specs=[pl.BlockSpec((1, ta, B), lambda e, a: (e, a, 0))],
        out_specs=pl.BlockSpec((1, ta, B), lambda e, a: (e, a, 0)),
        out_shape=jax.ShapeDtypeStruct(w.shape, BF16),
        compiler_params=_params("arbitrary", "arbitrary"),
        name="to_bf16",
    )(w)


def _pad_cols(a, n):
    return jnp.pad(a, ((0, 0), (0, n - a.shape[1])))


def kernel(x, w_in, b_gate, g_mix, fox_f_bias, dsa_kv_norm, dsa_w_uk, dsa_w_uv, rwkv_mu, rwkv_w0, rwkv_w_up, rwkv_a0, rwkv_a_up, rwkv_g_up, rwkv_k_k, rwkv_k_a, rwkv_r_k, rwkv_ln_g, rwkv_ln_b, vres_down, vres_mu, vres_up, vres_v0, p_fox, p_dsa, p_rwkv, w_out, g_ffn, ffn_w_gate, ffn_w_up, ffn_w_down, router_w, router_b, moe_w_gate, moe_w_up, moe_w_down, g_final):
    B, S, D = x.shape
    assert B == 1 and D == D_MODEL and S % LANE == 0
    depth = w_in.shape[0]
    topk = min(IDX_TOPK, S // 4)
    bf = lambda a: a.astype(BF16)
    xs = x[0]
    v_first = None
    w_in_bf = _to_bf16(w_in)
    for l in range(depth):
        wl = w_in_bf[l]
        w_fox, w_dsa, w_rwkv, w_gate = (wl[:, :FOX_IN], wl[:, FOX_IN:FOX_IN + DSA_IN],
                                        wl[:, FOX_IN + DSA_IN:FOX_IN + DSA_IN + RWKV_IN],
                                        wl[:, FOX_IN + DSA_IN + RWKV_IN:])
        o1 = DSA_W + DSA_KV_RANK + IDX_W
        w_att = jnp.concatenate([_pad_cols(w_fox, 3 * FOX_W + LANE), w_dsa[:, :o1],
                                 _pad_cols(w_dsa[:, o1:], LANE)], axis=1)
        if l > 0:
            w_rwkv = jnp.concatenate([w_rwkv, bf(_pad_cols(vres_down[l - 1], LANE))], axis=1)
        pa = _rms_proj(xs, g_mix[l], w_att)
        pr = _rms_proj(xs, g_mix[l], w_rwkv)
        gate_p = _rms_proj(xs, g_mix[l], w_gate)

        c0 = 3 * FOX_W
        fl = pa[:, c0:c0 + FOX_HEADS].T.reshape(FOX_HEADS, S // LANE, LANE)
        Fh, Fm, Fl = (a.reshape(FOX_HEADS, S) for a in _fox_cumsum(fl, fox_f_bias[l]))
        qT = bf(pa[:, 0:FOX_W].T.reshape(FOX_HEADS, FOX_HEAD_DIM, S) * (FOX_HEAD_DIM ** -0.5 * LOG2E))
        r = jnp.arange(FOX_KA - FOX_HEAD_DIM)
        rq, ck = r[None, :, None], r[None, None, :]
        pick = lambda idx, a, b, c, d: jnp.where(idx == a[0], a[1], jnp.where(
            idx == b[0], b[1], jnp.where(idx == c[0], c[1], jnp.where(idx < 6, d, 0)))).astype(BF16)
        fq = pick(rq, (0, Fh[:, None, :]), (1, Fm[:, None, :]), (2, Fl[:, None, :]), 1)
        fk = pick(ck, (3, -Fh[:, :, None]), (4, -Fm[:, :, None]), (5, -Fl[:, :, None]), 1)
        qaT = jnp.concatenate([qT, fq], axis=1)
        k4 = bf(pa[:, FOX_W:2 * FOX_W].reshape(S, FOX_HEADS, FOX_HEAD_DIM).transpose(1, 0, 2))
        ka = jnp.concatenate([k4, fk], axis=2)
        fvT = bf(pa[:, 2 * FOX_W:3 * FOX_W].T.reshape(FOX_HEADS, FOX_HEAD_DIM, S))
        o_fox = _fox_attention(qaT, ka, fvT).reshape(FOX_W, S).T

        t = min(DSA_T, S)
        nb = S // t
        c1 = c0 + LANE
        dqT = bf(pa[:, c1:c1 + DSA_W].T.reshape(DSA_HEADS, DSA_HEAD_DIM, nb, t).transpose(2, 0, 1, 3))
        c2 = c1 + DSA_W
        ckv = _rmsnorm(pa[:, c2:c2 + DSA_KV_RANK], dsa_kv_norm[l], BF16)
        c3 = c2 + DSA_KV_RANK
        qiT = bf(pa[:, c3:c3 + IDX_W].T.reshape(IDX_HEADS, IDX_HEAD_DIM, nb, t).transpose(2, 1, 0, 3)
                 .reshape(nb, IDX_HEAD_DIM, IDX_HEADS * t))
        c4 = c3 + IDX_W
        ki = bf(pa[:, c4:c4 + IDX_HEAD_DIM])
        c5 = c4 + IDX_HEAD_DIM
        wi = pa[:, c5:c5 + IDX_HEADS].T.reshape(IDX_HEADS, nb, t).transpose(1, 0, 2) * (IDX_W ** -0.5)
        o_dsa = _dsa_attention(qiT, wi, dqT, bf(dsa_w_uk[l].transpose(0, 2, 1)),
                               bf(dsa_w_uv[l].transpose(0, 2, 1)), ki, ckv, ckv.T, topk)
        o_dsa = o_dsa.transpose(1, 2, 0, 3).reshape(DSA_W, S).T

        zw = jnp.zeros((RWKV_W_LORA, RWKV_W), F32)
        wwa = bf(jnp.concatenate([jnp.concatenate([rwkv_w_up[l], zw], axis=1),
                                  jnp.concatenate([zw, rwkv_a_up[l]], axis=1)], axis=0))
        vres = None
        mu = rwkv_mu[l]
        if l > 0:
            vup = jnp.pad(vres_up[l - 1], ((0, LANE - RWKV_V_LORA), (0, 0)))
            vres = (bf(vup), vres_v0[l - 1], v_first)
            mu = jnp.concatenate([mu, jnp.pad(vres_mu[l - 1], (0, LANE - RWKV_V_LORA))])
        *scan_ops, bonus, g_rwkv, v = _rwkv_pre(
            pr, mu, rwkv_w0[l], rwkv_a0[l], wwa, bf(rwkv_g_up[l]),
            rwkv_k_k[l], rwkv_k_a[l], rwkv_r_k[l], vres)
        if l == 0:
            v_first = v
        y_rwkv = _rwkv_scan(*scan_ops)

        xs, h2 = _merge(xs, o_fox, o_dsa, y_rwkv, bonus, g_rwkv, rwkv_ln_g[l], rwkv_ln_b[l],
                        gate_p, b_gate[l], bf(p_fox[l]), bf(p_dsa[l]), bf(p_rwkv[l]), bf(w_out[l]),
                        g_ffn[l])

        last = l == depth - 1
        if l % 2 == 0:
            xs = _ffn(xs, h2, bf(ffn_w_gate[l // 2]), bf(ffn_w_up[l // 2]), bf(ffn_w_down[l // 2]),
                      g_final, last)
        else:
            rw = bf(_pad_cols(router_w[l // 2], LANE))
            rb = _pad_cols(router_b[l // 2].reshape(1, -1), LANE)
            xs = _moe(xs, h2, rw, rb, _to_bf16(moe_w_gate[l // 2]), _to_bf16(moe_w_up[l // 2]),
                      _to_bf16(moe_w_down[l // 2]), g_final, last)
    return xs[None]
```

```python
import functools

import jax
import jax.numpy as jnp
from jax import lax
from jax.experimental import pallas as pl
from jax.experimental.pallas import tpu as pltpu

F32 = jnp.float32
BF16 = jnp.bfloat16
I32 = jnp.int32

D_MODEL = 1024
CHUNK = 64
RMS_EPS = 1e-6
FOX_HEADS, FOX_HEAD_DIM = 4, 64
DSA_HEADS, DSA_HEAD_DIM, DSA_KV_RANK = 4, 64, 128
IDX_HEADS, IDX_HEAD_DIM, IDX_TOPK = 8, 32, 256
RWKV_HEADS, RWKV_HEAD_DIM = 8, 64
RWKV_W_LORA, RWKV_A_LORA, RWKV_V_LORA, RWKV_G_LORA = 64, 64, 32, 128
RWKV_LN_EPS = 64e-5
FOX_W = FOX_HEADS * FOX_HEAD_DIM
DSA_W = DSA_HEADS * DSA_HEAD_DIM
RWKV_W = RWKV_HEADS * RWKV_HEAD_DIM
IDX_W = IDX_HEADS * IDX_HEAD_DIM
N_EXPERTS = 8
FOX_IN = 3 * FOX_W + FOX_HEADS
DSA_IN = DSA_W + DSA_KV_RANK + IDX_W + IDX_HEAD_DIM + IDX_HEADS
RWKV_IN = 3 * RWKV_W + RWKV_W_LORA + RWKV_A_LORA + RWKV_G_LORA

LANE = 128
VMEM_LIMIT = 52 * 1024 * 1024
NEG = -1e30
INT_MIN = -(2 ** 31)
LOG2E = 1.4426950408889634
F32_MIN_NORMAL = 2.0 ** -126
HI = lax.Precision.HIGHEST


def _params(*sem):
    return pltpu.CompilerParams(dimension_semantics=sem, vmem_limit_bytes=VMEM_LIMIT)


def _pick_tile(n, cap):
    best = LANE
    for t in range(LANE, min(n, cap) + 1, LANE):
        if n % t == 0:
            best = t
    return best


def _softplus(x):
    return jnp.maximum(x, 0.0) + jnp.log1p(jnp.exp(-jnp.abs(x)))


def _sigmoid(x):
    return 1.0 / (1.0 + jnp.exp(-x))


def _rms(x, g):
    return x * lax.rsqrt(jnp.mean(x * x, axis=-1, keepdims=True) + RMS_EPS) * g


def _rms_proj_kernel(x_ref, g_ref, w_ref, o_ref):
    h = _rms(x_ref[...], g_ref[...])
    o_ref[...] = jnp.dot(h.astype(BF16), w_ref[...], preferred_element_type=F32)


def _rms_proj(x, g, w):
    S, D = x.shape
    N = w.shape[1]
    tm = min(512, S)
    tn = _pick_tile(N, 2304)
    return pl.pallas_call(
        _rms_proj_kernel,
        grid=(N // tn, S // tm),
        in_specs=[pl.BlockSpec((tm, D), lambda j, i: (i, 0)),
                  pl.BlockSpec((1, D), lambda j, i: (0, 0)),
                  pl.BlockSpec((D, tn), lambda j, i: (0, j))],
        out_specs=pl.BlockSpec((tm, tn), lambda j, i: (i, j)),
        out_shape=jax.ShapeDtypeStruct((S, N), F32),
        compiler_params=_params("arbitrary", "arbitrary"),
        name="rms_proj",
    )(x, g.reshape(1, D), w)


def _rmsnorm_kernel(x_ref, g_ref, o_ref):
    o_ref[...] = _rms(x_ref[...], g_ref[...]).astype(o_ref.dtype)


def _rmsnorm(x, g, dtype):
    S, D = x.shape
    tm = min(2048, S)
    return pl.pallas_call(
        _rmsnorm_kernel,
        grid=(S // tm,),
        in_specs=[pl.BlockSpec((tm, D), lambda i: (i, 0)),
                  pl.BlockSpec((1, D), lambda i: (0, 0))],
        out_specs=pl.BlockSpec((tm, D), lambda i: (i, 0)),
        out_shape=jax.ShapeDtypeStruct((S, D), dtype),
        compiler_params=_params("arbitrary"),
        name="rmsnorm",
    )(x, g.reshape(1, D))


def _fox_cumsum_kernel(fl_ref, b_ref, hi_ref, mid_ref, lo_ref):
    H, R, _ = fl_ref.shape
    upper = (lax.broadcasted_iota(I32, (LANE, LANE), 0)
             <= lax.broadcasted_iota(I32, (LANE, LANE), 1)).astype(F32)
    strict_lower = (lax.broadcasted_iota(I32, (R, R), 0)
                    > lax.broadcasted_iota(I32, (R, R), 1)).astype(F32)
    for h in range(H):
        log_f = -_softplus(-(fl_ref[h] + b_ref[h]))
        within = jnp.dot(log_f, upper, preferred_element_type=F32, precision=HI)
        row_tot = jnp.broadcast_to(within[:, LANE - 1:LANE], (R, LANE))
        before = jnp.dot(strict_lower, row_tot, preferred_element_type=F32, precision=HI)
        hi, mid, lo = _split3((within + before) * LOG2E)
        hi_ref[h], mid_ref[h], lo_ref[h] = hi, mid, lo


def _fox_cumsum(fl, bias):
    H, R, _ = fl.shape
    return pl.pallas_call(
        _fox_cumsum_kernel,
        out_shape=[jax.ShapeDtypeStruct((H, R, LANE), BF16)] * 3,
        compiler_params=pltpu.CompilerParams(vmem_limit_bytes=VMEM_LIMIT),
        name="fox_cumsum",
    )(fl, jnp.broadcast_to(bias.reshape(H, 1, 1), (H, 1, LANE)))


FOX_KA = 128


def _fox_kernel(qa_ref, ka_ref, vT_ref, o_ref, lga_sc, lgb_sc, m_sc, l_sc, acc_sc, *, t, tc, tg):
    i = pl.program_id(1)
    hb = qa_ref.shape[0]
    last = (i * t) // tg
    chains = [(h, q0) for h in range(hb) for q0 in range(0, t, tc)]

    def logits_into(dst_ref, g):
        off = pl.multiple_of(jnp.minimum(g, last) * tg, tg)
        for n, (h, q0) in enumerate(chains):
            dst_ref[n] = jnp.dot(ka_ref[h, pl.ds(off, tg), :], qa_ref[h, :, q0:q0 + tc],
                                 preferred_element_type=F32)

    def update(lg_ref, g, masked):
        off = pl.multiple_of(g * tg, tg)
        for n, (h, q0) in enumerate(chains):
            s = lg_ref[n]
            if masked:
                kpos = off + lax.broadcasted_iota(I32, (tg, tc), 0)
                qpos = i * t + q0 + lax.broadcasted_iota(I32, (tg, tc), 1)
                s = jnp.where(kpos <= qpos, s, NEG)
            m_prev = m_sc[n]
            m_new = jnp.maximum(m_prev, jnp.max(s, axis=0, keepdims=True))
            alpha = jnp.exp2(m_prev - m_new)
            p = jnp.exp2(s - m_new)
            l_sc[n] = alpha * l_sc[n] + jnp.sum(p, axis=0, keepdims=True)
            acc_sc[n] = alpha * acc_sc[n] + jnp.dot(vT_ref[h, :, pl.ds(off, tg)], p.astype(BF16),
                                                    preferred_element_type=F32)
            m_sc[n] = m_new

    m_sc[...] = jnp.full(m_sc.shape, NEG, F32)
    l_sc[...] = jnp.zeros(l_sc.shape, F32)
    acc_sc[...] = jnp.zeros(acc_sc.shape, F32)
    logits_into(lga_sc, 0)

    def pair(j, c):
        logits_into(lgb_sc, 2 * j + 1)
        update(lga_sc, 2 * j, False)
        logits_into(lga_sc, 2 * j + 2)
        update(lgb_sc, 2 * j + 1, False)
        return c

    lax.fori_loop(0, last // 2, pair, 0)
    tail = 2 * (last // 2)
    logits_into(lgb_sc, tail + 1)
    update(lga_sc, tail, True)

    @pl.when(tail + 1 <= last)
    def _():
        update(lgb_sc, tail + 1, True)

    for n, (h, q0) in enumerate(chains):
        o_ref[h, :, q0:q0 + tc] = (acc_sc[n] / l_sc[n]).astype(o_ref.dtype)


def _fox_attention(qaT, ka, vT):
    H, KA, S = qaT.shape
    Dh = vT.shape[1]
    t = min(512, S)
    tc = min(256, t)
    tg = min(1024, S)
    hb = 2
    nc = hb * (t // tc)
    return pl.pallas_call(
        functools.partial(_fox_kernel, t=t, tc=tc, tg=tg),
        grid=(H // hb, S // t),
        in_specs=[pl.BlockSpec((hb, KA, t), lambda h, i: (h, 0, i)),
                  pl.BlockSpec((hb, S, KA), lambda h, i: (h, 0, 0)),
                  pl.BlockSpec((hb, Dh, S), lambda h, i: (h, 0, 0))],
        out_specs=pl.BlockSpec((hb, Dh, t), lambda h, i: (h, 0, i)),
        out_shape=jax.ShapeDtypeStruct((H, Dh, S), BF16),
        scratch_shapes=[pltpu.VMEM((nc, tg, tc), F32), pltpu.VMEM((nc, tg, tc), F32),
                        pltpu.VMEM((nc, 1, tc), F32), pltpu.VMEM((nc, 1, tc), F32),
                        pltpu.VMEM((nc, Dh, tc), F32)],
        compiler_params=_params("arbitrary", "arbitrary"),
        name="fox_attention",
    )(qaT, ka, vT)


DSA_T = 256


def _dsa_kernel(qiT_ref, w_ref, qT_ref, wukT_ref, wuvT_ref, ki_ref, c_ref, cT_ref, o_ref,
                keys_sc, qlat_sc, m_sc, l_sc, acc_sc, lga_sc, lgb_sc, rela_sc, relb_sc,
                *, t, tg, topk):
    i = pl.program_id(0)
    last = (i * t) // tg
    ng = last + 1

    qiT = qiT_ref[0]
    w = w_ref[0]

    def rel_into(dst_ref, g):
        off = pl.multiple_of(jnp.minimum(g, last) * tg, tg)
        dst_ref[...] = jnp.dot(ki_ref[pl.ds(off, tg), :], qiT, preferred_element_type=F32)

    def score_keys(rel_ref, g, masked):
        off = pl.multiple_of(g * tg, tg)
        sc = jnp.maximum(rel_ref[:, 0:t], 0.0) * w[0:1, :]
        for h in range(1, IDX_HEADS):
            sc = sc + jnp.maximum(rel_ref[:, h * t:(h + 1) * t], 0.0) * w[h:h + 1, :]
        bits = pltpu.bitcast(sc, I32)
        key = bits ^ ((bits >> 31) & 0x7FFFFFFF)
        kpos = off + lax.broadcasted_iota(I32, (tg, t), 0)
        key = jnp.where(jnp.abs(sc) < F32_MIN_NORMAL, -1 - kpos, key)
        if masked:
            qchunk = (i * t + lax.broadcasted_iota(I32, (tg, t), 1)) // CHUNK
            key = jnp.where(kpos // CHUNK <= qchunk, key, INT_MIN)
        keys_sc[pl.ds(off, tg), :] = key

    def fill_pair(j, c):
        rel_into(relb_sc, 2 * j + 1)
        score_keys(rela_sc, 2 * j, False)
        rel_into(rela_sc, 2 * j + 2)
        score_keys(relb_sc, 2 * j + 1, False)
        return c

    rel_into(rela_sc, 0)
    lax.fori_loop(0, last // 2, fill_pair, 0)
    tail = 2 * (last // 2)
    rel_into(relb_sc, tail + 1)
    score_keys(rela_sc, tail, True)

    @pl.when(tail + 1 <= last)
    def _():
        score_keys(relb_sc, tail + 1, True)

    def count_ge(cand):
        rows = 16 * LANE * 8 // t
        cb = jnp.broadcast_to(cand, (rows, t))

        def body(g, acc):
            off = pl.multiple_of(g * tg, tg)
            for k in range(tg // rows):
                acc = acc + (keys_sc[pl.ds(off + k * rows, rows), :] >= cb).astype(F32)
            return acc

        acc = lax.fori_loop(0, ng, body, jnp.zeros((rows, t), F32))
        return jnp.sum(acc, axis=0, keepdims=True)

    def bisect(s, st):
        thr, n_ge = st
        cand = thr + jnp.left_shift(jnp.int32(1), 31 - s)
        c = count_ge(cand)
        keep = c >= topk
        return jnp.where(keep, cand, thr), jnp.where(keep, c, n_ge)

    thr, n_ge = lax.fori_loop(0, 32, bisect,
                              (jnp.full((1, t), INT_MIN, I32), jnp.zeros((1, t), F32)))
    thr = jnp.maximum(thr, INT_MIN + 1)
    has_ties = jnp.max(jnp.where(n_ge > topk, 1.0, 0.0)) > 0.0
    thr_b = jnp.broadcast_to(thr, (tg, t))

    for h in range(DSA_HEADS):
        ql = jnp.dot(wukT_ref[h], qT_ref[0, h], preferred_element_type=F32)
        qlat_sc[:, h * t:(h + 1) * t] = (ql * (DSA_HEAD_DIM ** -0.5 * LOG2E)).astype(BF16)
    m_sc[...] = jnp.full(m_sc.shape, NEG, F32)
    l_sc[...] = jnp.zeros(l_sc.shape, F32)
    acc_sc[...] = jnp.zeros(acc_sc.shape, F32)

    def logits_into(dst_ref, g):
        off = pl.multiple_of(jnp.minimum(g, last) * tg, tg)
        dst_ref[...] = jnp.dot(c_ref[pl.ds(off, tg), :], qlat_sc[...],
                               preferred_element_type=F32)

    def attend(lg_ref, g, seen_eq, with_ties, need):
        valid = g <= last
        off = pl.multiple_of(jnp.minimum(g, last) * tg, tg)
        key = keys_sc[pl.ds(off, tg), :]
        if with_ties:
            eq = (key == thr_b) & valid
            eqf = eq.astype(F32)
            strict_lower = (lax.broadcasted_iota(I32, (tg, tg), 0)
                            > lax.broadcasted_iota(I32, (tg, tg), 1)).astype(BF16)
            rank = jnp.dot(strict_lower, eqf.astype(BF16), preferred_element_type=F32) + seen_eq
            sel = (key > thr_b) | (eq & (rank < need))
            seen_eq = seen_eq + jnp.sum(eqf, axis=0, keepdims=True)
        else:
            sel = key >= thr_b
        bias = jnp.where(sel & valid, 0.0, NEG)
        probs, alphas = [], []
        for h in range(DSA_HEADS):
            s = lg_ref[:, h * t:(h + 1) * t] + bias
            m_prev = m_sc[h]
            m_new = jnp.maximum(m_prev, jnp.max(s, axis=0, keepdims=True))
            alpha = jnp.exp2(m_prev - m_new)
            p = jnp.exp2(s - m_new)
            l_sc[h] = alpha * l_sc[h] + jnp.sum(p, axis=0, keepdims=True)
            m_sc[h] = m_new
            probs.append(p.astype(BF16))
            alphas.append(alpha)
        pv = jnp.dot(cT_ref[:, pl.ds(off, tg)], jnp.concatenate(probs, axis=1),
                     preferred_element_type=F32)
        for h in range(DSA_HEADS):
            acc_sc[h] = alphas[h] * acc_sc[h] + pv[:, h * t:(h + 1) * t]
        return seen_eq

    def sweep(with_ties, need=None):
        logits_into(lga_sc, 0)

        def pair(j, seen_eq):
            g0 = 2 * j
            logits_into(lgb_sc, g0 + 1)
            seen_eq = attend(lga_sc, g0, seen_eq, with_ties, need)
            logits_into(lga_sc, g0 + 2)
            return attend(lgb_sc, g0 + 1, seen_eq, with_ties, need)

        lax.fori_loop(0, last // 2 + 1, pair, jnp.zeros((1, t), F32))

    @pl.when(has_ties)
    def _():
        sweep(True, topk - count_ge(thr + 1))

    @pl.when(jnp.logical_not(has_ties))
    def _():
        sweep(False)

    for h in range(DSA_HEADS):
        o_lat = (acc_sc[h] / l_sc[h]).astype(BF16)
        o_ref[0, h] = jnp.dot(wuvT_ref[h], o_lat, preferred_element_type=F32).astype(o_ref.dtype)


def _dsa_attention(qiT, w, qT, wukT, wuvT, ki, c, cT, topk):
    NB, H, Dh, t = qT.shape
    S = NB * t
    assert S < 2 ** 23
    R = DSA_KV_RANK
    tg = min(512, S)
    const2 = lambda i: (0, 0)
    const3 = lambda i: (0, 0, 0)
    return pl.pallas_call(
        functools.partial(_dsa_kernel, t=t, tg=tg, topk=topk),
        grid=(NB,),
        in_specs=[pl.BlockSpec((1, IDX_HEAD_DIM, IDX_HEADS * t), lambda i: (i, 0, 0)),
                  pl.BlockSpec((1, IDX_HEADS, t), lambda i: (i, 0, 0)),
                  pl.BlockSpec((1, H, Dh, t), lambda i: (i, 0, 0, 0)),
                  pl.BlockSpec((H, R, Dh), const3),
                  pl.BlockSpec((H, Dh, R), const3),
                  pl.BlockSpec((S, IDX_HEAD_DIM), const2, pipeline_mode=pl.Buffered(1)),
                  pl.BlockSpec((S, R), const2, pipeline_mode=pl.Buffered(1)),
                  pl.BlockSpec((R, S), const2, pipeline_mode=pl.Buffered(1))],
        out_specs=pl.BlockSpec((1, H, Dh, t), lambda i: (i, 0, 0, 0)),
        out_shape=jax.ShapeDtypeStruct((NB, H, Dh, t), BF16),
        scratch_shapes=[pltpu.VMEM((S, t), I32), pltpu.VMEM((R, H * t), BF16),
                        pltpu.VMEM((H, 1, t), F32), pltpu.VMEM((H, 1, t), F32),
                        pltpu.VMEM((H, R, t), F32),
                        pltpu.VMEM((tg, H * t), F32), pltpu.VMEM((tg, H * t), F32),
                        pltpu.VMEM((tg, IDX_HEADS * t), F32), pltpu.VMEM((tg, IDX_HEADS * t), F32)],
        compiler_params=_params("arbitrary"),
        name="dsa_attention",
    )(qiT, w, qT, wukT, wuvT, ki, c, cT)


def _shift_mix(p, prev_tail, mu, first_block):
    rows = lax.broadcasted_iota(I32, p.shape, 0)
    tail = jnp.where(first_block, 0.0, prev_tail)
    prev = jnp.where(rows == 0, tail, pltpu.roll(p, 1, axis=0))
    return p + (prev - p) * mu


def _split3(f):
    hi = f.astype(BF16)
    r1 = f - hi.astype(F32)
    mid = r1.astype(BF16)
    lo = (r1 - mid.astype(F32)).astype(BF16)
    return hi, mid, lo


def _dot3(x, w01):
    return sum(jnp.dot(part, w01, preferred_element_type=F32) for part in _split3(x))


def _dot3_left(w01, x):
    return sum(jnp.dot(w01, part, preferred_element_type=F32) for part in _split3(x))


def _rwkv_pre_kernel(*refs, with_vres):
    (p_ref, pp_ref, mu_ref, w0_ref, a0_ref, wwa_ref, gup_ref, kk_ref, ka_ref, rk_ref,
     tri_ref, blk_ref, bd_ref) = refs[:13]
    if with_vres:
        vup_ref, vv0_ref, vf_ref = refs[13:16]
    (at_o, rt_o, bt_o, kt_o, bh_o, kh_o, vb_o, pc_o, bonus_o, g_o, v_o) = refs[-11:]
    first = pl.program_id(0) == 0
    W = RWKV_W
    ps = _shift_mix(p_ref[...], pp_ref[7:8, :], mu_ref[...], first)
    r, k, v = ps[:, 0:W], ps[:, W:2 * W], ps[:, 2 * W:3 * W]
    wa = ps[:, 3 * W:3 * W + LANE]
    gl = ps[:, 3 * W + LANE:3 * W + 2 * LANE]
    lane = lax.broadcasted_iota(I32, wa.shape, 1)
    wa = jnp.where(lane < RWKV_W_LORA, jnp.tanh(wa), wa)
    up = jnp.dot(wa.astype(BF16), wwa_ref[...], preferred_element_type=F32)
    log_w = -_softplus(-(w0_ref[...] + up[:, 0:W])) - 0.5
    a = _sigmoid(a0_ref[...] + up[:, W:2 * W])
    g = jnp.dot(_sigmoid(gl).astype(BF16), gup_ref[...], preferred_element_type=F32)
    if with_vres:
        vl = ps[:, RWKV_IN:RWKV_IN + LANE]
        logit = vv0_ref[...] + jnp.dot(vl.astype(BF16), vup_ref[...], preferred_element_type=F32)
        v = v + (vf_ref[...] - v) * _sigmoid(logit)
    k2 = k * (1.0 + (a - 1.0) * ka_ref[...])
    kkr = k * kk_ref[...]
    bd = bd_ref[...]
    kk = kkr * lax.rsqrt(_dot3(kkr * kkr, bd) + 1e-12)
    lw = -jnp.exp(log_w)
    cum = _dot3_left(tri_ref[...], lw)
    cend = _dot3_left(blk_ref[...], lw)
    p_inv = jnp.exp(-cum)
    p_end = jnp.exp(cend - cum)
    beta = kk * a
    at_o[...] = (-kk * jnp.exp(cum - lw)).astype(BF16)
    rt_o[...] = (r * jnp.exp(cum)).astype(BF16)
    bt_o[...] = (beta * p_inv).astype(BF16)
    kt_o[...] = (k2 * p_inv).astype(BF16)
    bh_o[...] = (beta * p_end).astype(BF16)
    kh_o[...] = (k2 * p_end).astype(BF16)
    vb_o[...] = v.astype(BF16)
    pc_o[...] = jnp.exp(cend)
    bonus_o[...] = _dot3(r * k2 * rk_ref[...], bd) * v
    g_o[...] = g
    v_o[...] = v


def _block_ones(n, block, lower_tri=False):
    i = jnp.arange(n)
    m = (i[:, None] // block) == (i[None, :] // block)
    if lower_tri:
        m = m & (i[:, None] >= i[None, :])
    return m.astype(BF16)


def _rwkv_pre(p, mu, w0, a0, wwa, gup, k_k, k_a, r_k, vres=None):
    S, PW = p.shape
    W = RWKV_W
    tm = min(512, S)
    row = lambda i: (i, 0)
    const = lambda i: (0, 0)
    tail = lambda i: (jnp.maximum(i * (tm // 8) - 1, 0), 0)
    vec = lambda a: a.reshape(1, -1)
    args = [p, p, vec(mu), vec(w0), vec(a0), wwa, gup, vec(k_k), vec(k_a), vec(r_k),
            _block_ones(tm, CHUNK, lower_tri=True), _block_ones(tm, CHUNK),
            _block_ones(W, RWKV_HEAD_DIM)]
    specs = [pl.BlockSpec((tm, PW), row), pl.BlockSpec((8, PW), tail),
             pl.BlockSpec((1, PW), const), pl.BlockSpec((1, W), const), pl.BlockSpec((1, W), const),
             pl.BlockSpec(wwa.shape, const), pl.BlockSpec(gup.shape, const),
             pl.BlockSpec((1, W), const), pl.BlockSpec((1, W), const), pl.BlockSpec((1, W), const),
             pl.BlockSpec((tm, tm), const), pl.BlockSpec((tm, tm), const),
             pl.BlockSpec((W, W), const)]
    if vres is not None:
        vup, vv0, v_first = vres
        args += [vup, vec(vv0), v_first]
        specs += [pl.BlockSpec(vup.shape, const), pl.BlockSpec((1, W), const),
                  pl.BlockSpec((tm, W), row)]
    return pl.pallas_call(
        functools.partial(_rwkv_pre_kernel, with_vres=vres is not None),
        grid=(S // tm,),
        in_specs=specs,
        out_specs=[pl.BlockSpec((tm, W), row)] * 11,
        out_shape=[jax.ShapeDtypeStruct((S, W), BF16)] * 7 + [jax.ShapeDtypeStruct((S, W), F32)] * 4,
        compiler_params=_params("arbitrary"),
        name="rwkv_pre",
    )(*args)


def _rwkv_scan_kernel(at_ref, rt_ref, bt_ref, kt_ref, bh_ref, kh_ref, v_ref, pc_ref, y_ref, h_sc,
                      *, chunks):
    C = CHUNK
    N = RWKV_HEAD_DIM

    @pl.when(pl.program_id(0) == 0)
    def _():
        h_sc[...] = jnp.zeros(h_sc.shape, F32)

    ti = lax.broadcasted_iota(I32, (C, C), 0)
    tj = lax.broadcasted_iota(I32, (C, C), 1)
    lower_incl = ti >= tj
    lower_strict = ti > tj
    eye = (ti == tj).astype(F32)
    eye_n = (lax.broadcasted_iota(I32, (N, N), 0) == lax.broadcasted_iota(I32, (N, N), 1)).astype(F32)

    def mm(x, y):
        return jnp.dot(x.astype(BF16), y.astype(BF16), preferred_element_type=F32)

    def mm_nt(x, y):
        return lax.dot_general(x.astype(BF16), y.astype(BF16), (((1,), (1,)), ((), ())),
                               preferred_element_type=F32)

    def mm_tn(x, y):
        return lax.dot_general(x.astype(BF16), y.astype(BF16), (((0,), (0,)), ((), ())),
                               preferred_element_type=F32)

    units = [(c, h) for c in range(chunks) for h in range(RWKV_HEADS)]
    tile = lambda ref, u: ref[u[0] * C:(u[0] + 1) * C, u[1] * N:(u[1] + 1) * N]
    each = lambda fn: {u: fn(u) for u in units}

    At, Rt, Bt, Kt = (each(lambda u, r=ref: tile(r, u)) for ref in (at_ref, rt_ref, bt_ref, kt_ref))
    Bh, Kh, V = (each(lambda u, r=ref: tile(r, u)) for ref in (bh_ref, kh_ref, v_ref))
    AR = each(lambda u: jnp.concatenate([At[u], Rt[u]], axis=0))
    Mb = each(lambda u: mm_nt(AR[u], Bt[u]))
    Mk = each(lambda u: mm_nt(AR[u], Kt[u]))
    Lab = each(lambda u: jnp.where(lower_strict, Mb[u][0:C], 0.0))
    Mrb = each(lambda u: jnp.where(lower_incl, Mb[u][C:2 * C], 0.0))
    Lak = each(lambda u: jnp.where(lower_strict, Mk[u][0:C], 0.0))
    Mrk = each(lambda u: jnp.where(lower_incl, Mk[u][C:2 * C], 0.0))
    T = each(lambda u: eye + Lab[u])
    Lp = Lab
    span = 2
    while span < C:
        Lp = each(lambda u, Lp=Lp: mm(Lp[u], Lp[u]))
        T = each(lambda u, T=T, Lp=Lp: T[u] + mm(Lp[u], T[u]))
        span *= 2
    W1 = each(lambda u: mm(Lak[u], V[u]))
    A2 = each(lambda u: mm(T[u], At[u]))
    U0 = each(lambda u: mm(T[u], W1[u]))
    R2 = each(lambda u: Rt[u].astype(F32) + mm(Mrb[u], A2[u]))
    Y0 = each(lambda u: mm(Mrb[u], U0[u]) + mm(Mrk[u], V[u]))
    G = each(lambda u: eye_n * tile(pc_ref, u)[0:1, :] + mm_tn(Bh[u], A2[u]))
    H0 = each(lambda u: mm_tn(Bh[u], U0[u]) + mm_tn(Kh[u], V[u]))

    H = {h: h_sc[h] for h in range(RWKV_HEADS)}
    for c in range(chunks):
        ys = []
        for h in range(RWKV_HEADS):
            u = (c, h)
            ys.append(mm(R2[u], H[h]) + Y0[u])
            H[h] = mm(G[u], H[h]) + H0[u]
        y_ref[c * C:(c + 1) * C, :] = jnp.concatenate(ys, axis=1)
    for h in range(RWKV_HEADS):
        h_sc[h] = H[h]


def _rwkv_scan(at, rt, bt, kt, bh, kh, vb, pc):
    S, W = at.shape
    chunks = 2 if S % (2 * CHUNK) == 0 else 1
    tb = chunks * CHUNK
    seq = pl.BlockSpec((tb, W), lambda i: (i, 0))
    return pl.pallas_call(
        functools.partial(_rwkv_scan_kernel, chunks=chunks),
        grid=(S // tb,),
        in_specs=[seq] * 8,
        out_specs=seq,
        out_shape=jax.ShapeDtypeStruct((S, W), F32),
        scratch_shapes=[pltpu.VMEM((RWKV_HEADS, RWKV_HEAD_DIM, RWKV_HEAD_DIM), F32)],
        compiler_params=_params("arbitrary"),
        name="rwkv_scan",
    )(at, rt, bt, kt, bh, kh, vb, pc)


def _merge_kernel(x_ref, of_ref, od_ref, y_ref, bonus_ref, g_ref, lng_ref, lnb_ref, bd_ref,
                  gp_ref, bg_ref, pf_ref, pd_ref, pr_ref, wo_ref, gn_ref, xo_ref, h_ref):
    D = D_MODEL
    bd = bd_ref[...]
    inv_n = 1.0 / RWKV_HEAD_DIM
    y = y_ref[...]
    yc = y - _dot3(y, bd) * inv_n
    var = _dot3(yc * yc, bd) * inv_n
    yn = yc * lax.rsqrt(var + RWKV_LN_EPS) * lng_ref[...] + lnb_ref[...]
    o_rwkv = ((yn + bonus_ref[...]) * g_ref[...]).astype(BF16)
    gates = _sigmoid(gp_ref[...] + bg_ref[...])
    merged = (gates[:, 0:D] * jnp.dot(of_ref[...], pf_ref[...], preferred_element_type=F32)
              + gates[:, D:2 * D] * jnp.dot(od_ref[...], pd_ref[...], preferred_element_type=F32)
              + gates[:, 2 * D:3 * D] * jnp.dot(o_rwkv, pr_ref[...], preferred_element_type=F32))
    x = x_ref[...] + jnp.dot(merged.astype(BF16), wo_ref[...], preferred_element_type=F32)
    xo_ref[...] = x
    h_ref[...] = _rms(x, gn_ref[...]).astype(h_ref.dtype)


def _merge(x, o_fox, o_dsa, y_rwkv, bonus, g_rwkv, ln_g, ln_b, gate_p, b_gate, p_fox, p_dsa, p_rwkv,
           w_out, g_ffn):
    S, D = x.shape
    W = RWKV_W
    tm = min(512, S)
    row = lambda i: (i, 0)
    const = lambda i: (0, 0)
    return pl.pallas_call(
        _merge_kernel,
        grid=(S // tm,),
        in_specs=[pl.BlockSpec((tm, D), row), pl.BlockSpec((tm, FOX_W), row),
                  pl.BlockSpec((tm, DSA_W), row), pl.BlockSpec((tm, W), row),
                  pl.BlockSpec((tm, W), row), pl.BlockSpec((tm, W), row),
                  pl.BlockSpec((1, W), const), pl.BlockSpec((1, W), const),
                  pl.BlockSpec((W, W), const),
                  pl.BlockSpec((tm, 3 * D), row), pl.BlockSpec((1, 3 * D), const),
                  pl.BlockSpec((FOX_W, D), const), pl.BlockSpec((DSA_W, D), const),
                  pl.BlockSpec((W, D), const), pl.BlockSpec((D, D), const),
                  pl.BlockSpec((1, D), const)],
        out_specs=[pl.BlockSpec((tm, D), row), pl.BlockSpec((tm, D), row)],
        out_shape=[jax.ShapeDtypeStruct((S, D), F32), jax.ShapeDtypeStruct((S, D), BF16)],
        compiler_params=_params("arbitrary"),
        name="merge",
    )(x, o_fox, o_dsa, y_rwkv, bonus, g_rwkv, ln_g.reshape(1, W), ln_b.reshape(1, W),
      _block_ones(W, RWKV_HEAD_DIM), gate_p, b_gate.reshape(1, -1), p_fox, p_dsa, p_rwkv, w_out,
      g_ffn.reshape(1, D))


def _swiglu_tile(h, wg, wu):
    gate = jnp.dot(h, wg, preferred_element_type=F32)
    up = jnp.dot(h, wu, preferred_element_type=F32)
    return gate * _sigmoid(gate) * up


def _ffn_kernel(x_ref, h_ref, wg_ref, wu_ref, wd_ref, gf_ref, o_ref, acc_sc, *, final_norm):
    f = pl.program_id(1)

    @pl.when(f == 0)
    def _():
        acc_sc[...] = x_ref[...]

    act = _swiglu_tile(h_ref[...], wg_ref[...], wu_ref[...])
    acc_sc[...] += jnp.dot(act.astype(BF16), wd_ref[...], preferred_element_type=F32)

    @pl.when(f == pl.num_programs(1) - 1)
    def _():
        y = acc_sc[...]
        o_ref[...] = _rms(y, gf_ref[...]) if final_norm else y


def _ffn(x, h, wg, wu, wd, g_final, final_norm):
    S, D = x.shape
    Fd = wg.shape[1]
    tm = min(512, S)
    tf = _pick_tile(Fd, 1408)
    return pl.pallas_call(
        functools.partial(_ffn_kernel, final_norm=final_norm),
        grid=(S // tm, Fd // tf),
        in_specs=[pl.BlockSpec((tm, D), lambda i, f: (i, 0)),
                  pl.BlockSpec((tm, D), lambda i, f: (i, 0)),
                  pl.BlockSpec((D, tf), lambda i, f: (0, f)),
                  pl.BlockSpec((D, tf), lambda i, f: (0, f)),
                  pl.BlockSpec((tf, D), lambda i, f: (f, 0)),
                  pl.BlockSpec((1, D), lambda i, f: (0, 0))],
        out_specs=pl.BlockSpec((tm, D), lambda i, f: (i, 0)),
        out_shape=jax.ShapeDtypeStruct((S, D), F32),
        scratch_shapes=[pltpu.VMEM((tm, D), F32)],
        compiler_params=_params("arbitrary", "arbitrary"),
        name="ffn",
    )(x, h, wg, wu, wd, g_final.reshape(1, D))


MOE_CHUNK = 128


def _moe_kernel(x_ref, h_ref, rw_ref, rb_ref, wg_ref, wu_ref, wd_ref, gf_ref, o_ref,
                acc_sc, gate_sc, rank_sc, cnt_sc, hs_sc, ys_sc, *, final_norm):
    e = pl.program_id(1)
    f = pl.program_id(2)
    last_f = pl.num_programs(2) - 1
    T, D = h_ref.shape
    C = MOE_CHUNK
    lane = lax.broadcasted_iota(I32, (T, LANE), 1)

    @pl.when((e == 0) & (f == 0))
    def _():
        acc_sc[...] = x_ref[...]
        logits = jnp.dot(h_ref[...], rw_ref[...], preferred_element_type=F32) + rb_ref[...]
        logits = jnp.where(lane < N_EXPERTS, logits, -jnp.inf)
        v1 = jnp.max(logits, axis=-1, keepdims=True)
        i1 = jnp.min(jnp.where(logits == v1, lane, LANE), axis=-1, keepdims=True)
        rest = jnp.where(lane == i1, -jnp.inf, logits)
        v2 = jnp.max(rest, axis=-1, keepdims=True)
        i2 = jnp.min(jnp.where(rest == v2, lane, LANE), axis=-1, keepdims=True)
        e2 = jnp.exp(v2 - v1)
        p1 = 1.0 / (1.0 + e2)
        gate_sc[...] = jnp.where(lane == i1, p1, 0.0) + jnp.where(lane == i2, e2 * p1, 0.0)
        routed = ((lane == i1) | (lane == i2)).astype(F32)
        B = 256
        earlier = (lax.broadcasted_iota(I32, (B, B), 0) > lax.broadcasted_iota(I32, (B, B), 1)).astype(BF16)
        seen = jnp.zeros((1, LANE), F32)
        for b in range(T // B):
            blk = routed[b * B:(b + 1) * B]
            before = jnp.dot(earlier, blk.astype(BF16), preferred_element_type=F32) + seen
            rank_sc[b * B:(b + 1) * B, :] = jnp.where(blk > 0.0, before, -1.0)
            seen = seen + jnp.sum(blk, axis=0, keepdims=True)
        cnt_sc[...] = seen

    n_e = jnp.sum(jnp.where(lax.broadcasted_iota(I32, (1, LANE), 1) == e, cnt_sc[...], 0.0))
    n_chunks = sum(jnp.where(n_e > j * C, 1, 0) for j in range(T // C))

    @pl.when(f == 0)
    def _():
        pick8 = (lax.broadcasted_iota(I32, (8, LANE), 1) == e).astype(BF16)
        rk = rank_sc[...]
        rk_hi = rk.astype(BF16)
        rk_lo = (rk - rk_hi.astype(F32)).astype(BF16)
        nt = (((1,), (1,)), ((), ()))
        rrow = (lax.dot_general(pick8, rk_hi, nt, preferred_element_type=F32)
                + lax.dot_general(pick8, rk_lo, nt, preferred_element_type=F32))[0:1, :]

        def gather(c, carry):
            off = pl.multiple_of(c * C, C)
            slot = (off + lax.broadcasted_iota(I32, (C, T), 0)).astype(F32)
            onehot = (rrow == slot).astype(BF16)
            hs_sc[pl.ds(off, C), :] = jnp.dot(onehot, h_ref[...],
                                              preferred_element_type=F32).astype(BF16)
            ys_sc[pl.ds(off, C), :] = jnp.zeros((C, D), F32)
            return carry

        lax.fori_loop(0, n_chunks, gather, 0)

    def expert(c, carry):
        off = pl.multiple_of(c * C, C)
        act = _swiglu_tile(hs_sc[pl.ds(off, C), :], wg_ref[0], wu_ref[0])
        ys_sc[pl.ds(off, C), :] += jnp.dot(act.astype(BF16), wd_ref[0], preferred_element_type=F32)
        return carry

    lax.fori_loop(0, n_chunks, expert, 0)

    @pl.when(f == last_f)
    def _():
        mine = lane == e
        gate = jnp.sum(jnp.where(mine, gate_sc[...], 0.0), axis=-1, keepdims=True)
        rank = jnp.sum(jnp.where(mine, rank_sc[...], 0.0), axis=-1, keepdims=True)
        rank_b = jnp.broadcast_to(rank, (T, C))
        col = lax.broadcasted_iota(I32, (T, C), 1)

        def scatter(c, carry):
            off = pl.multiple_of(c * C, C)
            onehot_t = (rank_b == (col + off).astype(F32)).astype(BF16)
            y = jnp.dot(onehot_t, ys_sc[pl.ds(off, C), :].astype(BF16), preferred_element_type=F32)
            acc_sc[...] += gate * y
            return carry

        lax.fori_loop(0, n_chunks, scatter, 0)

    @pl.when((e == pl.num_programs(1) - 1) & (f == last_f))
    def _():
        y = acc_sc[...]
        o_ref[...] = _rms(y, gf_ref[...]) if final_norm else y


def _moe(x, h, rw, rb, wg, wu, wd, g_final, final_norm):
    S, D = x.shape
    E, _, Fe = wg.shape
    tm = min(1024, S)
    tf = _pick_tile(Fe, 896)
    return pl.pallas_call(
        functools.partial(_moe_kernel, final_norm=final_norm),
        grid=(S // tm, E, Fe // tf),
        in_specs=[pl.BlockSpec((tm, D), lambda i, e, f: (i, 0)),
                  pl.BlockSpec((tm, D), lambda i, e, f: (i, 0)),
                  pl.BlockSpec((D, LANE), lambda i, e, f: (0, 0)),
                  pl.BlockSpec((1, LANE), lambda i, e, f: (0, 0)),
                  pl.BlockSpec((1, D, tf), lambda i, e, f: (e, 0, f)),
                  pl.BlockSpec((1, D, tf), lambda i, e, f: (e, 0, f)),
                  pl.BlockSpec((1, tf, D), lambda i, e, f: (e, f, 0)),
                  pl.BlockSpec((1, D), lambda i, e, f: (0, 0))],
        out_specs=pl.BlockSpec((tm, D), lambda i, e, f: (i, 0)),
        out_shape=jax.ShapeDtypeStruct((S, D), F32),
        scratch_shapes=[pltpu.VMEM((tm, D), F32), pltpu.VMEM((tm, LANE), F32),
                        pltpu.VMEM((tm, LANE), F32), pltpu.VMEM((1, LANE), F32),
                        pltpu.VMEM((tm, D), BF16), pltpu.VMEM((tm, D), F32)],
        compiler_params=_params("arbitrary", "arbitrary", "arbitrary"),
        name="moe",
    )(x, h, rw, rb, wg, wu, wd, g_final.reshape(1, D))


def _cast_kernel(x_ref, o_ref):
    o_ref[...] = x_ref[...].astype(o_ref.dtype)


def _to_bf16(w):
    E, A, B = w.shape
    ta = 256
    return pl.pallas_call(
        _cast_kernel,
        grid=(E, A // ta),
        in_specs=[pl.BlockSpec((1, ta, B), lambda e, a: (e, a, 0))],
        out_specs=pl.BlockSpec((1, ta, B), lambda e, a: (e, a, 0)),
        out_shape=jax.ShapeDtypeStruct(w.shape, BF16),
        compiler_params=_params("arbitrary", "arbitrary"),
        name="to_bf16",
    )(w)


def _pad_cols(a, n):
    return jnp.pad(a, ((0, 0), (0, n - a.shape[1])))


def kernel(x, w_in, b_gate, g_mix, fox_f_bias, dsa_kv_norm, dsa_w_uk, dsa_w_uv, rwkv_mu, rwkv_w0, rwkv_w_up, rwkv_a0, rwkv_a_up, rwkv_g_up, rwkv_k_k, rwkv_k_a, rwkv_r_k, rwkv_ln_g, rwkv_ln_b, vres_down, vres_mu, vres_up, vres_v0, p_fox, p_dsa, p_rwkv, w_out, g_ffn, ffn_w_gate, ffn_w_up, ffn_w_down, router_w, router_b, moe_w_gate, moe_w_up, moe_w_down, g_final):
    B, S, D = x.shape
    assert B == 1 and D == D_MODEL and S % LANE == 0
    depth = w_in.shape[0]
    topk = min(IDX_TOPK, S // 4)
    bf = lambda a: a.astype(BF16)
    xs = x[0]
    v_first = None
    w_in_bf = _to_bf16(w_in)
    for l in range(depth):
        wl = w_in_bf[l]
        w_fox, w_dsa, w_rwkv, w_gate = (wl[:, :FOX_IN], wl[:, FOX_IN:FOX_IN + DSA_IN],
                                        wl[:, FOX_IN + DSA_IN:FOX_IN + DSA_IN + RWKV_IN],
                                        wl[:, FOX_IN + DSA_IN + RWKV_IN:])
        o1 = DSA_W + DSA_KV_RANK + IDX_W
        w_att = jnp.concatenate([_pad_cols(w_fox, 3 * FOX_W + LANE), w_dsa[:, :o1],
                                 _pad_cols(w_dsa[:, o1:], LANE)], axis=1)
        if l > 0:
            w_rwkv = jnp.concatenate([w_rwkv, bf(_pad_cols(vres_down[l - 1], LANE))], axis=1)
        pa = _rms_proj(xs, g_mix[l], w_att)
        pr = _rms_proj(xs, g_mix[l], w_rwkv)
        gate_p = _rms_proj(xs, g_mix[l], w_gate)

        c0 = 3 * FOX_W
        fl = pa[:, c0:c0 + FOX_HEADS].T.reshape(FOX_HEADS, S // LANE, LANE)
        Fh, Fm, Fl = (a.reshape(FOX_HEADS, S) for a in _fox_cumsum(fl, fox_f_bias[l]))
        qT = bf(pa[:, 0:FOX_W].T.reshape(FOX_HEADS, FOX_HEAD_DIM, S) * (FOX_HEAD_DIM ** -0.5 * LOG2E))
        r = jnp.arange(FOX_KA - FOX_HEAD_DIM)
        rq, ck = r[None, :, None], r[None, None, :]
        pick = lambda idx, a, b, c, d: jnp.where(idx == a[0], a[1], jnp.where(
            idx == b[0], b[1], jnp.where(idx == c[0], c[1], jnp.where(idx < 6, d, 0)))).astype(BF16)
        fq = pick(rq, (0, Fh[:, None, :]), (1, Fm[:, None, :]), (2, Fl[:, None, :]), 1)
        fk = pick(ck, (3, -Fh[:, :, None]), (4, -Fm[:, :, None]), (5, -Fl[:, :, None]), 1)
        qaT = jnp.concatenate([qT, fq], axis=1)
        k4 = bf(pa[:, FOX_W:2 * FOX_W].reshape(S, FOX_HEADS, FOX_HEAD_DIM).transpose(1, 0, 2))
        ka = jnp.concatenate([k4, fk], axis=2)
        fvT = bf(pa[:, 2 * FOX_W:3 * FOX_W].T.reshape(FOX_HEADS, FOX_HEAD_DIM, S))
        o_fox = _fox_attention(qaT, ka, fvT).reshape(FOX_W, S).T

        t = min(DSA_T, S)
        nb = S // t
        c1 = c0 + LANE
        dqT = bf(pa[:, c1:c1 + DSA_W].T.reshape(DSA_HEADS, DSA_HEAD_DIM, nb, t).transpose(2, 0, 1, 3))
        c2 = c1 + DSA_W
        ckv = _rmsnorm(pa[:, c2:c2 + DSA_KV_RANK], dsa_kv_norm[l], BF16)
        c3 = c2 + DSA_KV_RANK
        qiT = bf(pa[:, c3:c3 + IDX_W].T.reshape(IDX_HEADS, IDX_HEAD_DIM, nb, t).transpose(2, 1, 0, 3)
                 .reshape(nb, IDX_HEAD_DIM, IDX_HEADS * t))
        c4 = c3 + IDX_W
        ki = bf(pa[:, c4:c4 + IDX_HEAD_DIM])
        c5 = c4 + IDX_HEAD_DIM
        wi = pa[:, c5:c5 + IDX_HEADS].T.reshape(IDX_HEADS, nb, t).transpose(1, 0, 2) * (IDX_W ** -0.5)
        o_dsa = _dsa_attention(qiT, wi, dqT, bf(dsa_w_uk[l].transpose(0, 2, 1)),
                               bf(dsa_w_uv[l].transpose(0, 2, 1)), ki, ckv, ckv.T, topk)
        o_dsa = o_dsa.transpose(1, 2, 0, 3).reshape(DSA_W, S).T

        zw = jnp.zeros((RWKV_W_LORA, RWKV_W), F32)
        wwa = bf(jnp.concatenate([jnp.concatenate([rwkv_w_up[l], zw], axis=1),
                                  jnp.concatenate([zw, rwkv_a_up[l]], axis=1)], axis=0))
        vres = None
        mu = rwkv_mu[l]
        if l > 0:
            vup = jnp.pad(vres_up[l - 1], ((0, LANE - RWKV_V_LORA), (0, 0)))
            vres = (bf(vup), vres_v0[l - 1], v_first)
            mu = jnp.concatenate([mu, jnp.pad(vres_mu[l - 1], (0, LANE - RWKV_V_LORA))])
        *scan_ops, bonus, g_rwkv, v = _rwkv_pre(
            pr, mu, rwkv_w0[l], rwkv_a0[l], wwa, bf(rwkv_g_up[l]),
            rwkv_k_k[l], rwkv_k_a[l], rwkv_r_k[l], vres)
        if l == 0:
            v_first = v
        y_rwkv = _rwkv_scan(*scan_ops)

        xs, h2 = _merge(xs, o_fox, o_dsa, y_rwkv, bonus, g_rwkv, rwkv_ln_g[l], rwkv_ln_b[l],
                        gate_p, b_gate[l], bf(p_fox[l]), bf(p_dsa[l]), bf(p_rwkv[l]), bf(w_out[l]),
                        g_ffn[l])

        last = l == depth - 1
        if l % 2 == 0:
            xs = _ffn(xs, h2, bf(ffn_w_gate[l // 2]), bf(ffn_w_up[l // 2]), bf(ffn_w_down[l // 2]),
                      g_final, last)
        else:
            rw = bf(_pad_cols(router_w[l // 2], LANE))
            rb = _pad_cols(router_b[l // 2].reshape(1, -1), LANE)
            xs = _moe(xs, h2, rw, rb, _to_bf16(moe_w_gate[l // 2]), _to_bf16(moe_w_up[l // 2]),
                      _to_bf16(moe_w_down[l // 2]), g_final, last)
    return xs[None]
```

```python
import functools

import jax
import jax.numpy as jnp
from jax import lax
from jax.experimental import pallas as pl
from jax.experimental.pallas import tpu as pltpu

F32 = jnp.float32
BF16 = jnp.bfloat16
I32 = jnp.int32

D_MODEL = 1024
CHUNK = 64
RMS_EPS = 1e-6
FOX_HEADS, FOX_HEAD_DIM = 4, 64
DSA_HEADS, DSA_HEAD_DIM, DSA_KV_RANK = 4, 64, 128
IDX_HEADS, IDX_HEAD_DIM, IDX_TOPK = 8, 32, 256
RWKV_HEADS, RWKV_HEAD_DIM = 8, 64
RWKV_W_LORA, RWKV_A_LORA, RWKV_V_LORA, RWKV_G_LORA = 64, 64, 32, 128
RWKV_LN_EPS = 64e-5
FOX_W = FOX_HEADS * FOX_HEAD_DIM
DSA_W = DSA_HEADS * DSA_HEAD_DIM
RWKV_W = RWKV_HEADS * RWKV_HEAD_DIM
IDX_W = IDX_HEADS * IDX_HEAD_DIM
N_EXPERTS = 8
FOX_IN = 3 * FOX_W + FOX_HEADS
DSA_IN = DSA_W + DSA_KV_RANK + IDX_W + IDX_HEAD_DIM + IDX_HEADS
RWKV_IN = 3 * RWKV_W + RWKV_W_LORA + RWKV_A_LORA + RWKV_G_LORA

LANE = 128
VMEM_LIMIT = 52 * 1024 * 1024
NEG = -1e30
INT_MIN = -(2 ** 31)
LOG2E = 1.4426950408889634
F32_MIN_NORMAL = 2.0 ** -126
HI = lax.Precision.HIGHEST


def _params(*sem):
    return pltpu.CompilerParams(dimension_semantics=sem, vmem_limit_bytes=VMEM_LIMIT)


def _pick_tile(n, cap):
    best = LANE
    for t in range(LANE, min(n, cap) + 1, LANE):
        if n % t == 0:
            best = t
    return best


def _softplus(x):
    return jnp.maximum(x, 0.0) + jnp.log1p(jnp.exp(-jnp.abs(x)))


def _sigmoid(x):
    return 1.0 / (1.0 + jnp.exp(-x))


def _rms(x, g):
    return x * lax.rsqrt(jnp.mean(x * x, axis=-1, keepdims=True) + RMS_EPS) * g


def _rms_proj_kernel(x_ref, g_ref, w_ref, o_ref):
    h = _rms(x_ref[...], g_ref[...])
    o_ref[...] = jnp.dot(h.astype(BF16), w_ref[...], preferred_element_type=F32)


def _rms_proj(x, g, w):
    S, D = x.shape
    N = w.shape[1]
    tm = min(512, S)
    tn = _pick_tile(N, 2304)
    return pl.pallas_call(
        _rms_proj_kernel,
        grid=(N // tn, S // tm),
        in_specs=[pl.BlockSpec((tm, D), lambda j, i: (i, 0)),
                  pl.BlockSpec((1, D), lambda j, i: (0, 0)),
                  pl.BlockSpec((D, tn), lambda j, i: (0, j))],
        out_specs=pl.BlockSpec((tm, tn), lambda j, i: (i, j)),
        out_shape=jax.ShapeDtypeStruct((S, N), F32),
        compiler_params=_params("arbitrary", "arbitrary"),
        name="rms_proj",
    )(x, g.reshape(1, D), w)


def _rmsnorm_kernel(x_ref, g_ref, o_ref):
    o_ref[...] = _rms(x_ref[...], g_ref[...]).astype(o_ref.dtype)


def _rmsnorm(x, g, dtype):
    S, D = x.shape
    tm = min(2048, S)
    return pl.pallas_call(
        _rmsnorm_kernel,
        grid=(S // tm,),
        in_specs=[pl.BlockSpec((tm, D), lambda i: (i, 0)),
                  pl.BlockSpec((1, D), lambda i: (0, 0))],
        out_specs=pl.BlockSpec((tm, D), lambda i: (i, 0)),
        out_shape=jax.ShapeDtypeStruct((S, D), dtype),
        compiler_params=_params("arbitrary"),
        name="rmsnorm",
    )(x, g.reshape(1, D))


def _fox_cumsum_kernel(fl_ref, b_ref, hi_ref, mid_ref, lo_ref):
    H, R, _ = fl_ref.shape
    upper = (lax.broadcasted_iota(I32, (LANE, LANE), 0)
             <= lax.broadcasted_iota(I32, (LANE, LANE), 1)).astype(F32)
    strict_lower = (lax.broadcasted_iota(I32, (R, R), 0)
                    > lax.broadcasted_iota(I32, (R, R), 1)).astype(F32)
    for h in range(H):
        log_f = -_softplus(-(fl_ref[h] + b_ref[h]))
        within = jnp.dot(log_f, upper, preferred_element_type=F32, precision=HI)
        row_tot = jnp.broadcast_to(within[:, LANE - 1:LANE], (R, LANE))
        before = jnp.dot(strict_lower, row_tot, preferred_element_type=F32, precision=HI)
        hi, mid, lo = _split3((within + before) * LOG2E)
        hi_ref[h], mid_ref[h], lo_ref[h] = hi, mid, lo


def _fox_cumsum(fl, bias):
    H, R, _ = fl.shape
    return pl.pallas_call(
        _fox_cumsum_kernel,
        out_shape=[jax.ShapeDtypeStruct((H, R, LANE), BF16)] * 3,
        compiler_params=pltpu.CompilerParams(vmem_limit_bytes=VMEM_LIMIT),
        name="fox_cumsum",
    )(fl, jnp.broadcast_to(bias.reshape(H, 1, 1), (H, 1, LANE)))


FOX_KA = 128


def _fox_kernel(qa_ref, ka_ref, vT_ref, o_ref, lga_sc, lgb_sc, m_sc, l_sc, acc_sc, *, t, tc, tg):
    i = pl.program_id(1)
    hb = qa_ref.shape[0]
    last = (i * t) // tg
    chains = [(h, q0) for h in range(hb) for q0 in range(0, t, tc)]

    def logits_into(dst_ref, g):
        off = pl.multiple_of(jnp.minimum(g, last) * tg, tg)
        for n, (h, q0) in enumerate(chains):
            dst_ref[n] = jnp.dot(ka_ref[h, pl.ds(off, tg), :], qa_ref[h, :, q0:q0 + tc],
                                 preferred_element_type=F32)

    def update(lg_ref, g, masked):
        off = pl.multiple_of(g * tg, tg)
        for n, (h, q0) in enumerate(chains):
            s = lg_ref[n]
            if masked:
                kpos = off + lax.broadcasted_iota(I32, (tg, tc), 0)
                qpos = i * t + q0 + lax.broadcasted_iota(I32, (tg, tc), 1)
                s = jnp.where(kpos <= qpos, s, NEG)
            m_prev = m_sc[n]
            m_new = jnp.maximum(m_prev, jnp.max(s, axis=0, keepdims=True))
            alpha = jnp.exp2(m_prev - m_new)
            p = jnp.exp2(s - m_new)
            l_sc[n] = alpha * l_sc[n] + jnp.sum(p, axis=0, keepdims=True)
            acc_sc[n] = alpha * acc_sc[n] + jnp.dot(vT_ref[h, :, pl.ds(off, tg)], p.astype(BF16),
                                                    preferred_element_type=F32)
            m_sc[n] = m_new

    m_sc[...] = jnp.full(m_sc.shape, NEG, F32)
    l_sc[...] = jnp.zeros(l_sc.shape, F32)
    acc_sc[...] = jnp.zeros(acc_sc.shape, F32)
    logits_into(lga_sc, 0)

    def pair(j, c):
        logits_into(lgb_sc, 2 * j + 1)
        update(lga_sc, 2 * j, False)
        logits_into(lga_sc, 2 * j + 2)
        update(lgb_sc, 2 * j + 1, False)
        return c

    lax.fori_loop(0, last // 2, pair, 0)
    tail = 2 * (last // 2)
    logits_into(lgb_sc, tail + 1)
    update(lga_sc, tail, True)

    @pl.when(tail + 1 <= last)
    def _():
        update(lgb_sc, tail + 1, True)

    for n, (h, q0) in enumerate(chains):
        o_ref[h, :, q0:q0 + tc] = (acc_sc[n] / l_sc[n]).astype(o_ref.dtype)


def _fox_attention(qaT, ka, vT):
    H, KA, S = qaT.shape
    Dh = vT.shape[1]
    t = min(512, S)
    tc = min(256, t)
    tg = min(1024, S)
    hb = 2
    nc = hb * (t // tc)
    return pl.pallas_call(
        functools.partial(_fox_kernel, t=t, tc=tc, tg=tg),
        grid=(H // hb, S // t),
        in_specs=[pl.BlockSpec((hb, KA, t), lambda h, i: (h, 0, i)),
                  pl.BlockSpec((hb, S, KA), lambda h, i: (h, 0, 0)),
                  pl.BlockSpec((hb, Dh, S), lambda h, i: (h, 0, 0))],
        out_specs=pl.BlockSpec((hb, Dh, t), lambda h, i: (h, 0, i)),
        out_shape=jax.ShapeDtypeStruct((H, Dh, S), BF16),
        scratch_shapes=[pltpu.VMEM((nc, tg, tc), F32), pltpu.VMEM((nc, tg, tc), F32),
                        pltpu.VMEM((nc, 1, tc), F32), pltpu.VMEM((nc, 1, tc), F32),
                        pltpu.VMEM((nc, Dh, tc), F32)],
        compiler_params=_params("arbitrary", "arbitrary"),
        name="fox_attention",
    )(qaT, ka, vT)


DSA_T = 256


def _dsa_kernel(qiT_ref, w_ref, qT_ref, wukT_ref, wuvT_ref, ki_ref, c_ref, cT_ref, o_ref,
                keys_sc, qlat_sc, m_sc, l_sc, acc_sc, lga_sc, lgb_sc, rela_sc, relb_sc,
                *, t, tg, topk):
    i = pl.program_id(0)
    last = (i * t) // tg
    ng = last + 1

    qiT = qiT_ref[0]
    w = w_ref[0]

    def rel_into(dst_ref, g):
        off = pl.multiple_of(jnp.minimum(g, last) * tg, tg)
        dst_ref[...] = jnp.dot(ki_ref[pl.ds(off, tg), :], qiT, preferred_element_type=F32)

    def score_keys(rel_ref, g, masked):
        off = pl.multiple_of(g * tg, tg)
        sc = jnp.maximum(rel_ref[:, 0:t], 0.0) * w[0:1, :]
        for h in range(1, IDX_HEADS):
            sc = sc + jnp.maximum(rel_ref[:, h * t:(h + 1) * t], 0.0) * w[h:h + 1, :]
        bits = pltpu.bitcast(sc, I32)
        key = bits ^ ((bits >> 31) & 0x7FFFFFFF)
        kpos = off + lax.broadcasted_iota(I32, (tg, t), 0)
        key = jnp.where(jnp.abs(sc) < F32_MIN_NORMAL, -1 - kpos, key)
        if masked:
            qchunk = (i * t + lax.broadcasted_iota(I32, (tg, t), 1)) // CHUNK
            key = jnp.where(kpos // CHUNK <= qchunk, key, INT_MIN)
        keys_sc[pl.ds(off, tg), :] = key

    def fill_pair(j, c):
        rel_into(relb_sc, 2 * j + 1)
        score_keys(rela_sc, 2 * j, False)
        rel_into(rela_sc, 2 * j + 2)
        score_keys(relb_sc, 2 * j + 1, False)
        return c

    rel_into(rela_sc, 0)
    lax.fori_loop(0, last // 2, fill_pair, 0)
    tail = 2 * (last // 2)
    rel_into(relb_sc, tail + 1)
    score_keys(rela_sc, tail, True)

    @pl.when(tail + 1 <= last)
    def _():
        score_keys(relb_sc, tail + 1, True)

    def count_ge(cand):
        rows = 16 * LANE * 8 // t
        cb = jnp.broadcast_to(cand, (rows, t))

        def body(g, acc):
            off = pl.multiple_of(g * tg, tg)
            for k in range(tg // rows):
                acc = acc + (keys_sc[pl.ds(off + k * rows, rows), :] >= cb).astype(F32)
            return acc

        acc = lax.fori_loop(0, ng, body, jnp.zeros((rows, t), F32))
        return jnp.sum(acc, axis=0, keepdims=True)

    def bisect(bit, st):
        thr, n_ge = st
        cand = thr + jnp.left_shift(jnp.int32(1), bit)
        c = count_ge(cand)
        keep = c >= topk
        return jnp.where(keep, cand, thr), jnp.where(keep, c, n_ge)

    qpos = i * t + lax.broadcasted_iota(I32, (1, t), 1)
    takes_all = (qpos // CHUNK + 1) * CHUNK <= topk
    eager_bits, bits_per_check = 20, 3

    def unsettled(st):
        bit, _, n_ge = st
        done = jnp.where((n_ge == topk) | takes_all, 1.0, 0.0)
        return (bit >= 0) & (jnp.min(done) < 0.5)

    def low_bits(st):
        bit, thr, n_ge = st
        for k in range(bits_per_check):
            thr, n_ge = bisect(bit - k, (thr, n_ge))
        return bit - bits_per_check, thr, n_ge

    st = lax.fori_loop(0, eager_bits, lambda s, st: bisect(31 - s, st),
                       (jnp.full((1, t), INT_MIN, I32), jnp.zeros((1, t), F32)))
    _, thr, n_ge = lax.while_loop(unsettled, low_bits, (jnp.int32(31 - eager_bits),) + st)
    thr = jnp.maximum(thr, INT_MIN + 1)
    has_ties = jnp.max(jnp.where(n_ge > topk, 1.0, 0.0)) > 0.0
    thr_b = jnp.broadcast_to(thr, (tg, t))

    for h in range(DSA_HEADS):
        ql = jnp.dot(wukT_ref[h], qT_ref[0, h], preferred_element_type=F32)
        qlat_sc[:, h * t:(h + 1) * t] = (ql * (DSA_HEAD_DIM ** -0.5 * LOG2E)).astype(BF16)
    m_sc[...] = jnp.full(m_sc.shape, NEG, F32)
    l_sc[...] = jnp.zeros(l_sc.shape, F32)
    acc_sc[...] = jnp.zeros(acc_sc.shape, F32)

    def logits_into(dst_ref, g):
        off = pl.multiple_of(jnp.minimum(g, last) * tg, tg)
        dst_ref[...] = jnp.dot(c_ref[pl.ds(off, tg), :], qlat_sc[...],
                               preferred_element_type=F32)

    def attend(lg_ref, g, seen_eq, with_ties, need):
        valid = g <= last
        off = pl.multiple_of(jnp.minimum(g, last) * tg, tg)
        key = keys_sc[pl.ds(off, tg), :]
        if with_ties:
            eq = (key == thr_b) & valid
            eqf = eq.astype(F32)
            strict_lower = (lax.broadcasted_iota(I32, (tg, tg), 0)
                            > lax.broadcasted_iota(I32, (tg, tg), 1)).astype(BF16)
            rank = jnp.dot(strict_lower, eqf.astype(BF16), preferred_element_type=F32) + seen_eq
            sel = (key > thr_b) | (eq & (rank < need))
            seen_eq = seen_eq + jnp.sum(eqf, axis=0, keepdims=True)
        else:
            sel = key >= thr_b
        bias = jnp.where(sel & valid, 0.0, NEG)
        probs, alphas = [], []
        for h in range(DSA_HEADS):
            s = lg_ref[:, h * t:(h + 1) * t] + bias
            m_prev = m_sc[h]
            m_new = jnp.maximum(m_prev, jnp.max(s, axis=0, keepdims=True))
            alpha = jnp.exp2(m_prev - m_new)
            p = jnp.exp2(s - m_new)
            l_sc[h] = alpha * l_sc[h] + jnp.sum(p, axis=0, keepdims=True)
            m_sc[h] = m_new
            probs.append(p.astype(BF16))
            alphas.append(alpha)
        pv = jnp.dot(cT_ref[:, pl.ds(off, tg)], jnp.concatenate(probs, axis=1),
                     preferred_element_type=F32)
        for h in range(DSA_HEADS):
            acc_sc[h] = alphas[h] * acc_sc[h] + pv[:, h * t:(h + 1) * t]
        return seen_eq

    def sweep(with_ties, need=None):
        logits_into(lga_sc, 0)

        def pair(j, seen_eq):
            g0 = 2 * j
            logits_into(lgb_sc, g0 + 1)
            seen_eq = attend(lga_sc, g0, seen_eq, with_ties, need)
            logits_into(lga_sc, g0 + 2)
            return attend(lgb_sc, g0 + 1, seen_eq, with_ties, need)

        lax.fori_loop(0, last // 2 + 1, pair, jnp.zeros((1, t), F32))

    @pl.when(has_ties)
    def _():
        sweep(True, topk - count_ge(thr + 1))

    @pl.when(jnp.logical_not(has_ties))
    def _():
        sweep(False)

    for h in range(DSA_HEADS):
        o_lat = (acc_sc[h] / l_sc[h]).astype(BF16)
        o_ref[0, h] = jnp.dot(wuvT_ref[h], o_lat, preferred_element_type=F32).astype(o_ref.dtype)


def _dsa_attention(qiT, w, qT, wukT, wuvT, ki, c, cT, topk):
    NB, H, Dh, t = qT.shape
    S = NB * t
    assert S < 2 ** 23
    R = DSA_KV_RANK
    tg = min(512, S)
    const2 = lambda i: (0, 0)
    const3 = lambda i: (0, 0, 0)
    return pl.pallas_call(
        functools.partial(_dsa_kernel, t=t, tg=tg, topk=topk),
        grid=(NB,),
        in_specs=[pl.BlockSpec((1, IDX_HEAD_DIM, IDX_HEADS * t), lambda i: (i, 0, 0)),
                  pl.BlockSpec((1, IDX_HEADS, t), lambda i: (i, 0, 0)),
                  pl.BlockSpec((1, H, Dh, t), lambda i: (i, 0, 0, 0)),
                  pl.BlockSpec((H, R, Dh), const3),
                  pl.BlockSpec((H, Dh, R), const3),
                  pl.BlockSpec((S, IDX_HEAD_DIM), const2, pipeline_mode=pl.Buffered(1)),
                  pl.BlockSpec((S, R), const2, pipeline_mode=pl.Buffered(1)),
                  pl.BlockSpec((R, S), const2, pipeline_mode=pl.Buffered(1))],
        out_specs=pl.BlockSpec((1, H, Dh, t), lambda i: (i, 0, 0, 0)),
        out_shape=jax.ShapeDtypeStruct((NB, H, Dh, t), BF16),
        scratch_shapes=[pltpu.VMEM((S, t), I32), pltpu.VMEM((R, H * t), BF16),
                        pltpu.VMEM((H, 1, t), F32), pltpu.VMEM((H, 1, t), F32),
                        pltpu.VMEM((H, R, t), F32),
                        pltpu.VMEM((tg, H * t), F32), pltpu.VMEM((tg, H * t), F32),
                        pltpu.VMEM((tg, IDX_HEADS * t), F32), pltpu.VMEM((tg, IDX_HEADS * t), F32)],
        compiler_params=_params("arbitrary"),
        name="dsa_attention",
    )(qiT, w, qT, wukT, wuvT, ki, c, cT)


def _shift_mix(p, prev_tail, mu, first_block):
    rows = lax.broadcasted_iota(I32, p.shape, 0)
    tail = jnp.where(first_block, 0.0, prev_tail)
    prev = jnp.where(rows == 0, tail, pltpu.roll(p, 1, axis=0))
    return p + (prev - p) * mu


def _split3(f):
    hi = f.astype(BF16)
    r1 = f - hi.astype(F32)
    mid = r1.astype(BF16)
    lo = (r1 - mid.astype(F32)).astype(BF16)
    return hi, mid, lo


def _dot3(x, w01):
    return sum(jnp.dot(part, w01, preferred_element_type=F32) for part in _split3(x))


def _dot3_left(w01, x):
    return sum(jnp.dot(w01, part, preferred_element_type=F32) for part in _split3(x))


def _rwkv_pre_kernel(*refs, with_vres):
    (p_ref, pp_ref, mu_ref, w0_ref, a0_ref, wwa_ref, gup_ref, kk_ref, ka_ref, rk_ref,
     tri_ref, blk_ref, bd_ref) = refs[:13]
    if with_vres:
        vup_ref, vv0_ref, vf_ref = refs[13:16]
    (at_o, rt_o, bt_o, kt_o, bh_o, kh_o, vb_o, pc_o, bonus_o, g_o, v_o) = refs[-11:]
    first = pl.program_id(0) == 0
    W = RWKV_W
    ps = _shift_mix(p_ref[...], pp_ref[7:8, :], mu_ref[...], first)
    r, k, v = ps[:, 0:W], ps[:, W:2 * W], ps[:, 2 * W:3 * W]
    wa = ps[:, 3 * W:3 * W + LANE]
    gl = ps[:, 3 * W + LANE:3 * W + 2 * LANE]
    lane = lax.broadcasted_iota(I32, wa.shape, 1)
    wa = jnp.where(lane < RWKV_W_LORA, jnp.tanh(wa), wa)
    up = jnp.dot(wa.astype(BF16), wwa_ref[...], preferred_element_type=F32)
    log_w = -_softplus(-(w0_ref[...] + up[:, 0:W])) - 0.5
    a = _sigmoid(a0_ref[...] + up[:, W:2 * W])
    g = jnp.dot(_sigmoid(gl).astype(BF16), gup_ref[...], preferred_element_type=F32)
    if with_vres:
        vl = ps[:, RWKV_IN:RWKV_IN + LANE]
        logit = vv0_ref[...] + jnp.dot(vl.astype(BF16), vup_ref[...], preferred_element_type=F32)
        v = v + (vf_ref[...] - v) * _sigmoid(logit)
    k2 = k * (1.0 + (a - 1.0) * ka_ref[...])
    kkr = k * kk_ref[...]
    bd = bd_ref[...]
    kk = kkr * lax.rsqrt(_dot3(kkr * kkr, bd) + 1e-12)
    lw = -jnp.exp(log_w)
    cum = _dot3_left(tri_ref[...], lw)
    cend = _dot3_left(blk_ref[...], lw)
    p_inv = jnp.exp(-cum)
    p_end = jnp.exp(cend - cum)
    beta = kk * a
    at_o[...] = (-kk * jnp.exp(cum - lw)).astype(BF16)
    rt_o[...] = (r * jnp.exp(cum)).astype(BF16)
    bt_o[...] = (beta * p_inv).astype(BF16)
    kt_o[...] = (k2 * p_inv).astype(BF16)
    bh_o[...] = (beta * p_end).astype(BF16)
    kh_o[...] = (k2 * p_end).astype(BF16)
    vb_o[...] = v.astype(BF16)
    pc_o[...] = jnp.exp(cend)
    bonus_o[...] = _dot3(r * k2 * rk_ref[...], bd) * v
    g_o[...] = g
    v_o[...] = v


def _block_ones(n, block, lower_tri=False):
    i = jnp.arange(n)
    m = (i[:, None] // block) == (i[None, :] // block)
    if lower_tri:
        m = m & (i[:, None] >= i[None, :])
    return m.astype(BF16)


def _rwkv_pre(p, mu, w0, a0, wwa, gup, k_k, k_a, r_k, vres=None):
    S, PW = p.shape
    W = RWKV_W
    tm = min(512, S)
    row = lambda i: (i, 0)
    const = lambda i: (0, 0)
    tail = lambda i: (jnp.maximum(i * (tm // 8) - 1, 0), 0)
    vec = lambda a: a.reshape(1, -1)
    args = [p, p, vec(mu), vec(w0), vec(a0), wwa, gup, vec(k_k), vec(k_a), vec(r_k),
            _block_ones(tm, CHUNK, lower_tri=True), _block_ones(tm, CHUNK),
            _block_ones(W, RWKV_HEAD_DIM)]
    specs = [pl.BlockSpec((tm, PW), row), pl.BlockSpec((8, PW), tail),
             pl.BlockSpec((1, PW), const), pl.BlockSpec((1, W), const), pl.BlockSpec((1, W), const),
             pl.BlockSpec(wwa.shape, const), pl.BlockSpec(gup.shape, const),
             pl.BlockSpec((1, W), const), pl.BlockSpec((1, W), const), pl.BlockSpec((1, W), const),
             pl.BlockSpec((tm, tm), const), pl.BlockSpec((tm, tm), const),
             pl.BlockSpec((W, W), const)]
    if vres is not None:
        vup, vv0, v_first = vres
        args += [vup, vec(vv0), v_first]
        specs += [pl.BlockSpec(vup.shape, const), pl.BlockSpec((1, W), const),
                  pl.BlockSpec((tm, W), row)]
    return pl.pallas_call(
        functools.partial(_rwkv_pre_kernel, with_vres=vres is not None),
        grid=(S // tm,),
        in_specs=specs,
        out_specs=[pl.BlockSpec((tm, W), row)] * 11,
        out_shape=[jax.ShapeDtypeStruct((S, W), BF16)] * 7 + [jax.ShapeDtypeStruct((S, W), F32)] * 4,
        compiler_params=_params("arbitrary"),
        name="rwkv_pre",
    )(*args)


def _rwkv_scan_kernel(at_ref, rt_ref, bt_ref, kt_ref, bh_ref, kh_ref, v_ref, pc_ref, y_ref, h_sc,
                      *, chunks):
    C = CHUNK
    N = RWKV_HEAD_DIM

    @pl.when(pl.program_id(0) == 0)
    def _():
        h_sc[...] = jnp.zeros(h_sc.shape, F32)

    ti = lax.broadcasted_iota(I32, (C, C), 0)
    tj = lax.broadcasted_iota(I32, (C, C), 1)
    lower_incl = ti >= tj
    lower_strict = ti > tj
    eye = (ti == tj).astype(F32)
    eye_n = (lax.broadcasted_iota(I32, (N, N), 0) == lax.broadcasted_iota(I32, (N, N), 1)).astype(F32)

    def mm(x, y):
        return jnp.dot(x.astype(BF16), y.astype(BF16), preferred_element_type=F32)

    def mm_nt(x, y):
        return lax.dot_general(x.astype(BF16), y.astype(BF16), (((1,), (1,)), ((), ())),
                               preferred_element_type=F32)

    def mm_tn(x, y):
        return lax.dot_general(x.astype(BF16), y.astype(BF16), (((0,), (0,)), ((), ())),
                               preferred_element_type=F32)

    units = [(c, h) for c in range(chunks) for h in range(RWKV_HEADS)]
    tile = lambda ref, u: ref[u[0] * C:(u[0] + 1) * C, u[1] * N:(u[1] + 1) * N]
    each = lambda fn: {u: fn(u) for u in units}

    At, Rt, Bt, Kt = (each(lambda u, r=ref: tile(r, u)) for ref in (at_ref, rt_ref, bt_ref, kt_ref))
    Bh, Kh, V = (each(lambda u, r=ref: tile(r, u)) for ref in (bh_ref, kh_ref, v_ref))
    AR = each(lambda u: jnp.concatenate([At[u], Rt[u]], axis=0))
    Mb = each(lambda u: mm_nt(AR[u], Bt[u]))
    Mk = each(lambda u: mm_nt(AR[u], Kt[u]))
    Lab = each(lambda u: jnp.where(lower_strict, Mb[u][0:C], 0.0))
    Mrb = each(lambda u: jnp.where(lower_incl, Mb[u][C:2 * C], 0.0))
    Lak = each(lambda u: jnp.where(lower_strict, Mk[u][0:C], 0.0))
    Mrk = each(lambda u: jnp.where(lower_incl, Mk[u][C:2 * C], 0.0))
    T = each(lambda u: eye + Lab[u])
    Lp = Lab
    span = 2
    while span < C:
        Lp = each(lambda u, Lp=Lp: mm(Lp[u], Lp[u]))
        T = each(lambda u, T=T, Lp=Lp: T[u] + mm(Lp[u], T[u]))
        span *= 2
    W1 = each(lambda u: mm(Lak[u], V[u]))
    A2 = each(lambda u: mm(T[u], At[u]))
    U0 = each(lambda u: mm(T[u], W1[u]))
    R2 = each(lambda u: Rt[u].astype(F32) + mm(Mrb[u], A2[u]))
    Y0 = each(lambda u: mm(Mrb[u], U0[u]) + mm(Mrk[u], V[u]))
    G = each(lambda u: eye_n * tile(pc_ref, u)[0:1, :] + mm_tn(Bh[u], A2[u]))
    H0 = each(lambda u: mm_tn(Bh[u], U0[u]) + mm_tn(Kh[u], V[u]))

    H = {h: h_sc[h] for h in range(RWKV_HEADS)}
    for c in range(chunks):
        ys = []
        for h in range(RWKV_HEADS):
            u = (c, h)
            ys.append(mm(R2[u], H[h]) + Y0[u])
            H[h] = mm(G[u], H[h]) + H0[u]
        y_ref[c * C:(c + 1) * C, :] = jnp.concatenate(ys, axis=1)
    for h in range(RWKV_HEADS):
        h_sc[h] = H[h]


def _rwkv_scan(at, rt, bt, kt, bh, kh, vb, pc):
    S, W = at.shape
    chunks = 2 if S % (2 * CHUNK) == 0 else 1
    tb = chunks * CHUNK
    seq = pl.BlockSpec((tb, W), lambda i: (i, 0))
    return pl.pallas_call(
        functools.partial(_rwkv_scan_kernel, chunks=chunks),
        grid=(S // tb,),
        in_specs=[seq] * 8,
        out_specs=seq,
        out_shape=jax.ShapeDtypeStruct((S, W), F32),
        scratch_shapes=[pltpu.VMEM((RWKV_HEADS, RWKV_HEAD_DIM, RWKV_HEAD_DIM), F32)],
        compiler_params=_params("arbitrary"),
        name="rwkv_scan",
    )(at, rt, bt, kt, bh, kh, vb, pc)


def _merge_kernel(x_ref, of_ref, od_ref, y_ref, bonus_ref, g_ref, lng_ref, lnb_ref, bd_ref,
                  gp_ref, bg_ref, pf_ref, pd_ref, pr_ref, wo_ref, gn_ref, xo_ref, h_ref):
    D = D_MODEL
    bd = bd_ref[...]
    inv_n = 1.0 / RWKV_HEAD_DIM
    y = y_ref[...]
    yc = y - _dot3(y, bd) * inv_n
    var = _dot3(yc * yc, bd) * inv_n
    yn = yc * lax.rsqrt(var + RWKV_LN_EPS) * lng_ref[...] + lnb_ref[...]
    o_rwkv = ((yn + bonus_ref[...]) * g_ref[...]).astype(BF16)
    gates = _sigmoid(gp_ref[...] + bg_ref[...])
    merged = (gates[:, 0:D] * jnp.dot(of_ref[...], pf_ref[...], preferred_element_type=F32)
              + gates[:, D:2 * D] * jnp.dot(od_ref[...], pd_ref[...], preferred_element_type=F32)
              + gates[:, 2 * D:3 * D] * jnp.dot(o_rwkv, pr_ref[...], preferred_element_type=F32))
    x = x_ref[...] + jnp.dot(merged.astype(BF16), wo_ref[...], preferred_element_type=F32)
    xo_ref[...] = x
    h_ref[...] = _rms(x, gn_ref[...]).astype(h_ref.dtype)


def _merge(x, o_fox, o_dsa, y_rwkv, bonus, g_rwkv, ln_g, ln_b, gate_p, b_gate, p_fox, p_dsa, p_rwkv,
           w_out, g_ffn):
    S, D = x.shape
    W = RWKV_W
    tm = min(512, S)
    row = lambda i: (i, 0)
    const = lambda i: (0, 0)
    return pl.pallas_call(
        _merge_kernel,
        grid=(S // tm,),
        in_specs=[pl.BlockSpec((tm, D), row), pl.BlockSpec((tm, FOX_W), row),
                  pl.BlockSpec((tm, DSA_W), row), pl.BlockSpec((tm, W), row),
                  pl.BlockSpec((tm, W), row), pl.BlockSpec((tm, W), row),
                  pl.BlockSpec((1, W), const), pl.BlockSpec((1, W), const),
                  pl.BlockSpec((W, W), const),
                  pl.BlockSpec((tm, 3 * D), row), pl.BlockSpec((1, 3 * D), const),
                  pl.BlockSpec((FOX_W, D), const), pl.BlockSpec((DSA_W, D), const),
                  pl.BlockSpec((W, D), const), pl.BlockSpec((D, D), const),
                  pl.BlockSpec((1, D), const)],
        out_specs=[pl.BlockSpec((tm, D), row), pl.BlockSpec((tm, D), row)],
        out_shape=[jax.ShapeDtypeStruct((S, D), F32), jax.ShapeDtypeStruct((S, D), BF16)],
        compiler_params=_params("arbitrary"),
        name="merge",
    )(x, o_fox, o_dsa, y_rwkv, bonus, g_rwkv, ln_g.reshape(1, W), ln_b.reshape(1, W),
      _block_ones(W, RWKV_HEAD_DIM), gate_p, b_gate.reshape(1, -1), p_fox, p_dsa, p_rwkv, w_out,
      g_ffn.reshape(1, D))


def _swiglu_tile(h, wg, wu):
    gate = jnp.dot(h, wg, preferred_element_type=F32)
    up = jnp.dot(h, wu, preferred_element_type=F32)
    return gate * _sigmoid(gate) * up


def _ffn_kernel(x_ref, h_ref, wg_ref, wu_ref, wd_ref, gf_ref, o_ref, acc_sc, *, final_norm):
    f = pl.program_id(1)

    @pl.when(f == 0)
    def _():
        acc_sc[...] = x_ref[...]

    act = _swiglu_tile(h_ref[...], wg_ref[...], wu_ref[...])
    acc_sc[...] += jnp.dot(act.astype(BF16), wd_ref[...], preferred_element_type=F32)

    @pl.when(f == pl.num_programs(1) - 1)
    def _():
        y = acc_sc[...]
        o_ref[...] = _rms(y, gf_ref[...]) if final_norm else y


def _ffn(x, h, wg, wu, wd, g_final, final_norm):
    S, D = x.shape
    Fd = wg.shape[1]
    tm = min(512, S)
    tf = _pick_tile(Fd, 1408)
    return pl.pallas_call(
        functools.partial(_ffn_kernel, final_norm=final_norm),
        grid=(S // tm, Fd // tf),
        in_specs=[pl.BlockSpec((tm, D), lambda i, f: (i, 0)),
                  pl.BlockSpec((tm, D), lambda i, f: (i, 0)),
                  pl.BlockSpec((D, tf), lambda i, f: (0, f)),
                  pl.BlockSpec((D, tf), lambda i, f: (0, f)),
                  pl.BlockSpec((tf, D), lambda i, f: (f, 0)),
                  pl.BlockSpec((1, D), lambda i, f: (0, 0))],
        out_specs=pl.BlockSpec((tm, D), lambda i, f: (i, 0)),
        out_shape=jax.ShapeDtypeStruct((S, D), F32),
        scratch_shapes=[pltpu.VMEM((tm, D), F32)],
        compiler_params=_params("arbitrary", "arbitrary"),
        name="ffn",
    )(x, h, wg, wu, wd, g_final.reshape(1, D))


MOE_CHUNK = 128


def _moe_kernel(x_ref, h_ref, rw_ref, rb_ref, wg_ref, wu_ref, wd_ref, gf_ref, o_ref,
                acc_sc, gate_sc, rank_sc, cnt_sc, hs_sc, ys_sc, *, final_norm):
    e = pl.program_id(1)
    f = pl.program_id(2)
    last_f = pl.num_programs(2) - 1
    T, D = h_ref.shape
    C = MOE_CHUNK
    lane = lax.broadcasted_iota(I32, (T, LANE), 1)

    @pl.when((e == 0) & (f == 0))
    def _():
        acc_sc[...] = x_ref[...]
        logits = jnp.dot(h_ref[...], rw_ref[...], preferred_element_type=F32) + rb_ref[...]
        logits = jnp.where(lane < N_EXPERTS, logits, -jnp.inf)
        v1 = jnp.max(logits, axis=-1, keepdims=True)
        i1 = jnp.min(jnp.where(logits == v1, lane, LANE), axis=-1, keepdims=True)
        rest = jnp.where(lane == i1, -jnp.inf, logits)
        v2 = jnp.max(rest, axis=-1, keepdims=True)
        i2 = jnp.min(jnp.where(rest == v2, lane, LANE), axis=-1, keepdims=True)
        e2 = jnp.exp(v2 - v1)
        p1 = 1.0 / (1.0 + e2)
        gate_sc[...] = jnp.where(lane == i1, p1, 0.0) + jnp.where(lane == i2, e2 * p1, 0.0)
        routed = ((lane == i1) | (lane == i2)).astype(F32)
        B = 256
        earlier = (lax.broadcasted_iota(I32, (B, B), 0) > lax.broadcasted_iota(I32, (B, B), 1)).astype(BF16)
        seen = jnp.zeros((1, LANE), F32)
        for b in range(T // B):
            blk = routed[b * B:(b + 1) * B]
            before = jnp.dot(earlier, blk.astype(BF16), preferred_element_type=F32) + seen
            rank_sc[b * B:(b + 1) * B, :] = jnp.where(blk > 0.0, before, -1.0)
            seen = seen + jnp.sum(blk, axis=0, keepdims=True)
        cnt_sc[...] = seen

    n_e = jnp.sum(jnp.where(lax.broadcasted_iota(I32, (1, LANE), 1) == e, cnt_sc[...], 0.0))
    n_chunks = sum(jnp.where(n_e > j * C, 1, 0) for j in range(T // C))

    @pl.when(f == 0)
    def _():
        pick8 = (lax.broadcasted_iota(I32, (8, LANE), 1) == e).astype(BF16)
        rk = rank_sc[...]
        rk_hi = rk.astype(BF16)
        rk_lo = (rk - rk_hi.astype(F32)).astype(BF16)
        nt = (((1,), (1,)), ((), ()))
        rrow = (lax.dot_general(pick8, rk_hi, nt, preferred_element_type=F32)
                + lax.dot_general(pick8, rk_lo, nt, preferred_element_type=F32))[0:1, :]

        def gather(c, carry):
            off = pl.multiple_of(c * C, C)
            slot = (off + lax.broadcasted_iota(I32, (C, T), 0)).astype(F32)
            onehot = (rrow == slot).astype(BF16)
            hs_sc[pl.ds(off, C), :] = jnp.dot(onehot, h_ref[...],
                                              preferred_element_type=F32).astype(BF16)
            ys_sc[pl.ds(off, C), :] = jnp.zeros((C, D), F32)
            return carry

        lax.fori_loop(0, n_chunks, gather, 0)

    def expert(c, carry):
        off = pl.multiple_of(c * C, C)
        act = _swiglu_tile(hs_sc[pl.ds(off, C), :], wg_ref[0], wu_ref[0])
        ys_sc[pl.ds(off, C), :] += jnp.dot(act.astype(BF16), wd_ref[0], preferred_element_type=F32)
        return carry

    lax.fori_loop(0, n_chunks, expert, 0)

    @pl.when(f == last_f)
    def _():
        mine = lane == e
        gate = jnp.sum(jnp.where(mine, gate_sc[...], 0.0), axis=-1, keepdims=True)
        rank = jnp.sum(jnp.where(mine, rank_sc[...], 0.0), axis=-1, keepdims=True)
        rank_b = jnp.broadcast_to(rank, (T, C))
        col = lax.broadcasted_iota(I32, (T, C), 1)

        def scatter(c, carry):
            off = pl.multiple_of(c * C, C)
            onehot_t = (rank_b == (col + off).astype(F32)).astype(BF16)
            y = jnp.dot(onehot_t, ys_sc[pl.ds(off, C), :].astype(BF16), preferred_element_type=F32)
            acc_sc[...] += gate * y
            return carry

        lax.fori_loop(0, n_chunks, scatter, 0)

    @pl.when((e == pl.num_programs(1) - 1) & (f == last_f))
    def _():
        y = acc_sc[...]
        o_ref[...] = _rms(y, gf_ref[...]) if final_norm else y


def _moe(x, h, rw, rb, wg, wu, wd, g_final, final_norm):
    S, D = x.shape
    E, _, Fe = wg.shape
    tm = min(1024, S)
    tf = _pick_tile(Fe, 896)
    return pl.pallas_call(
        functools.partial(_moe_kernel, final_norm=final_norm),
        grid=(S // tm, E, Fe // tf),
        in_specs=[pl.BlockSpec((tm, D), lambda i, e, f: (i, 0)),
                  pl.BlockSpec((tm, D), lambda i, e, f: (i, 0)),
                  pl.BlockSpec((D, LANE), lambda i, e, f: (0, 0)),
                  pl.BlockSpec((1, LANE), lambda i, e, f: (0, 0)),
                  pl.BlockSpec((1, D, tf), lambda i, e, f: (e, 0, f)),
                  pl.BlockSpec((1, D, tf), lambda i, e, f: (e, 0, f)),
                  pl.BlockSpec((1, tf, D), lambda i, e, f: (e, f, 0)),
                  pl.BlockSpec((1, D), lambda i, e, f: (0, 0))],
        out_specs=pl.BlockSpec((tm, D), lambda i, e, f: (i, 0)),
        out_shape=jax.ShapeDtypeStruct((S, D), F32),
        scratch_shapes=[pltpu.VMEM((tm, D), F32), pltpu.VMEM((tm, LANE), F32),
                        pltpu.VMEM((tm, LANE), F32), pltpu.VMEM((1, LANE), F32),
                        pltpu.VMEM((tm, D), BF16), pltpu.VMEM((tm, D), F32)],
        compiler_params=_params("arbitrary", "arbitrary", "arbitrary"),
        name="moe",
    )(x, h, rw, rb, wg, wu, wd, g_final.reshape(1, D))


def _cast_kernel(x_ref, o_ref):
    o_ref[...] = x_ref[...].astype(o_ref.dtype)


def _to_bf16(w):
    E, A, B = w.shape
    ta = 256
    return pl.pallas_call(
        _cast_kernel,
        grid=(E, A // ta),
        in_specs=[pl.BlockSpec((1, ta, B), lambda e, a: (e, a, 0))],
        out_specs=pl.BlockSpec((1, ta, B), lambda e, a: (e, a, 0)),
        out_shape=jax.ShapeDtypeStruct(w.shape, BF16),
        compiler_params=_params("arbitrary", "arbitrary"),
        name="to_bf16",
    )(w)


def _pad_cols(a, n):
    return jnp.pad(a, ((0, 0), (0, n - a.shape[1])))


def kernel(x, w_in, b_gate, g_mix, fox_f_bias, dsa_kv_norm, dsa_w_uk, dsa_w_uv, rwkv_mu, rwkv_w0, rwkv_w_up, rwkv_a0, rwkv_a_up, rwkv_g_up, rwkv_k_k, rwkv_k_a, rwkv_r_k, rwkv_ln_g, rwkv_ln_b, vres_down, vres_mu, vres_up, vres_v0, p_fox, p_dsa, p_rwkv, w_out, g_ffn, ffn_w_gate, ffn_w_up, ffn_w_down, router_w, router_b, moe_w_gate, moe_w_up, moe_w_down, g_final):
    B, S, D = x.shape
    assert B == 1 and D == D_MODEL and S % LANE == 0
    depth = w_in.shape[0]
    topk = min(IDX_TOPK, S // 4)
    bf = lambda a: a.astype(BF16)
    xs = x[0]
    v_first = None
    w_in_bf = _to_bf16(w_in)
    for l in range(depth):
        wl = w_in_bf[l]
        w_fox, w_dsa, w_rwkv, w_gate = (wl[:, :FOX_IN], wl[:, FOX_IN:FOX_IN + DSA_IN],
                                        wl[:, FOX_IN + DSA_IN:FOX_IN + DSA_IN + RWKV_IN],
                                        wl[:, FOX_IN + DSA_IN + RWKV_IN:])
        o1 = DSA_W + DSA_KV_RANK + IDX_W
        w_att = jnp.concatenate([_pad_cols(w_fox, 3 * FOX_W + LANE), w_dsa[:, :o1],
                                 _pad_cols(w_dsa[:, o1:], LANE)], axis=1)
        if l > 0:
            w_rwkv = jnp.concatenate([w_rwkv, bf(_pad_cols(vres_down[l - 1], LANE))], axis=1)
        pa = _rms_proj(xs, g_mix[l], w_att)
        pr = _rms_proj(xs, g_mix[l], w_rwkv)
        gate_p = _rms_proj(xs, g_mix[l], w_gate)

        c0 = 3 * FOX_W
        fl = pa[:, c0:c0 + FOX_HEADS].T.reshape(FOX_HEADS, S // LANE, LANE)
        Fh, Fm, Fl = (a.reshape(FOX_HEADS, S) for a in _fox_cumsum(fl, fox_f_bias[l]))
        qT = bf(pa[:, 0:FOX_W].T.reshape(FOX_HEADS, FOX_HEAD_DIM, S) * (FOX_HEAD_DIM ** -0.5 * LOG2E))
        r = jnp.arange(FOX_KA - FOX_HEAD_DIM)
        rq, ck = r[None, :, None], r[None, None, :]
        pick = lambda idx, a, b, c, d: jnp.where(idx == a[0], a[1], jnp.where(
            idx == b[0], b[1], jnp.where(idx == c[0], c[1], jnp.where(idx < 6, d, 0)))).astype(BF16)
        fq = pick(rq, (0, Fh[:, None, :]), (1, Fm[:, None, :]), (2, Fl[:, None, :]), 1)
        fk = pick(ck, (3, -Fh[:, :, None]), (4, -Fm[:, :, None]), (5, -Fl[:, :, None]), 1)
        qaT = jnp.concatenate([qT, fq], axis=1)
        k4 = bf(pa[:, FOX_W:2 * FOX_W].reshape(S, FOX_HEADS, FOX_HEAD_DIM).transpose(1, 0, 2))
        ka = jnp.concatenate([k4, fk], axis=2)
        fvT = bf(pa[:, 2 * FOX_W:3 * FOX_W].T.reshape(FOX_HEADS, FOX_HEAD_DIM, S))
        o_fox = _fox_attention(qaT, ka, fvT).reshape(FOX_W, S).T

        t = min(DSA_T, S)
        nb = S // t
        c1 = c0 + LANE
        dqT = bf(pa[:, c1:c1 + DSA_W].T.reshape(DSA_HEADS, DSA_HEAD_DIM, nb, t).transpose(2, 0, 1, 3))
        c2 = c1 + DSA_W
        ckv = _rmsnorm(pa[:, c2:c2 + DSA_KV_RANK], dsa_kv_norm[l], BF16)
        c3 = c2 + DSA_KV_RANK
        qiT = bf(pa[:, c3:c3 + IDX_W].T.reshape(IDX_HEADS, IDX_HEAD_DIM, nb, t).transpose(2, 1, 0, 3)
                 .reshape(nb, IDX_HEAD_DIM, IDX_HEADS * t))
        c4 = c3 + IDX_W
        ki = bf(pa[:, c4:c4 + IDX_HEAD_DIM])
        c5 = c4 + IDX_HEAD_DIM
        wi = pa[:, c5:c5 + IDX_HEADS].T.reshape(IDX_HEADS, nb, t).transpose(1, 0, 2) * (IDX_W ** -0.5)
        o_dsa = _dsa_attention(qiT, wi, dqT, bf(dsa_w_uk[l].transpose(0, 2, 1)),
                               bf(dsa_w_uv[l].transpose(0, 2, 1)), ki, ckv, ckv.T, topk)
        o_dsa = o_dsa.transpose(1, 2, 0, 3).reshape(DSA_W, S).T

        zw = jnp.zeros((RWKV_W_LORA, RWKV_W), F32)
        wwa = bf(jnp.concatenate([jnp.concatenate([rwkv_w_up[l], zw], axis=1),
                                  jnp.concatenate([zw, rwkv_a_up[l]], axis=1)], axis=0))
        vres = None
        mu = rwkv_mu[l]
        if l > 0:
            vup = jnp.pad(vres_up[l - 1], ((0, LANE - RWKV_V_LORA), (0, 0)))
            vres = (bf(vup), vres_v0[l - 1], v_first)
            mu = jnp.concatenate([mu, jnp.pad(vres_mu[l - 1], (0, LANE - RWKV_V_LORA))])
        *scan_ops, bonus, g_rwkv, v = _rwkv_pre(
            pr, mu, rwkv_w0[l], rwkv_a0[l], wwa, bf(rwkv_g_up[l]),
            rwkv_k_k[l], rwkv_k_a[l], rwkv_r_k[l], vres)
        if l == 0:
            v_first = v
        y_rwkv = _rwkv_scan(*scan_ops)

        xs, h2 = _merge(xs, o_fox, o_dsa, y_rwkv, bonus, g_rwkv, rwkv_ln_g[l], rwkv_ln_b[l],
                        gate_p, b_gate[l], bf(p_fox[l]), bf(p_dsa[l]), bf(p_rwkv[l]), bf(w_out[l]),
                        g_ffn[l])

        last = l == depth - 1
        if l % 2 == 0:
            xs = _ffn(xs, h2, bf(ffn_w_gate[l // 2]), bf(ffn_w_up[l // 2]), bf(ffn_w_down[l // 2]),
                      g_final, last)
        else:
            rw = bf(_pad_cols(router_w[l // 2], LANE))
            rb = _pad_cols(router_b[l // 2].reshape(1, -1), LANE)
            xs = _moe(xs, h2, rw, rb, _to_bf16(moe_w_gate[l // 2]), _to_bf16(moe_w_up[l // 2]),
                      _to_bf16(moe_w_down[l // 2]), g_final, last)
    return xs[None]
```

```python
import functools

import jax
import jax.numpy as jnp
from jax import lax
from jax.experimental import pallas as pl
from jax.experimental.pallas import tpu as pltpu

F32 = jnp.float32
BF16 = jnp.bfloat16
I32 = jnp.int32

D_MODEL = 1024
CHUNK = 64
RMS_EPS = 1e-6
FOX_HEADS, FOX_HEAD_DIM = 4, 64
DSA_HEADS, DSA_HEAD_DIM, DSA_KV_RANK = 4, 64, 128
IDX_HEADS, IDX_HEAD_DIM, IDX_TOPK = 8, 32, 256
RWKV_HEADS, RWKV_HEAD_DIM = 8, 64
RWKV_W_LORA, RWKV_A_LORA, RWKV_V_LORA, RWKV_G_LORA = 64, 64, 32, 128
RWKV_LN_EPS = 64e-5
FOX_W = FOX_HEADS * FOX_HEAD_DIM
DSA_W = DSA_HEADS * DSA_HEAD_DIM
RWKV_W = RWKV_HEADS * RWKV_HEAD_DIM
IDX_W = IDX_HEADS * IDX_HEAD_DIM
N_EXPERTS = 8
FOX_IN = 3 * FOX_W + FOX_HEADS
DSA_IN = DSA_W + DSA_KV_RANK + IDX_W + IDX_HEAD_DIM + IDX_HEADS
RWKV_IN = 3 * RWKV_W + RWKV_W_LORA + RWKV_A_LORA + RWKV_G_LORA

LANE = 128
VMEM_LIMIT = 52 * 1024 * 1024
NEG = -1e30
INT_MIN = -(2 ** 31)
INT_MAX = 2 ** 31 - 1
LOG2E = 1.4426950408889634
F32_MIN_NORMAL = 2.0 ** -126
HI = lax.Precision.HIGHEST


def _params(*sem):
    return pltpu.CompilerParams(dimension_semantics=sem, vmem_limit_bytes=VMEM_LIMIT)


def _pick_tile(n, cap):
    best = LANE
    for t in range(LANE, min(n, cap) + 1, LANE):
        if n % t == 0:
            best = t
    return best


def _softplus(x):
    return jnp.maximum(x, 0.0) + jnp.log1p(jnp.exp(-jnp.abs(x)))


def _sigmoid(x):
    return 1.0 / (1.0 + jnp.exp(-x))


def _rms(x, g):
    return x * lax.rsqrt(jnp.mean(x * x, axis=-1, keepdims=True) + RMS_EPS) * g


def _rms_proj_kernel(x_ref, g_ref, w_ref, o_ref):
    h = _rms(x_ref[...], g_ref[...])
    o_ref[...] = jnp.dot(h.astype(BF16), w_ref[...], preferred_element_type=F32)


def _rms_proj(x, g, w):
    S, D = x.shape
    N = w.shape[1]
    tm = min(512, S)
    tn = _pick_tile(N, 2304)
    return pl.pallas_call(
        _rms_proj_kernel,
        grid=(N // tn, S // tm),
        in_specs=[pl.BlockSpec((tm, D), lambda j, i: (i, 0)),
                  pl.BlockSpec((1, D), lambda j, i: (0, 0)),
                  pl.BlockSpec((D, tn), lambda j, i: (0, j))],
        out_specs=pl.BlockSpec((tm, tn), lambda j, i: (i, j)),
        out_shape=jax.ShapeDtypeStruct((S, N), F32),
        compiler_params=_params("arbitrary", "arbitrary"),
        name="rms_proj",
    )(x, g.reshape(1, D), w)


def _rmsnorm_kernel(x_ref, g_ref, o_ref):
    o_ref[...] = _rms(x_ref[...], g_ref[...]).astype(o_ref.dtype)


def _rmsnorm(x, g, dtype):
    S, D = x.shape
    tm = min(2048, S)
    return pl.pallas_call(
        _rmsnorm_kernel,
        grid=(S // tm,),
        in_specs=[pl.BlockSpec((tm, D), lambda i: (i, 0)),
                  pl.BlockSpec((1, D), lambda i: (0, 0))],
        out_specs=pl.BlockSpec((tm, D), lambda i: (i, 0)),
        out_shape=jax.ShapeDtypeStruct((S, D), dtype),
        compiler_params=_params("arbitrary"),
        name="rmsnorm",
    )(x, g.reshape(1, D))


def _fox_cumsum_kernel(fl_ref, b_ref, hi_ref, mid_ref, lo_ref):
    H, R, _ = fl_ref.shape
    upper = (lax.broadcasted_iota(I32, (LANE, LANE), 0)
             <= lax.broadcasted_iota(I32, (LANE, LANE), 1)).astype(F32)
    strict_lower = (lax.broadcasted_iota(I32, (R, R), 0)
                    > lax.broadcasted_iota(I32, (R, R), 1)).astype(F32)
    for h in range(H):
        log_f = -_softplus(-(fl_ref[h] + b_ref[h]))
        within = jnp.dot(log_f, upper, preferred_element_type=F32, precision=HI)
        row_tot = jnp.broadcast_to(within[:, LANE - 1:LANE], (R, LANE))
        before = jnp.dot(strict_lower, row_tot, preferred_element_type=F32, precision=HI)
        hi, mid, lo = _split3((within + before) * LOG2E)
        hi_ref[h], mid_ref[h], lo_ref[h] = hi, mid, lo


def _fox_cumsum(fl, bias):
    H, R, _ = fl.shape
    return pl.pallas_call(
        _fox_cumsum_kernel,
        out_shape=[jax.ShapeDtypeStruct((H, R, LANE), BF16)] * 3,
        compiler_params=pltpu.CompilerParams(vmem_limit_bytes=VMEM_LIMIT),
        name="fox_cumsum",
    )(fl, jnp.broadcast_to(bias.reshape(H, 1, 1), (H, 1, LANE)))


FOX_KA = 128


def _fox_kernel(qa_ref, ka_ref, vT_ref, o_ref, lga_sc, lgb_sc, m_sc, l_sc, acc_sc, *, t, tc, tg):
    i = pl.program_id(1)
    hb = qa_ref.shape[0]
    last = (i * t) // tg
    chains = [(h, q0) for h in range(hb) for q0 in range(0, t, tc)]

    def logits_into(dst_ref, g):
        off = pl.multiple_of(jnp.minimum(g, last) * tg, tg)
        for n, (h, q0) in enumerate(chains):
            dst_ref[n] = jnp.dot(ka_ref[h, pl.ds(off, tg), :], qa_ref[h, :, q0:q0 + tc],
                                 preferred_element_type=F32)

    def update(lg_ref, g, masked):
        off = pl.multiple_of(g * tg, tg)
        for n, (h, q0) in enumerate(chains):
            s = lg_ref[n]
            if masked:
                kpos = off + lax.broadcasted_iota(I32, (tg, tc), 0)
                qpos = i * t + q0 + lax.broadcasted_iota(I32, (tg, tc), 1)
                s = jnp.where(kpos <= qpos, s, NEG)
            m_prev = m_sc[n]
            m_new = jnp.maximum(m_prev, jnp.max(s, axis=0, keepdims=True))
            alpha = jnp.exp2(m_prev - m_new)
            p = jnp.exp2(s - m_new)
            l_sc[n] = alpha * l_sc[n] + jnp.sum(p, axis=0, keepdims=True)
            acc_sc[n] = alpha * acc_sc[n] + jnp.dot(vT_ref[h, :, pl.ds(off, tg)], p.astype(BF16),
                                                    preferred_element_type=F32)
            m_sc[n] = m_new

    m_sc[...] = jnp.full(m_sc.shape, NEG, F32)
    l_sc[...] = jnp.zeros(l_sc.shape, F32)
    acc_sc[...] = jnp.zeros(acc_sc.shape, F32)
    logits_into(lga_sc, 0)

    def pair(j, c):
        logits_into(lgb_sc, 2 * j + 1)
        update(lga_sc, 2 * j, False)
        logits_into(lga_sc, 2 * j + 2)
        update(lgb_sc, 2 * j + 1, False)
        return c

    lax.fori_loop(0, last // 2, pair, 0)
    tail = 2 * (last // 2)
    logits_into(lgb_sc, tail + 1)
    update(lga_sc, tail, True)

    @pl.when(tail + 1 <= last)
    def _():
        update(lgb_sc, tail + 1, True)

    for n, (h, q0) in enumerate(chains):
        o_ref[h, :, q0:q0 + tc] = (acc_sc[n] / l_sc[n]).astype(o_ref.dtype)


def _fox_attention(qaT, ka, vT):
    H, KA, S = qaT.shape
    Dh = vT.shape[1]
    t = min(512, S)
    tc = min(256, t)
    tg = min(1024, S)
    hb = 2
    nc = hb * (t // tc)
    return pl.pallas_call(
        functools.partial(_fox_kernel, t=t, tc=tc, tg=tg),
        grid=(H // hb, S // t),
        in_specs=[pl.BlockSpec((hb, KA, t), lambda h, i: (h, 0, i)),
                  pl.BlockSpec((hb, S, KA), lambda h, i: (h, 0, 0)),
                  pl.BlockSpec((hb, Dh, S), lambda h, i: (h, 0, 0))],
        out_specs=pl.BlockSpec((hb, Dh, t), lambda h, i: (h, 0, i)),
        out_shape=jax.ShapeDtypeStruct((H, Dh, S), BF16),
        scratch_shapes=[pltpu.VMEM((nc, tg, tc), F32), pltpu.VMEM((nc, tg, tc), F32),
                        pltpu.VMEM((nc, 1, tc), F32), pltpu.VMEM((nc, 1, tc), F32),
                        pltpu.VMEM((nc, Dh, tc), F32)],
        compiler_params=_params("arbitrary", "arbitrary"),
        name="fox_attention",
    )(qaT, ka, vT)


DSA_T = 256


def _dsa_kernel(qiT_ref, w_ref, qT_ref, wukT_ref, wuvT_ref, ki_ref, c_ref, cT_ref, o_ref,
                keys_sc, qlat_sc, m_sc, l_sc, acc_sc, lga_sc, lgb_sc, rela_sc, relb_sc,
                *, t, tg, topk):
    i = pl.program_id(0)
    last = (i * t) // tg
    ng = last + 1

    qiT = qiT_ref[0]
    w = w_ref[0]

    def rel_into(dst_ref, g):
        off = pl.multiple_of(jnp.minimum(g, last) * tg, tg)
        dst_ref[...] = jnp.dot(ki_ref[pl.ds(off, tg), :], qiT, preferred_element_type=F32)

    def score_keys(rel_ref, g, masked):
        off = pl.multiple_of(g * tg, tg)
        sc = jnp.maximum(rel_ref[:, 0:t], 0.0) * w[0:1, :]
        for h in range(1, IDX_HEADS):
            sc = sc + jnp.maximum(rel_ref[:, h * t:(h + 1) * t], 0.0) * w[h:h + 1, :]
        bits = pltpu.bitcast(sc, I32)
        key = bits ^ ((bits >> 31) & 0x7FFFFFFF)
        kpos = off + lax.broadcasted_iota(I32, (tg, t), 0)
        key = jnp.where(jnp.abs(sc) < F32_MIN_NORMAL, -1 - kpos, key)
        if masked:
            qchunk = (i * t + lax.broadcasted_iota(I32, (tg, t), 1)) // CHUNK
            key = jnp.where(kpos // CHUNK <= qchunk, key, INT_MIN)
        keys_sc[pl.ds(off, tg), :] = key

    def fill_pair(j, c):
        rel_into(relb_sc, 2 * j + 1)
        score_keys(rela_sc, 2 * j, False)
        rel_into(rela_sc, 2 * j + 2)
        score_keys(relb_sc, 2 * j + 1, False)
        return c

    rel_into(rela_sc, 0)
    lax.fori_loop(0, last // 2, fill_pair, 0)
    tail = 2 * (last // 2)
    rel_into(relb_sc, tail + 1)
    score_keys(rela_sc, tail, True)

    @pl.when(tail + 1 <= last)
    def _():
        score_keys(relb_sc, tail + 1, True)

    def count_ge(cand):
        rows = 16 * LANE * 8 // t
        cb = jnp.broadcast_to(cand, (rows, t))

        def body(g, acc):
            off = pl.multiple_of(g * tg, tg)
            for k in range(tg // rows):
                acc = acc + (keys_sc[pl.ds(off + k * rows, rows), :] >= cb).astype(F32)
            return acc

        acc = lax.fori_loop(0, ng, body, jnp.zeros((rows, t), F32))
        return jnp.sum(acc, axis=0, keepdims=True)

    qpos = i * t + lax.broadcasted_iota(I32, (1, t), 1)
    n_adm = ((qpos // CHUNK + 1) * CHUNK).astype(F32)
    takes_all = n_adm <= topk

    def row_max(g, acc):
        off = pl.multiple_of(g * tg, tg)
        for k in range(tg // acc.shape[0]):
            acc = jnp.maximum(acc, keys_sc[pl.ds(off + k * acc.shape[0], acc.shape[0]), :])
        return acc

    kmax = jnp.max(lax.fori_loop(0, ng, row_max, jnp.full((16 * LANE * 8 // t, t), INT_MIN, I32)),
                   axis=0, keepdims=True)
    hi = jnp.where(kmax == INT_MAX, kmax, kmax + 1)
    guess = jnp.where(kmax > INT_MIN + 3 * 2 ** 23, kmax - 3 * 2 ** 23, INT_MIN + 1)
    n_guess = count_ge(guess)
    good = n_guess >= topk
    lo = jnp.where(good & jnp.logical_not(takes_all), guess, INT_MIN + 1)
    n_lo = jnp.where(good, n_guess, n_adm)
    hi = jnp.where(good | takes_all, hi, guess)
    steps_per_check = 3

    def unsettled(st):
        lo, hi, n_lo = st
        mid = (lo >> 1) + (hi >> 1) + (lo & hi & 1)
        done = (n_lo == topk) | (mid <= lo) | takes_all
        return jnp.min(jnp.where(done, 1.0, 0.0)) < 0.5

    def narrow(st):
        lo, hi, n_lo = st
        for _ in range(steps_per_check):
            cand = (lo >> 1) + (hi >> 1) + (lo & hi & 1)
            c = count_ge(cand)
            up = (c >= topk) & (cand > lo)
            down = (c < topk) & (cand > lo)
            lo = jnp.where(up, cand, lo)
            n_lo = jnp.where(up, c, n_lo)
            hi = jnp.where(down, cand, hi)
        return lo, hi, n_lo

    thr, _, n_ge = lax.while_loop(unsettled, narrow, (lo, hi, n_lo))
    has_ties = jnp.max(jnp.where((n_ge > topk) & jnp.logical_not(takes_all), 1.0, 0.0)) > 0.0
    thr_b = jnp.broadcast_to(thr, (tg, t))

    for h in range(DSA_HEADS):
        ql = jnp.dot(wukT_ref[h], qT_ref[0, h], preferred_element_type=F32)
        qlat_sc[:, h * t:(h + 1) * t] = (ql * (DSA_HEAD_DIM ** -0.5 * LOG2E)).astype(BF16)
    m_sc[...] = jnp.full(m_sc.shape, NEG, F32)
    l_sc[...] = jnp.zeros(l_sc.shape, F32)
    acc_sc[...] = jnp.zeros(acc_sc.shape, F32)

    def logits_into(dst_ref, g):
        off = pl.multiple_of(jnp.minimum(g, last) * tg, tg)
        dst_ref[...] = jnp.dot(c_ref[pl.ds(off, tg), :], qlat_sc[...],
                               preferred_element_type=F32)

    def attend(lg_ref, g, seen_eq, with_ties, need):
        valid = g <= last
        off = pl.multiple_of(jnp.minimum(g, last) * tg, tg)
        key = keys_sc[pl.ds(off, tg), :]
        if with_ties:
            eq = (key == thr_b) & valid
            eqf = eq.astype(F32)
            strict_lower = (lax.broadcasted_iota(I32, (tg, tg), 0)
                            > lax.broadcasted_iota(I32, (tg, tg), 1)).astype(BF16)
            rank = jnp.dot(strict_lower, eqf.astype(BF16), preferred_element_type=F32) + seen_eq
            sel = (key > thr_b) | (eq & (rank < need))
            seen_eq = seen_eq + jnp.sum(eqf, axis=0, keepdims=True)
        else:
            sel = key >= thr_b
        bias = jnp.where(sel & valid, 0.0, NEG)
        probs, alphas = [], []
        for h in range(DSA_HEADS):
            s = lg_ref[:, h * t:(h + 1) * t] + bias
            m_prev = m_sc[h]
            m_new = jnp.maximum(m_prev, jnp.max(s, axis=0, keepdims=True))
            alpha = jnp.exp2(m_prev - m_new)
            p = jnp.exp2(s - m_new)
            l_sc[h] = alpha * l_sc[h] + jnp.sum(p, axis=0, keepdims=True)
            m_sc[h] = m_new
            probs.append(p.astype(BF16))
            alphas.append(alpha)
        pv = jnp.dot(cT_ref[:, pl.ds(off, tg)], jnp.concatenate(probs, axis=1),
                     preferred_element_type=F32)
        for h in range(DSA_HEADS):
            acc_sc[h] = alphas[h] * acc_sc[h] + pv[:, h * t:(h + 1) * t]
        return seen_eq

    def sweep(with_ties, need=None):
        logits_into(lga_sc, 0)

        def pair(j, seen_eq):
            g0 = 2 * j
            logits_into(lgb_sc, g0 + 1)
            seen_eq = attend(lga_sc, g0, seen_eq, with_ties, need)
            logits_into(lga_sc, g0 + 2)
            return attend(lgb_sc, g0 + 1, seen_eq, with_ties, need)

        lax.fori_loop(0, last // 2 + 1, pair, jnp.zeros((1, t), F32))

    @pl.when(has_ties)
    def _():
        sweep(True, topk - count_ge(thr + 1))

    @pl.when(jnp.logical_not(has_ties))
    def _():
        sweep(False)

    for h in range(DSA_HEADS):
        o_lat = (acc_sc[h] / l_sc[h]).astype(BF16)
        o_ref[0, h] = jnp.dot(wuvT_ref[h], o_lat, preferred_element_type=F32).astype(o_ref.dtype)


def _dsa_attention(qiT, w, qT, wukT, wuvT, ki, c, cT, topk):
    NB, H, Dh, t = qT.shape
    S = NB * t
    assert S < 2 ** 23
    R = DSA_KV_RANK
    tg = min(512, S)
    const2 = lambda i: (0, 0)
    const3 = lambda i: (0, 0, 0)
    return pl.pallas_call(
        functools.partial(_dsa_kernel, t=t, tg=tg, topk=topk),
        grid=(NB,),
        in_specs=[pl.BlockSpec((1, IDX_HEAD_DIM, IDX_HEADS * t), lambda i: (i, 0, 0)),
                  pl.BlockSpec((1, IDX_HEADS, t), lambda i: (i, 0, 0)),
                  pl.BlockSpec((1, H, Dh, t), lambda i: (i, 0, 0, 0)),
                  pl.BlockSpec((H, R, Dh), const3),
                  pl.BlockSpec((H, Dh, R), const3),
                  pl.BlockSpec((S, IDX_HEAD_DIM), const2, pipeline_mode=pl.Buffered(1)),
                  pl.BlockSpec((S, R), const2, pipeline_mode=pl.Buffered(1)),
                  pl.BlockSpec((R, S), const2, pipeline_mode=pl.Buffered(1))],
        out_specs=pl.BlockSpec((1, H, Dh, t), lambda i: (i, 0, 0, 0)),
        out_shape=jax.ShapeDtypeStruct((NB, H, Dh, t), BF16),
        scratch_shapes=[pltpu.VMEM((S, t), I32), pltpu.VMEM((R, H * t), BF16),
                        pltpu.VMEM((H, 1, t), F32), pltpu.VMEM((H, 1, t), F32),
                        pltpu.VMEM((H, R, t), F32),
                        pltpu.VMEM((tg, H * t), F32), pltpu.VMEM((tg, H * t), F32),
                        pltpu.VMEM((tg, IDX_HEADS * t), F32), pltpu.VMEM((tg, IDX_HEADS * t), F32)],
        compiler_params=_params("arbitrary"),
        name="dsa_attention",
    )(qiT, w, qT, wukT, wuvT, ki, c, cT)


def _shift_mix(p, prev_tail, mu, first_block):
    rows = lax.broadcasted_iota(I32, p.shape, 0)
    tail = jnp.where(first_block, 0.0, prev_tail)
    prev = jnp.where(rows == 0, tail, pltpu.roll(p, 1, axis=0))
    return p + (prev - p) * mu


def _split3(f):
    hi = f.astype(BF16)
    r1 = f - hi.astype(F32)
    mid = r1.astype(BF16)
    lo = (r1 - mid.astype(F32)).astype(BF16)
    return hi, mid, lo


def _dot3(x, w01):
    return sum(jnp.dot(part, w01, preferred_element_type=F32) for part in _split3(x))


def _dot3_left(w01, x):
    return sum(jnp.dot(w01, part, preferred_element_type=F32) for part in _split3(x))


def _rwkv_pre_kernel(*refs, with_vres):
    (p_ref, pp_ref, mu_ref, w0_ref, a0_ref, wwa_ref, gup_ref, kk_ref, ka_ref, rk_ref,
     tri_ref, blk_ref, bd_ref) = refs[:13]
    if with_vres:
        vup_ref, vv0_ref, vf_ref = refs[13:16]
    (at_o, rt_o, bt_o, kt_o, bh_o, kh_o, vb_o, pc_o, bonus_o, g_o, v_o) = refs[-11:]
    first = pl.program_id(0) == 0
    W = RWKV_W
    ps = _shift_mix(p_ref[...], pp_ref[7:8, :], mu_ref[...], first)
    r, k, v = ps[:, 0:W], ps[:, W:2 * W], ps[:, 2 * W:3 * W]
    wa = ps[:, 3 * W:3 * W + LANE]
    gl = ps[:, 3 * W + LANE:3 * W + 2 * LANE]
    lane = lax.broadcasted_iota(I32, wa.shape, 1)
    wa = jnp.where(lane < RWKV_W_LORA, jnp.tanh(wa), wa)
    up = jnp.dot(wa.astype(BF16), wwa_ref[...], preferred_element_type=F32)
    log_w = -_softplus(-(w0_ref[...] + up[:, 0:W])) - 0.5
    a = _sigmoid(a0_ref[...] + up[:, W:2 * W])
    g = jnp.dot(_sigmoid(gl).astype(BF16), gup_ref[...], preferred_element_type=F32)
    if with_vres:
        vl = ps[:, RWKV_IN:RWKV_IN + LANE]
        logit = vv0_ref[...] + jnp.dot(vl.astype(BF16), vup_ref[...], preferred_element_type=F32)
        v = v + (vf_ref[...] - v) * _sigmoid(logit)
    k2 = k * (1.0 + (a - 1.0) * ka_ref[...])
    kkr = k * kk_ref[...]
    bd = bd_ref[...]
    kk = kkr * lax.rsqrt(_dot3(kkr * kkr, bd) + 1e-12)
    lw = -jnp.exp(log_w)
    cum = _dot3_left(tri_ref[...], lw)
    cend = _dot3_left(blk_ref[...], lw)
    p_inv = jnp.exp(-cum)
    p_end = jnp.exp(cend - cum)
    beta = kk * a
    at_o[...] = (-kk * jnp.exp(cum - lw)).astype(BF16)
    rt_o[...] = (r * jnp.exp(cum)).astype(BF16)
    bt_o[...] = (beta * p_inv).astype(BF16)
    kt_o[...] = (k2 * p_inv).astype(BF16)
    bh_o[...] = (beta * p_end).astype(BF16)
    kh_o[...] = (k2 * p_end).astype(BF16)
    vb_o[...] = v.astype(BF16)
    pc_o[...] = jnp.exp(cend)
    bonus_o[...] = _dot3(r * k2 * rk_ref[...], bd) * v
    g_o[...] = g
    v_o[...] = v


def _block_ones(n, block, lower_tri=False):
    i = jnp.arange(n)
    m = (i[:, None] // block) == (i[None, :] // block)
    if lower_tri:
        m = m & (i[:, None] >= i[None, :])
    return m.astype(BF16)


def _rwkv_pre(p, mu, w0, a0, wwa, gup, k_k, k_a, r_k, vres=None):
    S, PW = p.shape
    W = RWKV_W
    tm = min(512, S)
    row = lambda i: (i, 0)
    const = lambda i: (0, 0)
    tail = lambda i: (jnp.maximum(i * (tm // 8) - 1, 0), 0)
    vec = lambda a: a.reshape(1, -1)
    args = [p, p, vec(mu), vec(w0), vec(a0), wwa, gup, vec(k_k), vec(k_a), vec(r_k),
            _block_ones(tm, CHUNK, lower_tri=True), _block_ones(tm, CHUNK),
            _block_ones(W, RWKV_HEAD_DIM)]
    specs = [pl.BlockSpec((tm, PW), row), pl.BlockSpec((8, PW), tail),
             pl.BlockSpec((1, PW), const), pl.BlockSpec((1, W), const), pl.BlockSpec((1, W), const),
             pl.BlockSpec(wwa.shape, const), pl.BlockSpec(gup.shape, const),
             pl.BlockSpec((1, W), const), pl.BlockSpec((1, W), const), pl.BlockSpec((1, W), const),
             pl.BlockSpec((tm, tm), const), pl.BlockSpec((tm, tm), const),
             pl.BlockSpec((W, W), const)]
    if vres is not None:
        vup, vv0, v_first = vres
        args += [vup, vec(vv0), v_first]
        specs += [pl.BlockSpec(vup.shape, const), pl.BlockSpec((1, W), const),
                  pl.BlockSpec((tm, W), row)]
    return pl.pallas_call(
        functools.partial(_rwkv_pre_kernel, with_vres=vres is not None),
        grid=(S // tm,),
        in_specs=specs,
        out_specs=[pl.BlockSpec((tm, W), row)] * 11,
        out_shape=[jax.ShapeDtypeStruct((S, W), BF16)] * 7 + [jax.ShapeDtypeStruct((S, W), F32)] * 4,
        compiler_params=_params("arbitrary"),
        name="rwkv_pre",
    )(*args)


def _rwkv_scan_kernel(at_ref, rt_ref, bt_ref, kt_ref, bh_ref, kh_ref, v_ref, pc_ref, y_ref, h_sc,
                      *, chunks):
    C = CHUNK
    N = RWKV_HEAD_DIM

    @pl.when(pl.program_id(0) == 0)
    def _():
        h_sc[...] = jnp.zeros(h_sc.shape, F32)

    ti = lax.broadcasted_iota(I32, (C, C), 0)
    tj = lax.broadcasted_iota(I32, (C, C), 1)
    lower_incl = ti >= tj
    lower_strict = ti > tj
    eye = (ti == tj).astype(F32)
    eye_n = (lax.broadcasted_iota(I32, (N, N), 0) == lax.broadcasted_iota(I32, (N, N), 1)).astype(F32)

    def mm(x, y):
        return jnp.dot(x.astype(BF16), y.astype(BF16), preferred_element_type=F32)

    def mm_nt(x, y):
        return lax.dot_general(x.astype(BF16), y.astype(BF16), (((1,), (1,)), ((), ())),
                               preferred_element_type=F32)

    def mm_tn(x, y):
        return lax.dot_general(x.astype(BF16), y.astype(BF16), (((0,), (0,)), ((), ())),
                               preferred_element_type=F32)

    units = [(c, h) for c in range(chunks) for h in range(RWKV_HEADS)]
    tile = lambda ref, u: ref[u[0] * C:(u[0] + 1) * C, u[1] * N:(u[1] + 1) * N]
    each = lambda fn: {u: fn(u) for u in units}

    At, Rt, Bt, Kt = (each(lambda u, r=ref: tile(r, u)) for ref in (at_ref, rt_ref, bt_ref, kt_ref))
    Bh, Kh, V = (each(lambda u, r=ref: tile(r, u)) for ref in (bh_ref, kh_ref, v_ref))
    AR = each(lambda u: jnp.concatenate([At[u], Rt[u]], axis=0))
    Mb = each(lambda u: mm_nt(AR[u], Bt[u]))
    Mk = each(lambda u: mm_nt(AR[u], Kt[u]))
    Lab = each(lambda u: jnp.where(lower_strict, Mb[u][0:C], 0.0))
    Mrb = each(lambda u: jnp.where(lower_incl, Mb[u][C:2 * C], 0.0))
    Lak = each(lambda u: jnp.where(lower_strict, Mk[u][0:C], 0.0))
    Mrk = each(lambda u: jnp.where(lower_incl, Mk[u][C:2 * C], 0.0))
    T = each(lambda u: eye + Lab[u])
    Lp = Lab
    span = 2
    while span < C:
        Lp = each(lambda u, Lp=Lp: mm(Lp[u], Lp[u]))
        T = each(lambda u, T=T, Lp=Lp: T[u] + mm(Lp[u], T[u]))
        span *= 2
    W1 = each(lambda u: mm(Lak[u], V[u]))
    A2 = each(lambda u: mm(T[u], At[u]))
    U0 = each(lambda u: mm(T[u], W1[u]))
    R2 = each(lambda u: Rt[u].astype(F32) + mm(Mrb[u], A2[u]))
    Y0 = each(lambda u: mm(Mrb[u], U0[u]) + mm(Mrk[u], V[u]))
    G = each(lambda u: eye_n * tile(pc_ref, u)[0:1, :] + mm_tn(Bh[u], A2[u]))
    H0 = each(lambda u: mm_tn(Bh[u], U0[u]) + mm_tn(Kh[u], V[u]))

    H = {h: h_sc[h] for h in range(RWKV_HEADS)}
    for c in range(chunks):
        ys = []
        for h in range(RWKV_HEADS):
            u = (c, h)
            ys.append(mm(R2[u], H[h]) + Y0[u])
            H[h] = mm(G[u], H[h]) + H0[u]
        y_ref[c * C:(c + 1) * C, :] = jnp.concatenate(ys, axis=1)
    for h in range(RWKV_HEADS):
        h_sc[h] = H[h]


def _rwkv_scan(at, rt, bt, kt, bh, kh, vb, pc):
    S, W = at.shape
    chunks = 2 if S % (2 * CHUNK) == 0 else 1
    tb = chunks * CHUNK
    seq = pl.BlockSpec((tb, W), lambda i: (i, 0))
    return pl.pallas_call(
        functools.partial(_rwkv_scan_kernel, chunks=chunks),
        grid=(S // tb,),
        in_specs=[seq] * 8,
        out_specs=seq,
        out_shape=jax.ShapeDtypeStruct((S, W), F32),
        scratch_shapes=[pltpu.VMEM((RWKV_HEADS, RWKV_HEAD_DIM, RWKV_HEAD_DIM), F32)],
        compiler_params=_params("arbitrary"),
        name="rwkv_scan",
    )(at, rt, bt, kt, bh, kh, vb, pc)


def _merge_kernel(x_ref, of_ref, od_ref, y_ref, bonus_ref, g_ref, lng_ref, lnb_ref, bd_ref,
                  gp_ref, bg_ref, pf_ref, pd_ref, pr_ref, wo_ref, gn_ref, xo_ref, h_ref):
    D = D_MODEL
    bd = bd_ref[...]
    inv_n = 1.0 / RWKV_HEAD_DIM
    y = y_ref[...]
    yc = y - _dot3(y, bd) * inv_n
    var = _dot3(yc * yc, bd) * inv_n
    yn = yc * lax.rsqrt(var + RWKV_LN_EPS) * lng_ref[...] + lnb_ref[...]
    o_rwkv = ((yn + bonus_ref[...]) * g_ref[...]).astype(BF16)
    gates = _sigmoid(gp_ref[...] + bg_ref[...])
    merged = (gates[:, 0:D] * jnp.dot(of_ref[...], pf_ref[...], preferred_element_type=F32)
              + gates[:, D:2 * D] * jnp.dot(od_ref[...], pd_ref[...], preferred_element_type=F32)
              + gates[:, 2 * D:3 * D] * jnp.dot(o_rwkv, pr_ref[...], preferred_element_type=F32))
    x = x_ref[...] + jnp.dot(merged.astype(BF16), wo_ref[...], preferred_element_type=F32)
    xo_ref[...] = x
    h_ref[...] = _rms(x, gn_ref[...]).astype(h_ref.dtype)


def _merge(x, o_fox, o_dsa, y_rwkv, bonus, g_rwkv, ln_g, ln_b, gate_p, b_gate, p_fox, p_dsa, p_rwkv,
           w_out, g_ffn):
    S, D = x.shape
    W = RWKV_W
    tm = min(512, S)
    row = lambda i: (i, 0)
    const = lambda i: (0, 0)
    return pl.pallas_call(
        _merge_kernel,
        grid=(S // tm,),
        in_specs=[pl.BlockSpec((tm, D), row), pl.BlockSpec((tm, FOX_W), row),
                  pl.BlockSpec((tm, DSA_W), row), pl.BlockSpec((tm, W), row),
                  pl.BlockSpec((tm, W), row), pl.BlockSpec((tm, W), row),
                  pl.BlockSpec((1, W), const), pl.BlockSpec((1, W), const),
                  pl.BlockSpec((W, W), const),
                  pl.BlockSpec((tm, 3 * D), row), pl.BlockSpec((1, 3 * D), const),
                  pl.BlockSpec((FOX_W, D), const), pl.BlockSpec((DSA_W, D), const),
                  pl.BlockSpec((W, D), const), pl.BlockSpec((D, D), const),
                  pl.BlockSpec((1, D), const)],
        out_specs=[pl.BlockSpec((tm, D), row), pl.BlockSpec((tm, D), row)],
        out_shape=[jax.ShapeDtypeStruct((S, D), F32), jax.ShapeDtypeStruct((S, D), BF16)],
        compiler_params=_params("arbitrary"),
        name="merge",
    )(x, o_fox, o_dsa, y_rwkv, bonus, g_rwkv, ln_g.reshape(1, W), ln_b.reshape(1, W),
      _block_ones(W, RWKV_HEAD_DIM), gate_p, b_gate.reshape(1, -1), p_fox, p_dsa, p_rwkv, w_out,
      g_ffn.reshape(1, D))


def _swiglu_tile(h, wg, wu):
    gate = jnp.dot(h, wg, preferred_element_type=F32)
    up = jnp.dot(h, wu, preferred_element_type=F32)
    return gate * _sigmoid(gate) * up


def _ffn_kernel(x_ref, h_ref, wg_ref, wu_ref, wd_ref, gf_ref, o_ref, acc_sc, *, final_norm):
    f = pl.program_id(1)

    @pl.when(f == 0)
    def _():
        acc_sc[...] = x_ref[...]

    act = _swiglu_tile(h_ref[...], wg_ref[...], wu_ref[...])
    acc_sc[...] += jnp.dot(act.astype(BF16), wd_ref[...], preferred_element_type=F32)

    @pl.when(f == pl.num_programs(1) - 1)
    def _():
        y = acc_sc[...]
        o_ref[...] = _rms(y, gf_ref[...]) if final_norm else y


def _ffn(x, h, wg, wu, wd, g_final, final_norm):
    S, D = x.shape
    Fd = wg.shape[1]
    tm = min(512, S)
    tf = _pick_tile(Fd, 1408)
    return pl.pallas_call(
        functools.partial(_ffn_kernel, final_norm=final_norm),
        grid=(S // tm, Fd // tf),
        in_specs=[pl.BlockSpec((tm, D), lambda i, f: (i, 0)),
                  pl.BlockSpec((tm, D), lambda i, f: (i, 0)),
                  pl.BlockSpec((D, tf), lambda i, f: (0, f)),
                  pl.BlockSpec((D, tf), lambda i, f: (0, f)),
                  pl.BlockSpec((tf, D), lambda i, f: (f, 0)),
                  pl.BlockSpec((1, D), lambda i, f: (0, 0))],
        out_specs=pl.BlockSpec((tm, D), lambda i, f: (i, 0)),
        out_shape=jax.ShapeDtypeStruct((S, D), F32),
        scratch_shapes=[pltpu.VMEM((tm, D), F32)],
        compiler_params=_params("arbitrary", "arbitrary"),
        name="ffn",
    )(x, h, wg, wu, wd, g_final.reshape(1, D))


MOE_CHUNK = 128


def _moe_kernel(x_ref, h_ref, rw_ref, rb_ref, wg_ref, wu_ref, wd_ref, gf_ref, o_ref,
                acc_sc, gate_sc, rank_sc, cnt_sc, hs_sc, ys_sc, *, final_norm):
    e = pl.program_id(1)
    f = pl.program_id(2)
    last_f = pl.num_programs(2) - 1
    T, D = h_ref.shape
    C = MOE_CHUNK
    lane = lax.broadcasted_iota(I32, (T, LANE), 1)

    @pl.when((e == 0) & (f == 0))
    def _():
        acc_sc[...] = x_ref[...]
        logits = jnp.dot(h_ref[...], rw_ref[...], preferred_element_type=F32) + rb_ref[...]
        logits = jnp.where(lane < N_EXPERTS, logits, -jnp.inf)
        v1 = jnp.max(logits, axis=-1, keepdims=True)
        i1 = jnp.min(jnp.where(logits == v1, lane, LANE), axis=-1, keepdims=True)
        rest = jnp.where(lane == i1, -jnp.inf, logits)
        v2 = jnp.max(rest, axis=-1, keepdims=True)
        i2 = jnp.min(jnp.where(rest == v2, lane, LANE), axis=-1, keepdims=True)
        e2 = jnp.exp(v2 - v1)
        p1 = 1.0 / (1.0 + e2)
        gate_sc[...] = jnp.where(lane == i1, p1, 0.0) + jnp.where(lane == i2, e2 * p1, 0.0)
        routed = ((lane == i1) | (lane == i2)).astype(F32)
        B = 256
        earlier = (lax.broadcasted_iota(I32, (B, B), 0) > lax.broadcasted_iota(I32, (B, B), 1)).astype(BF16)
        seen = jnp.zeros((1, LANE), F32)
        for b in range(T // B):
            blk = routed[b * B:(b + 1) * B]
            before = jnp.dot(earlier, blk.astype(BF16), preferred_element_type=F32) + seen
            rank_sc[b * B:(b + 1) * B, :] = jnp.where(blk > 0.0, before, -1.0)
            seen = seen + jnp.sum(blk, axis=0, keepdims=True)
        cnt_sc[...] = seen

    n_e = jnp.sum(jnp.where(lax.broadcasted_iota(I32, (1, LANE), 1) == e, cnt_sc[...], 0.0))
    n_chunks = sum(jnp.where(n_e > j * C, 1, 0) for j in range(T // C))

    @pl.when(f == 0)
    def _():
        pick8 = (lax.broadcasted_iota(I32, (8, LANE), 1) == e).astype(BF16)
        rk = rank_sc[...]
        rk_hi = rk.astype(BF16)
        rk_lo = (rk - rk_hi.astype(F32)).astype(BF16)
        nt = (((1,), (1,)), ((), ()))
        rrow = (lax.dot_general(pick8, rk_hi, nt, preferred_element_type=F32)
                + lax.dot_general(pick8, rk_lo, nt, preferred_element_type=F32))[0:1, :]

        def gather(c, carry):
            off = pl.multiple_of(c * C, C)
            slot = (off + lax.broadcasted_iota(I32, (C, T), 0)).astype(F32)
            onehot = (rrow == slot).astype(BF16)
            hs_sc[pl.ds(off, C), :] = jnp.dot(onehot, h_ref[...],
                                              preferred_element_type=F32).astype(BF16)
            ys_sc[pl.ds(off, C), :] = jnp.zeros((C, D), F32)
            return carry

        lax.fori_loop(0, n_chunks, gather, 0)

    def expert(c, carry):
        off = pl.multiple_of(c * C, C)
        act = _swiglu_tile(hs_sc[pl.ds(off, C), :], wg_ref[0], wu_ref[0])
        ys_sc[pl.ds(off, C), :] += jnp.dot(act.astype(BF16), wd_ref[0], preferred_element_type=F32)
        return carry

    lax.fori_loop(0, n_chunks, expert, 0)

    @pl.when(f == last_f)
    def _():
        mine = lane == e
        gate = jnp.sum(jnp.where(mine, gate_sc[...], 0.0), axis=-1, keepdims=True)
        rank = jnp.sum(jnp.where(mine, rank_sc[...], 0.0), axis=-1, keepdims=True)
        rank_b = jnp.broadcast_to(rank, (T, C))
        col = lax.broadcasted_iota(I32, (T, C), 1)

        def scatter(c, carry):
            off = pl.multiple_of(c * C, C)
            onehot_t = (rank_b == (col + off).astype(F32)).astype(BF16)
            y = jnp.dot(onehot_t, ys_sc[pl.ds(off, C), :].astype(BF16), preferred_element_type=F32)
            acc_sc[...] += gate * y
            return carry

        lax.fori_loop(0, n_chunks, scatter, 0)

    @pl.when((e == pl.num_programs(1) - 1) & (f == last_f))
    def _():
        y = acc_sc[...]
        o_ref[...] = _rms(y, gf_ref[...]) if final_norm else y


def _moe(x, h, rw, rb, wg, wu, wd, g_final, final_norm):
    S, D = x.shape
    E, _, Fe = wg.shape
    tm = min(1024, S)
    tf = _pick_tile(Fe, 896)
    return pl.pallas_call(
        functools.partial(_moe_kernel, final_norm=final_norm),
        grid=(S // tm, E, Fe // tf),
        in_specs=[pl.BlockSpec((tm, D), lambda i, e, f: (i, 0)),
                  pl.BlockSpec((tm, D), lambda i, e, f: (i, 0)),
                  pl.BlockSpec((D, LANE), lambda i, e, f: (0, 0)),
                  pl.BlockSpec((1, LANE), lambda i, e, f: (0, 0)),
                  pl.BlockSpec((1, D, tf), lambda i, e, f: (e, 0, f)),
                  pl.BlockSpec((1, D, tf), lambda i, e, f: (e, 0, f)),
                  pl.BlockSpec((1, tf, D), lambda i, e, f: (e, f, 0)),
                  pl.BlockSpec((1, D), lambda i, e, f: (0, 0))],
        out_specs=pl.BlockSpec((tm, D), lambda i, e, f: (i, 0)),
        out_shape=jax.ShapeDtypeStruct((S, D), F32),
        scratch_shapes=[pltpu.VMEM((tm, D), F32), pltpu.VMEM((tm, LANE), F32),
                        pltpu.VMEM((tm, LANE), F32), pltpu.VMEM((1, LANE), F32),
                        pltpu.VMEM((tm, D), BF16), pltpu.VMEM((tm, D), F32)],
        compiler_params=_params("arbitrary", "arbitrary", "arbitrary"),
        name="moe",
    )(x, h, rw, rb, wg, wu, wd, g_final.reshape(1, D))


def _cast_kernel(x_ref, o_ref):
    o_ref[...] = x_ref[...].astype(o_ref.dtype)


def _to_bf16(w):
    E, A, B = w.shape
    ta = 256
    return pl.pallas_call(
        _cast_kernel,
        grid=(E, A // ta),
        in_specs=[pl.BlockSpec((1, ta, B), lambda e, a: (e, a, 0))],
        out_specs=pl.BlockSpec((1, ta, B), lambda e, a: (e, a, 0)),
        out_shape=jax.ShapeDtypeStruct(w.shape, BF16),
        compiler_params=_params("arbitrary", "arbitrary"),
        name="to_bf16",
    )(w)


def _pad_cols(a, n):
    return jnp.pad(a, ((0, 0), (0, n - a.shape[1])))


def kernel(x, w_in, b_gate, g_mix, fox_f_bias, dsa_kv_norm, dsa_w_uk, dsa_w_uv, rwkv_mu, rwkv_w0, rwkv_w_up, rwkv_a0, rwkv_a_up, rwkv_g_up, rwkv_k_k, rwkv_k_a, rwkv_r_k, rwkv_ln_g, rwkv_ln_b, vres_down, vres_mu, vres_up, vres_v0, p_fox, p_dsa, p_rwkv, w_out, g_ffn, ffn_w_gate, ffn_w_up, ffn_w_down, router_w, router_b, moe_w_gate, moe_w_up, moe_w_down, g_final):
    B, S, D = x.shape
    assert B == 1 and D == D_MODEL and S % LANE == 0
    depth = w_in.shape[0]
    topk = min(IDX_TOPK, S // 4)
    bf = lambda a: a.astype(BF16)
    xs = x[0]
    v_first = None
    w_in_bf = _to_bf16(w_in)
    for l in range(depth):
        wl = w_in_bf[l]
        w_fox, w_dsa, w_rwkv, w_gate = (wl[:, :FOX_IN], wl[:, FOX_IN:FOX_IN + DSA_IN],
                                        wl[:, FOX_IN + DSA_IN:FOX_IN + DSA_IN + RWKV_IN],
                                        wl[:, FOX_IN + DSA_IN + RWKV_IN:])
        o1 = DSA_W + DSA_KV_RANK + IDX_W
        w_att = jnp.concatenate([_pad_cols(w_fox, 3 * FOX_W + LANE), w_dsa[:, :o1],
                                 _pad_cols(w_dsa[:, o1:], LANE)], axis=1)
        if l > 0:
            w_rwkv = jnp.concatenate([w_rwkv, bf(_pad_cols(vres_down[l - 1], LANE))], axis=1)
        pa = _rms_proj(xs, g_mix[l], w_att)
        pr = _rms_proj(xs, g_mix[l], w_rwkv)
        gate_p = _rms_proj(xs, g_mix[l], w_gate)

        c0 = 3 * FOX_W
        fl = pa[:, c0:c0 + FOX_HEADS].T.reshape(FOX_HEADS, S // LANE, LANE)
        Fh, Fm, Fl = (a.reshape(FOX_HEADS, S) for a in _fox_cumsum(fl, fox_f_bias[l]))
        qT = bf(pa[:, 0:FOX_W].T.reshape(FOX_HEADS, FOX_HEAD_DIM, S) * (FOX_HEAD_DIM ** -0.5 * LOG2E))
        r = jnp.arange(FOX_KA - FOX_HEAD_DIM)
        rq, ck = r[None, :, None], r[None, None, :]
        pick = lambda idx, a, b, c, d: jnp.where(idx == a[0], a[1], jnp.where(
            idx == b[0], b[1], jnp.where(idx == c[0], c[1], jnp.where(idx < 6, d, 0)))).astype(BF16)
        fq = pick(rq, (0, Fh[:, None, :]), (1, Fm[:, None, :]), (2, Fl[:, None, :]), 1)
        fk = pick(ck, (3, -Fh[:, :, None]), (4, -Fm[:, :, None]), (5, -Fl[:, :, None]), 1)
        qaT = jnp.concatenate([qT, fq], axis=1)
        k4 = bf(pa[:, FOX_W:2 * FOX_W].reshape(S, FOX_HEADS, FOX_HEAD_DIM).transpose(1, 0, 2))
        ka = jnp.concatenate([k4, fk], axis=2)
        fvT = bf(pa[:, 2 * FOX_W:3 * FOX_W].T.reshape(FOX_HEADS, FOX_HEAD_DIM, S))
        o_fox = _fox_attention(qaT, ka, fvT).reshape(FOX_W, S).T

        t = min(DSA_T, S)
        nb = S // t
        c1 = c0 + LANE
        dqT = bf(pa[:, c1:c1 + DSA_W].T.reshape(DSA_HEADS, DSA_HEAD_DIM, nb, t).transpose(2, 0, 1, 3))
        c2 = c1 + DSA_W
        ckv = _rmsnorm(pa[:, c2:c2 + DSA_KV_RANK], dsa_kv_norm[l], BF16)
        c3 = c2 + DSA_KV_RANK
        qiT = bf(pa[:, c3:c3 + IDX_W].T.reshape(IDX_HEADS, IDX_HEAD_DIM, nb, t).transpose(2, 1, 0, 3)
                 .reshape(nb, IDX_HEAD_DIM, IDX_HEADS * t))
        c4 = c3 + IDX_W
        ki = bf(pa[:, c4:c4 + IDX_HEAD_DIM])
        c5 = c4 + IDX_HEAD_DIM
        wi = pa[:, c5:c5 + IDX_HEADS].T.reshape(IDX_HEADS, nb, t).transpose(1, 0, 2) * (IDX_W ** -0.5)
        o_dsa = _dsa_attention(qiT, wi, dqT, bf(dsa_w_uk[l].transpose(0, 2, 1)),
                               bf(dsa_w_uv[l].transpose(0, 2, 1)), ki, ckv, ckv.T, topk)
        o_dsa = o_dsa.transpose(1, 2, 0, 3).reshape(DSA_W, S).T

        zw = jnp.zeros((RWKV_W_LORA, RWKV_W), F32)
        wwa = bf(jnp.concatenate([jnp.concatenate([rwkv_w_up[l], zw], axis=1),
                                  jnp.concatenate([zw, rwkv_a_up[l]], axis=1)], axis=0))
        vres = None
        mu = rwkv_mu[l]
        if l > 0:
            vup = jnp.pad(vres_up[l - 1], ((0, LANE - RWKV_V_LORA), (0, 0)))
            vres = (bf(vup), vres_v0[l - 1], v_first)
            mu = jnp.concatenate([mu, jnp.pad(vres_mu[l - 1], (0, LANE - RWKV_V_LORA))])
        *scan_ops, bonus, g_rwkv, v = _rwkv_pre(
            pr, mu, rwkv_w0[l], rwkv_a0[l], wwa, bf(rwkv_g_up[l]),
            rwkv_k_k[l], rwkv_k_a[l], rwkv_r_k[l], vres)
        if l == 0:
            v_first = v
        y_rwkv = _rwkv_scan(*scan_ops)

        xs, h2 = _merge(xs, o_fox, o_dsa, y_rwkv, bonus, g_rwkv, rwkv_ln_g[l], rwkv_ln_b[l],
                        gate_p, b_gate[l], bf(p_fox[l]), bf(p_dsa[l]), bf(p_rwkv[l]), bf(w_out[l]),
                        g_ffn[l])

        last = l == depth - 1
        if l % 2 == 0:
            xs = _ffn(xs, h2, bf(ffn_w_gate[l // 2]), bf(ffn_w_up[l // 2]), bf(ffn_w_down[l // 2]),
                      g_final, last)
        else:
            rw = bf(_pad_cols(router_w[l // 2], LANE))
            rb = _pad_cols(router_b[l // 2].reshape(1, -1), LANE)
            xs = _moe(xs, h2, rw, rb, _to_bf16(moe_w_gate[l // 2]), _to_bf16(moe_w_up[l // 2]),
                      _to_bf16(moe_w_down[l // 2]), g_final, last)
    return xs[None]
```

```python
import functools

import jax
import jax.numpy as jnp
from jax import lax
from jax.experimental import pallas as pl
from jax.experimental.pallas import tpu as pltpu

F32 = jnp.float32
BF16 = jnp.bfloat16
I32 = jnp.int32

D_MODEL = 1024
CHUNK = 64
RMS_EPS = 1e-6
FOX_HEADS, FOX_HEAD_DIM = 4, 64
DSA_HEADS, DSA_HEAD_DIM, DSA_KV_RANK = 4, 64, 128
IDX_HEADS, IDX_HEAD_DIM, IDX_TOPK = 8, 32, 256
RWKV_HEADS, RWKV_HEAD_DIM = 8, 64
RWKV_W_LORA, RWKV_A_LORA, RWKV_V_LORA, RWKV_G_LORA = 64, 64, 32, 128
RWKV_LN_EPS = 64e-5
FOX_W = FOX_HEADS * FOX_HEAD_DIM
DSA_W = DSA_HEADS * DSA_HEAD_DIM
RWKV_W = RWKV_HEADS * RWKV_HEAD_DIM
IDX_W = IDX_HEADS * IDX_HEAD_DIM
N_EXPERTS = 8
FOX_IN = 3 * FOX_W + FOX_HEADS
DSA_IN = DSA_W + DSA_KV_RANK + IDX_W + IDX_HEAD_DIM + IDX_HEADS
RWKV_IN = 3 * RWKV_W + RWKV_W_LORA + RWKV_A_LORA + RWKV_G_LORA

LANE = 128
VMEM_LIMIT = 52 * 1024 * 1024
NEG = -1e30
INT_MIN = -(2 ** 31)
INT_MAX = 2 ** 31 - 1
LOG2E = 1.4426950408889634
F32_MIN_NORMAL = 2.0 ** -126
HI = lax.Precision.HIGHEST


def _params(*sem):
    return pltpu.CompilerParams(dimension_semantics=sem, vmem_limit_bytes=VMEM_LIMIT)


def _pick_tile(n, cap):
    best = LANE
    for t in range(LANE, min(n, cap) + 1, LANE):
        if n % t == 0:
            best = t
    return best


def _softplus(x):
    return jnp.maximum(x, 0.0) + jnp.log1p(jnp.exp(-jnp.abs(x)))


def _sigmoid(x):
    return 1.0 / (1.0 + jnp.exp(-x))


def _rms(x, g):
    return x * lax.rsqrt(jnp.mean(x * x, axis=-1, keepdims=True) + RMS_EPS) * g


def _rms_proj_kernel(x_ref, g_ref, w_ref, o_ref):
    h = _rms(x_ref[...], g_ref[...])
    o_ref[...] = jnp.dot(h.astype(BF16), w_ref[...], preferred_element_type=F32)


def _rms_proj(x, g, w):
    S, D = x.shape
    N = w.shape[1]
    tm = min(512, S)
    tn = _pick_tile(N, 2304)
    return pl.pallas_call(
        _rms_proj_kernel,
        grid=(N // tn, S // tm),
        in_specs=[pl.BlockSpec((tm, D), lambda j, i: (i, 0)),
                  pl.BlockSpec((1, D), lambda j, i: (0, 0)),
                  pl.BlockSpec((D, tn), lambda j, i: (0, j))],
        out_specs=pl.BlockSpec((tm, tn), lambda j, i: (i, j)),
        out_shape=jax.ShapeDtypeStruct((S, N), F32),
        compiler_params=_params("arbitrary", "arbitrary"),
        name="rms_proj",
    )(x, g.reshape(1, D), w)


def _rmsnorm_kernel(x_ref, g_ref, o_ref):
    o_ref[...] = _rms(x_ref[...], g_ref[...]).astype(o_ref.dtype)


def _rmsnorm(x, g, dtype):
    S, D = x.shape
    tm = min(2048, S)
    return pl.pallas_call(
        _rmsnorm_kernel,
        grid=(S // tm,),
        in_specs=[pl.BlockSpec((tm, D), lambda i: (i, 0)),
                  pl.BlockSpec((1, D), lambda i: (0, 0))],
        out_specs=pl.BlockSpec((tm, D), lambda i: (i, 0)),
        out_shape=jax.ShapeDtypeStruct((S, D), dtype),
        compiler_params=_params("arbitrary"),
        name="rmsnorm",
    )(x, g.reshape(1, D))


def _fox_cumsum_kernel(fl_ref, b_ref, hi_ref, mid_ref, lo_ref):
    H, R, _ = fl_ref.shape
    upper = (lax.broadcasted_iota(I32, (LANE, LANE), 0)
             <= lax.broadcasted_iota(I32, (LANE, LANE), 1)).astype(F32)
    strict_lower = (lax.broadcasted_iota(I32, (R, R), 0)
                    > lax.broadcasted_iota(I32, (R, R), 1)).astype(F32)
    for h in range(H):
        log_f = -_softplus(-(fl_ref[h] + b_ref[h]))
        within = jnp.dot(log_f, upper, preferred_element_type=F32, precision=HI)
        row_tot = jnp.broadcast_to(within[:, LANE - 1:LANE], (R, LANE))
        before = jnp.dot(strict_lower, row_tot, preferred_element_type=F32, precision=HI)
        hi, mid, lo = _split3((within + before) * LOG2E)
        hi_ref[h], mid_ref[h], lo_ref[h] = hi, mid, lo


def _fox_cumsum(fl, bias):
    H, R, _ = fl.shape
    return pl.pallas_call(
        _fox_cumsum_kernel,
        out_shape=[jax.ShapeDtypeStruct((H, R, LANE), BF16)] * 3,
        compiler_params=pltpu.CompilerParams(vmem_limit_bytes=VMEM_LIMIT),
        name="fox_cumsum",
    )(fl, jnp.broadcast_to(bias.reshape(H, 1, 1), (H, 1, LANE)))


FOX_KA = 128


def _fox_kernel(qa_ref, ka_ref, vT_ref, o_ref, lga_sc, lgb_sc, m_sc, l_sc, acc_sc, *, t, tc, tg):
    i = pl.program_id(1)
    hb = qa_ref.shape[0]
    last = (i * t) // tg
    chains = [(h, q0) for h in range(hb) for q0 in range(0, t, tc)]

    def logits_into(dst_ref, g):
        off = pl.multiple_of(jnp.minimum(g, last) * tg, tg)
        for n, (h, q0) in enumerate(chains):
            dst_ref[n] = jnp.dot(ka_ref[h, pl.ds(off, tg), :], qa_ref[h, :, q0:q0 + tc],
                                 preferred_element_type=F32)

    def update(lg_ref, g, masked):
        off = pl.multiple_of(g * tg, tg)
        for n, (h, q0) in enumerate(chains):
            s = lg_ref[n]
            if masked:
                kpos = off + lax.broadcasted_iota(I32, (tg, tc), 0)
                qpos = i * t + q0 + lax.broadcasted_iota(I32, (tg, tc), 1)
                s = jnp.where(kpos <= qpos, s, NEG)
            m_prev = m_sc[n]
            m_new = jnp.maximum(m_prev, jnp.max(s, axis=0, keepdims=True))
            alpha = jnp.exp2(m_prev - m_new)
            p = jnp.exp2(s - m_new)
            l_sc[n] = alpha * l_sc[n] + jnp.sum(p, axis=0, keepdims=True)
            acc_sc[n] = alpha * acc_sc[n] + jnp.dot(vT_ref[h, :, pl.ds(off, tg)], p.astype(BF16),
                                                    preferred_element_type=F32)
            m_sc[n] = m_new

    m_sc[...] = jnp.full(m_sc.shape, NEG, F32)
    l_sc[...] = jnp.zeros(l_sc.shape, F32)
    acc_sc[...] = jnp.zeros(acc_sc.shape, F32)
    logits_into(lga_sc, 0)

    def pair(j, c):
        logits_into(lgb_sc, 2 * j + 1)
        update(lga_sc, 2 * j, False)
        logits_into(lga_sc, 2 * j + 2)
        update(lgb_sc, 2 * j + 1, False)
        return c

    lax.fori_loop(0, last // 2, pair, 0)
    tail = 2 * (last // 2)
    logits_into(lgb_sc, tail + 1)
    update(lga_sc, tail, True)

    @pl.when(tail + 1 <= last)
    def _():
        update(lgb_sc, tail + 1, True)

    for n, (h, q0) in enumerate(chains):
        o_ref[h, :, q0:q0 + tc] = (acc_sc[n] / l_sc[n]).astype(o_ref.dtype)


def _fox_attention(qaT, ka, vT):
    H, KA, S = qaT.shape
    Dh = vT.shape[1]
    t = min(512, S)
    tc = min(256, t)
    tg = min(1024, S)
    hb = 2
    nc = hb * (t // tc)
    return pl.pallas_call(
        functools.partial(_fox_kernel, t=t, tc=tc, tg=tg),
        grid=(H // hb, S // t),
        in_specs=[pl.BlockSpec((hb, KA, t), lambda h, i: (h, 0, i)),
                  pl.BlockSpec((hb, S, KA), lambda h, i: (h, 0, 0)),
                  pl.BlockSpec((hb, Dh, S), lambda h, i: (h, 0, 0))],
        out_specs=pl.BlockSpec((hb, Dh, t), lambda h, i: (h, 0, i)),
        out_shape=jax.ShapeDtypeStruct((H, Dh, S), BF16),
        scratch_shapes=[pltpu.VMEM((nc, tg, tc), F32), pltpu.VMEM((nc, tg, tc), F32),
                        pltpu.VMEM((nc, 1, tc), F32), pltpu.VMEM((nc, 1, tc), F32),
                        pltpu.VMEM((nc, Dh, tc), F32)],
        compiler_params=_params("arbitrary", "arbitrary"),
        name="fox_attention",
    )(qaT, ka, vT)


DSA_T = 256


def _dsa_kernel(qiT_ref, w_ref, qT_ref, wukT_ref, wuvT_ref, ki_ref, c_ref, cT_ref, o_ref,
                keys_sc, qlat_sc, m_sc, l_sc, acc_sc, lga_sc, lgb_sc, rela_sc, relb_sc,
                *, t, tg, topk):
    i = pl.program_id(0)
    last = (i * t) // tg
    ng = last + 1

    qiT = qiT_ref[0]
    w = w_ref[0]

    def rel_into(dst_ref, g):
        off = pl.multiple_of(jnp.minimum(g, last) * tg, tg)
        dst_ref[...] = jnp.dot(ki_ref[pl.ds(off, tg), :], qiT, preferred_element_type=F32)

    def score_keys(rel_ref, g, masked):
        off = pl.multiple_of(g * tg, tg)
        sc = jnp.maximum(rel_ref[:, 0:t], 0.0) * w[0:1, :]
        for h in range(1, IDX_HEADS):
            sc = sc + jnp.maximum(rel_ref[:, h * t:(h + 1) * t], 0.0) * w[h:h + 1, :]
        bits = pltpu.bitcast(sc, I32)
        key = bits ^ ((bits >> 31) & 0x7FFFFFFF)
        kpos = off + lax.broadcasted_iota(I32, (tg, t), 0)
        key = jnp.where(jnp.abs(sc) < F32_MIN_NORMAL, -1 - kpos, key)
        if masked:
            qchunk = (i * t + lax.broadcasted_iota(I32, (tg, t), 1)) // CHUNK
            key = jnp.where(kpos // CHUNK <= qchunk, key, INT_MIN)
        keys_sc[pl.ds(off, tg), :] = key

    def fill_pair(j, c):
        rel_into(relb_sc, 2 * j + 1)
        score_keys(rela_sc, 2 * j, False)
        rel_into(rela_sc, 2 * j + 2)
        score_keys(relb_sc, 2 * j + 1, False)
        return c

    rel_into(rela_sc, 0)
    lax.fori_loop(0, last // 2, fill_pair, 0)
    tail = 2 * (last // 2)
    rel_into(relb_sc, tail + 1)
    score_keys(rela_sc, tail, True)

    @pl.when(tail + 1 <= last)
    def _():
        score_keys(relb_sc, tail + 1, True)

    def count_ge(cand):
        rows = 16 * LANE * 8 // t
        cb = jnp.broadcast_to(cand, (rows, t))

        def body(g, acc):
            off = pl.multiple_of(g * tg, tg)
            for k in range(tg // rows):
                acc = acc + (keys_sc[pl.ds(off + k * rows, rows), :] >= cb).astype(F32)
            return acc

        acc = lax.fori_loop(0, ng, body, jnp.zeros((rows, t), F32))
        return jnp.sum(acc, axis=0, keepdims=True)

    qpos = i * t + lax.broadcasted_iota(I32, (1, t), 1)
    n_adm = ((qpos // CHUNK + 1) * CHUNK).astype(F32)
    takes_all = n_adm <= topk

    def row_max(g, acc):
        off = pl.multiple_of(g * tg, tg)
        for k in range(tg // acc.shape[0]):
            acc = jnp.maximum(acc, keys_sc[pl.ds(off + k * acc.shape[0], acc.shape[0]), :])
        return acc

    kmax = jnp.max(lax.fori_loop(0, ng, row_max, jnp.full((16 * LANE * 8 // t, t), INT_MIN, I32)),
                   axis=0, keepdims=True)
    hi = jnp.where(kmax == INT_MAX, kmax, kmax + 1)
    guess = jnp.where(kmax > INT_MIN + 3 * 2 ** 23, kmax - 3 * 2 ** 23, INT_MIN + 1)
    n_guess = count_ge(guess)
    good = n_guess >= topk
    lo = jnp.where(good & jnp.logical_not(takes_all), guess, INT_MIN + 1)
    n_lo = jnp.where(good, n_guess, n_adm)
    hi = jnp.where(good | takes_all, hi, guess)
    steps_per_check = 3

    def unsettled(st):
        lo, hi, n_lo = st
        mid = (lo >> 1) + (hi >> 1) + (lo & hi & 1)
        done = (n_lo == topk) | (mid <= lo) | takes_all
        return jnp.min(jnp.where(done, 1.0, 0.0)) < 0.5

    def narrow(st):
        lo, hi, n_lo = st
        for _ in range(steps_per_check):
            cand = (lo >> 1) + (hi >> 1) + (lo & hi & 1)
            c = count_ge(cand)
            up = (c >= topk) & (cand > lo)
            down = (c < topk) & (cand > lo)
            lo = jnp.where(up, cand, lo)
            n_lo = jnp.where(up, c, n_lo)
            hi = jnp.where(down, cand, hi)
        return lo, hi, n_lo

    thr, _, n_ge = lax.while_loop(unsettled, narrow, (lo, hi, n_lo))
    has_ties = jnp.max(jnp.where((n_ge > topk) & jnp.logical_not(takes_all), 1.0, 0.0)) > 0.0
    thr_b = jnp.broadcast_to(thr, (tg, t))

    for h in range(DSA_HEADS):
        ql = jnp.dot(wukT_ref[h], qT_ref[0, h], preferred_element_type=F32)
        qlat_sc[:, h * t:(h + 1) * t] = (ql * (DSA_HEAD_DIM ** -0.5 * LOG2E)).astype(BF16)
    m_sc[...] = jnp.full(m_sc.shape, NEG, F32)
    l_sc[...] = jnp.zeros(l_sc.shape, F32)
    acc_sc[...] = jnp.zeros(acc_sc.shape, F32)

    def logits_into(dst_ref, g):
        off = pl.multiple_of(jnp.minimum(g, last) * tg, tg)
        dst_ref[...] = jnp.dot(c_ref[pl.ds(off, tg), :], qlat_sc[...],
                               preferred_element_type=F32)

    def attend(lg_ref, g, seen_eq, with_ties, need):
        valid = g <= last
        off = pl.multiple_of(jnp.minimum(g, last) * tg, tg)
        key = keys_sc[pl.ds(off, tg), :]
        if with_ties:
            eq = (key == thr_b) & valid
            eqf = eq.astype(F32)
            strict_lower = (lax.broadcasted_iota(I32, (tg, tg), 0)
                            > lax.broadcasted_iota(I32, (tg, tg), 1)).astype(BF16)
            rank = jnp.dot(strict_lower, eqf.astype(BF16), preferred_element_type=F32) + seen_eq
            sel = (key > thr_b) | (eq & (rank < need))
            seen_eq = seen_eq + jnp.sum(eqf, axis=0, keepdims=True)
        else:
            sel = key >= thr_b
        bias = jnp.where(sel & valid, 0.0, NEG)
        probs, alphas = [], []
        for h in range(DSA_HEADS):
            s = lg_ref[:, h * t:(h + 1) * t] + bias
            m_prev = m_sc[h]
            m_new = jnp.maximum(m_prev, jnp.max(s, axis=0, keepdims=True))
            alpha = jnp.exp2(m_prev - m_new)
            p = jnp.exp2(s - m_new)
            l_sc[h] = alpha * l_sc[h] + jnp.sum(p, axis=0, keepdims=True)
            m_sc[h] = m_new
            probs.append(p.astype(BF16))
            alphas.append(alpha)
        pv = jnp.dot(cT_ref[:, pl.ds(off, tg)], jnp.concatenate(probs, axis=1),
                     preferred_element_type=F32)
        for h in range(DSA_HEADS):
            acc_sc[h] = alphas[h] * acc_sc[h] + pv[:, h * t:(h + 1) * t]
        return seen_eq

    def sweep(with_ties, need=None):
        logits_into(lga_sc, 0)

        def pair(j, seen_eq):
            g0 = 2 * j
            logits_into(lgb_sc, g0 + 1)
            seen_eq = attend(lga_sc, g0, seen_eq, with_ties, need)
            logits_into(lga_sc, g0 + 2)
            return attend(lgb_sc, g0 + 1, seen_eq, with_ties, need)

        lax.fori_loop(0, last // 2 + 1, pair, jnp.zeros((1, t), F32))

    @pl.when(has_ties)
    def _():
        sweep(True, topk - count_ge(thr + 1))

    @pl.when(jnp.logical_not(has_ties))
    def _():
        sweep(False)

    for h in range(DSA_HEADS):
        o_lat = (acc_sc[h] / l_sc[h]).astype(BF16)
        o_ref[0, h] = jnp.dot(wuvT_ref[h], o_lat, preferred_element_type=F32).astype(o_ref.dtype)


def _dsa_attention(qiT, w, qT, wukT, wuvT, ki, c, cT, topk):
    NB, H, Dh, t = qT.shape
    S = NB * t
    assert S < 2 ** 23
    R = DSA_KV_RANK
    tg = min(512, S)
    const2 = lambda i: (0, 0)
    const3 = lambda i: (0, 0, 0)
    return pl.pallas_call(
        functools.partial(_dsa_kernel, t=t, tg=tg, topk=topk),
        grid=(NB,),
        in_specs=[pl.BlockSpec((1, IDX_HEAD_DIM, IDX_HEADS * t), lambda i: (i, 0, 0)),
                  pl.BlockSpec((1, IDX_HEADS, t), lambda i: (i, 0, 0)),
                  pl.BlockSpec((1, H, Dh, t), lambda i: (i, 0, 0, 0)),
                  pl.BlockSpec((H, R, Dh), const3),
                  pl.BlockSpec((H, Dh, R), const3),
                  pl.BlockSpec((S, IDX_HEAD_DIM), const2, pipeline_mode=pl.Buffered(1)),
                  pl.BlockSpec((S, R), const2, pipeline_mode=pl.Buffered(1)),
                  pl.BlockSpec((R, S), const2, pipeline_mode=pl.Buffered(1))],
        out_specs=pl.BlockSpec((1, H, Dh, t), lambda i: (i, 0, 0, 0)),
        out_shape=jax.ShapeDtypeStruct((NB, H, Dh, t), BF16),
        scratch_shapes=[pltpu.VMEM((S, t), I32), pltpu.VMEM((R, H * t), BF16),
                        pltpu.VMEM((H, 1, t), F32), pltpu.VMEM((H, 1, t), F32),
                        pltpu.VMEM((H, R, t), F32),
                        pltpu.VMEM((tg, H * t), F32), pltpu.VMEM((tg, H * t), F32),
                        pltpu.VMEM((tg, IDX_HEADS * t), F32), pltpu.VMEM((tg, IDX_HEADS * t), F32)],
        compiler_params=_params("arbitrary"),
        name="dsa_attention",
    )(qiT, w, qT, wukT, wuvT, ki, c, cT)


def _shift_mix(p, prev_tail, mu, first_block):
    rows = lax.broadcasted_iota(I32, p.shape, 0)
    tail = jnp.where(first_block, 0.0, prev_tail)
    prev = jnp.where(rows == 0, tail, pltpu.roll(p, 1, axis=0))
    return p + (prev - p) * mu


def _split3(f):
    hi = f.astype(BF16)
    r1 = f - hi.astype(F32)
    mid = r1.astype(BF16)
    lo = (r1 - mid.astype(F32)).astype(BF16)
    return hi, mid, lo


def _dot3(x, w01):
    return sum(jnp.dot(part, w01, preferred_element_type=F32) for part in _split3(x))


def _dot3_left(w01, x):
    return sum(jnp.dot(w01, part, preferred_element_type=F32) for part in _split3(x))


def _rwkv_pre_kernel(*refs, with_vres):
    (p_ref, pp_ref, mu_ref, w0_ref, a0_ref, wwa_ref, gup_ref, kk_ref, ka_ref, rk_ref,
     tri_ref, blk_ref, bd_ref) = refs[:13]
    if with_vres:
        vup_ref, vv0_ref, vf_ref = refs[13:16]
    (at_o, rt_o, bt_o, kt_o, bh_o, kh_o, vb_o, pc_o, bonus_o, g_o, v_o) = refs[-11:]
    first = pl.program_id(0) == 0
    W = RWKV_W
    ps = _shift_mix(p_ref[...], pp_ref[7:8, :], mu_ref[...], first)
    r, k, v = ps[:, 0:W], ps[:, W:2 * W], ps[:, 2 * W:3 * W]
    wa = ps[:, 3 * W:3 * W + LANE]
    gl = ps[:, 3 * W + LANE:3 * W + 2 * LANE]
    lane = lax.broadcasted_iota(I32, wa.shape, 1)
    wa = jnp.where(lane < RWKV_W_LORA, jnp.tanh(wa), wa)
    up = jnp.dot(wa.astype(BF16), wwa_ref[...], preferred_element_type=F32)
    log_w = -_softplus(-(w0_ref[...] + up[:, 0:W])) - 0.5
    a = _sigmoid(a0_ref[...] + up[:, W:2 * W])
    g = jnp.dot(_sigmoid(gl).astype(BF16), gup_ref[...], preferred_element_type=F32)
    if with_vres:
        vl = ps[:, RWKV_IN:RWKV_IN + LANE]
        logit = vv0_ref[...] + jnp.dot(vl.astype(BF16), vup_ref[...], preferred_element_type=F32)
        v = v + (vf_ref[...] - v) * _sigmoid(logit)
    k2 = k * (1.0 + (a - 1.0) * ka_ref[...])
    kkr = k * kk_ref[...]
    bd = bd_ref[...]
    kk = kkr * lax.rsqrt(_dot3(kkr * kkr, bd) + 1e-12)
    lw = -jnp.exp(log_w)
    cum = _dot3_left(tri_ref[...], lw)
    cend = _dot3_left(blk_ref[...], lw)
    p_inv = jnp.exp(-cum)
    p_end = jnp.exp(cend - cum)
    beta = kk * a
    at_o[...] = (-kk * jnp.exp(cum - lw)).astype(BF16)
    rt_o[...] = (r * jnp.exp(cum)).astype(BF16)
    bt_o[...] = (beta * p_inv).astype(BF16)
    kt_o[...] = (k2 * p_inv).astype(BF16)
    bh_o[...] = (beta * p_end).astype(BF16)
    kh_o[...] = (k2 * p_end).astype(BF16)
    vb_o[...] = v.astype(BF16)
    pc_o[...] = jnp.exp(cend)
    bonus_o[...] = _dot3(r * k2 * rk_ref[...], bd) * v
    g_o[...] = g
    v_o[...] = v


def _block_ones(n, block, lower_tri=False):
    i = jnp.arange(n)
    m = (i[:, None] // block) == (i[None, :] // block)
    if lower_tri:
        m = m & (i[:, None] >= i[None, :])
    return m.astype(BF16)


def _rwkv_pre(p, mu, w0, a0, wwa, gup, k_k, k_a, r_k, vres=None):
    S, PW = p.shape
    W = RWKV_W
    tm = min(512, S)
    row = lambda i: (i, 0)
    const = lambda i: (0, 0)
    tail = lambda i: (jnp.maximum(i * (tm // 8) - 1, 0), 0)
    vec = lambda a: a.reshape(1, -1)
    args = [p, p, vec(mu), vec(w0), vec(a0), wwa, gup, vec(k_k), vec(k_a), vec(r_k),
            _block_ones(tm, CHUNK, lower_tri=True), _block_ones(tm, CHUNK),
            _block_ones(W, RWKV_HEAD_DIM)]
    specs = [pl.BlockSpec((tm, PW), row), pl.BlockSpec((8, PW), tail),
             pl.BlockSpec((1, PW), const), pl.BlockSpec((1, W), const), pl.BlockSpec((1, W), const),
             pl.BlockSpec(wwa.shape, const), pl.BlockSpec(gup.shape, const),
             pl.BlockSpec((1, W), const), pl.BlockSpec((1, W), const), pl.BlockSpec((1, W), const),
             pl.BlockSpec((tm, tm), const), pl.BlockSpec((tm, tm), const),
             pl.BlockSpec((W, W), const)]
    if vres is not None:
        vup, vv0, v_first = vres
        args += [vup, vec(vv0), v_first]
        specs += [pl.BlockSpec(vup.shape, const), pl.BlockSpec((1, W), const),
                  pl.BlockSpec((tm, W), row)]
    return pl.pallas_call(
        functools.partial(_rwkv_pre_kernel, with_vres=vres is not None),
        grid=(S // tm,),
        in_specs=specs,
        out_specs=[pl.BlockSpec((tm, W), row)] * 11,
        out_shape=[jax.ShapeDtypeStruct((S, W), BF16)] * 7 + [jax.ShapeDtypeStruct((S, W), F32)] * 4,
        compiler_params=_params("arbitrary"),
        name="rwkv_pre",
    )(*args)


def _rwkv_scan_kernel(at_ref, rt_ref, bt_ref, kt_ref, bh_ref, kh_ref, v_ref, pc_ref, y_ref, h_sc,
                      *, chunks):
    C = CHUNK
    N = RWKV_HEAD_DIM

    @pl.when(pl.program_id(0) == 0)
    def _():
        h_sc[...] = jnp.zeros(h_sc.shape, F32)

    ti = lax.broadcasted_iota(I32, (C, C), 0)
    tj = lax.broadcasted_iota(I32, (C, C), 1)
    lower_incl = ti >= tj
    lower_strict = ti > tj
    eye = (ti == tj).astype(F32)
    eye_n = (lax.broadcasted_iota(I32, (N, N), 0) == lax.broadcasted_iota(I32, (N, N), 1)).astype(F32)

    def mm(x, y):
        return jnp.dot(x.astype(BF16), y.astype(BF16), preferred_element_type=F32)

    def mm_nt(x, y):
        return lax.dot_general(x.astype(BF16), y.astype(BF16), (((1,), (1,)), ((), ())),
                               preferred_element_type=F32)

    def mm_tn(x, y):
        return lax.dot_general(x.astype(BF16), y.astype(BF16), (((0,), (0,)), ((), ())),
                               preferred_element_type=F32)

    units = [(c, h) for c in range(chunks) for h in range(RWKV_HEADS)]
    tile = lambda ref, u: ref[u[0] * C:(u[0] + 1) * C, u[1] * N:(u[1] + 1) * N]
    each = lambda fn: {u: fn(u) for u in units}

    At, Rt, Bt, Kt = (each(lambda u, r=ref: tile(r, u)) for ref in (at_ref, rt_ref, bt_ref, kt_ref))
    Bh, Kh, V = (each(lambda u, r=ref: tile(r, u)) for ref in (bh_ref, kh_ref, v_ref))
    AR = each(lambda u: jnp.concatenate([At[u], Rt[u]], axis=0))
    Mb = each(lambda u: mm_nt(AR[u], Bt[u]))
    Mk = each(lambda u: mm_nt(AR[u], Kt[u]))
    Lab = each(lambda u: jnp.where(lower_strict, Mb[u][0:C], 0.0))
    Mrb = each(lambda u: jnp.where(lower_incl, Mb[u][C:2 * C], 0.0))
    Lak = each(lambda u: jnp.where(lower_strict, Mk[u][0:C], 0.0))
    Mrk = each(lambda u: jnp.where(lower_incl, Mk[u][C:2 * C], 0.0))
    T = each(lambda u: eye + Lab[u])
    Lp = Lab
    span = 2
    while span < C:
        Lp = each(lambda u, Lp=Lp: mm(Lp[u], Lp[u]))
        T = each(lambda u, T=T, Lp=Lp: T[u] + mm(Lp[u], T[u]))
        span *= 2
    W1 = each(lambda u: mm(Lak[u], V[u]))
    A2 = each(lambda u: mm(T[u], At[u]))
    U0 = each(lambda u: mm(T[u], W1[u]))
    R2 = each(lambda u: Rt[u].astype(F32) + mm(Mrb[u], A2[u]))
    Y0 = each(lambda u: mm(Mrb[u], U0[u]) + mm(Mrk[u], V[u]))
    G = each(lambda u: eye_n * tile(pc_ref, u)[0:1, :] + mm_tn(Bh[u], A2[u]))
    H0 = each(lambda u: mm_tn(Bh[u], U0[u]) + mm_tn(Kh[u], V[u]))

    H = {h: h_sc[h] for h in range(RWKV_HEADS)}
    for c in range(chunks):
        ys = []
        for h in range(RWKV_HEADS):
            u = (c, h)
            ys.append(mm(R2[u], H[h]) + Y0[u])
            H[h] = mm(G[u], H[h]) + H0[u]
        y_ref[c * C:(c + 1) * C, :] = jnp.concatenate(ys, axis=1)
    for h in range(RWKV_HEADS):
        h_sc[h] = H[h]


def _rwkv_scan(at, rt, bt, kt, bh, kh, vb, pc):
    S, W = at.shape
    chunks = 2 if S % (2 * CHUNK) == 0 else 1
    tb = chunks * CHUNK
    seq = pl.BlockSpec((tb, W), lambda i: (i, 0))
    return pl.pallas_call(
        functools.partial(_rwkv_scan_kernel, chunks=chunks),
        grid=(S // tb,),
        in_specs=[seq] * 8,
        out_specs=seq,
        out_shape=jax.ShapeDtypeStruct((S, W), F32),
        scratch_shapes=[pltpu.VMEM((RWKV_HEADS, RWKV_HEAD_DIM, RWKV_HEAD_DIM), F32)],
        compiler_params=_params("arbitrary"),
        name="rwkv_scan",
    )(at, rt, bt, kt, bh, kh, vb, pc)


def _merge_kernel(x_ref, of_ref, od_ref, y_ref, bonus_ref, g_ref, lng_ref, lnb_ref, bd_ref,
                  gp_ref, bg_ref, pf_ref, pd_ref, pr_ref, wo_ref, gn_ref, xo_ref, h_ref):
    D = D_MODEL
    bd = bd_ref[...]
    inv_n = 1.0 / RWKV_HEAD_DIM
    y = y_ref[...]
    yc = y - _dot3(y, bd) * inv_n
    var = _dot3(yc * yc, bd) * inv_n
    yn = yc * lax.rsqrt(var + RWKV_LN_EPS) * lng_ref[...] + lnb_ref[...]
    o_rwkv = ((yn + bonus_ref[...]) * g_ref[...]).astype(BF16)
    gates = _sigmoid(gp_ref[...] + bg_ref[...])
    merged = (gates[:, 0:D] * jnp.dot(of_ref[...], pf_ref[...], preferred_element_type=F32)
              + gates[:, D:2 * D] * jnp.dot(od_ref[...], pd_ref[...], preferred_element_type=F32)
              + gates[:, 2 * D:3 * D] * jnp.dot(o_rwkv, pr_ref[...], preferred_element_type=F32))
    x = x_ref[...] + jnp.dot(merged.astype(BF16), wo_ref[...], preferred_element_type=F32)
    xo_ref[...] = x
    h_ref[...] = _rms(x, gn_ref[...]).astype(h_ref.dtype)


def _merge(x, o_fox, o_dsa, y_rwkv, bonus, g_rwkv, ln_g, ln_b, gate_p, b_gate, p_fox, p_dsa, p_rwkv,
           w_out, g_ffn):
    S, D = x.shape
    W = RWKV_W
    tm = min(512, S)
    row = lambda i: (i, 0)
    const = lambda i: (0, 0)
    return pl.pallas_call(
        _merge_kernel,
        grid=(S // tm,),
        in_specs=[pl.BlockSpec((tm, D), row), pl.BlockSpec((tm, FOX_W), row),
                  pl.BlockSpec((tm, DSA_W), row), pl.BlockSpec((tm, W), row),
                  pl.BlockSpec((tm, W), row), pl.BlockSpec((tm, W), row),
                  pl.BlockSpec((1, W), const), pl.BlockSpec((1, W), const),
                  pl.BlockSpec((W, W), const),
                  pl.BlockSpec((tm, 3 * D), row), pl.BlockSpec((1, 3 * D), const),
                  pl.BlockSpec((FOX_W, D), const), pl.BlockSpec((DSA_W, D), const),
                  pl.BlockSpec((W, D), const), pl.BlockSpec((D, D), const),
                  pl.BlockSpec((1, D), const)],
        out_specs=[pl.BlockSpec((tm, D), row), pl.BlockSpec((tm, D), row)],
        out_shape=[jax.ShapeDtypeStruct((S, D), F32), jax.ShapeDtypeStruct((S, D), BF16)],
        compiler_params=_params("arbitrary"),
        name="merge",
    )(x, o_fox, o_dsa, y_rwkv, bonus, g_rwkv, ln_g.reshape(1, W), ln_b.reshape(1, W),
      _block_ones(W, RWKV_HEAD_DIM), gate_p, b_gate.reshape(1, -1), p_fox, p_dsa, p_rwkv, w_out,
      g_ffn.reshape(1, D))


def _swiglu_tile(h, wg, wu):
    gate = jnp.dot(h, wg, preferred_element_type=F32)
    up = jnp.dot(h, wu, preferred_element_type=F32)
    return gate * _sigmoid(gate) * up


def _ffn_kernel(x_ref, h_ref, wg_ref, wu_ref, wd_ref, gf_ref, o_ref, acc_sc, *, final_norm):
    f = pl.program_id(1)

    @pl.when(f == 0)
    def _():
        acc_sc[...] = x_ref[...]

    act = _swiglu_tile(h_ref[...], wg_ref[...], wu_ref[...])
    acc_sc[...] += jnp.dot(act.astype(BF16), wd_ref[...], preferred_element_type=F32)

    @pl.when(f == pl.num_programs(1) - 1)
    def _():
        y = acc_sc[...]
        o_ref[...] = _rms(y, gf_ref[...]) if final_norm else y


def _ffn(x, h, wg, wu, wd, g_final, final_norm):
    S, D = x.shape
    Fd = wg.shape[1]
    tm = min(512, S)
    tf = _pick_tile(Fd, 1408)
    return pl.pallas_call(
        functools.partial(_ffn_kernel, final_norm=final_norm),
        grid=(S // tm, Fd // tf),
        in_specs=[pl.BlockSpec((tm, D), lambda i, f: (i, 0)),
                  pl.BlockSpec((tm, D), lambda i, f: (i, 0)),
                  pl.BlockSpec((D, tf), lambda i, f: (0, f)),
                  pl.BlockSpec((D, tf), lambda i, f: (0, f)),
                  pl.BlockSpec((tf, D), lambda i, f: (f, 0)),
                  pl.BlockSpec((1, D), lambda i, f: (0, 0))],
        out_specs=pl.BlockSpec((tm, D), lambda i, f: (i, 0)),
        out_shape=jax.ShapeDtypeStruct((S, D), F32),
        scratch_shapes=[pltpu.VMEM((tm, D), F32)],
        compiler_params=_params("arbitrary", "arbitrary"),
        name="ffn",
    )(x, h, wg, wu, wd, g_final.reshape(1, D))


MOE_CHUNK = 128


def _moe_kernel(x_ref, h_ref, rw_ref, rb_ref, wg_ref, wu_ref, wd_ref, gf_ref, o_ref,
                acc_sc, gate_sc, rank_sc, cnt_sc, hs_sc, ys_sc, *, final_norm):
    e = pl.program_id(1)
    f = pl.program_id(2)
    last_f = pl.num_programs(2) - 1
    T, D = h_ref.shape
    C = MOE_CHUNK
    lane = lax.broadcasted_iota(I32, (T, LANE), 1)

    @pl.when((e == 0) & (f == 0))
    def _():
        acc_sc[...] = x_ref[...]
        logits = jnp.dot(h_ref[...], rw_ref[...], preferred_element_type=F32) + rb_ref[...]
        logits = jnp.where(lane < N_EXPERTS, logits, -jnp.inf)
        v1 = jnp.max(logits, axis=-1, keepdims=True)
        i1 = jnp.min(jnp.where(logits == v1, lane, LANE), axis=-1, keepdims=True)
        rest = jnp.where(lane == i1, -jnp.inf, logits)
        v2 = jnp.max(rest, axis=-1, keepdims=True)
        i2 = jnp.min(jnp.where(rest == v2, lane, LANE), axis=-1, keepdims=True)
        e2 = jnp.exp(v2 - v1)
        p1 = 1.0 / (1.0 + e2)
        gate_sc[...] = jnp.where(lane == i1, p1, 0.0) + jnp.where(lane == i2, e2 * p1, 0.0)
        routed = ((lane == i1) | (lane == i2)).astype(F32)
        B = 256
        earlier = (lax.broadcasted_iota(I32, (B, B), 0) > lax.broadcasted_iota(I32, (B, B), 1)).astype(BF16)
        seen = jnp.zeros((1, LANE), F32)
        for b in range(T // B):
            blk = routed[b * B:(b + 1) * B]
            before = jnp.dot(earlier, blk.astype(BF16), preferred_element_type=F32) + seen
            rank_sc[b * B:(b + 1) * B, :] = jnp.where(blk > 0.0, before, -1.0)
            seen = seen + jnp.sum(blk, axis=0, keepdims=True)
        cnt_sc[...] = seen

    n_e = jnp.sum(jnp.where(lax.broadcasted_iota(I32, (1, LANE), 1) == e, cnt_sc[...], 0.0))
    n_chunks = sum(jnp.where(n_e > j * C, 1, 0) for j in range(T // C))

    @pl.when(f == 0)
    def _():
        pick8 = (lax.broadcasted_iota(I32, (8, LANE), 1) == e).astype(BF16)
        rk = rank_sc[...]
        rk_hi = rk.astype(BF16)
        rk_lo = (rk - rk_hi.astype(F32)).astype(BF16)
        nt = (((1,), (1,)), ((), ()))
        rrow = (lax.dot_general(pick8, rk_hi, nt, preferred_element_type=F32)
                + lax.dot_general(pick8, rk_lo, nt, preferred_element_type=F32))[0:1, :]

        def gather(c, carry):
            off = pl.multiple_of(c * C, C)
            slot = (off + lax.broadcasted_iota(I32, (C, T), 0)).astype(F32)
            onehot = (rrow == slot).astype(BF16)
            hs_sc[pl.ds(off, C), :] = jnp.dot(onehot, h_ref[...],
                                              preferred_element_type=F32).astype(BF16)
            ys_sc[pl.ds(off, C), :] = jnp.zeros((C, D), F32)
            return carry

        lax.fori_loop(0, n_chunks, gather, 0)

    def expert(c, carry):
        off = pl.multiple_of(c * C, C)
        act = _swiglu_tile(hs_sc[pl.ds(off, C), :], wg_ref[0], wu_ref[0])
        ys_sc[pl.ds(off, C), :] += jnp.dot(act.astype(BF16), wd_ref[0], preferred_element_type=F32)
        return carry

    lax.fori_loop(0, n_chunks, expert, 0)

    @pl.when(f == last_f)
    def _():
        mine = lane == e
        gate = jnp.sum(jnp.where(mine, gate_sc[...], 0.0), axis=-1, keepdims=True)
        rank = jnp.sum(jnp.where(mine, rank_sc[...], 0.0), axis=-1, keepdims=True)
        rank_b = jnp.broadcast_to(rank, (T, C))
        col = lax.broadcasted_iota(I32, (T, C), 1)

        def scatter(c, carry):
            off = pl.multiple_of(c * C, C)
            onehot_t = (rank_b == (col + off).astype(F32)).astype(BF16)
            y = jnp.dot(onehot_t, ys_sc[pl.ds(off, C), :].astype(BF16), preferred_element_type=F32)
            acc_sc[...] += gate * y
            return carry

        lax.fori_loop(0, n_chunks, scatter, 0)

    @pl.when((e == pl.num_programs(1) - 1) & (f == last_f))
    def _():
        y = acc_sc[...]
        o_ref[...] = _rms(y, gf_ref[...]) if final_norm else y


def _moe(x, h, rw, rb, wg, wu, wd, g_final, final_norm):
    S, D = x.shape
    E, _, Fe = wg.shape
    tm = min(1024, S)
    tf = _pick_tile(Fe, 1792)
    once = pl.Buffered(1)
    return pl.pallas_call(
        functools.partial(_moe_kernel, final_norm=final_norm),
        grid=(S // tm, E, Fe // tf),
        in_specs=[pl.BlockSpec((tm, D), lambda i, e, f: (i, 0), pipeline_mode=once),
                  pl.BlockSpec((tm, D), lambda i, e, f: (i, 0), pipeline_mode=once),
                  pl.BlockSpec((D, LANE), lambda i, e, f: (0, 0)),
                  pl.BlockSpec((1, LANE), lambda i, e, f: (0, 0)),
                  pl.BlockSpec((1, D, tf), lambda i, e, f: (e, 0, f)),
                  pl.BlockSpec((1, D, tf), lambda i, e, f: (e, 0, f)),
                  pl.BlockSpec((1, tf, D), lambda i, e, f: (e, f, 0)),
                  pl.BlockSpec((1, D), lambda i, e, f: (0, 0))],
        out_specs=pl.BlockSpec((tm, D), lambda i, e, f: (i, 0)),
        out_shape=jax.ShapeDtypeStruct((S, D), F32),
        scratch_shapes=[pltpu.VMEM((tm, D), F32), pltpu.VMEM((tm, LANE), F32),
                        pltpu.VMEM((tm, LANE), F32), pltpu.VMEM((1, LANE), F32),
                        pltpu.VMEM((tm, D), BF16), pltpu.VMEM((tm, D), F32)],
        compiler_params=_params("arbitrary", "arbitrary", "arbitrary"),
        name="moe",
    )(x, h, rw, rb, wg, wu, wd, g_final.reshape(1, D))


def _cast_kernel(x_ref, o_ref):
    o_ref[...] = x_ref[...].astype(o_ref.dtype)


def _to_bf16(w):
    E, A, B = w.shape
    ta = 256
    return pl.pallas_call(
        _cast_kernel,
        grid=(E, A // ta),
        in_specs=[pl.BlockSpec((1, ta, B), lambda e, a: (e, a, 0))],
        out_specs=pl.BlockSpec((1, ta, B), lambda e, a: (e, a, 0)),
        out_shape=jax.ShapeDtypeStruct(w.shape, BF16),
        compiler_params=_params("arbitrary", "arbitrary"),
        name="to_bf16",
    )(w)


def _pad_cols(a, n):
    return jnp.pad(a, ((0, 0), (0, n - a.shape[1])))


def kernel(x, w_in, b_gate, g_mix, fox_f_bias, dsa_kv_norm, dsa_w_uk, dsa_w_uv, rwkv_mu, rwkv_w0, rwkv_w_up, rwkv_a0, rwkv_a_up, rwkv_g_up, rwkv_k_k, rwkv_k_a, rwkv_r_k, rwkv_ln_g, rwkv_ln_b, vres_down, vres_mu, vres_up, vres_v0, p_fox, p_dsa, p_rwkv, w_out, g_ffn, ffn_w_gate, ffn_w_up, ffn_w_down, router_w, router_b, moe_w_gate, moe_w_up, moe_w_down, g_final):
    B, S, D = x.shape
    assert B == 1 and D == D_MODEL and S % LANE == 0
    depth = w_in.shape[0]
    topk = min(IDX_TOPK, S // 4)
    bf = lambda a: a.astype(BF16)
    xs = x[0]
    v_first = None
    w_in_bf = _to_bf16(w_in)
    for l in range(depth):
        wl = w_in_bf[l]
        w_fox, w_dsa, w_rwkv, w_gate = (wl[:, :FOX_IN], wl[:, FOX_IN:FOX_IN + DSA_IN],
                                        wl[:, FOX_IN + DSA_IN:FOX_IN + DSA_IN + RWKV_IN],
                                        wl[:, FOX_IN + DSA_IN + RWKV_IN:])
        o1 = DSA_W + DSA_KV_RANK + IDX_W
        w_att = jnp.concatenate([_pad_cols(w_fox, 3 * FOX_W + LANE), w_dsa[:, :o1],
                                 _pad_cols(w_dsa[:, o1:], LANE)], axis=1)
        if l > 0:
            w_rwkv = jnp.concatenate([w_rwkv, bf(_pad_cols(vres_down[l - 1], LANE))], axis=1)
        pa = _rms_proj(xs, g_mix[l], w_att)
        pr = _rms_proj(xs, g_mix[l], w_rwkv)
        gate_p = _rms_proj(xs, g_mix[l], w_gate)

        c0 = 3 * FOX_W
        fl = pa[:, c0:c0 + FOX_HEADS].T.reshape(FOX_HEADS, S // LANE, LANE)
        Fh, Fm, Fl = (a.reshape(FOX_HEADS, S) for a in _fox_cumsum(fl, fox_f_bias[l]))
        qT = bf(pa[:, 0:FOX_W].T.reshape(FOX_HEADS, FOX_HEAD_DIM, S) * (FOX_HEAD_DIM ** -0.5 * LOG2E))
        r = jnp.arange(FOX_KA - FOX_HEAD_DIM)
        rq, ck = r[None, :, None], r[None, None, :]
        pick = lambda idx, a, b, c, d: jnp.where(idx == a[0], a[1], jnp.where(
            idx == b[0], b[1], jnp.where(idx == c[0], c[1], jnp.where(idx < 6, d, 0)))).astype(BF16)
        fq = pick(rq, (0, Fh[:, None, :]), (1, Fm[:, None, :]), (2, Fl[:, None, :]), 1)
        fk = pick(ck, (3, -Fh[:, :, None]), (4, -Fm[:, :, None]), (5, -Fl[:, :, None]), 1)
        qaT = jnp.concatenate([qT, fq], axis=1)
        k4 = bf(pa[:, FOX_W:2 * FOX_W].reshape(S, FOX_HEADS, FOX_HEAD_DIM).transpose(1, 0, 2))
        ka = jnp.concatenate([k4, fk], axis=2)
        fvT = bf(pa[:, 2 * FOX_W:3 * FOX_W].T.reshape(FOX_HEADS, FOX_HEAD_DIM, S))
        o_fox = _fox_attention(qaT, ka, fvT).reshape(FOX_W, S).T

        t = min(DSA_T, S)
        nb = S // t
        c1 = c0 + LANE
        dqT = bf(pa[:, c1:c1 + DSA_W].T.reshape(DSA_HEADS, DSA_HEAD_DIM, nb, t).transpose(2, 0, 1, 3))
        c2 = c1 + DSA_W
        ckv = _rmsnorm(pa[:, c2:c2 + DSA_KV_RANK], dsa_kv_norm[l], BF16)
        c3 = c2 + DSA_KV_RANK
        qiT = bf(pa[:, c3:c3 + IDX_W].T.reshape(IDX_HEADS, IDX_HEAD_DIM, nb, t).transpose(2, 1, 0, 3)
                 .reshape(nb, IDX_HEAD_DIM, IDX_HEADS * t))
        c4 = c3 + IDX_W
        ki = bf(pa[:, c4:c4 + IDX_HEAD_DIM])
        c5 = c4 + IDX_HEAD_DIM
        wi = pa[:, c5:c5 + IDX_HEADS].T.reshape(IDX_HEADS, nb, t).transpose(1, 0, 2) * (IDX_W ** -0.5)
        o_dsa = _dsa_attention(qiT, wi, dqT, bf(dsa_w_uk[l].transpose(0, 2, 1)),
                               bf(dsa_w_uv[l].transpose(0, 2, 1)), ki, ckv, ckv.T, topk)
        o_dsa = o_dsa.transpose(1, 2, 0, 3).reshape(DSA_W, S).T

        zw = jnp.zeros((RWKV_W_LORA, RWKV_W), F32)
        wwa = bf(jnp.concatenate([jnp.concatenate([rwkv_w_up[l], zw], axis=1),
                                  jnp.concatenate([zw, rwkv_a_up[l]], axis=1)], axis=0))
        vres = None
        mu = rwkv_mu[l]
        if l > 0:
            vup = jnp.pad(vres_up[l - 1], ((0, LANE - RWKV_V_LORA), (0, 0)))
            vres = (bf(vup), vres_v0[l - 1], v_first)
            mu = jnp.concatenate([mu, jnp.pad(vres_mu[l - 1], (0, LANE - RWKV_V_LORA))])
        *scan_ops, bonus, g_rwkv, v = _rwkv_pre(
            pr, mu, rwkv_w0[l], rwkv_a0[l], wwa, bf(rwkv_g_up[l]),
            rwkv_k_k[l], rwkv_k_a[l], rwkv_r_k[l], vres)
        if l == 0:
            v_first = v
        y_rwkv = _rwkv_scan(*scan_ops)

        xs, h2 = _merge(xs, o_fox, o_dsa, y_rwkv, bonus, g_rwkv, rwkv_ln_g[l], rwkv_ln_b[l],
                        gate_p, b_gate[l], bf(p_fox[l]), bf(p_dsa[l]), bf(p_rwkv[l]), bf(w_out[l]),
                        g_ffn[l])

        last = l == depth - 1
        if l % 2 == 0:
            xs = _ffn(xs, h2, bf(ffn_w_gate[l // 2]), bf(ffn_w_up[l // 2]), bf(ffn_w_down[l // 2]),
                      g_final, last)
        else:
            rw = bf(_pad_cols(router_w[l // 2], LANE))
            rb = _pad_cols(router_b[l // 2].reshape(1, -1), LANE)
            xs = _moe(xs, h2, rw, rb, _to_bf16(moe_w_gate[l // 2]), _to_bf16(moe_w_up[l // 2]),
                      _to_bf16(moe_w_down[l // 2]), g_final, last)
    return xs[None]
```

```python
import functools

import jax
import jax.numpy as jnp
from jax import lax
from jax.experimental import pallas as pl
from jax.experimental.pallas import tpu as pltpu

F32 = jnp.float32
BF16 = jnp.bfloat16
I32 = jnp.int32

D_MODEL = 1024
CHUNK = 64
RMS_EPS = 1e-6
FOX_HEADS, FOX_HEAD_DIM = 4, 64
DSA_HEADS, DSA_HEAD_DIM, DSA_KV_RANK = 4, 64, 128
IDX_HEADS, IDX_HEAD_DIM, IDX_TOPK = 8, 32, 256
RWKV_HEADS, RWKV_HEAD_DIM = 8, 64
RWKV_W_LORA, RWKV_A_LORA, RWKV_V_LORA, RWKV_G_LORA = 64, 64, 32, 128
RWKV_LN_EPS = 64e-5
FOX_W = FOX_HEADS * FOX_HEAD_DIM
DSA_W = DSA_HEADS * DSA_HEAD_DIM
RWKV_W = RWKV_HEADS * RWKV_HEAD_DIM
IDX_W = IDX_HEADS * IDX_HEAD_DIM
N_EXPERTS = 8
FOX_IN = 3 * FOX_W + FOX_HEADS
DSA_IN = DSA_W + DSA_KV_RANK + IDX_W + IDX_HEAD_DIM + IDX_HEADS
RWKV_IN = 3 * RWKV_W + RWKV_W_LORA + RWKV_A_LORA + RWKV_G_LORA

LANE = 128
VMEM_LIMIT = 52 * 1024 * 1024
NEG = -1e30
INT_MIN = -(2 ** 31)
INT_MAX = 2 ** 31 - 1
LOG2E = 1.4426950408889634
F32_MIN_NORMAL = 2.0 ** -126
HI = lax.Precision.HIGHEST


def _params(*sem):
    return pltpu.CompilerParams(dimension_semantics=sem, vmem_limit_bytes=VMEM_LIMIT)


def _pick_tile(n, cap):
    best = LANE
    for t in range(LANE, min(n, cap) + 1, LANE):
        if n % t == 0:
            best = t
    return best


def _softplus(x):
    return jnp.maximum(x, 0.0) + jnp.log1p(jnp.exp(-jnp.abs(x)))


def _sigmoid(x):
    return 1.0 / (1.0 + jnp.exp(-x))


def _rms(x, g):
    return x * lax.rsqrt(jnp.mean(x * x, axis=-1, keepdims=True) + RMS_EPS) * g


def _rms_proj_kernel(x_ref, g_ref, w_ref, o_ref):
    h = _rms(x_ref[...], g_ref[...])
    o_ref[...] = jnp.dot(h.astype(BF16), w_ref[...], preferred_element_type=F32)


def _rms_proj(x, g, w):
    S, D = x.shape
    N = w.shape[1]
    tm = min(512, S)
    tn = _pick_tile(N, 2304)
    return pl.pallas_call(
        _rms_proj_kernel,
        grid=(N // tn, S // tm),
        in_specs=[pl.BlockSpec((tm, D), lambda j, i: (i, 0)),
                  pl.BlockSpec((1, D), lambda j, i: (0, 0)),
                  pl.BlockSpec((D, tn), lambda j, i: (0, j))],
        out_specs=pl.BlockSpec((tm, tn), lambda j, i: (i, j)),
        out_shape=jax.ShapeDtypeStruct((S, N), F32),
        compiler_params=_params("arbitrary", "arbitrary"),
        name="rms_proj",
    )(x, g.reshape(1, D), w)


def _rmsnorm_kernel(x_ref, g_ref, o_ref):
    o_ref[...] = _rms(x_ref[...], g_ref[...]).astype(o_ref.dtype)


def _rmsnorm(x, g, dtype):
    S, D = x.shape
    tm = min(2048, S)
    return pl.pallas_call(
        _rmsnorm_kernel,
        grid=(S // tm,),
        in_specs=[pl.BlockSpec((tm, D), lambda i: (i, 0)),
                  pl.BlockSpec((1, D), lambda i: (0, 0))],
        out_specs=pl.BlockSpec((tm, D), lambda i: (i, 0)),
        out_shape=jax.ShapeDtypeStruct((S, D), dtype),
        compiler_params=_params("arbitrary"),
        name="rmsnorm",
    )(x, g.reshape(1, D))


def _fox_cumsum_kernel(fl_ref, b_ref, hi_ref, mid_ref, lo_ref):
    H, R, _ = fl_ref.shape
    upper = (lax.broadcasted_iota(I32, (LANE, LANE), 0)
             <= lax.broadcasted_iota(I32, (LANE, LANE), 1)).astype(F32)
    strict_lower = (lax.broadcasted_iota(I32, (R, R), 0)
                    > lax.broadcasted_iota(I32, (R, R), 1)).astype(F32)
    for h in range(H):
        log_f = -_softplus(-(fl_ref[h] + b_ref[h]))
        within = jnp.dot(log_f, upper, preferred_element_type=F32, precision=HI)
        row_tot = jnp.broadcast_to(within[:, LANE - 1:LANE], (R, LANE))
        before = jnp.dot(strict_lower, row_tot, preferred_element_type=F32, precision=HI)
        hi, mid, lo = _split3((within + before) * LOG2E)
        hi_ref[h], mid_ref[h], lo_ref[h] = hi, mid, lo


def _fox_cumsum(fl, bias):
    H, R, _ = fl.shape
    return pl.pallas_call(
        _fox_cumsum_kernel,
        out_shape=[jax.ShapeDtypeStruct((H, R, LANE), BF16)] * 3,
        compiler_params=pltpu.CompilerParams(vmem_limit_bytes=VMEM_LIMIT),
        name="fox_cumsum",
    )(fl, jnp.broadcast_to(bias.reshape(H, 1, 1), (H, 1, LANE)))


FOX_KA = 128


def _fox_kernel(qa_ref, ka_ref, vT_ref, o_ref, lga_sc, lgb_sc, m_sc, l_sc, acc_sc, *, t, tc, tg):
    i = pl.program_id(1)
    hb = qa_ref.shape[0]
    last = (i * t) // tg
    chains = [(h, q0) for h in range(hb) for q0 in range(0, t, tc)]

    def logits_into(dst_ref, g):
        off = pl.multiple_of(jnp.minimum(g, last) * tg, tg)
        for n, (h, q0) in enumerate(chains):
            dst_ref[n] = jnp.dot(ka_ref[h, pl.ds(off, tg), :], qa_ref[h, :, q0:q0 + tc],
                                 preferred_element_type=F32)

    def update(lg_ref, g, masked):
        off = pl.multiple_of(g * tg, tg)
        for n, (h, q0) in enumerate(chains):
            s = lg_ref[n]
            if masked:
                kpos = off + lax.broadcasted_iota(I32, (tg, tc), 0)
                qpos = i * t + q0 + lax.broadcasted_iota(I32, (tg, tc), 1)
                s = jnp.where(kpos <= qpos, s, NEG)
            m_prev = m_sc[n]
            m_new = jnp.maximum(m_prev, jnp.max(s, axis=0, keepdims=True))
            alpha = jnp.exp2(m_prev - m_new)
            p = jnp.exp2(s - m_new)
            l_sc[n] = alpha * l_sc[n] + jnp.sum(p, axis=0, keepdims=True)
            acc_sc[n] = alpha * acc_sc[n] + jnp.dot(vT_ref[h, :, pl.ds(off, tg)], p.astype(BF16),
                                                    preferred_element_type=F32)
            m_sc[n] = m_new

    m_sc[...] = jnp.full(m_sc.shape, NEG, F32)
    l_sc[...] = jnp.zeros(l_sc.shape, F32)
    acc_sc[...] = jnp.zeros(acc_sc.shape, F32)
    logits_into(lga_sc, 0)

    def pair(j, c):
        logits_into(lgb_sc, 2 * j + 1)
        update(lga_sc, 2 * j, False)
        logits_into(lga_sc, 2 * j + 2)
        update(lgb_sc, 2 * j + 1, False)
        return c

    lax.fori_loop(0, last // 2, pair, 0)
    tail = 2 * (last // 2)
    logits_into(lgb_sc, tail + 1)
    update(lga_sc, tail, True)

    @pl.when(tail + 1 <= last)
    def _():
        update(lgb_sc, tail + 1, True)

    for n, (h, q0) in enumerate(chains):
        o_ref[h, :, q0:q0 + tc] = (acc_sc[n] / l_sc[n]).astype(o_ref.dtype)


def _fox_attention(qaT, ka, vT):
    H, KA, S = qaT.shape
    Dh = vT.shape[1]
    t = min(512, S)
    tc = min(256, t)
    tg = min(1024, S)
    hb = 2
    nc = hb * (t // tc)
    return pl.pallas_call(
        functools.partial(_fox_kernel, t=t, tc=tc, tg=tg),
        grid=(H // hb, S // t),
        in_specs=[pl.BlockSpec((hb, KA, t), lambda h, i: (h, 0, i)),
                  pl.BlockSpec((hb, S, KA), lambda h, i: (h, 0, 0)),
                  pl.BlockSpec((hb, Dh, S), lambda h, i: (h, 0, 0))],
        out_specs=pl.BlockSpec((hb, Dh, t), lambda h, i: (h, 0, i)),
        out_shape=jax.ShapeDtypeStruct((H, Dh, S), BF16),
        scratch_shapes=[pltpu.VMEM((nc, tg, tc), F32), pltpu.VMEM((nc, tg, tc), F32),
                        pltpu.VMEM((nc, 1, tc), F32), pltpu.VMEM((nc, 1, tc), F32),
                        pltpu.VMEM((nc, Dh, tc), F32)],
        compiler_params=_params("arbitrary", "arbitrary"),
        name="fox_attention",
    )(qaT, ka, vT)


DSA_T = 256


def _dsa_kernel(qiT_ref, w_ref, qT_ref, wukT_ref, wuvT_ref, ki_ref, c_ref, cT_ref, o_ref,
                keys_sc, qlat_sc, m_sc, l_sc, acc_sc, lga_sc, lgb_sc, rela_sc, relb_sc,
                *, t, tg, topk):
    i = pl.program_id(0)
    last = (i * t) // tg
    ng = last + 1

    qiT = qiT_ref[0]
    w = w_ref[0]

    def rel_into(dst_ref, g):
        off = pl.multiple_of(jnp.minimum(g, last) * tg, tg)
        dst_ref[...] = jnp.dot(ki_ref[pl.ds(off, tg), :], qiT, preferred_element_type=F32)

    def score_keys(rel_ref, g, masked):
        off = pl.multiple_of(g * tg, tg)
        sc = jnp.maximum(rel_ref[:, 0:t], 0.0) * w[0:1, :]
        for h in range(1, IDX_HEADS):
            sc = sc + jnp.maximum(rel_ref[:, h * t:(h + 1) * t], 0.0) * w[h:h + 1, :]
        bits = pltpu.bitcast(sc, I32)
        key = bits ^ ((bits >> 31) & 0x7FFFFFFF)
        kpos = off + lax.broadcasted_iota(I32, (tg, t), 0)
        key = jnp.where(jnp.abs(sc) < F32_MIN_NORMAL, -1 - kpos, key)
        if masked:
            qchunk = (i * t + lax.broadcasted_iota(I32, (tg, t), 1)) // CHUNK
            key = jnp.where(kpos // CHUNK <= qchunk, key, INT_MIN)
        keys_sc[pl.ds(off, tg), :] = key

    def fill_pair(j, c):
        rel_into(relb_sc, 2 * j + 1)
        score_keys(rela_sc, 2 * j, False)
        rel_into(rela_sc, 2 * j + 2)
        score_keys(relb_sc, 2 * j + 1, False)
        return c

    rel_into(rela_sc, 0)
    lax.fori_loop(0, last // 2, fill_pair, 0)
    tail = 2 * (last // 2)
    rel_into(relb_sc, tail + 1)
    score_keys(rela_sc, tail, True)

    @pl.when(tail + 1 <= last)
    def _():
        score_keys(relb_sc, tail + 1, True)

    def count_ge(cand):
        rows = 16 * LANE * 8 // t
        cb = jnp.broadcast_to(cand, (rows, t))

        def body(g, acc):
            off = pl.multiple_of(g * tg, tg)
            for k in range(tg // rows):
                acc = acc + (keys_sc[pl.ds(off + k * rows, rows), :] >= cb).astype(F32)
            return acc

        acc = lax.fori_loop(0, ng, body, jnp.zeros((rows, t), F32))
        return jnp.sum(acc, axis=0, keepdims=True)

    qpos = i * t + lax.broadcasted_iota(I32, (1, t), 1)
    n_adm = ((qpos // CHUNK + 1) * CHUNK).astype(F32)
    takes_all = n_adm <= topk

    def row_max(g, acc):
        off = pl.multiple_of(g * tg, tg)
        for k in range(tg // acc.shape[0]):
            acc = jnp.maximum(acc, keys_sc[pl.ds(off + k * acc.shape[0], acc.shape[0]), :])
        return acc

    kmax = jnp.max(lax.fori_loop(0, ng, row_max, jnp.full((16 * LANE * 8 // t, t), INT_MIN, I32)),
                   axis=0, keepdims=True)
    hi = jnp.where(kmax == INT_MAX, kmax, kmax + 1)
    guess = jnp.where(kmax > INT_MIN + 3 * 2 ** 23, kmax - 3 * 2 ** 23, INT_MIN + 1)
    n_guess = count_ge(guess)
    good = n_guess >= topk
    lo = jnp.where(good & jnp.logical_not(takes_all), guess, INT_MIN + 1)
    n_lo = jnp.where(good, n_guess, n_adm)
    hi = jnp.where(good | takes_all, hi, guess)
    steps_per_check = 3

    def unsettled(st):
        lo, hi, n_lo = st
        mid = (lo >> 1) + (hi >> 1) + (lo & hi & 1)
        done = (n_lo == topk) | (mid <= lo) | takes_all
        return jnp.min(jnp.where(done, 1.0, 0.0)) < 0.5

    def narrow(st):
        lo, hi, n_lo = st
        for _ in range(steps_per_check):
            cand = (lo >> 1) + (hi >> 1) + (lo & hi & 1)
            c = count_ge(cand)
            up = (c >= topk) & (cand > lo)
            down = (c < topk) & (cand > lo)
            lo = jnp.where(up, cand, lo)
            n_lo = jnp.where(up, c, n_lo)
            hi = jnp.where(down, cand, hi)
        return lo, hi, n_lo

    thr, _, n_ge = lax.while_loop(unsettled, narrow, (lo, hi, n_lo))
    has_ties = jnp.max(jnp.where((n_ge > topk) & jnp.logical_not(takes_all), 1.0, 0.0)) > 0.0
    thr_b = jnp.broadcast_to(thr, (tg, t))

    for h in range(DSA_HEADS):
        ql = jnp.dot(wukT_ref[h], qT_ref[0, h], preferred_element_type=F32)
        qlat_sc[:, h * t:(h + 1) * t] = (ql * (DSA_HEAD_DIM ** -0.5 * LOG2E)).astype(BF16)
    m_sc[...] = jnp.full(m_sc.shape, NEG, F32)
    l_sc[...] = jnp.zeros(l_sc.shape, F32)
    acc_sc[...] = jnp.zeros(acc_sc.shape, F32)

    def logits_into(dst_ref, g):
        off = pl.multiple_of(jnp.minimum(g, last) * tg, tg)
        dst_ref[...] = jnp.dot(c_ref[pl.ds(off, tg), :], qlat_sc[...],
                               preferred_element_type=F32)

    def attend(lg_ref, g, seen_eq, with_ties, need):
        valid = g <= last
        off = pl.multiple_of(jnp.minimum(g, last) * tg, tg)
        key = keys_sc[pl.ds(off, tg), :]
        if with_ties:
            eq = (key == thr_b) & valid
            eqf = eq.astype(F32)
            strict_lower = (lax.broadcasted_iota(I32, (tg, tg), 0)
                            > lax.broadcasted_iota(I32, (tg, tg), 1)).astype(BF16)
            rank = jnp.dot(strict_lower, eqf.astype(BF16), preferred_element_type=F32) + seen_eq
            sel = (key > thr_b) | (eq & (rank < need))
            seen_eq = seen_eq + jnp.sum(eqf, axis=0, keepdims=True)
        else:
            sel = key >= thr_b
        bias = jnp.where(sel & valid, 0.0, NEG)
        probs, alphas = [], []
        for h in range(DSA_HEADS):
            s = lg_ref[:, h * t:(h + 1) * t] + bias
            m_prev = m_sc[h]
            m_new = jnp.maximum(m_prev, jnp.max(s, axis=0, keepdims=True))
            alpha = jnp.exp2(m_prev - m_new)
            p = jnp.exp2(s - m_new)
            l_sc[h] = alpha * l_sc[h] + jnp.sum(p, axis=0, keepdims=True)
            m_sc[h] = m_new
            probs.append(p.astype(BF16))
            alphas.append(alpha)
        pv = jnp.dot(cT_ref[:, pl.ds(off, tg)], jnp.concatenate(probs, axis=1),
                     preferred_element_type=F32)
        for h in range(DSA_HEADS):
            acc_sc[h] = alphas[h] * acc_sc[h] + pv[:, h * t:(h + 1) * t]
        return seen_eq

    def sweep(with_ties, need=None):
        logits_into(lga_sc, 0)

        def pair(j, seen_eq):
            g0 = 2 * j
            logits_into(lgb_sc, g0 + 1)
            seen_eq = attend(lga_sc, g0, seen_eq, with_ties, need)
            logits_into(lga_sc, g0 + 2)
            return attend(lgb_sc, g0 + 1, seen_eq, with_ties, need)

        lax.fori_loop(0, last // 2 + 1, pair, jnp.zeros((1, t), F32))

    @pl.when(has_ties)
    def _():
        sweep(True, topk - count_ge(thr + 1))

    @pl.when(jnp.logical_not(has_ties))
    def _():
        sweep(False)

    for h in range(DSA_HEADS):
        o_lat = (acc_sc[h] / l_sc[h]).astype(BF16)
        o_ref[0, h] = jnp.dot(wuvT_ref[h], o_lat, preferred_element_type=F32).astype(o_ref.dtype)


def _dsa_attention(qiT, w, qT, wukT, wuvT, ki, c, cT, topk):
    NB, H, Dh, t = qT.shape
    S = NB * t
    assert S < 2 ** 23
    R = DSA_KV_RANK
    tg = min(512, S)
    const2 = lambda i: (0, 0)
    const3 = lambda i: (0, 0, 0)
    return pl.pallas_call(
        functools.partial(_dsa_kernel, t=t, tg=tg, topk=topk),
        grid=(NB,),
        in_specs=[pl.BlockSpec((1, IDX_HEAD_DIM, IDX_HEADS * t), lambda i: (i, 0, 0)),
                  pl.BlockSpec((1, IDX_HEADS, t), lambda i: (i, 0, 0)),
                  pl.BlockSpec((1, H, Dh, t), lambda i: (i, 0, 0, 0)),
                  pl.BlockSpec((H, R, Dh), const3),
                  pl.BlockSpec((H, Dh, R), const3),
                  pl.BlockSpec((S, IDX_HEAD_DIM), const2, pipeline_mode=pl.Buffered(1)),
                  pl.BlockSpec((S, R), const2, pipeline_mode=pl.Buffered(1)),
                  pl.BlockSpec((R, S), const2, pipeline_mode=pl.Buffered(1))],
        out_specs=pl.BlockSpec((1, H, Dh, t), lambda i: (i, 0, 0, 0)),
        out_shape=jax.ShapeDtypeStruct((NB, H, Dh, t), BF16),
        scratch_shapes=[pltpu.VMEM((S, t), I32), pltpu.VMEM((R, H * t), BF16),
                        pltpu.VMEM((H, 1, t), F32), pltpu.VMEM((H, 1, t), F32),
                        pltpu.VMEM((H, R, t), F32),
                        pltpu.VMEM((tg, H * t), F32), pltpu.VMEM((tg, H * t), F32),
                        pltpu.VMEM((tg, IDX_HEADS * t), F32), pltpu.VMEM((tg, IDX_HEADS * t), F32)],
        compiler_params=_params("arbitrary"),
        name="dsa_attention",
    )(qiT, w, qT, wukT, wuvT, ki, c, cT)


def _shift_mix(p, prev_tail, mu, first_block):
    rows = lax.broadcasted_iota(I32, p.shape, 0)
    tail = jnp.where(first_block, 0.0, prev_tail)
    prev = jnp.where(rows == 0, tail, pltpu.roll(p, 1, axis=0))
    return p + (prev - p) * mu


def _split3(f):
    hi = f.astype(BF16)
    r1 = f - hi.astype(F32)
    mid = r1.astype(BF16)
    lo = (r1 - mid.astype(F32)).astype(BF16)
    return hi, mid, lo


def _dot3(x, w01):
    return sum(jnp.dot(part, w01, preferred_element_type=F32) for part in _split3(x))


def _dot3_left(w01, x):
    return sum(jnp.dot(w01, part, preferred_element_type=F32) for part in _split3(x))


def _rwkv_pre_kernel(*refs, with_vres):
    (p_ref, pp_ref, mu_ref, w0_ref, a0_ref, wwa_ref, gup_ref, kk_ref, ka_ref, rk_ref,
     tri_ref, blk_ref, bd_ref) = refs[:13]
    if with_vres:
        vup_ref, vv0_ref, vf_ref = refs[13:16]
    (at_o, rt_o, bt_o, kt_o, bh_o, kh_o, vb_o, pc_o, bonus_o, g_o, v_o) = refs[-11:]
    first = pl.program_id(0) == 0
    W = RWKV_W
    ps = _shift_mix(p_ref[...], pp_ref[7:8, :], mu_ref[...], first)
    r, k, v = ps[:, 0:W], ps[:, W:2 * W], ps[:, 2 * W:3 * W]
    wa = ps[:, 3 * W:3 * W + LANE]
    gl = ps[:, 3 * W + LANE:3 * W + 2 * LANE]
    lane = lax.broadcasted_iota(I32, wa.shape, 1)
    wa = jnp.where(lane < RWKV_W_LORA, jnp.tanh(wa), wa)
    up = jnp.dot(wa.astype(BF16), wwa_ref[...], preferred_element_type=F32)
    log_w = -_softplus(-(w0_ref[...] + up[:, 0:W])) - 0.5
    a = _sigmoid(a0_ref[...] + up[:, W:2 * W])
    g = jnp.dot(_sigmoid(gl).astype(BF16), gup_ref[...], preferred_element_type=F32)
    if with_vres:
        vl = ps[:, RWKV_IN:RWKV_IN + LANE]
        logit = vv0_ref[...] + jnp.dot(vl.astype(BF16), vup_ref[...], preferred_element_type=F32)
        v = v + (vf_ref[...] - v) * _sigmoid(logit)
    k2 = k * (1.0 + (a - 1.0) * ka_ref[...])
    kkr = k * kk_ref[...]
    bd = bd_ref[...]
    kk = kkr * lax.rsqrt(_dot3(kkr * kkr, bd) + 1e-12)
    lw = -jnp.exp(log_w)
    cum = _dot3_left(tri_ref[...], lw)
    cend = _dot3_left(blk_ref[...], lw)
    p_inv = jnp.exp(-cum)
    p_end = jnp.exp(cend - cum)
    beta = kk * a
    at_o[...] = (-kk * jnp.exp(cum - lw)).astype(BF16)
    rt_o[...] = (r * jnp.exp(cum)).astype(BF16)
    bt_o[...] = (beta * p_inv).astype(BF16)
    kt_o[...] = (k2 * p_inv).astype(BF16)
    bh_o[...] = (beta * p_end).astype(BF16)
    kh_o[...] = (k2 * p_end).astype(BF16)
    vb_o[...] = v.astype(BF16)
    pc_o[...] = jnp.exp(cend)
    bonus_o[...] = _dot3(r * k2 * rk_ref[...], bd) * v
    g_o[...] = g
    v_o[...] = v


def _block_ones(n, block, lower_tri=False):
    i = jnp.arange(n)
    m = (i[:, None] // block) == (i[None, :] // block)
    if lower_tri:
        m = m & (i[:, None] >= i[None, :])
    return m.astype(BF16)


def _rwkv_pre(p, mu, w0, a0, wwa, gup, k_k, k_a, r_k, vres=None):
    S, PW = p.shape
    W = RWKV_W
    tm = min(512, S)
    row = lambda i: (i, 0)
    const = lambda i: (0, 0)
    tail = lambda i: (jnp.maximum(i * (tm // 8) - 1, 0), 0)
    vec = lambda a: a.reshape(1, -1)
    args = [p, p, vec(mu), vec(w0), vec(a0), wwa, gup, vec(k_k), vec(k_a), vec(r_k),
            _block_ones(tm, CHUNK, lower_tri=True), _block_ones(tm, CHUNK),
            _block_ones(W, RWKV_HEAD_DIM)]
    specs = [pl.BlockSpec((tm, PW), row), pl.BlockSpec((8, PW), tail),
             pl.BlockSpec((1, PW), const), pl.BlockSpec((1, W), const), pl.BlockSpec((1, W), const),
             pl.BlockSpec(wwa.shape, const), pl.BlockSpec(gup.shape, const),
             pl.BlockSpec((1, W), const), pl.BlockSpec((1, W), const), pl.BlockSpec((1, W), const),
             pl.BlockSpec((tm, tm), const), pl.BlockSpec((tm, tm), const),
             pl.BlockSpec((W, W), const)]
    if vres is not None:
        vup, vv0, v_first = vres
        args += [vup, vec(vv0), v_first]
        specs += [pl.BlockSpec(vup.shape, const), pl.BlockSpec((1, W), const),
                  pl.BlockSpec((tm, W), row)]
    return pl.pallas_call(
        functools.partial(_rwkv_pre_kernel, with_vres=vres is not None),
        grid=(S // tm,),
        in_specs=specs,
        out_specs=[pl.BlockSpec((tm, W), row)] * 11,
        out_shape=[jax.ShapeDtypeStruct((S, W), BF16)] * 7 + [jax.ShapeDtypeStruct((S, W), F32)] * 4,
        compiler_params=_params("arbitrary"),
        name="rwkv_pre",
    )(*args)


def _rwkv_scan_kernel(at_ref, rt_ref, bt_ref, kt_ref, bh_ref, kh_ref, v_ref, pc_ref, y_ref, h_sc,
                      *, chunks):
    C = CHUNK
    N = RWKV_HEAD_DIM

    @pl.when(pl.program_id(0) == 0)
    def _():
        h_sc[...] = jnp.zeros(h_sc.shape, F32)

    ti = lax.broadcasted_iota(I32, (C, C), 0)
    tj = lax.broadcasted_iota(I32, (C, C), 1)
    lower_incl = ti >= tj
    lower_strict = ti > tj
    eye = (ti == tj).astype(F32)
    eye_n = (lax.broadcasted_iota(I32, (N, N), 0) == lax.broadcasted_iota(I32, (N, N), 1)).astype(F32)

    def mm(x, y):
        return jnp.dot(x.astype(BF16), y.astype(BF16), preferred_element_type=F32)

    def mm_nt(x, y):
        return lax.dot_general(x.astype(BF16), y.astype(BF16), (((1,), (1,)), ((), ())),
                               preferred_element_type=F32)

    def mm_tn(x, y):
        return lax.dot_general(x.astype(BF16), y.astype(BF16), (((0,), (0,)), ((), ())),
                               preferred_element_type=F32)

    units = [(c, h) for c in range(chunks) for h in range(RWKV_HEADS)]
    tile = lambda ref, u: ref[u[0] * C:(u[0] + 1) * C, u[1] * N:(u[1] + 1) * N]
    each = lambda fn: {u: fn(u) for u in units}

    At, Rt, Bt, Kt = (each(lambda u, r=ref: tile(r, u)) for ref in (at_ref, rt_ref, bt_ref, kt_ref))
    Bh, Kh, V = (each(lambda u, r=ref: tile(r, u)) for ref in (bh_ref, kh_ref, v_ref))
    AR = each(lambda u: jnp.concatenate([At[u], Rt[u]], axis=0))
    Mb = each(lambda u: mm_nt(AR[u], Bt[u]))
    Mk = each(lambda u: mm_nt(AR[u], Kt[u]))
    Lab = each(lambda u: jnp.where(lower_strict, Mb[u][0:C], 0.0))
    Mrb = each(lambda u: jnp.where(lower_incl, Mb[u][C:2 * C], 0.0))
    Lak = each(lambda u: jnp.where(lower_strict, Mk[u][0:C], 0.0))
    Mrk = each(lambda u: jnp.where(lower_incl, Mk[u][C:2 * C], 0.0))
    T = each(lambda u: eye + Lab[u])
    Lp = Lab
    span = 2
    while span < C:
        Lp = each(lambda u, Lp=Lp: mm(Lp[u], Lp[u]))
        T = each(lambda u, T=T, Lp=Lp: T[u] + mm(Lp[u], T[u]))
        span *= 2
    W1 = each(lambda u: mm(Lak[u], V[u]))
    A2 = each(lambda u: mm(T[u], At[u]))
    U0 = each(lambda u: mm(T[u], W1[u]))
    R2 = each(lambda u: Rt[u].astype(F32) + mm(Mrb[u], A2[u]))
    Y0 = each(lambda u: mm(Mrb[u], U0[u]) + mm(Mrk[u], V[u]))
    G = each(lambda u: eye_n * tile(pc_ref, u)[0:1, :] + mm_tn(Bh[u], A2[u]))
    H0 = each(lambda u: mm_tn(Bh[u], U0[u]) + mm_tn(Kh[u], V[u]))

    H = {h: h_sc[h] for h in range(RWKV_HEADS)}
    for c in range(chunks):
        ys = []
        for h in range(RWKV_HEADS):
            u = (c, h)
            ys.append(mm(R2[u], H[h]) + Y0[u])
            H[h] = mm(G[u], H[h]) + H0[u]
        y_ref[c * C:(c + 1) * C, :] = jnp.concatenate(ys, axis=1)
    for h in range(RWKV_HEADS):
        h_sc[h] = H[h]


def _rwkv_scan(at, rt, bt, kt, bh, kh, vb, pc):
    S, W = at.shape
    chunks = 2 if S % (2 * CHUNK) == 0 else 1
    tb = chunks * CHUNK
    seq = pl.BlockSpec((tb, W), lambda i: (i, 0))
    return pl.pallas_call(
        functools.partial(_rwkv_scan_kernel, chunks=chunks),
        grid=(S // tb,),
        in_specs=[seq] * 8,
        out_specs=seq,
        out_shape=jax.ShapeDtypeStruct((S, W), F32),
        scratch_shapes=[pltpu.VMEM((RWKV_HEADS, RWKV_HEAD_DIM, RWKV_HEAD_DIM), F32)],
        compiler_params=_params("arbitrary"),
        name="rwkv_scan",
    )(at, rt, bt, kt, bh, kh, vb, pc)


def _merge_kernel(x_ref, of_ref, od_ref, y_ref, bonus_ref, g_ref, lng_ref, lnb_ref, bd_ref,
                  gp_ref, bg_ref, pf_ref, pd_ref, pr_ref, wo_ref, gn_ref, xo_ref, h_ref):
    D = D_MODEL
    bd = bd_ref[...]
    inv_n = 1.0 / RWKV_HEAD_DIM
    y = y_ref[...]
    yc = y - _dot3(y, bd) * inv_n
    var = _dot3(yc * yc, bd) * inv_n
    yn = yc * lax.rsqrt(var + RWKV_LN_EPS) * lng_ref[...] + lnb_ref[...]
    o_rwkv = ((yn + bonus_ref[...]) * g_ref[...]).astype(BF16)
    gates = _sigmoid(gp_ref[...] + bg_ref[...])
    merged = (gates[:, 0:D] * jnp.dot(of_ref[...], pf_ref[...], preferred_element_type=F32)
              + gates[:, D:2 * D] * jnp.dot(od_ref[...], pd_ref[...], preferred_element_type=F32)
              + gates[:, 2 * D:3 * D] * jnp.dot(o_rwkv, pr_ref[...], preferred_element_type=F32))
    x = x_ref[...] + jnp.dot(merged.astype(BF16), wo_ref[...], preferred_element_type=F32)
    xo_ref[...] = x
    h_ref[...] = _rms(x, gn_ref[...]).astype(h_ref.dtype)


def _merge(x, o_fox, o_dsa, y_rwkv, bonus, g_rwkv, ln_g, ln_b, gate_p, b_gate, p_fox, p_dsa, p_rwkv,
           w_out, g_ffn):
    S, D = x.shape
    W = RWKV_W
    tm = min(512, S)
    row = lambda i: (i, 0)
    const = lambda i: (0, 0)
    return pl.pallas_call(
        _merge_kernel,
        grid=(S // tm,),
        in_specs=[pl.BlockSpec((tm, D), row), pl.BlockSpec((tm, FOX_W), row),
                  pl.BlockSpec((tm, DSA_W), row), pl.BlockSpec((tm, W), row),
                  pl.BlockSpec((tm, W), row), pl.BlockSpec((tm, W), row),
                  pl.BlockSpec((1, W), const), pl.BlockSpec((1, W), const),
                  pl.BlockSpec((W, W), const),
                  pl.BlockSpec((tm, 3 * D), row), pl.BlockSpec((1, 3 * D), const),
                  pl.BlockSpec((FOX_W, D), const), pl.BlockSpec((DSA_W, D), const),
                  pl.BlockSpec((W, D), const), pl.BlockSpec((D, D), const),
                  pl.BlockSpec((1, D), const)],
        out_specs=[pl.BlockSpec((tm, D), row), pl.BlockSpec((tm, D), row)],
        out_shape=[jax.ShapeDtypeStruct((S, D), F32), jax.ShapeDtypeStruct((S, D), BF16)],
        compiler_params=_params("arbitrary"),
        name="merge",
    )(x, o_fox, o_dsa, y_rwkv, bonus, g_rwkv, ln_g.reshape(1, W), ln_b.reshape(1, W),
      _block_ones(W, RWKV_HEAD_DIM), gate_p, b_gate.reshape(1, -1), p_fox, p_dsa, p_rwkv, w_out,
      g_ffn.reshape(1, D))


def _swiglu_tile(h, wg, wu):
    gate = jnp.dot(h, wg, preferred_element_type=F32)
    up = jnp.dot(h, wu, preferred_element_type=F32)
    return gate * _sigmoid(gate) * up


def _ffn_kernel(x_ref, h_ref, wg_ref, wu_ref, wd_ref, gf_ref, o_ref, acc_sc, *, final_norm):
    f = pl.program_id(1)

    @pl.when(f == 0)
    def _():
        acc_sc[...] = x_ref[...]

    act = _swiglu_tile(h_ref[...], wg_ref[...], wu_ref[...])
    acc_sc[...] += jnp.dot(act.astype(BF16), wd_ref[...], preferred_element_type=F32)

    @pl.when(f == pl.num_programs(1) - 1)
    def _():
        y = acc_sc[...]
        o_ref[...] = _rms(y, gf_ref[...]) if final_norm else y


def _ffn(x, h, wg, wu, wd, g_final, final_norm):
    S, D = x.shape
    Fd = wg.shape[1]
    tm = min(512, S)
    tf = _pick_tile(Fd, 1408)
    return pl.pallas_call(
        functools.partial(_ffn_kernel, final_norm=final_norm),
        grid=(S // tm, Fd // tf),
        in_specs=[pl.BlockSpec((tm, D), lambda i, f: (i, 0)),
                  pl.BlockSpec((tm, D), lambda i, f: (i, 0)),
                  pl.BlockSpec((D, tf), lambda i, f: (0, f)),
                  pl.BlockSpec((D, tf), lambda i, f: (0, f)),
                  pl.BlockSpec((tf, D), lambda i, f: (f, 0)),
                  pl.BlockSpec((1, D), lambda i, f: (0, 0))],
        out_specs=pl.BlockSpec((tm, D), lambda i, f: (i, 0)),
        out_shape=jax.ShapeDtypeStruct((S, D), F32),
        scratch_shapes=[pltpu.VMEM((tm, D), F32)],
        compiler_params=_params("arbitrary", "arbitrary"),
        name="ffn",
    )(x, h, wg, wu, wd, g_final.reshape(1, D))


MOE_CHUNK = 128


def _moe_kernel(x_ref, h_ref, rw_ref, rb_ref, wg_ref, wu_ref, wd_ref, gf_ref, o_ref,
                acc_sc, gate_sc, rank_sc, cnt_sc, hs_sc, ys_sc, *, final_norm):
    e = pl.program_id(1)
    f = pl.program_id(2)
    last_f = pl.num_programs(2) - 1
    T, D = h_ref.shape
    C = MOE_CHUNK
    lane = lax.broadcasted_iota(I32, (T, LANE), 1)

    @pl.when((e == 0) & (f == 0))
    def _():
        acc_sc[...] = x_ref[...]
        logits = jnp.dot(h_ref[...], rw_ref[...], preferred_element_type=F32) + rb_ref[...]
        logits = jnp.where(lane < N_EXPERTS, logits, -jnp.inf)
        v1 = jnp.max(logits, axis=-1, keepdims=True)
        i1 = jnp.min(jnp.where(logits == v1, lane, LANE), axis=-1, keepdims=True)
        rest = jnp.where(lane == i1, -jnp.inf, logits)
        v2 = jnp.max(rest, axis=-1, keepdims=True)
        i2 = jnp.min(jnp.where(rest == v2, lane, LANE), axis=-1, keepdims=True)
        e2 = jnp.exp(v2 - v1)
        p1 = 1.0 / (1.0 + e2)
        gate_sc[...] = jnp.where(lane == i1, p1, 0.0) + jnp.where(lane == i2, e2 * p1, 0.0)
        routed = ((lane == i1) | (lane == i2)).astype(F32)
        B = 256
        earlier = (lax.broadcasted_iota(I32, (B, B), 0) > lax.broadcasted_iota(I32, (B, B), 1)).astype(BF16)
        seen = jnp.zeros((1, LANE), F32)
        for b in range(T // B):
            blk = routed[b * B:(b + 1) * B]
            before = jnp.dot(earlier, blk.astype(BF16), preferred_element_type=F32) + seen
            rank_sc[b * B:(b + 1) * B, :] = jnp.where(blk > 0.0, before, -1.0)
            seen = seen + jnp.sum(blk, axis=0, keepdims=True)
        cnt_sc[...] = seen

    n_e = jnp.sum(jnp.where(lax.broadcasted_iota(I32, (1, LANE), 1) == e, cnt_sc[...], 0.0))
    n_chunks = sum(jnp.where(n_e > j * C, 1, 0) for j in range(T // C))
    C2 = 2 * C
    n_pairs = sum(jnp.where(n_e > j * C2, 1, 0) for j in range(T // C2))

    @pl.when(f == 0)
    def _():
        pick8 = (lax.broadcasted_iota(I32, (8, LANE), 1) == e).astype(BF16)
        rk = rank_sc[...]
        rk_hi = rk.astype(BF16)
        rk_lo = (rk - rk_hi.astype(F32)).astype(BF16)
        nt = (((1,), (1,)), ((), ()))
        rrow = (lax.dot_general(pick8, rk_hi, nt, preferred_element_type=F32)
                + lax.dot_general(pick8, rk_lo, nt, preferred_element_type=F32))[0:1, :]

        def gather(p, carry):
            off = pl.multiple_of(p * C2, C2)
            slot = (off + lax.broadcasted_iota(I32, (C2, T), 0)).astype(F32)
            onehot = (rrow == slot).astype(BF16)
            hs_sc[pl.ds(off, C2), :] = jnp.dot(onehot, h_ref[...],
                                               preferred_element_type=F32).astype(BF16)
            ys_sc[pl.ds(off, C2), :] = jnp.zeros((C2, D), F32)
            return carry

        lax.fori_loop(0, n_pairs, gather, 0)

    def expert(c, carry):
        off = pl.multiple_of(c * C, C)
        act = _swiglu_tile(hs_sc[pl.ds(off, C), :], wg_ref[0], wu_ref[0])
        ys_sc[pl.ds(off, C), :] += jnp.dot(act.astype(BF16), wd_ref[0], preferred_element_type=F32)
        return carry

    lax.fori_loop(0, n_chunks, expert, 0)

    @pl.when(f == last_f)
    def _():
        mine = lane == e
        gate = jnp.sum(jnp.where(mine, gate_sc[...], 0.0), axis=-1, keepdims=True)
        rank = jnp.sum(jnp.where(mine, rank_sc[...], 0.0), axis=-1, keepdims=True)
        rank_b = jnp.broadcast_to(rank, (T, C2))
        col = lax.broadcasted_iota(I32, (T, C2), 1)

        def scatter(p, carry):
            off = pl.multiple_of(p * C2, C2)
            onehot_t = (rank_b == (col + off).astype(F32)).astype(BF16)
            y = jnp.dot(onehot_t, ys_sc[pl.ds(off, C2), :].astype(BF16), preferred_element_type=F32)
            acc_sc[...] += gate * y
            return carry

        lax.fori_loop(0, n_pairs, scatter, 0)

    @pl.when((e == pl.num_programs(1) - 1) & (f == last_f))
    def _():
        y = acc_sc[...]
        o_ref[...] = _rms(y, gf_ref[...]) if final_norm else y


def _moe(x, h, rw, rb, wg, wu, wd, g_final, final_norm):
    S, D = x.shape
    E, _, Fe = wg.shape
    tm = min(1024, S)
    tf = _pick_tile(Fe, 1792)
    once = pl.Buffered(1)
    return pl.pallas_call(
        functools.partial(_moe_kernel, final_norm=final_norm),
        grid=(S // tm, E, Fe // tf),
        in_specs=[pl.BlockSpec((tm, D), lambda i, e, f: (i, 0), pipeline_mode=once),
                  pl.BlockSpec((tm, D), lambda i, e, f: (i, 0), pipeline_mode=once),
                  pl.BlockSpec((D, LANE), lambda i, e, f: (0, 0)),
                  pl.BlockSpec((1, LANE), lambda i, e, f: (0, 0)),
                  pl.BlockSpec((1, D, tf), lambda i, e, f: (e, 0, f)),
                  pl.BlockSpec((1, D, tf), lambda i, e, f: (e, 0, f)),
                  pl.BlockSpec((1, tf, D), lambda i, e, f: (e, f, 0)),
                  pl.BlockSpec((1, D), lambda i, e, f: (0, 0))],
        out_specs=pl.BlockSpec((tm, D), lambda i, e, f: (i, 0)),
        out_shape=jax.ShapeDtypeStruct((S, D), F32),
        scratch_shapes=[pltpu.VMEM((tm, D), F32), pltpu.VMEM((tm, LANE), F32),
                        pltpu.VMEM((tm, LANE), F32), pltpu.VMEM((1, LANE), F32),
                        pltpu.VMEM((tm, D), BF16), pltpu.VMEM((tm, D), F32)],
        compiler_params=_params("arbitrary", "arbitrary", "arbitrary"),
        name="moe",
    )(x, h, rw, rb, wg, wu, wd, g_final.reshape(1, D))


def _cast_kernel(x_ref, o_ref):
    o_ref[...] = x_ref[...].astype(o_ref.dtype)


def _to_bf16(w):
    E, A, B = w.shape
    ta = 256
    return pl.pallas_call(
        _cast_kernel,
        grid=(E, A // ta),
        in_specs=[pl.BlockSpec((1, ta, B), lambda e, a: (e, a, 0))],
        out_specs=pl.BlockSpec((1, ta, B), lambda e, a: (e, a, 0)),
        out_shape=jax.ShapeDtypeStruct(w.shape, BF16),
        compiler_params=_params("arbitrary", "arbitrary"),
        name="to_bf16",
    )(w)


def _pad_cols(a, n):
    return jnp.pad(a, ((0, 0), (0, n - a.shape[1])))


def kernel(x, w_in, b_gate, g_mix, fox_f_bias, dsa_kv_norm, dsa_w_uk, dsa_w_uv, rwkv_mu, rwkv_w0, rwkv_w_up, rwkv_a0, rwkv_a_up, rwkv_g_up, rwkv_k_k, rwkv_k_a, rwkv_r_k, rwkv_ln_g, rwkv_ln_b, vres_down, vres_mu, vres_up, vres_v0, p_fox, p_dsa, p_rwkv, w_out, g_ffn, ffn_w_gate, ffn_w_up, ffn_w_down, router_w, router_b, moe_w_gate, moe_w_up, moe_w_down, g_final):
    B, S, D = x.shape
    assert B == 1 and D == D_MODEL and S % LANE == 0
    depth = w_in.shape[0]
    topk = min(IDX_TOPK, S // 4)
    bf = lambda a: a.astype(BF16)
    xs = x[0]
    v_first = None
    w_in_bf = _to_bf16(w_in)
    for l in range(depth):
        wl = w_in_bf[l]
        w_fox, w_dsa, w_rwkv, w_gate = (wl[:, :FOX_IN], wl[:, FOX_IN:FOX_IN + DSA_IN],
                                        wl[:, FOX_IN + DSA_IN:FOX_IN + DSA_IN + RWKV_IN],
                                        wl[:, FOX_IN + DSA_IN + RWKV_IN:])
        o1 = DSA_W + DSA_KV_RANK + IDX_W
        w_att = jnp.concatenate([_pad_cols(w_fox, 3 * FOX_W + LANE), w_dsa[:, :o1],
                                 _pad_cols(w_dsa[:, o1:], LANE)], axis=1)
        if l > 0:
            w_rwkv = jnp.concatenate([w_rwkv, bf(_pad_cols(vres_down[l - 1], LANE))], axis=1)
        pa = _rms_proj(xs, g_mix[l], w_att)
        pr = _rms_proj(xs, g_mix[l], w_rwkv)
        gate_p = _rms_proj(xs, g_mix[l], w_gate)

        c0 = 3 * FOX_W
        fl = pa[:, c0:c0 + FOX_HEADS].T.reshape(FOX_HEADS, S // LANE, LANE)
        Fh, Fm, Fl = (a.reshape(FOX_HEADS, S) for a in _fox_cumsum(fl, fox_f_bias[l]))
        qT = bf(pa[:, 0:FOX_W].T.reshape(FOX_HEADS, FOX_HEAD_DIM, S) * (FOX_HEAD_DIM ** -0.5 * LOG2E))
        r = jnp.arange(FOX_KA - FOX_HEAD_DIM)
        rq, ck = r[None, :, None], r[None, None, :]
        pick = lambda idx, a, b, c, d: jnp.where(idx == a[0], a[1], jnp.where(
            idx == b[0], b[1], jnp.where(idx == c[0], c[1], jnp.where(idx < 6, d, 0)))).astype(BF16)
        fq = pick(rq, (0, Fh[:, None, :]), (1, Fm[:, None, :]), (2, Fl[:, None, :]), 1)
        fk = pick(ck, (3, -Fh[:, :, None]), (4, -Fm[:, :, None]), (5, -Fl[:, :, None]), 1)
        qaT = jnp.concatenate([qT, fq], axis=1)
        k4 = bf(pa[:, FOX_W:2 * FOX_W].reshape(S, FOX_HEADS, FOX_HEAD_DIM).transpose(1, 0, 2))
        ka = jnp.concatenate([k4, fk], axis=2)
        fvT = bf(pa[:, 2 * FOX_W:3 * FOX_W].T.reshape(FOX_HEADS, FOX_HEAD_DIM, S))
        o_fox = _fox_attention(qaT, ka, fvT).reshape(FOX_W, S).T

        t = min(DSA_T, S)
        nb = S // t
        c1 = c0 + LANE
        dqT = bf(pa[:, c1:c1 + DSA_W].T.reshape(DSA_HEADS, DSA_HEAD_DIM, nb, t).transpose(2, 0, 1, 3))
        c2 = c1 + DSA_W
        ckv = _rmsnorm(pa[:, c2:c2 + DSA_KV_RANK], dsa_kv_norm[l], BF16)
        c3 = c2 + DSA_KV_RANK
        qiT = bf(pa[:, c3:c3 + IDX_W].T.reshape(IDX_HEADS, IDX_HEAD_DIM, nb, t).transpose(2, 1, 0, 3)
                 .reshape(nb, IDX_HEAD_DIM, IDX_HEADS * t))
        c4 = c3 + IDX_W
        ki = bf(pa[:, c4:c4 + IDX_HEAD_DIM])
        c5 = c4 + IDX_HEAD_DIM
        wi = pa[:, c5:c5 + IDX_HEADS].T.reshape(IDX_HEADS, nb, t).transpose(1, 0, 2) * (IDX_W ** -0.5)
        o_dsa = _dsa_attention(qiT, wi, dqT, bf(dsa_w_uk[l].transpose(0, 2, 1)),
                               bf(dsa_w_uv[l].transpose(0, 2, 1)), ki, ckv, ckv.T, topk)
        o_dsa = o_dsa.transpose(1, 2, 0, 3).reshape(DSA_W, S).T

        zw = jnp.zeros((RWKV_W_LORA, RWKV_W), F32)
        wwa = bf(jnp.concatenate([jnp.concatenate([rwkv_w_up[l], zw], axis=1),
                                  jnp.concatenate([zw, rwkv_a_up[l]], axis=1)], axis=0))
        vres = None
        mu = rwkv_mu[l]
        if l > 0:
            vup = jnp.pad(vres_up[l - 1], ((0, LANE - RWKV_V_LORA), (0, 0)))
            vres = (bf(vup), vres_v0[l - 1], v_first)
            mu = jnp.concatenate([mu, jnp.pad(vres_mu[l - 1], (0, LANE - RWKV_V_LORA))])
        *scan_ops, bonus, g_rwkv, v = _rwkv_pre(
            pr, mu, rwkv_w0[l], rwkv_a0[l], wwa, bf(rwkv_g_up[l]),
            rwkv_k_k[l], rwkv_k_a[l], rwkv_r_k[l], vres)
        if l == 0:
            v_first = v
        y_rwkv = _rwkv_scan(*scan_ops)

        xs, h2 = _merge(xs, o_fox, o_dsa, y_rwkv, bonus, g_rwkv, rwkv_ln_g[l], rwkv_ln_b[l],
                        gate_p, b_gate[l], bf(p_fox[l]), bf(p_dsa[l]), bf(p_rwkv[l]), bf(w_out[l]),
                        g_ffn[l])

        last = l == depth - 1
        if l % 2 == 0:
            xs = _ffn(xs, h2, bf(ffn_w_gate[l // 2]), bf(ffn_w_up[l // 2]), bf(ffn_w_down[l // 2]),
                      g_final, last)
        else:
            rw = bf(_pad_cols(router_w[l // 2], LANE))
            rb = _pad_cols(router_b[l // 2].reshape(1, -1), LANE)
            xs = _moe(xs, h2, rw, rb, _to_bf16(moe_w_gate[l // 2]), _to_bf16(moe_w_up[l // 2]),
                      _to_bf16(moe_w_down[l // 2]), g_final, last)
    return xs[None]
```

```python
import functools

import jax
import jax.numpy as jnp
from jax import lax
from jax.experimental import pallas as pl
from jax.experimental.pallas import tpu as pltpu

F32 = jnp.float32
BF16 = jnp.bfloat16
I32 = jnp.int32

D_MODEL = 1024
CHUNK = 64
RMS_EPS = 1e-6
FOX_HEADS, FOX_HEAD_DIM = 4, 64
DSA_HEADS, DSA_HEAD_DIM, DSA_KV_RANK = 4, 64, 128
IDX_HEADS, IDX_HEAD_DIM, IDX_TOPK = 8, 32, 256
RWKV_HEADS, RWKV_HEAD_DIM = 8, 64
RWKV_W_LORA, RWKV_A_LORA, RWKV_V_LORA, RWKV_G_LORA = 64, 64, 32, 128
RWKV_LN_EPS = 64e-5
FOX_W = FOX_HEADS * FOX_HEAD_DIM
DSA_W = DSA_HEADS * DSA_HEAD_DIM
RWKV_W = RWKV_HEADS * RWKV_HEAD_DIM
IDX_W = IDX_HEADS * IDX_HEAD_DIM
N_EXPERTS = 8
FOX_IN = 3 * FOX_W + FOX_HEADS
DSA_IN = DSA_W + DSA_KV_RANK + IDX_W + IDX_HEAD_DIM + IDX_HEADS
RWKV_IN = 3 * RWKV_W + RWKV_W_LORA + RWKV_A_LORA + RWKV_G_LORA

LANE = 128
VMEM_LIMIT = 52 * 1024 * 1024
NEG = -1e30
INT_MIN = -(2 ** 31)
INT_MAX = 2 ** 31 - 1
LOG2E = 1.4426950408889634
F32_MIN_NORMAL = 2.0 ** -126
HI = lax.Precision.HIGHEST


def _params(*sem):
    return pltpu.CompilerParams(dimension_semantics=sem, vmem_limit_bytes=VMEM_LIMIT)


def _pick_tile(n, cap):
    best = LANE
    for t in range(LANE, min(n, cap) + 1, LANE):
        if n % t == 0:
            best = t
    return best


def _softplus(x):
    return jnp.maximum(x, 0.0) + jnp.log1p(jnp.exp(-jnp.abs(x)))


def _sigmoid(x):
    return 1.0 / (1.0 + jnp.exp(-x))


def _rms(x, g):
    return x * lax.rsqrt(jnp.mean(x * x, axis=-1, keepdims=True) + RMS_EPS) * g


def _rms_proj_kernel(x_ref, g_ref, w_ref, o_ref):
    h = _rms(x_ref[...], g_ref[...])
    o_ref[...] = jnp.dot(h.astype(BF16), w_ref[...], preferred_element_type=F32)


def _rms_proj(x, g, w):
    S, D = x.shape
    N = w.shape[1]
    tm = min(512, S)
    tn = _pick_tile(N, 2304)
    return pl.pallas_call(
        _rms_proj_kernel,
        grid=(N // tn, S // tm),
        in_specs=[pl.BlockSpec((tm, D), lambda j, i: (i, 0)),
                  pl.BlockSpec((1, D), lambda j, i: (0, 0)),
                  pl.BlockSpec((D, tn), lambda j, i: (0, j))],
        out_specs=pl.BlockSpec((tm, tn), lambda j, i: (i, j)),
        out_shape=jax.ShapeDtypeStruct((S, N), F32),
        compiler_params=_params("arbitrary", "arbitrary"),
        name="rms_proj",
    )(x, g.reshape(1, D), w)


def _rmsnorm_kernel(x_ref, g_ref, o_ref):
    o_ref[...] = _rms(x_ref[...], g_ref[...]).astype(o_ref.dtype)


def _rmsnorm(x, g, dtype):
    S, D = x.shape
    tm = min(2048, S)
    return pl.pallas_call(
        _rmsnorm_kernel,
        grid=(S // tm,),
        in_specs=[pl.BlockSpec((tm, D), lambda i: (i, 0)),
                  pl.BlockSpec((1, D), lambda i: (0, 0))],
        out_specs=pl.BlockSpec((tm, D), lambda i: (i, 0)),
        out_shape=jax.ShapeDtypeStruct((S, D), dtype),
        compiler_params=_params("arbitrary"),
        name="rmsnorm",
    )(x, g.reshape(1, D))


def _fox_cumsum_kernel(fl_ref, b_ref, hi_ref, mid_ref, lo_ref):
    H, R, _ = fl_ref.shape
    upper = (lax.broadcasted_iota(I32, (LANE, LANE), 0)
             <= lax.broadcasted_iota(I32, (LANE, LANE), 1)).astype(F32)
    strict_lower = (lax.broadcasted_iota(I32, (R, R), 0)
                    > lax.broadcasted_iota(I32, (R, R), 1)).astype(F32)
    for h in range(H):
        log_f = -_softplus(-(fl_ref[h] + b_ref[h]))
        within = jnp.dot(log_f, upper, preferred_element_type=F32, precision=HI)
        row_tot = jnp.broadcast_to(within[:, LANE - 1:LANE], (R, LANE))
        before = jnp.dot(strict_lower, row_tot, preferred_element_type=F32, precision=HI)
        hi, mid, lo = _split3((within + before) * LOG2E)
        hi_ref[h], mid_ref[h], lo_ref[h] = hi, mid, lo


def _fox_cumsum(fl, bias):
    H, R, _ = fl.shape
    return pl.pallas_call(
        _fox_cumsum_kernel,
        out_shape=[jax.ShapeDtypeStruct((H, R, LANE), BF16)] * 3,
        compiler_params=pltpu.CompilerParams(vmem_limit_bytes=VMEM_LIMIT),
        name="fox_cumsum",
    )(fl, jnp.broadcast_to(bias.reshape(H, 1, 1), (H, 1, LANE)))


FOX_KA = 128


def _fox_kernel(qa_ref, ka_ref, vT_ref, o_ref, lga_sc, lgb_sc, m_sc, l_sc, acc_sc, *, t, tc, tg):
    i = pl.program_id(1)
    hb = qa_ref.shape[0]
    last = (i * t) // tg
    chains = [(h, q0) for h in range(hb) for q0 in range(0, t, tc)]

    def logits_into(dst_ref, g):
        off = pl.multiple_of(jnp.minimum(g, last) * tg, tg)
        for n, (h, q0) in enumerate(chains):
            dst_ref[n] = jnp.dot(ka_ref[h, pl.ds(off, tg), :], qa_ref[h, :, q0:q0 + tc],
                                 preferred_element_type=F32)

    def update(lg_ref, g, masked):
        off = pl.multiple_of(g * tg, tg)
        for n, (h, q0) in enumerate(chains):
            s = lg_ref[n]
            if masked:
                kpos = off + lax.broadcasted_iota(I32, (tg, tc), 0)
                qpos = i * t + q0 + lax.broadcasted_iota(I32, (tg, tc), 1)
                s = jnp.where(kpos <= qpos, s, NEG)
            m_prev = m_sc[n]
            m_new = jnp.maximum(m_prev, jnp.max(s, axis=0, keepdims=True))
            alpha = jnp.exp2(m_prev - m_new)
            p = jnp.exp2(s - m_new)
            l_sc[n] = alpha * l_sc[n] + jnp.sum(p, axis=0, keepdims=True)
            acc_sc[n] = alpha * acc_sc[n] + jnp.dot(vT_ref[h, :, pl.ds(off, tg)], p.astype(BF16),
                                                    preferred_element_type=F32)
            m_sc[n] = m_new

    m_sc[...] = jnp.full(m_sc.shape, NEG, F32)
    l_sc[...] = jnp.zeros(l_sc.shape, F32)
    acc_sc[...] = jnp.zeros(acc_sc.shape, F32)
    logits_into(lga_sc, 0)

    def pair(j, c):
        logits_into(lgb_sc, 2 * j + 1)
        update(lga_sc, 2 * j, False)
        logits_into(lga_sc, 2 * j + 2)
        update(lgb_sc, 2 * j + 1, False)
        return c

    lax.fori_loop(0, last // 2, pair, 0)
    tail = 2 * (last // 2)
    logits_into(lgb_sc, tail + 1)
    update(lga_sc, tail, True)

    @pl.when(tail + 1 <= last)
    def _():
        update(lgb_sc, tail + 1, True)

    for n, (h, q0) in enumerate(chains):
        o_ref[h, :, q0:q0 + tc] = (acc_sc[n] / l_sc[n]).astype(o_ref.dtype)


def _fox_attention(qaT, ka, vT):
    H, KA, S = qaT.shape
    Dh = vT.shape[1]
    t = min(512, S)
    tc = min(256, t)
    tg = min(1024, S)
    hb = 2
    nc = hb * (t // tc)
    return pl.pallas_call(
        functools.partial(_fox_kernel, t=t, tc=tc, tg=tg),
        grid=(H // hb, S // t),
        in_specs=[pl.BlockSpec((hb, KA, t), lambda h, i: (h, 0, i)),
                  pl.BlockSpec((hb, S, KA), lambda h, i: (h, 0, 0)),
                  pl.BlockSpec((hb, Dh, S), lambda h, i: (h, 0, 0))],
        out_specs=pl.BlockSpec((hb, Dh, t), lambda h, i: (h, 0, i)),
        out_shape=jax.ShapeDtypeStruct((H, Dh, S), BF16),
        scratch_shapes=[pltpu.VMEM((nc, tg, tc), F32), pltpu.VMEM((nc, tg, tc), F32),
                        pltpu.VMEM((nc, 1, tc), F32), pltpu.VMEM((nc, 1, tc), F32),
                        pltpu.VMEM((nc, Dh, tc), F32)],
        compiler_params=_params("arbitrary", "arbitrary"),
        name="fox_attention",
    )(qaT, ka, vT)


DSA_T = 256


def _dsa_kernel(qiT_ref, w_ref, qT_ref, wukT_ref, wuvT_ref, ki_ref, c_ref, cT_ref, o_ref,
                keys_sc, qlat_sc, m_sc, l_sc, acc_sc, lga_sc, lgb_sc, rela_sc, relb_sc,
                *, t, tg, topk):
    i = pl.program_id(0)
    last = (i * t) // tg
    ng = last + 1

    qiT = qiT_ref[0]
    w = w_ref[0]

    def rel_into(dst_ref, g):
        off = pl.multiple_of(jnp.minimum(g, last) * tg, tg)
        dst_ref[...] = jnp.dot(ki_ref[pl.ds(off, tg), :], qiT, preferred_element_type=F32)

    def score_keys(rel_ref, g, masked):
        off = pl.multiple_of(g * tg, tg)
        sc = jnp.maximum(rel_ref[:, 0:t], 0.0) * w[0:1, :]
        for h in range(1, IDX_HEADS):
            sc = sc + jnp.maximum(rel_ref[:, h * t:(h + 1) * t], 0.0) * w[h:h + 1, :]
        bits = pltpu.bitcast(sc, I32)
        key = bits ^ ((bits >> 31) & 0x7FFFFFFF)
        kpos = off + lax.broadcasted_iota(I32, (tg, t), 0)
        key = jnp.where(jnp.abs(sc) < F32_MIN_NORMAL, -1 - kpos, key)
        if masked:
            qchunk = (i * t + lax.broadcasted_iota(I32, (tg, t), 1)) // CHUNK
            key = jnp.where(kpos // CHUNK <= qchunk, key, INT_MIN)
        keys_sc[pl.ds(off, tg), :] = key

    def fill_pair(j, c):
        rel_into(relb_sc, 2 * j + 1)
        score_keys(rela_sc, 2 * j, False)
        rel_into(rela_sc, 2 * j + 2)
        score_keys(relb_sc, 2 * j + 1, False)
        return c

    rel_into(rela_sc, 0)
    lax.fori_loop(0, last // 2, fill_pair, 0)
    tail = 2 * (last // 2)
    rel_into(relb_sc, tail + 1)
    score_keys(rela_sc, tail, True)

    @pl.when(tail + 1 <= last)
    def _():
        score_keys(relb_sc, tail + 1, True)

    def count_ge(cand):
        rows = 16 * LANE * 8 // t
        cb = jnp.broadcast_to(cand, (rows, t))

        def body(g, acc):
            off = pl.multiple_of(g * tg, tg)
            for k in range(tg // rows):
                acc = acc + (keys_sc[pl.ds(off + k * rows, rows), :] >= cb).astype(F32)
            return acc

        acc = lax.fori_loop(0, ng, body, jnp.zeros((rows, t), F32))
        return jnp.sum(acc, axis=0, keepdims=True)

    qpos = i * t + lax.broadcasted_iota(I32, (1, t), 1)
    n_adm = ((qpos // CHUNK + 1) * CHUNK).astype(F32)
    takes_all = n_adm <= topk

    def row_max(g, acc):
        off = pl.multiple_of(g * tg, tg)
        for k in range(tg // acc.shape[0]):
            acc = jnp.maximum(acc, keys_sc[pl.ds(off + k * acc.shape[0], acc.shape[0]), :])
        return acc

    kmax = jnp.max(lax.fori_loop(0, ng, row_max, jnp.full((16 * LANE * 8 // t, t), INT_MIN, I32)),
                   axis=0, keepdims=True)
    hi = jnp.where(kmax == INT_MAX, kmax, kmax + 1)
    guess = jnp.where(kmax > INT_MIN + 3 * 2 ** 23, kmax - 3 * 2 ** 23, INT_MIN + 1)
    n_guess = count_ge(guess)
    good = n_guess >= topk
    lo = jnp.where(good & jnp.logical_not(takes_all), guess, INT_MIN + 1)
    n_lo = jnp.where(good, n_guess, n_adm)
    n_hi = jnp.where(good | takes_all, 0.0, n_guess)
    hi = jnp.where(good | takes_all, hi, guess)
    steps_per_check = 3
    n_extract = 4

    def settled(lo, hi, n_lo):
        mid = (lo >> 1) + (hi >> 1) + (lo & hi & 1)
        return (n_lo == topk) | (mid <= lo) | takes_all

    def unsettled(st):
        lo, hi, n_lo, _ = st
        return jnp.min(jnp.where(settled(lo, hi, n_lo), 1.0, 0.0)) < 0.5

    def crowded(st):
        lo, hi, n_lo, n_hi = st
        ok = settled(lo, hi, n_lo) | (n_lo - n_hi <= n_extract)
        return jnp.min(jnp.where(ok, 1.0, 0.0)) < 0.5

    def narrow(st):
        lo, hi, n_lo, n_hi = st
        for _ in range(steps_per_check):
            cand = (lo >> 1) + (hi >> 1) + (lo & hi & 1)
            c = count_ge(cand)
            up = (c >= topk) & (cand > lo)
            down = (c < topk) & (cand > lo)
            lo = jnp.where(up, cand, lo)
            n_lo = jnp.where(up, c, n_lo)
            hi = jnp.where(down, cand, hi)
            n_hi = jnp.where(down, c, n_hi)
        return lo, hi, n_lo, n_hi

    lo, hi, n_lo, n_hi = lax.while_loop(crowded, narrow, (lo, hi, n_lo, n_hi))

    def largest_below(cur):
        rows = 16 * LANE * 8 // t
        cb = jnp.broadcast_to(cur, (rows, t))

        def body(g, acc):
            off = pl.multiple_of(g * tg, tg)
            for k in range(tg // rows):
                x = keys_sc[pl.ds(off + k * rows, rows), :]
                acc = jnp.maximum(acc, jnp.where(x < cb, x, INT_MIN))
            return acc

        acc = lax.fori_loop(0, ng, body, jnp.full((rows, t), INT_MIN, I32))
        return jnp.max(acc, axis=0, keepdims=True)

    want = topk - n_hi
    cur, pick = hi, lo
    for r in range(1, n_extract + 1):
        cur = largest_below(cur)
        pick = jnp.where(want == r, cur, pick)
    n_pick = count_ge(pick)
    take = jnp.logical_not(settled(lo, hi, n_lo)) & (pick > lo) & (pick < hi) & (n_pick >= topk)
    lo = jnp.where(take, pick, lo)
    n_lo = jnp.where(take, n_pick, n_lo)
    thr, _, n_ge, _ = lax.while_loop(unsettled, narrow, (lo, hi, n_lo, n_hi))
    has_ties = jnp.max(jnp.where((n_ge > topk) & jnp.logical_not(takes_all), 1.0, 0.0)) > 0.0
    thr_b = jnp.broadcast_to(thr, (tg, t))

    for h in range(DSA_HEADS):
        ql = jnp.dot(wukT_ref[h], qT_ref[0, h], preferred_element_type=F32)
        qlat_sc[:, h * t:(h + 1) * t] = (ql * (DSA_HEAD_DIM ** -0.5 * LOG2E)).astype(BF16)
    m_sc[...] = jnp.full(m_sc.shape, NEG, F32)
    l_sc[...] = jnp.zeros(l_sc.shape, F32)
    acc_sc[...] = jnp.zeros(acc_sc.shape, F32)

    def logits_into(dst_ref, g):
        off = pl.multiple_of(jnp.minimum(g, last) * tg, tg)
        dst_ref[...] = jnp.dot(c_ref[pl.ds(off, tg), :], qlat_sc[...],
                               preferred_element_type=F32)

    def attend(lg_ref, g, seen_eq, with_ties, need):
        valid = g <= last
        off = pl.multiple_of(jnp.minimum(g, last) * tg, tg)
        key = keys_sc[pl.ds(off, tg), :]
        if with_ties:
            eq = (key == thr_b) & valid
            eqf = eq.astype(F32)
            strict_lower = (lax.broadcasted_iota(I32, (tg, tg), 0)
                            > lax.broadcasted_iota(I32, (tg, tg), 1)).astype(BF16)
            rank = jnp.dot(strict_lower, eqf.astype(BF16), preferred_element_type=F32) + seen_eq
            sel = (key > thr_b) | (eq & (rank < need))
            seen_eq = seen_eq + jnp.sum(eqf, axis=0, keepdims=True)
        else:
            sel = key >= thr_b
        bias = jnp.where(sel & valid, 0.0, NEG)
        probs, alphas = [], []
        for h in range(DSA_HEADS):
            s = lg_ref[:, h * t:(h + 1) * t] + bias
            m_prev = m_sc[h]
            m_new = jnp.maximum(m_prev, jnp.max(s, axis=0, keepdims=True))
            alpha = jnp.exp2(m_prev - m_new)
            p = jnp.exp2(s - m_new)
            l_sc[h] = alpha * l_sc[h] + jnp.sum(p, axis=0, keepdims=True)
            m_sc[h] = m_new
            probs.append(p.astype(BF16))
            alphas.append(alpha)
        pv = jnp.dot(cT_ref[:, pl.ds(off, tg)], jnp.concatenate(probs, axis=1),
                     preferred_element_type=F32)
        for h in range(DSA_HEADS):
            acc_sc[h] = alphas[h] * acc_sc[h] + pv[:, h * t:(h + 1) * t]
        return seen_eq

    def sweep(with_ties, need=None):
        logits_into(lga_sc, 0)

        def pair(j, seen_eq):
            g0 = 2 * j
            logits_into(lgb_sc, g0 + 1)
            seen_eq = attend(lga_sc, g0, seen_eq, with_ties, need)
            logits_into(lga_sc, g0 + 2)
            return attend(lgb_sc, g0 + 1, seen_eq, with_ties, need)

        lax.fori_loop(0, last // 2 + 1, pair, jnp.zeros((1, t), F32))

    @pl.when(has_ties)
    def _():
        sweep(True, topk - count_ge(thr + 1))

    @pl.when(jnp.logical_not(has_ties))
    def _():
        sweep(False)

    for h in range(DSA_HEADS):
        o_lat = (acc_sc[h] / l_sc[h]).astype(BF16)
        o_ref[0, h] = jnp.dot(wuvT_ref[h], o_lat, preferred_element_type=F32).astype(o_ref.dtype)


def _dsa_attention(qiT, w, qT, wukT, wuvT, ki, c, cT, topk):
    NB, H, Dh, t = qT.shape
    S = NB * t
    assert S < 2 ** 23
    R = DSA_KV_RANK
    tg = min(512, S)
    const2 = lambda i: (0, 0)
    const3 = lambda i: (0, 0, 0)
    return pl.pallas_call(
        functools.partial(_dsa_kernel, t=t, tg=tg, topk=topk),
        grid=(NB,),
        in_specs=[pl.BlockSpec((1, IDX_HEAD_DIM, IDX_HEADS * t), lambda i: (i, 0, 0)),
                  pl.BlockSpec((1, IDX_HEADS, t), lambda i: (i, 0, 0)),
                  pl.BlockSpec((1, H, Dh, t), lambda i: (i, 0, 0, 0)),
                  pl.BlockSpec((H, R, Dh), const3),
                  pl.BlockSpec((H, Dh, R), const3),
                  pl.BlockSpec((S, IDX_HEAD_DIM), const2, pipeline_mode=pl.Buffered(1)),
                  pl.BlockSpec((S, R), const2, pipeline_mode=pl.Buffered(1)),
                  pl.BlockSpec((R, S), const2, pipeline_mode=pl.Buffered(1))],
        out_specs=pl.BlockSpec((1, H, Dh, t), lambda i: (i, 0, 0, 0)),
        out_shape=jax.ShapeDtypeStruct((NB, H, Dh, t), BF16),
        scratch_shapes=[pltpu.VMEM((S, t), I32), pltpu.VMEM((R, H * t), BF16),
                        pltpu.VMEM((H, 1, t), F32), pltpu.VMEM((H, 1, t), F32),
                        pltpu.VMEM((H, R, t), F32),
                        pltpu.VMEM((tg, H * t), F32), pltpu.VMEM((tg, H * t), F32),
                        pltpu.VMEM((tg, IDX_HEADS * t), F32), pltpu.VMEM((tg, IDX_HEADS * t), F32)],
        compiler_params=_params("arbitrary"),
        name="dsa_attention",
    )(qiT, w, qT, wukT, wuvT, ki, c, cT)


def _shift_mix(p, prev_tail, mu, first_block):
    rows = lax.broadcasted_iota(I32, p.shape, 0)
    tail = jnp.where(first_block, 0.0, prev_tail)
    prev = jnp.where(rows == 0, tail, pltpu.roll(p, 1, axis=0))
    return p + (prev - p) * mu


def _split3(f):
    hi = f.astype(BF16)
    r1 = f - hi.astype(F32)
    mid = r1.astype(BF16)
    lo = (r1 - mid.astype(F32)).astype(BF16)
    return hi, mid, lo


def _dot3(x, w01):
    return sum(jnp.dot(part, w01, preferred_element_type=F32) for part in _split3(x))


def _dot3_left(w01, x):
    return sum(jnp.dot(w01, part, preferred_element_type=F32) for part in _split3(x))


def _rwkv_pre_kernel(*refs, with_vres):
    (p_ref, pp_ref, mu_ref, w0_ref, a0_ref, wwa_ref, gup_ref, kk_ref, ka_ref, rk_ref,
     tri_ref, blk_ref, bd_ref) = refs[:13]
    if with_vres:
        vup_ref, vv0_ref, vf_ref = refs[13:16]
    (at_o, rt_o, bt_o, kt_o, bh_o, kh_o, vb_o, pc_o, bonus_o, g_o, v_o) = refs[-11:]
    first = pl.program_id(0) == 0
    W = RWKV_W
    ps = _shift_mix(p_ref[...], pp_ref[7:8, :], mu_ref[...], first)
    r, k, v = ps[:, 0:W], ps[:, W:2 * W], ps[:, 2 * W:3 * W]
    wa = ps[:, 3 * W:3 * W + LANE]
    gl = ps[:, 3 * W + LANE:3 * W + 2 * LANE]
    lane = lax.broadcasted_iota(I32, wa.shape, 1)
    wa = jnp.where(lane < RWKV_W_LORA, jnp.tanh(wa), wa)
    up = jnp.dot(wa.astype(BF16), wwa_ref[...], preferred_element_type=F32)
    log_w = -_softplus(-(w0_ref[...] + up[:, 0:W])) - 0.5
    a = _sigmoid(a0_ref[...] + up[:, W:2 * W])
    g = jnp.dot(_sigmoid(gl).astype(BF16), gup_ref[...], preferred_element_type=F32)
    if with_vres:
        vl = ps[:, RWKV_IN:RWKV_IN + LANE]
        logit = vv0_ref[...] + jnp.dot(vl.astype(BF16), vup_ref[...], preferred_element_type=F32)
        v = v + (vf_ref[...] - v) * _sigmoid(logit)
    k2 = k * (1.0 + (a - 1.0) * ka_ref[...])
    kkr = k * kk_ref[...]
    bd = bd_ref[...]
    kk = kkr * lax.rsqrt(_dot3(kkr * kkr, bd) + 1e-12)
    lw = -jnp.exp(log_w)
    cum = _dot3_left(tri_ref[...], lw)
    cend = _dot3_left(blk_ref[...], lw)
    p_inv = jnp.exp(-cum)
    p_end = jnp.exp(cend - cum)
    beta = kk * a
    at_o[...] = (-kk * jnp.exp(cum - lw)).astype(BF16)
    rt_o[...] = (r * jnp.exp(cum)).astype(BF16)
    bt_o[...] = (beta * p_inv).astype(BF16)
    kt_o[...] = (k2 * p_inv).astype(BF16)
    bh_o[...] = (beta * p_end).astype(BF16)
    kh_o[...] = (k2 * p_end).astype(BF16)
    vb_o[...] = v.astype(BF16)
    pc_o[...] = jnp.exp(cend)
    bonus_o[...] = _dot3(r * k2 * rk_ref[...], bd) * v
    g_o[...] = g
    v_o[...] = v


def _block_ones(n, block, lower_tri=False):
    i = jnp.arange(n)
    m = (i[:, None] // block) == (i[None, :] // block)
    if lower_tri:
        m = m & (i[:, None] >= i[None, :])
    return m.astype(BF16)


def _rwkv_pre(p, mu, w0, a0, wwa, gup, k_k, k_a, r_k, vres=None):
    S, PW = p.shape
    W = RWKV_W
    tm = min(512, S)
    row = lambda i: (i, 0)
    const = lambda i: (0, 0)
    tail = lambda i: (jnp.maximum(i * (tm // 8) - 1, 0), 0)
    vec = lambda a: a.reshape(1, -1)
    args = [p, p, vec(mu), vec(w0), vec(a0), wwa, gup, vec(k_k), vec(k_a), vec(r_k),
            _block_ones(tm, CHUNK, lower_tri=True), _block_ones(tm, CHUNK),
            _block_ones(W, RWKV_HEAD_DIM)]
    specs = [pl.BlockSpec((tm, PW), row), pl.BlockSpec((8, PW), tail),
             pl.BlockSpec((1, PW), const), pl.BlockSpec((1, W), const), pl.BlockSpec((1, W), const),
             pl.BlockSpec(wwa.shape, const), pl.BlockSpec(gup.shape, const),
             pl.BlockSpec((1, W), const), pl.BlockSpec((1, W), const), pl.BlockSpec((1, W), const),
             pl.BlockSpec((tm, tm), const), pl.BlockSpec((tm, tm), const),
             pl.BlockSpec((W, W), const)]
    if vres is not None:
        vup, vv0, v_first = vres
        args += [vup, vec(vv0), v_first]
        specs += [pl.BlockSpec(vup.shape, const), pl.BlockSpec((1, W), const),
                  pl.BlockSpec((tm, W), row)]
    return pl.pallas_call(
        functools.partial(_rwkv_pre_kernel, with_vres=vres is not None),
        grid=(S // tm,),
        in_specs=specs,
        out_specs=[pl.BlockSpec((tm, W), row)] * 11,
        out_shape=[jax.ShapeDtypeStruct((S, W), BF16)] * 7 + [jax.ShapeDtypeStruct((S, W), F32)] * 4,
        compiler_params=_params("arbitrary"),
        name="rwkv_pre",
    )(*args)


def _rwkv_scan_kernel(at_ref, rt_ref, bt_ref, kt_ref, bh_ref, kh_ref, v_ref, pc_ref, y_ref, h_sc,
                      *, chunks):
    C = CHUNK
    N = RWKV_HEAD_DIM

    @pl.when(pl.program_id(0) == 0)
    def _():
        h_sc[...] = jnp.zeros(h_sc.shape, F32)

    ti = lax.broadcasted_iota(I32, (C, C), 0)
    tj = lax.broadcasted_iota(I32, (C, C), 1)
    lower_incl = ti >= tj
    lower_strict = ti > tj
    eye = (ti == tj).astype(F32)
    eye_n = (lax.broadcasted_iota(I32, (N, N), 0) == lax.broadcasted_iota(I32, (N, N), 1)).astype(F32)

    def mm(x, y):
        return jnp.dot(x.astype(BF16), y.astype(BF16), preferred_element_type=F32)

    def mm_nt(x, y):
        return lax.dot_general(x.astype(BF16), y.astype(BF16), (((1,), (1,)), ((), ())),
                               preferred_element_type=F32)

    def mm_tn(x, y):
        return lax.dot_general(x.astype(BF16), y.astype(BF16), (((0,), (0,)), ((), ())),
                               preferred_element_type=F32)

    units = [(c, h) for c in range(chunks) for h in range(RWKV_HEADS)]
    tile = lambda ref, u: ref[u[0] * C:(u[0] + 1) * C, u[1] * N:(u[1] + 1) * N]
    each = lambda fn: {u: fn(u) for u in units}

    At, Rt, Bt, Kt = (each(lambda u, r=ref: tile(r, u)) for ref in (at_ref, rt_ref, bt_ref, kt_ref))
    Bh, Kh, V = (each(lambda u, r=ref: tile(r, u)) for ref in (bh_ref, kh_ref, v_ref))
    AR = each(lambda u: jnp.concatenate([At[u], Rt[u]], axis=0))
    Mb = each(lambda u: mm_nt(AR[u], Bt[u]))
    Mk = each(lambda u: mm_nt(AR[u], Kt[u]))
    Lab = each(lambda u: jnp.where(lower_strict, Mb[u][0:C], 0.0))
    Mrb = each(lambda u: jnp.where(lower_incl, Mb[u][C:2 * C], 0.0))
    Lak = each(lambda u: jnp.where(lower_strict, Mk[u][0:C], 0.0))
    Mrk = each(lambda u: jnp.where(lower_incl, Mk[u][C:2 * C], 0.0))
    T = each(lambda u: eye + Lab[u])
    Lp = Lab
    span = 2
    while span < C:
        Lp = each(lambda u, Lp=Lp: mm(Lp[u], Lp[u]))
        T = each(lambda u, T=T, Lp=Lp: T[u] + mm(Lp[u], T[u]))
        span *= 2
    W1 = each(lambda u: mm(Lak[u], V[u]))
    A2 = each(lambda u: mm(T[u], At[u]))
    U0 = each(lambda u: mm(T[u], W1[u]))
    R2 = each(lambda u: Rt[u].astype(F32) + mm(Mrb[u], A2[u]))
    Y0 = each(lambda u: mm(Mrb[u], U0[u]) + mm(Mrk[u], V[u]))
    G = each(lambda u: eye_n * tile(pc_ref, u)[0:1, :] + mm_tn(Bh[u], A2[u]))
    H0 = each(lambda u: mm_tn(Bh[u], U0[u]) + mm_tn(Kh[u], V[u]))

    H = {h: h_sc[h] for h in range(RWKV_HEADS)}
    for c in range(chunks):
        ys = []
        for h in range(RWKV_HEADS):
            u = (c, h)
            ys.append(mm(R2[u], H[h]) + Y0[u])
            H[h] = mm(G[u], H[h]) + H0[u]
        y_ref[c * C:(c + 1) * C, :] = jnp.concatenate(ys, axis=1)
    for h in range(RWKV_HEADS):
        h_sc[h] = H[h]


def _rwkv_scan(at, rt, bt, kt, bh, kh, vb, pc):
    S, W = at.shape
    chunks = 2 if S % (2 * CHUNK) == 0 else 1
    tb = chunks * CHUNK
    seq = pl.BlockSpec((tb, W), lambda i: (i, 0))
    return pl.pallas_call(
        functools.partial(_rwkv_scan_kernel, chunks=chunks),
        grid=(S // tb,),
        in_specs=[seq] * 8,
        out_specs=seq,
        out_shape=jax.ShapeDtypeStruct((S, W), F32),
        scratch_shapes=[pltpu.VMEM((RWKV_HEADS, RWKV_HEAD_DIM, RWKV_HEAD_DIM), F32)],
        compiler_params=_params("arbitrary"),
        name="rwkv_scan",
    )(at, rt, bt, kt, bh, kh, vb, pc)


def _merge_kernel(x_ref, of_ref, od_ref, y_ref, bonus_ref, g_ref, lng_ref, lnb_ref, bd_ref,
                  gp_ref, bg_ref, pf_ref, pd_ref, pr_ref, wo_ref, gn_ref, xo_ref, h_ref):
    D = D_MODEL
    bd = bd_ref[...]
    inv_n = 1.0 / RWKV_HEAD_DIM
    y = y_ref[...]
    yc = y - _dot3(y, bd) * inv_n
    var = _dot3(yc * yc, bd) * inv_n
    yn = yc * lax.rsqrt(var + RWKV_LN_EPS) * lng_ref[...] + lnb_ref[...]
    o_rwkv = ((yn + bonus_ref[...]) * g_ref[...]).astype(BF16)
    gates = _sigmoid(gp_ref[...] + bg_ref[...])
    merged = (gates[:, 0:D] * jnp.dot(of_ref[...], pf_ref[...], preferred_element_type=F32)
              + gates[:, D:2 * D] * jnp.dot(od_ref[...], pd_ref[...], preferred_element_type=F32)
              + gates[:, 2 * D:3 * D] * jnp.dot(o_rwkv, pr_ref[...], preferred_element_type=F32))
    x = x_ref[...] + jnp.dot(merged.astype(BF16), wo_ref[...], preferred_element_type=F32)
    xo_ref[...] = x
    h_ref[...] = _rms(x, gn_ref[...]).astype(h_ref.dtype)


def _merge(x, o_fox, o_dsa, y_rwkv, bonus, g_rwkv, ln_g, ln_b, gate_p, b_gate, p_fox, p_dsa, p_rwkv,
           w_out, g_ffn):
    S, D = x.shape
    W = RWKV_W
    tm = min(512, S)
    row = lambda i: (i, 0)
    const = lambda i: (0, 0)
    return pl.pallas_call(
        _merge_kernel,
        grid=(S // tm,),
        in_specs=[pl.BlockSpec((tm, D), row), pl.BlockSpec((tm, FOX_W), row),
                  pl.BlockSpec((tm, DSA_W), row), pl.BlockSpec((tm, W), row),
                  pl.BlockSpec((tm, W), row), pl.BlockSpec((tm, W), row),
                  pl.BlockSpec((1, W), const), pl.BlockSpec((1, W), const),
                  pl.BlockSpec((W, W), const),
                  pl.BlockSpec((tm, 3 * D), row), pl.BlockSpec((1, 3 * D), const),
                  pl.BlockSpec((FOX_W, D), const), pl.BlockSpec((DSA_W, D), const),
                  pl.BlockSpec((W, D), const), pl.BlockSpec((D, D), const),
                  pl.BlockSpec((1, D), const)],
        out_specs=[pl.BlockSpec((tm, D), row), pl.BlockSpec((tm, D), row)],
        out_shape=[jax.ShapeDtypeStruct((S, D), F32), jax.ShapeDtypeStruct((S, D), BF16)],
        compiler_params=_params("arbitrary"),
        name="merge",
    )(x, o_fox, o_dsa, y_rwkv, bonus, g_rwkv, ln_g.reshape(1, W), ln_b.reshape(1, W),
      _block_ones(W, RWKV_HEAD_DIM), gate_p, b_gate.reshape(1, -1), p_fox, p_dsa, p_rwkv, w_out,
      g_ffn.reshape(1, D))


def _swiglu_tile(h, wg, wu):
    gate = jnp.dot(h, wg, preferred_element_type=F32)
    up = jnp.dot(h, wu, preferred_element_type=F32)
    return gate * _sigmoid(gate) * up


def _ffn_kernel(x_ref, h_ref, wg_ref, wu_ref, wd_ref, gf_ref, o_ref, acc_sc, *, final_norm):
    f = pl.program_id(1)

    @pl.when(f == 0)
    def _():
        acc_sc[...] = x_ref[...]

    act = _swiglu_tile(h_ref[...], wg_ref[...], wu_ref[...])
    acc_sc[...] += jnp.dot(act.astype(BF16), wd_ref[...], preferred_element_type=F32)

    @pl.when(f == pl.num_programs(1) - 1)
    def _():
        y = acc_sc[...]
        o_ref[...] = _rms(y, gf_ref[...]) if final_norm else y


def _ffn(x, h, wg, wu, wd, g_final, final_norm):
    S, D = x.shape
    Fd = wg.shape[1]
    tm = min(512, S)
    tf = _pick_tile(Fd, 1408)
    return pl.pallas_call(
        functools.partial(_ffn_kernel, final_norm=final_norm),
        grid=(S // tm, Fd // tf),
        in_specs=[pl.BlockSpec((tm, D), lambda i, f: (i, 0)),
                  pl.BlockSpec((tm, D), lambda i, f: (i, 0)),
                  pl.BlockSpec((D, tf), lambda i, f: (0, f)),
                  pl.BlockSpec((D, tf), lambda i, f: (0, f)),
                  pl.BlockSpec((tf, D), lambda i, f: (f, 0)),
                  pl.BlockSpec((1, D), lambda i, f: (0, 0))],
        out_specs=pl.BlockSpec((tm, D), lambda i, f: (i, 0)),
        out_shape=jax.ShapeDtypeStruct((S, D), F32),
        scratch_shapes=[pltpu.VMEM((tm, D), F32)],
        compiler_params=_params("arbitrary", "arbitrary"),
        name="ffn",
    )(x, h, wg, wu, wd, g_final.reshape(1, D))


MOE_CHUNK = 128


def _moe_kernel(x_ref, h_ref, rw_ref, rb_ref, wg_ref, wu_ref, wd_ref, gf_ref, o_ref,
                acc_sc, gate_sc, rank_sc, cnt_sc, hs_sc, ys_sc, *, final_norm):
    e = pl.program_id(1)
    f = pl.program_id(2)
    last_f = pl.num_programs(2) - 1
    T, D = h_ref.shape
    C = MOE_CHUNK
    lane = lax.broadcasted_iota(I32, (T, LANE), 1)

    @pl.when((e == 0) & (f == 0))
    def _():
        acc_sc[...] = x_ref[...]
        logits = jnp.dot(h_ref[...], rw_ref[...], preferred_element_type=F32) + rb_ref[...]
        logits = jnp.where(lane < N_EXPERTS, logits, -jnp.inf)
        v1 = jnp.max(logits, axis=-1, keepdims=True)
        i1 = jnp.min(jnp.where(logits == v1, lane, LANE), axis=-1, keepdims=True)
        rest = jnp.where(lane == i1, -jnp.inf, logits)
        v2 = jnp.max(rest, axis=-1, keepdims=True)
        i2 = jnp.min(jnp.where(rest == v2, lane, LANE), axis=-1, keepdims=True)
        e2 = jnp.exp(v2 - v1)
        p1 = 1.0 / (1.0 + e2)
        gate_sc[...] = jnp.where(lane == i1, p1, 0.0) + jnp.where(lane == i2, e2 * p1, 0.0)
        routed = ((lane == i1) | (lane == i2)).astype(F32)
        B = 256
        earlier = (lax.broadcasted_iota(I32, (B, B), 0) > lax.broadcasted_iota(I32, (B, B), 1)).astype(BF16)
        seen = jnp.zeros((1, LANE), F32)
        for b in range(T // B):
            blk = routed[b * B:(b + 1) * B]
            before = jnp.dot(earlier, blk.astype(BF16), preferred_element_type=F32) + seen
            rank_sc[b * B:(b + 1) * B, :] = jnp.where(blk > 0.0, before, -1.0)
            seen = seen + jnp.sum(blk, axis=0, keepdims=True)
        cnt_sc[...] = seen

    n_e = jnp.sum(jnp.where(lax.broadcasted_iota(I32, (1, LANE), 1) == e, cnt_sc[...], 0.0))
    n_chunks = sum(jnp.where(n_e > j * C, 1, 0) for j in range(T // C))
    C2 = 2 * C
    n_pairs = sum(jnp.where(n_e > j * C2, 1, 0) for j in range(T // C2))

    @pl.when(f == 0)
    def _():
        pick8 = (lax.broadcasted_iota(I32, (8, LANE), 1) == e).astype(BF16)
        rk = rank_sc[...]
        rk_hi = rk.astype(BF16)
        rk_lo = (rk - rk_hi.astype(F32)).astype(BF16)
        nt = (((1,), (1,)), ((), ()))
        rrow = (lax.dot_general(pick8, rk_hi, nt, preferred_element_type=F32)
                + lax.dot_general(pick8, rk_lo, nt, preferred_element_type=F32))[0:1, :]

        def gather(p, carry):
            off = pl.multiple_of(p * C2, C2)
            slot = (off + lax.broadcasted_iota(I32, (C2, T), 0)).astype(F32)
            onehot = (rrow == slot).astype(BF16)
            hs_sc[pl.ds(off, C2), :] = jnp.dot(onehot, h_ref[...],
                                               preferred_element_type=F32).astype(BF16)
            ys_sc[pl.ds(off, C2), :] = jnp.zeros((C2, D), F32)
            return carry

        lax.fori_loop(0, n_pairs, gather, 0)

    def expert(c, carry):
        off = pl.multiple_of(c * C, C)
        act = _swiglu_tile(hs_sc[pl.ds(off, C), :], wg_ref[0], wu_ref[0])
        ys_sc[pl.ds(off, C), :] += jnp.dot(act.astype(BF16), wd_ref[0], preferred_element_type=F32)
        return carry

    lax.fori_loop(0, n_chunks, expert, 0)

    @pl.when(f == last_f)
    def _():
        mine = lane == e
        gate = jnp.sum(jnp.where(mine, gate_sc[...], 0.0), axis=-1, keepdims=True)
        rank = jnp.sum(jnp.where(mine, rank_sc[...], 0.0), axis=-1, keepdims=True)
        rank_b = jnp.broadcast_to(rank, (T, C2))
        col = lax.broadcasted_iota(I32, (T, C2), 1)

        def scatter(p, carry):
            off = pl.multiple_of(p * C2, C2)
            onehot_t = (rank_b == (col + off).astype(F32)).astype(BF16)
            y = jnp.dot(onehot_t, ys_sc[pl.ds(off, C2), :].astype(BF16), preferred_element_type=F32)
            acc_sc[...] += gate * y
            return carry

        lax.fori_loop(0, n_pairs, scatter, 0)

    @pl.when((e == pl.num_programs(1) - 1) & (f == last_f))
    def _():
        y = acc_sc[...]
        o_ref[...] = _rms(y, gf_ref[...]) if final_norm else y


def _moe(x, h, rw, rb, wg, wu, wd, g_final, final_norm):
    S, D = x.shape
    E, _, Fe = wg.shape
    tm = min(1024, S)
    tf = _pick_tile(Fe, 1792)
    once = pl.Buffered(1)
    return pl.pallas_call(
        functools.partial(_moe_kernel, final_norm=final_norm),
        grid=(S // tm, E, Fe // tf),
        in_specs=[pl.BlockSpec((tm, D), lambda i, e, f: (i, 0), pipeline_mode=once),
                  pl.BlockSpec((tm, D), lambda i, e, f: (i, 0), pipeline_mode=once),
                  pl.BlockSpec((D, LANE), lambda i, e, f: (0, 0)),
                  pl.BlockSpec((1, LANE), lambda i, e, f: (0, 0)),
                  pl.BlockSpec((1, D, tf), lambda i, e, f: (e, 0, f)),
                  pl.BlockSpec((1, D, tf), lambda i, e, f: (e, 0, f)),
                  pl.BlockSpec((1, tf, D), lambda i, e, f: (e, f, 0)),
                  pl.BlockSpec((1, D), lambda i, e, f: (0, 0))],
        out_specs=pl.BlockSpec((tm, D), lambda i, e, f: (i, 0)),
        out_shape=jax.ShapeDtypeStruct((S, D), F32),
        scratch_shapes=[pltpu.VMEM((tm, D), F32), pltpu.VMEM((tm, LANE), F32),
                        pltpu.VMEM((tm, LANE), F32), pltpu.VMEM((1, LANE), F32),
                        pltpu.VMEM((tm, D), BF16), pltpu.VMEM((tm, D), F32)],
        compiler_params=_params("arbitrary", "arbitrary", "arbitrary"),
        name="moe",
    )(x, h, rw, rb, wg, wu, wd, g_final.reshape(1, D))


def _cast_kernel(x_ref, o_ref):
    o_ref[...] = x_ref[...].astype(o_ref.dtype)


def _to_bf16(w):
    E, A, B = w.shape
    ta = 256
    return pl.pallas_call(
        _cast_kernel,
        grid=(E, A // ta),
        in_specs=[pl.BlockSpec((1, ta, B), lambda e, a: (e, a, 0))],
        out_specs=pl.BlockSpec((1, ta, B), lambda e, a: (e, a, 0)),
        out_shape=jax.ShapeDtypeStruct(w.shape, BF16),
        compiler_params=_params("arbitrary", "arbitrary"),
        name="to_bf16",
    )(w)


def _pad_cols(a, n):
    return jnp.pad(a, ((0, 0), (0, n - a.shape[1])))


def kernel(x, w_in, b_gate, g_mix, fox_f_bias, dsa_kv_norm, dsa_w_uk, dsa_w_uv, rwkv_mu, rwkv_w0, rwkv_w_up, rwkv_a0, rwkv_a_up, rwkv_g_up, rwkv_k_k, rwkv_k_a, rwkv_r_k, rwkv_ln_g, rwkv_ln_b, vres_down, vres_mu, vres_up, vres_v0, p_fox, p_dsa, p_rwkv, w_out, g_ffn, ffn_w_gate, ffn_w_up, ffn_w_down, router_w, router_b, moe_w_gate, moe_w_up, moe_w_down, g_final):
    B, S, D = x.shape
    assert B == 1 and D == D_MODEL and S % LANE == 0
    depth = w_in.shape[0]
    topk = min(IDX_TOPK, S // 4)
    bf = lambda a: a.astype(BF16)
    xs = x[0]
    v_first = None
    w_in_bf = _to_bf16(w_in)
    for l in range(depth):
        wl = w_in_bf[l]
        w_fox, w_dsa, w_rwkv, w_gate = (wl[:, :FOX_IN], wl[:, FOX_IN:FOX_IN + DSA_IN],
                                        wl[:, FOX_IN + DSA_IN:FOX_IN + DSA_IN + RWKV_IN],
                                        wl[:, FOX_IN + DSA_IN + RWKV_IN:])
        o1 = DSA_W + DSA_KV_RANK + IDX_W
        w_att = jnp.concatenate([_pad_cols(w_fox, 3 * FOX_W + LANE), w_dsa[:, :o1],
                                 _pad_cols(w_dsa[:, o1:], LANE)], axis=1)
        if l > 0:
            w_rwkv = jnp.concatenate([w_rwkv, bf(_pad_cols(vres_down[l - 1], LANE))], axis=1)
        pa = _rms_proj(xs, g_mix[l], w_att)
        pr = _rms_proj(xs, g_mix[l], w_rwkv)
        gate_p = _rms_proj(xs, g_mix[l], w_gate)

        c0 = 3 * FOX_W
        fl = pa[:, c0:c0 + FOX_HEADS].T.reshape(FOX_HEADS, S // LANE, LANE)
        Fh, Fm, Fl = (a.reshape(FOX_HEADS, S) for a in _fox_cumsum(fl, fox_f_bias[l]))
        qT = bf(pa[:, 0:FOX_W].T.reshape(FOX_HEADS, FOX_HEAD_DIM, S) * (FOX_HEAD_DIM ** -0.5 * LOG2E))
        r = jnp.arange(FOX_KA - FOX_HEAD_DIM)
        rq, ck = r[None, :, None], r[None, None, :]
        pick = lambda idx, a, b, c, d: jnp.where(idx == a[0], a[1], jnp.where(
            idx == b[0], b[1], jnp.where(idx == c[0], c[1], jnp.where(idx < 6, d, 0)))).astype(BF16)
        fq = pick(rq, (0, Fh[:, None, :]), (1, Fm[:, None, :]), (2, Fl[:, None, :]), 1)
        fk = pick(ck, (3, -Fh[:, :, None]), (4, -Fm[:, :, None]), (5, -Fl[:, :, None]), 1)
        qaT = jnp.concatenate([qT, fq], axis=1)
        k4 = bf(pa[:, FOX_W:2 * FOX_W].reshape(S, FOX_HEADS, FOX_HEAD_DIM).transpose(1, 0, 2))
        ka = jnp.concatenate([k4, fk], axis=2)
        fvT = bf(pa[:, 2 * FOX_W:3 * FOX_W].T.reshape(FOX_HEADS, FOX_HEAD_DIM, S))
        o_fox = _fox_attention(qaT, ka, fvT).reshape(FOX_W, S).T

        t = min(DSA_T, S)
        nb = S // t
        c1 = c0 + LANE
        dqT = bf(pa[:, c1:c1 + DSA_W].T.reshape(DSA_HEADS, DSA_HEAD_DIM, nb, t).transpose(2, 0, 1, 3))
        c2 = c1 + DSA_W
        ckv = _rmsnorm(pa[:, c2:c2 + DSA_KV_RANK], dsa_kv_norm[l], BF16)
        c3 = c2 + DSA_KV_RANK
        qiT = bf(pa[:, c3:c3 + IDX_W].T.reshape(IDX_HEADS, IDX_HEAD_DIM, nb, t).transpose(2, 1, 0, 3)
                 .reshape(nb, IDX_HEAD_DIM, IDX_HEADS * t))
        c4 = c3 + IDX_W
        ki = bf(pa[:, c4:c4 + IDX_HEAD_DIM])
        c5 = c4 + IDX_HEAD_DIM
        wi = pa[:, c5:c5 + IDX_HEADS].T.reshape(IDX_HEADS, nb, t).transpose(1, 0, 2) * (IDX_W ** -0.5)
        o_dsa = _dsa_attention(qiT, wi, dqT, bf(dsa_w_uk[l].transpose(0, 2, 1)),
                               bf(dsa_w_uv[l].transpose(0, 2, 1)), ki, ckv, ckv.T, topk)
        o_dsa = o_dsa.transpose(1, 2, 0, 3).reshape(DSA_W, S).T

        zw = jnp.zeros((RWKV_W_LORA, RWKV_W), F32)
        wwa = bf(jnp.concatenate([jnp.concatenate([rwkv_w_up[l], zw], axis=1),
                                  jnp.concatenate([zw, rwkv_a_up[l]], axis=1)], axis=0))
        vres = None
        mu = rwkv_mu[l]
        if l > 0:
            vup = jnp.pad(vres_up[l - 1], ((0, LANE - RWKV_V_LORA), (0, 0)))
            vres = (bf(vup), vres_v0[l - 1], v_first)
            mu = jnp.concatenate([mu, jnp.pad(vres_mu[l - 1], (0, LANE - RWKV_V_LORA))])
        *scan_ops, bonus, g_rwkv, v = _rwkv_pre(
            pr, mu, rwkv_w0[l], rwkv_a0[l], wwa, bf(rwkv_g_up[l]),
            rwkv_k_k[l], rwkv_k_a[l], rwkv_r_k[l], vres)
        if l == 0:
            v_first = v
        y_rwkv = _rwkv_scan(*scan_ops)

        xs, h2 = _merge(xs, o_fox, o_dsa, y_rwkv, bonus, g_rwkv, rwkv_ln_g[l], rwkv_ln_b[l],
                        gate_p, b_gate[l], bf(p_fox[l]), bf(p_dsa[l]), bf(p_rwkv[l]), bf(w_out[l]),
                        g_ffn[l])

        last = l == depth - 1
        if l % 2 == 0:
            xs = _ffn(xs, h2, bf(ffn_w_gate[l // 2]), bf(ffn_w_up[l // 2]), bf(ffn_w_down[l // 2]),
                      g_final, last)
        else:
            rw = bf(_pad_cols(router_w[l // 2], LANE))
            rb = _pad_cols(router_b[l // 2].reshape(1, -1), LANE)
            xs = _moe(xs, h2, rw, rb, _to_bf16(moe_w_gate[l // 2]), _to_bf16(moe_w_up[l // 2]),
                      _to_bf16(moe_w_down[l // 2]), g_final, last)
    return xs[None]
```

```python
import functools

import jax
import jax.numpy as jnp
from jax import lax
from jax.experimental import pallas as pl
from jax.experimental.pallas import tpu as pltpu

F32 = jnp.float32
BF16 = jnp.bfloat16
I32 = jnp.int32

D_MODEL = 1024
CHUNK = 64
RMS_EPS = 1e-6
FOX_HEADS, FOX_HEAD_DIM = 4, 64
DSA_HEADS, DSA_HEAD_DIM, DSA_KV_RANK = 4, 64, 128
IDX_HEADS, IDX_HEAD_DIM, IDX_TOPK = 8, 32, 256
RWKV_HEADS, RWKV_HEAD_DIM = 8, 64
RWKV_W_LORA, RWKV_A_LORA, RWKV_V_LORA, RWKV_G_LORA = 64, 64, 32, 128
RWKV_LN_EPS = 64e-5
FOX_W = FOX_HEADS * FOX_HEAD_DIM
DSA_W = DSA_HEADS * DSA_HEAD_DIM
RWKV_W = RWKV_HEADS * RWKV_HEAD_DIM
IDX_W = IDX_HEADS * IDX_HEAD_DIM
N_EXPERTS = 8
FOX_IN = 3 * FOX_W + FOX_HEADS
DSA_IN = DSA_W + DSA_KV_RANK + IDX_W + IDX_HEAD_DIM + IDX_HEADS
RWKV_IN = 3 * RWKV_W + RWKV_W_LORA + RWKV_A_LORA + RWKV_G_LORA

LANE = 128
VMEM_LIMIT = 52 * 1024 * 1024
NEG = -1e30
INT_MIN = -(2 ** 31)
INT_MAX = 2 ** 31 - 1
LOG2E = 1.4426950408889634
F32_MIN_NORMAL = 2.0 ** -126
HI = lax.Precision.HIGHEST


def _params(*sem):
    return pltpu.CompilerParams(dimension_semantics=sem, vmem_limit_bytes=VMEM_LIMIT)


def _pick_tile(n, cap):
    best = LANE
    for t in range(LANE, min(n, cap) + 1, LANE):
        if n % t == 0:
            best = t
    return best


def _softplus(x):
    return jnp.maximum(x, 0.0) + jnp.log1p(jnp.exp(-jnp.abs(x)))


def _sigmoid(x):
    return 1.0 / (1.0 + jnp.exp(-x))


def _rms(x, g):
    return x * lax.rsqrt(jnp.mean(x * x, axis=-1, keepdims=True) + RMS_EPS) * g


def _rms_proj_kernel(x_ref, g_ref, w_ref, o_ref):
    h = _rms(x_ref[...], g_ref[...])
    o_ref[...] = jnp.dot(h.astype(BF16), w_ref[...], preferred_element_type=F32)


def _rms_proj(x, g, w):
    S, D = x.shape
    N = w.shape[1]
    tm = min(512, S)
    tn = _pick_tile(N, 2304)
    return pl.pallas_call(
        _rms_proj_kernel,
        grid=(N // tn, S // tm),
        in_specs=[pl.BlockSpec((tm, D), lambda j, i: (i, 0)),
                  pl.BlockSpec((1, D), lambda j, i: (0, 0)),
                  pl.BlockSpec((D, tn), lambda j, i: (0, j))],
        out_specs=pl.BlockSpec((tm, tn), lambda j, i: (i, j)),
        out_shape=jax.ShapeDtypeStruct((S, N), F32),
        compiler_params=_params("arbitrary", "arbitrary"),
        name="rms_proj",
    )(x, g.reshape(1, D), w)


def _rmsnorm_kernel(x_ref, g_ref, o_ref):
    o_ref[...] = _rms(x_ref[...], g_ref[...]).astype(o_ref.dtype)


def _rmsnorm(x, g, dtype):
    S, D = x.shape
    tm = min(2048, S)
    return pl.pallas_call(
        _rmsnorm_kernel,
        grid=(S // tm,),
        in_specs=[pl.BlockSpec((tm, D), lambda i: (i, 0)),
                  pl.BlockSpec((1, D), lambda i: (0, 0))],
        out_specs=pl.BlockSpec((tm, D), lambda i: (i, 0)),
        out_shape=jax.ShapeDtypeStruct((S, D), dtype),
        compiler_params=_params("arbitrary"),
        name="rmsnorm",
    )(x, g.reshape(1, D))


def _fox_cumsum_kernel(fl_ref, b_ref, hi_ref, mid_ref, lo_ref):
    H, R, _ = fl_ref.shape
    upper = (lax.broadcasted_iota(I32, (LANE, LANE), 0)
             <= lax.broadcasted_iota(I32, (LANE, LANE), 1)).astype(F32)
    strict_lower = (lax.broadcasted_iota(I32, (R, R), 0)
                    > lax.broadcasted_iota(I32, (R, R), 1)).astype(F32)
    for h in range(H):
        log_f = -_softplus(-(fl_ref[h] + b_ref[h]))
        within = jnp.dot(log_f, upper, preferred_element_type=F32, precision=HI)
        row_tot = jnp.broadcast_to(within[:, LANE - 1:LANE], (R, LANE))
        before = jnp.dot(strict_lower, row_tot, preferred_element_type=F32, precision=HI)
        hi, mid, lo = _split3((within + before) * LOG2E)
        hi_ref[h], mid_ref[h], lo_ref[h] = hi, mid, lo


def _fox_cumsum(fl, bias):
    H, R, _ = fl.shape
    return pl.pallas_call(
        _fox_cumsum_kernel,
        out_shape=[jax.ShapeDtypeStruct((H, R, LANE), BF16)] * 3,
        compiler_params=pltpu.CompilerParams(vmem_limit_bytes=VMEM_LIMIT),
        name="fox_cumsum",
    )(fl, jnp.broadcast_to(bias.reshape(H, 1, 1), (H, 1, LANE)))


FOX_KA = 128


def _fox_kernel(qa_ref, ka_ref, vT_ref, o_ref, lga_sc, lgb_sc, m_sc, l_sc, acc_sc, *, t, tc, tg):
    i = pl.program_id(1)
    hb = qa_ref.shape[0]
    last = (i * t) // tg
    chains = [(h, q0) for h in range(hb) for q0 in range(0, t, tc)]

    def logits_into(dst_ref, g):
        off = pl.multiple_of(jnp.minimum(g, last) * tg, tg)
        for n, (h, q0) in enumerate(chains):
            dst_ref[n] = jnp.dot(ka_ref[h, pl.ds(off, tg), :], qa_ref[h, :, q0:q0 + tc],
                                 preferred_element_type=F32)

    def update(lg_ref, g, masked):
        off = pl.multiple_of(g * tg, tg)
        for n, (h, q0) in enumerate(chains):
            s = lg_ref[n]
            if masked:
                kpos = off + lax.broadcasted_iota(I32, (tg, tc), 0)
                qpos = i * t + q0 + lax.broadcasted_iota(I32, (tg, tc), 1)
                s = jnp.where(kpos <= qpos, s, NEG)
            m_prev = m_sc[n]
            m_new = jnp.maximum(m_prev, jnp.max(s, axis=0, keepdims=True))
            alpha = jnp.exp2(m_prev - m_new)
            p = jnp.exp2(s - m_new)
            l_sc[n] = alpha * l_sc[n] + jnp.sum(p, axis=0, keepdims=True)
            acc_sc[n] = alpha * acc_sc[n] + jnp.dot(vT_ref[h, :, pl.ds(off, tg)], p.astype(BF16),
                                                    preferred_element_type=F32)
            m_sc[n] = m_new

    m_sc[...] = jnp.full(m_sc.shape, NEG, F32)
    l_sc[...] = jnp.zeros(l_sc.shape, F32)
    acc_sc[...] = jnp.zeros(acc_sc.shape, F32)
    logits_into(lga_sc, 0)

    def pair(j, c):
        logits_into(lgb_sc, 2 * j + 1)
        update(lga_sc, 2 * j, False)
        logits_into(lga_sc, 2 * j + 2)
        update(lgb_sc, 2 * j + 1, False)
        return c

    lax.fori_loop(0, last // 2, pair, 0)
    tail = 2 * (last // 2)
    logits_into(lgb_sc, tail + 1)
    update(lga_sc, tail, True)

    @pl.when(tail + 1 <= last)
    def _():
        update(lgb_sc, tail + 1, True)

    for n, (h, q0) in enumerate(chains):
        o_ref[h, :, q0:q0 + tc] = (acc_sc[n] / l_sc[n]).astype(o_ref.dtype)


def _fox_attention(qaT, ka, vT):
    H, KA, S = qaT.shape
    Dh = vT.shape[1]
    t = min(512, S)
    tc = min(256, t)
    tg = min(1024, S)
    hb = 2
    nc = hb * (t // tc)
    return pl.pallas_call(
        functools.partial(_fox_kernel, t=t, tc=tc, tg=tg),
        grid=(H // hb, S // t),
        in_specs=[pl.BlockSpec((hb, KA, t), lambda h, i: (h, 0, i)),
                  pl.BlockSpec((hb, S, KA), lambda h, i: (h, 0, 0)),
                  pl.BlockSpec((hb, Dh, S), lambda h, i: (h, 0, 0))],
        out_specs=pl.BlockSpec((hb, Dh, t), lambda h, i: (h, 0, i)),
        out_shape=jax.ShapeDtypeStruct((H, Dh, S), BF16),
        scratch_shapes=[pltpu.VMEM((nc, tg, tc), F32), pltpu.VMEM((nc, tg, tc), F32),
                        pltpu.VMEM((nc, 1, tc), F32), pltpu.VMEM((nc, 1, tc), F32),
                        pltpu.VMEM((nc, Dh, tc), F32)],
        compiler_params=_params("arbitrary", "arbitrary"),
        name="fox_attention",
    )(qaT, ka, vT)


DSA_T = 256


def _dsa_kernel(qiT_ref, w_ref, qT_ref, wukT_ref, wuvT_ref, ki_ref, c_ref, cT_ref, o_ref,
                keys_sc, qlat_sc, m_sc, l_sc, acc_sc, lga_sc, lgb_sc, rela_sc, relb_sc,
                *, t, tg, topk):
    i = pl.program_id(0)
    last = (i * t) // tg
    ng = last + 1

    qiT = qiT_ref[0]
    w = w_ref[0]

    def rel_into(dst_ref, g):
        off = pl.multiple_of(jnp.minimum(g, last) * tg, tg)
        dst_ref[...] = jnp.dot(ki_ref[pl.ds(off, tg), :], qiT, preferred_element_type=F32)

    def score_keys(rel_ref, g, masked):
        off = pl.multiple_of(g * tg, tg)
        sc = jnp.maximum(rel_ref[:, 0:t], 0.0) * w[0:1, :]
        for h in range(1, IDX_HEADS):
            sc = sc + jnp.maximum(rel_ref[:, h * t:(h + 1) * t], 0.0) * w[h:h + 1, :]
        bits = pltpu.bitcast(sc, I32)
        key = bits ^ ((bits >> 31) & 0x7FFFFFFF)
        kpos = off + lax.broadcasted_iota(I32, (tg, t), 0)
        key = jnp.where(jnp.abs(sc) < F32_MIN_NORMAL, -1 - kpos, key)
        if masked:
            qchunk = (i * t + lax.broadcasted_iota(I32, (tg, t), 1)) // CHUNK
            key = jnp.where(kpos // CHUNK <= qchunk, key, INT_MIN)
        keys_sc[pl.ds(off, tg), :] = key

    def fill_pair(j, c):
        rel_into(relb_sc, 2 * j + 1)
        score_keys(rela_sc, 2 * j, False)
        rel_into(rela_sc, 2 * j + 2)
        score_keys(relb_sc, 2 * j + 1, False)
        return c

    rel_into(rela_sc, 0)
    lax.fori_loop(0, last // 2, fill_pair, 0)
    tail = 2 * (last // 2)
    rel_into(relb_sc, tail + 1)
    score_keys(rela_sc, tail, True)

    @pl.when(tail + 1 <= last)
    def _():
        score_keys(relb_sc, tail + 1, True)

    def count_ge(cand):
        rows = 16 * LANE * 8 // t
        cb = jnp.broadcast_to(cand, (rows, t))

        def body(g, acc):
            off = pl.multiple_of(g * tg, tg)
            for k in range(tg // rows):
                acc = acc + (keys_sc[pl.ds(off + k * rows, rows), :] >= cb).astype(F32)
            return acc

        acc = lax.fori_loop(0, ng, body, jnp.zeros((rows, t), F32))
        return jnp.sum(acc, axis=0, keepdims=True)

    qpos = i * t + lax.broadcasted_iota(I32, (1, t), 1)
    n_adm = ((qpos // CHUNK + 1) * CHUNK).astype(F32)
    takes_all = n_adm <= topk

    def row_max(g, acc):
        off = pl.multiple_of(g * tg, tg)
        for k in range(tg // acc.shape[0]):
            acc = jnp.maximum(acc, keys_sc[pl.ds(off + k * acc.shape[0], acc.shape[0]), :])
        return acc

    kmax = jnp.max(lax.fori_loop(0, ng, row_max, jnp.full((16 * LANE * 8 // t, t), INT_MIN, I32)),
                   axis=0, keepdims=True)
    hi = jnp.where(kmax == INT_MAX, kmax, kmax + 1)
    guess = jnp.where(kmax > INT_MIN + 3 * 2 ** 23, kmax - 3 * 2 ** 23, INT_MIN + 1)
    n_guess = count_ge(guess)
    good = n_guess >= topk
    lo = jnp.where(good & jnp.logical_not(takes_all), guess, INT_MIN + 1)
    n_lo = jnp.where(good, n_guess, n_adm)
    hi = jnp.where(good | takes_all, hi, guess)
    steps_per_check = 3

    def unsettled(st):
        lo, hi, n_lo = st
        mid = (lo >> 1) + (hi >> 1) + (lo & hi & 1)
        done = (n_lo == topk) | (mid <= lo) | takes_all
        return jnp.min(jnp.where(done, 1.0, 0.0)) < 0.5

    def narrow(st):
        lo, hi, n_lo = st
        for _ in range(steps_per_check):
            cand = (lo >> 1) + (hi >> 1) + (lo & hi & 1)
            c = count_ge(cand)
            up = (c >= topk) & (cand > lo)
            down = (c < topk) & (cand > lo)
            lo = jnp.where(up, cand, lo)
            n_lo = jnp.where(up, c, n_lo)
            hi = jnp.where(down, cand, hi)
        return lo, hi, n_lo

    thr, _, n_ge = lax.while_loop(unsettled, narrow, (lo, hi, n_lo))
    has_ties = jnp.max(jnp.where((n_ge > topk) & jnp.logical_not(takes_all), 1.0, 0.0)) > 0.0
    thr_b = jnp.broadcast_to(thr, (tg, t))

    for h in range(DSA_HEADS):
        ql = jnp.dot(wukT_ref[h], qT_ref[0, h], preferred_element_type=F32)
        qlat_sc[:, h * t:(h + 1) * t] = (ql * (DSA_HEAD_DIM ** -0.5 * LOG2E)).astype(BF16)
    m_sc[...] = jnp.full(m_sc.shape, NEG, F32)
    l_sc[...] = jnp.zeros(l_sc.shape, F32)
    acc_sc[...] = jnp.zeros(acc_sc.shape, F32)

    def logits_into(dst_ref, g):
        off = pl.multiple_of(jnp.minimum(g, last) * tg, tg)
        dst_ref[...] = jnp.dot(c_ref[pl.ds(off, tg), :], qlat_sc[...],
                               preferred_element_type=F32)

    def attend(lg_ref, g, seen_eq, with_ties, need):
        valid = g <= last
        off = pl.multiple_of(jnp.minimum(g, last) * tg, tg)
        key = keys_sc[pl.ds(off, tg), :]
        if with_ties:
            eq = (key == thr_b) & valid
            eqf = eq.astype(F32)
            strict_lower = (lax.broadcasted_iota(I32, (tg, tg), 0)
                            > lax.broadcasted_iota(I32, (tg, tg), 1)).astype(BF16)
            rank = jnp.dot(strict_lower, eqf.astype(BF16), preferred_element_type=F32) + seen_eq
            sel = (key > thr_b) | (eq & (rank < need))
            seen_eq = seen_eq + jnp.sum(eqf, axis=0, keepdims=True)
        else:
            sel = key >= thr_b
        bias = jnp.where(sel & valid, 0.0, NEG)
        probs, alphas = [], []
        for h in range(DSA_HEADS):
            s = lg_ref[:, h * t:(h + 1) * t] + bias
            m_prev = m_sc[h]
            m_new = jnp.maximum(m_prev, jnp.max(s, axis=0, keepdims=True))
            alpha = jnp.exp2(m_prev - m_new)
            p = jnp.exp2(s - m_new)
            l_sc[h] = alpha * l_sc[h] + jnp.sum(p, axis=0, keepdims=True)
            m_sc[h] = m_new
            probs.append(p.astype(BF16))
            alphas.append(alpha)
        pv = jnp.dot(cT_ref[:, pl.ds(off, tg)], jnp.concatenate(probs, axis=1),
                     preferred_element_type=F32)
        for h in range(DSA_HEADS):
            acc_sc[h] = alphas[h] * acc_sc[h] + pv[:, h * t:(h + 1) * t]
        return seen_eq

    def sweep(with_ties, need=None):
        logits_into(lga_sc, 0)

        def pair(j, seen_eq):
            g0 = 2 * j
            logits_into(lgb_sc, g0 + 1)
            seen_eq = attend(lga_sc, g0, seen_eq, with_ties, need)
            logits_into(lga_sc, g0 + 2)
            return attend(lgb_sc, g0 + 1, seen_eq, with_ties, need)

        lax.fori_loop(0, last // 2 + 1, pair, jnp.zeros((1, t), F32))

    @pl.when(has_ties)
    def _():
        sweep(True, topk - count_ge(thr + 1))

    @pl.when(jnp.logical_not(has_ties))
    def _():
        sweep(False)

    for h in range(DSA_HEADS):
        o_lat = (acc_sc[h] / l_sc[h]).astype(BF16)
        o_ref[0, h] = jnp.dot(wuvT_ref[h], o_lat, preferred_element_type=F32).astype(o_ref.dtype)


def _dsa_attention(qiT, w, qT, wukT, wuvT, ki, c, cT, topk):
    NB, H, Dh, t = qT.shape
    S = NB * t
    assert S < 2 ** 23
    R = DSA_KV_RANK
    tg = min(512, S)
    const2 = lambda i: (0, 0)
    const3 = lambda i: (0, 0, 0)
    return pl.pallas_call(
        functools.partial(_dsa_kernel, t=t, tg=tg, topk=topk),
        grid=(NB,),
        in_specs=[pl.BlockSpec((1, IDX_HEAD_DIM, IDX_HEADS * t), lambda i: (i, 0, 0)),
                  pl.BlockSpec((1, IDX_HEADS, t), lambda i: (i, 0, 0)),
                  pl.BlockSpec((1, H, Dh, t), lambda i: (i, 0, 0, 0)),
                  pl.BlockSpec((H, R, Dh), const3),
                  pl.BlockSpec((H, Dh, R), const3),
                  pl.BlockSpec((S, IDX_HEAD_DIM), const2, pipeline_mode=pl.Buffered(1)),
                  pl.BlockSpec((S, R), const2, pipeline_mode=pl.Buffered(1)),
                  pl.BlockSpec((R, S), const2, pipeline_mode=pl.Buffered(1))],
        out_specs=pl.BlockSpec((1, H, Dh, t), lambda i: (i, 0, 0, 0)),
        out_shape=jax.ShapeDtypeStruct((NB, H, Dh, t), BF16),
        scratch_shapes=[pltpu.VMEM((S, t), I32), pltpu.VMEM((R, H * t), BF16),
                        pltpu.VMEM((H, 1, t), F32), pltpu.VMEM((H, 1, t), F32),
                        pltpu.VMEM((H, R, t), F32),
                        pltpu.VMEM((tg, H * t), F32), pltpu.VMEM((tg, H * t), F32),
                        pltpu.VMEM((tg, IDX_HEADS * t), F32), pltpu.VMEM((tg, IDX_HEADS * t), F32)],
        compiler_params=_params("arbitrary"),
        name="dsa_attention",
    )(qiT, w, qT, wukT, wuvT, ki, c, cT)


def _shift_mix(p, prev_tail, mu, first_block):
    rows = lax.broadcasted_iota(I32, p.shape, 0)
    tail = jnp.where(first_block, 0.0, prev_tail)
    prev = jnp.where(rows == 0, tail, pltpu.roll(p, 1, axis=0))
    return p + (prev - p) * mu


def _split3(f):
    hi = f.astype(BF16)
    r1 = f - hi.astype(F32)
    mid = r1.astype(BF16)
    lo = (r1 - mid.astype(F32)).astype(BF16)
    return hi, mid, lo


def _dot3(x, w01):
    return sum(jnp.dot(part, w01, preferred_element_type=F32) for part in _split3(x))


def _dot3_left(w01, x):
    return sum(jnp.dot(w01, part, preferred_element_type=F32) for part in _split3(x))


def _rwkv_pre_kernel(*refs, with_vres):
    (p_ref, pp_ref, mu_ref, w0_ref, a0_ref, wwa_ref, gup_ref, kk_ref, ka_ref, rk_ref,
     tri_ref, blk_ref, bd_ref) = refs[:13]
    if with_vres:
        vup_ref, vv0_ref, vf_ref = refs[13:16]
    (at_o, rt_o, bt_o, kt_o, bh_o, kh_o, vb_o, pc_o, bonus_o, g_o, v_o) = refs[-11:]
    first = pl.program_id(0) == 0
    W = RWKV_W
    ps = _shift_mix(p_ref[...], pp_ref[7:8, :], mu_ref[...], first)
    r, k, v = ps[:, 0:W], ps[:, W:2 * W], ps[:, 2 * W:3 * W]
    wa = ps[:, 3 * W:3 * W + LANE]
    gl = ps[:, 3 * W + LANE:3 * W + 2 * LANE]
    lane = lax.broadcasted_iota(I32, wa.shape, 1)
    wa = jnp.where(lane < RWKV_W_LORA, jnp.tanh(wa), wa)
    up = jnp.dot(wa.astype(BF16), wwa_ref[...], preferred_element_type=F32)
    log_w = -_softplus(-(w0_ref[...] + up[:, 0:W])) - 0.5
    a = _sigmoid(a0_ref[...] + up[:, W:2 * W])
    g = jnp.dot(_sigmoid(gl).astype(BF16), gup_ref[...], preferred_element_type=F32)
    if with_vres:
        vl = ps[:, RWKV_IN:RWKV_IN + LANE]
        logit = vv0_ref[...] + jnp.dot(vl.astype(BF16), vup_ref[...], preferred_element_type=F32)
        v = v + (vf_ref[...] - v) * _sigmoid(logit)
    k2 = k * (1.0 + (a - 1.0) * ka_ref[...])
    kkr = k * kk_ref[...]
    bd = bd_ref[...]
    kk = kkr * lax.rsqrt(_dot3(kkr * kkr, bd) + 1e-12)
    lw = -jnp.exp(log_w)
    cum = _dot3_left(tri_ref[...], lw)
    cend = _dot3_left(blk_ref[...], lw)
    p_inv = jnp.exp(-cum)
    p_end = jnp.exp(cend - cum)
    beta = kk * a
    at_o[...] = (-kk * jnp.exp(cum - lw)).astype(BF16)
    rt_o[...] = (r * jnp.exp(cum)).astype(BF16)
    bt_o[...] = (beta * p_inv).astype(BF16)
    kt_o[...] = (k2 * p_inv).astype(BF16)
    bh_o[...] = (beta * p_end).astype(BF16)
    kh_o[...] = (k2 * p_end).astype(BF16)
    vb_o[...] = v.astype(BF16)
    pc_o[...] = jnp.exp(cend)
    bonus_o[...] = _dot3(r * k2 * rk_ref[...], bd) * v
    g_o[...] = g
    v_o[...] = v


def _block_ones(n, block, lower_tri=False):
    i = jnp.arange(n)
    m = (i[:, None] // block) == (i[None, :] // block)
    if lower_tri:
        m = m & (i[:, None] >= i[None, :])
    return m.astype(BF16)


def _rwkv_pre(p, mu, w0, a0, wwa, gup, k_k, k_a, r_k, vres=None):
    S, PW = p.shape
    W = RWKV_W
    tm = min(512, S)
    row = lambda i: (i, 0)
    const = lambda i: (0, 0)
    tail = lambda i: (jnp.maximum(i * (tm // 8) - 1, 0), 0)
    vec = lambda a: a.reshape(1, -1)
    args = [p, p, vec(mu), vec(w0), vec(a0), wwa, gup, vec(k_k), vec(k_a), vec(r_k),
            _block_ones(tm, CHUNK, lower_tri=True), _block_ones(tm, CHUNK),
            _block_ones(W, RWKV_HEAD_DIM)]
    specs = [pl.BlockSpec((tm, PW), row), pl.BlockSpec((8, PW), tail),
             pl.BlockSpec((1, PW), const), pl.BlockSpec((1, W), const), pl.BlockSpec((1, W), const),
             pl.BlockSpec(wwa.shape, const), pl.BlockSpec(gup.shape, const),
             pl.BlockSpec((1, W), const), pl.BlockSpec((1, W), const), pl.BlockSpec((1, W), const),
             pl.BlockSpec((tm, tm), const), pl.BlockSpec((tm, tm), const),
             pl.BlockSpec((W, W), const)]
    if vres is not None:
        vup, vv0, v_first = vres
        args += [vup, vec(vv0), v_first]
        specs += [pl.BlockSpec(vup.shape, const), pl.BlockSpec((1, W), const),
                  pl.BlockSpec((tm, W), row)]
    return pl.pallas_call(
        functools.partial(_rwkv_pre_kernel, with_vres=vres is not None),
        grid=(S // tm,),
        in_specs=specs,
        out_specs=[pl.BlockSpec((tm, W), row)] * 11,
        out_shape=[jax.ShapeDtypeStruct((S, W), BF16)] * 7 + [jax.ShapeDtypeStruct((S, W), F32)] * 4,
        compiler_params=_params("arbitrary"),
        name="rwkv_pre",
    )(*args)


def _rwkv_scan_kernel(at_ref, rt_ref, bt_ref, kt_ref, bh_ref, kh_ref, v_ref, pc_ref, y_ref, h_sc,
                      *, chunks):
    C = CHUNK
    N = RWKV_HEAD_DIM

    @pl.when(pl.program_id(0) == 0)
    def _():
        h_sc[...] = jnp.zeros(h_sc.shape, F32)

    ti = lax.broadcasted_iota(I32, (C, C), 0)
    tj = lax.broadcasted_iota(I32, (C, C), 1)
    lower_incl = ti >= tj
    lower_strict = ti > tj
    eye = (ti == tj).astype(F32)
    eye_n = (lax.broadcasted_iota(I32, (N, N), 0) == lax.broadcasted_iota(I32, (N, N), 1)).astype(F32)

    def mm(x, y):
        return jnp.dot(x.astype(BF16), y.astype(BF16), preferred_element_type=F32)

    def mm_nt(x, y):
        return lax.dot_general(x.astype(BF16), y.astype(BF16), (((1,), (1,)), ((), ())),
                               preferred_element_type=F32)

    def mm_tn(x, y):
        return lax.dot_general(x.astype(BF16), y.astype(BF16), (((0,), (0,)), ((), ())),
                               preferred_element_type=F32)

    units = [(c, h) for c in range(chunks) for h in range(RWKV_HEADS)]
    tile = lambda ref, u: ref[u[0] * C:(u[0] + 1) * C, u[1] * N:(u[1] + 1) * N]
    each = lambda fn: {u: fn(u) for u in units}

    At, Rt, Bt, Kt = (each(lambda u, r=ref: tile(r, u)) for ref in (at_ref, rt_ref, bt_ref, kt_ref))
    Bh, Kh, V = (each(lambda u, r=ref: tile(r, u)) for ref in (bh_ref, kh_ref, v_ref))
    AR = each(lambda u: jnp.concatenate([At[u], Rt[u]], axis=0))
    Mb = each(lambda u: mm_nt(AR[u], Bt[u]))
    Mk = each(lambda u: mm_nt(AR[u], Kt[u]))
    Lab = each(lambda u: jnp.where(lower_strict, Mb[u][0:C], 0.0))
    Mrb = each(lambda u: jnp.where(lower_incl, Mb[u][C:2 * C], 0.0))
    Lak = each(lambda u: jnp.where(lower_strict, Mk[u][0:C], 0.0))
    Mrk = each(lambda u: jnp.where(lower_incl, Mk[u][C:2 * C], 0.0))
    T = each(lambda u: eye + Lab[u])
    Lp = Lab
    span = 2
    while span < C:
        Lp = each(lambda u, Lp=Lp: mm(Lp[u], Lp[u]))
        T = each(lambda u, T=T, Lp=Lp: T[u] + mm(Lp[u], T[u]))
        span *= 2
    W1 = each(lambda u: mm(Lak[u], V[u]))
    A2 = each(lambda u: mm(T[u], At[u]))
    U0 = each(lambda u: mm(T[u], W1[u]))
    R2 = each(lambda u: Rt[u].astype(F32) + mm(Mrb[u], A2[u]))
    Y0 = each(lambda u: mm(Mrb[u], U0[u]) + mm(Mrk[u], V[u]))
    G = each(lambda u: eye_n * tile(pc_ref, u)[0:1, :] + mm_tn(Bh[u], A2[u]))
    H0 = each(lambda u: mm_tn(Bh[u], U0[u]) + mm_tn(Kh[u], V[u]))

    H = {h: h_sc[h] for h in range(RWKV_HEADS)}
    for c in range(chunks):
        ys = []
        for h in range(RWKV_HEADS):
            u = (c, h)
            ys.append(mm(R2[u], H[h]) + Y0[u])
            H[h] = mm(G[u], H[h]) + H0[u]
        y_ref[c * C:(c + 1) * C, :] = jnp.concatenate(ys, axis=1)
    for h in range(RWKV_HEADS):
        h_sc[h] = H[h]


def _rwkv_scan(at, rt, bt, kt, bh, kh, vb, pc):
    S, W = at.shape
    chunks = 4 if S % (4 * CHUNK) == 0 else 1
    tb = chunks * CHUNK
    seq = pl.BlockSpec((tb, W), lambda i: (i, 0))
    return pl.pallas_call(
        functools.partial(_rwkv_scan_kernel, chunks=chunks),
        grid=(S // tb,),
        in_specs=[seq] * 8,
        out_specs=seq,
        out_shape=jax.ShapeDtypeStruct((S, W), F32),
        scratch_shapes=[pltpu.VMEM((RWKV_HEADS, RWKV_HEAD_DIM, RWKV_HEAD_DIM), F32)],
        compiler_params=_params("arbitrary"),
        name="rwkv_scan",
    )(at, rt, bt, kt, bh, kh, vb, pc)


def _merge_kernel(x_ref, of_ref, od_ref, y_ref, bonus_ref, g_ref, lng_ref, lnb_ref, bd_ref,
                  gp_ref, bg_ref, pf_ref, pd_ref, pr_ref, wo_ref, gn_ref, xo_ref, h_ref):
    D = D_MODEL
    bd = bd_ref[...]
    inv_n = 1.0 / RWKV_HEAD_DIM
    y = y_ref[...]
    yc = y - _dot3(y, bd) * inv_n
    var = _dot3(yc * yc, bd) * inv_n
    yn = yc * lax.rsqrt(var + RWKV_LN_EPS) * lng_ref[...] + lnb_ref[...]
    o_rwkv = ((yn + bonus_ref[...]) * g_ref[...]).astype(BF16)
    gates = _sigmoid(gp_ref[...] + bg_ref[...])
    merged = (gates[:, 0:D] * jnp.dot(of_ref[...], pf_ref[...], preferred_element_type=F32)
              + gates[:, D:2 * D] * jnp.dot(od_ref[...], pd_ref[...], preferred_element_type=F32)
              + gates[:, 2 * D:3 * D] * jnp.dot(o_rwkv, pr_ref[...], preferred_element_type=F32))
    x = x_ref[...] + jnp.dot(merged.astype(BF16), wo_ref[...], preferred_element_type=F32)
    xo_ref[...] = x
    h_ref[...] = _rms(x, gn_ref[...]).astype(h_ref.dtype)


def _merge(x, o_fox, o_dsa, y_rwkv, bonus, g_rwkv, ln_g, ln_b, gate_p, b_gate, p_fox, p_dsa, p_rwkv,
           w_out, g_ffn):
    S, D = x.shape
    W = RWKV_W
    tm = min(512, S)
    row = lambda i: (i, 0)
    const = lambda i: (0, 0)
    return pl.pallas_call(
        _merge_kernel,
        grid=(S // tm,),
        in_specs=[pl.BlockSpec((tm, D), row), pl.BlockSpec((tm, FOX_W), row),
                  pl.BlockSpec((tm, DSA_W), row), pl.BlockSpec((tm, W), row),
                  pl.BlockSpec((tm, W), row), pl.BlockSpec((tm, W), row),
                  pl.BlockSpec((1, W), const), pl.BlockSpec((1, W), const),
                  pl.BlockSpec((W, W), const),
                  pl.BlockSpec((tm, 3 * D), row), pl.BlockSpec((1, 3 * D), const),
                  pl.BlockSpec((FOX_W, D), const), pl.BlockSpec((DSA_W, D), const),
                  pl.BlockSpec((W, D), const), pl.BlockSpec((D, D), const),
                  pl.BlockSpec((1, D), const)],
        out_specs=[pl.BlockSpec((tm, D), row), pl.BlockSpec((tm, D), row)],
        out_shape=[jax.ShapeDtypeStruct((S, D), F32), jax.ShapeDtypeStruct((S, D), BF16)],
        compiler_params=_params("arbitrary"),
        name="merge",
    )(x, o_fox, o_dsa, y_rwkv, bonus, g_rwkv, ln_g.reshape(1, W), ln_b.reshape(1, W),
      _block_ones(W, RWKV_HEAD_DIM), gate_p, b_gate.reshape(1, -1), p_fox, p_dsa, p_rwkv, w_out,
      g_ffn.reshape(1, D))


def _swiglu_tile(h, wg, wu):
    gate = jnp.dot(h, wg, preferred_element_type=F32)
    up = jnp.dot(h, wu, preferred_element_type=F32)
    return gate * _sigmoid(gate) * up


def _ffn_kernel(x_ref, h_ref, wg_ref, wu_ref, wd_ref, gf_ref, o_ref, acc_sc, *, final_norm):
    f = pl.program_id(1)

    @pl.when(f == 0)
    def _():
        acc_sc[...] = x_ref[...]

    act = _swiglu_tile(h_ref[...], wg_ref[...], wu_ref[...])
    acc_sc[...] += jnp.dot(act.astype(BF16), wd_ref[...], preferred_element_type=F32)

    @pl.when(f == pl.num_programs(1) - 1)
    def _():
        y = acc_sc[...]
        o_ref[...] = _rms(y, gf_ref[...]) if final_norm else y


def _ffn(x, h, wg, wu, wd, g_final, final_norm):
    S, D = x.shape
    Fd = wg.shape[1]
    tm = min(512, S)
    tf = _pick_tile(Fd, 1408)
    return pl.pallas_call(
        functools.partial(_ffn_kernel, final_norm=final_norm),
        grid=(S // tm, Fd // tf),
        in_specs=[pl.BlockSpec((tm, D), lambda i, f: (i, 0)),
                  pl.BlockSpec((tm, D), lambda i, f: (i, 0)),
                  pl.BlockSpec((D, tf), lambda i, f: (0, f)),
                  pl.BlockSpec((D, tf), lambda i, f: (0, f)),
                  pl.BlockSpec((tf, D), lambda i, f: (f, 0)),
                  pl.BlockSpec((1, D), lambda i, f: (0, 0))],
        out_specs=pl.BlockSpec((tm, D), lambda i, f: (i, 0)),
        out_shape=jax.ShapeDtypeStruct((S, D), F32),
        scratch_shapes=[pltpu.VMEM((tm, D), F32)],
        compiler_params=_params("arbitrary", "arbitrary"),
        name="ffn",
    )(x, h, wg, wu, wd, g_final.reshape(1, D))


MOE_CHUNK = 128


def _moe_kernel(x_ref, h_ref, rw_ref, rb_ref, wg_ref, wu_ref, wd_ref, gf_ref, o_ref,
                acc_sc, gate_sc, rank_sc, cnt_sc, hs_sc, ys_sc, *, final_norm):
    e = pl.program_id(1)
    f = pl.program_id(2)
    last_f = pl.num_programs(2) - 1
    T, D = h_ref.shape
    C = MOE_CHUNK
    lane = lax.broadcasted_iota(I32, (T, LANE), 1)

    @pl.when((e == 0) & (f == 0))
    def _():
        acc_sc[...] = x_ref[...]
        logits = jnp.dot(h_ref[...], rw_ref[...], preferred_element_type=F32) + rb_ref[...]
        logits = jnp.where(lane < N_EXPERTS, logits, -jnp.inf)
        v1 = jnp.max(logits, axis=-1, keepdims=True)
        i1 = jnp.min(jnp.where(logits == v1, lane, LANE), axis=-1, keepdims=True)
        rest = jnp.where(lane == i1, -jnp.inf, logits)
        v2 = jnp.max(rest, axis=-1, keepdims=True)
        i2 = jnp.min(jnp.where(rest == v2, lane, LANE), axis=-1, keepdims=True)
        e2 = jnp.exp(v2 - v1)
        p1 = 1.0 / (1.0 + e2)
        gate_sc[...] = jnp.where(lane == i1, p1, 0.0) + jnp.where(lane == i2, e2 * p1, 0.0)
        routed = ((lane == i1) | (lane == i2)).astype(F32)
        B = 256
        earlier = (lax.broadcasted_iota(I32, (B, B), 0) > lax.broadcasted_iota(I32, (B, B), 1)).astype(BF16)
        seen = jnp.zeros((1, LANE), F32)
        for b in range(T // B):
            blk = routed[b * B:(b + 1) * B]
            before = jnp.dot(earlier, blk.astype(BF16), preferred_element_type=F32) + seen
            rank_sc[b * B:(b + 1) * B, :] = jnp.where(blk > 0.0, before, -1.0)
            seen = seen + jnp.sum(blk, axis=0, keepdims=True)
        cnt_sc[...] = seen

    n_e = jnp.sum(jnp.where(lax.broadcasted_iota(I32, (1, LANE), 1) == e, cnt_sc[...], 0.0))
    n_chunks = sum(jnp.where(n_e > j * C, 1, 0) for j in range(T // C))
    C2 = 2 * C
    n_pairs = sum(jnp.where(n_e > j * C2, 1, 0) for j in range(T // C2))

    @pl.when(f == 0)
    def _():
        pick8 = (lax.broadcasted_iota(I32, (8, LANE), 1) == e).astype(BF16)
        rk = rank_sc[...]
        rk_hi = rk.astype(BF16)
        rk_lo = (rk - rk_hi.astype(F32)).astype(BF16)
        nt = (((1,), (1,)), ((), ()))
        rrow = (lax.dot_general(pick8, rk_hi, nt, preferred_element_type=F32)
                + lax.dot_general(pick8, rk_lo, nt, preferred_element_type=F32))[0:1, :]

        def gather(p, carry):
            off = pl.multiple_of(p * C2, C2)
            slot = (off + lax.broadcasted_iota(I32, (C2, T), 0)).astype(F32)
            onehot = (rrow == slot).astype(BF16)
            hs_sc[pl.ds(off, C2), :] = jnp.dot(onehot, h_ref[...],
                                               preferred_element_type=F32).astype(BF16)
            ys_sc[pl.ds(off, C2), :] = jnp.zeros((C2, D), F32)
            return carry

        lax.fori_loop(0, n_pairs, gather, 0)

    def expert(c, carry):
        off = pl.multiple_of(c * C, C)
        act = _swiglu_tile(hs_sc[pl.ds(off, C), :], wg_ref[0], wu_ref[0])
        ys_sc[pl.ds(off, C), :] += jnp.dot(act.astype(BF16), wd_ref[0], preferred_element_type=F32)
        return carry

    lax.fori_loop(0, n_chunks, expert, 0)

    @pl.when(f == last_f)
    def _():
        mine = lane == e
        gate = jnp.sum(jnp.where(mine, gate_sc[...], 0.0), axis=-1, keepdims=True)
        rank = jnp.sum(jnp.where(mine, rank_sc[...], 0.0), axis=-1, keepdims=True)
        rank_b = jnp.broadcast_to(rank, (T, C2))
        col = lax.broadcasted_iota(I32, (T, C2), 1)

        def scatter(p, carry):
            off = pl.multiple_of(p * C2, C2)
            onehot_t = (rank_b == (col + off).astype(F32)).astype(BF16)
            y = jnp.dot(onehot_t, ys_sc[pl.ds(off, C2), :].astype(BF16), preferred_element_type=F32)
            acc_sc[...] += gate * y
            return carry

        lax.fori_loop(0, n_pairs, scatter, 0)

    @pl.when((e == pl.num_programs(1) - 1) & (f == last_f))
    def _():
        y = acc_sc[...]
        o_ref[...] = _rms(y, gf_ref[...]) if final_norm else y


def _moe(x, h, rw, rb, wg, wu, wd, g_final, final_norm):
    S, D = x.shape
    E, _, Fe = wg.shape
    tm = min(1024, S)
    tf = _pick_tile(Fe, 1792)
    once = pl.Buffered(1)
    return pl.pallas_call(
        functools.partial(_moe_kernel, final_norm=final_norm),
        grid=(S // tm, E, Fe // tf),
        in_specs=[pl.BlockSpec((tm, D), lambda i, e, f: (i, 0), pipeline_mode=once),
                  pl.BlockSpec((tm, D), lambda i, e, f: (i, 0), pipeline_mode=once),
                  pl.BlockSpec((D, LANE), lambda i, e, f: (0, 0)),
                  pl.BlockSpec((1, LANE), lambda i, e, f: (0, 0)),
                  pl.BlockSpec((1, D, tf), lambda i, e, f: (e, 0, f)),
                  pl.BlockSpec((1, D, tf), lambda i, e, f: (e, 0, f)),
                  pl.BlockSpec((1, tf, D), lambda i, e, f: (e, f, 0)),
                  pl.BlockSpec((1, D), lambda i, e, f: (0, 0))],
        out_specs=pl.BlockSpec((tm, D), lambda i, e, f: (i, 0)),
        out_shape=jax.ShapeDtypeStruct((S, D), F32),
        scratch_shapes=[pltpu.VMEM((tm, D), F32), pltpu.VMEM((tm, LANE), F32),
                        pltpu.VMEM((tm, LANE), F32), pltpu.VMEM((1, LANE), F32),
                        pltpu.VMEM((tm, D), BF16), pltpu.VMEM((tm, D), F32)],
        compiler_params=_params("arbitrary", "arbitrary", "arbitrary"),
        name="moe",
    )(x, h, rw, rb, wg, wu, wd, g_final.reshape(1, D))


def _cast_kernel(x_ref, o_ref):
    o_ref[...] = x_ref[...].astype(o_ref.dtype)


def _to_bf16(w):
    E, A, B = w.shape
    ta = 256
    return pl.pallas_call(
        _cast_kernel,
        grid=(E, A // ta),
        in_specs=[pl.BlockSpec((1, ta, B), lambda e, a: (e, a, 0))],
        out_specs=pl.BlockSpec((1, ta, B), lambda e, a: (e, a, 0)),
        out_shape=jax.ShapeDtypeStruct(w.shape, BF16),
        compiler_params=_params("arbitrary", "arbitrary"),
        name="to_bf16",
    )(w)


def _pad_cols(a, n):
    return jnp.pad(a, ((0, 0), (0, n - a.shape[1])))


def kernel(x, w_in, b_gate, g_mix, fox_f_bias, dsa_kv_norm, dsa_w_uk, dsa_w_uv, rwkv_mu, rwkv_w0, rwkv_w_up, rwkv_a0, rwkv_a_up, rwkv_g_up, rwkv_k_k, rwkv_k_a, rwkv_r_k, rwkv_ln_g, rwkv_ln_b, vres_down, vres_mu, vres_up, vres_v0, p_fox, p_dsa, p_rwkv, w_out, g_ffn, ffn_w_gate, ffn_w_up, ffn_w_down, router_w, router_b, moe_w_gate, moe_w_up, moe_w_down, g_final):
    B, S, D = x.shape
    assert B == 1 and D == D_MODEL and S % LANE == 0
    depth = w_in.shape[0]
    topk = min(IDX_TOPK, S // 4)
    bf = lambda a: a.astype(BF16)
    xs = x[0]
    v_first = None
    w_in_bf = _to_bf16(w_in)
    for l in range(depth):
        wl = w_in_bf[l]
        w_fox, w_dsa, w_rwkv, w_gate = (wl[:, :FOX_IN], wl[:, FOX_IN:FOX_IN + DSA_IN],
                                        wl[:, FOX_IN + DSA_IN:FOX_IN + DSA_IN + RWKV_IN],
                                        wl[:, FOX_IN + DSA_IN + RWKV_IN:])
        o1 = DSA_W + DSA_KV_RANK + IDX_W
        w_att = jnp.concatenate([_pad_cols(w_fox, 3 * FOX_W + LANE), w_dsa[:, :o1],
                                 _pad_cols(w_dsa[:, o1:], LANE)], axis=1)
        if l > 0:
            w_rwkv = jnp.concatenate([w_rwkv, bf(_pad_cols(vres_down[l - 1], LANE))], axis=1)
        pa = _rms_proj(xs, g_mix[l], w_att)
        pr = _rms_proj(xs, g_mix[l], w_rwkv)
        gate_p = _rms_proj(xs, g_mix[l], w_gate)

        c0 = 3 * FOX_W
        fl = pa[:, c0:c0 + FOX_HEADS].T.reshape(FOX_HEADS, S // LANE, LANE)
        Fh, Fm, Fl = (a.reshape(FOX_HEADS, S) for a in _fox_cumsum(fl, fox_f_bias[l]))
        qT = bf(pa[:, 0:FOX_W].T.reshape(FOX_HEADS, FOX_HEAD_DIM, S) * (FOX_HEAD_DIM ** -0.5 * LOG2E))
        r = jnp.arange(FOX_KA - FOX_HEAD_DIM)
        rq, ck = r[None, :, None], r[None, None, :]
        pick = lambda idx, a, b, c, d: jnp.where(idx == a[0], a[1], jnp.where(
            idx == b[0], b[1], jnp.where(idx == c[0], c[1], jnp.where(idx < 6, d, 0)))).astype(BF16)
        fq = pick(rq, (0, Fh[:, None, :]), (1, Fm[:, None, :]), (2, Fl[:, None, :]), 1)
        fk = pick(ck, (3, -Fh[:, :, None]), (4, -Fm[:, :, None]), (5, -Fl[:, :, None]), 1)
        qaT = jnp.concatenate([qT, fq], axis=1)
        k4 = bf(pa[:, FOX_W:2 * FOX_W].reshape(S, FOX_HEADS, FOX_HEAD_DIM).transpose(1, 0, 2))
        ka = jnp.concatenate([k4, fk], axis=2)
        fvT = bf(pa[:, 2 * FOX_W:3 * FOX_W].T.reshape(FOX_HEADS, FOX_HEAD_DIM, S))
        o_fox = _fox_attention(qaT, ka, fvT).reshape(FOX_W, S).T

        t = min(DSA_T, S)
        nb = S // t
        c1 = c0 + LANE
        dqT = bf(pa[:, c1:c1 + DSA_W].T.reshape(DSA_HEADS, DSA_HEAD_DIM, nb, t).transpose(2, 0, 1, 3))
        c2 = c1 + DSA_W
        ckv = _rmsnorm(pa[:, c2:c2 + DSA_KV_RANK], dsa_kv_norm[l], BF16)
        c3 = c2 + DSA_KV_RANK
        qiT = bf(pa[:, c3:c3 + IDX_W].T.reshape(IDX_HEADS, IDX_HEAD_DIM, nb, t).transpose(2, 1, 0, 3)
                 .reshape(nb, IDX_HEAD_DIM, IDX_HEADS * t))
        c4 = c3 + IDX_W
        ki = bf(pa[:, c4:c4 + IDX_HEAD_DIM])
        c5 = c4 + IDX_HEAD_DIM
        wi = pa[:, c5:c5 + IDX_HEADS].T.reshape(IDX_HEADS, nb, t).transpose(1, 0, 2) * (IDX_W ** -0.5)
        o_dsa = _dsa_attention(qiT, wi, dqT, bf(dsa_w_uk[l].transpose(0, 2, 1)),
                               bf(dsa_w_uv[l].transpose(0, 2, 1)), ki, ckv, ckv.T, topk)
        o_dsa = o_dsa.transpose(1, 2, 0, 3).reshape(DSA_W, S).T

        zw = jnp.zeros((RWKV_W_LORA, RWKV_W), F32)
        wwa = bf(jnp.concatenate([jnp.concatenate([rwkv_w_up[l], zw], axis=1),
                                  jnp.concatenate([zw, rwkv_a_up[l]], axis=1)], axis=0))
        vres = None
        mu = rwkv_mu[l]
        if l > 0:
            vup = jnp.pad(vres_up[l - 1], ((0, LANE - RWKV_V_LORA), (0, 0)))
            vres = (bf(vup), vres_v0[l - 1], v_first)
            mu = jnp.concatenate([mu, jnp.pad(vres_mu[l - 1], (0, LANE - RWKV_V_LORA))])
        *scan_ops, bonus, g_rwkv, v = _rwkv_pre(
            pr, mu, rwkv_w0[l], rwkv_a0[l], wwa, bf(rwkv_g_up[l]),
            rwkv_k_k[l], rwkv_k_a[l], rwkv_r_k[l], vres)
        if l == 0:
            v_first = v
        y_rwkv = _rwkv_scan(*scan_ops)

        xs, h2 = _merge(xs, o_fox, o_dsa, y_rwkv, bonus, g_rwkv, rwkv_ln_g[l], rwkv_ln_b[l],
                        gate_p, b_gate[l], bf(p_fox[l]), bf(p_dsa[l]), bf(p_rwkv[l]), bf(w_out[l]),
                        g_ffn[l])

        last = l == depth - 1
        if l % 2 == 0:
            xs = _ffn(xs, h2, bf(ffn_w_gate[l // 2]), bf(ffn_w_up[l // 2]), bf(ffn_w_down[l // 2]),
                      g_final, last)
        else:
            rw = bf(_pad_cols(router_w[l // 2], LANE))
            rb = _pad_cols(router_b[l // 2].reshape(1, -1), LANE)
            xs = _moe(xs, h2, rw, rb, _to_bf16(moe_w_gate[l // 2]), _to_bf16(moe_w_up[l // 2]),
                      _to_bf16(moe_w_down[l // 2]), g_final, last)
    return xs[None]
```

```python
import functools

import jax
import jax.numpy as jnp
from jax import lax
from jax.experimental import pallas as pl
from jax.experimental.pallas import tpu as pltpu

F32 = jnp.float32
BF16 = jnp.bfloat16
I32 = jnp.int32

D_MODEL = 1024
CHUNK = 64
RMS_EPS = 1e-6
FOX_HEADS, FOX_HEAD_DIM = 4, 64
DSA_HEADS, DSA_HEAD_DIM, DSA_KV_RANK = 4, 64, 128
IDX_HEADS, IDX_HEAD_DIM, IDX_TOPK = 8, 32, 256
RWKV_HEADS, RWKV_HEAD_DIM = 8, 64
RWKV_W_LORA, RWKV_A_LORA, RWKV_V_LORA, RWKV_G_LORA = 64, 64, 32, 128
RWKV_LN_EPS = 64e-5
FOX_W = FOX_HEADS * FOX_HEAD_DIM
DSA_W = DSA_HEADS * DSA_HEAD_DIM
RWKV_W = RWKV_HEADS * RWKV_HEAD_DIM
IDX_W = IDX_HEADS * IDX_HEAD_DIM
N_EXPERTS = 8
FOX_IN = 3 * FOX_W + FOX_HEADS
DSA_IN = DSA_W + DSA_KV_RANK + IDX_W + IDX_HEAD_DIM + IDX_HEADS
RWKV_IN = 3 * RWKV_W + RWKV_W_LORA + RWKV_A_LORA + RWKV_G_LORA

LANE = 128
VMEM_LIMIT = 52 * 1024 * 1024
NEG = -1e30
INT_MIN = -(2 ** 31)
INT_MAX = 2 ** 31 - 1
LOG2E = 1.4426950408889634
F32_MIN_NORMAL = 2.0 ** -126
HI = lax.Precision.HIGHEST


def _params(*sem):
    return pltpu.CompilerParams(dimension_semantics=sem, vmem_limit_bytes=VMEM_LIMIT)


def _pick_tile(n, cap):
    best = LANE
    for t in range(LANE, min(n, cap) + 1, LANE):
        if n % t == 0:
            best = t
    return best


def _softplus(x):
    return jnp.maximum(x, 0.0) + jnp.log1p(jnp.exp(-jnp.abs(x)))


def _sigmoid(x):
    return 1.0 / (1.0 + jnp.exp(-x))


def _rms(x, g):
    return x * lax.rsqrt(jnp.mean(x * x, axis=-1, keepdims=True) + RMS_EPS) * g


def _rms_proj_kernel(x_ref, g_ref, w_ref, o_ref):
    h = _rms(x_ref[...], g_ref[...])
    o_ref[...] = jnp.dot(h.astype(BF16), w_ref[...], preferred_element_type=F32)


def _rms_proj(x, g, w):
    S, D = x.shape
    N = w.shape[1]
    tm = min(512, S)
    tn = _pick_tile(N, 2304)
    return pl.pallas_call(
        _rms_proj_kernel,
        grid=(N // tn, S // tm),
        in_specs=[pl.BlockSpec((tm, D), lambda j, i: (i, 0)),
                  pl.BlockSpec((1, D), lambda j, i: (0, 0)),
                  pl.BlockSpec((D, tn), lambda j, i: (0, j))],
        out_specs=pl.BlockSpec((tm, tn), lambda j, i: (i, j)),
        out_shape=jax.ShapeDtypeStruct((S, N), F32),
        compiler_params=_params("arbitrary", "arbitrary"),
        name="rms_proj",
    )(x, g.reshape(1, D), w)


def _rmsnorm_kernel(x_ref, g_ref, o_ref):
    o_ref[...] = _rms(x_ref[...], g_ref[...]).astype(o_ref.dtype)


def _rmsnorm(x, g, dtype):
    S, D = x.shape
    tm = min(2048, S)
    return pl.pallas_call(
        _rmsnorm_kernel,
        grid=(S // tm,),
        in_specs=[pl.BlockSpec((tm, D), lambda i: (i, 0)),
                  pl.BlockSpec((1, D), lambda i: (0, 0))],
        out_specs=pl.BlockSpec((tm, D), lambda i: (i, 0)),
        out_shape=jax.ShapeDtypeStruct((S, D), dtype),
        compiler_params=_params("arbitrary"),
        name="rmsnorm",
    )(x, g.reshape(1, D))


def _fox_cumsum_kernel(fl_ref, b_ref, hi_ref, mid_ref, lo_ref):
    H, R, _ = fl_ref.shape
    upper = (lax.broadcasted_iota(I32, (LANE, LANE), 0)
             <= lax.broadcasted_iota(I32, (LANE, LANE), 1)).astype(F32)
    strict_lower = (lax.broadcasted_iota(I32, (R, R), 0)
                    > lax.broadcasted_iota(I32, (R, R), 1)).astype(F32)
    for h in range(H):
        log_f = -_softplus(-(fl_ref[h] + b_ref[h]))
        within = jnp.dot(log_f, upper, preferred_element_type=F32, precision=HI)
        row_tot = jnp.broadcast_to(within[:, LANE - 1:LANE], (R, LANE))
        before = jnp.dot(strict_lower, row_tot, preferred_element_type=F32, precision=HI)
        hi, mid, lo = _split3((within + before) * LOG2E)
        hi_ref[h], mid_ref[h], lo_ref[h] = hi, mid, lo


def _fox_cumsum(fl, bias):
    H, R, _ = fl.shape
    return pl.pallas_call(
        _fox_cumsum_kernel,
        out_shape=[jax.ShapeDtypeStruct((H, R, LANE), BF16)] * 3,
        compiler_params=pltpu.CompilerParams(vmem_limit_bytes=VMEM_LIMIT),
        name="fox_cumsum",
    )(fl, jnp.broadcast_to(bias.reshape(H, 1, 1), (H, 1, LANE)))


FOX_KA = 128


def _fox_kernel(qa_ref, ka_ref, vT_ref, o_ref, lga_sc, lgb_sc, m_sc, l_sc, acc_sc, *, t, tc, tg):
    i = pl.program_id(1)
    hb = qa_ref.shape[0]
    last = (i * t) // tg
    chains = [(h, q0) for h in range(hb) for q0 in range(0, t, tc)]

    def logits_into(dst_ref, g):
        off = pl.multiple_of(jnp.minimum(g, last) * tg, tg)
        for n, (h, q0) in enumerate(chains):
            dst_ref[n] = jnp.dot(ka_ref[h, pl.ds(off, tg), :], qa_ref[h, :, q0:q0 + tc],
                                 preferred_element_type=F32)

    def update(lg_ref, g, masked):
        off = pl.multiple_of(g * tg, tg)
        for n, (h, q0) in enumerate(chains):
            s = lg_ref[n]
            if masked:
                kpos = off + lax.broadcasted_iota(I32, (tg, tc), 0)
                qpos = i * t + q0 + lax.broadcasted_iota(I32, (tg, tc), 1)
                s = jnp.where(kpos <= qpos, s, NEG)
            m_prev = m_sc[n]
            m_new = jnp.maximum(m_prev, jnp.max(s, axis=0, keepdims=True))
            alpha = jnp.exp2(m_prev - m_new)
            p = jnp.exp2(s - m_new)
            l_sc[n] = alpha * l_sc[n] + jnp.sum(p, axis=0, keepdims=True)
            acc_sc[n] = alpha * acc_sc[n] + jnp.dot(vT_ref[h, :, pl.ds(off, tg)], p.astype(BF16),
                                                    preferred_element_type=F32)
            m_sc[n] = m_new

    m_sc[...] = jnp.full(m_sc.shape, NEG, F32)
    l_sc[...] = jnp.zeros(l_sc.shape, F32)
    acc_sc[...] = jnp.zeros(acc_sc.shape, F32)
    logits_into(lga_sc, 0)

    def pair(j, c):
        logits_into(lgb_sc, 2 * j + 1)
        update(lga_sc, 2 * j, False)
        logits_into(lga_sc, 2 * j + 2)
        update(lgb_sc, 2 * j + 1, False)
        return c

    lax.fori_loop(0, last // 2, pair, 0)
    tail = 2 * (last // 2)
    logits_into(lgb_sc, tail + 1)
    update(lga_sc, tail, True)

    @pl.when(tail + 1 <= last)
    def _():
        update(lgb_sc, tail + 1, True)

    for n, (h, q0) in enumerate(chains):
        o_ref[h, :, q0:q0 + tc] = (acc_sc[n] / l_sc[n]).astype(o_ref.dtype)


def _fox_attention(qaT, ka, vT):
    H, KA, S = qaT.shape
    Dh = vT.shape[1]
    t = min(512, S)
    tc = min(256, t)
    tg = min(512, S)
    hb = 2
    nc = hb * (t // tc)
    return pl.pallas_call(
        functools.partial(_fox_kernel, t=t, tc=tc, tg=tg),
        grid=(H // hb, S // t),
        in_specs=[pl.BlockSpec((hb, KA, t), lambda h, i: (h, 0, i)),
                  pl.BlockSpec((hb, S, KA), lambda h, i: (h, 0, 0)),
                  pl.BlockSpec((hb, Dh, S), lambda h, i: (h, 0, 0))],
        out_specs=pl.BlockSpec((hb, Dh, t), lambda h, i: (h, 0, i)),
        out_shape=jax.ShapeDtypeStruct((H, Dh, S), BF16),
        scratch_shapes=[pltpu.VMEM((nc, tg, tc), F32), pltpu.VMEM((nc, tg, tc), F32),
                        pltpu.VMEM((nc, 1, tc), F32), pltpu.VMEM((nc, 1, tc), F32),
                        pltpu.VMEM((nc, Dh, tc), F32)],
        compiler_params=_params("arbitrary", "arbitrary"),
        name="fox_attention",
    )(qaT, ka, vT)


DSA_T = 256


def _dsa_kernel(qiT_ref, w_ref, qT_ref, wukT_ref, wuvT_ref, ki_ref, c_ref, cT_ref, o_ref,
                keys_sc, qlat_sc, m_sc, l_sc, acc_sc, lga_sc, lgb_sc, rela_sc, relb_sc,
                *, t, tg, topk):
    i = pl.program_id(0)
    last = (i * t) // tg
    ng = last + 1

    qiT = qiT_ref[0]
    w = w_ref[0]

    def rel_into(dst_ref, g):
        off = pl.multiple_of(jnp.minimum(g, last) * tg, tg)
        dst_ref[...] = jnp.dot(ki_ref[pl.ds(off, tg), :], qiT, preferred_element_type=F32)

    def score_keys(rel_ref, g, masked):
        off = pl.multiple_of(g * tg, tg)
        sc = jnp.maximum(rel_ref[:, 0:t], 0.0) * w[0:1, :]
        for h in range(1, IDX_HEADS):
            sc = sc + jnp.maximum(rel_ref[:, h * t:(h + 1) * t], 0.0) * w[h:h + 1, :]
        bits = pltpu.bitcast(sc, I32)
        key = bits ^ ((bits >> 31) & 0x7FFFFFFF)
        kpos = off + lax.broadcasted_iota(I32, (tg, t), 0)
        key = jnp.where(jnp.abs(sc) < F32_MIN_NORMAL, -1 - kpos, key)
        if masked:
            qchunk = (i * t + lax.broadcasted_iota(I32, (tg, t), 1)) // CHUNK
            key = jnp.where(kpos // CHUNK <= qchunk, key, INT_MIN)
        keys_sc[pl.ds(off, tg), :] = key

    def fill_pair(j, c):
        rel_into(relb_sc, 2 * j + 1)
        score_keys(rela_sc, 2 * j, False)
        rel_into(rela_sc, 2 * j + 2)
        score_keys(relb_sc, 2 * j + 1, False)
        return c

    rel_into(rela_sc, 0)
    lax.fori_loop(0, last // 2, fill_pair, 0)
    tail = 2 * (last // 2)
    rel_into(relb_sc, tail + 1)
    score_keys(rela_sc, tail, True)

    @pl.when(tail + 1 <= last)
    def _():
        score_keys(relb_sc, tail + 1, True)

    def count_ge(cand):
        rows = 16 * LANE * 8 // t
        cb = jnp.broadcast_to(cand, (rows, t))

        def body(g, acc):
            off = pl.multiple_of(g * tg, tg)
            for k in range(tg // rows):
                acc = acc + (keys_sc[pl.ds(off + k * rows, rows), :] >= cb).astype(F32)
            return acc

        acc = lax.fori_loop(0, ng, body, jnp.zeros((rows, t), F32))
        return jnp.sum(acc, axis=0, keepdims=True)

    qpos = i * t + lax.broadcasted_iota(I32, (1, t), 1)
    n_adm = ((qpos // CHUNK + 1) * CHUNK).astype(F32)
    takes_all = n_adm <= topk

    def row_max(g, acc):
        off = pl.multiple_of(g * tg, tg)
        for k in range(tg // acc.shape[0]):
            acc = jnp.maximum(acc, keys_sc[pl.ds(off + k * acc.shape[0], acc.shape[0]), :])
        return acc

    kmax = jnp.max(lax.fori_loop(0, ng, row_max, jnp.full((16 * LANE * 8 // t, t), INT_MIN, I32)),
                   axis=0, keepdims=True)
    hi = jnp.where(kmax == INT_MAX, kmax, kmax + 1)
    guess = jnp.where(kmax > INT_MIN + 3 * 2 ** 23, kmax - 3 * 2 ** 23, INT_MIN + 1)
    n_guess = count_ge(guess)
    good = n_guess >= topk
    lo = jnp.where(good & jnp.logical_not(takes_all), guess, INT_MIN + 1)
    n_lo = jnp.where(good, n_guess, n_adm)
    hi = jnp.where(good | takes_all, hi, guess)
    steps_per_check = 3

    def unsettled(st):
        lo, hi, n_lo = st
        mid = (lo >> 1) + (hi >> 1) + (lo & hi & 1)
        done = (n_lo == topk) | (mid <= lo) | takes_all
        return jnp.min(jnp.where(done, 1.0, 0.0)) < 0.5

    def narrow(st):
        lo, hi, n_lo = st
        for _ in range(steps_per_check):
            cand = (lo >> 1) + (hi >> 1) + (lo & hi & 1)
            c = count_ge(cand)
            up = (c >= topk) & (cand > lo)
            down = (c < topk) & (cand > lo)
            lo = jnp.where(up, cand, lo)
            n_lo = jnp.where(up, c, n_lo)
            hi = jnp.where(down, cand, hi)
        return lo, hi, n_lo

    thr, _, n_ge = lax.while_loop(unsettled, narrow, (lo, hi, n_lo))
    has_ties = jnp.max(jnp.where((n_ge > topk) & jnp.logical_not(takes_all), 1.0, 0.0)) > 0.0
    thr_b = jnp.broadcast_to(thr, (tg, t))

    for h in range(DSA_HEADS):
        ql = jnp.dot(wukT_ref[h], qT_ref[0, h], preferred_element_type=F32)
        qlat_sc[:, h * t:(h + 1) * t] = (ql * (DSA_HEAD_DIM ** -0.5 * LOG2E)).astype(BF16)
    m_sc[...] = jnp.full(m_sc.shape, NEG, F32)
    l_sc[...] = jnp.zeros(l_sc.shape, F32)
    acc_sc[...] = jnp.zeros(acc_sc.shape, F32)

    def logits_into(dst_ref, g):
        off = pl.multiple_of(jnp.minimum(g, last) * tg, tg)
        dst_ref[...] = jnp.dot(c_ref[pl.ds(off, tg), :], qlat_sc[...],
                               preferred_element_type=F32)

    def attend(lg_ref, g, seen_eq, with_ties, need):
        valid = g <= last
        off = pl.multiple_of(jnp.minimum(g, last) * tg, tg)
        key = keys_sc[pl.ds(off, tg), :]
        if with_ties:
            eq = (key == thr_b) & valid
            eqf = eq.astype(F32)
            strict_lower = (lax.broadcasted_iota(I32, (tg, tg), 0)
                            > lax.broadcasted_iota(I32, (tg, tg), 1)).astype(BF16)
            rank = jnp.dot(strict_lower, eqf.astype(BF16), preferred_element_type=F32) + seen_eq
            sel = (key > thr_b) | (eq & (rank < need))
            seen_eq = seen_eq + jnp.sum(eqf, axis=0, keepdims=True)
        else:
            sel = key >= thr_b
        bias = jnp.where(sel & valid, 0.0, NEG)
        probs, alphas = [], []
        for h in range(DSA_HEADS):
            s = lg_ref[:, h * t:(h + 1) * t] + bias
            m_prev = m_sc[h]
            m_new = jnp.maximum(m_prev, jnp.max(s, axis=0, keepdims=True))
            alpha = jnp.exp2(m_prev - m_new)
            p = jnp.exp2(s - m_new)
            l_sc[h] = alpha * l_sc[h] + jnp.sum(p, axis=0, keepdims=True)
            m_sc[h] = m_new
            probs.append(p.astype(BF16))
            alphas.append(alpha)
        pv = jnp.dot(cT_ref[:, pl.ds(off, tg)], jnp.concatenate(probs, axis=1),
                     preferred_element_type=F32)
        for h in range(DSA_HEADS):
            acc_sc[h] = alphas[h] * acc_sc[h] + pv[:, h * t:(h + 1) * t]
        return seen_eq

    def sweep(with_ties, need=None):
        logits_into(lga_sc, 0)

        def pair(j, seen_eq):
            g0 = 2 * j
            logits_into(lgb_sc, g0 + 1)
            seen_eq = attend(lga_sc, g0, seen_eq, with_ties, need)
            logits_into(lga_sc, g0 + 2)
            return attend(lgb_sc, g0 + 1, seen_eq, with_ties, need)

        lax.fori_loop(0, last // 2 + 1, pair, jnp.zeros((1, t), F32))

    @pl.when(has_ties)
    def _():
        sweep(True, topk - count_ge(thr + 1))

    @pl.when(jnp.logical_not(has_ties))
    def _():
        sweep(False)

    for h in range(DSA_HEADS):
        o_lat = (acc_sc[h] / l_sc[h]).astype(BF16)
        o_ref[0, h] = jnp.dot(wuvT_ref[h], o_lat, preferred_element_type=F32).astype(o_ref.dtype)


def _dsa_attention(qiT, w, qT, wukT, wuvT, ki, c, cT, topk):
    NB, H, Dh, t = qT.shape
    S = NB * t
    assert S < 2 ** 23
    R = DSA_KV_RANK
    tg = min(512, S)
    const2 = lambda i: (0, 0)
    const3 = lambda i: (0, 0, 0)
    return pl.pallas_call(
        functools.partial(_dsa_kernel, t=t, tg=tg, topk=topk),
        grid=(NB,),
        in_specs=[pl.BlockSpec((1, IDX_HEAD_DIM, IDX_HEADS * t), lambda i: (i, 0, 0)),
                  pl.BlockSpec((1, IDX_HEADS, t), lambda i: (i, 0, 0)),
                  pl.BlockSpec((1, H, Dh, t), lambda i: (i, 0, 0, 0)),
                  pl.BlockSpec((H, R, Dh), const3),
                  pl.BlockSpec((H, Dh, R), const3),
                  pl.BlockSpec((S, IDX_HEAD_DIM), const2, pipeline_mode=pl.Buffered(1)),
                  pl.BlockSpec((S, R), const2, pipeline_mode=pl.Buffered(1)),
                  pl.BlockSpec((R, S), const2, pipeline_mode=pl.Buffered(1))],
        out_specs=pl.BlockSpec((1, H, Dh, t), lambda i: (i, 0, 0, 0)),
        out_shape=jax.ShapeDtypeStruct((NB, H, Dh, t), BF16),
        scratch_shapes=[pltpu.VMEM((S, t), I32), pltpu.VMEM((R, H * t), BF16),
                        pltpu.VMEM((H, 1, t), F32), pltpu.VMEM((H, 1, t), F32),
                        pltpu.VMEM((H, R, t), F32),
                        pltpu.VMEM((tg, H * t), F32), pltpu.VMEM((tg, H * t), F32),
                        pltpu.VMEM((tg, IDX_HEADS * t), F32), pltpu.VMEM((tg, IDX_HEADS * t), F32)],
        compiler_params=_params("arbitrary"),
        name="dsa_attention",
    )(qiT, w, qT, wukT, wuvT, ki, c, cT)


def _shift_mix(p, prev_tail, mu, first_block):
    rows = lax.broadcasted_iota(I32, p.shape, 0)
    tail = jnp.where(first_block, 0.0, prev_tail)
    prev = jnp.where(rows == 0, tail, pltpu.roll(p, 1, axis=0))
    return p + (prev - p) * mu


def _split3(f):
    hi = f.astype(BF16)
    r1 = f - hi.astype(F32)
    mid = r1.astype(BF16)
    lo = (r1 - mid.astype(F32)).astype(BF16)
    return hi, mid, lo


def _dot3(x, w01):
    return sum(jnp.dot(part, w01, preferred_element_type=F32) for part in _split3(x))


def _dot3_left(w01, x):
    return sum(jnp.dot(w01, part, preferred_element_type=F32) for part in _split3(x))


def _rwkv_pre_kernel(*refs, with_vres):
    (p_ref, pp_ref, mu_ref, w0_ref, a0_ref, wwa_ref, gup_ref, kk_ref, ka_ref, rk_ref,
     tri_ref, blk_ref, bd_ref) = refs[:13]
    if with_vres:
        vup_ref, vv0_ref, vf_ref = refs[13:16]
    (at_o, rt_o, bt_o, kt_o, bh_o, kh_o, vb_o, pc_o, bonus_o, g_o, v_o) = refs[-11:]
    first = pl.program_id(0) == 0
    W = RWKV_W
    ps = _shift_mix(p_ref[...], pp_ref[7:8, :], mu_ref[...], first)
    r, k, v = ps[:, 0:W], ps[:, W:2 * W], ps[:, 2 * W:3 * W]
    wa = ps[:, 3 * W:3 * W + LANE]
    gl = ps[:, 3 * W + LANE:3 * W + 2 * LANE]
    lane = lax.broadcasted_iota(I32, wa.shape, 1)
    wa = jnp.where(lane < RWKV_W_LORA, jnp.tanh(wa), wa)
    up = jnp.dot(wa.astype(BF16), wwa_ref[...], preferred_element_type=F32)
    log_w = -_softplus(-(w0_ref[...] + up[:, 0:W])) - 0.5
    a = _sigmoid(a0_ref[...] + up[:, W:2 * W])
    g = jnp.dot(_sigmoid(gl).astype(BF16), gup_ref[...], preferred_element_type=F32)
    if with_vres:
        vl = ps[:, RWKV_IN:RWKV_IN + LANE]
        logit = vv0_ref[...] + jnp.dot(vl.astype(BF16), vup_ref[...], preferred_element_type=F32)
        v = v + (vf_ref[...] - v) * _sigmoid(logit)
    k2 = k * (1.0 + (a - 1.0) * ka_ref[...])
    kkr = k * kk_ref[...]
    bd = bd_ref[...]
    kk = kkr * lax.rsqrt(_dot3(kkr * kkr, bd) + 1e-12)
    lw = -jnp.exp(log_w)
    cum = _dot3_left(tri_ref[...], lw)
    cend = _dot3_left(blk_ref[...], lw)
    p_inv = jnp.exp(-cum)
    p_end = jnp.exp(cend - cum)
    beta = kk * a
    at_o[...] = (-kk * jnp.exp(cum - lw)).astype(BF16)
    rt_o[...] = (r * jnp.exp(cum)).astype(BF16)
    bt_o[...] = (beta * p_inv).astype(BF16)
    kt_o[...] = (k2 * p_inv).astype(BF16)
    bh_o[...] = (beta * p_end).astype(BF16)
    kh_o[...] = (k2 * p_end).astype(BF16)
    vb_o[...] = v.astype(BF16)
    pc_o[...] = jnp.exp(cend)
    bonus_o[...] = _dot3(r * k2 * rk_ref[...], bd) * v
    g_o[...] = g
    v_o[...] = v


def _block_ones(n, block, lower_tri=False):
    i = jnp.arange(n)
    m = (i[:, None] // block) == (i[None, :] // block)
    if lower_tri:
        m = m & (i[:, None] >= i[None, :])
    return m.astype(BF16)


def _rwkv_pre(p, mu, w0, a0, wwa, gup, k_k, k_a, r_k, vres=None):
    S, PW = p.shape
    W = RWKV_W
    tm = min(512, S)
    row = lambda i: (i, 0)
    const = lambda i: (0, 0)
    tail = lambda i: (jnp.maximum(i * (tm // 8) - 1, 0), 0)
    vec = lambda a: a.reshape(1, -1)
    args = [p, p, vec(mu), vec(w0), vec(a0), wwa, gup, vec(k_k), vec(k_a), vec(r_k),
            _block_ones(tm, CHUNK, lower_tri=True), _block_ones(tm, CHUNK),
            _block_ones(W, RWKV_HEAD_DIM)]
    specs = [pl.BlockSpec((tm, PW), row), pl.BlockSpec((8, PW), tail),
             pl.BlockSpec((1, PW), const), pl.BlockSpec((1, W), const), pl.BlockSpec((1, W), const),
             pl.BlockSpec(wwa.shape, const), pl.BlockSpec(gup.shape, const),
             pl.BlockSpec((1, W), const), pl.BlockSpec((1, W), const), pl.BlockSpec((1, W), const),
             pl.BlockSpec((tm, tm), const), pl.BlockSpec((tm, tm), const),
             pl.BlockSpec((W, W), const)]
    if vres is not None:
        vup, vv0, v_first = vres
        args += [vup, vec(vv0), v_first]
        specs += [pl.BlockSpec(vup.shape, const), pl.BlockSpec((1, W), const),
                  pl.BlockSpec((tm, W), row)]
    return pl.pallas_call(
        functools.partial(_rwkv_pre_kernel, with_vres=vres is not None),
        grid=(S // tm,),
        in_specs=specs,
        out_specs=[pl.BlockSpec((tm, W), row)] * 11,
        out_shape=[jax.ShapeDtypeStruct((S, W), BF16)] * 7 + [jax.ShapeDtypeStruct((S, W), F32)] * 4,
        compiler_params=_params("arbitrary"),
        name="rwkv_pre",
    )(*args)


def _rwkv_scan_kernel(at_ref, rt_ref, bt_ref, kt_ref, bh_ref, kh_ref, v_ref, pc_ref, y_ref, h_sc,
                      *, chunks):
    C = CHUNK
    N = RWKV_HEAD_DIM

    @pl.when(pl.program_id(0) == 0)
    def _():
        h_sc[...] = jnp.zeros(h_sc.shape, F32)

    ti = lax.broadcasted_iota(I32, (C, C), 0)
    tj = lax.broadcasted_iota(I32, (C, C), 1)
    lower_incl = ti >= tj
    lower_strict = ti > tj
    eye = (ti == tj).astype(F32)
    eye_n = (lax.broadcasted_iota(I32, (N, N), 0) == lax.broadcasted_iota(I32, (N, N), 1)).astype(F32)

    def mm(x, y):
        return jnp.dot(x.astype(BF16), y.astype(BF16), preferred_element_type=F32)

    def mm_nt(x, y):
        return lax.dot_general(x.astype(BF16), y.astype(BF16), (((1,), (1,)), ((), ())),
                               preferred_element_type=F32)

    def mm_tn(x, y):
        return lax.dot_general(x.astype(BF16), y.astype(BF16), (((0,), (0,)), ((), ())),
                               preferred_element_type=F32)

    units = [(c, h) for c in range(chunks) for h in range(RWKV_HEADS)]
    tile = lambda ref, u: ref[u[0] * C:(u[0] + 1) * C, u[1] * N:(u[1] + 1) * N]
    each = lambda fn: {u: fn(u) for u in units}

    At, Rt, Bt, Kt = (each(lambda u, r=ref: tile(r, u)) for ref in (at_ref, rt_ref, bt_ref, kt_ref))
    Bh, Kh, V = (each(lambda u, r=ref: tile(r, u)) for ref in (bh_ref, kh_ref, v_ref))
    AR = each(lambda u: jnp.concatenate([At[u], Rt[u]], axis=0))
    Mb = each(lambda u: mm_nt(AR[u], Bt[u]))
    Mk = each(lambda u: mm_nt(AR[u], Kt[u]))
    Lab = each(lambda u: jnp.where(lower_strict, Mb[u][0:C], 0.0))
    Mrb = each(lambda u: jnp.where(lower_incl, Mb[u][C:2 * C], 0.0))
    Lak = each(lambda u: jnp.where(lower_strict, Mk[u][0:C], 0.0))
    Mrk = each(lambda u: jnp.where(lower_incl, Mk[u][C:2 * C], 0.0))
    T = each(lambda u: eye + Lab[u])
    Lp = Lab
    span = 2
    while span < C:
        Lp = each(lambda u, Lp=Lp: mm(Lp[u], Lp[u]))
        T = each(lambda u, T=T, Lp=Lp: T[u] + mm(Lp[u], T[u]))
        span *= 2
    W1 = each(lambda u: mm(Lak[u], V[u]))
    A2 = each(lambda u: mm(T[u], At[u]))
    U0 = each(lambda u: mm(T[u], W1[u]))
    R2 = each(lambda u: Rt[u].astype(F32) + mm(Mrb[u], A2[u]))
    Y0 = each(lambda u: mm(Mrb[u], U0[u]) + mm(Mrk[u], V[u]))
    G = each(lambda u: eye_n * tile(pc_ref, u)[0:1, :] + mm_tn(Bh[u], A2[u]))
    H0 = each(lambda u: mm_tn(Bh[u], U0[u]) + mm_tn(Kh[u], V[u]))

    H = {h: h_sc[h] for h in range(RWKV_HEADS)}
    for c in range(chunks):
        ys = []
        for h in range(RWKV_HEADS):
            u = (c, h)
            ys.append(mm(R2[u], H[h]) + Y0[u])
            H[h] = mm(G[u], H[h]) + H0[u]
        y_ref[c * C:(c + 1) * C, :] = jnp.concatenate(ys, axis=1)
    for h in range(RWKV_HEADS):
        h_sc[h] = H[h]


def _rwkv_scan(at, rt, bt, kt, bh, kh, vb, pc):
    S, W = at.shape
    chunks = 4 if S % (4 * CHUNK) == 0 else 1
    tb = chunks * CHUNK
    seq = pl.BlockSpec((tb, W), lambda i: (i, 0))
    return pl.pallas_call(
        functools.partial(_rwkv_scan_kernel, chunks=chunks),
        grid=(S // tb,),
        in_specs=[seq] * 8,
        out_specs=seq,
        out_shape=jax.ShapeDtypeStruct((S, W), F32),
        scratch_shapes=[pltpu.VMEM((RWKV_HEADS, RWKV_HEAD_DIM, RWKV_HEAD_DIM), F32)],
        compiler_params=_params("arbitrary"),
        name="rwkv_scan",
    )(at, rt, bt, kt, bh, kh, vb, pc)


def _merge_kernel(x_ref, of_ref, od_ref, y_ref, bonus_ref, g_ref, lng_ref, lnb_ref, bd_ref,
                  gp_ref, bg_ref, pf_ref, pd_ref, pr_ref, wo_ref, gn_ref, xo_ref, h_ref):
    D = D_MODEL
    bd = bd_ref[...]
    inv_n = 1.0 / RWKV_HEAD_DIM
    y = y_ref[...]
    yc = y - _dot3(y, bd) * inv_n
    var = _dot3(yc * yc, bd) * inv_n
    yn = yc * lax.rsqrt(var + RWKV_LN_EPS) * lng_ref[...] + lnb_ref[...]
    o_rwkv = ((yn + bonus_ref[...]) * g_ref[...]).astype(BF16)
    gates = _sigmoid(gp_ref[...] + bg_ref[...])
    merged = (gates[:, 0:D] * jnp.dot(of_ref[...], pf_ref[...], preferred_element_type=F32)
              + gates[:, D:2 * D] * jnp.dot(od_ref[...], pd_ref[...], preferred_element_type=F32)
              + gates[:, 2 * D:3 * D] * jnp.dot(o_rwkv, pr_ref[...], preferred_element_type=F32))
    x = x_ref[...] + jnp.dot(merged.astype(BF16), wo_ref[...], preferred_element_type=F32)
    xo_ref[...] = x
    h_ref[...] = _rms(x, gn_ref[...]).astype(h_ref.dtype)


def _merge(x, o_fox, o_dsa, y_rwkv, bonus, g_rwkv, ln_g, ln_b, gate_p, b_gate, p_fox, p_dsa, p_rwkv,
           w_out, g_ffn):
    S, D = x.shape
    W = RWKV_W
    tm = min(512, S)
    row = lambda i: (i, 0)
    const = lambda i: (0, 0)
    return pl.pallas_call(
        _merge_kernel,
        grid=(S // tm,),
        in_specs=[pl.BlockSpec((tm, D), row), pl.BlockSpec((tm, FOX_W), row),
                  pl.BlockSpec((tm, DSA_W), row), pl.BlockSpec((tm, W), row),
                  pl.BlockSpec((tm, W), row), pl.BlockSpec((tm, W), row),
                  pl.BlockSpec((1, W), const), pl.BlockSpec((1, W), const),
                  pl.BlockSpec((W, W), const),
                  pl.BlockSpec((tm, 3 * D), row), pl.BlockSpec((1, 3 * D), const),
                  pl.BlockSpec((FOX_W, D), const), pl.BlockSpec((DSA_W, D), const),
                  pl.BlockSpec((W, D), const), pl.BlockSpec((D, D), const),
                  pl.BlockSpec((1, D), const)],
        out_specs=[pl.BlockSpec((tm, D), row), pl.BlockSpec((tm, D), row)],
        out_shape=[jax.ShapeDtypeStruct((S, D), F32), jax.ShapeDtypeStruct((S, D), BF16)],
        compiler_params=_params("arbitrary"),
        name="merge",
    )(x, o_fox, o_dsa, y_rwkv, bonus, g_rwkv, ln_g.reshape(1, W), ln_b.reshape(1, W),
      _block_ones(W, RWKV_HEAD_DIM), gate_p, b_gate.reshape(1, -1), p_fox, p_dsa, p_rwkv, w_out,
      g_ffn.reshape(1, D))


def _swiglu_tile(h, wg, wu):
    gate = jnp.dot(h, wg, preferred_element_type=F32)
    up = jnp.dot(h, wu, preferred_element_type=F32)
    return gate * _sigmoid(gate) * up


def _ffn_kernel(x_ref, h_ref, wg_ref, wu_ref, wd_ref, gf_ref, o_ref, acc_sc, *, final_norm):
    f = pl.program_id(1)

    @pl.when(f == 0)
    def _():
        acc_sc[...] = x_ref[...]

    act = _swiglu_tile(h_ref[...], wg_ref[...], wu_ref[...])
    acc_sc[...] += jnp.dot(act.astype(BF16), wd_ref[...], preferred_element_type=F32)

    @pl.when(f == pl.num_programs(1) - 1)
    def _():
        y = acc_sc[...]
        o_ref[...] = _rms(y, gf_ref[...]) if final_norm else y


def _ffn(x, h, wg, wu, wd, g_final, final_norm):
    S, D = x.shape
    Fd = wg.shape[1]
    tm = min(512, S)
    tf = _pick_tile(Fd, 1408)
    return pl.pallas_call(
        functools.partial(_ffn_kernel, final_norm=final_norm),
        grid=(S // tm, Fd // tf),
        in_specs=[pl.BlockSpec((tm, D), lambda i, f: (i, 0)),
                  pl.BlockSpec((tm, D), lambda i, f: (i, 0)),
                  pl.BlockSpec((D, tf), lambda i, f: (0, f)),
                  pl.BlockSpec((D, tf), lambda i, f: (0, f)),
                  pl.BlockSpec((tf, D), lambda i, f: (f, 0)),
                  pl.BlockSpec((1, D), lambda i, f: (0, 0))],
        out_specs=pl.BlockSpec((tm, D), lambda i, f: (i, 0)),
        out_shape=jax.ShapeDtypeStruct((S, D), F32),
        scratch_shapes=[pltpu.VMEM((tm, D), F32)],
        compiler_params=_params("arbitrary", "arbitrary"),
        name="ffn",
    )(x, h, wg, wu, wd, g_final.reshape(1, D))


MOE_CHUNK = 128


def _moe_kernel(x_ref, h_ref, rw_ref, rb_ref, wg_ref, wu_ref, wd_ref, gf_ref, o_ref,
                acc_sc, gate_sc, rank_sc, cnt_sc, hs_sc, ys_sc, *, final_norm):
    e = pl.program_id(1)
    f = pl.program_id(2)
    last_f = pl.num_programs(2) - 1
    T, D = h_ref.shape
    C = MOE_CHUNK
    lane = lax.broadcasted_iota(I32, (T, LANE), 1)

    @pl.when((e == 0) & (f == 0))
    def _():
        acc_sc[...] = x_ref[...]
        logits = jnp.dot(h_ref[...], rw_ref[...], preferred_element_type=F32) + rb_ref[...]
        logits = jnp.where(lane < N_EXPERTS, logits, -jnp.inf)
        v1 = jnp.max(logits, axis=-1, keepdims=True)
        i1 = jnp.min(jnp.where(logits == v1, lane, LANE), axis=-1, keepdims=True)
        rest = jnp.where(lane == i1, -jnp.inf, logits)
        v2 = jnp.max(rest, axis=-1, keepdims=True)
        i2 = jnp.min(jnp.where(rest == v2, lane, LANE), axis=-1, keepdims=True)
        e2 = jnp.exp(v2 - v1)
        p1 = 1.0 / (1.0 + e2)
        gate_sc[...] = jnp.where(lane == i1, p1, 0.0) + jnp.where(lane == i2, e2 * p1, 0.0)
        routed = ((lane == i1) | (lane == i2)).astype(F32)
        B = 256
        earlier = (lax.broadcasted_iota(I32, (B, B), 0) > lax.broadcasted_iota(I32, (B, B), 1)).astype(BF16)
        seen = jnp.zeros((1, LANE), F32)
        for b in range(T // B):
            blk = routed[b * B:(b + 1) * B]
            before = jnp.dot(earlier, blk.astype(BF16), preferred_element_type=F32) + seen
            rank_sc[b * B:(b + 1) * B, :] = jnp.where(blk > 0.0, before, -1.0)
            seen = seen + jnp.sum(blk, axis=0, keepdims=True)
        cnt_sc[...] = seen

    n_e = jnp.sum(jnp.where(lax.broadcasted_iota(I32, (1, LANE), 1) == e, cnt_sc[...], 0.0))
    n_chunks = sum(jnp.where(n_e > j * C, 1, 0) for j in range(T // C))
    C2 = 2 * C
    n_pairs = sum(jnp.where(n_e > j * C2, 1, 0) for j in range(T // C2))

    @pl.when(f == 0)
    def _():
        pick8 = (lax.broadcasted_iota(I32, (8, LANE), 1) == e).astype(BF16)
        rk = rank_sc[...]
        rk_hi = rk.astype(BF16)
        rk_lo = (rk - rk_hi.astype(F32)).astype(BF16)
        nt = (((1,), (1,)), ((), ()))
        rrow = (lax.dot_general(pick8, rk_hi, nt, preferred_element_type=F32)
                + lax.dot_general(pick8, rk_lo, nt, preferred_element_type=F32))[0:1, :]

        def gather(p, carry):
            off = pl.multiple_of(p * C2, C2)
            slot = (off + lax.broadcasted_iota(I32, (C2, T), 0)).astype(F32)
            onehot = (rrow == slot).astype(BF16)
            hs_sc[pl.ds(off, C2), :] = jnp.dot(onehot, h_ref[...],
                                               preferred_element_type=F32).astype(BF16)
            ys_sc[pl.ds(off, C2), :] = jnp.zeros((C2, D), F32)
            return carry

        lax.fori_loop(0, n_pairs, gather, 0)

    def expert(c, carry):
        off = pl.multiple_of(c * C, C)
        act = _swiglu_tile(hs_sc[pl.ds(off, C), :], wg_ref[0], wu_ref[0])
        ys_sc[pl.ds(off, C), :] += jnp.dot(act.astype(BF16), wd_ref[0], preferred_element_type=F32)
        return carry

    lax.fori_loop(0, n_chunks, expert, 0)

    @pl.when(f == last_f)
    def _():
        mine = lane == e
        gate = jnp.sum(jnp.where(mine, gate_sc[...], 0.0), axis=-1, keepdims=True)
        rank = jnp.sum(jnp.where(mine, rank_sc[...], 0.0), axis=-1, keepdims=True)
        rank_b = jnp.broadcast_to(rank, (T, C2))
        col = lax.broadcasted_iota(I32, (T, C2), 1)

        def scatter(p, carry):
            off = pl.multiple_of(p * C2, C2)
            onehot_t = (rank_b == (col + off).astype(F32)).astype(BF16)
            y = jnp.dot(onehot_t, ys_sc[pl.ds(off, C2), :].astype(BF16), preferred_element_type=F32)
            acc_sc[...] += gate * y
            return carry

        lax.fori_loop(0, n_pairs, scatter, 0)

    @pl.when((e == pl.num_programs(1) - 1) & (f == last_f))
    def _():
        y = acc_sc[...]
        o_ref[...] = _rms(y, gf_ref[...]) if final_norm else y


def _moe(x, h, rw, rb, wg, wu, wd, g_final, final_norm):
    S, D = x.shape
    E, _, Fe = wg.shape
    tm = min(1024, S)
    tf = _pick_tile(Fe, 1792)
    once = pl.Buffered(1)
    return pl.pallas_call(
        functools.partial(_moe_kernel, final_norm=final_norm),
        grid=(S // tm, E, Fe // tf),
        in_specs=[pl.BlockSpec((tm, D), lambda i, e, f: (i, 0), pipeline_mode=once),
                  pl.BlockSpec((tm, D), lambda i, e, f: (i, 0), pipeline_mode=once),
                  pl.BlockSpec((D, LANE), lambda i, e, f: (0, 0)),
                  pl.BlockSpec((1, LANE), lambda i, e, f: (0, 0)),
                  pl.BlockSpec((1, D, tf), lambda i, e, f: (e, 0, f)),
                  pl.BlockSpec((1, D, tf), lambda i, e, f: (e, 0, f)),
                  pl.BlockSpec((1, tf, D), lambda i, e, f: (e, f, 0)),
                  pl.BlockSpec((1, D), lambda i, e, f: (0, 0))],
        out_specs=pl.BlockSpec((tm, D), lambda i, e, f: (i, 0)),
        out_shape=jax.ShapeDtypeStruct((S, D), F32),
        scratch_shapes=[pltpu.VMEM((tm, D), F32), pltpu.VMEM((tm, LANE), F32),
                        pltpu.VMEM((tm, LANE), F32), pltpu.VMEM((1, LANE), F32),
                        pltpu.VMEM((tm, D), BF16), pltpu.VMEM((tm, D), F32)],
        compiler_params=_params("arbitrary", "arbitrary", "arbitrary"),
        name="moe",
    )(x, h, rw, rb, wg, wu, wd, g_final.reshape(1, D))


def _cast_kernel(x_ref, o_ref):
    o_ref[...] = x_ref[...].astype(o_ref.dtype)


def _to_bf16(w):
    E, A, B = w.shape
    ta = 256
    return pl.pallas_call(
        _cast_kernel,
        grid=(E, A // ta),
        in_specs=[pl.BlockSpec((1, ta, B), lambda e, a: (e, a, 0))],
        out_specs=pl.BlockSpec((1, ta, B), lambda e, a: (e, a, 0)),
        out_shape=jax.ShapeDtypeStruct(w.shape, BF16),
        compiler_params=_params("arbitrary", "arbitrary"),
        name="to_bf16",
    )(w)


def _pad_cols(a, n):
    return jnp.pad(a, ((0, 0), (0, n - a.shape[1])))


def kernel(x, w_in, b_gate, g_mix, fox_f_bias, dsa_kv_norm, dsa_w_uk, dsa_w_uv, rwkv_mu, rwkv_w0, rwkv_w_up, rwkv_a0, rwkv_a_up, rwkv_g_up, rwkv_k_k, rwkv_k_a, rwkv_r_k, rwkv_ln_g, rwkv_ln_b, vres_down, vres_mu, vres_up, vres_v0, p_fox, p_dsa, p_rwkv, w_out, g_ffn, ffn_w_gate, ffn_w_up, ffn_w_down, router_w, router_b, moe_w_gate, moe_w_up, moe_w_down, g_final):
    B, S, D = x.shape
    assert B == 1 and D == D_MODEL and S % LANE == 0
    depth = w_in.shape[0]
    topk = min(IDX_TOPK, S // 4)
    bf = lambda a: a.astype(BF16)
    xs = x[0]
    v_first = None
    w_in_bf = _to_bf16(w_in)
    for l in range(depth):
        wl = w_in_bf[l]
        w_fox, w_dsa, w_rwkv, w_gate = (wl[:, :FOX_IN], wl[:, FOX_IN:FOX_IN + DSA_IN],
                                        wl[:, FOX_IN + DSA_IN:FOX_IN + DSA_IN + RWKV_IN],
                                        wl[:, FOX_IN + DSA_IN + RWKV_IN:])
        o1 = DSA_W + DSA_KV_RANK + IDX_W
        w_att = jnp.concatenate([_pad_cols(w_fox, 3 * FOX_W + LANE), w_dsa[:, :o1],
                                 _pad_cols(w_dsa[:, o1:], LANE)], axis=1)
        if l > 0:
            w_rwkv = jnp.concatenate([w_rwkv, bf(_pad_cols(vres_down[l - 1], LANE))], axis=1)
        pa = _rms_proj(xs, g_mix[l], w_att)
        pr = _rms_proj(xs, g_mix[l], w_rwkv)
        gate_p = _rms_proj(xs, g_mix[l], w_gate)

        c0 = 3 * FOX_W
        fl = pa[:, c0:c0 + FOX_HEADS].T.reshape(FOX_HEADS, S // LANE, LANE)
        Fh, Fm, Fl = (a.reshape(FOX_HEADS, S) for a in _fox_cumsum(fl, fox_f_bias[l]))
        qT = bf(pa[:, 0:FOX_W].T.reshape(FOX_HEADS, FOX_HEAD_DIM, S) * (FOX_HEAD_DIM ** -0.5 * LOG2E))
        r = jnp.arange(FOX_KA - FOX_HEAD_DIM)
        rq, ck = r[None, :, None], r[None, None, :]
        pick = lambda idx, a, b, c, d: jnp.where(idx == a[0], a[1], jnp.where(
            idx == b[0], b[1], jnp.where(idx == c[0], c[1], jnp.where(idx < 6, d, 0)))).astype(BF16)
        fq = pick(rq, (0, Fh[:, None, :]), (1, Fm[:, None, :]), (2, Fl[:, None, :]), 1)
        fk = pick(ck, (3, -Fh[:, :, None]), (4, -Fm[:, :, None]), (5, -Fl[:, :, None]), 1)
        qaT = jnp.concatenate([qT, fq], axis=1)
        k4 = bf(pa[:, FOX_W:2 * FOX_W].reshape(S, FOX_HEADS, FOX_HEAD_DIM).transpose(1, 0, 2))
        ka = jnp.concatenate([k4, fk], axis=2)
        fvT = bf(pa[:, 2 * FOX_W:3 * FOX_W].T.reshape(FOX_HEADS, FOX_HEAD_DIM, S))
        o_fox = _fox_attention(qaT, ka, fvT).reshape(FOX_W, S).T

        t = min(DSA_T, S)
        nb = S // t
        c1 = c0 + LANE
        dqT = bf(pa[:, c1:c1 + DSA_W].T.reshape(DSA_HEADS, DSA_HEAD_DIM, nb, t).transpose(2, 0, 1, 3))
        c2 = c1 + DSA_W
        ckv = _rmsnorm(pa[:, c2:c2 + DSA_KV_RANK], dsa_kv_norm[l], BF16)
        c3 = c2 + DSA_KV_RANK
        qiT = bf(pa[:, c3:c3 + IDX_W].T.reshape(IDX_HEADS, IDX_HEAD_DIM, nb, t).transpose(2, 1, 0, 3)
                 .reshape(nb, IDX_HEAD_DIM, IDX_HEADS * t))
        c4 = c3 + IDX_W
        ki = bf(pa[:, c4:c4 + IDX_HEAD_DIM])
        c5 = c4 + IDX_HEAD_DIM
        wi = pa[:, c5:c5 + IDX_HEADS].T.reshape(IDX_HEADS, nb, t).transpose(1, 0, 2) * (IDX_W ** -0.5)
        o_dsa = _dsa_attention(qiT, wi, dqT, bf(dsa_w_uk[l].transpose(0, 2, 1)),
                               bf(dsa_w_uv[l].transpose(0, 2, 1)), ki, ckv, ckv.T, topk)
        o_dsa = o_dsa.transpose(1, 2, 0, 3).reshape(DSA_W, S).T

        zw = jnp.zeros((RWKV_W_LORA, RWKV_W), F32)
        wwa = bf(jnp.concatenate([jnp.concatenate([rwkv_w_up[l], zw], axis=1),
                                  jnp.concatenate([zw, rwkv_a_up[l]], axis=1)], axis=0))
        vres = None
        mu = rwkv_mu[l]
        if l > 0:
            vup = jnp.pad(vres_up[l - 1], ((0, LANE - RWKV_V_LORA), (0, 0)))
            vres = (bf(vup), vres_v0[l - 1], v_first)
            mu = jnp.concatenate([mu, jnp.pad(vres_mu[l - 1], (0, LANE - RWKV_V_LORA))])
        *scan_ops, bonus, g_rwkv, v = _rwkv_pre(
            pr, mu, rwkv_w0[l], rwkv_a0[l], wwa, bf(rwkv_g_up[l]),
            rwkv_k_k[l], rwkv_k_a[l], rwkv_r_k[l], vres)
        if l == 0:
            v_first = v
        y_rwkv = _rwkv_scan(*scan_ops)

        xs, h2 = _merge(xs, o_fox, o_dsa, y_rwkv, bonus, g_rwkv, rwkv_ln_g[l], rwkv_ln_b[l],
                        gate_p, b_gate[l], bf(p_fox[l]), bf(p_dsa[l]), bf(p_rwkv[l]), bf(w_out[l]),
                        g_ffn[l])

        last = l == depth - 1
        if l % 2 == 0:
            xs = _ffn(xs, h2, bf(ffn_w_gate[l // 2]), bf(ffn_w_up[l // 2]), bf(ffn_w_down[l // 2]),
                      g_final, last)
        else:
            rw = bf(_pad_cols(router_w[l // 2], LANE))
            rb = _pad_cols(router_b[l // 2].reshape(1, -1), LANE)
            xs = _moe(xs, h2, rw, rb, _to_bf16(moe_w_gate[l // 2]), _to_bf16(moe_w_up[l // 2]),
                      _to_bf16(moe_w_down[l // 2]), g_final, last)
    return xs[None]
```
